```python
import jax, jax.numpy as jnp
from jax import lax
import numpy as np


D_MODEL = 2048
BATCH = 8
SEQ = 4096
DEPTH = 2

GRID_W = 64
CTX_LEN = 256
N_HEADS = 8
N_KV_HEADS = 2
HEAD_DIM = 128
ATTN_W = N_HEADS * HEAD_DIM
KV_W = N_KV_HEADS * HEAD_DIM
Q_BLOCK = 128
ROPE_THETA = 10000.0
ROPE_PAIRS = HEAD_DIM // 4
D_RNN = 1024
RNN_BLOCKS = 8
RNN_BLOCK_W = D_RNN // RNN_BLOCKS
CONV_W = 4
RG_C = 8.0
AR_IN = ATTN_W + 2 * KV_W + 2 * D_RNN
AR_OUT = ATTN_W + D_RNN
D_GM = 2048
GM_GROUPS = 16
GM_GROUP_W = D_GM // GM_GROUPS
CHUNK = 128
D_FF = 4 * D_MODEL
EPS = 1e-6
N_EVEN = (DEPTH + 1) // 2
N_ODD = DEPTH // 2

kernel_name = 'hybrid_attn_rglru_chunkgmlp_diffusion'


def rms_norm(x, g):
    xf = x.astype(jnp.float32)
    y = xf * lax.rsqrt(jnp.mean(xf * xf, axis=-1, keepdims=True) + EPS)
    return (y * g.astype(jnp.float32)).astype(x.dtype)


def layer_norm(x, g, b):
    xf = x.astype(jnp.float32)
    mu = jnp.mean(xf, axis=-1, keepdims=True)
    xc = xf - mu
    y = xc * lax.rsqrt(jnp.mean(xc * xc, axis=-1, keepdims=True) + EPS)
    return (y * g.astype(jnp.float32) + b.astype(jnp.float32)).astype(x.dtype)


def modulate(h, shift, scale):
    return h * (1 + scale) + shift


def axial_angles(n):
    rows = n // GRID_W
    r_idx, c_idx = jnp.meshgrid(jnp.arange(rows), jnp.arange(GRID_W), indexing='ij')
    r_idx = r_idx.reshape(-1).astype(jnp.float32)
    c_idx = c_idx.reshape(-1).astype(jnp.float32)
    freqs = ROPE_THETA ** (-jnp.arange(ROPE_PAIRS, dtype=jnp.float32) / ROPE_PAIRS)
    return r_idx[:, None] * freqs, c_idx[:, None] * freqs


def rope_1d(x, ang):
    x1, x2 = jnp.split(x.astype(jnp.float32), 2, axis=-1)
    cos = jnp.cos(ang)[None, :, None, :]
    sin = jnp.sin(ang)[None, :, None, :]
    return jnp.concatenate([x1 * cos - x2 * sin, x2 * cos + x1 * sin], axis=-1)


def rope_2d(x, ang_row, ang_col):
    half = HEAD_DIM // 2
    out = jnp.concatenate([rope_1d(x[..., :half], ang_row), rope_1d(x[..., half:], ang_col)], axis=-1)
    return out.astype(x.dtype)


def gqa_attend(q, k, v):
    bsz, n = q.shape[0], q.shape[1]
    nb = n // Q_BLOCK
    groups = N_HEADS // N_KV_HEADS
    scale = HEAD_DIM ** -0.5
    qb = q.reshape(bsz, nb, Q_BLOCK, N_KV_HEADS, groups, HEAD_DIM).transpose(1, 0, 2, 3, 4, 5)

    def one_block(qi):
        s = jnp.einsum('bqkgd,btkd->bkgqt', qi, k).astype(jnp.float32) * scale
        p = jax.nn.softmax(s, axis=-1).astype(v.dtype)
        return jnp.einsum('bkgqt,btkd->bqkgd', p, v)

    o = lax.map(one_block, qb)
    return o.transpose(1, 0, 2, 3, 4, 5).reshape(bsz, n, N_HEADS * HEAD_DIM)


def centred_dwconv(x, w, b):
    n = x.shape[1]
    left = CONV_W // 2
    xp = jnp.pad(x, ((0, 0), (left, CONV_W - 1 - left), (0, 0)))
    y = b
    for j in range(CONV_W):
        y = y + xp[:, j:j + n] * w[j]
    return y


def block_diag(x, w, b):
    xb = x.reshape(x.shape[0], x.shape[1], RNN_BLOCKS, RNN_BLOCK_W)
    return jnp.einsum('bsnc,ncd->bsnd', xb, w).reshape(x.shape) + b


def rglru_direction(x, wa, ba, wx, bx, lam, h0, reverse):
    r = jax.nn.sigmoid(block_diag(x, wa, ba).astype(jnp.float32))
    i = jax.nn.sigmoid(block_diag(x, wx, bx).astype(jnp.float32))
    log_a = -RG_C * r * jax.nn.softplus(-lam.astype(jnp.float32))
    a = jnp.exp(log_a)
    b = jnp.sqrt(-jnp.expm1(2.0 * log_a)) * (i * x.astype(jnp.float32))

    def combine(lhs, rhs):
        return (lhs[0] * rhs[0], rhs[0] * lhs[1] + rhs[1])

    a_cum, h = lax.associative_scan(combine, (a, b), reverse=reverse, axis=1)
    h = h + a_cum * h0[:, None, :]
    final = h[:, 0] if reverse else h[:, -1]
    return h, final


def mix_attn_rglru(hl, hc, w_in, q_g, k_g, conv_w, conv_b, wa, ba, wx, bx, lam, w_out, need_ctx):
    s1 = ATTN_W
    s2 = s1 + KV_W
    s3 = s2 + KV_W
    s4 = s3 + D_RNN

    def project(h):
        bsz, n = h.shape[0], h.shape[1]
        q, k, v, xr, gr = jnp.split(h @ w_in, [s1, s2, s3, s4], axis=-1)
        q = rms_norm(q.reshape(bsz, n, N_HEADS, HEAD_DIM), q_g)
        k = rms_norm(k.reshape(bsz, n, N_KV_HEADS, HEAD_DIM), k_g)
        v = v.reshape(bsz, n, N_KV_HEADS, HEAD_DIM)
        xr = centred_dwconv(xr, conv_w, conv_b)
        return q, k, v, xr, gr

    ql, kl, vl, xl, gl = project(hl)
    qc, kc, vc, xc, gc = project(hc)

    ang_r, ang_c = axial_angles(hl.shape[1])
    ql = rope_2d(ql, ang_r, ang_c)
    kl = rope_2d(kl, ang_r, ang_c)
    k_all = jnp.concatenate([kc, kl], axis=1)
    v_all = jnp.concatenate([vc, vl], axis=1)
    attn_l = gqa_attend(ql, k_all, v_all)

    zeros = jnp.zeros((hc.shape[0], D_RNN), jnp.float32)
    hcf, fin_f = rglru_direction(xc, wa[0], ba[0], wx[0], bx[0], lam[0], zeros, False)
    hcb, fin_b = rglru_direction(xc, wa[1], ba[1], wx[1], bx[1], lam[1], zeros, True)
    hlf, _ = rglru_direction(xl, wa[0], ba[0], wx[0], bx[0], lam[0], fin_f, False)
    hlb, _ = rglru_direction(xl, wa[1], ba[1], wx[1], bx[1], lam[1], fin_b, True)
    rnn_l = ((hlf + hlb) * jax.nn.gelu(gl.astype(jnp.float32))).astype(hl.dtype)
    out_l = jnp.concatenate([attn_l, rnn_l], axis=-1) @ w_out

    out_c = None
    if need_ctx:
        attn_c = gqa_attend(qc, kc, vc)
        rnn_c = ((hcf + hcb) * jax.nn.gelu(gc.astype(jnp.float32))).astype(hc.dtype)
        out_c = jnp.concatenate([attn_c, rnn_c], axis=-1) @ w_out
    return out_l, out_c


def chunk_gmlp(h, w_in, b_in, v_g, v_b, w_sp, b_sp, w_out):
    bsz, n = h.shape[0], h.shape[1]
    z = jax.nn.gelu(h @ w_in + b_in)
    u, v = jnp.split(z, 2, axis=-1)
    v = layer_norm(v, v_g, v_b)
    v = v.reshape(bsz, n // CHUNK, CHUNK, GM_GROUPS, GM_GROUP_W)
    sv = jnp.einsum('gpq,bcqgd->bcpgd', w_sp, v) + b_sp.T[None, None, :, :, None]
    return (u * sv.reshape(bsz, n, D_GM)) @ w_out


def sq_relu_mlp(h, w1, w2):
    return jnp.square(jax.nn.relu(h @ w1)) @ w2


def _fwd_setup_inputs(seed: int = 0) -> dict:
    key = jax.random.key(seed)
    ks = jax.random.split(key, 32)
    f32 = jnp.float32
    D = D_MODEL

    def nrm(k, shape, scale):
        return jax.random.normal(k, shape, f32) * scale

    lam_u = jax.random.uniform(ks[20], (N_EVEN, 2, D_RNN), f32, 0.9, 0.999)
    a0 = lam_u ** (1.0 / RG_C)
    return {
        'x': nrm(ks[0], (BATCH, SEQ, D), 1.0),
        'c': nrm(ks[1], (BATCH, D), 1.0),
        'ctx': nrm(ks[2], (BATCH, CTX_LEN, D), 1.0),
        'c_ctx': nrm(ks[3], (D,), 1.0),
        'w_mod': nrm(ks[4], (DEPTH, D, 6 * D), 0.5 * D ** -0.5),
        'b_mod': nrm(ks[5], (DEPTH, 6 * D), 0.02),
        'norm_g': 1.0 + nrm(ks[6], (DEPTH, 4, D), 0.02),
        'w_ff_in': nrm(ks[7], (DEPTH, D, D_FF), D ** -0.5),
        'w_ff_out': nrm(ks[8], (DEPTH, D_FF, D), D_FF ** -0.5),
        'ar_w_in': nrm(ks[9], (N_EVEN, D, AR_IN), D ** -0.5),
        'ar_q_g': 1.0 + nrm(ks[10], (N_EVEN, HEAD_DIM), 0.02),
        'ar_k_g': 1.0 + nrm(ks[11], (N_EVEN, HEAD_DIM), 0.02),
        'ar_conv_w': nrm(ks[12], (N_EVEN, CONV_W, D_RNN), CONV_W ** -0.5),
        'ar_conv_b': nrm(ks[13], (N_EVEN, D_RNN), 0.02),
        'ar_wa': nrm(ks[14], (N_EVEN, 2, RNN_BLOCKS, RNN_BLOCK_W, RNN_BLOCK_W), RNN_BLOCK_W ** -0.5),
        'ar_ba': nrm(ks[15], (N_EVEN, 2, D_RNN), 0.02),
        'ar_wx': nrm(ks[16], (N_EVEN, 2, RNN_BLOCKS, RNN_BLOCK_W, RNN_BLOCK_W), RNN_BLOCK_W ** -0.5),
        'ar_bx': nrm(ks[17], (N_EVEN, 2, D_RNN), 0.02),
        'ar_lambda': jnp.log(a0) - jnp.log1p(-a0),
        'ar_w_out': nrm(ks[18], (N_EVEN, AR_OUT, D), AR_OUT ** -0.5),
        'gm_w_in': nrm(ks[19], (N_ODD, D, 2 * D_GM), D ** -0.5),
        'gm_b_in': nrm(ks[21], (N_ODD, 2 * D_GM), 0.02),
        'gm_v_g': 1.0 + nrm(ks[22], (N_ODD, D_GM), 0.02),
        'gm_v_b': nrm(ks[23], (N_ODD, D_GM), 0.02),
        'gm_w_sp': nrm(ks[24], (N_ODD, GM_GROUPS, CHUNK, CHUNK), CHUNK ** -0.5),
        'gm_b_sp': 1.0 + nrm(ks[25], (N_ODD, GM_GROUPS, CHUNK), 0.02),
        'gm_w_out': nrm(ks[26], (N_ODD, D_GM, D), D_GM ** -0.5),
    }


def _fwd_reference(x, c, ctx, c_ctx, w_mod, b_mod, norm_g, w_ff_in, w_ff_out, ar_w_in, ar_q_g, ar_k_g, ar_conv_w, ar_conv_b, ar_wa, ar_ba, ar_wx, ar_bx, ar_lambda, ar_w_out, gm_w_in, gm_b_in, gm_v_g, gm_v_b, gm_w_sp, gm_b_sp, gm_w_out):
    xl, xc = x, ctx
    s_c = jax.nn.silu(c)
    s_ctx = jax.nn.silu(c_ctx)
    for i in range(DEPTH):
        j = i // 2
        need_ctx = any(l % 2 == 0 for l in range(i + 1, DEPTH))
        g = norm_g[i]
        ml = jnp.split((s_c @ w_mod[i] + b_mod[i])[:, None, :], 6, axis=-1)
        mc = jnp.split((s_ctx @ w_mod[i] + b_mod[i])[None, None, :], 6, axis=-1)
        hl = modulate(rms_norm(xl, g[0]), ml[0], ml[1])
        oc = None
        if i % 2 == 0:
            hc = modulate(rms_norm(xc, g[0]), mc[0], mc[1])
            ol, oc = mix_attn_rglru(hl, hc, ar_w_in[j], ar_q_g[j], ar_k_g[j], ar_conv_w[j], ar_conv_b[j],
                                    ar_wa[j], ar_ba[j], ar_wx[j], ar_bx[j], ar_lambda[j], ar_w_out[j], need_ctx)
        else:
            ol = chunk_gmlp(hl, gm_w_in[j], gm_b_in[j], gm_v_g[j], gm_v_b[j], gm_w_sp[j], gm_b_sp[j], gm_w_out[j])
            if need_ctx:
                hc = modulate(rms_norm(xc, g[0]), mc[0], mc[1])
                oc = chunk_gmlp(hc, gm_w_in[j], gm_b_in[j], gm_v_g[j], gm_v_b[j], gm_w_sp[j], gm_b_sp[j], gm_w_out[j])
        xl = xl + ml[2] * rms_norm(ol, g[1])
        hl = modulate(rms_norm(xl, g[2]), ml[3], ml[4])
        xl = xl + ml[5] * rms_norm(sq_relu_mlp(hl, w_ff_in[i], w_ff_out[i]), g[3])
        if need_ctx:
            xc = xc + mc[2] * rms_norm(oc, g[1])
            hc = modulate(rms_norm(xc, g[2]), mc[3], mc[4])
            xc = xc + mc[5] * rms_norm(sq_relu_mlp(hc, w_ff_in[i], w_ff_out[i]), g[3])
    return xl


import jax as _jax
import jax.numpy as _jnp

TWIN_FORMAT = 'train_step'
FWD_PARAMS = ['x', 'c', 'ctx', 'c_ctx', 'w_mod', 'b_mod', 'norm_g', 'w_ff_in', 'w_ff_out', 'ar_w_in', 'ar_q_g', 'ar_k_g', 'ar_conv_w', 'ar_conv_b', 'ar_wa', 'ar_ba', 'ar_wx', 'ar_bx', 'ar_lambda', 'ar_w_out', 'gm_w_in', 'gm_b_in', 'gm_v_g', 'gm_v_b', 'gm_w_sp', 'gm_b_sp', 'gm_w_out']
TWIN_WEIGHTS = ['c_ctx', 'w_mod', 'b_mod', 'norm_g', 'w_ff_in', 'w_ff_out', 'ar_w_in', 'ar_q_g', 'ar_k_g', 'ar_conv_w', 'ar_conv_b', 'ar_wa', 'ar_ba', 'ar_wx', 'ar_bx', 'ar_lambda', 'ar_w_out', 'gm_w_in', 'gm_b_in', 'gm_v_g', 'gm_v_b', 'gm_w_sp', 'gm_b_sp', 'gm_w_out']
TWIN_DIFF_INPUT = 'x'
TWIN_INPUTS = ['x', 'c', 'ctx', 'c_ctx', 'w_mod', 'b_mod', 'norm_g', 'w_ff_in', 'w_ff_out', 'ar_w_in', 'ar_q_g', 'ar_k_g', 'ar_conv_w', 'ar_conv_b', 'ar_wa', 'ar_ba', 'ar_wx', 'ar_bx', 'ar_lambda', 'ar_w_out', 'gm_w_in', 'gm_b_in', 'gm_v_g', 'gm_v_b', 'gm_w_sp', 'gm_b_sp', 'gm_w_out', 'loss_target', 'm_c_ctx', 'm_w_mod', 'm_b_mod', 'm_norm_g', 'm_w_ff_in', 'm_w_ff_out', 'm_ar_w_in', 'm_ar_q_g', 'm_ar_k_g', 'm_ar_conv_w', 'm_ar_conv_b', 'm_ar_wa', 'm_ar_ba', 'm_ar_wx', 'm_ar_bx', 'm_ar_lambda', 'm_ar_w_out', 'm_gm_w_in', 'm_gm_b_in', 'm_gm_v_g', 'm_gm_v_b', 'm_gm_w_sp', 'm_gm_b_sp', 'm_gm_w_out', 'v_c_ctx', 'v_w_mod', 'v_b_mod', 'v_norm_g', 'v_w_ff_in', 'v_w_ff_out', 'v_ar_w_in', 'v_ar_q_g', 'v_ar_k_g', 'v_ar_conv_w', 'v_ar_conv_b', 'v_ar_wa', 'v_ar_ba', 'v_ar_wx', 'v_ar_bx', 'v_ar_lambda', 'v_ar_w_out', 'v_gm_w_in', 'v_gm_b_in', 'v_gm_v_g', 'v_gm_v_b', 'v_gm_w_sp', 'v_gm_b_sp', 'v_gm_w_out']
TWIN_OUTPUTS = ['loss', 'grad_x', 'grad_c_ctx', 'grad_w_mod', 'grad_b_mod', 'grad_norm_g', 'grad_w_ff_in', 'grad_w_ff_out', 'grad_ar_w_in', 'grad_ar_q_g', 'grad_ar_k_g', 'grad_ar_conv_w', 'grad_ar_conv_b', 'grad_ar_wa', 'grad_ar_ba', 'grad_ar_wx', 'grad_ar_bx', 'grad_ar_lambda', 'grad_ar_w_out', 'grad_gm_w_in', 'grad_gm_b_in', 'grad_gm_v_g', 'grad_gm_v_b', 'grad_gm_w_sp', 'grad_gm_b_sp', 'grad_gm_w_out', 'delta_c_ctx', 'delta_w_mod', 'delta_b_mod', 'delta_norm_g', 'delta_w_ff_in', 'delta_w_ff_out', 'delta_ar_w_in', 'delta_ar_q_g', 'delta_ar_k_g', 'delta_ar_conv_w', 'delta_ar_conv_b', 'delta_ar_wa', 'delta_ar_ba', 'delta_ar_wx', 'delta_ar_bx', 'delta_ar_lambda', 'delta_ar_w_out', 'delta_gm_w_in', 'delta_gm_b_in', 'delta_gm_v_g', 'delta_gm_v_b', 'delta_gm_w_sp', 'delta_gm_b_sp', 'delta_gm_w_out', 'new_m_c_ctx', 'new_m_w_mod', 'new_m_b_mod', 'new_m_norm_g', 'new_m_w_ff_in', 'new_m_w_ff_out', 'new_m_ar_w_in', 'new_m_ar_q_g', 'new_m_ar_k_g', 'new_m_ar_conv_w', 'new_m_ar_conv_b', 'new_m_ar_wa', 'new_m_ar_ba', 'new_m_ar_wx', 'new_m_ar_bx', 'new_m_ar_lambda', 'new_m_ar_w_out', 'new_m_gm_w_in', 'new_m_gm_b_in', 'new_m_gm_v_g', 'new_m_gm_v_b', 'new_m_gm_w_sp', 'new_m_gm_b_sp', 'new_m_gm_w_out', 'new_v_c_ctx', 'new_v_w_mod', 'new_v_b_mod', 'new_v_norm_g', 'new_v_w_ff_in', 'new_v_w_ff_out', 'new_v_ar_w_in', 'new_v_ar_q_g', 'new_v_ar_k_g', 'new_v_ar_conv_w', 'new_v_ar_conv_b', 'new_v_ar_wa', 'new_v_ar_ba', 'new_v_ar_wx', 'new_v_ar_bx', 'new_v_ar_lambda', 'new_v_ar_w_out', 'new_v_gm_w_in', 'new_v_gm_b_in', 'new_v_gm_v_g', 'new_v_gm_v_b', 'new_v_gm_w_sp', 'new_v_gm_b_sp', 'new_v_gm_w_out']
TWIN_LEAF_KINDS = {'loss': 'loss', 'grad_x': 'grad_x', 'grad_c_ctx': 'grad_w', 'grad_w_mod': 'grad_w', 'grad_b_mod': 'grad_w', 'grad_norm_g': 'grad_w', 'grad_w_ff_in': 'grad_w', 'grad_w_ff_out': 'grad_w', 'grad_ar_w_in': 'grad_w', 'grad_ar_q_g': 'grad_w', 'grad_ar_k_g': 'grad_w', 'grad_ar_conv_w': 'grad_w', 'grad_ar_conv_b': 'grad_w', 'grad_ar_wa': 'grad_w', 'grad_ar_ba': 'grad_w', 'grad_ar_wx': 'grad_w', 'grad_ar_bx': 'grad_w', 'grad_ar_lambda': 'grad_w', 'grad_ar_w_out': 'grad_w', 'grad_gm_w_in': 'grad_w', 'grad_gm_b_in': 'grad_w', 'grad_gm_v_g': 'grad_w', 'grad_gm_v_b': 'grad_w', 'grad_gm_w_sp': 'grad_w', 'grad_gm_b_sp': 'grad_w', 'grad_gm_w_out': 'grad_w', 'delta_c_ctx': 'delta_w', 'delta_w_mod': 'delta_w', 'delta_b_mod': 'delta_w', 'delta_norm_g': 'delta_w', 'delta_w_ff_in': 'delta_w', 'delta_w_ff_out': 'delta_w', 'delta_ar_w_in': 'delta_w', 'delta_ar_q_g': 'delta_w', 'delta_ar_k_g': 'delta_w', 'delta_ar_conv_w': 'delta_w', 'delta_ar_conv_b': 'delta_w', 'delta_ar_wa': 'delta_w', 'delta_ar_ba': 'delta_w', 'delta_ar_wx': 'delta_w', 'delta_ar_bx': 'delta_w', 'delta_ar_lambda': 'delta_w', 'delta_ar_w_out': 'delta_w', 'delta_gm_w_in': 'delta_w', 'delta_gm_b_in': 'delta_w', 'delta_gm_v_g': 'delta_w', 'delta_gm_v_b': 'delta_w', 'delta_gm_w_sp': 'delta_w', 'delta_gm_b_sp': 'delta_w', 'delta_gm_w_out': 'delta_w', 'new_m_c_ctx': 'new_m', 'new_m_w_mod': 'new_m', 'new_m_b_mod': 'new_m', 'new_m_norm_g': 'new_m', 'new_m_w_ff_in': 'new_m', 'new_m_w_ff_out': 'new_m', 'new_m_ar_w_in': 'new_m', 'new_m_ar_q_g': 'new_m', 'new_m_ar_k_g': 'new_m', 'new_m_ar_conv_w': 'new_m', 'new_m_ar_conv_b': 'new_m', 'new_m_ar_wa': 'new_m', 'new_m_ar_ba': 'new_m', 'new_m_ar_wx': 'new_m', 'new_m_ar_bx': 'new_m', 'new_m_ar_lambda': 'new_m', 'new_m_ar_w_out': 'new_m', 'new_m_gm_w_in': 'new_m', 'new_m_gm_b_in': 'new_m', 'new_m_gm_v_g': 'new_m', 'new_m_gm_v_b': 'new_m', 'new_m_gm_w_sp': 'new_m', 'new_m_gm_b_sp': 'new_m', 'new_m_gm_w_out': 'new_m', 'new_v_c_ctx': 'new_v', 'new_v_w_mod': 'new_v', 'new_v_b_mod': 'new_v', 'new_v_norm_g': 'new_v', 'new_v_w_ff_in': 'new_v', 'new_v_w_ff_out': 'new_v', 'new_v_ar_w_in': 'new_v', 'new_v_ar_q_g': 'new_v', 'new_v_ar_k_g': 'new_v', 'new_v_ar_conv_w': 'new_v', 'new_v_ar_conv_b': 'new_v', 'new_v_ar_wa': 'new_v', 'new_v_ar_ba': 'new_v', 'new_v_ar_wx': 'new_v', 'new_v_ar_bx': 'new_v', 'new_v_ar_lambda': 'new_v', 'new_v_ar_w_out': 'new_v', 'new_v_gm_w_in': 'new_v', 'new_v_gm_b_in': 'new_v', 'new_v_gm_v_g': 'new_v', 'new_v_gm_v_b': 'new_v', 'new_v_gm_w_sp': 'new_v', 'new_v_gm_b_sp': 'new_v', 'new_v_gm_w_out': 'new_v'}


def _forward(args):
    return _fwd_reference(*[args[k] for k in FWD_PARAMS])


def _output_shape():
    def fwd():
        inp = _fwd_setup_inputs(0)
        return _fwd_reference(*[inp[k] for k in FWD_PARAMS])
    out = _jax.eval_shape(fwd)
    return out.shape, out.dtype

N_MICROBATCH = 1
ADAM_LR = 0.001
ADAM_B1 = 0.9
ADAM_B2 = 0.999
ADAM_EPS = 1e-08
ADAM_WD = 0.01
ADAM_STEP = 10
PER_EXAMPLE_BATCH_AXIS = {'x': 0, 'c': 0, 'ctx': 0, 'loss_target': 0}
SHARED_INPUTS = []
_WEIGHT_DTYPES = {'c_ctx': _jnp.float32, 'w_mod': _jnp.float32, 'b_mod': _jnp.float32, 'norm_g': _jnp.float32, 'w_ff_in': _jnp.float32, 'w_ff_out': _jnp.float32, 'ar_w_in': _jnp.float32, 'ar_q_g': _jnp.float32, 'ar_k_g': _jnp.float32, 'ar_conv_w': _jnp.float32, 'ar_conv_b': _jnp.float32, 'ar_wa': _jnp.float32, 'ar_ba': _jnp.float32, 'ar_wx': _jnp.float32, 'ar_bx': _jnp.float32, 'ar_lambda': _jnp.float32, 'ar_w_out': _jnp.float32, 'gm_w_in': _jnp.float32, 'gm_b_in': _jnp.float32, 'gm_v_g': _jnp.float32, 'gm_v_b': _jnp.float32, 'gm_w_sp': _jnp.float32, 'gm_b_sp': _jnp.float32, 'gm_w_out': _jnp.float32}
MOMENT_SCALE = {'c_ctx': 3.353226e-02, 'w_mod': 6.790012e-01, 'b_mod': 1.425183e+00, 'norm_g': 1.180560e+00, 'w_ff_in': 5.573914e-02, 'w_ff_out': 2.334814e-01, 'ar_w_in': 1.914866e-01, 'ar_q_g': 9.444081e-03, 'ar_k_g': 9.488011e-03, 'ar_conv_w': 3.272540e-01, 'ar_conv_b': 1.095297e+00, 'ar_wa': 1.548667e-02, 'ar_ba': 2.287198e-02, 'ar_wx': 3.373269e-02, 'ar_bx': 6.849857e-02, 'ar_lambda': 6.233638e-02, 'ar_w_out': 2.359479e-01, 'gm_w_in': 6.466325e-02, 'gm_b_in': 1.746067e-01, 'gm_v_g': 3.217801e-02, 'gm_v_b': 3.253074e-02, 'gm_w_sp': 3.099448e-02, 'gm_b_sp': 3.119945e-02, 'gm_w_out': 1.714783e-01}


def _to_microbatches(a, axis):
    t = _jnp.moveaxis(a, axis, 0)
    t = t.reshape((N_MICROBATCH, t.shape[0] // N_MICROBATCH) + t.shape[1:])
    return _jnp.moveaxis(t, 1, axis + 1)


def setup_inputs(seed: int = 0) -> dict:
    inp = _fwd_setup_inputs(seed)
    key = _jax.random.fold_in(_jax.random.key(seed), 7919)
    shape, _ = _output_shape()
    out = dict(inp)
    out["loss_target"] = _jax.random.normal(_jax.random.fold_in(key, 0), shape, _jnp.float32)
    for i, name in enumerate(TWIN_WEIGHTS):
        w = inp[name].astype(_jnp.float32)
        if MOMENT_SCALE is None:
            s = _jnp.sqrt(_jnp.mean(_jnp.square(w)) + 1e-30)
        else:
            s = MOMENT_SCALE[name]
        km, kv = _jax.random.split(_jax.random.fold_in(key, i + 1))
        out[name] = w
        out["m_" + name] = s * _jax.random.normal(km, w.shape, _jnp.float32)
        out["v_" + name] = (s * s) * _jax.random.uniform(kv, w.shape, _jnp.float32, 0.5, 1.5)
    if N_MICROBATCH > 1:
        for name, axis in PER_EXAMPLE_BATCH_AXIS.items():
            out[name] = _to_microbatches(out[name], axis)
    return {'x': out['x'], 'c': out['c'], 'ctx': out['ctx'], 'c_ctx': out['c_ctx'], 'w_mod': out['w_mod'], 'b_mod': out['b_mod'], 'norm_g': out['norm_g'], 'w_ff_in': out['w_ff_in'], 'w_ff_out': out['w_ff_out'], 'ar_w_in': out['ar_w_in'], 'ar_q_g': out['ar_q_g'], 'ar_k_g': out['ar_k_g'], 'ar_conv_w': out['ar_conv_w'], 'ar_conv_b': out['ar_conv_b'], 'ar_wa': out['ar_wa'], 'ar_ba': out['ar_ba'], 'ar_wx': out['ar_wx'], 'ar_bx': out['ar_bx'], 'ar_lambda': out['ar_lambda'], 'ar_w_out': out['ar_w_out'], 'gm_w_in': out['gm_w_in'], 'gm_b_in': out['gm_b_in'], 'gm_v_g': out['gm_v_g'], 'gm_v_b': out['gm_v_b'], 'gm_w_sp': out['gm_w_sp'], 'gm_b_sp': out['gm_b_sp'], 'gm_w_out': out['gm_w_out'], 'loss_target': out['loss_target'], 'm_c_ctx': out['m_c_ctx'], 'm_w_mod': out['m_w_mod'], 'm_b_mod': out['m_b_mod'], 'm_norm_g': out['m_norm_g'], 'm_w_ff_in': out['m_w_ff_in'], 'm_w_ff_out': out['m_w_ff_out'], 'm_ar_w_in': out['m_ar_w_in'], 'm_ar_q_g': out['m_ar_q_g'], 'm_ar_k_g': out['m_ar_k_g'], 'm_ar_conv_w': out['m_ar_conv_w'], 'm_ar_conv_b': out['m_ar_conv_b'], 'm_ar_wa': out['m_ar_wa'], 'm_ar_ba': out['m_ar_ba'], 'm_ar_wx': out['m_ar_wx'], 'm_ar_bx': out['m_ar_bx'], 'm_ar_lambda': out['m_ar_lambda'], 'm_ar_w_out': out['m_ar_w_out'], 'm_gm_w_in': out['m_gm_w_in'], 'm_gm_b_in': out['m_gm_b_in'], 'm_gm_v_g': out['m_gm_v_g'], 'm_gm_v_b': out['m_gm_v_b'], 'm_gm_w_sp': out['m_gm_w_sp'], 'm_gm_b_sp': out['m_gm_b_sp'], 'm_gm_w_out': out['m_gm_w_out'], 'v_c_ctx': out['v_c_ctx'], 'v_w_mod': out['v_w_mod'], 'v_b_mod': out['v_b_mod'], 'v_norm_g': out['v_norm_g'], 'v_w_ff_in': out['v_w_ff_in'], 'v_w_ff_out': out['v_w_ff_out'], 'v_ar_w_in': out['v_ar_w_in'], 'v_ar_q_g': out['v_ar_q_g'], 'v_ar_k_g': out['v_ar_k_g'], 'v_ar_conv_w': out['v_ar_conv_w'], 'v_ar_conv_b': out['v_ar_conv_b'], 'v_ar_wa': out['v_ar_wa'], 'v_ar_ba': out['v_ar_ba'], 'v_ar_wx': out['v_ar_wx'], 'v_ar_bx': out['v_ar_bx'], 'v_ar_lambda': out['v_ar_lambda'], 'v_ar_w_out': out['v_ar_w_out'], 'v_gm_w_in': out['v_gm_w_in'], 'v_gm_b_in': out['v_gm_b_in'], 'v_gm_v_g': out['v_gm_v_g'], 'v_gm_v_b': out['v_gm_v_b'], 'v_gm_w_sp': out['v_gm_w_sp'], 'v_gm_b_sp': out['v_gm_b_sp'], 'v_gm_w_out': out['v_gm_w_out']}


def _loss(weights, diff, rest, loss_target):
    with _jax.named_scope("forward"):
        args = {**rest, TWIN_DIFF_INPUT: diff, **{k: w.astype(_WEIGHT_DTYPES[k]) for k, w in weights.items()}}
        y = _forward(args)
    with _jax.named_scope("loss_head"):
        err = _jnp.square(y.astype(_jnp.float32) - loss_target)
        return 0.5 * _jnp.sum(_jnp.mean(err, axis=-1)) if err.ndim else 0.5 * err


def _adamw(w, g, m, v):
    m = ADAM_B1 * m + (1.0 - ADAM_B1) * g
    v = ADAM_B2 * v + (1.0 - ADAM_B2) * _jnp.square(g)
    m_hat = m / (1.0 - ADAM_B1 ** ADAM_STEP)
    v_hat = v / (1.0 - ADAM_B2 ** ADAM_STEP)
    delta = -ADAM_LR * (m_hat / (_jnp.sqrt(v_hat) + ADAM_EPS) + ADAM_WD * w)
    return delta, m, v


def reference(x, c, ctx, c_ctx, w_mod, b_mod, norm_g, w_ff_in, w_ff_out, ar_w_in, ar_q_g, ar_k_g, ar_conv_w, ar_conv_b, ar_wa, ar_ba, ar_wx, ar_bx, ar_lambda, ar_w_out, gm_w_in, gm_b_in, gm_v_g, gm_v_b, gm_w_sp, gm_b_sp, gm_w_out, loss_target, m_c_ctx, m_w_mod, m_b_mod, m_norm_g, m_w_ff_in, m_w_ff_out, m_ar_w_in, m_ar_q_g, m_ar_k_g, m_ar_conv_w, m_ar_conv_b, m_ar_wa, m_ar_ba, m_ar_wx, m_ar_bx, m_ar_lambda, m_ar_w_out, m_gm_w_in, m_gm_b_in, m_gm_v_g, m_gm_v_b, m_gm_w_sp, m_gm_b_sp, m_gm_w_out, v_c_ctx, v_w_mod, v_b_mod, v_norm_g, v_w_ff_in, v_w_ff_out, v_ar_w_in, v_ar_q_g, v_ar_k_g, v_ar_conv_w, v_ar_conv_b, v_ar_wa, v_ar_ba, v_ar_wx, v_ar_bx, v_ar_lambda, v_ar_w_out, v_gm_w_in, v_gm_b_in, v_gm_v_g, v_gm_v_b, v_gm_w_sp, v_gm_b_sp, v_gm_w_out):
    given = dict(x=x, c=c, ctx=ctx, c_ctx=c_ctx, w_mod=w_mod, b_mod=b_mod, norm_g=norm_g, w_ff_in=w_ff_in, w_ff_out=w_ff_out, ar_w_in=ar_w_in, ar_q_g=ar_q_g, ar_k_g=ar_k_g, ar_conv_w=ar_conv_w, ar_conv_b=ar_conv_b, ar_wa=ar_wa, ar_ba=ar_ba, ar_wx=ar_wx, ar_bx=ar_bx, ar_lambda=ar_lambda, ar_w_out=ar_w_out, gm_w_in=gm_w_in, gm_b_in=gm_b_in, gm_v_g=gm_v_g, gm_v_b=gm_v_b, gm_w_sp=gm_w_sp, gm_b_sp=gm_b_sp, gm_w_out=gm_w_out, loss_target=loss_target, m_c_ctx=m_c_ctx, m_w_mod=m_w_mod, m_b_mod=m_b_mod, m_norm_g=m_norm_g, m_w_ff_in=m_w_ff_in, m_w_ff_out=m_w_ff_out, m_ar_w_in=m_ar_w_in, m_ar_q_g=m_ar_q_g, m_ar_k_g=m_ar_k_g, m_ar_conv_w=m_ar_conv_w, m_ar_conv_b=m_ar_conv_b, m_ar_wa=m_ar_wa, m_ar_ba=m_ar_ba, m_ar_wx=m_ar_wx, m_ar_bx=m_ar_bx, m_ar_lambda=m_ar_lambda, m_ar_w_out=m_ar_w_out, m_gm_w_in=m_gm_w_in, m_gm_b_in=m_gm_b_in, m_gm_v_g=m_gm_v_g, m_gm_v_b=m_gm_v_b, m_gm_w_sp=m_gm_w_sp, m_gm_b_sp=m_gm_b_sp, m_gm_w_out=m_gm_w_out, v_c_ctx=v_c_ctx, v_w_mod=v_w_mod, v_b_mod=v_b_mod, v_norm_g=v_norm_g, v_w_ff_in=v_w_ff_in, v_w_ff_out=v_w_ff_out, v_ar_w_in=v_ar_w_in, v_ar_q_g=v_ar_q_g, v_ar_k_g=v_ar_k_g, v_ar_conv_w=v_ar_conv_w, v_ar_conv_b=v_ar_conv_b, v_ar_wa=v_ar_wa, v_ar_ba=v_ar_ba, v_ar_wx=v_ar_wx, v_ar_bx=v_ar_bx, v_ar_lambda=v_ar_lambda, v_ar_w_out=v_ar_w_out, v_gm_w_in=v_gm_w_in, v_gm_b_in=v_gm_b_in, v_gm_v_g=v_gm_v_g, v_gm_v_b=v_gm_v_b, v_gm_w_sp=v_gm_w_sp, v_gm_b_sp=v_gm_b_sp, v_gm_w_out=v_gm_w_out)
    weights = {n: given[n] for n in TWIN_WEIGHTS}
    shared = {n: given[n] for n in SHARED_INPUTS}
    per_example = {n: given[n] for n in ['x', 'c', 'ctx']}
    grad_fn = _jax.value_and_grad(_loss, argnums=(0, 1))

    def one_microbatch(ex, loss_target):
        ex = dict(ex)
        diff = ex.pop(TWIN_DIFF_INPUT)
        return grad_fn(weights, diff, {**shared, **ex}, loss_target)

    if N_MICROBATCH == 1:
        loss, (grad_w, grad_x) = one_microbatch(per_example, given["loss_target"])
    else:
        def body(carry, xs):
            loss_sum, grad_sum = carry
            l_k, (gw_k, gx_k) = one_microbatch(xs[0], xs[1])
            with _jax.named_scope("update"):
                return (loss_sum + l_k, _jax.tree.map(_jnp.add, grad_sum, gw_k)), gx_k

        init = (_jnp.zeros((), _jnp.float32), _jax.tree.map(_jnp.zeros_like, weights))
        (loss, grad_w), grad_x = _jax.lax.scan(body, init, (per_example, given["loss_target"]))
    with _jax.named_scope("update"):
        delta_w, new_m, new_v = {}, {}, {}
        for n in TWIN_WEIGHTS:
            delta_w[n], new_m[n], new_v[n] = _adamw(weights[n], grad_w[n], given["m_" + n], given["v_" + n])
    return (loss, grad_x, *[grad_w[n] for n in TWIN_WEIGHTS], *[delta_w[n] for n in TWIN_WEIGHTS],
            *[new_m[n] for n in TWIN_WEIGHTS], *[new_v[n] for n in TWIN_WEIGHTS])
```

```python
import functools
import math

import jax
import jax.numpy as jnp
from jax import lax
from jax.experimental import pallas as pl
from jax.experimental.pallas import tpu as pltpu

F32 = jnp.float32
BF16 = jnp.bfloat16
MESH = pl.DeviceIdType.MESH
ANY = pl.BlockSpec(memory_space=pl.ANY)

VMEM_LIMIT_BYTES = 52 * 1024 * 1024
LANES = 128
N_CHIPS = 4
N_DEV = 8

HEAD_DIM = 128
N_HEADS = 8
N_KV = 2
GROUP = N_HEADS // N_KV
ATTN_W = N_HEADS * HEAD_DIM
KV_W = N_KV * HEAD_DIM
D_RNN = 1024
RNN_BLOCKS = 8
RNN_BW = D_RNN // RNN_BLOCKS
CONV_W = 4
RG_C = 8.0
GRID_W = 64
ROPE_THETA = 10000.0
ROPE_PAIRS = HEAD_DIM // 4
GM_GROUPS = 16
CHUNK = 128
EPS = 1e-6
ADAM_LR, ADAM_B1, ADAM_B2, ADAM_EPS, ADAM_WD, ADAM_STEP = 0.001, 0.9, 0.999, 1e-08, 0.01, 10
GELU_C = math.sqrt(2.0 / math.pi)


def _params(sem=None):
    return pltpu.CompilerParams(dimension_semantics=sem, vmem_limit_bytes=VMEM_LIMIT_BYTES)


def _tile(dim, pref, unit):
    best = None
    t = unit
    while t <= min(dim, pref):
        if dim % t == 0:
            best = t
        t += unit
    return best if best is not None else dim


def _full(shape):
    nd = len(shape)
    return pl.BlockSpec(shape, lambda *_: (0,) * nd)


def _blocked_map(split, per_q):
    if split == 'n':
        return lambda r, c: (c // per_q, r, c % per_q)
    if split == 'k':
        return lambda r, c: (r // per_q, r % per_q, c)
    return lambda r, c: (r, c)


def _logical_shape(arr, split):
    if split == 'n':
        return arr.shape[1], arr.shape[0] * arr.shape[2]
    if split == 'k':
        return arr.shape[0] * arr.shape[1], arr.shape[2]
    return arr.shape


def _matmul(a, b, *, kind, name, out_dtype, b_split=None, out_split=None, bias=None, epilogue=None, extra=None,
            pref=(1024, 1024, 1024)):
    b_rows, b_cols = _logical_shape(b, b_split)
    if kind == 'nn':
        m, kc = a.shape
        n = b_cols
        assert b_rows == kc
    elif kind == 'nt':
        m, kc = a.shape
        n = b_rows
        assert b_cols == kc
    else:
        kc, m = a.shape
        n = b_cols
        assert b_rows == kc
    b_row_ext = b.shape[1] if b_split == 'k' else b_rows
    b_col_ext = b.shape[2] if b_split == 'n' else b_cols
    out_row_ext = m // N_CHIPS if out_split == 'k' else m
    out_col_ext = n // N_CHIPS if out_split == 'n' else n
    if kind == 'nn':
        ti = _tile(min(m, out_row_ext), pref[0], 16)
        tj = _tile(math.gcd(b_col_ext, out_col_ext), pref[1], LANES)
        tl = _tile(b_row_ext, pref[2], LANES)
        a_spec = pl.BlockSpec((ti, tl), lambda i, j, l: (i, l))
        b_tile, b_rc = (tl, tj), (lambda i, j, l: (l, j))
        dims = (((1,), (0,)), ((), ()))
    elif kind == 'nt':
        ti = _tile(min(m, out_row_ext), pref[0], 16)
        tj = _tile(math.gcd(b_row_ext, out_col_ext), pref[1], LANES)
        tl = _tile(b_col_ext, pref[2], LANES)
        a_spec = pl.BlockSpec((ti, tl), lambda i, j, l: (i, l))
        b_tile, b_rc = (tj, tl), (lambda i, j, l: (j, l))
        dims = (((1,), (1,)), ((), ()))
    else:
        ti = _tile(out_row_ext, pref[0], LANES)
        tj = _tile(math.gcd(b_col_ext, out_col_ext), pref[1], LANES)
        tl = _tile(b_row_ext, pref[2], 16)
        a_spec = pl.BlockSpec((tl, ti), lambda i, j, l: (l, i))
        b_tile, b_rc = (tl, tj), (lambda i, j, l: (l, j))
        dims = (((0,), (0,)), ((), ()))
    grid = (m // ti, n // tj, kc // tl)
    n_l = grid[2]

    if b_split is None:
        b_spec = pl.BlockSpec(b_tile, b_rc)
    else:
        per_q = (b.shape[2] // b_tile[1]) if b_split == 'n' else (b.shape[1] // b_tile[0])
        bmap = _blocked_map(b_split, per_q)
        b_spec = pl.BlockSpec((None,) + b_tile, lambda i, j, l: bmap(*b_rc(i, j, l)))
    if out_split is None:
        out_shape2 = (m, n)
        o_spec = pl.BlockSpec((ti, tj), lambda i, j, l: (i, j))
    else:
        out_shape2 = (N_CHIPS, m // N_CHIPS, n) if out_split == 'k' else (N_CHIPS, m, n // N_CHIPS)
        per_q = (out_shape2[2] // tj) if out_split == 'n' else (out_shape2[1] // ti)
        omap = _blocked_map(out_split, per_q)
        o_spec = pl.BlockSpec((None, ti, tj), lambda i, j, l: omap(i, j))

    in_specs = [a_spec, b_spec]
    operands = [a, b]
    if bias is not None:
        in_specs.append(pl.BlockSpec((1, tj), lambda i, j, l: (0, j)))
        operands.append(bias)
    if extra is not None:
        in_specs.append(pl.BlockSpec((ti, tj), lambda i, j, l: (i, j)))
        operands.append(extra)
    if epilogue == 'relu2':
        out_shape = (jax.ShapeDtypeStruct(out_shape2, out_dtype), jax.ShapeDtypeStruct(out_shape2, out_dtype))
        out_specs = (o_spec, o_spec)
    else:
        out_shape = jax.ShapeDtypeStruct(out_shape2, out_dtype)
        out_specs = o_spec
    has_bias, has_extra = bias is not None, extra is not None

    def body(*refs):
        a_ref, b_ref = refs[0], refs[1]
        pos = 2
        bias_ref = extra_ref = None
        if has_bias:
            bias_ref = refs[pos]
            pos += 1
        if has_extra:
            extra_ref = refs[pos]
            pos += 1
        outs = refs[pos:-1]
        acc_ref = refs[-1]
        step = pl.program_id(2)

        @pl.when(step == 0)
        def _():
            acc_ref[...] = jnp.zeros_like(acc_ref)

        acc_ref[...] += lax.dot_general(a_ref[...].astype(BF16), b_ref[...].astype(BF16), dims,
                                        preferred_element_type=F32)

        @pl.when(step == n_l - 1)
        def _():
            acc = acc_ref[...]
            if has_bias:
                acc = acc + bias_ref[...]
            if epilogue == 'relu2':
                r = jnp.maximum(acc, 0.0)
                outs[0][...] = r.astype(outs[0].dtype)
                outs[1][...] = (r * r).astype(outs[1].dtype)
            elif epilogue == 'times2x':
                outs[0][...] = (acc * (2.0 * extra_ref[...].astype(F32))).astype(outs[0].dtype)
            else:
                outs[0][...] = acc.astype(outs[0].dtype)

    return pl.pallas_call(
        body, name=name, grid=grid, in_specs=in_specs, out_specs=out_specs, out_shape=out_shape,
        scratch_shapes=[pltpu.VMEM((ti, tj), F32)],
        compiler_params=_params(("parallel", "parallel", "arbitrary")),
    )(*operands)


def _small(fn, out_shapes, *arrays, name):
    n_in = len(arrays)

    def body(*refs):
        res = fn(*[r[...] for r in refs[:n_in]])
        for o_ref, v in zip(refs[n_in:], res):
            o_ref[...] = v.astype(o_ref.dtype)

    return pl.pallas_call(
        body, name=name, out_shape=tuple(jax.ShapeDtypeStruct(s, d) for s, d in out_shapes),
        in_specs=[_full(a.shape) for a in arrays], out_specs=tuple(_full(s) for s, _ in out_shapes), grid=(1,),
        compiler_params=_params(("arbitrary",)),
    )(*arrays)


def _rows_tile(rows, cols, itemsize=4, budget=2 * 1024 * 1024):
    return _tile(rows, max(16, budget // (cols * itemsize)), 16)


def _rowwise(fn, out_dtypes, *arrays, name):
    rows, cols = arrays[0].shape
    tr = _rows_tile(rows, cols)
    n_in = len(arrays)

    def body(*refs):
        res = fn(*[r[...] for r in refs[:n_in]])
        for o_ref, v in zip(refs[n_in:], res):
            o_ref[...] = v.astype(o_ref.dtype)

    spec = pl.BlockSpec((tr, cols), lambda i: (i, 0))
    return pl.pallas_call(
        body, name=name, grid=(rows // tr,), in_specs=[spec] * n_in, out_specs=tuple(spec for _ in out_dtypes),
        out_shape=tuple(jax.ShapeDtypeStruct((rows, cols), d) for d in out_dtypes),
        compiler_params=_params(("parallel",)),
    )(*arrays)


def _as2d(a):
    n = a.size
    for cols in (2048, 1024, 512, 256, 128):
        if n % cols == 0:
            return a.reshape(n // cols, cols)
    return a.reshape(1, n)


def _cast_bf16(a, name):
    flat = _as2d(a)
    return _rowwise(lambda v: (v,), (BF16,), flat, name=name)[0].reshape(a.shape)


def _norm_fwd(x, g, a, b=None, res=None, *, out_dtype, name):
    rows, d = x.shape
    tr = _rows_tile(rows, d)
    has_b, has_res = b is not None, res is not None

    def body(*refs):
        x_ref, g_ref, a_ref = refs[:3]
        pos = 3
        xv = x_ref[...]
        rstd = lax.rsqrt(jnp.mean(xv * xv, axis=-1, keepdims=True) + EPS)
        y = (xv * rstd * g_ref[...]) * a_ref[...]
        if has_b:
            y = y + refs[pos][...]
            pos += 1
        if has_res:
            y = y + refs[pos][...]
            pos += 1
        refs[pos][...] = y.astype(refs[pos].dtype)

    row = pl.BlockSpec((tr, d), lambda i: (i, 0))
    vec = pl.BlockSpec((1, d), lambda i: (0, 0))
    operands, specs = [x, g, a], [row, vec, vec]
    if has_b:
        operands.append(b)
        specs.append(vec)
    if has_res:
        operands.append(res)
        specs.append(row)
    return pl.pallas_call(
        body, name=name, grid=(rows // tr,), in_specs=specs, out_specs=row,
        out_shape=jax.ShapeDtypeStruct((rows, d), out_dtype), compiler_params=_params(("parallel",)),
    )(*operands)


def _norm_bwd(dy, x, g, a, extra=None, *, out_dtype, name):
    rows, d = x.shape
    tr = _rows_tile(rows, d)
    has_extra = extra is not None

    def body(*refs):
        dy_ref, x_ref, g_ref, a_ref = refs[:4]
        pos = 4
        extra_ref = None
        if has_extra:
            extra_ref = refs[pos]
            pos += 1
        dx_ref, dg_ref, da_ref, db_ref = refs[pos:pos + 4]

        @pl.when(pl.program_id(0) == 0)
        def _():
            dg_ref[...] = jnp.zeros_like(dg_ref)
            da_ref[...] = jnp.zeros_like(da_ref)
            db_ref[...] = jnp.zeros_like(db_ref)

        xv = x_ref[...]
        dyv = dy_ref[...].astype(F32)
        rstd = lax.rsqrt(jnp.mean(xv * xv, axis=-1, keepdims=True) + EPS)
        nrm = xv * rstd
        gv = g_ref[...]
        da_ref[...] += jnp.sum(dyv * (nrm * gv), axis=0, keepdims=True)
        db_ref[...] += jnp.sum(dyv, axis=0, keepdims=True)
        dt = dyv * a_ref[...]
        dg_ref[...] += jnp.sum(dt * nrm, axis=0, keepdims=True)
        dn = dt * gv
        dx = rstd * (dn - nrm * jnp.mean(dn * nrm, axis=-1, keepdims=True))
        if has_extra:
            dx = dx + extra_ref[...]
        dx_ref[...] = dx.astype(dx_ref.dtype)

    row = pl.BlockSpec((tr, d), lambda i: (i, 0))
    vec = pl.BlockSpec((1, d), lambda i: (0, 0))
    operands, specs = [dy, x, g, a], [row, row, vec, vec]
    if has_extra:
        operands.append(extra)
        specs.append(row)
    vshape = jax.ShapeDtypeStruct((1, d), F32)
    return pl.pallas_call(
        body, name=name, grid=(rows // tr,), in_specs=specs, out_specs=(row, vec, vec, vec),
        out_shape=(jax.ShapeDtypeStruct((rows, d), out_dtype), vshape, vshape, vshape),
        compiler_params=_params(("arbitrary",)),
    )(*operands)


def _loss_head(y, target):
    rows, d = y.shape
    tr = _rows_tile(rows, d)

    def body(y_ref, t_ref, dy_ref, loss_ref):
        @pl.when(pl.program_id(0) == 0)
        def _():
            loss_ref[...] = jnp.zeros_like(loss_ref)

        err = y_ref[...] - t_ref[...]
        dy_ref[...] = err * (1.0 / d)
        loss_ref[...] += jnp.sum(jnp.sum(err * err, axis=-1, keepdims=True), axis=0, keepdims=True) * (0.5 / d)

    row = pl.BlockSpec((tr, d), lambda i: (i, 0))
    return pl.pallas_call(
        body, name="loss_head", grid=(rows // tr,), in_specs=[row, row], out_specs=(row, _full((1, 1))),
        out_shape=(jax.ShapeDtypeStruct((rows, d), F32), jax.ShapeDtypeStruct((1, 1), F32)),
        compiler_params=_params(("arbitrary",)),
    )(y, target)


def _rope_partner(v):
    lane = lax.broadcasted_iota(jnp.int32, v.shape, 1)
    up = pltpu.roll(v, HEAD_DIM - ROPE_PAIRS, 1)
    down = pltpu.roll(v, ROPE_PAIRS, 1)
    return jnp.where((lane % (2 * ROPE_PAIRS)) < ROPE_PAIRS, up, down)


def _qk_fwd(proj, q_g, k_g, cos, sin, *, name):
    rows = proj.shape[0]
    tr = _tile(rows, 256, 16)
    width = ATTN_W + 2 * KV_W

    def body(p_ref, qg_ref, kg_ref, cos_ref, sin_ref, q_ref, k_ref, v_ref):
        cosv, sinv = cos_ref[...], sin_ref[...]
        for h in range(N_HEADS + N_KV):
            xv = p_ref[:, h * HEAD_DIM:(h + 1) * HEAD_DIM]
            gain = qg_ref[...] if h < N_HEADS else kg_ref[...]
            t = xv * lax.rsqrt(jnp.mean(xv * xv, axis=-1, keepdims=True) + EPS) * gain
            y = t * cosv + _rope_partner(t) * sinv
            if h < N_HEADS:
                q_ref[:, h * HEAD_DIM:(h + 1) * HEAD_DIM] = y.astype(BF16)
            else:
                k_ref[:, (h - N_HEADS) * HEAD_DIM:(h - N_HEADS + 1) * HEAD_DIM] = y.astype(BF16)
        v_ref[...] = p_ref[:, ATTN_W + KV_W:width].astype(BF16)

    vec = _full((1, HEAD_DIM))
    tab = pl.BlockSpec((tr, HEAD_DIM), lambda i: (i, 0))
    return pl.pallas_call(
        body, name=name, grid=(rows // tr,),
        in_specs=[pl.BlockSpec((tr, width), lambda i: (i, 0)), vec, vec, tab, tab],
        out_specs=(pl.BlockSpec((tr, ATTN_W), lambda i: (i, 0)), pl.BlockSpec((tr, KV_W), lambda i: (i, 0)),
                   pl.BlockSpec((tr, KV_W), lambda i: (i, 0))),
        out_shape=(jax.ShapeDtypeStruct((rows, ATTN_W), BF16), jax.ShapeDtypeStruct((rows, KV_W), BF16),
                   jax.ShapeDtypeStruct((rows, KV_W), BF16)),
        compiler_params=_params(("parallel",)),
    )(proj, q_g, k_g, cos, sin)


def _qk_bwd(dq, dk, proj, q_g, k_g, cos, sin, *, name):
    rows = proj.shape[0]
    tr = _tile(rows, 256, 16)
    width = ATTN_W + KV_W
    has_q = dq is not None

    def body(*refs):
        pos = 0
        dq_ref = None
        if has_q:
            dq_ref = refs[0]
            pos = 1
        dk_ref, p_ref, qg_ref, kg_ref, cos_ref, sin_ref, dp_ref, dqg_ref, dkg_ref = refs[pos:pos + 9]

        @pl.when(pl.program_id(0) == 0)
        def _():
            dqg_ref[...] = jnp.zeros_like(dqg_ref)
            dkg_ref[...] = jnp.zeros_like(dkg_ref)

        cosv, sinv = cos_ref[...], sin_ref[...]
        for h in range(N_HEADS + N_KV):
            cols = slice(h * HEAD_DIM, (h + 1) * HEAD_DIM)
            if h < N_HEADS and not has_q:
                dp_ref[:, cols] = jnp.zeros((tr, HEAD_DIM), dp_ref.dtype)
                continue
            if h < N_HEADS:
                dyv, gain, dgain_ref = dq_ref[:, cols], qg_ref[...], dqg_ref
            else:
                hk = h - N_HEADS
                dyv, gain, dgain_ref = dk_ref[:, hk * HEAD_DIM:(hk + 1) * HEAD_DIM], kg_ref[...], dkg_ref
            dyv = dyv.astype(F32)
            dt = dyv * cosv + _rope_partner(dyv * sinv)
            xv = p_ref[:, cols]
            rstd = lax.rsqrt(jnp.mean(xv * xv, axis=-1, keepdims=True) + EPS)
            nrm = xv * rstd
            dgain_ref[...] += jnp.sum(dt * nrm, axis=0, keepdims=True)
            dn = dt * gain
            dp_ref[:, cols] = (rstd * (dn - nrm * jnp.mean(dn * nrm, axis=-1, keepdims=True))).astype(dp_ref.dtype)

    vec = _full((1, HEAD_DIM))
    tab = pl.BlockSpec((tr, HEAD_DIM), lambda i: (i, 0))
    operands = ([dq] if has_q else []) + [dk, proj, q_g, k_g, cos, sin]
    specs = ([pl.BlockSpec((tr, ATTN_W), lambda i: (i, 0))] if has_q else []) + [
        pl.BlockSpec((tr, KV_W), lambda i: (i, 0)), pl.BlockSpec((tr, width), lambda i: (i, 0)), vec, vec, tab, tab]
    return pl.pallas_call(
        body, name=name, grid=(rows // tr,), in_specs=specs,
        out_specs=(pl.BlockSpec((tr, width), lambda i: (i, 0)), vec, vec),
        out_shape=(jax.ShapeDtypeStruct((rows, width), BF16), jax.ShapeDtypeStruct((1, HEAD_DIM), F32),
                   jax.ShapeDtypeStruct((1, HEAD_DIM), F32)),
        compiler_params=_params(("arbitrary",)),
    )(*operands)


def _attn_fwd(q, k, v):
    n_q, n_k = q.shape[0], k.shape[0]
    tq = _tile(n_q, 256, 16)
    gw = GROUP * HEAD_DIM
    scale = HEAD_DIM ** -0.5

    def body(q_ref, k_ref, v_ref, o_ref, lse_ref):
        kv, vv = k_ref[...], v_ref[...]
        for g in range(GROUP):
            cols = slice(g * HEAD_DIM, (g + 1) * HEAD_DIM)
            s = lax.dot_general(q_ref[:, cols], kv, (((1,), (1,)), ((), ())), preferred_element_type=F32) * scale
            m = jnp.max(s, axis=-1, keepdims=True)
            p = jnp.exp(s - m)
            l = jnp.sum(p, axis=-1, keepdims=True)
            o = jnp.dot(p.astype(BF16), vv, preferred_element_type=F32) / l
            o_ref[:, cols] = o.astype(o_ref.dtype)
            lse_ref[:, g:g + 1] = m + jnp.log(l)

    return pl.pallas_call(
        body, name="attn_fwd", grid=(N_KV, n_q // tq),
        in_specs=[pl.BlockSpec((tq, gw), lambda h, i: (i, h)), pl.BlockSpec((n_k, HEAD_DIM), lambda h, i: (0, h)),
                  pl.BlockSpec((n_k, HEAD_DIM), lambda h, i: (0, h))],
        out_specs=(pl.BlockSpec((tq, gw), lambda h, i: (i, h)), pl.BlockSpec((None, tq, GROUP), lambda h, i: (h, i, 0))),
        out_shape=(jax.ShapeDtypeStruct((n_q, ATTN_W), BF16), jax.ShapeDtypeStruct((N_KV, n_q, GROUP), F32)),
        compiler_params=_params(("parallel", "parallel")),
    )(q, k, v)


def _attn_bwd(q, k, v, o, lse, do):
    n_q, n_k = q.shape[0], k.shape[0]
    tq = _tile(n_q, 256, 16)
    gw = GROUP * HEAD_DIM
    scale = HEAD_DIM ** -0.5

    def body(q_ref, k_ref, v_ref, o_ref, lse_ref, do_ref, dq_ref, dk_ref, dv_ref):
        @pl.when(pl.program_id(1) == 0)
        def _():
            dk_ref[...] = jnp.zeros_like(dk_ref)
            dv_ref[...] = jnp.zeros_like(dv_ref)

        kv, vv = k_ref[...], v_ref[...]
        for g in range(GROUP):
            cols = slice(g * HEAD_DIM, (g + 1) * HEAD_DIM)
            qg = q_ref[:, cols]
            dof = do_ref[:, cols].astype(F32)
            dog = dof.astype(BF16)
            s = lax.dot_general(qg, kv, (((1,), (1,)), ((), ())), preferred_element_type=F32) * scale
            p = jnp.exp(s - lse_ref[:, g:g + 1])
            delta = jnp.sum(dof * o_ref[:, cols].astype(F32), axis=-1, keepdims=True)
            dp = lax.dot_general(dog, vv, (((1,), (1,)), ((), ())), preferred_element_type=F32)
            ds = (p * (dp - delta) * scale).astype(BF16)
            pb = p.astype(BF16)
            dq_ref[:, cols] = jnp.dot(ds, kv, preferred_element_type=F32)
            dk_ref[...] += lax.dot_general(ds, qg, (((0,), (0,)), ((), ())), preferred_element_type=F32)
            dv_ref[...] += lax.dot_general(pb, dog, (((0,), (0,)), ((), ())), preferred_element_type=F32)

    qspec = pl.BlockSpec((tq, gw), lambda h, i: (i, h))
    kspec = pl.BlockSpec((n_k, HEAD_DIM), lambda h, i: (0, h))
    return pl.pallas_call(
        body, name="attn_bwd", grid=(N_KV, n_q // tq),
        in_specs=[qspec, kspec, kspec, qspec, pl.BlockSpec((None, tq, GROUP), lambda h, i: (h, i, 0)), qspec],
        out_specs=(qspec, kspec, kspec),
        out_shape=(jax.ShapeDtypeStruct((n_q, ATTN_W), F32), jax.ShapeDtypeStruct((n_k, KV_W), F32),
                   jax.ShapeDtypeStruct((n_k, KV_W), F32)),
        compiler_params=_params(("parallel", "arbitrary")),
    )(q, k, v, o, lse, do)


CONV_COLS = 256
XR_COL0 = ATTN_W + 2 * KV_W


def _shift_rows(v, off):
    if off == 0:
        return v
    n = v.shape[0]
    rolled = pltpu.roll(v, (-off) % n, 0)
    t = lax.broadcasted_iota(jnp.int32, v.shape, 0)
    keep = (t + off >= 0) & (t + off < n)
    return jnp.where(keep, rolled, 0.0)


def _conv_fwd(proj, w, b, *, name):
    rows = proj.shape[0]
    blk0 = XR_COL0 // CONV_COLS

    def body(x_ref, w_ref, b_ref, y_ref):
        xv = x_ref[...]
        y = b_ref[...] + jnp.zeros_like(xv)
        for j in range(CONV_W):
            y = y + _shift_rows(xv, j - CONV_W // 2) * w_ref[j:j + 1, :]
        y_ref[...] = y

    return pl.pallas_call(
        body, name=name, grid=(D_RNN // CONV_COLS,),
        in_specs=[pl.BlockSpec((rows, CONV_COLS), lambda i: (0, blk0 + i)),
                  pl.BlockSpec((CONV_W, CONV_COLS), lambda i: (0, i)), pl.BlockSpec((1, CONV_COLS), lambda i: (0, i))],
        out_specs=pl.BlockSpec((rows, CONV_COLS), lambda i: (0, i)),
        out_shape=jax.ShapeDtypeStruct((rows, D_RNN), F32), compiler_params=_params(("parallel",)),
    )(proj, w, b)


def _conv_bwd(d1, d2, proj, w, *, name):
    rows = proj.shape[0]
    blk0 = XR_COL0 // CONV_COLS

    def body(d1_ref, d2_ref, x_ref, w_ref, dx_ref, dw_ref, db_ref):
        dv = d1_ref[...] + d2_ref[...]
        xv = x_ref[...]
        dx = jnp.zeros_like(dv)
        for j in range(CONV_W):
            off = j - CONV_W // 2
            dx = dx + _shift_rows(dv, -off) * w_ref[j:j + 1, :]
            dw_ref[j:j + 1, :] = jnp.sum(dv * _shift_rows(xv, off), axis=0, keepdims=True)
        dx_ref[...] = dx.astype(dx_ref.dtype)
        db_ref[...] = jnp.sum(dv, axis=0, keepdims=True)

    col = pl.BlockSpec((rows, CONV_COLS), lambda i: (0, i))
    return pl.pallas_call(
        body, name=name, grid=(D_RNN // CONV_COLS,),
        in_specs=[col, col, pl.BlockSpec((rows, CONV_COLS), lambda i: (0, blk0 + i)),
                  pl.BlockSpec((CONV_W, CONV_COLS), lambda i: (0, i))],
        out_specs=(col, pl.BlockSpec((CONV_W, CONV_COLS), lambda i: (0, i)), pl.BlockSpec((1, CONV_COLS), lambda i: (0, i))),
        out_shape=(jax.ShapeDtypeStruct((rows, D_RNN), BF16), jax.ShapeDtypeStruct((CONV_W, D_RNN), F32),
                   jax.ShapeDtypeStruct((1, D_RNN), F32)),
        compiler_params=_params(("parallel",)),
    )(d1, d2, proj, w)


RNN_TB = 256
SCAN_ROWS = 8


def _sigmoid(z):
    return 1.0 / (1.0 + jnp.exp(-z))


def _softplus(z):
    return jnp.maximum(z, 0.0) + jnp.log(1.0 + jnp.exp(-jnp.abs(z)))


def _one_minus_exp(y):
    series = -y * (1.0 + y * (0.5 + y * (1.0 / 6.0 + y * (1.0 / 24.0))))
    return jnp.where(y > -0.03, series, 1.0 - jnp.exp(y))


def _rglru_gates(xv, wa_ref, ba_ref, wx_ref, bx_ref, lam_ref):
    xb = xv.astype(BF16)
    zr = jnp.concatenate([jnp.dot(xb[:, n * RNN_BW:(n + 1) * RNN_BW], wa_ref[n].astype(BF16),
                                  preferred_element_type=F32) for n in range(RNN_BLOCKS)], axis=-1) + ba_ref[...]
    zi = jnp.concatenate([jnp.dot(xb[:, n * RNN_BW:(n + 1) * RNN_BW], wx_ref[n].astype(BF16),
                                  preferred_element_type=F32) for n in range(RNN_BLOCKS)], axis=-1) + bx_ref[...]
    r = _sigmoid(zr)
    gi = _sigmoid(zi)
    sp = _softplus(-lam_ref[...])
    log_a = -RG_C * r * sp
    a = jnp.exp(log_a)
    s = jnp.sqrt(_one_minus_exp(2.0 * log_a))
    return r, gi, sp, a, s


def _scan_rows(n_rows, reverse, step_fn, carry):
    groups = n_rows // SCAN_ROWS

    def trip(gidx, carry):
        gi = (groups - 1 - gidx) if reverse else gidx
        base = pl.multiple_of(gi * SCAN_ROWS, SCAN_ROWS)
        return step_fn(base, carry)

    return lax.fori_loop(0, groups, trip, carry)


def _rglru_fwd(xs, wa, ba, wx, bx, lam, *, reverse, name):
    rows = xs.shape[0]
    tb = _tile(rows, RNN_TB, SCAN_ROWS)
    nb = rows // tb
    order = (lambda i: (nb - 1 - i, 0)) if reverse else (lambda i: (i, 0))

    def body(x_ref, wa_ref, ba_ref, wx_ref, bx_ref, lam_ref, h_ref, hp_ref, a_s, b_s, state):
        @pl.when(pl.program_id(0) == 0)
        def _():
            state[...] = jnp.zeros_like(state)

        xv = x_ref[...]
        _, gi, _, a, s = _rglru_gates(xv, wa_ref, ba_ref, wx_ref, bx_ref, lam_ref)
        a_s[...] = a
        b_s[...] = s * (gi * xv)

        def group(base, h):
            av = a_s[pl.ds(base, SCAN_ROWS), :]
            bv = b_s[pl.ds(base, SCAN_ROWS), :]
            outs, prevs = [None] * SCAN_ROWS, [None] * SCAN_ROWS
            for k in range(SCAN_ROWS):
                r_ = SCAN_ROWS - 1 - k if reverse else k
                prevs[r_] = h
                h = av[r_:r_ + 1, :] * h + bv[r_:r_ + 1, :]
                outs[r_] = h
            h_ref[pl.ds(base, SCAN_ROWS), :] = jnp.concatenate(outs, axis=0)
            hp_ref[pl.ds(base, SCAN_ROWS), :] = jnp.concatenate(prevs, axis=0)
            return h

        state[0:1, :] = _scan_rows(tb, reverse, group, state[0:1, :])

    blk = pl.BlockSpec((tb, D_RNN), order)
    wspec = _full((RNN_BLOCKS, RNN_BW, RNN_BW))
    vec = _full((1, D_RNN))
    return pl.pallas_call(
        body, name=name, grid=(nb,), in_specs=[blk, wspec, vec, wspec, vec, vec], out_specs=(blk, blk),
        out_shape=(jax.ShapeDtypeStruct((rows, D_RNN), F32), jax.ShapeDtypeStruct((rows, D_RNN), F32)),
        scratch_shapes=[pltpu.VMEM((tb, D_RNN), F32), pltpu.VMEM((tb, D_RNN), F32), pltpu.VMEM((SCAN_ROWS, D_RNN), F32)],
        compiler_params=_params(("arbitrary",)),
    )(xs, wa, ba, wx, bx, lam)


def _rglru_bwd(xs, h_prev, dh, wa, ba, wx, bx, lam, *, reverse, name):
    rows = xs.shape[0]
    tb = _tile(rows, RNN_TB, SCAN_ROWS)
    nb = rows // tb
    back = not reverse
    order = (lambda i: (nb - 1 - i, 0)) if back else (lambda i: (i, 0))

    def body(x_ref, hp_ref, dh_ref, wa_ref, ba_ref, wx_ref, bx_ref, lam_ref,
             dx_ref, dwa_ref, dba_ref, dwx_ref, dbx_ref, dlam_ref, a_s, g_s, state):
        @pl.when(pl.program_id(0) == 0)
        def _():
            state[...] = jnp.zeros_like(state)
            dwa_ref[...] = jnp.zeros_like(dwa_ref)
            dwx_ref[...] = jnp.zeros_like(dwx_ref)
            dba_ref[...] = jnp.zeros_like(dba_ref)
            dbx_ref[...] = jnp.zeros_like(dbx_ref)
            dlam_ref[...] = jnp.zeros_like(dlam_ref)

        xv = x_ref[...]
        r, gi, sp, a, s = _rglru_gates(xv, wa_ref, ba_ref, wx_ref, bx_ref, lam_ref)
        a_s[...] = a

        def group(base, carry):
            av = a_s[pl.ds(base, SCAN_ROWS), :]
            dv = dh_ref[pl.ds(base, SCAN_ROWS), :]
            outs = [None] * SCAN_ROWS
            for k in range(SCAN_ROWS):
                r_ = SCAN_ROWS - 1 - k if back else k
                gt = dv[r_:r_ + 1, :] + carry
                outs[r_] = gt
                carry = av[r_:r_ + 1, :] * gt
            g_s[pl.ds(base, SCAN_ROWS), :] = jnp.concatenate(outs, axis=0)
            return carry

        state[0:1, :] = _scan_rows(tb, back, group, state[0:1, :])

        gv = g_s[...]
        d_a = gv * hp_ref[...]
        d_s = gv * (gi * xv)
        d_gi = gv * (s * xv)
        dx = gv * (s * gi)
        d_log_a = d_a * a - d_s * (a * a) / s
        d_r = d_log_a * (-RG_C * sp)
        lamv = lam_ref[...]
        d_sp = jnp.sum(d_log_a * (-RG_C * r), axis=0, keepdims=True)
        dlam_ref[...] += d_sp * (-_sigmoid(-lamv))
        d_zr = d_r * r * (1.0 - r)
        d_zi = d_gi * gi * (1.0 - gi)
        dba_ref[...] += jnp.sum(d_zr, axis=0, keepdims=True)
        dbx_ref[...] += jnp.sum(d_zi, axis=0, keepdims=True)
        xb = xv.astype(BF16)
        zrb, zib = d_zr.astype(BF16), d_zi.astype(BF16)
        parts = []
        for n in range(RNN_BLOCKS):
            cols = slice(n * RNN_BW, (n + 1) * RNN_BW)
            dwa_ref[n] += lax.dot_general(xb[:, cols], zrb[:, cols], (((0,), (0,)), ((), ())), preferred_element_type=F32)
            dwx_ref[n] += lax.dot_general(xb[:, cols], zib[:, cols], (((0,), (0,)), ((), ())), preferred_element_type=F32)
            parts.append(
                lax.dot_general(zrb[:, cols], wa_ref[n].astype(BF16), (((1,), (1,)), ((), ())), preferred_element_type=F32)
                + lax.dot_general(zib[:, cols], wx_ref[n].astype(BF16), (((1,), (1,)), ((), ())), preferred_element_type=F32))
        dx_ref[...] = dx + jnp.concatenate(parts, axis=-1)

    blk = pl.BlockSpec((tb, D_RNN), order)
    wspec = _full((RNN_BLOCKS, RNN_BW, RNN_BW))
    vec = _full((1, D_RNN))
    wshape = jax.ShapeDtypeStruct((RNN_BLOCKS, RNN_BW, RNN_BW), F32)
    vshape = jax.ShapeDtypeStruct((1, D_RNN), F32)
    return pl.pallas_call(
        body, name=name, grid=(nb,), in_specs=[blk, blk, blk, wspec, vec, wspec, vec, vec],
        out_specs=(blk, wspec, vec, wspec, vec, vec),
        out_shape=(jax.ShapeDtypeStruct((rows, D_RNN), F32), wshape, vshape, wshape, vshape, vshape),
        scratch_shapes=[pltpu.VMEM((tb, D_RNN), F32), pltpu.VMEM((tb, D_RNN), F32), pltpu.VMEM((SCAN_ROWS, D_RNN), F32)],
        compiler_params=_params(("arbitrary",)),
    )(xs, h_prev, dh, wa, ba, wx, bx, lam)


def _gelu(z):
    return 0.5 * z * (1.0 + jnp.tanh(GELU_C * (z + 0.044715 * z * z * z)))


def _gelu_grad(z):
    t = jnp.tanh(GELU_C * (z + 0.044715 * z * z * z))
    return 0.5 * (1.0 + t) + 0.5 * z * (1.0 - t * t) * (GELU_C * (1.0 + 3.0 * 0.044715 * z * z))


GATE_COL0 = XR_COL0 + D_RNN


RNN_OUT_COLS = 512


def _rnn_out_specs(rows, hf_off, hb_off):
    tr = _tile(rows, 256, 16)
    assert hf_off % tr == 0 and hb_off % tr == 0 and GATE_COL0 % RNN_OUT_COLS == 0
    fo, bo, go = hf_off // tr, hb_off // tr, GATE_COL0 // RNN_OUT_COLS
    hf_spec = pl.BlockSpec((tr, RNN_OUT_COLS), lambda i, j: (i + fo, j))
    hb_spec = pl.BlockSpec((tr, RNN_OUT_COLS), lambda i, j: (i + bo, j))
    gate_spec = pl.BlockSpec((tr, RNN_OUT_COLS), lambda i, j: (i, j + go))
    out_spec = pl.BlockSpec((tr, RNN_OUT_COLS), lambda i, j: (i, j))
    return (rows // tr, D_RNN // RNN_OUT_COLS), hf_spec, hb_spec, gate_spec, out_spec


def _rnn_out_fwd(hf, hb, proj, hf_off, hb_off):
    rows = proj.shape[0]
    grid, hf_spec, hb_spec, gate_spec, out_spec = _rnn_out_specs(rows, hf_off, hb_off)

    def body(hf_ref, hb_ref, g_ref, o_ref):
        o_ref[...] = ((hf_ref[...] + hb_ref[...]) * _gelu(g_ref[...])).astype(o_ref.dtype)

    return pl.pallas_call(
        body, name="rnn_out_fwd", grid=grid, in_specs=[hf_spec, hb_spec, gate_spec], out_specs=out_spec,
        out_shape=jax.ShapeDtypeStruct((rows, D_RNN), BF16), compiler_params=_params(("parallel", "parallel")),
    )(hf, hb, proj)


def _rnn_out_bwd(d_cat, hf, hb, proj, hf_off, hb_off):
    rows = proj.shape[0]
    grid, hf_spec, hb_spec, gate_spec, out_spec = _rnn_out_specs(rows, hf_off, hb_off)
    do = ATTN_W // RNN_OUT_COLS

    def body(d_ref, hf_ref, hb_ref, g_ref, dh_ref, dg_ref):
        dv, gv = d_ref[...].astype(F32), g_ref[...]
        dh_ref[...] = dv * _gelu(gv)
        dg_ref[...] = (dv * (hf_ref[...] + hb_ref[...]) * _gelu_grad(gv)).astype(dg_ref.dtype)

    tr = out_spec.block_shape[0]
    return pl.pallas_call(
        body, name="rnn_out_bwd", grid=grid,
        in_specs=[pl.BlockSpec((tr, RNN_OUT_COLS), lambda i, j: (i, j + do)), hf_spec, hb_spec, gate_spec],
        out_specs=(out_spec, out_spec),
        out_shape=(jax.ShapeDtypeStruct((rows, D_RNN), F32), jax.ShapeDtypeStruct((rows, D_RNN), BF16)),
        compiler_params=_params(("parallel", "parallel")),
    )(d_cat, hf, hb, proj)


def _gmlp_parts(z_ref, vg_ref, vb_ref, d_gm):
    zu, zv = z_ref[:, :d_gm], z_ref[:, d_gm:]
    u = _gelu(zu)
    v = _gelu(zv)
    mu = jnp.mean(v, axis=-1, keepdims=True)
    vc = v - mu
    rstd = lax.rsqrt(jnp.mean(vc * vc, axis=-1, keepdims=True) + EPS)
    vhat = vc * rstd
    vn = vhat * vg_ref[...] + vb_ref[...]
    return zu, zv, u, vhat, rstd, vn


def _gmlp_fwd(z, v_g, v_b, w_sp, b_sp_t):
    rows, d_gm = z.shape[0], z.shape[1] // 2
    tr = _tile(rows, 256, CHUNK)
    gwid = d_gm // GM_GROUPS

    def body(z_ref, vg_ref, vb_ref, w_ref, b_ref, o_ref):
        _, _, u, _, _, vn = _gmlp_parts(z_ref, vg_ref, vb_ref, d_gm)
        vnb = vn.astype(BF16)
        for g in range(GM_GROUPS):
            wg = w_ref[g].astype(BF16)
            for c in range(tr // CHUNK):
                rs, cs = slice(c * CHUNK, (c + 1) * CHUNK), slice(g * gwid, (g + 1) * gwid)
                sv = jnp.dot(wg, vnb[rs, cs], preferred_element_type=F32) + b_ref[:, g:g + 1]
                o_ref[rs, cs] = (u[rs, cs] * sv).astype(o_ref.dtype)

    return pl.pallas_call(
        body, name="gmlp_fwd", grid=(rows // tr,),
        in_specs=[pl.BlockSpec((tr, 2 * d_gm), lambda i: (i, 0)), _full((1, d_gm)), _full((1, d_gm)),
                  _full(w_sp.shape), _full(b_sp_t.shape)],
        out_specs=pl.BlockSpec((tr, d_gm), lambda i: (i, 0)),
        out_shape=jax.ShapeDtypeStruct((rows, d_gm), BF16), compiler_params=_params(("parallel",)),
    )(z, v_g, v_b, w_sp, b_sp_t)


def _gmlp_bwd(z, dgate, v_g, v_b, w_sp, b_sp_t):
    rows, d_gm = z.shape[0], z.shape[1] // 2
    tr = _tile(rows, 256, CHUNK)
    gwid = d_gm // GM_GROUPS

    def body(z_ref, dg_ref, vg_ref, vb_ref, w_ref, b_ref, dz_ref, dbin_ref, dvg_ref, dvb_ref, dw_ref, dbs_ref, dvn_s):
        @pl.when(pl.program_id(0) == 0)
        def _():
            dbin_ref[...] = jnp.zeros_like(dbin_ref)
            dvg_ref[...] = jnp.zeros_like(dvg_ref)
            dvb_ref[...] = jnp.zeros_like(dvb_ref)
            dw_ref[...] = jnp.zeros_like(dw_ref)
            dbs_ref[...] = jnp.zeros_like(dbs_ref)

        zu, zv, u, vhat, rstd, vn = _gmlp_parts(z_ref, vg_ref, vb_ref, d_gm)
        vnb = vn.astype(BF16)
        dgv = dg_ref[...].astype(F32)
        dsv = dgv * u
        dsvb = dsv.astype(BF16)
        for g in range(GM_GROUPS):
            wg = w_ref[g].astype(BF16)
            cs = slice(g * gwid, (g + 1) * gwid)
            for c in range(tr // CHUNK):
                rs = slice(c * CHUNK, (c + 1) * CHUNK)
                sv = jnp.dot(wg, vnb[rs, cs], preferred_element_type=F32) + b_ref[:, g:g + 1]
                dz_ref[rs, cs] = (dgv[rs, cs] * sv * _gelu_grad(zu[rs, cs])).astype(dz_ref.dtype)
                dw_ref[g] += lax.dot_general(dsvb[rs, cs], vnb[rs, cs], (((1,), (1,)), ((), ())),
                                             preferred_element_type=F32)
                dbs_ref[:, g:g + 1] += jnp.sum(dsv[rs, cs], axis=-1, keepdims=True)
                dvn_s[rs, cs] = lax.dot_general(wg, dsvb[rs, cs], (((0,), (0,)), ((), ())), preferred_element_type=F32)
        dvn = dvn_s[...]
        dvg_ref[...] += jnp.sum(dvn * vhat, axis=0, keepdims=True)
        dvb_ref[...] += jnp.sum(dvn, axis=0, keepdims=True)
        dvh = dvn * vg_ref[...]
        dv = rstd * (dvh - jnp.mean(dvh, axis=-1, keepdims=True) - vhat * jnp.mean(dvh * vhat, axis=-1, keepdims=True))
        dzv = dv * _gelu_grad(zv)
        dz_ref[:, d_gm:] = dzv.astype(dz_ref.dtype)
        dbin_ref[:, d_gm:] += jnp.sum(dzv, axis=0, keepdims=True)
        dbin_ref[:, :d_gm] += jnp.sum(dz_ref[:, :d_gm].astype(F32), axis=0, keepdims=True)

    return pl.pallas_call(
        body, name="gmlp_bwd", grid=(rows // tr,),
        in_specs=[pl.BlockSpec((tr, 2 * d_gm), lambda i: (i, 0)), pl.BlockSpec((tr, d_gm), lambda i: (i, 0)),
                  _full((1, d_gm)), _full((1, d_gm)), _full(w_sp.shape), _full(b_sp_t.shape)],
        out_specs=(pl.BlockSpec((tr, 2 * d_gm), lambda i: (i, 0)), _full((1, 2 * d_gm)), _full((1, d_gm)),
                   _full((1, d_gm)), _full(w_sp.shape), _full(b_sp_t.shape)),
        out_shape=(jax.ShapeDtypeStruct((rows, 2 * d_gm), BF16), jax.ShapeDtypeStruct((1, 2 * d_gm), F32),
                   jax.ShapeDtypeStruct((1, d_gm), F32), jax.ShapeDtypeStruct((1, d_gm), F32),
                   jax.ShapeDtypeStruct(w_sp.shape, F32), jax.ShapeDtypeStruct(b_sp_t.shape, F32)),
        scratch_shapes=[pltpu.VMEM((tr, d_gm), F32)],
        compiler_params=_params(("arbitrary",)),
    )(z, dgate, v_g, v_b, w_sp, b_sp_t)


def _adamw_math(w, g, m, v):
    m = ADAM_B1 * m + (1.0 - ADAM_B1) * g
    v = ADAM_B2 * v + (1.0 - ADAM_B2) * (g * g)
    m_hat = m / (1.0 - ADAM_B1 ** ADAM_STEP)
    v_hat = v / (1.0 - ADAM_B2 ** ADAM_STEP)
    delta = -ADAM_LR * (m_hat / (jnp.sqrt(v_hat) + ADAM_EPS) + ADAM_WD * w)
    return delta, m, v


def _adamw(w, g, m, v, name):
    shape = w.shape
    outs = _rowwise(_adamw_math, (F32, F32, F32), _as2d(w), _as2d(g), _as2d(m), _as2d(v), name=name)
    return (g.reshape(shape),) + tuple(o.reshape(shape) for o in outs)


PACK_COLS = 1024


def _pack(arrays):
    flat = jnp.concatenate([a.reshape(-1).astype(F32) for a in arrays])
    pad = (-flat.size) % (8 * PACK_COLS)
    return jnp.pad(flat, (0, pad)).reshape(-1, PACK_COLS)


def _unpack(flat, shapes):
    out, pos = [], 0
    for shp in shapes:
        n = math.prod(shp)
        out.append(flat[pos:pos + n].reshape(shp))
        pos += n
    return out


def _unpack_devices(packed8, shapes):
    flat8 = packed8.reshape(N_DEV, -1)
    out, pos = [], 0
    for shp in shapes:
        n = math.prod(shp)
        out.append(flat8[:, pos:pos + n].reshape((N_DEV,) + tuple(shp)))
        pos += n
    return out


def _sum_devices(g8):
    _, rows, cols = g8.shape
    tr = _rows_tile(rows, cols, budget=256 * 1024)

    def body(g_ref, o_ref):
        acc = g_ref[0]
        for d in range(1, N_DEV):
            acc = acc + g_ref[d]
        o_ref[...] = acc

    return pl.pallas_call(
        body, name="sum_devices", grid=(rows // tr,), in_specs=[pl.BlockSpec((N_DEV, tr, cols), lambda i: (0, i, 0))],
        out_specs=pl.BlockSpec((tr, cols), lambda i: (i, 0)), out_shape=jax.ShapeDtypeStruct((rows, cols), F32),
        compiler_params=_params(("parallel",)),
    )(g8)


def _place():
    return lax.axis_index("x"), lax.axis_index("y"), lax.axis_index("c")


def _other_chips(x, y):
    return [(1 - x, y), (x, 1 - y), (1 - x, 1 - y)]


def _remote(src, dst, send_sem, recv_sem, to):
    return pltpu.make_async_remote_copy(src_ref=src, dst_ref=dst, send_sem=send_sem, recv_sem=recv_sem, device_id=to,
                                        device_id_type=MESH)


def _comm_call(body, name, operands, out_shapes, n_remote, n_local):
    return pl.pallas_call(
        body, name=name, out_shape=tuple(out_shapes), in_specs=[ANY] * len(operands), out_specs=tuple(ANY for _ in out_shapes),
        scratch_shapes=[pltpu.SemaphoreType.DMA((n_remote,)), pltpu.SemaphoreType.DMA((n_remote,)),
                        pltpu.SemaphoreType.DMA((max(n_local, 1),))],
    )(*operands)


def _allgather8(arrs, name):
    n = len(arrs)

    def body(*refs):
        ins, outs = refs[:n], refs[n:2 * n]
        send, recv, lsem = refs[2 * n:]
        x, y, c = _place()
        me, sib = (x, y, c), (x, y, 1 - c)
        chips = _other_chips(x, y)

        def slot(t, px, py, pc):
            return outs[t].at[4 * px + 2 * py + pc]

        def cp(t, k, block, to, from_input=False):
            src = ins[t] if from_input else slot(t, *block)
            return _remote(src, slot(t, *block), send.at[7 * t + k], recv.at[7 * t + k], to)

        mine = [pltpu.make_async_copy(ins[t], slot(t, *me), lsem.at[t]) for t in range(n)]
        for cpy in mine:
            cpy.start()
        first = []
        for t in range(n):
            first.append(cp(t, 0, me, sib, True))
            first += [cp(t, 1 + j, me, (*chip, c), True) for j, chip in enumerate(chips)]
        for cpy in first:
            cpy.start()
        passed = []
        for t in range(n):
            for j, chip in enumerate(chips):
                cp(t, 1 + j, (*chip, c), me).wait_recv()
                fwd = cp(t, 4 + j, (*chip, c), sib)
                fwd.start()
                passed.append(fwd)
        for t in range(n):
            cp(t, 0, sib, me).wait_recv()
            for j, chip in enumerate(chips):
                cp(t, 4 + j, (*chip, 1 - c), me).wait_recv()
        for cpy in first + passed:
            cpy.wait_send()
        for cpy in mine:
            cpy.wait()

    outs = _comm_call(body, name, arrs, [jax.ShapeDtypeStruct((N_DEV,) + a.shape, a.dtype) for a in arrs], 7 * n, n)
    return list(outs)


def _units(arrs):
    units = []
    for t, a in enumerate(arrs):
        units += [(t, None)] if a.ndim == 2 else [(t, i) for i in range(a.shape[0])]
    return units


def _gather_weights(shards):
    units = _units(shards)
    n_in, n_u = len(shards), len(units)

    def body(*refs):
        ins, outs = refs[:n_in], refs[n_in:n_in + n_u]
        send, recv, lsem = refs[n_in + n_u:]
        x, y, c = _place()
        me, sib, q = (x, y, c), (x, y, 1 - c), 2 * x + y
        chips = _other_chips(x, y)
        local, sent = [], []
        for u, (t, layer) in enumerate(units):
            src = ins[t] if layer is None else ins[t].at[layer]
            half = src.shape[0] // 2
            mine_rows = pl.ds(c * half, half)
            cpy = pltpu.make_async_copy(src, outs[u].at[q], lsem.at[u])
            cpy.start()
            local.append(cpy)
            for j, chip in enumerate(chips):
                cpy = _remote(src.at[mine_rows], outs[u].at[q, mine_rows], send.at[6 * u + j], recv.at[6 * u + j], (*chip, c))
                cpy.start()
                sent.append(cpy)
        for u in range(n_u):
            half = outs[u].shape[1] // 2
            mine_rows = pl.ds(c * half, half)
            for j, chip in enumerate(chips):
                landed = outs[u].at[2 * chip[0] + chip[1], mine_rows]
                _remote(landed, landed, send.at[6 * u + j], recv.at[6 * u + j], me).wait_recv()
                cpy = _remote(landed, landed, send.at[6 * u + 3 + j], recv.at[6 * u + 3 + j], sib)
                cpy.start()
                sent.append(cpy)
        for u in range(n_u):
            half = outs[u].shape[1] // 2
            other_rows = pl.ds((1 - c) * half, half)
            for j, chip in enumerate(chips):
                landed = outs[u].at[2 * chip[0] + chip[1], other_rows]
                _remote(landed, landed, send.at[6 * u + 3 + j], recv.at[6 * u + 3 + j], me).wait_recv()
        for cpy in sent:
            cpy.wait_send()
        for cpy in local:
            cpy.wait()

    shapes = [jax.ShapeDtypeStruct((N_CHIPS,) + shards[t].shape[-2:], shards[t].dtype) for t, _ in units]
    return list(_comm_call(body, "gather_weights", shards, shapes, 6 * n_u, n_u))


def _exchange_halves(grads):
    n = len(grads)

    def body(*refs):
        ins, outs = refs[:n], refs[n:2 * n]
        send, recv, _ = refs[2 * n:]
        x, y, c = _place()
        sib = (x, y, 1 - c)
        sent = []
        for k in range(n):
            half = ins[k].shape[1] // 2
            cpy = _remote(ins[k].at[pl.ds(0, N_CHIPS), pl.ds((1 - c) * half, half)], outs[k], send.at[k], recv.at[k], sib)
            cpy.start()
            sent.append(cpy)
        for cpy in sent:
            cpy.wait()

    shapes = [jax.ShapeDtypeStruct((N_CHIPS, g.shape[1] // 2, g.shape[2]), g.dtype) for g in grads]
    return list(_comm_call(body, "exchange_halves", grads, shapes, n, 0))


def _chips_all_to_all(sums):
    n = len(sums)

    def body(*refs):
        ins, outs = refs[:n], refs[n:2 * n]
        send, recv, _ = refs[2 * n:]
        x, y, c = _place()
        sent = []
        for k in range(n):
            for j, chip in enumerate(_other_chips(x, y)):
                cpy = _remote(ins[k].at[2 * chip[0] + chip[1]], outs[k].at[j], send.at[3 * k + j], recv.at[3 * k + j], (*chip, c))
                cpy.start()
                sent.append(cpy)
        for cpy in sent:
            cpy.wait()

    shapes = [jax.ShapeDtypeStruct((N_CHIPS - 1,) + s.shape[1:], s.dtype) for s in sums]
    return list(_comm_call(body, "chips_all_to_all", sums, shapes, 3 * n, 0))


def _join_halves(halves, layout):
    n = len(halves)
    params = sorted({p for p, _ in layout})
    n_p = len(params)
    layers = {p: max([(l if l is not None else -1) for pp, l in layout if pp == p]) + 1 for p in params}

    def body(*refs):
        ins, outs = refs[:n], refs[n:n + n_p]
        send, recv, lsem = refs[n + n_p:]
        x, y, c = _place()
        sib = (x, y, 1 - c)
        local, sent = [], []
        for k, (p, layer) in enumerate(layout):
            half = ins[k].shape[0]
            rows = pl.ds(c * half, half)
            out = outs[params.index(p)]
            dst = out.at[rows] if layer is None else out.at[layer, rows]
            cpy = pltpu.make_async_copy(ins[k], dst, lsem.at[k])
            cpy.start()
            local.append(cpy)
            cpy = _remote(ins[k], dst, send.at[k], recv.at[k], sib)
            cpy.start()
            sent.append(cpy)
        for cpy in sent:
            cpy.wait()
        for cpy in local:
            cpy.wait()

    shapes = []
    for p in params:
        k = [i for i, (pp, _) in enumerate(layout) if pp == p][0]
        full = (2 * halves[k].shape[0], halves[k].shape[1])
        shapes.append(jax.ShapeDtypeStruct(((layers[p],) + full) if layers[p] else full, halves[k].dtype))
    return list(_comm_call(body, "join_halves", halves, shapes, n, n))


def _add_halves(grad, other, place):
    _, rows, cols = grad.shape
    half = rows // 2
    tr = _rows_tile(half, cols, itemsize=2, budget=1024 * 1024)
    per_half = half // tr

    def body(place_ref, g_ref, o_ref, s_ref):
        s_ref[...] = (g_ref[...].astype(F32) + o_ref[...].astype(F32)).astype(s_ref.dtype)

    return pl.pallas_call(
        body, name="add_halves", out_shape=jax.ShapeDtypeStruct((N_CHIPS, half, cols), grad.dtype),
        grid_spec=pltpu.PrefetchScalarGridSpec(
            num_scalar_prefetch=1, grid=(N_CHIPS, per_half),
            in_specs=[pl.BlockSpec((None, tr, cols), lambda k, i, pr: (k, pr[1] * per_half + i, 0)),
                      pl.BlockSpec((None, tr, cols), lambda k, i, pr: (k, i, 0))],
            out_specs=pl.BlockSpec((None, tr, cols), lambda k, i, pr: (k, i, 0))),
        compiler_params=_params(("parallel", "parallel")),
    )(place, grad, other)


def _add_chips(sums, others, place):
    _, half, cols = sums.shape
    tr = _rows_tile(half, cols, itemsize=4, budget=1024 * 1024)

    def body(place_ref, s_ref, o_ref, t_ref):
        acc = s_ref[...].astype(F32)
        for j in range(N_CHIPS - 1):
            acc = acc + o_ref[j].astype(F32)
        t_ref[...] = acc

    return pl.pallas_call(
        body, name="add_chips", out_shape=jax.ShapeDtypeStruct((half, cols), F32),
        grid_spec=pltpu.PrefetchScalarGridSpec(
            num_scalar_prefetch=1, grid=(half // tr,),
            in_specs=[pl.BlockSpec((None, tr, cols), lambda i, pr: (pr[0], i, 0)),
                      pl.BlockSpec((N_CHIPS - 1, tr, cols), lambda i, pr: (0, i, 0))],
            out_specs=pl.BlockSpec((tr, cols), lambda i, pr: (i, 0))),
        compiler_params=_params(("parallel",)),
    )(place, sums, others)


def _reduce_scatter(grads, layout, place):
    received = _exchange_halves(grads)
    sums = [_add_halves(g, r, place) for g, r in zip(grads, received)]
    others = _chips_all_to_all(sums)
    halves = [_add_chips(s, o, place) for s, o in zip(sums, others)]
    return _join_halves(halves, layout)


def _rope_tables(n):
    t = jnp.arange(n)
    freqs = ROPE_THETA ** (-jnp.arange(ROPE_PAIRS, dtype=F32) / ROPE_PAIRS)
    ang_r = (t // GRID_W).astype(F32)[:, None] * freqs
    ang_c = (t % GRID_W).astype(F32)[:, None] * freqs
    cos = jnp.concatenate([jnp.cos(ang_r), jnp.cos(ang_r), jnp.cos(ang_c), jnp.cos(ang_c)], axis=-1)
    sin = jnp.concatenate([-jnp.sin(ang_r), jnp.sin(ang_r), -jnp.sin(ang_c), jnp.sin(ang_c)], axis=-1)
    return cos, sin


def _ffn_fwd(h2, w1, w2, tag):
    r, a = _matmul(h2, w1, kind='nn', b_split='n', out_dtype=BF16, epilogue='relu2', name=f"ffn_in_{tag}")
    f = _matmul(a, w2, kind='nn', b_split='k', out_dtype=F32, name=f"ffn_out_{tag}")
    return r, a, f


def _ffn_bwd(d_f, h2, r, a, w1, w2, tag):
    d_u = _matmul(d_f, w2, kind='nt', b_split='k', out_dtype=BF16, epilogue='times2x', extra=r, name=f"ffn_out_dx_{tag}")
    d_w2 = _matmul(a, d_f, kind='tn', out_split='k', out_dtype=BF16, name=f"ffn_out_dw_{tag}")
    d_w1 = _matmul(h2, d_u, kind='tn', out_split='n', out_dtype=BF16, name=f"ffn_in_dw_{tag}")
    d_h2 = _matmul(d_u, w1, kind='nt', b_split='n', out_dtype=F32, name=f"ffn_in_dx_{tag}")
    return d_h2, d_w1, d_w2


def _local_step(xl0, xc0, target, ml, mc0, sp, big):
    n_lat, n_ctx = xl0.shape[0], xc0.shape[0]
    one = lambda v: 1.0 + v
    g = [[sp['norm_g'][i, k][None, :] for k in range(4)] for i in range(2)]

    sh1, sc1, gt1, sh2, sc2, gt2 = ml[0]
    hl = _norm_fwd(xl0, g[0][0], one(sc1), b=sh1, out_dtype=BF16, name="l0_mod1")
    hc = _norm_fwd(xc0, g[0][0], one(mc0[1]), b=mc0[0], out_dtype=BF16, name="l0_mod1_ctx")
    proj_l = _matmul(hl, big['ar_in'], kind='nn', b_split='n', out_dtype=F32, name="ar_in_lat")
    proj_c = _matmul(hc, big['ar_in'], kind='nn', b_split='n', out_dtype=F32, name="ar_in_ctx")
    cos_l, sin_l = _rope_tables(n_lat)
    cos_c, sin_c = jnp.ones((n_ctx, HEAD_DIM), F32), jnp.zeros((n_ctx, HEAD_DIM), F32)
    q_g, k_g = sp['q_g'], sp['k_g']
    q_l, k_l, v_l = _qk_fwd(proj_l, q_g, k_g, cos_l, sin_l, name="qk_fwd_lat")
    _, k_c, v_c = _qk_fwd(proj_c, q_g, k_g, cos_c, sin_c, name="qk_fwd_ctx")
    k_all = jnp.concatenate([k_c, k_l], axis=0)
    v_all = jnp.concatenate([v_c, v_l], axis=0)
    attn, lse = _attn_fwd(q_l, k_all, v_all)
    conv_l = _conv_fwd(proj_l, sp['conv_w'], sp['conv_b'], name="conv_fwd_lat")
    conv_c = _conv_fwd(proj_c, sp['conv_w'], sp['conv_b'], name="conv_fwd_ctx")
    xs_f = jnp.concatenate([conv_c, conv_l], axis=0)
    xs_r = jnp.concatenate([conv_l, conv_c], axis=0)
    rnn_w = [(sp['wa'][d], sp['ba'][d][None, :], sp['wx'][d], sp['bx'][d][None, :], sp['lam'][d][None, :]) for d in range(2)]
    h_f, hp_f = _rglru_fwd(xs_f, *rnn_w[0], reverse=False, name="rglru_fwd_f")
    h_r, hp_r = _rglru_fwd(xs_r, *rnn_w[1], reverse=True, name="rglru_fwd_r")
    rnn = _rnn_out_fwd(h_f, h_r, proj_l, n_ctx, 0)
    cat = jnp.concatenate([attn, rnn], axis=1)
    ol0 = _matmul(cat, big['ar_out'], kind='nn', b_split='k', out_dtype=F32, name="ar_out")
    xm0 = _norm_fwd(ol0, g[0][1], gt1, res=xl0, out_dtype=F32, name="l0_res1")
    h2_0 = _norm_fwd(xm0, g[0][2], one(sc2), b=sh2, out_dtype=BF16, name="l0_mod2")
    r0, a0, f0 = _ffn_fwd(h2_0, big['ff_in'][0], big['ff_out'][0], "l0")
    xl1 = _norm_fwd(f0, g[0][3], gt2, res=xm0, out_dtype=F32, name="l0_res2")

    th1, tc1, tg1, th2, tc2, tg2 = ml[1]
    hl1 = _norm_fwd(xl1, g[1][0], one(tc1), b=th1, out_dtype=BF16, name="l1_mod1")
    z = _matmul(hl1, big['gm_in'], kind='nn', b_split='n', bias=sp['gm_b_in'], out_dtype=F32, name="gm_in")
    b_sp_t = sp['gm_b_sp'].T
    gated = _gmlp_fwd(z, sp['gm_v_g'], sp['gm_v_b'], sp['gm_w_sp'], b_sp_t)
    ol1 = _matmul(gated, big['gm_out'], kind='nn', b_split='k', out_dtype=F32, name="gm_out")
    xm1 = _norm_fwd(ol1, g[1][1], tg1, res=xl1, out_dtype=F32, name="l1_res1")
    h2_1 = _norm_fwd(xm1, g[1][2], one(tc2), b=th2, out_dtype=BF16, name="l1_mod2")
    r1, a1, f1 = _ffn_fwd(h2_1, big['ff_in'][1], big['ff_out'][1], "l1")
    y = _norm_fwd(f1, g[1][3], tg2, res=xm1, out_dtype=F32, name="l1_res2")

    dy, loss = _loss_head(y, target)

    d_f1, dg13, d_tg2, _ = _norm_bwd(dy, f1, g[1][3], tg2, out_dtype=BF16, name="l1_res2_bwd")
    d_h2, dw_ff_in1, dw_ff_out1 = _ffn_bwd(d_f1, h2_1, r1, a1, big['ff_in'][1], big['ff_out'][1], "l1")
    dxm1, dg12, d_tc2, d_th2 = _norm_bwd(d_h2, xm1, g[1][2], one(tc2), extra=dy, out_dtype=F32, name="l1_mod2_bwd")
    d_ol1, dg11, d_tg1, _ = _norm_bwd(dxm1, ol1, g[1][1], tg1, out_dtype=BF16, name="l1_res1_bwd")
    d_gated = _matmul(d_ol1, big['gm_out'], kind='nt', b_split='k', out_dtype=F32, name="gm_out_dx")
    dw_gm_out = _matmul(gated, d_ol1, kind='tn', out_split='k', out_dtype=BF16, name="gm_out_dw")
    d_z, d_gm_b_in, d_vg, d_vb, d_wsp, d_bsp_t = _gmlp_bwd(z, d_gated, sp['gm_v_g'], sp['gm_v_b'], sp['gm_w_sp'], b_sp_t)
    dw_gm_in = _matmul(hl1, d_z, kind='tn', out_split='n', out_dtype=BF16, name="gm_in_dw")
    d_hl1 = _matmul(d_z, big['gm_in'], kind='nt', b_split='n', out_dtype=F32, name="gm_in_dx")
    dxl1, dg10, d_tc1, d_th1 = _norm_bwd(d_hl1, xl1, g[1][0], one(tc1), extra=dxm1, out_dtype=F32, name="l1_mod1_bwd")

    d_f0, dg03, d_gt2, _ = _norm_bwd(dxl1, f0, g[0][3], gt2, out_dtype=BF16, name="l0_res2_bwd")
    d_h2, dw_ff_in0, dw_ff_out0 = _ffn_bwd(d_f0, h2_0, r0, a0, big['ff_in'][0], big['ff_out'][0], "l0")
    dxm0, dg02, d_sc2, d_sh2 = _norm_bwd(d_h2, xm0, g[0][2], one(sc2), extra=dxl1, out_dtype=F32, name="l0_mod2_bwd")
    d_ol0, dg01, d_gt1, _ = _norm_bwd(dxm0, ol0, g[0][1], gt1, out_dtype=BF16, name="l0_res1_bwd")
    d_cat = _matmul(d_ol0, big['ar_out'], kind='nt', b_split='k', out_dtype=F32, name="ar_out_dx")
    dw_ar_out = _matmul(cat, d_ol0, kind='tn', out_split='k', out_dtype=BF16, name="ar_out_dw")
    dq, dk_all, dv_all = _attn_bwd(q_l, k_all, v_all, attn, lse, d_cat)
    d_h, d_gate = _rnn_out_bwd(d_cat, h_f, h_r, proj_l, n_ctx, 0)
    zeros_c = jnp.zeros((n_ctx, D_RNN), F32)
    dxs_f, d_wa0, d_ba0, d_wx0, d_bx0, d_lam0 = _rglru_bwd(
        xs_f, hp_f, jnp.concatenate([zeros_c, d_h], axis=0), *rnn_w[0], reverse=False, name="rglru_bwd_f")
    dxs_r, d_wa1, d_ba1, d_wx1, d_bx1, d_lam1 = _rglru_bwd(
        xs_r, hp_r, jnp.concatenate([d_h, zeros_c], axis=0), *rnn_w[1], reverse=True, name="rglru_bwd_r")
    d_xr_l, d_cw_l, d_cb_l = _conv_bwd(dxs_f[n_ctx:], dxs_r[:n_lat], proj_l, sp['conv_w'], name="conv_bwd_lat")
    d_xr_c, d_cw_c, d_cb_c = _conv_bwd(dxs_f[:n_ctx], dxs_r[n_lat:], proj_c, sp['conv_w'], name="conv_bwd_ctx")
    dp_qk_l, d_qg, d_kg_l = _qk_bwd(dq, dk_all[n_ctx:], proj_l, q_g, k_g, cos_l, sin_l, name="qk_bwd_lat")
    dp_qk_c, _, d_kg_c = _qk_bwd(None, dk_all[:n_ctx], proj_c, q_g, k_g, cos_c, sin_c, name="qk_bwd_ctx")
    dv_b = dv_all.astype(BF16)
    d_proj_l = jnp.concatenate([dp_qk_l, dv_b[n_ctx:], d_xr_l, d_gate], axis=1)
    d_proj_c = jnp.concatenate([dp_qk_c, dv_b[:n_ctx], d_xr_c, jnp.zeros((n_ctx, D_RNN), BF16)], axis=1)
    dw_ar_in = _matmul(jnp.concatenate([hc, hl], axis=0), jnp.concatenate([d_proj_c, d_proj_l], axis=0), kind='tn',
                       out_split='n', out_dtype=BF16, name="ar_in_dw")
    d_hl = _matmul(d_proj_l, big['ar_in'], kind='nt', b_split='n', out_dtype=F32, name="ar_in_dx_lat")
    d_hc = _matmul(d_proj_c, big['ar_in'], kind='nt', b_split='n', out_dtype=F32, name="ar_in_dx_ctx")
    grad_x, dg00, d_sc1, d_sh1 = _norm_bwd(d_hl, xl0, g[0][0], one(sc1), extra=dxm0, out_dtype=F32, name="l0_mod1_bwd")
    _, dg00c, d_mc_scale, d_mc_shift = _norm_bwd(d_hc, xc0, g[0][0], one(mc0[1]), out_dtype=BF16, name="l0_mod1_ctx_bwd")

    zeros_d = jnp.zeros_like(d_sh1)
    small = {
        'd_ml0': jnp.concatenate([d_sh1, d_sc1, d_gt1, d_sh2, d_sc2, d_gt2], axis=1),
        'd_ml1': jnp.concatenate([d_th1, d_tc1, d_tg1, d_th2, d_tc2, d_tg2], axis=1),
        'd_mc0': jnp.concatenate([d_mc_shift, d_mc_scale] + [zeros_d] * 4, axis=1),
        'norm_g': jnp.stack([jnp.concatenate([dg00 + dg00c, dg01, dg02, dg03], axis=0),
                             jnp.concatenate([dg10, dg11, dg12, dg13], axis=0)]),
        'q_g': d_qg, 'k_g': d_kg_l + d_kg_c, 'conv_w': d_cw_l + d_cw_c, 'conv_b': d_cb_l + d_cb_c,
        'wa': jnp.stack([d_wa0, d_wa1]), 'ba': jnp.concatenate([d_ba0, d_ba1], axis=0),
        'wx': jnp.stack([d_wx0, d_wx1]), 'bx': jnp.concatenate([d_bx0, d_bx1], axis=0),
        'lam': jnp.concatenate([d_lam0, d_lam1], axis=0),
        'gm_b_in': d_gm_b_in, 'gm_v_g': d_vg, 'gm_v_b': d_vb, 'gm_w_sp': d_wsp, 'gm_b_sp': d_bsp_t.T,
        'loss': loss,
    }
    big_grads = {'ff_in': [dw_ff_in0, dw_ff_in1], 'ff_out': [dw_ff_out0, dw_ff_out1], 'ar_in': dw_ar_in,
                 'ar_out': dw_ar_out, 'gm_in': dw_gm_in, 'gm_out': dw_gm_out}
    return grad_x, big_grads, small


MOD_ROWS = 16
SMALL_ORDER = ('d_ml0', 'd_ml1', 'd_mc0', 'norm_g', 'q_g', 'k_g', 'conv_w', 'conv_b', 'wa', 'ba', 'wx', 'bx', 'lam',
               'gm_b_in', 'gm_v_g', 'gm_v_b', 'gm_w_sp', 'gm_b_sp', 'loss')


def _silu(v):
    return v * _sigmoid(v)


def _chip_concat(gathered, axis):
    return jnp.concatenate([gathered[2 * q] for q in range(N_CHIPS)], axis=axis)


def kernel(x, c, ctx, c_ctx, w_mod, b_mod, norm_g, w_ff_in, w_ff_out, ar_w_in, ar_q_g, ar_k_g, ar_conv_w, ar_conv_b, ar_wa, ar_ba, ar_wx, ar_bx, ar_lambda, ar_w_out, gm_w_in, gm_b_in, gm_v_g, gm_v_b, gm_w_sp, gm_b_sp, gm_w_out, loss_target, m_c_ctx, m_w_mod, m_b_mod, m_norm_g, m_w_ff_in, m_w_ff_out, m_ar_w_in, m_ar_q_g, m_ar_k_g, m_ar_conv_w, m_ar_conv_b, m_ar_wa, m_ar_ba, m_ar_wx, m_ar_bx, m_ar_lambda, m_ar_w_out, m_gm_w_in, m_gm_b_in, m_gm_v_g, m_gm_v_b, m_gm_w_sp, m_gm_b_sp, m_gm_w_out, v_c_ctx, v_w_mod, v_b_mod, v_norm_g, v_w_ff_in, v_w_ff_out, v_ar_w_in, v_ar_q_g, v_ar_k_g, v_ar_conv_w, v_ar_conv_b, v_ar_wa, v_ar_ba, v_ar_wx, v_ar_bx, v_ar_lambda, v_ar_w_out, v_gm_w_in, v_gm_b_in, v_gm_v_g, v_gm_v_b, v_gm_w_sp, v_gm_b_sp, v_gm_w_out):
    weights = dict(c_ctx=c_ctx, w_mod=w_mod, b_mod=b_mod, norm_g=norm_g, w_ff_in=w_ff_in, w_ff_out=w_ff_out, ar_w_in=ar_w_in,
                   ar_q_g=ar_q_g, ar_k_g=ar_k_g, ar_conv_w=ar_conv_w, ar_conv_b=ar_conv_b, ar_wa=ar_wa, ar_ba=ar_ba, ar_wx=ar_wx,
                   ar_bx=ar_bx, ar_lambda=ar_lambda, ar_w_out=ar_w_out, gm_w_in=gm_w_in, gm_b_in=gm_b_in, gm_v_g=gm_v_g,
                   gm_v_b=gm_v_b, gm_w_sp=gm_w_sp, gm_b_sp=gm_b_sp, gm_w_out=gm_w_out)
    m_in = dict(c_ctx=m_c_ctx, w_mod=m_w_mod, b_mod=m_b_mod, norm_g=m_norm_g, w_ff_in=m_w_ff_in, w_ff_out=m_w_ff_out,
                ar_w_in=m_ar_w_in, ar_q_g=m_ar_q_g, ar_k_g=m_ar_k_g, ar_conv_w=m_ar_conv_w, ar_conv_b=m_ar_conv_b, ar_wa=m_ar_wa,
                ar_ba=m_ar_ba, ar_wx=m_ar_wx, ar_bx=m_ar_bx, ar_lambda=m_ar_lambda, ar_w_out=m_ar_w_out, gm_w_in=m_gm_w_in,
                gm_b_in=m_gm_b_in, gm_v_g=m_gm_v_g, gm_v_b=m_gm_v_b, gm_w_sp=m_gm_w_sp, gm_b_sp=m_gm_b_sp, gm_w_out=m_gm_w_out)
    v_in = dict(c_ctx=v_c_ctx, w_mod=v_w_mod, b_mod=v_b_mod, norm_g=v_norm_g, w_ff_in=v_w_ff_in, w_ff_out=v_w_ff_out,
                ar_w_in=v_ar_w_in, ar_q_g=v_ar_q_g, ar_k_g=v_ar_k_g, ar_conv_w=v_ar_conv_w, ar_conv_b=v_ar_conv_b, ar_wa=v_ar_wa,
                ar_ba=v_ar_ba, ar_wx=v_ar_wx, ar_bx=v_ar_bx, ar_lambda=v_ar_lambda, ar_w_out=v_ar_w_out, gm_w_in=v_gm_w_in,
                gm_b_in=v_gm_b_in, gm_v_g=v_gm_v_g, gm_v_b=v_gm_v_b, gm_w_sp=v_gm_w_sp, gm_b_sp=v_gm_b_sp, gm_w_out=v_gm_w_out)

    xi, yi, ci = lax.axis_index("x"), lax.axis_index("y"), lax.axis_index("c")
    chip = 2 * xi + yi
    dev = 4 * xi + 2 * yi + ci
    place = jnp.stack([chip, ci]).astype(jnp.int32)
    n_lat, d = x.shape[1], x.shape[2]
    d6 = 6 * d
    cols_mod = w_mod.shape[2]

    mine = [c, norm_g, ar_conv_w[0], ar_ba[0], ar_bx[0], ar_lambda[0], gm_b_in, gm_v_g, gm_v_b]
    gathered = _allgather8([_pack(mine)], "gather_small_params")[0]
    parts = _unpack_devices(gathered, [a.shape for a in mine])
    c_all = parts[0].reshape(N_DEV, d)
    sp = {'norm_g': _chip_concat(parts[1], 2), 'q_g': ar_q_g, 'k_g': ar_k_g, 'conv_w': _chip_concat(parts[2], 1),
          'conv_b': ar_conv_b, 'wa': ar_wa[0], 'ba': _chip_concat(parts[3], 1), 'wx': ar_wx[0], 'bx': _chip_concat(parts[4], 1),
          'lam': _chip_concat(parts[5], 1), 'gm_b_in': _chip_concat(parts[6], 1), 'gm_v_g': _chip_concat(parts[7], 1),
          'gm_v_b': _chip_concat(parts[8], 1), 'gm_w_sp': gm_w_sp[0], 'gm_b_sp': gm_b_sp[0]}

    def mod_operand(c_rows, cc):
        row = lax.broadcasted_iota(jnp.int32, (MOD_ROWS - N_DEV, d), 0)
        lower = jnp.where(row == 0, jnp.broadcast_to(_silu(cc), (MOD_ROWS - N_DEV, d)), 0.0)
        sig = _sigmoid(cc)
        return jnp.concatenate([_silu(c_rows), lower], axis=0), sig * (1.0 + cc * (1.0 - sig))

    s_mod, dsilu_ctx = _small(mod_operand, [((MOD_ROWS, d), F32), ((1, d), F32)], c_all, c_ctx[None, :], name="mod_operand")
    b_mod_mine = lax.dynamic_slice(b_mod, (0, chip * cols_mod), (2, cols_mod))
    mod = [_matmul(s_mod, w_mod[i], kind='nn', bias=b_mod_mine[i][None, :], out_dtype=F32, name=f"mod_fwd_{i}") for i in range(2)]
    mod_all = _allgather8([jnp.concatenate(mod, axis=0)], "gather_mod")[0]
    mod_all = _chip_concat(mod_all, 1).reshape(2, MOD_ROWS, d6)
    ml = [jnp.split(lax.dynamic_slice(mod_all[i], (dev, 0), (1, d6)), 6, axis=1) for i in range(2)]
    mc0 = jnp.split(mod_all[0, N_DEV:N_DEV + 1], 6, axis=1)[:2]

    names = ('w_ff_in', 'w_ff_out', 'ar_w_in', 'ar_w_out', 'gm_w_in', 'gm_w_out')
    shards = [_cast_bf16(weights[n], f"cast_{n}") for n in names]
    shards = [s if s.shape[0] > 1 else s[0] for s in shards]
    full = _gather_weights(shards)
    big = {'ff_in': full[0:2], 'ff_out': full[2:4], 'ar_in': full[4], 'ar_out': full[5], 'gm_in': full[6], 'gm_out': full[7]}

    grad_x, bg, small = _local_step(x[0], ctx[0], loss_target[0], ml, mc0, sp, big)

    grads = bg['ff_in'] + bg['ff_out'] + [bg['ar_in'], bg['ar_out'], bg['gm_in'], bg['gm_out']]
    layout = [(0, 0), (0, 1), (1, 0), (1, 1), (2, 0), (3, 0), (4, 0), (5, 0)]
    reduced = dict(zip(names, _reduce_scatter(grads, layout, place)))

    small_list = [small[k] for k in SMALL_ORDER]
    small8 = _allgather8([_pack(small_list)], "gather_small_grads")[0]
    total = _unpack(_sum_devices(small8).reshape(-1), [a.shape for a in small_list])
    total = dict(zip(SMALL_ORDER, total))
    per_dev = _unpack_devices(small8, [(d6,), (d6,)])
    pad_rows = jnp.zeros((MOD_ROWS - N_DEV - 1, d6), F32)
    d_mod = [jnp.concatenate([per_dev[0], total['d_mc0'], pad_rows], axis=0),
             jnp.concatenate([per_dev[1], jnp.zeros((MOD_ROWS - N_DEV, d6), F32)], axis=0)]
    d_mod_mine = [lax.dynamic_slice(dm, (0, chip * cols_mod), (MOD_ROWS, cols_mod)) for dm in d_mod]
    g_w_mod = jnp.stack([_matmul(s_mod, d_mod_mine[i], kind='tn', out_dtype=F32, name=f"mod_dw_{i}") for i in range(2)])
    d_s_part = _matmul(d_mod_mine[0], w_mod[0], kind='nt', out_dtype=F32, name="mod_ds")
    d_s_all = _allgather8([d_s_part[N_DEV:]], "gather_mod_ds")[0]

    def c_ctx_grad(parts_, dsilu):
        acc = parts_[0, 0:1]
        for q in range(1, N_CHIPS):
            acc = acc + parts_[2 * q, 0:1]
        return (acc * dsilu,)

    g_c_ctx = _small(c_ctx_grad, [((1, d), F32)], d_s_all, dsilu_ctx, name="c_ctx_grad")[0].reshape(d)

    def mine_of(full_grad, axis, n_shard):
        return lax.dynamic_slice_in_dim(full_grad, chip * n_shard, n_shard, axis=axis)

    grads_out = {
        'c_ctx': g_c_ctx, 'w_mod': g_w_mod,
        'b_mod': jnp.stack([total['d_ml0'][0] + total['d_mc0'][0], total['d_ml1'][0]]),
        'norm_g': mine_of(total['norm_g'], 2, norm_g.shape[2]),
        'w_ff_in': reduced['w_ff_in'], 'w_ff_out': reduced['w_ff_out'], 'ar_w_in': reduced['ar_w_in'],
        'ar_q_g': total['q_g'], 'ar_k_g': total['k_g'], 'ar_conv_w': mine_of(total['conv_w'], 1, ar_conv_w.shape[2])[None],
        'ar_conv_b': total['conv_b'], 'ar_wa': total['wa'][None], 'ar_ba': mine_of(total['ba'], 1, ar_ba.shape[2])[None],
        'ar_wx': total['wx'][None], 'ar_bx': mine_of(total['bx'], 1, ar_bx.shape[2])[None],
        'ar_lambda': mine_of(total['lam'], 1, ar_lambda.shape[2])[None], 'ar_w_out': reduced['ar_w_out'],
        'gm_w_in': reduced['gm_w_in'], 'gm_b_in': mine_of(total['gm_b_in'], 1, gm_b_in.shape[1]),
        'gm_v_g': mine_of(total['gm_v_g'], 1, gm_v_g.shape[1]), 'gm_v_b': mine_of(total['gm_v_b'], 1, gm_v_b.shape[1]),
        'gm_w_sp': total['gm_w_sp'][None], 'gm_b_sp': total['gm_b_sp'][None], 'gm_w_out': reduced['gm_w_out'],
    }
    order = list(weights)
    stepped = [_adamw(weights[n], grads_out[n].reshape(weights[n].shape), m_in[n], v_in[n], f"adamw_{n}") for n in order]
    loss = total['loss'].reshape(())
    return (loss, grad_x[None], *[s[0] for s in stepped], *[s[1] for s in stepped], *[s[2] for s in stepped],
            *[s[3] for s in stepped])
```

```python
import functools
import math

import jax
import jax.numpy as jnp
from jax import lax
from jax.experimental import pallas as pl
from jax.experimental.pallas import tpu as pltpu

F32 = jnp.float32
BF16 = jnp.bfloat16
MESH = pl.DeviceIdType.MESH
ANY = pl.BlockSpec(memory_space=pl.ANY)

VMEM_LIMIT_BYTES = 52 * 1024 * 1024
LANES = 128
N_CHIPS = 4
N_DEV = 8

HEAD_DIM = 128
N_HEADS = 8
N_KV = 2
GROUP = N_HEADS // N_KV
ATTN_W = N_HEADS * HEAD_DIM
KV_W = N_KV * HEAD_DIM
D_RNN = 1024
RNN_BLOCKS = 8
RNN_BW = D_RNN // RNN_BLOCKS
CONV_W = 4
RG_C = 8.0
GRID_W = 64
ROPE_THETA = 10000.0
ROPE_PAIRS = HEAD_DIM // 4
GM_GROUPS = 16
CHUNK = 128
EPS = 1e-6
ADAM_LR, ADAM_B1, ADAM_B2, ADAM_EPS, ADAM_WD, ADAM_STEP = 0.001, 0.9, 0.999, 1e-08, 0.01, 10
GELU_C = math.sqrt(2.0 / math.pi)


def _params(sem=None):
    return pltpu.CompilerParams(dimension_semantics=sem, vmem_limit_bytes=VMEM_LIMIT_BYTES)


def _tile(dim, pref, unit):
    best = None
    t = unit
    while t <= min(dim, pref):
        if dim % t == 0:
            best = t
        t += unit
    return best if best is not None else dim


def _full(shape):
    nd = len(shape)
    return pl.BlockSpec(shape, lambda *_: (0,) * nd)


def _blocked_map(split, per_q):
    if split == 'n':
        return lambda r, c: (c // per_q, r, c % per_q)
    if split == 'k':
        return lambda r, c: (r // per_q, r % per_q, c)
    return lambda r, c: (r, c)


def _logical_shape(arr, split):
    if split == 'n':
        return arr.shape[1], arr.shape[0] * arr.shape[2]
    if split == 'k':
        return arr.shape[0] * arr.shape[1], arr.shape[2]
    return arr.shape


def _matmul(a, b, *, kind, name, out_dtype, b_split=None, out_split=None, bias=None, epilogue=None, extra=None,
            pref=(1024, 1024, 1024)):
    b_rows, b_cols = _logical_shape(b, b_split)
    if kind == 'nn':
        m, kc = a.shape
        n = b_cols
        assert b_rows == kc
    elif kind == 'nt':
        m, kc = a.shape
        n = b_rows
        assert b_cols == kc
    else:
        kc, m = a.shape
        n = b_cols
        assert b_rows == kc
    b_row_ext = b.shape[1] if b_split == 'k' else b_rows
    b_col_ext = b.shape[2] if b_split == 'n' else b_cols
    out_row_ext = m // N_CHIPS if out_split == 'k' else m
    out_col_ext = n // N_CHIPS if out_split == 'n' else n
    if kind == 'nn':
        ti = _tile(min(m, out_row_ext), pref[0], 16)
        tj = _tile(math.gcd(b_col_ext, out_col_ext), pref[1], LANES)
        tl = _tile(b_row_ext, pref[2], LANES)
        a_spec = pl.BlockSpec((ti, tl), lambda i, j, l: (i, l))
        b_tile, b_rc = (tl, tj), (lambda i, j, l: (l, j))
        dims = (((1,), (0,)), ((), ()))
    elif kind == 'nt':
        ti = _tile(min(m, out_row_ext), pref[0], 16)
        tj = _tile(math.gcd(b_row_ext, out_col_ext), pref[1], LANES)
        tl = _tile(b_col_ext, pref[2], LANES)
        a_spec = pl.BlockSpec((ti, tl), lambda i, j, l: (i, l))
        b_tile, b_rc = (tj, tl), (lambda i, j, l: (j, l))
        dims = (((1,), (1,)), ((), ()))
    else:
        ti = _tile(out_row_ext, pref[0], LANES)
        tj = _tile(math.gcd(b_col_ext, out_col_ext), pref[1], LANES)
        tl = _tile(b_row_ext, pref[2], 16)
        a_spec = pl.BlockSpec((tl, ti), lambda i, j, l: (l, i))
        b_tile, b_rc = (tl, tj), (lambda i, j, l: (l, j))
        dims = (((0,), (0,)), ((), ()))
    grid = (m // ti, n // tj, kc // tl)
    n_l = grid[2]

    if b_split is None:
        b_spec = pl.BlockSpec(b_tile, b_rc)
    else:
        per_q = (b.shape[2] // b_tile[1]) if b_split == 'n' else (b.shape[1] // b_tile[0])
        bmap = _blocked_map(b_split, per_q)
        b_spec = pl.BlockSpec((None,) + b_tile, lambda i, j, l: bmap(*b_rc(i, j, l)))
    if out_split is None:
        out_shape2 = (m, n)
        o_spec = pl.BlockSpec((ti, tj), lambda i, j, l: (i, j))
    else:
        out_shape2 = (N_CHIPS, m // N_CHIPS, n) if out_split == 'k' else (N_CHIPS, m, n // N_CHIPS)
        per_q = (out_shape2[2] // tj) if out_split == 'n' else (out_shape2[1] // ti)
        omap = _blocked_map(out_split, per_q)
        o_spec = pl.BlockSpec((None, ti, tj), lambda i, j, l: omap(i, j))

    in_specs = [a_spec, b_spec]
    operands = [a, b]
    if bias is not None:
        in_specs.append(pl.BlockSpec((1, tj), lambda i, j, l: (0, j)))
        operands.append(bias)
    if extra is not None:
        in_specs.append(pl.BlockSpec((ti, tj), lambda i, j, l: (i, j)))
        operands.append(extra)
    if epilogue == 'relu2':
        out_shape = (jax.ShapeDtypeStruct(out_shape2, out_dtype), jax.ShapeDtypeStruct(out_shape2, out_dtype))
        out_specs = (o_spec, o_spec)
    else:
        out_shape = jax.ShapeDtypeStruct(out_shape2, out_dtype)
        out_specs = o_spec
    has_bias, has_extra = bias is not None, extra is not None

    def body(*refs):
        a_ref, b_ref = refs[0], refs[1]
        pos = 2
        bias_ref = extra_ref = None
        if has_bias:
            bias_ref = refs[pos]
            pos += 1
        if has_extra:
            extra_ref = refs[pos]
            pos += 1
        outs = refs[pos:-1]
        acc_ref = refs[-1]
        step = pl.program_id(2)

        @pl.when(step == 0)
        def _():
            acc_ref[...] = jnp.zeros_like(acc_ref)

        acc_ref[...] += lax.dot_general(a_ref[...].astype(BF16), b_ref[...].astype(BF16), dims,
                                        preferred_element_type=F32)

        @pl.when(step == n_l - 1)
        def _():
            acc = acc_ref[...]
            if has_bias:
                acc = acc + bias_ref[...]
            if epilogue == 'relu2':
                r = jnp.maximum(acc, 0.0)
                outs[0][...] = r.astype(outs[0].dtype)
                outs[1][...] = (r * r).astype(outs[1].dtype)
            elif epilogue == 'times2x':
                outs[0][...] = (acc * (2.0 * extra_ref[...].astype(F32))).astype(outs[0].dtype)
            else:
                outs[0][...] = acc.astype(outs[0].dtype)

    return pl.pallas_call(
        body, name=name, grid=grid, in_specs=in_specs, out_specs=out_specs, out_shape=out_shape,
        scratch_shapes=[pltpu.VMEM((ti, tj), F32)],
        compiler_params=_params(("parallel", "parallel", "arbitrary")),
    )(*operands)


def _small(fn, out_shapes, *arrays, name):
    n_in = len(arrays)

    def body(*refs):
        res = fn(*[r[...] for r in refs[:n_in]])
        for o_ref, v in zip(refs[n_in:], res):
            o_ref[...] = v.astype(o_ref.dtype)

    return pl.pallas_call(
        body, name=name, out_shape=tuple(jax.ShapeDtypeStruct(s, d) for s, d in out_shapes),
        in_specs=[_full(a.shape) for a in arrays], out_specs=tuple(_full(s) for s, _ in out_shapes), grid=(1,),
        compiler_params=_params(("arbitrary",)),
    )(*arrays)


def _rows_tile(rows, cols, itemsize=4, budget=2 * 1024 * 1024):
    return _tile(rows, max(16, budget // (cols * itemsize)), 16)


def _rowwise(fn, out_dtypes, *arrays, name):
    rows, cols = arrays[0].shape
    tr = _rows_tile(rows, cols)
    n_in = len(arrays)

    def body(*refs):
        res = fn(*[r[...] for r in refs[:n_in]])
        for o_ref, v in zip(refs[n_in:], res):
            o_ref[...] = v.astype(o_ref.dtype)

    spec = pl.BlockSpec((tr, cols), lambda i: (i, 0))
    return pl.pallas_call(
        body, name=name, grid=(rows // tr,), in_specs=[spec] * n_in, out_specs=tuple(spec for _ in out_dtypes),
        out_shape=tuple(jax.ShapeDtypeStruct((rows, cols), d) for d in out_dtypes),
        compiler_params=_params(("parallel",)),
    )(*arrays)


def _as2d(a):
    return a.reshape(1, a.size) if a.ndim < 2 else a.reshape(-1, a.shape[-1])


def _cast_shard(w, place, layer, name):
    _, rows, cols = w.shape
    tr = _rows_tile(rows, cols)

    def body(place_ref, w_ref, o_ref):
        o_ref[...] = w_ref[...].astype(o_ref.dtype)

    return pl.pallas_call(
        body, name=name, out_shape=jax.ShapeDtypeStruct((N_CHIPS, rows, cols), BF16),
        grid_spec=pltpu.PrefetchScalarGridSpec(
            num_scalar_prefetch=1, grid=(rows // tr,),
            in_specs=[pl.BlockSpec((None, tr, cols), lambda i, pr: (layer, i, 0))],
            out_specs=pl.BlockSpec((None, tr, cols), lambda i, pr: (pr[0], i, 0))),
        compiler_params=_params(("parallel",)),
    )(place, w)


def _norm_fwd(x, g, a, b=None, res=None, *, out_dtype, name):
    rows, d = x.shape
    tr = _rows_tile(rows, d)
    has_b, has_res = b is not None, res is not None

    def body(*refs):
        x_ref, g_ref, a_ref = refs[:3]
        pos = 3
        xv = x_ref[...]
        rstd = lax.rsqrt(jnp.mean(xv * xv, axis=-1, keepdims=True) + EPS)
        y = (xv * rstd * g_ref[...]) * a_ref[...]
        if has_b:
            y = y + refs[pos][...]
            pos += 1
        if has_res:
            y = y + refs[pos][...]
            pos += 1
        refs[pos][...] = y.astype(refs[pos].dtype)

    row = pl.BlockSpec((tr, d), lambda i: (i, 0))
    vec = pl.BlockSpec((1, d), lambda i: (0, 0))
    operands, specs = [x, g, a], [row, vec, vec]
    if has_b:
        operands.append(b)
        specs.append(vec)
    if has_res:
        operands.append(res)
        specs.append(row)
    return pl.pallas_call(
        body, name=name, grid=(rows // tr,), in_specs=specs, out_specs=row,
        out_shape=jax.ShapeDtypeStruct((rows, d), out_dtype), compiler_params=_params(("parallel",)),
    )(*operands)


def _norm_bwd(dy, x, g, a, extra=None, *, out_dtype, name):
    rows, d = x.shape
    tr = _rows_tile(rows, d)
    has_extra = extra is not None

    def body(*refs):
        dy_ref, x_ref, g_ref, a_ref = refs[:4]
        pos = 4
        extra_ref = None
        if has_extra:
            extra_ref = refs[pos]
            pos += 1
        dx_ref, dg_ref, da_ref, db_ref = refs[pos:pos + 4]

        @pl.when(pl.program_id(0) == 0)
        def _():
            dg_ref[...] = jnp.zeros_like(dg_ref)
            da_ref[...] = jnp.zeros_like(da_ref)
            db_ref[...] = jnp.zeros_like(db_ref)

        xv = x_ref[...]
        dyv = dy_ref[...].astype(F32)
        rstd = lax.rsqrt(jnp.mean(xv * xv, axis=-1, keepdims=True) + EPS)
        nrm = xv * rstd
        gv = g_ref[...]
        da_ref[...] += jnp.sum(dyv * (nrm * gv), axis=0, keepdims=True)
        db_ref[...] += jnp.sum(dyv, axis=0, keepdims=True)
        dt = dyv * a_ref[...]
        dg_ref[...] += jnp.sum(dt * nrm, axis=0, keepdims=True)
        dn = dt * gv
        dx = rstd * (dn - nrm * jnp.mean(dn * nrm, axis=-1, keepdims=True))
        if has_extra:
            dx = dx + extra_ref[...]
        dx_ref[...] = dx.astype(dx_ref.dtype)

    row = pl.BlockSpec((tr, d), lambda i: (i, 0))
    vec = pl.BlockSpec((1, d), lambda i: (0, 0))
    operands, specs = [dy, x, g, a], [row, row, vec, vec]
    if has_extra:
        operands.append(extra)
        specs.append(row)
    vshape = jax.ShapeDtypeStruct((1, d), F32)
    return pl.pallas_call(
        body, name=name, grid=(rows // tr,), in_specs=specs, out_specs=(row, vec, vec, vec),
        out_shape=(jax.ShapeDtypeStruct((rows, d), out_dtype), vshape, vshape, vshape),
        compiler_params=_params(("arbitrary",)),
    )(*operands)


def _loss_head(y, target):
    rows, d = y.shape
    tr = _rows_tile(rows, d)

    def body(y_ref, t_ref, dy_ref, loss_ref):
        @pl.when(pl.program_id(0) == 0)
        def _():
            loss_ref[...] = jnp.zeros_like(loss_ref)

        err = y_ref[...] - t_ref[...]
        dy_ref[...] = err * (1.0 / d)
        loss_ref[...] += jnp.sum(jnp.sum(err * err, axis=-1, keepdims=True), axis=0, keepdims=True) * (0.5 / d)

    row = pl.BlockSpec((tr, d), lambda i: (i, 0))
    return pl.pallas_call(
        body, name="loss_head", grid=(rows // tr,), in_specs=[row, row], out_specs=(row, _full((1, 1))),
        out_shape=(jax.ShapeDtypeStruct((rows, d), F32), jax.ShapeDtypeStruct((1, 1), F32)),
        compiler_params=_params(("arbitrary",)),
    )(y, target)


def _rope_partner(v):
    lane = lax.broadcasted_iota(jnp.int32, v.shape, 1)
    up = pltpu.roll(v, HEAD_DIM - ROPE_PAIRS, 1)
    down = pltpu.roll(v, ROPE_PAIRS, 1)
    return jnp.where((lane % (2 * ROPE_PAIRS)) < ROPE_PAIRS, up, down)


def _qk_fwd(proj, q_g, k_g, cos, sin, *, name):
    rows = proj.shape[0]
    tr = _tile(rows, 256, 16)
    width = ATTN_W + 2 * KV_W

    def body(p_ref, qg_ref, kg_ref, cos_ref, sin_ref, q_ref, k_ref, v_ref):
        cosv, sinv = cos_ref[...], sin_ref[...]
        for h in range(N_HEADS + N_KV):
            xv = p_ref[:, h * HEAD_DIM:(h + 1) * HEAD_DIM]
            gain = qg_ref[...] if h < N_HEADS else kg_ref[...]
            t = xv * lax.rsqrt(jnp.mean(xv * xv, axis=-1, keepdims=True) + EPS) * gain
            y = t * cosv + _rope_partner(t) * sinv
            if h < N_HEADS:
                q_ref[:, h * HEAD_DIM:(h + 1) * HEAD_DIM] = y.astype(BF16)
            else:
                k_ref[:, (h - N_HEADS) * HEAD_DIM:(h - N_HEADS + 1) * HEAD_DIM] = y.astype(BF16)
        v_ref[...] = p_ref[:, ATTN_W + KV_W:width].astype(BF16)

    vec = _full((1, HEAD_DIM))
    tab = pl.BlockSpec((tr, HEAD_DIM), lambda i: (i, 0))
    return pl.pallas_call(
        body, name=name, grid=(rows // tr,),
        in_specs=[pl.BlockSpec((tr, width), lambda i: (i, 0)), vec, vec, tab, tab],
        out_specs=(pl.BlockSpec((tr, ATTN_W), lambda i: (i, 0)), pl.BlockSpec((tr, KV_W), lambda i: (i, 0)),
                   pl.BlockSpec((tr, KV_W), lambda i: (i, 0))),
        out_shape=(jax.ShapeDtypeStruct((rows, ATTN_W), BF16), jax.ShapeDtypeStruct((rows, KV_W), BF16),
                   jax.ShapeDtypeStruct((rows, KV_W), BF16)),
        compiler_params=_params(("parallel",)),
    )(proj, q_g, k_g, cos, sin)


def _qk_bwd(dq, dk, proj, q_g, k_g, cos, sin, *, name):
    rows = proj.shape[0]
    tr = _tile(rows, 256, 16)
    width = ATTN_W + KV_W
    has_q = dq is not None

    def body(*refs):
        pos = 0
        dq_ref = None
        if has_q:
            dq_ref = refs[0]
            pos = 1
        dk_ref, p_ref, qg_ref, kg_ref, cos_ref, sin_ref, dp_ref, dqg_ref, dkg_ref = refs[pos:pos + 9]

        @pl.when(pl.program_id(0) == 0)
        def _():
            dqg_ref[...] = jnp.zeros_like(dqg_ref)
            dkg_ref[...] = jnp.zeros_like(dkg_ref)

        cosv, sinv = cos_ref[...], sin_ref[...]
        for h in range(N_HEADS + N_KV):
            cols = slice(h * HEAD_DIM, (h + 1) * HEAD_DIM)
            if h < N_HEADS and not has_q:
                dp_ref[:, cols] = jnp.zeros((tr, HEAD_DIM), dp_ref.dtype)
                continue
            if h < N_HEADS:
                dyv, gain, dgain_ref = dq_ref[:, cols], qg_ref[...], dqg_ref
            else:
                hk = h - N_HEADS
                dyv, gain, dgain_ref = dk_ref[:, hk * HEAD_DIM:(hk + 1) * HEAD_DIM], kg_ref[...], dkg_ref
            dyv = dyv.astype(F32)
            dt = dyv * cosv + _rope_partner(dyv * sinv)
            xv = p_ref[:, cols]
            rstd = lax.rsqrt(jnp.mean(xv * xv, axis=-1, keepdims=True) + EPS)
            nrm = xv * rstd
            dgain_ref[...] += jnp.sum(dt * nrm, axis=0, keepdims=True)
            dn = dt * gain
            dp_ref[:, cols] = (rstd * (dn - nrm * jnp.mean(dn * nrm, axis=-1, keepdims=True))).astype(dp_ref.dtype)

    vec = _full((1, HEAD_DIM))
    tab = pl.BlockSpec((tr, HEAD_DIM), lambda i: (i, 0))
    operands = ([dq] if has_q else []) + [dk, proj, q_g, k_g, cos, sin]
    specs = ([pl.BlockSpec((tr, ATTN_W), lambda i: (i, 0))] if has_q else []) + [
        pl.BlockSpec((tr, KV_W), lambda i: (i, 0)), pl.BlockSpec((tr, width), lambda i: (i, 0)), vec, vec, tab, tab]
    return pl.pallas_call(
        body, name=name, grid=(rows // tr,), in_specs=specs,
        out_specs=(pl.BlockSpec((tr, width), lambda i: (i, 0)), vec, vec),
        out_shape=(jax.ShapeDtypeStruct((rows, width), BF16), jax.ShapeDtypeStruct((1, HEAD_DIM), F32),
                   jax.ShapeDtypeStruct((1, HEAD_DIM), F32)),
        compiler_params=_params(("arbitrary",)),
    )(*operands)


def _attn_fwd(q, k, v):
    n_q, n_k = q.shape[0], k.shape[0]
    tq = _tile(n_q, 256, 16)
    gw = GROUP * HEAD_DIM
    scale = HEAD_DIM ** -0.5

    def body(q_ref, k_ref, v_ref, o_ref, lse_ref):
        kv, vv = k_ref[...], v_ref[...]
        for g in range(GROUP):
            cols = slice(g * HEAD_DIM, (g + 1) * HEAD_DIM)
            s = lax.dot_general(q_ref[:, cols], kv, (((1,), (1,)), ((), ())), preferred_element_type=F32) * scale
            m = jnp.max(s, axis=-1, keepdims=True)
            p = jnp.exp(s - m)
            l = jnp.sum(p, axis=-1, keepdims=True)
            o = jnp.dot(p.astype(BF16), vv, preferred_element_type=F32) / l
            o_ref[:, cols] = o.astype(o_ref.dtype)
            lse_ref[:, g:g + 1] = m + jnp.log(l)

    return pl.pallas_call(
        body, name="attn_fwd", grid=(N_KV, n_q // tq),
        in_specs=[pl.BlockSpec((tq, gw), lambda h, i: (i, h)), pl.BlockSpec((n_k, HEAD_DIM), lambda h, i: (0, h)),
                  pl.BlockSpec((n_k, HEAD_DIM), lambda h, i: (0, h))],
        out_specs=(pl.BlockSpec((tq, gw), lambda h, i: (i, h)), pl.BlockSpec((None, tq, GROUP), lambda h, i: (h, i, 0))),
        out_shape=(jax.ShapeDtypeStruct((n_q, ATTN_W), BF16), jax.ShapeDtypeStruct((N_KV, n_q, GROUP), F32)),
        compiler_params=_params(("parallel", "parallel")),
    )(q, k, v)


def _attn_bwd(q, k, v, o, lse, do):
    n_q, n_k = q.shape[0], k.shape[0]
    tq = _tile(n_q, 256, 16)
    gw = GROUP * HEAD_DIM
    scale = HEAD_DIM ** -0.5

    def body(q_ref, k_ref, v_ref, o_ref, lse_ref, do_ref, dq_ref, dk_ref, dv_ref):
        @pl.when(pl.program_id(1) == 0)
        def _():
            dk_ref[...] = jnp.zeros_like(dk_ref)
            dv_ref[...] = jnp.zeros_like(dv_ref)

        kv, vv = k_ref[...], v_ref[...]
        for g in range(GROUP):
            cols = slice(g * HEAD_DIM, (g + 1) * HEAD_DIM)
            qg = q_ref[:, cols]
            dof = do_ref[:, cols].astype(F32)
            dog = dof.astype(BF16)
            s = lax.dot_general(qg, kv, (((1,), (1,)), ((), ())), preferred_element_type=F32) * scale
            p = jnp.exp(s - lse_ref[:, g:g + 1])
            delta = jnp.sum(dof * o_ref[:, cols].astype(F32), axis=-1, keepdims=True)
            dp = lax.dot_general(dog, vv, (((1,), (1,)), ((), ())), preferred_element_type=F32)
            ds = (p * (dp - delta) * scale).astype(BF16)
            pb = p.astype(BF16)
            dq_ref[:, cols] = jnp.dot(ds, kv, preferred_element_type=F32)
            dk_ref[...] += lax.dot_general(ds, qg, (((0,), (0,)), ((), ())), preferred_element_type=F32)
            dv_ref[...] += lax.dot_general(pb, dog, (((0,), (0,)), ((), ())), preferred_element_type=F32)

    qspec = pl.BlockSpec((tq, gw), lambda h, i: (i, h))
    kspec = pl.BlockSpec((n_k, HEAD_DIM), lambda h, i: (0, h))
    return pl.pallas_call(
        body, name="attn_bwd", grid=(N_KV, n_q // tq),
        in_specs=[qspec, kspec, kspec, qspec, pl.BlockSpec((None, tq, GROUP), lambda h, i: (h, i, 0)), qspec],
        out_specs=(qspec, kspec, kspec),
        out_shape=(jax.ShapeDtypeStruct((n_q, ATTN_W), F32), jax.ShapeDtypeStruct((n_k, KV_W), F32),
                   jax.ShapeDtypeStruct((n_k, KV_W), F32)),
        compiler_params=_params(("parallel", "arbitrary")),
    )(q, k, v, o, lse, do)


CONV_COLS = 256
XR_COL0 = ATTN_W + 2 * KV_W


def _shift_rows(v, off):
    if off == 0:
        return v
    n = v.shape[0]
    rolled = pltpu.roll(v, (-off) % n, 0)
    t = lax.broadcasted_iota(jnp.int32, v.shape, 0)
    keep = (t + off >= 0) & (t + off < n)
    return jnp.where(keep, rolled, 0.0)


def _conv_fwd(proj, w, b, *, name):
    rows = proj.shape[0]
    blk0 = XR_COL0 // CONV_COLS

    def body(x_ref, w_ref, b_ref, y_ref):
        xv = x_ref[...]
        y = b_ref[...] + jnp.zeros_like(xv)
        for j in range(CONV_W):
            y = y + _shift_rows(xv, j - CONV_W // 2) * w_ref[j:j + 1, :]
        y_ref[...] = y

    return pl.pallas_call(
        body, name=name, grid=(D_RNN // CONV_COLS,),
        in_specs=[pl.BlockSpec((rows, CONV_COLS), lambda i: (0, blk0 + i)),
                  pl.BlockSpec((CONV_W, CONV_COLS), lambda i: (0, i)), pl.BlockSpec((1, CONV_COLS), lambda i: (0, i))],
        out_specs=pl.BlockSpec((rows, CONV_COLS), lambda i: (0, i)),
        out_shape=jax.ShapeDtypeStruct((rows, D_RNN), F32), compiler_params=_params(("parallel",)),
    )(proj, w, b)


def _conv_bwd(d1, d2, proj, w, *, name):
    rows = proj.shape[0]
    blk0 = XR_COL0 // CONV_COLS

    def body(d1_ref, d2_ref, x_ref, w_ref, dx_ref, dw_ref, db_ref):
        dv = d1_ref[...] + d2_ref[...]
        xv = x_ref[...]
        dx = jnp.zeros_like(dv)
        for j in range(CONV_W):
            off = j - CONV_W // 2
            dx = dx + _shift_rows(dv, -off) * w_ref[j:j + 1, :]
            dw_ref[j:j + 1, :] = jnp.sum(dv * _shift_rows(xv, off), axis=0, keepdims=True)
        dx_ref[...] = dx.astype(dx_ref.dtype)
        db_ref[...] = jnp.sum(dv, axis=0, keepdims=True)

    col = pl.BlockSpec((rows, CONV_COLS), lambda i: (0, i))
    return pl.pallas_call(
        body, name=name, grid=(D_RNN // CONV_COLS,),
        in_specs=[col, col, pl.BlockSpec((rows, CONV_COLS), lambda i: (0, blk0 + i)),
                  pl.BlockSpec((CONV_W, CONV_COLS), lambda i: (0, i))],
        out_specs=(col, pl.BlockSpec((CONV_W, CONV_COLS), lambda i: (0, i)), pl.BlockSpec((1, CONV_COLS), lambda i: (0, i))),
        out_shape=(jax.ShapeDtypeStruct((rows, D_RNN), BF16), jax.ShapeDtypeStruct((CONV_W, D_RNN), F32),
                   jax.ShapeDtypeStruct((1, D_RNN), F32)),
        compiler_params=_params(("parallel",)),
    )(d1, d2, proj, w)


RNN_TB = 256
SCAN_ROWS = 8


def _sigmoid(z):
    return 1.0 / (1.0 + jnp.exp(-z))


def _softplus(z):
    return jnp.maximum(z, 0.0) + jnp.log(1.0 + jnp.exp(-jnp.abs(z)))


def _one_minus_exp(y):
    series = -y * (1.0 + y * (0.5 + y * (1.0 / 6.0 + y * (1.0 / 24.0))))
    return jnp.where(y > -0.03, series, 1.0 - jnp.exp(y))


def _rglru_gates(xv, wa_ref, ba_ref, wx_ref, bx_ref, lam_ref):
    xb = xv.astype(BF16)
    zr = jnp.concatenate([jnp.dot(xb[:, n * RNN_BW:(n + 1) * RNN_BW], wa_ref[n].astype(BF16),
                                  preferred_element_type=F32) for n in range(RNN_BLOCKS)], axis=-1) + ba_ref[...]
    zi = jnp.concatenate([jnp.dot(xb[:, n * RNN_BW:(n + 1) * RNN_BW], wx_ref[n].astype(BF16),
                                  preferred_element_type=F32) for n in range(RNN_BLOCKS)], axis=-1) + bx_ref[...]
    r = _sigmoid(zr)
    gi = _sigmoid(zi)
    sp = _softplus(-lam_ref[...])
    log_a = -RG_C * r * sp
    a = jnp.exp(log_a)
    s = jnp.sqrt(_one_minus_exp(2.0 * log_a))
    return r, gi, sp, a, s


def _scan_rows(n_rows, reverse, step_fn, carry):
    groups = n_rows // SCAN_ROWS

    def trip(gidx, carry):
        gi = (groups - 1 - gidx) if reverse else gidx
        base = pl.multiple_of(gi * SCAN_ROWS, SCAN_ROWS)
        return step_fn(base, carry)

    return lax.fori_loop(0, groups, trip, carry)


def _rglru_fwd(xs, wa, ba, wx, bx, lam, *, reverse, name):
    rows = xs.shape[0]
    tb = _tile(rows, RNN_TB, SCAN_ROWS)
    nb = rows // tb
    order = (lambda i: (nb - 1 - i, 0)) if reverse else (lambda i: (i, 0))

    def body(x_ref, wa_ref, ba_ref, wx_ref, bx_ref, lam_ref, h_ref, hp_ref, a_s, b_s, state):
        @pl.when(pl.program_id(0) == 0)
        def _():
            state[...] = jnp.zeros_like(state)

        xv = x_ref[...]
        _, gi, _, a, s = _rglru_gates(xv, wa_ref, ba_ref, wx_ref, bx_ref, lam_ref)
        a_s[...] = a
        b_s[...] = s * (gi * xv)

        def group(base, h):
            av = a_s[pl.ds(base, SCAN_ROWS), :]
            bv = b_s[pl.ds(base, SCAN_ROWS), :]
            outs, prevs = [None] * SCAN_ROWS, [None] * SCAN_ROWS
            for k in range(SCAN_ROWS):
                r_ = SCAN_ROWS - 1 - k if reverse else k
                prevs[r_] = h
                h = av[r_:r_ + 1, :] * h + bv[r_:r_ + 1, :]
                outs[r_] = h
            h_ref[pl.ds(base, SCAN_ROWS), :] = jnp.concatenate(outs, axis=0)
            hp_ref[pl.ds(base, SCAN_ROWS), :] = jnp.concatenate(prevs, axis=0)
            return h

        state[0:1, :] = _scan_rows(tb, reverse, group, state[0:1, :])

    blk = pl.BlockSpec((tb, D_RNN), order)
    wspec = _full((RNN_BLOCKS, RNN_BW, RNN_BW))
    vec = _full((1, D_RNN))
    return pl.pallas_call(
        body, name=name, grid=(nb,), in_specs=[blk, wspec, vec, wspec, vec, vec], out_specs=(blk, blk),
        out_shape=(jax.ShapeDtypeStruct((rows, D_RNN), F32), jax.ShapeDtypeStruct((rows, D_RNN), F32)),
        scratch_shapes=[pltpu.VMEM((tb, D_RNN), F32), pltpu.VMEM((tb, D_RNN), F32), pltpu.VMEM((SCAN_ROWS, D_RNN), F32)],
        compiler_params=_params(("arbitrary",)),
    )(xs, wa, ba, wx, bx, lam)


def _rglru_bwd(xs, h_prev, dh, wa, ba, wx, bx, lam, *, reverse, name):
    rows = xs.shape[0]
    tb = _tile(rows, RNN_TB, SCAN_ROWS)
    nb = rows // tb
    back = not reverse
    order = (lambda i: (nb - 1 - i, 0)) if back else (lambda i: (i, 0))

    def body(x_ref, hp_ref, dh_ref, wa_ref, ba_ref, wx_ref, bx_ref, lam_ref,
             dx_ref, dwa_ref, dba_ref, dwx_ref, dbx_ref, dlam_ref, a_s, g_s, state):
        @pl.when(pl.program_id(0) == 0)
        def _():
            state[...] = jnp.zeros_like(state)
            dwa_ref[...] = jnp.zeros_like(dwa_ref)
            dwx_ref[...] = jnp.zeros_like(dwx_ref)
            dba_ref[...] = jnp.zeros_like(dba_ref)
            dbx_ref[...] = jnp.zeros_like(dbx_ref)
            dlam_ref[...] = jnp.zeros_like(dlam_ref)

        xv = x_ref[...]
        r, gi, sp, a, s = _rglru_gates(xv, wa_ref, ba_ref, wx_ref, bx_ref, lam_ref)
        a_s[...] = a

        def group(base, carry):
            av = a_s[pl.ds(base, SCAN_ROWS), :]
            dv = dh_ref[pl.ds(base, SCAN_ROWS), :]
            outs = [None] * SCAN_ROWS
            for k in range(SCAN_ROWS):
                r_ = SCAN_ROWS - 1 - k if back else k
                gt = dv[r_:r_ + 1, :] + carry
                outs[r_] = gt
                carry = av[r_:r_ + 1, :] * gt
            g_s[pl.ds(base, SCAN_ROWS), :] = jnp.concatenate(outs, axis=0)
            return carry

        state[0:1, :] = _scan_rows(tb, back, group, state[0:1, :])

        gv = g_s[...]
        d_a = gv * hp_ref[...]
        d_s = gv * (gi * xv)
        d_gi = gv * (s * xv)
        dx = gv * (s * gi)
        d_log_a = d_a * a - d_s * (a * a) / s
        d_r = d_log_a * (-RG_C * sp)
        lamv = lam_ref[...]
        d_sp = jnp.sum(d_log_a * (-RG_C * r), axis=0, keepdims=True)
        dlam_ref[...] += d_sp * (-_sigmoid(-lamv))
        d_zr = d_r * r * (1.0 - r)
        d_zi = d_gi * gi * (1.0 - gi)
        dba_ref[...] += jnp.sum(d_zr, axis=0, keepdims=True)
        dbx_ref[...] += jnp.sum(d_zi, axis=0, keepdims=True)
        xb = xv.astype(BF16)
        zrb, zib = d_zr.astype(BF16), d_zi.astype(BF16)
        parts = []
        for n in range(RNN_BLOCKS):
            cols = slice(n * RNN_BW, (n + 1) * RNN_BW)
            dwa_ref[n] += lax.dot_general(xb[:, cols], zrb[:, cols], (((0,), (0,)), ((), ())), preferred_element_type=F32)
            dwx_ref[n] += lax.dot_general(xb[:, cols], zib[:, cols], (((0,), (0,)), ((), ())), preferred_element_type=F32)
            parts.append(
                lax.dot_general(zrb[:, cols], wa_ref[n].astype(BF16), (((1,), (1,)), ((), ())), preferred_element_type=F32)
                + lax.dot_general(zib[:, cols], wx_ref[n].astype(BF16), (((1,), (1,)), ((), ())), preferred_element_type=F32))
        dx_ref[...] = dx + jnp.concatenate(parts, axis=-1)

    blk = pl.BlockSpec((tb, D_RNN), order)
    wspec = _full((RNN_BLOCKS, RNN_BW, RNN_BW))
    vec = _full((1, D_RNN))
    wshape = jax.ShapeDtypeStruct((RNN_BLOCKS, RNN_BW, RNN_BW), F32)
    vshape = jax.ShapeDtypeStruct((1, D_RNN), F32)
    return pl.pallas_call(
        body, name=name, grid=(nb,), in_specs=[blk, blk, blk, wspec, vec, wspec, vec, vec],
        out_specs=(blk, wspec, vec, wspec, vec, vec),
        out_shape=(jax.ShapeDtypeStruct((rows, D_RNN), F32), wshape, vshape, wshape, vshape, vshape),
        scratch_shapes=[pltpu.VMEM((tb, D_RNN), F32), pltpu.VMEM((tb, D_RNN), F32), pltpu.VMEM((SCAN_ROWS, D_RNN), F32)],
        compiler_params=_params(("arbitrary",)),
    )(xs, h_prev, dh, wa, ba, wx, bx, lam)


def _gelu(z):
    return 0.5 * z * (1.0 + jnp.tanh(GELU_C * (z + 0.044715 * z * z * z)))


def _gelu_grad(z):
    t = jnp.tanh(GELU_C * (z + 0.044715 * z * z * z))
    return 0.5 * (1.0 + t) + 0.5 * z * (1.0 - t * t) * (GELU_C * (1.0 + 3.0 * 0.044715 * z * z))


GATE_COL0 = XR_COL0 + D_RNN


RNN_OUT_COLS = 512


def _rnn_out_specs(rows, hf_off, hb_off):
    tr = _tile(rows, 256, 16)
    assert hf_off % tr == 0 and hb_off % tr == 0 and GATE_COL0 % RNN_OUT_COLS == 0
    fo, bo, go = hf_off // tr, hb_off // tr, GATE_COL0 // RNN_OUT_COLS
    hf_spec = pl.BlockSpec((tr, RNN_OUT_COLS), lambda i, j: (i + fo, j))
    hb_spec = pl.BlockSpec((tr, RNN_OUT_COLS), lambda i, j: (i + bo, j))
    gate_spec = pl.BlockSpec((tr, RNN_OUT_COLS), lambda i, j: (i, j + go))
    out_spec = pl.BlockSpec((tr, RNN_OUT_COLS), lambda i, j: (i, j))
    return (rows // tr, D_RNN // RNN_OUT_COLS), hf_spec, hb_spec, gate_spec, out_spec


def _rnn_out_fwd(hf, hb, proj, hf_off, hb_off):
    rows = proj.shape[0]
    grid, hf_spec, hb_spec, gate_spec, out_spec = _rnn_out_specs(rows, hf_off, hb_off)

    def body(hf_ref, hb_ref, g_ref, o_ref):
        o_ref[...] = ((hf_ref[...] + hb_ref[...]) * _gelu(g_ref[...])).astype(o_ref.dtype)

    return pl.pallas_call(
        body, name="rnn_out_fwd", grid=grid, in_specs=[hf_spec, hb_spec, gate_spec], out_specs=out_spec,
        out_shape=jax.ShapeDtypeStruct((rows, D_RNN), BF16), compiler_params=_params(("parallel", "parallel")),
    )(hf, hb, proj)


def _rnn_out_bwd(d_cat, hf, hb, proj, hf_off, hb_off):
    rows = proj.shape[0]
    grid, hf_spec, hb_spec, gate_spec, out_spec = _rnn_out_specs(rows, hf_off, hb_off)
    do = ATTN_W // RNN_OUT_COLS

    def body(d_ref, hf_ref, hb_ref, g_ref, dh_ref, dg_ref):
        dv, gv = d_ref[...].astype(F32), g_ref[...]
        dh_ref[...] = dv * _gelu(gv)
        dg_ref[...] = (dv * (hf_ref[...] + hb_ref[...]) * _gelu_grad(gv)).astype(dg_ref.dtype)

    tr = out_spec.block_shape[0]
    return pl.pallas_call(
        body, name="rnn_out_bwd", grid=grid,
        in_specs=[pl.BlockSpec((tr, RNN_OUT_COLS), lambda i, j: (i, j + do)), hf_spec, hb_spec, gate_spec],
        out_specs=(out_spec, out_spec),
        out_shape=(jax.ShapeDtypeStruct((rows, D_RNN), F32), jax.ShapeDtypeStruct((rows, D_RNN), BF16)),
        compiler_params=_params(("parallel", "parallel")),
    )(d_cat, hf, hb, proj)


def _gmlp_parts(z_ref, vg_ref, vb_ref, d_gm):
    zu, zv = z_ref[:, :d_gm], z_ref[:, d_gm:]
    u = _gelu(zu)
    v = _gelu(zv)
    mu = jnp.mean(v, axis=-1, keepdims=True)
    vc = v - mu
    rstd = lax.rsqrt(jnp.mean(vc * vc, axis=-1, keepdims=True) + EPS)
    vhat = vc * rstd
    vn = vhat * vg_ref[...] + vb_ref[...]
    return zu, zv, u, vhat, rstd, vn


def _gmlp_fwd(z, v_g, v_b, w_sp, b_sp_t):
    rows, d_gm = z.shape[0], z.shape[1] // 2
    tr = _tile(rows, 256, CHUNK)
    gwid = d_gm // GM_GROUPS

    def body(z_ref, vg_ref, vb_ref, w_ref, b_ref, o_ref):
        _, _, u, _, _, vn = _gmlp_parts(z_ref, vg_ref, vb_ref, d_gm)
        vnb = vn.astype(BF16)
        for g in range(GM_GROUPS):
            wg = w_ref[g].astype(BF16)
            for c in range(tr // CHUNK):
                rs, cs = slice(c * CHUNK, (c + 1) * CHUNK), slice(g * gwid, (g + 1) * gwid)
                sv = jnp.dot(wg, vnb[rs, cs], preferred_element_type=F32) + b_ref[:, g:g + 1]
                o_ref[rs, cs] = (u[rs, cs] * sv).astype(o_ref.dtype)

    return pl.pallas_call(
        body, name="gmlp_fwd", grid=(rows // tr,),
        in_specs=[pl.BlockSpec((tr, 2 * d_gm), lambda i: (i, 0)), _full((1, d_gm)), _full((1, d_gm)),
                  _full(w_sp.shape), _full(b_sp_t.shape)],
        out_specs=pl.BlockSpec((tr, d_gm), lambda i: (i, 0)),
        out_shape=jax.ShapeDtypeStruct((rows, d_gm), BF16), compiler_params=_params(("parallel",)),
    )(z, v_g, v_b, w_sp, b_sp_t)


def _gmlp_bwd(z, dgate, v_g, v_b, w_sp, b_sp_t):
    rows, d_gm = z.shape[0], z.shape[1] // 2
    tr = _tile(rows, 256, CHUNK)
    gwid = d_gm // GM_GROUPS

    def body(z_ref, dg_ref, vg_ref, vb_ref, w_ref, b_ref, dz_ref, dbin_ref, dvg_ref, dvb_ref, dw_ref, dbs_ref, dvn_s):
        @pl.when(pl.program_id(0) == 0)
        def _():
            dbin_ref[...] = jnp.zeros_like(dbin_ref)
            dvg_ref[...] = jnp.zeros_like(dvg_ref)
            dvb_ref[...] = jnp.zeros_like(dvb_ref)
            dw_ref[...] = jnp.zeros_like(dw_ref)
            dbs_ref[...] = jnp.zeros_like(dbs_ref)

        zu, zv, u, vhat, rstd, vn = _gmlp_parts(z_ref, vg_ref, vb_ref, d_gm)
        vnb = vn.astype(BF16)
        dgv = dg_ref[...].astype(F32)
        dsv = dgv * u
        dsvb = dsv.astype(BF16)
        for g in range(GM_GROUPS):
            wg = w_ref[g].astype(BF16)
            cs = slice(g * gwid, (g + 1) * gwid)
            for c in range(tr // CHUNK):
                rs = slice(c * CHUNK, (c + 1) * CHUNK)
                sv = jnp.dot(wg, vnb[rs, cs], preferred_element_type=F32) + b_ref[:, g:g + 1]
                dz_ref[rs, cs] = (dgv[rs, cs] * sv * _gelu_grad(zu[rs, cs])).astype(dz_ref.dtype)
                dw_ref[g] += lax.dot_general(dsvb[rs, cs], vnb[rs, cs], (((1,), (1,)), ((), ())),
                                             preferred_element_type=F32)
                dbs_ref[:, g:g + 1] += jnp.sum(dsv[rs, cs], axis=-1, keepdims=True)
                dvn_s[rs, cs] = lax.dot_general(wg, dsvb[rs, cs], (((0,), (0,)), ((), ())), preferred_element_type=F32)
        dvn = dvn_s[...]
        dvg_ref[...] += jnp.sum(dvn * vhat, axis=0, keepdims=True)
        dvb_ref[...] += jnp.sum(dvn, axis=0, keepdims=True)
        dvh = dvn * vg_ref[...]
        dv = rstd * (dvh - jnp.mean(dvh, axis=-1, keepdims=True) - vhat * jnp.mean(dvh * vhat, axis=-1, keepdims=True))
        dzv = dv * _gelu_grad(zv)
        dz_ref[:, d_gm:] = dzv.astype(dz_ref.dtype)
        dbin_ref[:, d_gm:] += jnp.sum(dzv, axis=0, keepdims=True)
        dbin_ref[:, :d_gm] += jnp.sum(dz_ref[:, :d_gm].astype(F32), axis=0, keepdims=True)

    return pl.pallas_call(
        body, name="gmlp_bwd", grid=(rows // tr,),
        in_specs=[pl.BlockSpec((tr, 2 * d_gm), lambda i: (i, 0)), pl.BlockSpec((tr, d_gm), lambda i: (i, 0)),
                  _full((1, d_gm)), _full((1, d_gm)), _full(w_sp.shape), _full(b_sp_t.shape)],
        out_specs=(pl.BlockSpec((tr, 2 * d_gm), lambda i: (i, 0)), _full((1, 2 * d_gm)), _full((1, d_gm)),
                   _full((1, d_gm)), _full(w_sp.shape), _full(b_sp_t.shape)),
        out_shape=(jax.ShapeDtypeStruct((rows, 2 * d_gm), BF16), jax.ShapeDtypeStruct((1, 2 * d_gm), F32),
                   jax.ShapeDtypeStruct((1, d_gm), F32), jax.ShapeDtypeStruct((1, d_gm), F32),
                   jax.ShapeDtypeStruct(w_sp.shape, F32), jax.ShapeDtypeStruct(b_sp_t.shape, F32)),
        scratch_shapes=[pltpu.VMEM((tr, d_gm), F32)],
        compiler_params=_params(("arbitrary",)),
    )(z, dgate, v_g, v_b, w_sp, b_sp_t)


def _adamw_math(w, g, m, v):
    m = ADAM_B1 * m + (1.0 - ADAM_B1) * g
    v = ADAM_B2 * v + (1.0 - ADAM_B2) * (g * g)
    m_hat = m / (1.0 - ADAM_B1 ** ADAM_STEP)
    v_hat = v / (1.0 - ADAM_B2 ** ADAM_STEP)
    delta = -ADAM_LR * (m_hat / (jnp.sqrt(v_hat) + ADAM_EPS) + ADAM_WD * w)
    return delta, m, v


def _adamw(w, g, m, v, name):
    shape = w.shape
    outs = _rowwise(_adamw_math, (F32, F32, F32), _as2d(w), _as2d(g), _as2d(m), _as2d(v), name=name)
    return (g.reshape(shape),) + tuple(o.reshape(shape) for o in outs)


PACK_COLS = 1024


def _pack(arrays):
    flat = jnp.concatenate([a.reshape(-1).astype(F32) for a in arrays])
    pad = (-flat.size) % (8 * PACK_COLS)
    return jnp.pad(flat, (0, pad)).reshape(-1, PACK_COLS)


def _unpack(flat, shapes):
    out, pos = [], 0
    for shp in shapes:
        n = math.prod(shp)
        out.append(flat[pos:pos + n].reshape(shp))
        pos += n
    return out


def _unpack_devices(packed8, shapes):
    flat8 = packed8.reshape(N_DEV, -1)
    out, pos = [], 0
    for shp in shapes:
        n = math.prod(shp)
        out.append(flat8[:, pos:pos + n].reshape((N_DEV,) + tuple(shp)))
        pos += n
    return out


def _sum_devices(g8):
    _, rows, cols = g8.shape
    tr = _rows_tile(rows, cols, budget=256 * 1024)

    def body(g_ref, o_ref):
        acc = g_ref[0]
        for d in range(1, N_DEV):
            acc = acc + g_ref[d]
        o_ref[...] = acc

    return pl.pallas_call(
        body, name="sum_devices", grid=(rows // tr,), in_specs=[pl.BlockSpec((N_DEV, tr, cols), lambda i: (0, i, 0))],
        out_specs=pl.BlockSpec((tr, cols), lambda i: (i, 0)), out_shape=jax.ShapeDtypeStruct((rows, cols), F32),
        compiler_params=_params(("parallel",)),
    )(g8)


def _place():
    return lax.axis_index("x"), lax.axis_index("y"), lax.axis_index("c")


def _other_chips(x, y):
    return [(1 - x, y), (x, 1 - y), (1 - x, 1 - y)]


def _remote(src, dst, send_sem, recv_sem, to):
    return pltpu.make_async_remote_copy(src_ref=src, dst_ref=dst, send_sem=send_sem, recv_sem=recv_sem, device_id=to,
                                        device_id_type=MESH)


def _comm_call(body, name, operands, out_shapes, n_remote, n_local, aliases=None):
    return pl.pallas_call(
        body, name=name, out_shape=tuple(out_shapes), in_specs=[ANY] * len(operands), out_specs=tuple(ANY for _ in out_shapes),
        scratch_shapes=[pltpu.SemaphoreType.DMA((n_remote,)), pltpu.SemaphoreType.DMA((n_remote,)),
                        pltpu.SemaphoreType.DMA((max(n_local, 1),))],
        input_output_aliases=aliases or {},
    )(*operands)


def _in_place(arrays):
    return [jax.ShapeDtypeStruct(a.shape, a.dtype) for a in arrays], {i: i for i in range(len(arrays))}


def _allgather8(arrs, name):
    n = len(arrs)

    def body(*refs):
        ins, outs = refs[:n], refs[n:2 * n]
        send, recv, lsem = refs[2 * n:]
        x, y, c = _place()
        me, sib = (x, y, c), (x, y, 1 - c)
        chips = _other_chips(x, y)

        def slot(t, px, py, pc):
            return outs[t].at[4 * px + 2 * py + pc]

        def cp(t, k, block, to, from_input=False):
            src = ins[t] if from_input else slot(t, *block)
            return _remote(src, slot(t, *block), send.at[7 * t + k], recv.at[7 * t + k], to)

        mine = [pltpu.make_async_copy(ins[t], slot(t, *me), lsem.at[t]) for t in range(n)]
        for cpy in mine:
            cpy.start()
        first = []
        for t in range(n):
            first.append(cp(t, 0, me, sib, True))
            first += [cp(t, 1 + j, me, (*chip, c), True) for j, chip in enumerate(chips)]
        for cpy in first:
            cpy.start()
        passed = []
        for t in range(n):
            for j, chip in enumerate(chips):
                cp(t, 1 + j, (*chip, c), me).wait_recv()
                fwd = cp(t, 4 + j, (*chip, c), sib)
                fwd.start()
                passed.append(fwd)
        for t in range(n):
            cp(t, 0, sib, me).wait_recv()
            for j, chip in enumerate(chips):
                cp(t, 4 + j, (*chip, 1 - c), me).wait_recv()
        for cpy in first + passed:
            cpy.wait_send()
        for cpy in mine:
            cpy.wait()

    outs = _comm_call(body, name, arrs, [jax.ShapeDtypeStruct((N_DEV,) + a.shape, a.dtype) for a in arrs], 7 * n, n)
    return list(outs)


def _gather_weights(bufs):
    n_u = len(bufs)

    def body(*refs):
        bufs_ = refs[n_u:2 * n_u]
        send, recv, _ = refs[2 * n_u:]
        x, y, c = _place()
        me, sib, q = (x, y, c), (x, y, 1 - c), 2 * x + y
        chips = _other_chips(x, y)
        sent = []
        for u in range(n_u):
            half = bufs_[u].shape[1] // 2
            mine = bufs_[u].at[q, pl.ds(c * half, half)]
            for j, chip in enumerate(chips):
                cpy = _remote(mine, mine, send.at[6 * u + j], recv.at[6 * u + j], (*chip, c))
                cpy.start()
                sent.append(cpy)
        for u in range(n_u):
            half = bufs_[u].shape[1] // 2
            for j, chip in enumerate(chips):
                landed = bufs_[u].at[2 * chip[0] + chip[1], pl.ds(c * half, half)]
                _remote(landed, landed, send.at[6 * u + j], recv.at[6 * u + j], me).wait_recv()
                cpy = _remote(landed, landed, send.at[6 * u + 3 + j], recv.at[6 * u + 3 + j], sib)
                cpy.start()
                sent.append(cpy)
        for u in range(n_u):
            half = bufs_[u].shape[1] // 2
            for j, chip in enumerate(chips):
                landed = bufs_[u].at[2 * chip[0] + chip[1], pl.ds((1 - c) * half, half)]
                _remote(landed, landed, send.at[6 * u + 3 + j], recv.at[6 * u + 3 + j], me).wait_recv()
        for cpy in sent:
            cpy.wait_send()

    shapes, aliases = _in_place(bufs)
    return list(_comm_call(body, "gather_weights", bufs, shapes, 6 * n_u, 0, aliases))


def _exchange_halves(grads):
    n = len(grads)

    def body(*refs):
        ins, outs = refs[:n], refs[n:2 * n]
        send, recv, _ = refs[2 * n:]
        x, y, c = _place()
        sib = (x, y, 1 - c)
        sent = []
        for k in range(n):
            half = ins[k].shape[1] // 2
            cpy = _remote(ins[k].at[pl.ds(0, N_CHIPS), pl.ds((1 - c) * half, half)], outs[k], send.at[k], recv.at[k], sib)
            cpy.start()
            sent.append(cpy)
        for cpy in sent:
            cpy.wait()

    shapes = [jax.ShapeDtypeStruct((N_CHIPS, g.shape[1] // 2, g.shape[2]), g.dtype) for g in grads]
    return list(_comm_call(body, "exchange_halves", grads, shapes, n, 0))


def _chips_all_to_all(sums):
    n = len(sums)

    def body(*refs):
        ins, outs = refs[:n], refs[n:2 * n]
        send, recv, _ = refs[2 * n:]
        x, y, c = _place()
        sent = []
        for k in range(n):
            for j, chip in enumerate(_other_chips(x, y)):
                cpy = _remote(ins[k].at[2 * chip[0] + chip[1]], outs[k].at[j], send.at[3 * k + j], recv.at[3 * k + j], (*chip, c))
                cpy.start()
                sent.append(cpy)
        for cpy in sent:
            cpy.wait()

    shapes = [jax.ShapeDtypeStruct((N_CHIPS - 1,) + s.shape[1:], s.dtype) for s in sums]
    return list(_comm_call(body, "chips_all_to_all", sums, shapes, 3 * n, 0))


def _join_halves(bufs):
    n = len(bufs)
    units = [(k, layer) for k in range(n) for layer in range(bufs[k].shape[0])]

    def body(*refs):
        bufs_ = refs[n:2 * n]
        send, recv, _ = refs[2 * n:]
        x, y, c = _place()
        sent = []
        for u, (k, layer) in enumerate(units):
            half = bufs_[k].shape[1] // 2
            mine = bufs_[k].at[layer, pl.ds(c * half, half)]
            cpy = _remote(mine, mine, send.at[u], recv.at[u], (x, y, 1 - c))
            cpy.start()
            sent.append(cpy)
        for u, (k, layer) in enumerate(units):
            half = bufs_[k].shape[1] // 2
            theirs = bufs_[k].at[layer, pl.ds((1 - c) * half, half)]
            _remote(theirs, theirs, send.at[u], recv.at[u], (x, y, c)).wait_recv()
        for cpy in sent:
            cpy.wait_send()

    shapes, aliases = _in_place(bufs)
    return list(_comm_call(body, "join_halves", bufs, shapes, len(units), 0, aliases))


def _add_halves(grad, other, place):
    _, rows, cols = grad.shape
    half = rows // 2
    tr = _rows_tile(half, cols, itemsize=2, budget=1024 * 1024)
    per_half = half // tr

    def body(place_ref, g_ref, o_ref, s_ref):
        s_ref[...] = (g_ref[...].astype(F32) + o_ref[...].astype(F32)).astype(s_ref.dtype)

    return pl.pallas_call(
        body, name="add_halves", out_shape=jax.ShapeDtypeStruct((N_CHIPS, half, cols), grad.dtype),
        grid_spec=pltpu.PrefetchScalarGridSpec(
            num_scalar_prefetch=1, grid=(N_CHIPS, per_half),
            in_specs=[pl.BlockSpec((None, tr, cols), lambda k, i, pr: (k, pr[1] * per_half + i, 0)),
                      pl.BlockSpec((None, tr, cols), lambda k, i, pr: (k, i, 0))],
            out_specs=pl.BlockSpec((None, tr, cols), lambda k, i, pr: (k, i, 0))),
        compiler_params=_params(("parallel", "parallel")),
    )(place, grad, other)


def _add_chips(sums, others, place, dest, layer, n_layers):
    _, half, cols = sums.shape
    tr = _rows_tile(half, cols, itemsize=4, budget=1024 * 1024)
    per_half = half // tr

    def body(place_ref, s_ref, o_ref, *rest):
        acc = s_ref[...].astype(F32)
        for j in range(N_CHIPS - 1):
            acc = acc + o_ref[j].astype(F32)
        rest[-1][...] = acc

    operands = [place, sums, others] + ([] if dest is None else [dest])
    return pl.pallas_call(
        body, name="add_chips", out_shape=jax.ShapeDtypeStruct((n_layers, 2 * half, cols), F32),
        grid_spec=pltpu.PrefetchScalarGridSpec(
            num_scalar_prefetch=1, grid=(per_half,),
            in_specs=[pl.BlockSpec((None, tr, cols), lambda i, pr: (pr[0], i, 0)),
                      pl.BlockSpec((N_CHIPS - 1, tr, cols), lambda i, pr: (0, i, 0))] + ([] if dest is None else [ANY]),
            out_specs=pl.BlockSpec((None, tr, cols), lambda i, pr: (layer, pr[1] * per_half + i, 0))),
        input_output_aliases={} if dest is None else {3: 0},
        compiler_params=_params(("parallel",)),
    )(*operands)


def _reduce_scatter(grads, layout, place):
    received = _exchange_halves(grads)
    sums = [_add_halves(g, r, place) for g, r in zip(grads, received)]
    others = _chips_all_to_all(sums)
    n_layers = {p: 1 + max(l for pp, l in layout if pp == p) for p, _ in layout}
    bufs = {}
    for (p, layer), s, o in zip(layout, sums, others):
        bufs[p] = _add_chips(s, o, place, bufs.get(p), layer, n_layers[p])
    return _join_halves([bufs[p] for p in sorted(bufs)])


def _rope_tables(n):
    t = jnp.arange(n)
    freqs = ROPE_THETA ** (-jnp.arange(ROPE_PAIRS, dtype=F32) / ROPE_PAIRS)
    ang_r = (t // GRID_W).astype(F32)[:, None] * freqs
    ang_c = (t % GRID_W).astype(F32)[:, None] * freqs
    cos = jnp.concatenate([jnp.cos(ang_r), jnp.cos(ang_r), jnp.cos(ang_c), jnp.cos(ang_c)], axis=-1)
    sin = jnp.concatenate([-jnp.sin(ang_r), jnp.sin(ang_r), -jnp.sin(ang_c), jnp.sin(ang_c)], axis=-1)
    return cos, sin


def _ffn_fwd(h2, w1, w2, tag):
    r, a = _matmul(h2, w1, kind='nn', b_split='n', out_dtype=BF16, epilogue='relu2', name=f"ffn_in_{tag}")
    f = _matmul(a, w2, kind='nn', b_split='k', out_dtype=F32, name=f"ffn_out_{tag}")
    return r, a, f


def _ffn_bwd(d_f, h2, r, a, w1, w2, tag):
    d_u = _matmul(d_f, w2, kind='nt', b_split='k', out_dtype=BF16, epilogue='times2x', extra=r, name=f"ffn_out_dx_{tag}")
    d_w2 = _matmul(a, d_f, kind='tn', out_split='k', out_dtype=BF16, name=f"ffn_out_dw_{tag}")
    d_w1 = _matmul(h2, d_u, kind='tn', out_split='n', out_dtype=BF16, name=f"ffn_in_dw_{tag}")
    d_h2 = _matmul(d_u, w1, kind='nt', b_split='n', out_dtype=F32, name=f"ffn_in_dx_{tag}")
    return d_h2, d_w1, d_w2


def _local_step(xl0, xc0, target, ml, mc0, sp, big):
    n_lat, n_ctx = xl0.shape[0], xc0.shape[0]
    one = lambda v: 1.0 + v
    g = [[sp['norm_g'][i, k][None, :] for k in range(4)] for i in range(2)]

    sh1, sc1, gt1, sh2, sc2, gt2 = ml[0]
    hl = _norm_fwd(xl0, g[0][0], one(sc1), b=sh1, out_dtype=BF16, name="l0_mod1")
    hc = _norm_fwd(xc0, g[0][0], one(mc0[1]), b=mc0[0], out_dtype=BF16, name="l0_mod1_ctx")
    proj_l = _matmul(hl, big['ar_in'], kind='nn', b_split='n', out_dtype=F32, name="ar_in_lat")
    proj_c = _matmul(hc, big['ar_in'], kind='nn', b_split='n', out_dtype=F32, name="ar_in_ctx")
    cos_l, sin_l = _rope_tables(n_lat)
    cos_c, sin_c = jnp.ones((n_ctx, HEAD_DIM), F32), jnp.zeros((n_ctx, HEAD_DIM), F32)
    q_g, k_g = sp['q_g'], sp['k_g']
    q_l, k_l, v_l = _qk_fwd(proj_l, q_g, k_g, cos_l, sin_l, name="qk_fwd_lat")
    _, k_c, v_c = _qk_fwd(proj_c, q_g, k_g, cos_c, sin_c, name="qk_fwd_ctx")
    k_all = jnp.concatenate([k_c, k_l], axis=0)
    v_all = jnp.concatenate([v_c, v_l], axis=0)
    attn, lse = _attn_fwd(q_l, k_all, v_all)
    conv_l = _conv_fwd(proj_l, sp['conv_w'], sp['conv_b'], name="conv_fwd_lat")
    conv_c = _conv_fwd(proj_c, sp['conv_w'], sp['conv_b'], name="conv_fwd_ctx")
    xs_f = jnp.concatenate([conv_c, conv_l], axis=0)
    xs_r = jnp.concatenate([conv_l, conv_c], axis=0)
    rnn_w = [(sp['wa'][d], sp['ba'][d][None, :], sp['wx'][d], sp['bx'][d][None, :], sp['lam'][d][None, :]) for d in range(2)]
    h_f, hp_f = _rglru_fwd(xs_f, *rnn_w[0], reverse=False, name="rglru_fwd_f")
    h_r, hp_r = _rglru_fwd(xs_r, *rnn_w[1], reverse=True, name="rglru_fwd_r")
    rnn = _rnn_out_fwd(h_f, h_r, proj_l, n_ctx, 0)
    cat = jnp.concatenate([attn, rnn], axis=1)
    ol0 = _matmul(cat, big['ar_out'], kind='nn', b_split='k', out_dtype=F32, name="ar_out")
    xm0 = _norm_fwd(ol0, g[0][1], gt1, res=xl0, out_dtype=F32, name="l0_res1")
    h2_0 = _norm_fwd(xm0, g[0][2], one(sc2), b=sh2, out_dtype=BF16, name="l0_mod2")
    r0, a0, f0 = _ffn_fwd(h2_0, big['ff_in'][0], big['ff_out'][0], "l0")
    xl1 = _norm_fwd(f0, g[0][3], gt2, res=xm0, out_dtype=F32, name="l0_res2")

    th1, tc1, tg1, th2, tc2, tg2 = ml[1]
    hl1 = _norm_fwd(xl1, g[1][0], one(tc1), b=th1, out_dtype=BF16, name="l1_mod1")
    z = _matmul(hl1, big['gm_in'], kind='nn', b_split='n', bias=sp['gm_b_in'], out_dtype=F32, name="gm_in")
    b_sp_t = sp['gm_b_sp'].T
    gated = _gmlp_fwd(z, sp['gm_v_g'], sp['gm_v_b'], sp['gm_w_sp'], b_sp_t)
    ol1 = _matmul(gated, big['gm_out'], kind='nn', b_split='k', out_dtype=F32, name="gm_out")
    xm1 = _norm_fwd(ol1, g[1][1], tg1, res=xl1, out_dtype=F32, name="l1_res1")
    h2_1 = _norm_fwd(xm1, g[1][2], one(tc2), b=th2, out_dtype=BF16, name="l1_mod2")
    r1, a1, f1 = _ffn_fwd(h2_1, big['ff_in'][1], big['ff_out'][1], "l1")
    y = _norm_fwd(f1, g[1][3], tg2, res=xm1, out_dtype=F32, name="l1_res2")

    dy, loss = _loss_head(y, target)

    d_f1, dg13, d_tg2, _ = _norm_bwd(dy, f1, g[1][3], tg2, out_dtype=BF16, name="l1_res2_bwd")
    d_h2, dw_ff_in1, dw_ff_out1 = _ffn_bwd(d_f1, h2_1, r1, a1, big['ff_in'][1], big['ff_out'][1], "l1")
    dxm1, dg12, d_tc2, d_th2 = _norm_bwd(d_h2, xm1, g[1][2], one(tc2), extra=dy, out_dtype=F32, name="l1_mod2_bwd")
    d_ol1, dg11, d_tg1, _ = _norm_bwd(dxm1, ol1, g[1][1], tg1, out_dtype=BF16, name="l1_res1_bwd")
    d_gated = _matmul(d_ol1, big['gm_out'], kind='nt', b_split='k', out_dtype=F32, name="gm_out_dx")
    dw_gm_out = _matmul(gated, d_ol1, kind='tn', out_split='k', out_dtype=BF16, name="gm_out_dw")
    d_z, d_gm_b_in, d_vg, d_vb, d_wsp, d_bsp_t = _gmlp_bwd(z, d_gated, sp['gm_v_g'], sp['gm_v_b'], sp['gm_w_sp'], b_sp_t)
    dw_gm_in = _matmul(hl1, d_z, kind='tn', out_split='n', out_dtype=BF16, name="gm_in_dw")
    d_hl1 = _matmul(d_z, big['gm_in'], kind='nt', b_split='n', out_dtype=F32, name="gm_in_dx")
    dxl1, dg10, d_tc1, d_th1 = _norm_bwd(d_hl1, xl1, g[1][0], one(tc1), extra=dxm1, out_dtype=F32, name="l1_mod1_bwd")

    d_f0, dg03, d_gt2, _ = _norm_bwd(dxl1, f0, g[0][3], gt2, out_dtype=BF16, name="l0_res2_bwd")
    d_h2, dw_ff_in0, dw_ff_out0 = _ffn_bwd(d_f0, h2_0, r0, a0, big['ff_in'][0], big['ff_out'][0], "l0")
    dxm0, dg02, d_sc2, d_sh2 = _norm_bwd(d_h2, xm0, g[0][2], one(sc2), extra=dxl1, out_dtype=F32, name="l0_mod2_bwd")
    d_ol0, dg01, d_gt1, _ = _norm_bwd(dxm0, ol0, g[0][1], gt1, out_dtype=BF16, name="l0_res1_bwd")
    d_cat = _matmul(d_ol0, big['ar_out'], kind='nt', b_split='k', out_dtype=F32, name="ar_out_dx")
    dw_ar_out = _matmul(cat, d_ol0, kind='tn', out_split='k', out_dtype=BF16, name="ar_out_dw")
    dq, dk_all, dv_all = _attn_bwd(q_l, k_all, v_all, attn, lse, d_cat)
    d_h, d_gate = _rnn_out_bwd(d_cat, h_f, h_r, proj_l, n_ctx, 0)
    zeros_c = jnp.zeros((n_ctx, D_RNN), F32)
    dxs_f, d_wa0, d_ba0, d_wx0, d_bx0, d_lam0 = _rglru_bwd(
        xs_f, hp_f, jnp.concatenate([zeros_c, d_h], axis=0), *rnn_w[0], reverse=False, name="rglru_bwd_f")
    dxs_r, d_wa1, d_ba1, d_wx1, d_bx1, d_lam1 = _rglru_bwd(
        xs_r, hp_r, jnp.concatenate([d_h, zeros_c], axis=0), *rnn_w[1], reverse=True, name="rglru_bwd_r")
    d_xr_l, d_cw_l, d_cb_l = _conv_bwd(dxs_f[n_ctx:], dxs_r[:n_lat], proj_l, sp['conv_w'], name="conv_bwd_lat")
    d_xr_c, d_cw_c, d_cb_c = _conv_bwd(dxs_f[:n_ctx], dxs_r[n_lat:], proj_c, sp['conv_w'], name="conv_bwd_ctx")
    dp_qk_l, d_qg, d_kg_l = _qk_bwd(dq, dk_all[n_ctx:], proj_l, q_g, k_g, cos_l, sin_l, name="qk_bwd_lat")
    dp_qk_c, _, d_kg_c = _qk_bwd(None, dk_all[:n_ctx], proj_c, q_g, k_g, cos_c, sin_c, name="qk_bwd_ctx")
    dv_b = dv_all.astype(BF16)
    d_proj_l = jnp.concatenate([dp_qk_l, dv_b[n_ctx:], d_xr_l, d_gate], axis=1)
    d_proj_c = jnp.concatenate([dp_qk_c, dv_b[:n_ctx], d_xr_c, jnp.zeros((n_ctx, D_RNN), BF16)], axis=1)
    dw_ar_in = _matmul(jnp.concatenate([hc, hl], axis=0), jnp.concatenate([d_proj_c, d_proj_l], axis=0), kind='tn',
                       out_split='n', out_dtype=BF16, name="ar_in_dw")
    d_hl = _matmul(d_proj_l, big['ar_in'], kind='nt', b_split='n', out_dtype=F32, name="ar_in_dx_lat")
    d_hc = _matmul(d_proj_c, big['ar_in'], kind='nt', b_split='n', out_dtype=F32, name="ar_in_dx_ctx")
    grad_x, dg00, d_sc1, d_sh1 = _norm_bwd(d_hl, xl0, g[0][0], one(sc1), extra=dxm0, out_dtype=F32, name="l0_mod1_bwd")
    _, dg00c, d_mc_scale, d_mc_shift = _norm_bwd(d_hc, xc0, g[0][0], one(mc0[1]), out_dtype=BF16, name="l0_mod1_ctx_bwd")

    zeros_d = jnp.zeros_like(d_sh1)
    small = {
        'd_ml0': jnp.concatenate([d_sh1, d_sc1, d_gt1, d_sh2, d_sc2, d_gt2], axis=1),
        'd_ml1': jnp.concatenate([d_th1, d_tc1, d_tg1, d_th2, d_tc2, d_tg2], axis=1),
        'd_mc0': jnp.concatenate([d_mc_shift, d_mc_scale] + [zeros_d] * 4, axis=1),
        'norm_g': jnp.stack([jnp.concatenate([dg00 + dg00c, dg01, dg02, dg03], axis=0),
                             jnp.concatenate([dg10, dg11, dg12, dg13], axis=0)]),
        'q_g': d_qg, 'k_g': d_kg_l + d_kg_c, 'conv_w': d_cw_l + d_cw_c, 'conv_b': d_cb_l + d_cb_c,
        'wa': jnp.stack([d_wa0, d_wa1]), 'ba': jnp.concatenate([d_ba0, d_ba1], axis=0),
        'wx': jnp.stack([d_wx0, d_wx1]), 'bx': jnp.concatenate([d_bx0, d_bx1], axis=0),
        'lam': jnp.concatenate([d_lam0, d_lam1], axis=0),
        'gm_b_in': d_gm_b_in, 'gm_v_g': d_vg, 'gm_v_b': d_vb, 'gm_w_sp': d_wsp, 'gm_b_sp': d_bsp_t.T,
        'loss': loss,
    }
    big_grads = {'ff_in': [dw_ff_in0, dw_ff_in1], 'ff_out': [dw_ff_out0, dw_ff_out1], 'ar_in': dw_ar_in,
                 'ar_out': dw_ar_out, 'gm_in': dw_gm_in, 'gm_out': dw_gm_out}
    return grad_x, big_grads, small


MOD_ROWS = 16
SMALL_ORDER = ('d_ml0', 'd_ml1', 'd_mc0', 'norm_g', 'q_g', 'k_g', 'conv_w', 'conv_b', 'wa', 'ba', 'wx', 'bx', 'lam',
               'gm_b_in', 'gm_v_g', 'gm_v_b', 'gm_w_sp', 'gm_b_sp', 'loss')


def _silu(v):
    return v * _sigmoid(v)


def _chip_concat(gathered, axis):
    return jnp.concatenate([gathered[2 * q] for q in range(N_CHIPS)], axis=axis)


def kernel(x, c, ctx, c_ctx, w_mod, b_mod, norm_g, w_ff_in, w_ff_out, ar_w_in, ar_q_g, ar_k_g, ar_conv_w, ar_conv_b, ar_wa, ar_ba, ar_wx, ar_bx, ar_lambda, ar_w_out, gm_w_in, gm_b_in, gm_v_g, gm_v_b, gm_w_sp, gm_b_sp, gm_w_out, loss_target, m_c_ctx, m_w_mod, m_b_mod, m_norm_g, m_w_ff_in, m_w_ff_out, m_ar_w_in, m_ar_q_g, m_ar_k_g, m_ar_conv_w, m_ar_conv_b, m_ar_wa, m_ar_ba, m_ar_wx, m_ar_bx, m_ar_lambda, m_ar_w_out, m_gm_w_in, m_gm_b_in, m_gm_v_g, m_gm_v_b, m_gm_w_sp, m_gm_b_sp, m_gm_w_out, v_c_ctx, v_w_mod, v_b_mod, v_norm_g, v_w_ff_in, v_w_ff_out, v_ar_w_in, v_ar_q_g, v_ar_k_g, v_ar_conv_w, v_ar_conv_b, v_ar_wa, v_ar_ba, v_ar_wx, v_ar_bx, v_ar_lambda, v_ar_w_out, v_gm_w_in, v_gm_b_in, v_gm_v_g, v_gm_v_b, v_gm_w_sp, v_gm_b_sp, v_gm_w_out):
    weights = dict(c_ctx=c_ctx, w_mod=w_mod, b_mod=b_mod, norm_g=norm_g, w_ff_in=w_ff_in, w_ff_out=w_ff_out, ar_w_in=ar_w_in,
                   ar_q_g=ar_q_g, ar_k_g=ar_k_g, ar_conv_w=ar_conv_w, ar_conv_b=ar_conv_b, ar_wa=ar_wa, ar_ba=ar_ba, ar_wx=ar_wx,
                   ar_bx=ar_bx, ar_lambda=ar_lambda, ar_w_out=ar_w_out, gm_w_in=gm_w_in, gm_b_in=gm_b_in, gm_v_g=gm_v_g,
                   gm_v_b=gm_v_b, gm_w_sp=gm_w_sp, gm_b_sp=gm_b_sp, gm_w_out=gm_w_out)
    m_in = dict(c_ctx=m_c_ctx, w_mod=m_w_mod, b_mod=m_b_mod, norm_g=m_norm_g, w_ff_in=m_w_ff_in, w_ff_out=m_w_ff_out,
                ar_w_in=m_ar_w_in, ar_q_g=m_ar_q_g, ar_k_g=m_ar_k_g, ar_conv_w=m_ar_conv_w, ar_conv_b=m_ar_conv_b, ar_wa=m_ar_wa,
                ar_ba=m_ar_ba, ar_wx=m_ar_wx, ar_bx=m_ar_bx, ar_lambda=m_ar_lambda, ar_w_out=m_ar_w_out, gm_w_in=m_gm_w_in,
                gm_b_in=m_gm_b_in, gm_v_g=m_gm_v_g, gm_v_b=m_gm_v_b, gm_w_sp=m_gm_w_sp, gm_b_sp=m_gm_b_sp, gm_w_out=m_gm_w_out)
    v_in = dict(c_ctx=v_c_ctx, w_mod=v_w_mod, b_mod=v_b_mod, norm_g=v_norm_g, w_ff_in=v_w_ff_in, w_ff_out=v_w_ff_out,
                ar_w_in=v_ar_w_in, ar_q_g=v_ar_q_g, ar_k_g=v_ar_k_g, ar_conv_w=v_ar_conv_w, ar_conv_b=v_ar_conv_b, ar_wa=v_ar_wa,
                ar_ba=v_ar_ba, ar_wx=v_ar_wx, ar_bx=v_ar_bx, ar_lambda=v_ar_lambda, ar_w_out=v_ar_w_out, gm_w_in=v_gm_w_in,
                gm_b_in=v_gm_b_in, gm_v_g=v_gm_v_g, gm_v_b=v_gm_v_b, gm_w_sp=v_gm_w_sp, gm_b_sp=v_gm_b_sp, gm_w_out=v_gm_w_out)

    xi, yi, ci = lax.axis_index("x"), lax.axis_index("y"), lax.axis_index("c")
    chip = 2 * xi + yi
    dev = 4 * xi + 2 * yi + ci
    place = jnp.stack([chip, ci]).astype(jnp.int32)
    n_lat, d = x.shape[1], x.shape[2]
    d6 = 6 * d
    cols_mod = w_mod.shape[2]

    mine = [c, norm_g, ar_conv_w[0], ar_ba[0], ar_bx[0], ar_lambda[0], gm_b_in, gm_v_g, gm_v_b]
    gathered = _allgather8([_pack(mine)], "gather_small_params")[0]
    parts = _unpack_devices(gathered, [a.shape for a in mine])
    c_all = parts[0].reshape(N_DEV, d)
    sp = {'norm_g': _chip_concat(parts[1], 2), 'q_g': ar_q_g, 'k_g': ar_k_g, 'conv_w': _chip_concat(parts[2], 1),
          'conv_b': ar_conv_b, 'wa': ar_wa[0], 'ba': _chip_concat(parts[3], 1), 'wx': ar_wx[0], 'bx': _chip_concat(parts[4], 1),
          'lam': _chip_concat(parts[5], 1), 'gm_b_in': _chip_concat(parts[6], 1), 'gm_v_g': _chip_concat(parts[7], 1),
          'gm_v_b': _chip_concat(parts[8], 1), 'gm_w_sp': gm_w_sp[0], 'gm_b_sp': gm_b_sp[0]}

    def mod_operand(c_rows, cc):
        row = lax.broadcasted_iota(jnp.int32, (MOD_ROWS - N_DEV, d), 0)
        lower = jnp.where(row == 0, jnp.broadcast_to(_silu(cc), (MOD_ROWS - N_DEV, d)), 0.0)
        sig = _sigmoid(cc)
        return jnp.concatenate([_silu(c_rows), lower], axis=0), sig * (1.0 + cc * (1.0 - sig))

    s_mod, dsilu_ctx = _small(mod_operand, [((MOD_ROWS, d), F32), ((1, d), F32)], c_all, c_ctx[None, :], name="mod_operand")
    b_mod_mine = lax.dynamic_slice(b_mod, (0, chip * cols_mod), (2, cols_mod))
    mod = [_matmul(s_mod, w_mod[i], kind='nn', bias=b_mod_mine[i][None, :], out_dtype=F32, name=f"mod_fwd_{i}") for i in range(2)]
    mod_all = _allgather8([jnp.concatenate(mod, axis=0)], "gather_mod")[0]
    mod_all = _chip_concat(mod_all, 1).reshape(2, MOD_ROWS, d6)
    ml = [jnp.split(lax.dynamic_slice(mod_all[i], (dev, 0), (1, d6)), 6, axis=1) for i in range(2)]
    mc0 = jnp.split(mod_all[0, N_DEV:N_DEV + 1], 6, axis=1)[:2]

    names = ('w_ff_in', 'w_ff_out', 'ar_w_in', 'ar_w_out', 'gm_w_in', 'gm_w_out')
    full = _gather_weights([_cast_shard(weights[n], place, layer, f"cast_{n}_{layer}")
                            for n in names for layer in range(weights[n].shape[0])])
    big = {'ff_in': full[0:2], 'ff_out': full[2:4], 'ar_in': full[4], 'ar_out': full[5], 'gm_in': full[6], 'gm_out': full[7]}

    grad_x, bg, small = _local_step(x[0], ctx[0], loss_target[0], ml, mc0, sp, big)

    grads = bg['ff_in'] + bg['ff_out'] + [bg['ar_in'], bg['ar_out'], bg['gm_in'], bg['gm_out']]
    layout = [(0, 0), (0, 1), (1, 0), (1, 1), (2, 0), (3, 0), (4, 0), (5, 0)]
    reduced = dict(zip(names, _reduce_scatter(grads, layout, place)))

    small_list = [small[k] for k in SMALL_ORDER]
    small8 = _allgather8([_pack(small_list)], "gather_small_grads")[0]
    total = _unpack(_sum_devices(small8).reshape(-1), [a.shape for a in small_list])
    total = dict(zip(SMALL_ORDER, total))
    per_dev = _unpack_devices(small8, [(d6,), (d6,)])
    pad_rows = jnp.zeros((MOD_ROWS - N_DEV - 1, d6), F32)
    d_mod = [jnp.concatenate([per_dev[0], total['d_mc0'], pad_rows], axis=0),
             jnp.concatenate([per_dev[1], jnp.zeros((MOD_ROWS - N_DEV, d6), F32)], axis=0)]
    d_mod_mine = [lax.dynamic_slice(dm, (0, chip * cols_mod), (MOD_ROWS, cols_mod)) for dm in d_mod]
    g_w_mod = jnp.stack([_matmul(s_mod, d_mod_mine[i], kind='tn', out_dtype=F32, name=f"mod_dw_{i}") for i in range(2)])
    d_s_part = _matmul(d_mod_mine[0], w_mod[0], kind='nt', out_dtype=F32, name="mod_ds")
    d_s_all = _allgather8([d_s_part[N_DEV:]], "gather_mod_ds")[0]

    def c_ctx_grad(parts_, dsilu):
        acc = parts_[0, 0:1]
        for q in range(1, N_CHIPS):
            acc = acc + parts_[2 * q, 0:1]
        return (acc * dsilu,)

    g_c_ctx = _small(c_ctx_grad, [((1, d), F32)], d_s_all, dsilu_ctx, name="c_ctx_grad")[0].reshape(d)

    def mine_of(full_grad, axis, n_shard):
        return lax.dynamic_slice_in_dim(full_grad, chip * n_shard, n_shard, axis=axis)

    grads_out = {
        'c_ctx': g_c_ctx, 'w_mod': g_w_mod,
        'b_mod': jnp.stack([total['d_ml0'][0] + total['d_mc0'][0], total['d_ml1'][0]]),
        'norm_g': mine_of(total['norm_g'], 2, norm_g.shape[2]),
        'w_ff_in': reduced['w_ff_in'], 'w_ff_out': reduced['w_ff_out'], 'ar_w_in': reduced['ar_w_in'],
        'ar_q_g': total['q_g'], 'ar_k_g': total['k_g'], 'ar_conv_w': mine_of(total['conv_w'], 1, ar_conv_w.shape[2])[None],
        'ar_conv_b': total['conv_b'], 'ar_wa': total['wa'][None], 'ar_ba': mine_of(total['ba'], 1, ar_ba.shape[2])[None],
        'ar_wx': total['wx'][None], 'ar_bx': mine_of(total['bx'], 1, ar_bx.shape[2])[None],
        'ar_lambda': mine_of(total['lam'], 1, ar_lambda.shape[2])[None], 'ar_w_out': reduced['ar_w_out'],
        'gm_w_in': reduced['gm_w_in'], 'gm_b_in': mine_of(total['gm_b_in'], 1, gm_b_in.shape[1]),
        'gm_v_g': mine_of(total['gm_v_g'], 1, gm_v_g.shape[1]), 'gm_v_b': mine_of(total['gm_v_b'], 1, gm_v_b.shape[1]),
        'gm_w_sp': total['gm_w_sp'][None], 'gm_b_sp': total['gm_b_sp'][None], 'gm_w_out': reduced['gm_w_out'],
    }
    order = list(weights)
    stepped = [_adamw(weights[n], grads_out[n].reshape(weights[n].shape), m_in[n], v_in[n], f"adamw_{n}") for n in order]
    loss = total['loss'].reshape(())
    return (loss, grad_x[None], *[s[0] for s in stepped], *[s[1] for s in stepped], *[s[2] for s in stepped],
            *[s[3] for s in stepped])
```

```python
import functools
import math

import jax
import jax.numpy as jnp
from jax import lax
from jax.experimental import pallas as pl
from jax.experimental.pallas import tpu as pltpu

F32 = jnp.float32
BF16 = jnp.bfloat16
MESH = pl.DeviceIdType.MESH
ANY = pl.BlockSpec(memory_space=pl.ANY)

VMEM_LIMIT_BYTES = 52 * 1024 * 1024
LANES = 128
N_CHIPS = 4
N_DEV = 8

HEAD_DIM = 128
N_HEADS = 8
N_KV = 2
GROUP = N_HEADS // N_KV
ATTN_W = N_HEADS * HEAD_DIM
KV_W = N_KV * HEAD_DIM
D_RNN = 1024
RNN_BLOCKS = 8
RNN_BW = D_RNN // RNN_BLOCKS
CONV_W = 4
RG_C = 8.0
GRID_W = 64
ROPE_THETA = 10000.0
ROPE_PAIRS = HEAD_DIM // 4
GM_GROUPS = 16
CHUNK = 128
EPS = 1e-6
ADAM_LR, ADAM_B1, ADAM_B2, ADAM_EPS, ADAM_WD, ADAM_STEP = 0.001, 0.9, 0.999, 1e-08, 0.01, 10
GELU_C = math.sqrt(2.0 / math.pi)


def _params(sem=None):
    return pltpu.CompilerParams(dimension_semantics=sem, vmem_limit_bytes=VMEM_LIMIT_BYTES)


def _tile(dim, pref, unit):
    best = None
    t = unit
    while t <= min(dim, pref):
        if dim % t == 0:
            best = t
        t += unit
    return best if best is not None else dim


def _full(shape):
    nd = len(shape)
    return pl.BlockSpec(shape, lambda *_: (0,) * nd)


def _blocked_map(split, per_q):
    if split == 'n':
        return lambda r, c: (c // per_q, r, c % per_q)
    if split == 'k':
        return lambda r, c: (r // per_q, r % per_q, c)
    return lambda r, c: (r, c)


def _logical_shape(arr, split):
    if split == 'n':
        return arr.shape[1], arr.shape[0] * arr.shape[2]
    if split == 'k':
        return arr.shape[0] * arr.shape[1], arr.shape[2]
    return arr.shape


def _matmul(a, b, *, kind, name, out_dtype, b_split=None, out_split=None, bias=None, epilogue=None, extra=None,
            pref=(1024, 1024, 1024)):
    b_rows, b_cols = _logical_shape(b, b_split)
    if kind == 'nn':
        m, kc = a.shape
        n = b_cols
        assert b_rows == kc
    elif kind == 'nt':
        m, kc = a.shape
        n = b_rows
        assert b_cols == kc
    else:
        kc, m = a.shape
        n = b_cols
        assert b_rows == kc
    b_row_ext = b.shape[1] if b_split == 'k' else b_rows
    b_col_ext = b.shape[2] if b_split == 'n' else b_cols
    out_row_ext = m // N_CHIPS if out_split == 'k' else m
    out_col_ext = n // N_CHIPS if out_split == 'n' else n
    if kind == 'nn':
        ti = _tile(min(m, out_row_ext), pref[0], 16)
        tj = _tile(math.gcd(b_col_ext, out_col_ext), pref[1], LANES)
        tl = _tile(b_row_ext, pref[2], LANES)
        a_spec = pl.BlockSpec((ti, tl), lambda i, j, l: (i, l))
        b_tile, b_rc = (tl, tj), (lambda i, j, l: (l, j))
        dims = (((1,), (0,)), ((), ()))
    elif kind == 'nt':
        ti = _tile(min(m, out_row_ext), pref[0], 16)
        tj = _tile(math.gcd(b_row_ext, out_col_ext), pref[1], LANES)
        tl = _tile(b_col_ext, pref[2], LANES)
        a_spec = pl.BlockSpec((ti, tl), lambda i, j, l: (i, l))
        b_tile, b_rc = (tj, tl), (lambda i, j, l: (j, l))
        dims = (((1,), (1,)), ((), ()))
    else:
        ti = _tile(out_row_ext, pref[0], LANES)
        tj = _tile(math.gcd(b_col_ext, out_col_ext), pref[1], LANES)
        tl = _tile(b_row_ext, pref[2], 16)
        a_spec = pl.BlockSpec((tl, ti), lambda i, j, l: (l, i))
        b_tile, b_rc = (tl, tj), (lambda i, j, l: (l, j))
        dims = (((0,), (0,)), ((), ()))
    grid = (m // ti, n // tj, kc // tl)
    n_l = grid[2]

    if b_split is None:
        b_spec = pl.BlockSpec(b_tile, b_rc)
    else:
        per_q = (b.shape[2] // b_tile[1]) if b_split == 'n' else (b.shape[1] // b_tile[0])
        bmap = _blocked_map(b_split, per_q)
        b_spec = pl.BlockSpec((None,) + b_tile, lambda i, j, l: bmap(*b_rc(i, j, l)))
    if out_split is None:
        out_shape2 = (m, n)
        o_spec = pl.BlockSpec((ti, tj), lambda i, j, l: (i, j))
    else:
        out_shape2 = (N_CHIPS, m // N_CHIPS, n) if out_split == 'k' else (N_CHIPS, m, n // N_CHIPS)
        per_q = (out_shape2[2] // tj) if out_split == 'n' else (out_shape2[1] // ti)
        omap = _blocked_map(out_split, per_q)
        o_spec = pl.BlockSpec((None, ti, tj), lambda i, j, l: omap(i, j))

    in_specs = [a_spec, b_spec]
    operands = [a, b]
    if bias is not None:
        in_specs.append(pl.BlockSpec((1, tj), lambda i, j, l: (0, j)))
        operands.append(bias)
    if extra is not None:
        in_specs.append(pl.BlockSpec((ti, tj), lambda i, j, l: (i, j)))
        operands.append(extra)
    if epilogue == 'relu2':
        out_shape = (jax.ShapeDtypeStruct(out_shape2, out_dtype), jax.ShapeDtypeStruct(out_shape2, out_dtype))
        out_specs = (o_spec, o_spec)
    else:
        out_shape = jax.ShapeDtypeStruct(out_shape2, out_dtype)
        out_specs = o_spec
    has_bias, has_extra = bias is not None, extra is not None

    def body(*refs):
        a_ref, b_ref = refs[0], refs[1]
        pos = 2
        bias_ref = extra_ref = None
        if has_bias:
            bias_ref = refs[pos]
            pos += 1
        if has_extra:
            extra_ref = refs[pos]
            pos += 1
        outs = refs[pos:-1]
        acc_ref = refs[-1]
        step = pl.program_id(2)

        @pl.when(step == 0)
        def _():
            acc_ref[...] = jnp.zeros_like(acc_ref)

        acc_ref[...] += lax.dot_general(a_ref[...].astype(BF16), b_ref[...].astype(BF16), dims,
                                        preferred_element_type=F32)

        @pl.when(step == n_l - 1)
        def _():
            acc = acc_ref[...]
            if has_bias:
                acc = acc + bias_ref[...]
            if epilogue == 'relu2':
                r = jnp.maximum(acc, 0.0)
                outs[0][...] = r.astype(outs[0].dtype)
                outs[1][...] = (r * r).astype(outs[1].dtype)
            elif epilogue == 'times2x':
                outs[0][...] = (acc * (2.0 * extra_ref[...].astype(F32))).astype(outs[0].dtype)
            else:
                outs[0][...] = acc.astype(outs[0].dtype)

    return pl.pallas_call(
        body, name=name, grid=grid, in_specs=in_specs, out_specs=out_specs, out_shape=out_shape,
        scratch_shapes=[pltpu.VMEM((ti, tj), F32)],
        compiler_params=_params(("parallel", "parallel", "arbitrary")),
    )(*operands)


def _small(fn, out_shapes, *arrays, name):
    n_in = len(arrays)

    def body(*refs):
        res = fn(*[r[...] for r in refs[:n_in]])
        for o_ref, v in zip(refs[n_in:], res):
            o_ref[...] = v.astype(o_ref.dtype)

    return pl.pallas_call(
        body, name=name, out_shape=tuple(jax.ShapeDtypeStruct(s, d) for s, d in out_shapes),
        in_specs=[_full(a.shape) for a in arrays], out_specs=tuple(_full(s) for s, _ in out_shapes), grid=(1,),
        compiler_params=_params(("arbitrary",)),
    )(*arrays)


def _rows_tile(rows, cols, itemsize=4, budget=2 * 1024 * 1024):
    return _tile(rows, max(16, budget // (cols * itemsize)), 16)


def _rowwise(fn, out_dtypes, *arrays, name):
    rows, cols = arrays[0].shape
    tr = _rows_tile(rows, cols)
    n_in = len(arrays)

    def body(*refs):
        res = fn(*[r[...] for r in refs[:n_in]])
        for o_ref, v in zip(refs[n_in:], res):
            o_ref[...] = v.astype(o_ref.dtype)

    spec = pl.BlockSpec((tr, cols), lambda i: (i, 0))
    return pl.pallas_call(
        body, name=name, grid=(rows // tr,), in_specs=[spec] * n_in, out_specs=tuple(spec for _ in out_dtypes),
        out_shape=tuple(jax.ShapeDtypeStruct((rows, cols), d) for d in out_dtypes),
        compiler_params=_params(("parallel",)),
    )(*arrays)


def _as2d(a):
    return a.reshape(1, a.size) if a.ndim < 2 else a.reshape(-1, a.shape[-1])


def _cast_shard(w, place, layer, name):
    _, rows, cols = w.shape
    tr = _rows_tile(rows, cols)

    def body(place_ref, w_ref, o_ref):
        o_ref[...] = w_ref[...].astype(o_ref.dtype)

    return pl.pallas_call(
        body, name=name, out_shape=jax.ShapeDtypeStruct((N_CHIPS, rows, cols), BF16),
        grid_spec=pltpu.PrefetchScalarGridSpec(
            num_scalar_prefetch=1, grid=(rows // tr,),
            in_specs=[pl.BlockSpec((None, tr, cols), lambda i, pr: (layer, i, 0))],
            out_specs=pl.BlockSpec((None, tr, cols), lambda i, pr: (pr[0], i, 0))),
        compiler_params=_params(("parallel",)),
    )(place, w)


def _norm_fwd(x, g, a, b=None, res=None, *, out_dtype, name):
    rows, d = x.shape
    tr = _rows_tile(rows, d)
    has_b, has_res = b is not None, res is not None

    def body(*refs):
        x_ref, g_ref, a_ref = refs[:3]
        pos = 3
        xv = x_ref[...]
        rstd = lax.rsqrt(jnp.mean(xv * xv, axis=-1, keepdims=True) + EPS)
        y = (xv * rstd * g_ref[...]) * a_ref[...]
        if has_b:
            y = y + refs[pos][...]
            pos += 1
        if has_res:
            y = y + refs[pos][...]
            pos += 1
        refs[pos][...] = y.astype(refs[pos].dtype)

    row = pl.BlockSpec((tr, d), lambda i: (i, 0))
    vec = pl.BlockSpec((1, d), lambda i: (0, 0))
    operands, specs = [x, g, a], [row, vec, vec]
    if has_b:
        operands.append(b)
        specs.append(vec)
    if has_res:
        operands.append(res)
        specs.append(row)
    return pl.pallas_call(
        body, name=name, grid=(rows // tr,), in_specs=specs, out_specs=row,
        out_shape=jax.ShapeDtypeStruct((rows, d), out_dtype), compiler_params=_params(("parallel",)),
    )(*operands)


def _norm_bwd(dy, x, g, a, extra=None, *, out_dtype, name):
    rows, d = x.shape
    tr = _rows_tile(rows, d)
    has_extra = extra is not None

    def body(*refs):
        dy_ref, x_ref, g_ref, a_ref = refs[:4]
        pos = 4
        extra_ref = None
        if has_extra:
            extra_ref = refs[pos]
            pos += 1
        dx_ref, dg_ref, da_ref, db_ref = refs[pos:pos + 4]

        @pl.when(pl.program_id(0) == 0)
        def _():
            dg_ref[...] = jnp.zeros_like(dg_ref)
            da_ref[...] = jnp.zeros_like(da_ref)
            db_ref[...] = jnp.zeros_like(db_ref)

        xv = x_ref[...]
        dyv = dy_ref[...].astype(F32)
        rstd = lax.rsqrt(jnp.mean(xv * xv, axis=-1, keepdims=True) + EPS)
        nrm = xv * rstd
        gv = g_ref[...]
        da_ref[...] += jnp.sum(dyv * (nrm * gv), axis=0, keepdims=True)
        db_ref[...] += jnp.sum(dyv, axis=0, keepdims=True)
        dt = dyv * a_ref[...]
        dg_ref[...] += jnp.sum(dt * nrm, axis=0, keepdims=True)
        dn = dt * gv
        dx = rstd * (dn - nrm * jnp.mean(dn * nrm, axis=-1, keepdims=True))
        if has_extra:
            dx = dx + extra_ref[...]
        dx_ref[...] = dx.astype(dx_ref.dtype)

    row = pl.BlockSpec((tr, d), lambda i: (i, 0))
    vec = pl.BlockSpec((1, d), lambda i: (0, 0))
    operands, specs = [dy, x, g, a], [row, row, vec, vec]
    if has_extra:
        operands.append(extra)
        specs.append(row)
    vshape = jax.ShapeDtypeStruct((1, d), F32)
    return pl.pallas_call(
        body, name=name, grid=(rows // tr,), in_specs=specs, out_specs=(row, vec, vec, vec),
        out_shape=(jax.ShapeDtypeStruct((rows, d), out_dtype), vshape, vshape, vshape),
        compiler_params=_params(("arbitrary",)),
    )(*operands)


def _loss_head(y, target):
    rows, d = y.shape
    tr = _rows_tile(rows, d)

    def body(y_ref, t_ref, dy_ref, loss_ref):
        @pl.when(pl.program_id(0) == 0)
        def _():
            loss_ref[...] = jnp.zeros_like(loss_ref)

        err = y_ref[...] - t_ref[...]
        dy_ref[...] = err * (1.0 / d)
        loss_ref[...] += jnp.sum(jnp.sum(err * err, axis=-1, keepdims=True), axis=0, keepdims=True) * (0.5 / d)

    row = pl.BlockSpec((tr, d), lambda i: (i, 0))
    return pl.pallas_call(
        body, name="loss_head", grid=(rows // tr,), in_specs=[row, row], out_specs=(row, _full((1, 1))),
        out_shape=(jax.ShapeDtypeStruct((rows, d), F32), jax.ShapeDtypeStruct((1, 1), F32)),
        compiler_params=_params(("arbitrary",)),
    )(y, target)


def _rope_partner(v):
    lane = lax.broadcasted_iota(jnp.int32, v.shape, 1)
    up = pltpu.roll(v, HEAD_DIM - ROPE_PAIRS, 1)
    down = pltpu.roll(v, ROPE_PAIRS, 1)
    return jnp.where((lane % (2 * ROPE_PAIRS)) < ROPE_PAIRS, up, down)


def _qk_fwd(proj, q_g, k_g, cos, sin, *, name):
    rows = proj.shape[0]
    tr = _tile(rows, 256, 16)
    width = ATTN_W + 2 * KV_W

    def body(p_ref, qg_ref, kg_ref, cos_ref, sin_ref, q_ref, k_ref, v_ref):
        cosv, sinv = cos_ref[...], sin_ref[...]
        for h in range(N_HEADS + N_KV):
            xv = p_ref[:, h * HEAD_DIM:(h + 1) * HEAD_DIM]
            gain = qg_ref[...] if h < N_HEADS else kg_ref[...]
            t = xv * lax.rsqrt(jnp.mean(xv * xv, axis=-1, keepdims=True) + EPS) * gain
            y = t * cosv + _rope_partner(t) * sinv
            if h < N_HEADS:
                q_ref[:, h * HEAD_DIM:(h + 1) * HEAD_DIM] = y.astype(BF16)
            else:
                k_ref[:, (h - N_HEADS) * HEAD_DIM:(h - N_HEADS + 1) * HEAD_DIM] = y.astype(BF16)
        v_ref[...] = p_ref[:, ATTN_W + KV_W:width].astype(BF16)

    vec = _full((1, HEAD_DIM))
    tab = pl.BlockSpec((tr, HEAD_DIM), lambda i: (i, 0))
    return pl.pallas_call(
        body, name=name, grid=(rows // tr,),
        in_specs=[pl.BlockSpec((tr, width), lambda i: (i, 0)), vec, vec, tab, tab],
        out_specs=(pl.BlockSpec((tr, ATTN_W), lambda i: (i, 0)), pl.BlockSpec((tr, KV_W), lambda i: (i, 0)),
                   pl.BlockSpec((tr, KV_W), lambda i: (i, 0))),
        out_shape=(jax.ShapeDtypeStruct((rows, ATTN_W), BF16), jax.ShapeDtypeStruct((rows, KV_W), BF16),
                   jax.ShapeDtypeStruct((rows, KV_W), BF16)),
        compiler_params=_params(("parallel",)),
    )(proj, q_g, k_g, cos, sin)


def _qk_bwd(dq, dk, proj, q_g, k_g, cos, sin, *, name):
    rows = proj.shape[0]
    tr = _tile(rows, 256, 16)
    width = ATTN_W + KV_W
    has_q = dq is not None

    def body(*refs):
        pos = 0
        dq_ref = None
        if has_q:
            dq_ref = refs[0]
            pos = 1
        dk_ref, p_ref, qg_ref, kg_ref, cos_ref, sin_ref, dp_ref, dqg_ref, dkg_ref = refs[pos:pos + 9]

        @pl.when(pl.program_id(0) == 0)
        def _():
            dqg_ref[...] = jnp.zeros_like(dqg_ref)
            dkg_ref[...] = jnp.zeros_like(dkg_ref)

        cosv, sinv = cos_ref[...], sin_ref[...]
        for h in range(N_HEADS + N_KV):
            cols = slice(h * HEAD_DIM, (h + 1) * HEAD_DIM)
            if h < N_HEADS and not has_q:
                dp_ref[:, cols] = jnp.zeros((tr, HEAD_DIM), dp_ref.dtype)
                continue
            if h < N_HEADS:
                dyv, gain, dgain_ref = dq_ref[:, cols], qg_ref[...], dqg_ref
            else:
                hk = h - N_HEADS
                dyv, gain, dgain_ref = dk_ref[:, hk * HEAD_DIM:(hk + 1) * HEAD_DIM], kg_ref[...], dkg_ref
            dyv = dyv.astype(F32)
            dt = dyv * cosv + _rope_partner(dyv * sinv)
            xv = p_ref[:, cols]
            rstd = lax.rsqrt(jnp.mean(xv * xv, axis=-1, keepdims=True) + EPS)
            nrm = xv * rstd
            dgain_ref[...] += jnp.sum(dt * nrm, axis=0, keepdims=True)
            dn = dt * gain
            dp_ref[:, cols] = (rstd * (dn - nrm * jnp.mean(dn * nrm, axis=-1, keepdims=True))).astype(dp_ref.dtype)

    vec = _full((1, HEAD_DIM))
    tab = pl.BlockSpec((tr, HEAD_DIM), lambda i: (i, 0))
    operands = ([dq] if has_q else []) + [dk, proj, q_g, k_g, cos, sin]
    specs = ([pl.BlockSpec((tr, ATTN_W), lambda i: (i, 0))] if has_q else []) + [
        pl.BlockSpec((tr, KV_W), lambda i: (i, 0)), pl.BlockSpec((tr, width), lambda i: (i, 0)), vec, vec, tab, tab]
    return pl.pallas_call(
        body, name=name, grid=(rows // tr,), in_specs=specs,
        out_specs=(pl.BlockSpec((tr, width), lambda i: (i, 0)), vec, vec),
        out_shape=(jax.ShapeDtypeStruct((rows, width), BF16), jax.ShapeDtypeStruct((1, HEAD_DIM), F32),
                   jax.ShapeDtypeStruct((1, HEAD_DIM), F32)),
        compiler_params=_params(("arbitrary",)),
    )(*operands)


def _attn_fwd(q, k, v):
    n_q, n_k = q.shape[0], k.shape[0]
    tq = _tile(n_q, 256, 16)
    gw = GROUP * HEAD_DIM
    scale = HEAD_DIM ** -0.5

    def body(q_ref, k_ref, v_ref, o_ref, lse_ref):
        kv, vv = k_ref[...], v_ref[...]
        for g in range(GROUP):
            cols = slice(g * HEAD_DIM, (g + 1) * HEAD_DIM)
            s = lax.dot_general(q_ref[:, cols], kv, (((1,), (1,)), ((), ())), preferred_element_type=F32) * scale
            m = jnp.max(s, axis=-1, keepdims=True)
            p = jnp.exp(s - m)
            l = jnp.sum(p, axis=-1, keepdims=True)
            o = jnp.dot(p.astype(BF16), vv, preferred_element_type=F32) / l
            o_ref[:, cols] = o.astype(o_ref.dtype)
            lse_ref[:, g:g + 1] = m + jnp.log(l)

    return pl.pallas_call(
        body, name="attn_fwd", grid=(N_KV, n_q // tq),
        in_specs=[pl.BlockSpec((tq, gw), lambda h, i: (i, h)), pl.BlockSpec((n_k, HEAD_DIM), lambda h, i: (0, h)),
                  pl.BlockSpec((n_k, HEAD_DIM), lambda h, i: (0, h))],
        out_specs=(pl.BlockSpec((tq, gw), lambda h, i: (i, h)), pl.BlockSpec((None, tq, GROUP), lambda h, i: (h, i, 0))),
        out_shape=(jax.ShapeDtypeStruct((n_q, ATTN_W), BF16), jax.ShapeDtypeStruct((N_KV, n_q, GROUP), F32)),
        compiler_params=_params(("parallel", "parallel")),
    )(q, k, v)


def _attn_bwd(q, k, v, o, lse, do):
    n_q, n_k = q.shape[0], k.shape[0]
    tq = _tile(n_q, 256, 16)
    gw = GROUP * HEAD_DIM
    scale = HEAD_DIM ** -0.5

    def body(q_ref, k_ref, v_ref, o_ref, lse_ref, do_ref, dq_ref, dk_ref, dv_ref):
        @pl.when(pl.program_id(1) == 0)
        def _():
            dk_ref[...] = jnp.zeros_like(dk_ref)
            dv_ref[...] = jnp.zeros_like(dv_ref)

        kv, vv = k_ref[...], v_ref[...]
        for g in range(GROUP):
            cols = slice(g * HEAD_DIM, (g + 1) * HEAD_DIM)
            qg = q_ref[:, cols]
            dof = do_ref[:, cols].astype(F32)
            dog = dof.astype(BF16)
            s = lax.dot_general(qg, kv, (((1,), (1,)), ((), ())), preferred_element_type=F32) * scale
            p = jnp.exp(s - lse_ref[:, g:g + 1])
            delta = jnp.sum(dof * o_ref[:, cols].astype(F32), axis=-1, keepdims=True)
            dp = lax.dot_general(dog, vv, (((1,), (1,)), ((), ())), preferred_element_type=F32)
            ds = (p * (dp - delta) * scale).astype(BF16)
            pb = p.astype(BF16)
            dq_ref[:, cols] = jnp.dot(ds, kv, preferred_element_type=F32)
            dk_ref[...] += lax.dot_general(ds, qg, (((0,), (0,)), ((), ())), preferred_element_type=F32)
            dv_ref[...] += lax.dot_general(pb, dog, (((0,), (0,)), ((), ())), preferred_element_type=F32)

    qspec = pl.BlockSpec((tq, gw), lambda h, i: (i, h))
    kspec = pl.BlockSpec((n_k, HEAD_DIM), lambda h, i: (0, h))
    return pl.pallas_call(
        body, name="attn_bwd", grid=(N_KV, n_q // tq),
        in_specs=[qspec, kspec, kspec, qspec, pl.BlockSpec((None, tq, GROUP), lambda h, i: (h, i, 0)), qspec],
        out_specs=(qspec, kspec, kspec),
        out_shape=(jax.ShapeDtypeStruct((n_q, ATTN_W), F32), jax.ShapeDtypeStruct((n_k, KV_W), F32),
                   jax.ShapeDtypeStruct((n_k, KV_W), F32)),
        compiler_params=_params(("parallel", "arbitrary")),
    )(q, k, v, o, lse, do)


CONV_COLS = 256
XR_COL0 = ATTN_W + 2 * KV_W


def _shift_rows(v, off):
    if off == 0:
        return v
    n = v.shape[0]
    rolled = pltpu.roll(v, (-off) % n, 0)
    t = lax.broadcasted_iota(jnp.int32, v.shape, 0)
    keep = (t + off >= 0) & (t + off < n)
    return jnp.where(keep, rolled, 0.0)


def _conv_fwd(proj, w, b, *, name):
    rows = proj.shape[0]
    blk0 = XR_COL0 // CONV_COLS

    def body(x_ref, w_ref, b_ref, y_ref):
        xv = x_ref[...]
        y = b_ref[...] + jnp.zeros_like(xv)
        for j in range(CONV_W):
            y = y + _shift_rows(xv, j - CONV_W // 2) * w_ref[j:j + 1, :]
        y_ref[...] = y

    return pl.pallas_call(
        body, name=name, grid=(D_RNN // CONV_COLS,),
        in_specs=[pl.BlockSpec((rows, CONV_COLS), lambda i: (0, blk0 + i)),
                  pl.BlockSpec((CONV_W, CONV_COLS), lambda i: (0, i)), pl.BlockSpec((1, CONV_COLS), lambda i: (0, i))],
        out_specs=pl.BlockSpec((rows, CONV_COLS), lambda i: (0, i)),
        out_shape=jax.ShapeDtypeStruct((rows, D_RNN), F32), compiler_params=_params(("parallel",)),
    )(proj, w, b)


def _conv_bwd(d1, d2, proj, w, *, name):
    rows = proj.shape[0]
    blk0 = XR_COL0 // CONV_COLS

    def body(d1_ref, d2_ref, x_ref, w_ref, dx_ref, dw_ref, db_ref):
        dv = d1_ref[...] + d2_ref[...]
        xv = x_ref[...]
        dx = jnp.zeros_like(dv)
        for j in range(CONV_W):
            off = j - CONV_W // 2
            dx = dx + _shift_rows(dv, -off) * w_ref[j:j + 1, :]
            dw_ref[j:j + 1, :] = jnp.sum(dv * _shift_rows(xv, off), axis=0, keepdims=True)
        dx_ref[...] = dx.astype(dx_ref.dtype)
        db_ref[...] = jnp.sum(dv, axis=0, keepdims=True)

    col = pl.BlockSpec((rows, CONV_COLS), lambda i: (0, i))
    return pl.pallas_call(
        body, name=name, grid=(D_RNN // CONV_COLS,),
        in_specs=[col, col, pl.BlockSpec((rows, CONV_COLS), lambda i: (0, blk0 + i)),
                  pl.BlockSpec((CONV_W, CONV_COLS), lambda i: (0, i))],
        out_specs=(col, pl.BlockSpec((CONV_W, CONV_COLS), lambda i: (0, i)), pl.BlockSpec((1, CONV_COLS), lambda i: (0, i))),
        out_shape=(jax.ShapeDtypeStruct((rows, D_RNN), BF16), jax.ShapeDtypeStruct((CONV_W, D_RNN), F32),
                   jax.ShapeDtypeStruct((1, D_RNN), F32)),
        compiler_params=_params(("parallel",)),
    )(d1, d2, proj, w)


RNN_TB = 256
SCAN_ROWS = 8


def _sigmoid(z):
    return 1.0 / (1.0 + jnp.exp(-z))


def _softplus(z):
    return jnp.maximum(z, 0.0) + jnp.log(1.0 + jnp.exp(-jnp.abs(z)))


def _one_minus_exp(y):
    series = -y * (1.0 + y * (0.5 + y * (1.0 / 6.0 + y * (1.0 / 24.0))))
    return jnp.where(y > -0.03, series, 1.0 - jnp.exp(y))


def _rglru_gates(xv, wa_ref, ba_ref, wx_ref, bx_ref, lam_ref):
    xb = xv.astype(BF16)
    zr = jnp.concatenate([jnp.dot(xb[:, n * RNN_BW:(n + 1) * RNN_BW], wa_ref[n].astype(BF16),
                                  preferred_element_type=F32) for n in range(RNN_BLOCKS)], axis=-1) + ba_ref[...]
    zi = jnp.concatenate([jnp.dot(xb[:, n * RNN_BW:(n + 1) * RNN_BW], wx_ref[n].astype(BF16),
                                  preferred_element_type=F32) for n in range(RNN_BLOCKS)], axis=-1) + bx_ref[...]
    r = _sigmoid(zr)
    gi = _sigmoid(zi)
    sp = _softplus(-lam_ref[...])
    log_a = -RG_C * r * sp
    a = jnp.exp(log_a)
    s = jnp.sqrt(_one_minus_exp(2.0 * log_a))
    return r, gi, sp, a, s


def _scan_rows(n_rows, reverse, step_fn, carry):
    groups = n_rows // SCAN_ROWS

    def trip(gidx, carry):
        gi = (groups - 1 - gidx) if reverse else gidx
        base = pl.multiple_of(gi * SCAN_ROWS, SCAN_ROWS)
        return step_fn(base, carry)

    return lax.fori_loop(0, groups, trip, carry)


def _rglru_fwd(xs, wa, ba, wx, bx, lam, *, reverse, name):
    rows = xs.shape[0]
    tb = _tile(rows, RNN_TB, SCAN_ROWS)
    nb = rows // tb
    order = (lambda i: (nb - 1 - i, 0)) if reverse else (lambda i: (i, 0))

    def body(x_ref, wa_ref, ba_ref, wx_ref, bx_ref, lam_ref, h_ref, hp_ref, a_s, b_s, state):
        @pl.when(pl.program_id(0) == 0)
        def _():
            state[...] = jnp.zeros_like(state)

        xv = x_ref[...]
        _, gi, _, a, s = _rglru_gates(xv, wa_ref, ba_ref, wx_ref, bx_ref, lam_ref)
        a_s[...] = a
        b_s[...] = s * (gi * xv)

        def group(base, h):
            av = a_s[pl.ds(base, SCAN_ROWS), :]
            bv = b_s[pl.ds(base, SCAN_ROWS), :]
            outs, prevs = [None] * SCAN_ROWS, [None] * SCAN_ROWS
            for k in range(SCAN_ROWS):
                r_ = SCAN_ROWS - 1 - k if reverse else k
                prevs[r_] = h
                h = av[r_:r_ + 1, :] * h + bv[r_:r_ + 1, :]
                outs[r_] = h
            h_ref[pl.ds(base, SCAN_ROWS), :] = jnp.concatenate(outs, axis=0)
            hp_ref[pl.ds(base, SCAN_ROWS), :] = jnp.concatenate(prevs, axis=0)
            return h

        state[0:1, :] = _scan_rows(tb, reverse, group, state[0:1, :])

    blk = pl.BlockSpec((tb, D_RNN), order)
    wspec = _full((RNN_BLOCKS, RNN_BW, RNN_BW))
    vec = _full((1, D_RNN))
    return pl.pallas_call(
        body, name=name, grid=(nb,), in_specs=[blk, wspec, vec, wspec, vec, vec], out_specs=(blk, blk),
        out_shape=(jax.ShapeDtypeStruct((rows, D_RNN), F32), jax.ShapeDtypeStruct((rows, D_RNN), F32)),
        scratch_shapes=[pltpu.VMEM((tb, D_RNN), F32), pltpu.VMEM((tb, D_RNN), F32), pltpu.VMEM((SCAN_ROWS, D_RNN), F32)],
        compiler_params=_params(("arbitrary",)),
    )(xs, wa, ba, wx, bx, lam)


def _rglru_bwd(xs, h_prev, dh, wa, ba, wx, bx, lam, *, reverse, name):
    rows = xs.shape[0]
    tb = _tile(rows, RNN_TB, SCAN_ROWS)
    nb = rows // tb
    back = not reverse
    order = (lambda i: (nb - 1 - i, 0)) if back else (lambda i: (i, 0))

    def body(x_ref, hp_ref, dh_ref, wa_ref, ba_ref, wx_ref, bx_ref, lam_ref,
             dx_ref, dwa_ref, dba_ref, dwx_ref, dbx_ref, dlam_ref, a_s, g_s, state):
        @pl.when(pl.program_id(0) == 0)
        def _():
            state[...] = jnp.zeros_like(state)
            dwa_ref[...] = jnp.zeros_like(dwa_ref)
            dwx_ref[...] = jnp.zeros_like(dwx_ref)
            dba_ref[...] = jnp.zeros_like(dba_ref)
            dbx_ref[...] = jnp.zeros_like(dbx_ref)
            dlam_ref[...] = jnp.zeros_like(dlam_ref)

        xv = x_ref[...]
        r, gi, sp, a, s = _rglru_gates(xv, wa_ref, ba_ref, wx_ref, bx_ref, lam_ref)
        a_s[...] = a

        def group(base, carry):
            av = a_s[pl.ds(base, SCAN_ROWS), :]
            dv = dh_ref[pl.ds(base, SCAN_ROWS), :]
            outs = [None] * SCAN_ROWS
            for k in range(SCAN_ROWS):
                r_ = SCAN_ROWS - 1 - k if back else k
                gt = dv[r_:r_ + 1, :] + carry
                outs[r_] = gt
                carry = av[r_:r_ + 1, :] * gt
            g_s[pl.ds(base, SCAN_ROWS), :] = jnp.concatenate(outs, axis=0)
            return carry

        state[0:1, :] = _scan_rows(tb, back, group, state[0:1, :])

        gv = g_s[...]
        d_a = gv * hp_ref[...]
        d_s = gv * (gi * xv)
        d_gi = gv * (s * xv)
        dx = gv * (s * gi)
        d_log_a = d_a * a - d_s * (a * a) / s
        d_r = d_log_a * (-RG_C * sp)
        lamv = lam_ref[...]
        d_sp = jnp.sum(d_log_a * (-RG_C * r), axis=0, keepdims=True)
        dlam_ref[...] += d_sp * (-_sigmoid(-lamv))
        d_zr = d_r * r * (1.0 - r)
        d_zi = d_gi * gi * (1.0 - gi)
        dba_ref[...] += jnp.sum(d_zr, axis=0, keepdims=True)
        dbx_ref[...] += jnp.sum(d_zi, axis=0, keepdims=True)
        xb = xv.astype(BF16)
        zrb, zib = d_zr.astype(BF16), d_zi.astype(BF16)
        parts = []
        for n in range(RNN_BLOCKS):
            cols = slice(n * RNN_BW, (n + 1) * RNN_BW)
            dwa_ref[n] += lax.dot_general(xb[:, cols], zrb[:, cols], (((0,), (0,)), ((), ())), preferred_element_type=F32)
            dwx_ref[n] += lax.dot_general(xb[:, cols], zib[:, cols], (((0,), (0,)), ((), ())), preferred_element_type=F32)
            parts.append(
                lax.dot_general(zrb[:, cols], wa_ref[n].astype(BF16), (((1,), (1,)), ((), ())), preferred_element_type=F32)
                + lax.dot_general(zib[:, cols], wx_ref[n].astype(BF16), (((1,), (1,)), ((), ())), preferred_element_type=F32))
        dx_ref[...] = dx + jnp.concatenate(parts, axis=-1)

    blk = pl.BlockSpec((tb, D_RNN), order)
    wspec = _full((RNN_BLOCKS, RNN_BW, RNN_BW))
    vec = _full((1, D_RNN))
    wshape = jax.ShapeDtypeStruct((RNN_BLOCKS, RNN_BW, RNN_BW), F32)
    vshape = jax.ShapeDtypeStruct((1, D_RNN), F32)
    return pl.pallas_call(
        body, name=name, grid=(nb,), in_specs=[blk, blk, blk, wspec, vec, wspec, vec, vec],
        out_specs=(blk, wspec, vec, wspec, vec, vec),
        out_shape=(jax.ShapeDtypeStruct((rows, D_RNN), F32), wshape, vshape, wshape, vshape, vshape),
        scratch_shapes=[pltpu.VMEM((tb, D_RNN), F32), pltpu.VMEM((tb, D_RNN), F32), pltpu.VMEM((SCAN_ROWS, D_RNN), F32)],
        compiler_params=_params(("arbitrary",)),
    )(xs, h_prev, dh, wa, ba, wx, bx, lam)


def _gelu(z):
    return 0.5 * z * (1.0 + jnp.tanh(GELU_C * (z + 0.044715 * z * z * z)))


def _gelu_grad(z):
    t = jnp.tanh(GELU_C * (z + 0.044715 * z * z * z))
    return 0.5 * (1.0 + t) + 0.5 * z * (1.0 - t * t) * (GELU_C * (1.0 + 3.0 * 0.044715 * z * z))


GATE_COL0 = XR_COL0 + D_RNN


RNN_OUT_COLS = 512


def _rnn_out_specs(rows, hf_off, hb_off):
    tr = _tile(rows, 256, 16)
    assert hf_off % tr == 0 and hb_off % tr == 0 and GATE_COL0 % RNN_OUT_COLS == 0
    fo, bo, go = hf_off // tr, hb_off // tr, GATE_COL0 // RNN_OUT_COLS
    hf_spec = pl.BlockSpec((tr, RNN_OUT_COLS), lambda i, j: (i + fo, j))
    hb_spec = pl.BlockSpec((tr, RNN_OUT_COLS), lambda i, j: (i + bo, j))
    gate_spec = pl.BlockSpec((tr, RNN_OUT_COLS), lambda i, j: (i, j + go))
    out_spec = pl.BlockSpec((tr, RNN_OUT_COLS), lambda i, j: (i, j))
    return (rows // tr, D_RNN // RNN_OUT_COLS), hf_spec, hb_spec, gate_spec, out_spec


def _rnn_out_fwd(hf, hb, proj, hf_off, hb_off):
    rows = proj.shape[0]
    grid, hf_spec, hb_spec, gate_spec, out_spec = _rnn_out_specs(rows, hf_off, hb_off)

    def body(hf_ref, hb_ref, g_ref, o_ref):
        o_ref[...] = ((hf_ref[...] + hb_ref[...]) * _gelu(g_ref[...])).astype(o_ref.dtype)

    return pl.pallas_call(
        body, name="rnn_out_fwd", grid=grid, in_specs=[hf_spec, hb_spec, gate_spec], out_specs=out_spec,
        out_shape=jax.ShapeDtypeStruct((rows, D_RNN), BF16), compiler_params=_params(("parallel", "parallel")),
    )(hf, hb, proj)


def _rnn_out_bwd(d_cat, hf, hb, proj, hf_off, hb_off):
    rows = proj.shape[0]
    grid, hf_spec, hb_spec, gate_spec, out_spec = _rnn_out_specs(rows, hf_off, hb_off)
    do = ATTN_W // RNN_OUT_COLS

    def body(d_ref, hf_ref, hb_ref, g_ref, dh_ref, dg_ref):
        dv, gv = d_ref[...].astype(F32), g_ref[...]
        dh_ref[...] = dv * _gelu(gv)
        dg_ref[...] = (dv * (hf_ref[...] + hb_ref[...]) * _gelu_grad(gv)).astype(dg_ref.dtype)

    tr = out_spec.block_shape[0]
    return pl.pallas_call(
        body, name="rnn_out_bwd", grid=grid,
        in_specs=[pl.BlockSpec((tr, RNN_OUT_COLS), lambda i, j: (i, j + do)), hf_spec, hb_spec, gate_spec],
        out_specs=(out_spec, out_spec),
        out_shape=(jax.ShapeDtypeStruct((rows, D_RNN), F32), jax.ShapeDtypeStruct((rows, D_RNN), BF16)),
        compiler_params=_params(("parallel", "parallel")),
    )(d_cat, hf, hb, proj)


def _gmlp_parts(z_ref, vg_ref, vb_ref, d_gm):
    zu, zv = z_ref[:, :d_gm], z_ref[:, d_gm:]
    u = _gelu(zu)
    v = _gelu(zv)
    mu = jnp.mean(v, axis=-1, keepdims=True)
    vc = v - mu
    rstd = lax.rsqrt(jnp.mean(vc * vc, axis=-1, keepdims=True) + EPS)
    vhat = vc * rstd
    vn = vhat * vg_ref[...] + vb_ref[...]
    return zu, zv, u, vhat, rstd, vn


def _gmlp_fwd(z, v_g, v_b, w_sp, b_sp_t):
    rows, d_gm = z.shape[0], z.shape[1] // 2
    tr = _tile(rows, 256, CHUNK)
    gwid = d_gm // GM_GROUPS

    def body(z_ref, vg_ref, vb_ref, w_ref, b_ref, o_ref):
        _, _, u, _, _, vn = _gmlp_parts(z_ref, vg_ref, vb_ref, d_gm)
        vnb = vn.astype(BF16)
        for g in range(GM_GROUPS):
            wg = w_ref[g].astype(BF16)
            for c in range(tr // CHUNK):
                rs, cs = slice(c * CHUNK, (c + 1) * CHUNK), slice(g * gwid, (g + 1) * gwid)
                sv = jnp.dot(wg, vnb[rs, cs], preferred_element_type=F32) + b_ref[:, g:g + 1]
                o_ref[rs, cs] = (u[rs, cs] * sv).astype(o_ref.dtype)

    return pl.pallas_call(
        body, name="gmlp_fwd", grid=(rows // tr,),
        in_specs=[pl.BlockSpec((tr, 2 * d_gm), lambda i: (i, 0)), _full((1, d_gm)), _full((1, d_gm)),
                  _full(w_sp.shape), _full(b_sp_t.shape)],
        out_specs=pl.BlockSpec((tr, d_gm), lambda i: (i, 0)),
        out_shape=jax.ShapeDtypeStruct((rows, d_gm), BF16), compiler_params=_params(("parallel",)),
    )(z, v_g, v_b, w_sp, b_sp_t)


def _gmlp_bwd(z, dgate, v_g, v_b, w_sp, b_sp_t):
    rows, d_gm = z.shape[0], z.shape[1] // 2
    tr = _tile(rows, 256, CHUNK)
    gwid = d_gm // GM_GROUPS

    def body(z_ref, dg_ref, vg_ref, vb_ref, w_ref, b_ref, dz_ref, dbin_ref, dvg_ref, dvb_ref, dw_ref, dbs_ref, dvn_s):
        @pl.when(pl.program_id(0) == 0)
        def _():
            dbin_ref[...] = jnp.zeros_like(dbin_ref)
            dvg_ref[...] = jnp.zeros_like(dvg_ref)
            dvb_ref[...] = jnp.zeros_like(dvb_ref)
            dw_ref[...] = jnp.zeros_like(dw_ref)
            dbs_ref[...] = jnp.zeros_like(dbs_ref)

        zu, zv, u, vhat, rstd, vn = _gmlp_parts(z_ref, vg_ref, vb_ref, d_gm)
        vnb = vn.astype(BF16)
        dgv = dg_ref[...].astype(F32)
        dsv = dgv * u
        dsvb = dsv.astype(BF16)
        for g in range(GM_GROUPS):
            wg = w_ref[g].astype(BF16)
            cs = slice(g * gwid, (g + 1) * gwid)
            for c in range(tr // CHUNK):
                rs = slice(c * CHUNK, (c + 1) * CHUNK)
                sv = jnp.dot(wg, vnb[rs, cs], preferred_element_type=F32) + b_ref[:, g:g + 1]
                dz_ref[rs, cs] = (dgv[rs, cs] * sv * _gelu_grad(zu[rs, cs])).astype(dz_ref.dtype)
                dw_ref[g] += lax.dot_general(dsvb[rs, cs], vnb[rs, cs], (((1,), (1,)), ((), ())),
                                             preferred_element_type=F32)
                dbs_ref[:, g:g + 1] += jnp.sum(dsv[rs, cs], axis=-1, keepdims=True)
                dvn_s[rs, cs] = lax.dot_general(wg, dsvb[rs, cs], (((0,), (0,)), ((), ())), preferred_element_type=F32)
        dvn = dvn_s[...]
        dvg_ref[...] += jnp.sum(dvn * vhat, axis=0, keepdims=True)
        dvb_ref[...] += jnp.sum(dvn, axis=0, keepdims=True)
        dvh = dvn * vg_ref[...]
        dv = rstd * (dvh - jnp.mean(dvh, axis=-1, keepdims=True) - vhat * jnp.mean(dvh * vhat, axis=-1, keepdims=True))
        dzv = dv * _gelu_grad(zv)
        dz_ref[:, d_gm:] = dzv.astype(dz_ref.dtype)
        dbin_ref[:, d_gm:] += jnp.sum(dzv, axis=0, keepdims=True)
        dbin_ref[:, :d_gm] += jnp.sum(dz_ref[:, :d_gm].astype(F32), axis=0, keepdims=True)

    return pl.pallas_call(
        body, name="gmlp_bwd", grid=(rows // tr,),
        in_specs=[pl.BlockSpec((tr, 2 * d_gm), lambda i: (i, 0)), pl.BlockSpec((tr, d_gm), lambda i: (i, 0)),
                  _full((1, d_gm)), _full((1, d_gm)), _full(w_sp.shape), _full(b_sp_t.shape)],
        out_specs=(pl.BlockSpec((tr, 2 * d_gm), lambda i: (i, 0)), _full((1, 2 * d_gm)), _full((1, d_gm)),
                   _full((1, d_gm)), _full(w_sp.shape), _full(b_sp_t.shape)),
        out_shape=(jax.ShapeDtypeStruct((rows, 2 * d_gm), BF16), jax.ShapeDtypeStruct((1, 2 * d_gm), F32),
                   jax.ShapeDtypeStruct((1, d_gm), F32), jax.ShapeDtypeStruct((1, d_gm), F32),
                   jax.ShapeDtypeStruct(w_sp.shape, F32), jax.ShapeDtypeStruct(b_sp_t.shape, F32)),
        scratch_shapes=[pltpu.VMEM((tr, d_gm), F32)],
        compiler_params=_params(("arbitrary",)),
    )(z, dgate, v_g, v_b, w_sp, b_sp_t)


def _adamw_math(w, g, m, v):
    m = ADAM_B1 * m + (1.0 - ADAM_B1) * g
    v = ADAM_B2 * v + (1.0 - ADAM_B2) * (g * g)
    m_hat = m / (1.0 - ADAM_B1 ** ADAM_STEP)
    v_hat = v / (1.0 - ADAM_B2 ** ADAM_STEP)
    delta = -ADAM_LR * (m_hat / (jnp.sqrt(v_hat) + ADAM_EPS) + ADAM_WD * w)
    return delta, m, v


def _adamw(w, g, m, v, name):
    shape = w.shape
    outs = _rowwise(_adamw_math, (F32, F32, F32), _as2d(w), _as2d(g), _as2d(m), _as2d(v), name=name)
    return (g.reshape(shape),) + tuple(o.reshape(shape) for o in outs)


PACK_COLS = 1024


def _pack(arrays):
    flat = jnp.concatenate([a.reshape(-1).astype(F32) for a in arrays])
    pad = (-flat.size) % (8 * PACK_COLS)
    return jnp.pad(flat, (0, pad)).reshape(-1, PACK_COLS)


def _unpack(flat, shapes):
    out, pos = [], 0
    for shp in shapes:
        n = math.prod(shp)
        out.append(flat[pos:pos + n].reshape(shp))
        pos += n
    return out


def _unpack_devices(packed8, shapes):
    flat8 = packed8.reshape(N_DEV, -1)
    out, pos = [], 0
    for shp in shapes:
        n = math.prod(shp)
        out.append(flat8[:, pos:pos + n].reshape((N_DEV,) + tuple(shp)))
        pos += n
    return out


def _sum_devices(g8):
    _, rows, cols = g8.shape
    tr = _rows_tile(rows, cols, budget=256 * 1024)

    def body(g_ref, o_ref):
        acc = g_ref[0]
        for d in range(1, N_DEV):
            acc = acc + g_ref[d]
        o_ref[...] = acc

    return pl.pallas_call(
        body, name="sum_devices", grid=(rows // tr,), in_specs=[pl.BlockSpec((N_DEV, tr, cols), lambda i: (0, i, 0))],
        out_specs=pl.BlockSpec((tr, cols), lambda i: (i, 0)), out_shape=jax.ShapeDtypeStruct((rows, cols), F32),
        compiler_params=_params(("parallel",)),
    )(g8)


def _place():
    return lax.axis_index("x"), lax.axis_index("y"), lax.axis_index("c")


def _other_chips(x, y):
    return [(1 - x, y), (x, 1 - y), (1 - x, 1 - y)]


def _remote(src, dst, send_sem, recv_sem, to):
    return pltpu.make_async_remote_copy(src_ref=src, dst_ref=dst, send_sem=send_sem, recv_sem=recv_sem, device_id=to,
                                        device_id_type=MESH)


def _comm_call(body, name, operands, out_shapes, n_remote, n_local, aliases=None):
    return pl.pallas_call(
        body, name=name, out_shape=tuple(out_shapes), in_specs=[ANY] * len(operands), out_specs=tuple(ANY for _ in out_shapes),
        scratch_shapes=[pltpu.SemaphoreType.DMA((n_remote,)), pltpu.SemaphoreType.DMA((n_remote,)),
                        pltpu.SemaphoreType.DMA((max(n_local, 1),))],
        input_output_aliases=aliases or {},
    )(*operands)


def _in_place(arrays):
    return [jax.ShapeDtypeStruct(a.shape, a.dtype) for a in arrays], {i: i for i in range(len(arrays))}


def _allgather8(arrs, name):
    n = len(arrs)

    def body(*refs):
        ins, outs = refs[:n], refs[n:2 * n]
        send, recv, lsem = refs[2 * n:]
        x, y, c = _place()
        me, sib = (x, y, c), (x, y, 1 - c)
        chips = _other_chips(x, y)

        def slot(t, px, py, pc):
            return outs[t].at[4 * px + 2 * py + pc]

        def cp(t, k, block, to, from_input=False):
            src = ins[t] if from_input else slot(t, *block)
            return _remote(src, slot(t, *block), send.at[7 * t + k], recv.at[7 * t + k], to)

        mine = [pltpu.make_async_copy(ins[t], slot(t, *me), lsem.at[t]) for t in range(n)]
        for cpy in mine:
            cpy.start()
        first = []
        for t in range(n):
            first.append(cp(t, 0, me, sib, True))
            first += [cp(t, 1 + j, me, (*chip, c), True) for j, chip in enumerate(chips)]
        for cpy in first:
            cpy.start()
        passed = []
        for t in range(n):
            for j, chip in enumerate(chips):
                cp(t, 1 + j, (*chip, c), me).wait_recv()
                fwd = cp(t, 4 + j, (*chip, c), sib)
                fwd.start()
                passed.append(fwd)
        for t in range(n):
            cp(t, 0, sib, me).wait_recv()
            for j, chip in enumerate(chips):
                cp(t, 4 + j, (*chip, 1 - c), me).wait_recv()
        for cpy in first + passed:
            cpy.wait_send()
        for cpy in mine:
            cpy.wait()

    outs = _comm_call(body, name, arrs, [jax.ShapeDtypeStruct((N_DEV,) + a.shape, a.dtype) for a in arrs], 7 * n, n)
    return list(outs)


def _gather_weights(bufs):
    n_u = len(bufs)

    def body(*refs):
        bufs_ = refs[n_u:2 * n_u]
        send, recv, _ = refs[2 * n_u:]
        x, y, c = _place()
        me, sib, q = (x, y, c), (x, y, 1 - c), 2 * x + y
        chips = _other_chips(x, y)
        sent = []
        for u in range(n_u):
            half = bufs_[u].shape[1] // 2
            mine = bufs_[u].at[q, pl.ds(c * half, half)]
            for j, chip in enumerate(chips):
                cpy = _remote(mine, mine, send.at[6 * u + j], recv.at[6 * u + j], (*chip, c))
                cpy.start()
                sent.append(cpy)
        for u in range(n_u):
            half = bufs_[u].shape[1] // 2
            for j, chip in enumerate(chips):
                landed = bufs_[u].at[2 * chip[0] + chip[1], pl.ds(c * half, half)]
                _remote(landed, landed, send.at[6 * u + j], recv.at[6 * u + j], me).wait_recv()
                cpy = _remote(landed, landed, send.at[6 * u + 3 + j], recv.at[6 * u + 3 + j], sib)
                cpy.start()
                sent.append(cpy)
        for u in range(n_u):
            half = bufs_[u].shape[1] // 2
            for j, chip in enumerate(chips):
                landed = bufs_[u].at[2 * chip[0] + chip[1], pl.ds((1 - c) * half, half)]
                _remote(landed, landed, send.at[6 * u + 3 + j], recv.at[6 * u + 3 + j], me).wait_recv()
        for cpy in sent:
            cpy.wait_send()

    shapes, aliases = _in_place(bufs)
    return list(_comm_call(body, "gather_weights", bufs, shapes, 6 * n_u, 0, aliases))


def _exchange_halves(grads):
    n = len(grads)

    def body(*refs):
        ins, outs = refs[:n], refs[n:2 * n]
        send, recv, _ = refs[2 * n:]
        x, y, c = _place()
        sib = (x, y, 1 - c)
        sent = []
        for k in range(n):
            half = ins[k].shape[1] // 2
            cpy = _remote(ins[k].at[pl.ds(0, N_CHIPS), pl.ds((1 - c) * half, half)], outs[k], send.at[k], recv.at[k], sib)
            cpy.start()
            sent.append(cpy)
        for cpy in sent:
            cpy.wait()

    shapes = [jax.ShapeDtypeStruct((N_CHIPS, g.shape[1] // 2, g.shape[2]), g.dtype) for g in grads]
    return list(_comm_call(body, "exchange_halves", grads, shapes, n, 0))


def _chips_all_to_all(sums):
    n = len(sums)

    def body(*refs):
        ins, outs = refs[:n], refs[n:2 * n]
        send, recv, _ = refs[2 * n:]
        x, y, c = _place()
        sent = []
        for k in range(n):
            for j, chip in enumerate(_other_chips(x, y)):
                cpy = _remote(ins[k].at[2 * chip[0] + chip[1]], outs[k].at[j], send.at[3 * k + j], recv.at[3 * k + j], (*chip, c))
                cpy.start()
                sent.append(cpy)
        for cpy in sent:
            cpy.wait()

    shapes = [jax.ShapeDtypeStruct((N_CHIPS - 1,) + s.shape[1:], s.dtype) for s in sums]
    return list(_comm_call(body, "chips_all_to_all", sums, shapes, 3 * n, 0))


def _join_halves(bufs):
    n = len(bufs)
    units = [(k, layer) for k in range(n) for layer in range(bufs[k].shape[0])]

    def body(*refs):
        bufs_ = refs[n:2 * n]
        send, recv, _ = refs[2 * n:]
        x, y, c = _place()
        sent = []
        for u, (k, layer) in enumerate(units):
            half = bufs_[k].shape[1] // 2
            mine = bufs_[k].at[layer, pl.ds(c * half, half)]
            cpy = _remote(mine, mine, send.at[u], recv.at[u], (x, y, 1 - c))
            cpy.start()
            sent.append(cpy)
        for u, (k, layer) in enumerate(units):
            half = bufs_[k].shape[1] // 2
            theirs = bufs_[k].at[layer, pl.ds((1 - c) * half, half)]
            _remote(theirs, theirs, send.at[u], recv.at[u], (x, y, c)).wait_recv()
        for cpy in sent:
            cpy.wait_send()

    shapes, aliases = _in_place(bufs)
    return list(_comm_call(body, "join_halves", bufs, shapes, len(units), 0, aliases))


def _add_halves(grad, other, place):
    _, rows, cols = grad.shape
    half = rows // 2
    tr = _rows_tile(half, cols, itemsize=2, budget=1024 * 1024)
    per_half = half // tr

    def body(place_ref, g_ref, o_ref, s_ref):
        s_ref[...] = (g_ref[...].astype(F32) + o_ref[...].astype(F32)).astype(s_ref.dtype)

    return pl.pallas_call(
        body, name="add_halves", out_shape=jax.ShapeDtypeStruct((N_CHIPS, half, cols), grad.dtype),
        grid_spec=pltpu.PrefetchScalarGridSpec(
            num_scalar_prefetch=1, grid=(N_CHIPS, per_half),
            in_specs=[pl.BlockSpec((None, tr, cols), lambda k, i, pr: (k, pr[1] * per_half + i, 0)),
                      pl.BlockSpec((None, tr, cols), lambda k, i, pr: (k, i, 0))],
            out_specs=pl.BlockSpec((None, tr, cols), lambda k, i, pr: (k, i, 0))),
        compiler_params=_params(("parallel", "parallel")),
    )(place, grad, other)


def _add_chips(sums, others, place, dest, layer, n_layers):
    _, half, cols = sums.shape
    tr = _rows_tile(half, cols, itemsize=4, budget=1024 * 1024)
    per_half = half // tr

    def body(place_ref, s_ref, o_ref, *rest):
        acc = s_ref[...].astype(F32)
        for j in range(N_CHIPS - 1):
            acc = acc + o_ref[j].astype(F32)
        rest[-1][...] = acc

    operands = [place, sums, others] + ([] if dest is None else [dest])
    return pl.pallas_call(
        body, name="add_chips", out_shape=jax.ShapeDtypeStruct((n_layers, 2 * half, cols), F32),
        grid_spec=pltpu.PrefetchScalarGridSpec(
            num_scalar_prefetch=1, grid=(per_half,),
            in_specs=[pl.BlockSpec((None, tr, cols), lambda i, pr: (pr[0], i, 0)),
                      pl.BlockSpec((N_CHIPS - 1, tr, cols), lambda i, pr: (0, i, 0))] + ([] if dest is None else [ANY]),
            out_specs=pl.BlockSpec((None, tr, cols), lambda i, pr: (layer, pr[1] * per_half + i, 0))),
        input_output_aliases={} if dest is None else {3: 0},
        compiler_params=_params(("parallel",)),
    )(*operands)


HBM = pl.BlockSpec(memory_space=pltpu.HBM)
SEM = pl.BlockSpec(memory_space=pltpu.SEMAPHORE)
DATAFLOW = pltpu.SideEffectType.DATAFLOW_SIDE_EFFECTING


def _split_start(name, bufs, copies, n_copies, after=None):
    n = len(bufs)
    extra = 0 if after is None else 1

    def body(*refs):
        for cpy in copies(refs[:n], refs[n + extra], refs[n + extra + 1]):
            cpy.start()
        refs[-1][...] = jnp.zeros_like(refs[-1])

    outs = pl.pallas_call(
        body, name=name,
        out_shape=(pltpu.SemaphoreType.DMA((n_copies,)), pltpu.SemaphoreType.DMA((n_copies,)),
                   *[pltpu.HBM(b.shape, b.dtype) for b in bufs], jax.ShapeDtypeStruct((8, LANES), F32)),
        in_specs=[HBM] * n + [ANY] * extra,
        out_specs=(SEM, SEM, *[HBM] * n, pl.BlockSpec(memory_space=pltpu.VMEM)),
        input_output_aliases={i: 2 + i for i in range(n)},
        compiler_params=pltpu.CompilerParams(has_side_effects=DATAFLOW),
    )(*[pltpu.with_memory_space_constraint(b, pltpu.HBM) for b in bufs], *([] if after is None else [after]))
    return outs[0], outs[1], list(outs[2:2 + n]), outs[-1]


def _split_wait(name, bufs, send, recv, copies, after):
    n = len(bufs)

    def body(*refs):
        for cpy in copies(refs[:n], refs[n], refs[n + 1]):
            cpy.wait_send()
            cpy.wait_recv()

    return list(pl.pallas_call(
        body, name=name, out_shape=tuple(pltpu.HBM(b.shape, b.dtype) for b in bufs),
        in_specs=[HBM] * n + [SEM, SEM, ANY], out_specs=tuple([HBM] * n),
        input_output_aliases={i: i for i in range(n)},
        compiler_params=pltpu.CompilerParams(has_side_effects=DATAFLOW),
    )(*bufs, send, recv, after))


def _gather_copies(bufs, send, recv):
    x, y, c = _place()
    out = []
    for u, buf in enumerate(bufs):
        half = buf.shape[1] // 2
        mine = buf.at[2 * x + y, pl.ds(c * half, half)]
        out += [_remote(mine, mine, send.at[3 * u + j], recv.at[3 * u + j], (*chip, c))
                for j, chip in enumerate(_other_chips(x, y))]
    return out


def _all_to_all_copies(bufs, send, recv):
    x, y, c = _place()
    n = len(bufs) // 2
    return [_remote(bufs[k].at[2 * chip[0] + chip[1]], bufs[n + k].at[j], send.at[3 * k + j], recv.at[3 * k + j], (*chip, c))
            for k in range(n) for j, chip in enumerate(_other_chips(x, y))]


def _forward_halves(bufs, name):
    n_u = len(bufs)

    def body(*refs):
        bufs_ = refs[n_u:2 * n_u]
        send, recv, _ = refs[2 * n_u:]
        x, y, c = _place()
        chips = _other_chips(x, y)
        sent = []
        for u in range(n_u):
            half = bufs_[u].shape[1] // 2
            for j, chip in enumerate(chips):
                landed = bufs_[u].at[2 * chip[0] + chip[1], pl.ds(c * half, half)]
                cpy = _remote(landed, landed, send.at[3 * u + j], recv.at[3 * u + j], (x, y, 1 - c))
                cpy.start()
                sent.append(cpy)
        for u in range(n_u):
            half = bufs_[u].shape[1] // 2
            for j, chip in enumerate(chips):
                theirs = bufs_[u].at[2 * chip[0] + chip[1], pl.ds((1 - c) * half, half)]
                _remote(theirs, theirs, send.at[3 * u + j], recv.at[3 * u + j], (x, y, c)).wait_recv()
        for cpy in sent:
            cpy.wait_send()

    shapes, aliases = _in_place(bufs)
    return list(_comm_call(body, name, bufs, shapes, 3 * n_u, 0, aliases))


FWD_GROUPS = {'ffn0': ('ff_in0', 'ff_out0'), 'l1': ('gm_in', 'gm_out', 'ff_in1', 'ff_out1')}
GRAD_LAYOUT = {'ff_in0': (0, 0), 'ff_in1': (0, 1), 'ff_out0': (1, 0), 'ff_out1': (1, 1), 'ar_in': (2, 0), 'ar_out': (3, 0),
               'gm_in': (4, 0), 'gm_out': (5, 0)}


class _MeshLink:
    def __init__(self, place, shards):
        self.place = place
        ar = _gather_weights([shards['ar_in'], shards['ar_out']])
        self.ready = {'ar_in': ar[0], 'ar_out': ar[1]}
        self.pending, after = {}, ar[1]
        for group, names in FWD_GROUPS.items():
            send, recv, bufs, token = _split_start(f"gather_{group}_start", [shards[n] for n in names], _gather_copies,
                                                   3 * len(names), after)
            self.pending[group] = (names, send, recv, bufs)
            after = token
        self.start_token = after[0, 0]
        self.sent, self.reduced = {}, {}

    def weights(self, group, after):
        if group in self.pending:
            names, send, recv, bufs = self.pending.pop(group)
            bufs = _split_wait(f"gather_{group}_wait", bufs, send, recv, _gather_copies, after)
            self.ready.update(zip(names, _forward_halves(bufs, f"gather_{group}_forward")))
        return self.ready

    def gradients(self, group, grads):
        names = list(grads)
        received = _exchange_halves([grads[n] for n in names])
        sums = [_add_halves(grads[n], r, self.place) for n, r in zip(names, received)]
        if group == 'ar':
            self.reduced.update(zip(names, zip(sums, _chips_all_to_all(sums))))
            return 0.0
        landing = [lax.empty((N_CHIPS - 1,) + s.shape[1:], s.dtype) for s in sums]
        send, recv, bufs, token = _split_start(f"grads_{group}_start", sums + landing, _all_to_all_copies, 3 * len(names))
        self.sent[group] = (names, send, recv, bufs)
        return token[0, 0]

    def finish(self, after):
        for group, (names, send, recv, bufs) in self.sent.items():
            bufs = _split_wait(f"grads_{group}_wait", bufs, send, recv, _all_to_all_copies, after)
            self.reduced.update(zip(names, zip(bufs[:len(names)], bufs[len(names):])))
        n_layers = {p: 1 + max(l for pp, l in GRAD_LAYOUT.values() if pp == p) for p, _ in GRAD_LAYOUT.values()}
        out = {}
        for name, (p, layer) in GRAD_LAYOUT.items():
            sums, others = self.reduced[name]
            out[p] = _add_chips(sums, others, self.place, out.get(p), layer, n_layers[p])
        return _join_halves([out[p] for p in sorted(out)])


def _rope_tables(n):
    t = jnp.arange(n)
    freqs = ROPE_THETA ** (-jnp.arange(ROPE_PAIRS, dtype=F32) / ROPE_PAIRS)
    ang_r = (t // GRID_W).astype(F32)[:, None] * freqs
    ang_c = (t % GRID_W).astype(F32)[:, None] * freqs
    cos = jnp.concatenate([jnp.cos(ang_r), jnp.cos(ang_r), jnp.cos(ang_c), jnp.cos(ang_c)], axis=-1)
    sin = jnp.concatenate([-jnp.sin(ang_r), jnp.sin(ang_r), -jnp.sin(ang_c), jnp.sin(ang_c)], axis=-1)
    return cos, sin


def _ffn_fwd(h2, w1, w2, tag):
    r, a = _matmul(h2, w1, kind='nn', b_split='n', out_dtype=BF16, epilogue='relu2', name=f"ffn_in_{tag}")
    f = _matmul(a, w2, kind='nn', b_split='k', out_dtype=F32, name=f"ffn_out_{tag}")
    return r, a, f


def _ffn_bwd(d_f, h2, r, a, w1, w2, tag):
    d_u = _matmul(d_f, w2, kind='nt', b_split='k', out_dtype=BF16, epilogue='times2x', extra=r, name=f"ffn_out_dx_{tag}")
    d_w2 = _matmul(a, d_f, kind='tn', out_split='k', out_dtype=BF16, name=f"ffn_out_dw_{tag}")
    d_w1 = _matmul(h2, d_u, kind='tn', out_split='n', out_dtype=BF16, name=f"ffn_in_dw_{tag}")
    d_h2 = _matmul(d_u, w1, kind='nt', b_split='n', out_dtype=F32, name=f"ffn_in_dx_{tag}")
    return d_h2, d_w1, d_w2


class _LocalLink:
    def __init__(self, big):
        self.big, self.grads, self.start_token = big, {}, 0.0

    def weights(self, group, after):
        return self.big

    def gradients(self, group, grads):
        self.grads.update(grads)
        return 0.0


def _local_step(xl0, xc0, target, ml, mc0, sp, link):
    n_lat, n_ctx = xl0.shape[0], xc0.shape[0]
    one = lambda v: 1.0 + v
    g = [[sp['norm_g'][i, k][None, :] for k in range(4)] for i in range(2)]

    sh1, sc1, gt1, sh2, sc2, gt2 = ml[0]
    big = link.weights('ar', None)
    sh1 = sh1 + link.start_token
    hl = _norm_fwd(xl0, g[0][0], one(sc1), b=sh1, out_dtype=BF16, name="l0_mod1")
    hc = _norm_fwd(xc0, g[0][0], one(mc0[1]), b=mc0[0], out_dtype=BF16, name="l0_mod1_ctx")
    proj_l = _matmul(hl, big['ar_in'], kind='nn', b_split='n', out_dtype=F32, name="ar_in_lat")
    proj_c = _matmul(hc, big['ar_in'], kind='nn', b_split='n', out_dtype=F32, name="ar_in_ctx")
    cos_l, sin_l = _rope_tables(n_lat)
    cos_c, sin_c = jnp.ones((n_ctx, HEAD_DIM), F32), jnp.zeros((n_ctx, HEAD_DIM), F32)
    q_g, k_g = sp['q_g'], sp['k_g']
    q_l, k_l, v_l = _qk_fwd(proj_l, q_g, k_g, cos_l, sin_l, name="qk_fwd_lat")
    _, k_c, v_c = _qk_fwd(proj_c, q_g, k_g, cos_c, sin_c, name="qk_fwd_ctx")
    k_all = jnp.concatenate([k_c, k_l], axis=0)
    v_all = jnp.concatenate([v_c, v_l], axis=0)
    attn, lse = _attn_fwd(q_l, k_all, v_all)
    conv_l = _conv_fwd(proj_l, sp['conv_w'], sp['conv_b'], name="conv_fwd_lat")
    conv_c = _conv_fwd(proj_c, sp['conv_w'], sp['conv_b'], name="conv_fwd_ctx")
    xs_f = jnp.concatenate([conv_c, conv_l], axis=0)
    xs_r = jnp.concatenate([conv_l, conv_c], axis=0)
    rnn_w = [(sp['wa'][d], sp['ba'][d][None, :], sp['wx'][d], sp['bx'][d][None, :], sp['lam'][d][None, :]) for d in range(2)]
    h_f, hp_f = _rglru_fwd(xs_f, *rnn_w[0], reverse=False, name="rglru_fwd_f")
    h_r, hp_r = _rglru_fwd(xs_r, *rnn_w[1], reverse=True, name="rglru_fwd_r")
    rnn = _rnn_out_fwd(h_f, h_r, proj_l, n_ctx, 0)
    cat = jnp.concatenate([attn, rnn], axis=1)
    ol0 = _matmul(cat, big['ar_out'], kind='nn', b_split='k', out_dtype=F32, name="ar_out")
    xm0 = _norm_fwd(ol0, g[0][1], gt1, res=xl0, out_dtype=F32, name="l0_res1")
    h2_0 = _norm_fwd(xm0, g[0][2], one(sc2), b=sh2, out_dtype=BF16, name="l0_mod2")
    w_f0 = link.weights('ffn0', h2_0)
    r0, a0, f0 = _ffn_fwd(h2_0, w_f0['ff_in0'], w_f0['ff_out0'], "l0")
    xl1 = _norm_fwd(f0, g[0][3], gt2, res=xm0, out_dtype=F32, name="l0_res2")

    th1, tc1, tg1, th2, tc2, tg2 = ml[1]
    w_l1 = link.weights('l1', xl1)
    hl1 = _norm_fwd(xl1, g[1][0], one(tc1), b=th1, out_dtype=BF16, name="l1_mod1")
    z = _matmul(hl1, w_l1['gm_in'], kind='nn', b_split='n', bias=sp['gm_b_in'], out_dtype=F32, name="gm_in")
    b_sp_t = sp['gm_b_sp'].T
    gated = _gmlp_fwd(z, sp['gm_v_g'], sp['gm_v_b'], sp['gm_w_sp'], b_sp_t)
    ol1 = _matmul(gated, w_l1['gm_out'], kind='nn', b_split='k', out_dtype=F32, name="gm_out")
    xm1 = _norm_fwd(ol1, g[1][1], tg1, res=xl1, out_dtype=F32, name="l1_res1")
    h2_1 = _norm_fwd(xm1, g[1][2], one(tc2), b=th2, out_dtype=BF16, name="l1_mod2")
    r1, a1, f1 = _ffn_fwd(h2_1, w_l1['ff_in1'], w_l1['ff_out1'], "l1")
    y = _norm_fwd(f1, g[1][3], tg2, res=xm1, out_dtype=F32, name="l1_res2")

    dy, loss = _loss_head(y, target)

    d_f1, dg13, d_tg2, _ = _norm_bwd(dy, f1, g[1][3], tg2, out_dtype=BF16, name="l1_res2_bwd")
    d_h2, dw_ff_in1, dw_ff_out1 = _ffn_bwd(d_f1, h2_1, r1, a1, w_l1['ff_in1'], w_l1['ff_out1'], "l1")
    tok = link.gradients('ffn1', {'ff_in1': dw_ff_in1, 'ff_out1': dw_ff_out1})
    dxm1, dg12, d_tc2, d_th2 = _norm_bwd(d_h2, xm1, g[1][2], one(tc2) + tok, extra=dy, out_dtype=F32, name="l1_mod2_bwd")
    d_ol1, dg11, d_tg1, _ = _norm_bwd(dxm1, ol1, g[1][1], tg1, out_dtype=BF16, name="l1_res1_bwd")
    d_gated = _matmul(d_ol1, w_l1['gm_out'], kind='nt', b_split='k', out_dtype=F32, name="gm_out_dx")
    dw_gm_out = _matmul(gated, d_ol1, kind='tn', out_split='k', out_dtype=BF16, name="gm_out_dw")
    d_z, d_gm_b_in, d_vg, d_vb, d_wsp, d_bsp_t = _gmlp_bwd(z, d_gated, sp['gm_v_g'], sp['gm_v_b'], sp['gm_w_sp'], b_sp_t)
    dw_gm_in = _matmul(hl1, d_z, kind='tn', out_split='n', out_dtype=BF16, name="gm_in_dw")
    d_hl1 = _matmul(d_z, w_l1['gm_in'], kind='nt', b_split='n', out_dtype=F32, name="gm_in_dx")
    tok = link.gradients('gm', {'gm_in': dw_gm_in, 'gm_out': dw_gm_out})
    dxl1, dg10, d_tc1, d_th1 = _norm_bwd(d_hl1, xl1, g[1][0], one(tc1) + tok, extra=dxm1, out_dtype=F32, name="l1_mod1_bwd")

    d_f0, dg03, d_gt2, _ = _norm_bwd(dxl1, f0, g[0][3], gt2, out_dtype=BF16, name="l0_res2_bwd")
    d_h2, dw_ff_in0, dw_ff_out0 = _ffn_bwd(d_f0, h2_0, r0, a0, w_f0['ff_in0'], w_f0['ff_out0'], "l0")
    tok = link.gradients('ffn0', {'ff_in0': dw_ff_in0, 'ff_out0': dw_ff_out0})
    dxm0, dg02, d_sc2, d_sh2 = _norm_bwd(d_h2, xm0, g[0][2], one(sc2) + tok, extra=dxl1, out_dtype=F32, name="l0_mod2_bwd")
    d_ol0, dg01, d_gt1, _ = _norm_bwd(dxm0, ol0, g[0][1], gt1, out_dtype=BF16, name="l0_res1_bwd")
    d_cat = _matmul(d_ol0, big['ar_out'], kind='nt', b_split='k', out_dtype=F32, name="ar_out_dx")
    dw_ar_out = _matmul(cat, d_ol0, kind='tn', out_split='k', out_dtype=BF16, name="ar_out_dw")
    dq, dk_all, dv_all = _attn_bwd(q_l, k_all, v_all, attn, lse, d_cat)
    d_h, d_gate = _rnn_out_bwd(d_cat, h_f, h_r, proj_l, n_ctx, 0)
    zeros_c = jnp.zeros((n_ctx, D_RNN), F32)
    dxs_f, d_wa0, d_ba0, d_wx0, d_bx0, d_lam0 = _rglru_bwd(
        xs_f, hp_f, jnp.concatenate([zeros_c, d_h], axis=0), *rnn_w[0], reverse=False, name="rglru_bwd_f")
    dxs_r, d_wa1, d_ba1, d_wx1, d_bx1, d_lam1 = _rglru_bwd(
        xs_r, hp_r, jnp.concatenate([d_h, zeros_c], axis=0), *rnn_w[1], reverse=True, name="rglru_bwd_r")
    d_xr_l, d_cw_l, d_cb_l = _conv_bwd(dxs_f[n_ctx:], dxs_r[:n_lat], proj_l, sp['conv_w'], name="conv_bwd_lat")
    d_xr_c, d_cw_c, d_cb_c = _conv_bwd(dxs_f[:n_ctx], dxs_r[n_lat:], proj_c, sp['conv_w'], name="conv_bwd_ctx")
    dp_qk_l, d_qg, d_kg_l = _qk_bwd(dq, dk_all[n_ctx:], proj_l, q_g, k_g, cos_l, sin_l, name="qk_bwd_lat")
    dp_qk_c, _, d_kg_c = _qk_bwd(None, dk_all[:n_ctx], proj_c, q_g, k_g, cos_c, sin_c, name="qk_bwd_ctx")
    dv_b = dv_all.astype(BF16)
    d_proj_l = jnp.concatenate([dp_qk_l, dv_b[n_ctx:], d_xr_l, d_gate], axis=1)
    d_proj_c = jnp.concatenate([dp_qk_c, dv_b[:n_ctx], d_xr_c, jnp.zeros((n_ctx, D_RNN), BF16)], axis=1)
    dw_ar_in = _matmul(jnp.concatenate([hc, hl], axis=0), jnp.concatenate([d_proj_c, d_proj_l], axis=0), kind='tn',
                       out_split='n', out_dtype=BF16, name="ar_in_dw")
    d_hl = _matmul(d_proj_l, big['ar_in'], kind='nt', b_split='n', out_dtype=F32, name="ar_in_dx_lat")
    d_hc = _matmul(d_proj_c, big['ar_in'], kind='nt', b_split='n', out_dtype=F32, name="ar_in_dx_ctx")
    grad_x, dg00, d_sc1, d_sh1 = _norm_bwd(d_hl, xl0, g[0][0], one(sc1), extra=dxm0, out_dtype=F32, name="l0_mod1_bwd")
    _, dg00c, d_mc_scale, d_mc_shift = _norm_bwd(d_hc, xc0, g[0][0], one(mc0[1]), out_dtype=BF16, name="l0_mod1_ctx_bwd")

    zeros_d = jnp.zeros_like(d_sh1)
    small = {
        'd_ml0': jnp.concatenate([d_sh1, d_sc1, d_gt1, d_sh2, d_sc2, d_gt2], axis=1),
        'd_ml1': jnp.concatenate([d_th1, d_tc1, d_tg1, d_th2, d_tc2, d_tg2], axis=1),
        'd_mc0': jnp.concatenate([d_mc_shift, d_mc_scale] + [zeros_d] * 4, axis=1),
        'norm_g': jnp.stack([jnp.concatenate([dg00 + dg00c, dg01, dg02, dg03], axis=0),
                             jnp.concatenate([dg10, dg11, dg12, dg13], axis=0)]),
        'q_g': d_qg, 'k_g': d_kg_l + d_kg_c, 'conv_w': d_cw_l + d_cw_c, 'conv_b': d_cb_l + d_cb_c,
        'wa': jnp.stack([d_wa0, d_wa1]), 'ba': jnp.concatenate([d_ba0, d_ba1], axis=0),
        'wx': jnp.stack([d_wx0, d_wx1]), 'bx': jnp.concatenate([d_bx0, d_bx1], axis=0),
        'lam': jnp.concatenate([d_lam0, d_lam1], axis=0),
        'gm_b_in': d_gm_b_in, 'gm_v_g': d_vg, 'gm_v_b': d_vb, 'gm_w_sp': d_wsp, 'gm_b_sp': d_bsp_t.T,
        'loss': loss,
    }
    link.gradients('ar', {'ar_in': dw_ar_in, 'ar_out': dw_ar_out})
    return grad_x, small


MOD_ROWS = 16
SMALL_ORDER = ('d_ml0', 'd_ml1', 'd_mc0', 'norm_g', 'q_g', 'k_g', 'conv_w', 'conv_b', 'wa', 'ba', 'wx', 'bx', 'lam',
               'gm_b_in', 'gm_v_g', 'gm_v_b', 'gm_w_sp', 'gm_b_sp', 'loss')


def _silu(v):
    return v * _sigmoid(v)


def _chip_concat(gathered, axis):
    return jnp.concatenate([gathered[2 * q] for q in range(N_CHIPS)], axis=axis)


def kernel(x, c, ctx, c_ctx, w_mod, b_mod, norm_g, w_ff_in, w_ff_out, ar_w_in, ar_q_g, ar_k_g, ar_conv_w, ar_conv_b, ar_wa, ar_ba, ar_wx, ar_bx, ar_lambda, ar_w_out, gm_w_in, gm_b_in, gm_v_g, gm_v_b, gm_w_sp, gm_b_sp, gm_w_out, loss_target, m_c_ctx, m_w_mod, m_b_mod, m_norm_g, m_w_ff_in, m_w_ff_out, m_ar_w_in, m_ar_q_g, m_ar_k_g, m_ar_conv_w, m_ar_conv_b, m_ar_wa, m_ar_ba, m_ar_wx, m_ar_bx, m_ar_lambda, m_ar_w_out, m_gm_w_in, m_gm_b_in, m_gm_v_g, m_gm_v_b, m_gm_w_sp, m_gm_b_sp, m_gm_w_out, v_c_ctx, v_w_mod, v_b_mod, v_norm_g, v_w_ff_in, v_w_ff_out, v_ar_w_in, v_ar_q_g, v_ar_k_g, v_ar_conv_w, v_ar_conv_b, v_ar_wa, v_ar_ba, v_ar_wx, v_ar_bx, v_ar_lambda, v_ar_w_out, v_gm_w_in, v_gm_b_in, v_gm_v_g, v_gm_v_b, v_gm_w_sp, v_gm_b_sp, v_gm_w_out):
    weights = dict(c_ctx=c_ctx, w_mod=w_mod, b_mod=b_mod, norm_g=norm_g, w_ff_in=w_ff_in, w_ff_out=w_ff_out, ar_w_in=ar_w_in,
                   ar_q_g=ar_q_g, ar_k_g=ar_k_g, ar_conv_w=ar_conv_w, ar_conv_b=ar_conv_b, ar_wa=ar_wa, ar_ba=ar_ba, ar_wx=ar_wx,
                   ar_bx=ar_bx, ar_lambda=ar_lambda, ar_w_out=ar_w_out, gm_w_in=gm_w_in, gm_b_in=gm_b_in, gm_v_g=gm_v_g,
                   gm_v_b=gm_v_b, gm_w_sp=gm_w_sp, gm_b_sp=gm_b_sp, gm_w_out=gm_w_out)
    m_in = dict(c_ctx=m_c_ctx, w_mod=m_w_mod, b_mod=m_b_mod, norm_g=m_norm_g, w_ff_in=m_w_ff_in, w_ff_out=m_w_ff_out,
                ar_w_in=m_ar_w_in, ar_q_g=m_ar_q_g, ar_k_g=m_ar_k_g, ar_conv_w=m_ar_conv_w, ar_conv_b=m_ar_conv_b, ar_wa=m_ar_wa,
                ar_ba=m_ar_ba, ar_wx=m_ar_wx, ar_bx=m_ar_bx, ar_lambda=m_ar_lambda, ar_w_out=m_ar_w_out, gm_w_in=m_gm_w_in,
                gm_b_in=m_gm_b_in, gm_v_g=m_gm_v_g, gm_v_b=m_gm_v_b, gm_w_sp=m_gm_w_sp, gm_b_sp=m_gm_b_sp, gm_w_out=m_gm_w_out)
    v_in = dict(c_ctx=v_c_ctx, w_mod=v_w_mod, b_mod=v_b_mod, norm_g=v_norm_g, w_ff_in=v_w_ff_in, w_ff_out=v_w_ff_out,
                ar_w_in=v_ar_w_in, ar_q_g=v_ar_q_g, ar_k_g=v_ar_k_g, ar_conv_w=v_ar_conv_w, ar_conv_b=v_ar_conv_b, ar_wa=v_ar_wa,
                ar_ba=v_ar_ba, ar_wx=v_ar_wx, ar_bx=v_ar_bx, ar_lambda=v_ar_lambda, ar_w_out=v_ar_w_out, gm_w_in=v_gm_w_in,
                gm_b_in=v_gm_b_in, gm_v_g=v_gm_v_g, gm_v_b=v_gm_v_b, gm_w_sp=v_gm_w_sp, gm_b_sp=v_gm_b_sp, gm_w_out=v_gm_w_out)

    xi, yi, ci = lax.axis_index("x"), lax.axis_index("y"), lax.axis_index("c")
    chip = 2 * xi + yi
    dev = 4 * xi + 2 * yi + ci
    place = jnp.stack([chip, ci]).astype(jnp.int32)
    n_lat, d = x.shape[1], x.shape[2]
    d6 = 6 * d
    cols_mod = w_mod.shape[2]

    mine = [c, norm_g, ar_conv_w[0], ar_ba[0], ar_bx[0], ar_lambda[0], gm_b_in, gm_v_g, gm_v_b]
    gathered = _allgather8([_pack(mine)], "gather_small_params")[0]
    parts = _unpack_devices(gathered, [a.shape for a in mine])
    c_all = parts[0].reshape(N_DEV, d)
    sp = {'norm_g': _chip_concat(parts[1], 2), 'q_g': ar_q_g, 'k_g': ar_k_g, 'conv_w': _chip_concat(parts[2], 1),
          'conv_b': ar_conv_b, 'wa': ar_wa[0], 'ba': _chip_concat(parts[3], 1), 'wx': ar_wx[0], 'bx': _chip_concat(parts[4], 1),
          'lam': _chip_concat(parts[5], 1), 'gm_b_in': _chip_concat(parts[6], 1), 'gm_v_g': _chip_concat(parts[7], 1),
          'gm_v_b': _chip_concat(parts[8], 1), 'gm_w_sp': gm_w_sp[0], 'gm_b_sp': gm_b_sp[0]}

    def mod_operand(c_rows, cc):
        row = lax.broadcasted_iota(jnp.int32, (MOD_ROWS - N_DEV, d), 0)
        lower = jnp.where(row == 0, jnp.broadcast_to(_silu(cc), (MOD_ROWS - N_DEV, d)), 0.0)
        sig = _sigmoid(cc)
        return jnp.concatenate([_silu(c_rows), lower], axis=0), sig * (1.0 + cc * (1.0 - sig))

    s_mod, dsilu_ctx = _small(mod_operand, [((MOD_ROWS, d), F32), ((1, d), F32)], c_all, c_ctx[None, :], name="mod_operand")
    b_mod_mine = lax.dynamic_slice(b_mod, (0, chip * cols_mod), (2, cols_mod))
    mod = [_matmul(s_mod, w_mod[i], kind='nn', bias=b_mod_mine[i][None, :], out_dtype=F32, name=f"mod_fwd_{i}") for i in range(2)]
    mod_all = _allgather8([jnp.concatenate(mod, axis=0)], "gather_mod")[0]
    mod_all = _chip_concat(mod_all, 1).reshape(2, MOD_ROWS, d6)
    ml = [jnp.split(lax.dynamic_slice(mod_all[i], (dev, 0), (1, d6)), 6, axis=1) for i in range(2)]
    mc0 = jnp.split(mod_all[0, N_DEV:N_DEV + 1], 6, axis=1)[:2]

    names = ('w_ff_in', 'w_ff_out', 'ar_w_in', 'ar_w_out', 'gm_w_in', 'gm_w_out')
    keys = {'w_ff_in': ('ff_in0', 'ff_in1'), 'w_ff_out': ('ff_out0', 'ff_out1'), 'ar_w_in': ('ar_in',), 'ar_w_out': ('ar_out',),
            'gm_w_in': ('gm_in',), 'gm_w_out': ('gm_out',)}
    shards = {key: _cast_shard(weights[n], place, layer, f"cast_{key}") for n in names for layer, key in enumerate(keys[n])}
    link = _MeshLink(place, shards)

    grad_x, small = _local_step(x[0], ctx[0], loss_target[0], ml, mc0, sp, link)
    reduced = dict(zip(names, link.finish(grad_x)))

    small_list = [small[k] for k in SMALL_ORDER]
    small8 = _allgather8([_pack(small_list)], "gather_small_grads")[0]
    total = _unpack(_sum_devices(small8).reshape(-1), [a.shape for a in small_list])
    total = dict(zip(SMALL_ORDER, total))
    per_dev = _unpack_devices(small8, [(d6,), (d6,)])
    pad_rows = jnp.zeros((MOD_ROWS - N_DEV - 1, d6), F32)
    d_mod = [jnp.concatenate([per_dev[0], total['d_mc0'], pad_rows], axis=0),
             jnp.concatenate([per_dev[1], jnp.zeros((MOD_ROWS - N_DEV, d6), F32)], axis=0)]
    d_mod_mine = [lax.dynamic_slice(dm, (0, chip * cols_mod), (MOD_ROWS, cols_mod)) for dm in d_mod]
    g_w_mod = jnp.stack([_matmul(s_mod, d_mod_mine[i], kind='tn', out_dtype=F32, name=f"mod_dw_{i}") for i in range(2)])
    d_s_part = _matmul(d_mod_mine[0], w_mod[0], kind='nt', out_dtype=F32, name="mod_ds")
    d_s_all = _allgather8([d_s_part[N_DEV:]], "gather_mod_ds")[0]

    def c_ctx_grad(parts_, dsilu):
        acc = parts_[0, 0:1]
        for q in range(1, N_CHIPS):
            acc = acc + parts_[2 * q, 0:1]
        return (acc * dsilu,)

    g_c_ctx = _small(c_ctx_grad, [((1, d), F32)], d_s_all, dsilu_ctx, name="c_ctx_grad")[0].reshape(d)

    def mine_of(full_grad, axis, n_shard):
        return lax.dynamic_slice_in_dim(full_grad, chip * n_shard, n_shard, axis=axis)

    grads_out = {
        'c_ctx': g_c_ctx, 'w_mod': g_w_mod,
        'b_mod': jnp.stack([total['d_ml0'][0] + total['d_mc0'][0], total['d_ml1'][0]]),
        'norm_g': mine_of(total['norm_g'], 2, norm_g.shape[2]),
        'w_ff_in': reduced['w_ff_in'], 'w_ff_out': reduced['w_ff_out'], 'ar_w_in': reduced['ar_w_in'],
        'ar_q_g': total['q_g'], 'ar_k_g': total['k_g'], 'ar_conv_w': mine_of(total['conv_w'], 1, ar_conv_w.shape[2])[None],
        'ar_conv_b': total['conv_b'], 'ar_wa': total['wa'][None], 'ar_ba': mine_of(total['ba'], 1, ar_ba.shape[2])[None],
        'ar_wx': total['wx'][None], 'ar_bx': mine_of(total['bx'], 1, ar_bx.shape[2])[None],
        'ar_lambda': mine_of(total['lam'], 1, ar_lambda.shape[2])[None], 'ar_w_out': reduced['ar_w_out'],
        'gm_w_in': reduced['gm_w_in'], 'gm_b_in': mine_of(total['gm_b_in'], 1, gm_b_in.shape[1]),
        'gm_v_g': mine_of(total['gm_v_g'], 1, gm_v_g.shape[1]), 'gm_v_b': mine_of(total['gm_v_b'], 1, gm_v_b.shape[1]),
        'gm_w_sp': total['gm_w_sp'][None], 'gm_b_sp': total['gm_b_sp'][None], 'gm_w_out': reduced['gm_w_out'],
    }
    order = list(weights)
    stepped = [_adamw(weights[n], grads_out[n].reshape(weights[n].shape), m_in[n], v_in[n], f"adamw_{n}") for n in order]
    loss = total['loss'].reshape(())
    return (loss, grad_x[None], *[s[0] for s in stepped], *[s[1] for s in stepped], *[s[2] for s in stepped],
            *[s[3] for s in stepped])
```

```python
import functools
import math

import jax
import jax.numpy as jnp
from jax import lax
from jax.experimental import pallas as pl
from jax.experimental.pallas import tpu as pltpu

F32 = jnp.float32
BF16 = jnp.bfloat16
MESH = pl.DeviceIdType.MESH
ANY = pl.BlockSpec(memory_space=pl.ANY)

VMEM_LIMIT_BYTES = 52 * 1024 * 1024
LANES = 128
N_CHIPS = 4
N_DEV = 8

HEAD_DIM = 128
N_HEADS = 8
N_KV = 2
GROUP = N_HEADS // N_KV
ATTN_W = N_HEADS * HEAD_DIM
KV_W = N_KV * HEAD_DIM
D_RNN = 1024
RNN_BLOCKS = 8
RNN_BW = D_RNN // RNN_BLOCKS
CONV_W = 4
RG_C = 8.0
GRID_W = 64
ROPE_THETA = 10000.0
ROPE_PAIRS = HEAD_DIM // 4
GM_GROUPS = 16
CHUNK = 128
EPS = 1e-6
ADAM_LR, ADAM_B1, ADAM_B2, ADAM_EPS, ADAM_WD, ADAM_STEP = 0.001, 0.9, 0.999, 1e-08, 0.01, 10
GELU_C = math.sqrt(2.0 / math.pi)
LOG2E = math.log2(math.e)


def _params(sem=None):
    return pltpu.CompilerParams(dimension_semantics=sem, vmem_limit_bytes=VMEM_LIMIT_BYTES)


def _tile(dim, pref, unit):
    best = None
    t = unit
    while t <= min(dim, pref):
        if dim % t == 0:
            best = t
        t += unit
    return best if best is not None else dim


def _full(shape):
    nd = len(shape)
    return pl.BlockSpec(shape, lambda *_: (0,) * nd)


def _blocked_map(split, per_q):
    if split == 'n':
        return lambda r, c: (c // per_q, r, c % per_q)
    if split == 'k':
        return lambda r, c: (r // per_q, r % per_q, c)
    return lambda r, c: (r, c)


def _logical_shape(arr, split):
    if split == 'n':
        return arr.shape[1], arr.shape[0] * arr.shape[2]
    if split == 'k':
        return arr.shape[0] * arr.shape[1], arr.shape[2]
    return arr.shape


def _matmul(a, b, *, kind, name, out_dtype, b_split=None, out_split=None, bias=None, epilogue=None, extra=None,
            pref=(1024, 1024, 2048)):
    b_rows, b_cols = _logical_shape(b, b_split)
    if kind == 'nn':
        m, kc = a.shape
        n = b_cols
        assert b_rows == kc
    elif kind == 'nt':
        m, kc = a.shape
        n = b_rows
        assert b_cols == kc
    else:
        kc, m = a.shape
        n = b_cols
        assert b_rows == kc
    b_row_ext = b.shape[1] if b_split == 'k' else b_rows
    b_col_ext = b.shape[2] if b_split == 'n' else b_cols
    out_row_ext = m // N_CHIPS if out_split == 'k' else m
    out_col_ext = n // N_CHIPS if out_split == 'n' else n
    if kind == 'nn':
        ti = _tile(min(m, out_row_ext), pref[0], 16)
        tj = _tile(math.gcd(b_col_ext, out_col_ext), pref[1], LANES)
        tl = _tile(b_row_ext, pref[2], LANES)
        a_spec = pl.BlockSpec((ti, tl), lambda i, j, l: (i, l))
        b_tile, b_rc = (tl, tj), (lambda i, j, l: (l, j))
        dims = (((1,), (0,)), ((), ()))
    elif kind == 'nt':
        ti = _tile(min(m, out_row_ext), pref[0], 16)
        tj = _tile(math.gcd(b_row_ext, out_col_ext), pref[1], LANES)
        tl = _tile(b_col_ext, pref[2], LANES)
        a_spec = pl.BlockSpec((ti, tl), lambda i, j, l: (i, l))
        b_tile, b_rc = (tj, tl), (lambda i, j, l: (j, l))
        dims = (((1,), (1,)), ((), ()))
    else:
        ti = _tile(out_row_ext, pref[0], LANES)
        tj = _tile(math.gcd(b_col_ext, out_col_ext), pref[1], LANES)
        tl = _tile(b_row_ext, pref[2], 16)
        a_spec = pl.BlockSpec((tl, ti), lambda i, j, l: (l, i))
        b_tile, b_rc = (tl, tj), (lambda i, j, l: (l, j))
        dims = (((0,), (0,)), ((), ()))
    grid = (m // ti, n // tj, kc // tl)
    n_l = grid[2]

    if b_split is None:
        b_spec = pl.BlockSpec(b_tile, b_rc)
    else:
        per_q = (b.shape[2] // b_tile[1]) if b_split == 'n' else (b.shape[1] // b_tile[0])
        bmap = _blocked_map(b_split, per_q)
        b_spec = pl.BlockSpec((None,) + b_tile, lambda i, j, l: bmap(*b_rc(i, j, l)))
    if out_split is None:
        out_shape2 = (m, n)
        o_spec = pl.BlockSpec((ti, tj), lambda i, j, l: (i, j))
    else:
        out_shape2 = (N_CHIPS, m // N_CHIPS, n) if out_split == 'k' else (N_CHIPS, m, n // N_CHIPS)
        per_q = (out_shape2[2] // tj) if out_split == 'n' else (out_shape2[1] // ti)
        omap = _blocked_map(out_split, per_q)
        o_spec = pl.BlockSpec((None, ti, tj), lambda i, j, l: omap(i, j))

    in_specs = [a_spec, b_spec]
    operands = [a, b]
    if bias is not None:
        in_specs.append(pl.BlockSpec((1, tj), lambda i, j, l: (0, j)))
        operands.append(bias)
    if extra is not None:
        in_specs.append(pl.BlockSpec((ti, tj), lambda i, j, l: (i, j)))
        operands.append(extra)
    if epilogue == 'relu2':
        out_shape = (jax.ShapeDtypeStruct(out_shape2, out_dtype), jax.ShapeDtypeStruct(out_shape2, out_dtype))
        out_specs = (o_spec, o_spec)
    else:
        out_shape = jax.ShapeDtypeStruct(out_shape2, out_dtype)
        out_specs = o_spec
    has_bias, has_extra = bias is not None, extra is not None

    def body(*refs):
        a_ref, b_ref = refs[0], refs[1]
        pos = 2
        bias_ref = extra_ref = None
        if has_bias:
            bias_ref = refs[pos]
            pos += 1
        if has_extra:
            extra_ref = refs[pos]
            pos += 1
        outs = refs[pos:] if n_l == 1 else refs[pos:-1]

        def finish(acc):
            if has_bias:
                acc = acc + bias_ref[...]
            if epilogue == 'relu2':
                r = jnp.maximum(acc, 0.0)
                outs[0][...] = r.astype(outs[0].dtype)
                outs[1][...] = (r * r).astype(outs[1].dtype)
            elif epilogue == 'times2x':
                outs[0][...] = (acc * (2.0 * extra_ref[...].astype(F32))).astype(outs[0].dtype)
            else:
                outs[0][...] = acc.astype(outs[0].dtype)

        def product():
            return lax.dot_general(a_ref[...].astype(BF16), b_ref[...].astype(BF16), dims, preferred_element_type=F32)

        if n_l == 1:
            finish(product())
            return
        acc_ref = refs[-1]
        step = pl.program_id(2)

        @pl.when(step == 0)
        def _():
            acc_ref[...] = jnp.zeros_like(acc_ref)

        acc_ref[...] += product()

        @pl.when(step == n_l - 1)
        def _():
            finish(acc_ref[...])

    return pl.pallas_call(
        body, name=name, grid=grid, in_specs=in_specs, out_specs=out_specs, out_shape=out_shape,
        scratch_shapes=[] if n_l == 1 else [pltpu.VMEM((ti, tj), F32)],
        compiler_params=_params(("parallel", "parallel", "arbitrary")),
    )(*operands)


def _small(fn, out_shapes, *arrays, name):
    n_in = len(arrays)

    def body(*refs):
        res = fn(*[r[...] for r in refs[:n_in]])
        for o_ref, v in zip(refs[n_in:], res):
            o_ref[...] = v.astype(o_ref.dtype)

    return pl.pallas_call(
        body, name=name, out_shape=tuple(jax.ShapeDtypeStruct(s, d) for s, d in out_shapes),
        in_specs=[_full(a.shape) for a in arrays], out_specs=tuple(_full(s) for s, _ in out_shapes), grid=(1,),
        compiler_params=_params(("arbitrary",)),
    )(*arrays)


def _rows_tile(rows, cols, itemsize=4, budget=2 * 1024 * 1024):
    return _tile(rows, max(16, budget // (cols * itemsize)), 16)


def _rowwise(fn, out_dtypes, *arrays, name):
    rows, cols = arrays[0].shape
    tr = _rows_tile(rows, cols)
    n_in = len(arrays)

    def body(*refs):
        res = fn(*[r[...] for r in refs[:n_in]])
        for o_ref, v in zip(refs[n_in:], res):
            o_ref[...] = v.astype(o_ref.dtype)

    spec = pl.BlockSpec((tr, cols), lambda i: (i, 0))
    return pl.pallas_call(
        body, name=name, grid=(rows // tr,), in_specs=[spec] * n_in, out_specs=tuple(spec for _ in out_dtypes),
        out_shape=tuple(jax.ShapeDtypeStruct((rows, cols), d) for d in out_dtypes),
        compiler_params=_params(("parallel",)),
    )(*arrays)


def _as2d(a):
    return a.reshape(1, a.size) if a.ndim < 2 else a.reshape(-1, a.shape[-1])


def _cast_shard(w, place, layer, name):
    _, rows, cols = w.shape
    tr = _rows_tile(rows, cols)

    def body(place_ref, w_ref, o_ref):
        o_ref[...] = w_ref[...].astype(o_ref.dtype)

    return pl.pallas_call(
        body, name=name, out_shape=jax.ShapeDtypeStruct((N_CHIPS, rows, cols), BF16),
        grid_spec=pltpu.PrefetchScalarGridSpec(
            num_scalar_prefetch=1, grid=(rows // tr,),
            in_specs=[pl.BlockSpec((None, tr, cols), lambda i, pr: (layer, i, 0))],
            out_specs=pl.BlockSpec((None, tr, cols), lambda i, pr: (pr[0], i, 0))),
        compiler_params=_params(("parallel",)),
    )(place, w)


def _norm_fwd(x, g, a, b=None, res=None, *, out_dtype, name):
    rows, d = x.shape
    tr = _rows_tile(rows, d)
    has_b, has_res = b is not None, res is not None

    def body(*refs):
        x_ref, g_ref, a_ref = refs[:3]
        pos = 3
        xv = x_ref[...]
        rstd = lax.rsqrt(jnp.mean(xv * xv, axis=-1, keepdims=True) + EPS)
        y = (xv * rstd * g_ref[...]) * a_ref[...]
        if has_b:
            y = y + refs[pos][...]
            pos += 1
        if has_res:
            y = y + refs[pos][...]
            pos += 1
        refs[pos][...] = y.astype(refs[pos].dtype)

    row = pl.BlockSpec((tr, d), lambda i: (i, 0))
    vec = pl.BlockSpec((1, d), lambda i: (0, 0))
    operands, specs = [x, g, a], [row, vec, vec]
    if has_b:
        operands.append(b)
        specs.append(vec)
    if has_res:
        operands.append(res)
        specs.append(row)
    return pl.pallas_call(
        body, name=name, grid=(rows // tr,), in_specs=specs, out_specs=row,
        out_shape=jax.ShapeDtypeStruct((rows, d), out_dtype), compiler_params=_params(("parallel",)),
    )(*operands)


def _norm_bwd(dy, x, g, a, extra=None, *, out_dtype, name):
    rows, d = x.shape
    tr = _rows_tile(rows, d)
    has_extra = extra is not None

    def body(*refs):
        dy_ref, x_ref, g_ref, a_ref = refs[:4]
        pos = 4
        extra_ref = None
        if has_extra:
            extra_ref = refs[pos]
            pos += 1
        dx_ref, dg_ref, da_ref, db_ref = refs[pos:pos + 4]

        @pl.when(pl.program_id(0) == 0)
        def _():
            dg_ref[...] = jnp.zeros_like(dg_ref)
            da_ref[...] = jnp.zeros_like(da_ref)
            db_ref[...] = jnp.zeros_like(db_ref)

        xv = x_ref[...]
        dyv = dy_ref[...].astype(F32)
        rstd = lax.rsqrt(jnp.mean(xv * xv, axis=-1, keepdims=True) + EPS)
        nrm = xv * rstd
        gv = g_ref[...]
        da_ref[...] += jnp.sum(dyv * (nrm * gv), axis=0, keepdims=True)
        db_ref[...] += jnp.sum(dyv, axis=0, keepdims=True)
        dt = dyv * a_ref[...]
        dg_ref[...] += jnp.sum(dt * nrm, axis=0, keepdims=True)
        dn = dt * gv
        dx = rstd * (dn - nrm * jnp.mean(dn * nrm, axis=-1, keepdims=True))
        if has_extra:
            dx = dx + extra_ref[...]
        dx_ref[...] = dx.astype(dx_ref.dtype)

    row = pl.BlockSpec((tr, d), lambda i: (i, 0))
    vec = pl.BlockSpec((1, d), lambda i: (0, 0))
    operands, specs = [dy, x, g, a], [row, row, vec, vec]
    if has_extra:
        operands.append(extra)
        specs.append(row)
    vshape = jax.ShapeDtypeStruct((1, d), F32)
    return pl.pallas_call(
        body, name=name, grid=(rows // tr,), in_specs=specs, out_specs=(row, vec, vec, vec),
        out_shape=(jax.ShapeDtypeStruct((rows, d), out_dtype), vshape, vshape, vshape),
        compiler_params=_params(("arbitrary",)),
    )(*operands)


def _loss_head(y, target):
    rows, d = y.shape
    tr = _rows_tile(rows, d)

    def body(y_ref, t_ref, dy_ref, loss_ref):
        @pl.when(pl.program_id(0) == 0)
        def _():
            loss_ref[...] = jnp.zeros_like(loss_ref)

        err = y_ref[...] - t_ref[...]
        dy_ref[...] = err * (1.0 / d)
        loss_ref[...] += jnp.sum(jnp.sum(err * err, axis=-1, keepdims=True), axis=0, keepdims=True) * (0.5 / d)

    row = pl.BlockSpec((tr, d), lambda i: (i, 0))
    return pl.pallas_call(
        body, name="loss_head", grid=(rows // tr,), in_specs=[row, row], out_specs=(row, _full((1, 1))),
        out_shape=(jax.ShapeDtypeStruct((rows, d), F32), jax.ShapeDtypeStruct((1, 1), F32)),
        compiler_params=_params(("arbitrary",)),
    )(y, target)


def _rope_partner(v):
    lane = lax.broadcasted_iota(jnp.int32, v.shape, 1)
    up = pltpu.roll(v, HEAD_DIM - ROPE_PAIRS, 1)
    down = pltpu.roll(v, ROPE_PAIRS, 1)
    return jnp.where((lane % (2 * ROPE_PAIRS)) < ROPE_PAIRS, up, down)


def _qk_fwd(proj, q_g, k_g, cos, sin, *, name):
    rows = proj.shape[0]
    tr = _tile(rows, 256, 16)
    width = ATTN_W + 2 * KV_W

    def body(p_ref, qg_ref, kg_ref, cos_ref, sin_ref, q_ref, k_ref, v_ref):
        cosv, sinv = cos_ref[...], sin_ref[...]
        for h in range(N_HEADS + N_KV):
            xv = p_ref[:, h * HEAD_DIM:(h + 1) * HEAD_DIM]
            gain = qg_ref[...] if h < N_HEADS else kg_ref[...]
            t = xv * lax.rsqrt(jnp.mean(xv * xv, axis=-1, keepdims=True) + EPS) * gain
            y = t * cosv + _rope_partner(t) * sinv
            if h < N_HEADS:
                q_ref[:, h * HEAD_DIM:(h + 1) * HEAD_DIM] = y.astype(BF16)
            else:
                k_ref[:, (h - N_HEADS) * HEAD_DIM:(h - N_HEADS + 1) * HEAD_DIM] = y.astype(BF16)
        v_ref[...] = p_ref[:, ATTN_W + KV_W:width].astype(BF16)

    vec = _full((1, HEAD_DIM))
    tab = pl.BlockSpec((tr, HEAD_DIM), lambda i: (i, 0))
    return pl.pallas_call(
        body, name=name, grid=(rows // tr,),
        in_specs=[pl.BlockSpec((tr, width), lambda i: (i, 0)), vec, vec, tab, tab],
        out_specs=(pl.BlockSpec((tr, ATTN_W), lambda i: (i, 0)), pl.BlockSpec((tr, KV_W), lambda i: (i, 0)),
                   pl.BlockSpec((tr, KV_W), lambda i: (i, 0))),
        out_shape=(jax.ShapeDtypeStruct((rows, ATTN_W), BF16), jax.ShapeDtypeStruct((rows, KV_W), BF16),
                   jax.ShapeDtypeStruct((rows, KV_W), BF16)),
        compiler_params=_params(("parallel",)),
    )(proj, q_g, k_g, cos, sin)


def _qk_bwd(dq, dk, proj, q_g, k_g, cos, sin, *, name):
    rows = proj.shape[0]
    tr = _tile(rows, 256, 16)
    width = ATTN_W + KV_W
    has_q = dq is not None

    def body(*refs):
        pos = 0
        dq_ref = None
        if has_q:
            dq_ref = refs[0]
            pos = 1
        dk_ref, p_ref, qg_ref, kg_ref, cos_ref, sin_ref, dp_ref, dqg_ref, dkg_ref = refs[pos:pos + 9]

        @pl.when(pl.program_id(0) == 0)
        def _():
            dqg_ref[...] = jnp.zeros_like(dqg_ref)
            dkg_ref[...] = jnp.zeros_like(dkg_ref)

        cosv, sinv = cos_ref[...], sin_ref[...]
        for h in range(N_HEADS + N_KV):
            cols = slice(h * HEAD_DIM, (h + 1) * HEAD_DIM)
            if h < N_HEADS and not has_q:
                dp_ref[:, cols] = jnp.zeros((tr, HEAD_DIM), dp_ref.dtype)
                continue
            if h < N_HEADS:
                dyv, gain, dgain_ref = dq_ref[:, cols], qg_ref[...], dqg_ref
            else:
                hk = h - N_HEADS
                dyv, gain, dgain_ref = dk_ref[:, hk * HEAD_DIM:(hk + 1) * HEAD_DIM], kg_ref[...], dkg_ref
            dyv = dyv.astype(F32)
            dt = dyv * cosv + _rope_partner(dyv * sinv)
            xv = p_ref[:, cols]
            rstd = lax.rsqrt(jnp.mean(xv * xv, axis=-1, keepdims=True) + EPS)
            nrm = xv * rstd
            dgain_ref[...] += jnp.sum(dt * nrm, axis=0, keepdims=True)
            dn = dt * gain
            dp_ref[:, cols] = (rstd * (dn - nrm * jnp.mean(dn * nrm, axis=-1, keepdims=True))).astype(dp_ref.dtype)

    vec = _full((1, HEAD_DIM))
    tab = pl.BlockSpec((tr, HEAD_DIM), lambda i: (i, 0))
    operands = ([dq] if has_q else []) + [dk, proj, q_g, k_g, cos, sin]
    specs = ([pl.BlockSpec((tr, ATTN_W), lambda i: (i, 0))] if has_q else []) + [
        pl.BlockSpec((tr, KV_W), lambda i: (i, 0)), pl.BlockSpec((tr, width), lambda i: (i, 0)), vec, vec, tab, tab]
    return pl.pallas_call(
        body, name=name, grid=(rows // tr,), in_specs=specs,
        out_specs=(pl.BlockSpec((tr, width), lambda i: (i, 0)), vec, vec),
        out_shape=(jax.ShapeDtypeStruct((rows, width), BF16), jax.ShapeDtypeStruct((1, HEAD_DIM), F32),
                   jax.ShapeDtypeStruct((1, HEAD_DIM), F32)),
        compiler_params=_params(("arbitrary",)),
    )(*operands)


def _attn_fwd(q, k, v):
    n_q, n_k = q.shape[0], k.shape[0]
    tq = _tile(n_q, 256, 16)
    gw = GROUP * HEAD_DIM
    scale = HEAD_DIM ** -0.5

    def body(q_ref, k_ref, v_ref, o_ref, lse_ref):
        kv, vv = k_ref[...], v_ref[...]
        for g in range(GROUP):
            cols = slice(g * HEAD_DIM, (g + 1) * HEAD_DIM)
            s = lax.dot_general(q_ref[:, cols], kv, (((1,), (1,)), ((), ())), preferred_element_type=F32) * (scale * LOG2E)
            m = jnp.max(s, axis=-1, keepdims=True)
            p = jnp.exp2(s - m)
            l = jnp.sum(p, axis=-1, keepdims=True)
            o = jnp.dot(p.astype(BF16), vv, preferred_element_type=F32) / l
            o_ref[:, cols] = o.astype(o_ref.dtype)
            lse_ref[:, g:g + 1] = m + jnp.log(l) * LOG2E

    return pl.pallas_call(
        body, name="attn_fwd", grid=(N_KV, n_q // tq),
        in_specs=[pl.BlockSpec((tq, gw), lambda h, i: (i, h)), pl.BlockSpec((n_k, HEAD_DIM), lambda h, i: (0, h)),
                  pl.BlockSpec((n_k, HEAD_DIM), lambda h, i: (0, h))],
        out_specs=(pl.BlockSpec((tq, gw), lambda h, i: (i, h)), pl.BlockSpec((None, tq, GROUP), lambda h, i: (h, i, 0))),
        out_shape=(jax.ShapeDtypeStruct((n_q, ATTN_W), BF16), jax.ShapeDtypeStruct((N_KV, n_q, GROUP), F32)),
        compiler_params=_params(("parallel", "parallel")),
    )(q, k, v)


def _attn_bwd(q, k, v, o, lse, do):
    n_q, n_k = q.shape[0], k.shape[0]
    tq = _tile(n_q, 256, 16)
    gw = GROUP * HEAD_DIM
    scale = HEAD_DIM ** -0.5

    def body(q_ref, k_ref, v_ref, o_ref, lse_ref, do_ref, dq_ref, dk_ref, dv_ref):
        @pl.when(pl.program_id(1) == 0)
        def _():
            dk_ref[...] = jnp.zeros_like(dk_ref)
            dv_ref[...] = jnp.zeros_like(dv_ref)

        kv, vv = k_ref[...], v_ref[...]
        for g in range(GROUP):
            cols = slice(g * HEAD_DIM, (g + 1) * HEAD_DIM)
            qg = q_ref[:, cols]
            dof = do_ref[:, cols].astype(F32)
            dog = dof.astype(BF16)
            s = lax.dot_general(qg, kv, (((1,), (1,)), ((), ())), preferred_element_type=F32) * (scale * LOG2E)
            p = jnp.exp2(s - lse_ref[:, g:g + 1])
            delta = jnp.sum(dof * o_ref[:, cols].astype(F32), axis=-1, keepdims=True)
            dp = lax.dot_general(dog, vv, (((1,), (1,)), ((), ())), preferred_element_type=F32)
            ds = (p * (dp - delta) * scale).astype(BF16)
            pb = p.astype(BF16)
            dq_ref[:, cols] = jnp.dot(ds, kv, preferred_element_type=F32)
            dk_ref[...] += lax.dot_general(ds, qg, (((0,), (0,)), ((), ())), preferred_element_type=F32)
            dv_ref[...] += lax.dot_general(pb, dog, (((0,), (0,)), ((), ())), preferred_element_type=F32)

    qspec = pl.BlockSpec((tq, gw), lambda h, i: (i, h))
    kspec = pl.BlockSpec((n_k, HEAD_DIM), lambda h, i: (0, h))
    return pl.pallas_call(
        body, name="attn_bwd", grid=(N_KV, n_q // tq),
        in_specs=[qspec, kspec, kspec, qspec, pl.BlockSpec((None, tq, GROUP), lambda h, i: (h, i, 0)), qspec],
        out_specs=(qspec, kspec, kspec),
        out_shape=(jax.ShapeDtypeStruct((n_q, ATTN_W), F32), jax.ShapeDtypeStruct((n_k, KV_W), F32),
                   jax.ShapeDtypeStruct((n_k, KV_W), F32)),
        compiler_params=_params(("parallel", "arbitrary")),
    )(q, k, v, o, lse, do)


CONV_COLS = 256
XR_COL0 = ATTN_W + 2 * KV_W


def _shift_rows(v, off):
    if off == 0:
        return v
    n = v.shape[0]
    rolled = pltpu.roll(v, (-off) % n, 0)
    t = lax.broadcasted_iota(jnp.int32, v.shape, 0)
    keep = (t + off >= 0) & (t + off < n)
    return jnp.where(keep, rolled, 0.0)


def _conv_fwd(proj, w, b, *, name):
    rows = proj.shape[0]
    blk0 = XR_COL0 // CONV_COLS

    def body(x_ref, w_ref, b_ref, y_ref):
        xv = x_ref[...]
        y = b_ref[...] + jnp.zeros_like(xv)
        for j in range(CONV_W):
            y = y + _shift_rows(xv, j - CONV_W // 2) * w_ref[j:j + 1, :]
        y_ref[...] = y

    return pl.pallas_call(
        body, name=name, grid=(D_RNN // CONV_COLS,),
        in_specs=[pl.BlockSpec((rows, CONV_COLS), lambda i: (0, blk0 + i)),
                  pl.BlockSpec((CONV_W, CONV_COLS), lambda i: (0, i)), pl.BlockSpec((1, CONV_COLS), lambda i: (0, i))],
        out_specs=pl.BlockSpec((rows, CONV_COLS), lambda i: (0, i)),
        out_shape=jax.ShapeDtypeStruct((rows, D_RNN), F32), compiler_params=_params(("parallel",)),
    )(proj, w, b)


def _conv_bwd(d1, d2, proj, w, *, name):
    rows = proj.shape[0]
    blk0 = XR_COL0 // CONV_COLS

    def body(d1_ref, d2_ref, x_ref, w_ref, dx_ref, dw_ref, db_ref):
        dv = d1_ref[...] + d2_ref[...]
        xv = x_ref[...]
        dx = jnp.zeros_like(dv)
        for j in range(CONV_W):
            off = j - CONV_W // 2
            dx = dx + _shift_rows(dv, -off) * w_ref[j:j + 1, :]
            dw_ref[j:j + 1, :] = jnp.sum(dv * _shift_rows(xv, off), axis=0, keepdims=True)
        dx_ref[...] = dx.astype(dx_ref.dtype)
        db_ref[...] = jnp.sum(dv, axis=0, keepdims=True)

    col = pl.BlockSpec((rows, CONV_COLS), lambda i: (0, i))
    return pl.pallas_call(
        body, name=name, grid=(D_RNN // CONV_COLS,),
        in_specs=[col, col, pl.BlockSpec((rows, CONV_COLS), lambda i: (0, blk0 + i)),
                  pl.BlockSpec((CONV_W, CONV_COLS), lambda i: (0, i))],
        out_specs=(col, pl.BlockSpec((CONV_W, CONV_COLS), lambda i: (0, i)), pl.BlockSpec((1, CONV_COLS), lambda i: (0, i))),
        out_shape=(jax.ShapeDtypeStruct((rows, D_RNN), BF16), jax.ShapeDtypeStruct((CONV_W, D_RNN), F32),
                   jax.ShapeDtypeStruct((1, D_RNN), F32)),
        compiler_params=_params(("parallel",)),
    )(d1, d2, proj, w)


RNN_TB = 256
SCAN_ROWS = 8


def _sigmoid(z):
    return 1.0 / (1.0 + jnp.exp(-z))


def _softplus(z):
    return jnp.maximum(z, 0.0) + jnp.log(1.0 + jnp.exp(-jnp.abs(z)))


def _one_minus_exp(y):
    series = -y * (1.0 + y * (0.5 + y * (1.0 / 6.0 + y * (1.0 / 24.0))))
    return jnp.where(y > -0.03, series, 1.0 - jnp.exp(y))


def _rglru_gates(xv, wa_ref, ba_ref, wx_ref, bx_ref, lam_ref):
    xb = xv.astype(BF16)
    zr = jnp.concatenate([jnp.dot(xb[:, n * RNN_BW:(n + 1) * RNN_BW], wa_ref[n].astype(BF16),
                                  preferred_element_type=F32) for n in range(RNN_BLOCKS)], axis=-1) + ba_ref[...]
    zi = jnp.concatenate([jnp.dot(xb[:, n * RNN_BW:(n + 1) * RNN_BW], wx_ref[n].astype(BF16),
                                  preferred_element_type=F32) for n in range(RNN_BLOCKS)], axis=-1) + bx_ref[...]
    r = _sigmoid(zr)
    gi = _sigmoid(zi)
    sp = _softplus(-lam_ref[...])
    log_a = -RG_C * r * sp
    a = jnp.exp(log_a)
    s = jnp.sqrt(_one_minus_exp(2.0 * log_a))
    return r, gi, sp, a, s


def _scan_rows(n_rows, reverse, step_fn, carry):
    groups = n_rows // SCAN_ROWS

    def trip(gidx, carry):
        gi = (groups - 1 - gidx) if reverse else gidx
        base = pl.multiple_of(gi * SCAN_ROWS, SCAN_ROWS)
        return step_fn(base, carry)

    return lax.fori_loop(0, groups, trip, carry)


def _rglru_fwd(xs, wa, ba, wx, bx, lam, *, reverse, name):
    rows = xs.shape[0]
    tb = _tile(rows, RNN_TB, SCAN_ROWS)
    nb = rows // tb
    order = (lambda i: (nb - 1 - i, 0)) if reverse else (lambda i: (i, 0))

    def body(x_ref, wa_ref, ba_ref, wx_ref, bx_ref, lam_ref, h_ref, hp_ref, a_s, b_s, state):
        @pl.when(pl.program_id(0) == 0)
        def _():
            state[...] = jnp.zeros_like(state)

        xv = x_ref[...]
        _, gi, _, a, s = _rglru_gates(xv, wa_ref, ba_ref, wx_ref, bx_ref, lam_ref)
        a_s[...] = a
        b_s[...] = s * (gi * xv)

        def group(base, h):
            av = a_s[pl.ds(base, SCAN_ROWS), :]
            bv = b_s[pl.ds(base, SCAN_ROWS), :]
            outs, prevs = [None] * SCAN_ROWS, [None] * SCAN_ROWS
            for k in range(SCAN_ROWS):
                r_ = SCAN_ROWS - 1 - k if reverse else k
                prevs[r_] = h
                h = av[r_:r_ + 1, :] * h + bv[r_:r_ + 1, :]
                outs[r_] = h
            h_ref[pl.ds(base, SCAN_ROWS), :] = jnp.concatenate(outs, axis=0)
            hp_ref[pl.ds(base, SCAN_ROWS), :] = jnp.concatenate(prevs, axis=0)
            return h

        state[0:1, :] = _scan_rows(tb, reverse, group, state[0:1, :])

    blk = pl.BlockSpec((tb, D_RNN), order)
    wspec = _full((RNN_BLOCKS, RNN_BW, RNN_BW))
    vec = _full((1, D_RNN))
    return pl.pallas_call(
        body, name=name, grid=(nb,), in_specs=[blk, wspec, vec, wspec, vec, vec], out_specs=(blk, blk),
        out_shape=(jax.ShapeDtypeStruct((rows, D_RNN), F32), jax.ShapeDtypeStruct((rows, D_RNN), F32)),
        scratch_shapes=[pltpu.VMEM((tb, D_RNN), F32), pltpu.VMEM((tb, D_RNN), F32), pltpu.VMEM((SCAN_ROWS, D_RNN), F32)],
        compiler_params=_params(("arbitrary",)),
    )(xs, wa, ba, wx, bx, lam)


def _rglru_bwd(xs, h_prev, dh, wa, ba, wx, bx, lam, *, reverse, name):
    rows = xs.shape[0]
    tb = _tile(rows, RNN_TB, SCAN_ROWS)
    nb = rows // tb
    back = not reverse
    order = (lambda i: (nb - 1 - i, 0)) if back else (lambda i: (i, 0))

    def body(x_ref, hp_ref, dh_ref, wa_ref, ba_ref, wx_ref, bx_ref, lam_ref,
             dx_ref, dwa_ref, dba_ref, dwx_ref, dbx_ref, dlam_ref, a_s, g_s, state):
        @pl.when(pl.program_id(0) == 0)
        def _():
            state[...] = jnp.zeros_like(state)
            dwa_ref[...] = jnp.zeros_like(dwa_ref)
            dwx_ref[...] = jnp.zeros_like(dwx_ref)
            dba_ref[...] = jnp.zeros_like(dba_ref)
            dbx_ref[...] = jnp.zeros_like(dbx_ref)
            dlam_ref[...] = jnp.zeros_like(dlam_ref)

        xv = x_ref[...]
        r, gi, sp, a, s = _rglru_gates(xv, wa_ref, ba_ref, wx_ref, bx_ref, lam_ref)
        a_s[...] = a

        def group(base, carry):
            av = a_s[pl.ds(base, SCAN_ROWS), :]
            dv = dh_ref[pl.ds(base, SCAN_ROWS), :]
            outs = [None] * SCAN_ROWS
            for k in range(SCAN_ROWS):
                r_ = SCAN_ROWS - 1 - k if back else k
                gt = dv[r_:r_ + 1, :] + carry
                outs[r_] = gt
                carry = av[r_:r_ + 1, :] * gt
            g_s[pl.ds(base, SCAN_ROWS), :] = jnp.concatenate(outs, axis=0)
            return carry

        state[0:1, :] = _scan_rows(tb, back, group, state[0:1, :])

        gv = g_s[...]
        d_a = gv * hp_ref[...]
        d_s = gv * (gi * xv)
        d_gi = gv * (s * xv)
        dx = gv * (s * gi)
        d_log_a = d_a * a - d_s * (a * a) / s
        d_r = d_log_a * (-RG_C * sp)
        lamv = lam_ref[...]
        d_sp = jnp.sum(d_log_a * (-RG_C * r), axis=0, keepdims=True)
        dlam_ref[...] += d_sp * (-_sigmoid(-lamv))
        d_zr = d_r * r * (1.0 - r)
        d_zi = d_gi * gi * (1.0 - gi)
        dba_ref[...] += jnp.sum(d_zr, axis=0, keepdims=True)
        dbx_ref[...] += jnp.sum(d_zi, axis=0, keepdims=True)
        xb = xv.astype(BF16)
        zrb, zib = d_zr.astype(BF16), d_zi.astype(BF16)
        parts = []
        for n in range(RNN_BLOCKS):
            cols = slice(n * RNN_BW, (n + 1) * RNN_BW)
            dwa_ref[n] += lax.dot_general(xb[:, cols], zrb[:, cols], (((0,), (0,)), ((), ())), preferred_element_type=F32)
            dwx_ref[n] += lax.dot_general(xb[:, cols], zib[:, cols], (((0,), (0,)), ((), ())), preferred_element_type=F32)
            parts.append(
                lax.dot_general(zrb[:, cols], wa_ref[n].astype(BF16), (((1,), (1,)), ((), ())), preferred_element_type=F32)
                + lax.dot_general(zib[:, cols], wx_ref[n].astype(BF16), (((1,), (1,)), ((), ())), preferred_element_type=F32))
        dx_ref[...] = dx + jnp.concatenate(parts, axis=-1)

    blk = pl.BlockSpec((tb, D_RNN), order)
    wspec = _full((RNN_BLOCKS, RNN_BW, RNN_BW))
    vec = _full((1, D_RNN))
    wshape = jax.ShapeDtypeStruct((RNN_BLOCKS, RNN_BW, RNN_BW), F32)
    vshape = jax.ShapeDtypeStruct((1, D_RNN), F32)
    return pl.pallas_call(
        body, name=name, grid=(nb,), in_specs=[blk, blk, blk, wspec, vec, wspec, vec, vec],
        out_specs=(blk, wspec, vec, wspec, vec, vec),
        out_shape=(jax.ShapeDtypeStruct((rows, D_RNN), F32), wshape, vshape, wshape, vshape, vshape),
        scratch_shapes=[pltpu.VMEM((tb, D_RNN), F32), pltpu.VMEM((tb, D_RNN), F32), pltpu.VMEM((SCAN_ROWS, D_RNN), F32)],
        compiler_params=_params(("arbitrary",)),
    )(xs, h_prev, dh, wa, ba, wx, bx, lam)


def _gelu(z):
    return 0.5 * z * (1.0 + jnp.tanh(GELU_C * (z + 0.044715 * z * z * z)))


def _gelu_grad(z):
    t = jnp.tanh(GELU_C * (z + 0.044715 * z * z * z))
    return 0.5 * (1.0 + t) + 0.5 * z * (1.0 - t * t) * (GELU_C * (1.0 + 3.0 * 0.044715 * z * z))


GATE_COL0 = XR_COL0 + D_RNN


RNN_OUT_COLS = 512


def _rnn_out_specs(rows, hf_off, hb_off):
    tr = _tile(rows, 256, 16)
    assert hf_off % tr == 0 and hb_off % tr == 0 and GATE_COL0 % RNN_OUT_COLS == 0
    fo, bo, go = hf_off // tr, hb_off // tr, GATE_COL0 // RNN_OUT_COLS
    hf_spec = pl.BlockSpec((tr, RNN_OUT_COLS), lambda i, j: (i + fo, j))
    hb_spec = pl.BlockSpec((tr, RNN_OUT_COLS), lambda i, j: (i + bo, j))
    gate_spec = pl.BlockSpec((tr, RNN_OUT_COLS), lambda i, j: (i, j + go))
    out_spec = pl.BlockSpec((tr, RNN_OUT_COLS), lambda i, j: (i, j))
    return (rows // tr, D_RNN // RNN_OUT_COLS), hf_spec, hb_spec, gate_spec, out_spec


def _rnn_out_fwd(hf, hb, proj, hf_off, hb_off):
    rows = proj.shape[0]
    grid, hf_spec, hb_spec, gate_spec, out_spec = _rnn_out_specs(rows, hf_off, hb_off)

    def body(hf_ref, hb_ref, g_ref, o_ref):
        o_ref[...] = ((hf_ref[...] + hb_ref[...]) * _gelu(g_ref[...])).astype(o_ref.dtype)

    return pl.pallas_call(
        body, name="rnn_out_fwd", grid=grid, in_specs=[hf_spec, hb_spec, gate_spec], out_specs=out_spec,
        out_shape=jax.ShapeDtypeStruct((rows, D_RNN), BF16), compiler_params=_params(("parallel", "parallel")),
    )(hf, hb, proj)


def _rnn_out_bwd(d_cat, hf, hb, proj, hf_off, hb_off):
    rows = proj.shape[0]
    grid, hf_spec, hb_spec, gate_spec, out_spec = _rnn_out_specs(rows, hf_off, hb_off)
    do = ATTN_W // RNN_OUT_COLS

    def body(d_ref, hf_ref, hb_ref, g_ref, dh_ref, dg_ref):
        dv, gv = d_ref[...].astype(F32), g_ref[...]
        dh_ref[...] = dv * _gelu(gv)
        dg_ref[...] = (dv * (hf_ref[...] + hb_ref[...]) * _gelu_grad(gv)).astype(dg_ref.dtype)

    tr = out_spec.block_shape[0]
    return pl.pallas_call(
        body, name="rnn_out_bwd", grid=grid,
        in_specs=[pl.BlockSpec((tr, RNN_OUT_COLS), lambda i, j: (i, j + do)), hf_spec, hb_spec, gate_spec],
        out_specs=(out_spec, out_spec),
        out_shape=(jax.ShapeDtypeStruct((rows, D_RNN), F32), jax.ShapeDtypeStruct((rows, D_RNN), BF16)),
        compiler_params=_params(("parallel", "parallel")),
    )(d_cat, hf, hb, proj)


def _gmlp_parts(z_ref, vg_ref, vb_ref, d_gm):
    zu, zv = z_ref[:, :d_gm], z_ref[:, d_gm:]
    u = _gelu(zu)
    v = _gelu(zv)
    mu = jnp.mean(v, axis=-1, keepdims=True)
    vc = v - mu
    rstd = lax.rsqrt(jnp.mean(vc * vc, axis=-1, keepdims=True) + EPS)
    vhat = vc * rstd
    vn = vhat * vg_ref[...] + vb_ref[...]
    return zu, zv, u, vhat, rstd, vn


def _gmlp_fwd(z, v_g, v_b, w_sp, b_sp_t):
    rows, d_gm = z.shape[0], z.shape[1] // 2
    tr = _tile(rows, 256, CHUNK)
    gwid = d_gm // GM_GROUPS

    def body(z_ref, vg_ref, vb_ref, w_ref, b_ref, o_ref):
        _, _, u, _, _, vn = _gmlp_parts(z_ref, vg_ref, vb_ref, d_gm)
        vnb = vn.astype(BF16)
        for g in range(GM_GROUPS):
            wg = w_ref[g].astype(BF16)
            for c in range(tr // CHUNK):
                rs, cs = slice(c * CHUNK, (c + 1) * CHUNK), slice(g * gwid, (g + 1) * gwid)
                sv = jnp.dot(wg, vnb[rs, cs], preferred_element_type=F32) + b_ref[:, g:g + 1]
                o_ref[rs, cs] = (u[rs, cs] * sv).astype(o_ref.dtype)

    return pl.pallas_call(
        body, name="gmlp_fwd", grid=(rows // tr,),
        in_specs=[pl.BlockSpec((tr, 2 * d_gm), lambda i: (i, 0)), _full((1, d_gm)), _full((1, d_gm)),
                  _full(w_sp.shape), _full(b_sp_t.shape)],
        out_specs=pl.BlockSpec((tr, d_gm), lambda i: (i, 0)),
        out_shape=jax.ShapeDtypeStruct((rows, d_gm), BF16), compiler_params=_params(("parallel",)),
    )(z, v_g, v_b, w_sp, b_sp_t)


def _gmlp_bwd(z, dgate, v_g, v_b, w_sp, b_sp_t):
    rows, d_gm = z.shape[0], z.shape[1] // 2
    tr = _tile(rows, 256, CHUNK)
    gwid = d_gm // GM_GROUPS

    def body(z_ref, dg_ref, vg_ref, vb_ref, w_ref, b_ref, dz_ref, dbin_ref, dvg_ref, dvb_ref, dw_ref, dbs_ref, dvn_s):
        @pl.when(pl.program_id(0) == 0)
        def _():
            dbin_ref[...] = jnp.zeros_like(dbin_ref)
            dvg_ref[...] = jnp.zeros_like(dvg_ref)
            dvb_ref[...] = jnp.zeros_like(dvb_ref)
            dw_ref[...] = jnp.zeros_like(dw_ref)
            dbs_ref[...] = jnp.zeros_like(dbs_ref)

        zu, zv, u, vhat, rstd, vn = _gmlp_parts(z_ref, vg_ref, vb_ref, d_gm)
        vnb = vn.astype(BF16)
        dgv = dg_ref[...].astype(F32)
        dsv = dgv * u
        dsvb = dsv.astype(BF16)
        for g in range(GM_GROUPS):
            wg = w_ref[g].astype(BF16)
            cs = slice(g * gwid, (g + 1) * gwid)
            for c in range(tr // CHUNK):
                rs = slice(c * CHUNK, (c + 1) * CHUNK)
                sv = jnp.dot(wg, vnb[rs, cs], preferred_element_type=F32) + b_ref[:, g:g + 1]
                dz_ref[rs, cs] = (dgv[rs, cs] * sv * _gelu_grad(zu[rs, cs])).astype(dz_ref.dtype)
                dw_ref[g] += lax.dot_general(dsvb[rs, cs], vnb[rs, cs], (((1,), (1,)), ((), ())),
                                             preferred_element_type=F32)
                dbs_ref[:, g:g + 1] += jnp.sum(dsv[rs, cs], axis=-1, keepdims=True)
                dvn_s[rs, cs] = lax.dot_general(wg, dsvb[rs, cs], (((0,), (0,)), ((), ())), preferred_element_type=F32)
        dvn = dvn_s[...]
        dvg_ref[...] += jnp.sum(dvn * vhat, axis=0, keepdims=True)
        dvb_ref[...] += jnp.sum(dvn, axis=0, keepdims=True)
        dvh = dvn * vg_ref[...]
        dv = rstd * (dvh - jnp.mean(dvh, axis=-1, keepdims=True) - vhat * jnp.mean(dvh * vhat, axis=-1, keepdims=True))
        dzv = dv * _gelu_grad(zv)
        dz_ref[:, d_gm:] = dzv.astype(dz_ref.dtype)
        dbin_ref[:, d_gm:] += jnp.sum(dzv, axis=0, keepdims=True)
        dbin_ref[:, :d_gm] += jnp.sum(dz_ref[:, :d_gm].astype(F32), axis=0, keepdims=True)

    return pl.pallas_call(
        body, name="gmlp_bwd", grid=(rows // tr,),
        in_specs=[pl.BlockSpec((tr, 2 * d_gm), lambda i: (i, 0)), pl.BlockSpec((tr, d_gm), lambda i: (i, 0)),
                  _full((1, d_gm)), _full((1, d_gm)), _full(w_sp.shape), _full(b_sp_t.shape)],
        out_specs=(pl.BlockSpec((tr, 2 * d_gm), lambda i: (i, 0)), _full((1, 2 * d_gm)), _full((1, d_gm)),
                   _full((1, d_gm)), _full(w_sp.shape), _full(b_sp_t.shape)),
        out_shape=(jax.ShapeDtypeStruct((rows, 2 * d_gm), BF16), jax.ShapeDtypeStruct((1, 2 * d_gm), F32),
                   jax.ShapeDtypeStruct((1, d_gm), F32), jax.ShapeDtypeStruct((1, d_gm), F32),
                   jax.ShapeDtypeStruct(w_sp.shape, F32), jax.ShapeDtypeStruct(b_sp_t.shape, F32)),
        scratch_shapes=[pltpu.VMEM((tr, d_gm), F32)],
        compiler_params=_params(("arbitrary",)),
    )(z, dgate, v_g, v_b, w_sp, b_sp_t)


def _adamw_math(w, g, m, v):
    m = ADAM_B1 * m + (1.0 - ADAM_B1) * g
    v = ADAM_B2 * v + (1.0 - ADAM_B2) * (g * g)
    m_hat = m / (1.0 - ADAM_B1 ** ADAM_STEP)
    v_hat = v / (1.0 - ADAM_B2 ** ADAM_STEP)
    delta = -ADAM_LR * (m_hat / (jnp.sqrt(v_hat) + ADAM_EPS) + ADAM_WD * w)
    return delta, m, v


def _adamw(w, g, m, v, name):
    shape = w.shape
    outs = _rowwise(_adamw_math, (F32, F32, F32), _as2d(w), _as2d(g), _as2d(m), _as2d(v), name=name)
    return (g.reshape(shape),) + tuple(o.reshape(shape) for o in outs)


PACK_COLS = 1024


def _pack(arrays):
    flat = jnp.concatenate([a.reshape(-1).astype(F32) for a in arrays])
    pad = (-flat.size) % (8 * PACK_COLS)
    return jnp.pad(flat, (0, pad)).reshape(-1, PACK_COLS)


def _unpack(flat, shapes):
    out, pos = [], 0
    for shp in shapes:
        n = math.prod(shp)
        out.append(flat[pos:pos + n].reshape(shp))
        pos += n
    return out


def _unpack_devices(packed8, shapes):
    flat8 = packed8.reshape(N_DEV, -1)
    out, pos = [], 0
    for shp in shapes:
        n = math.prod(shp)
        out.append(flat8[:, pos:pos + n].reshape((N_DEV,) + tuple(shp)))
        pos += n
    return out


def _sum_devices(g8):
    _, rows, cols = g8.shape
    tr = _rows_tile(rows, cols, budget=256 * 1024)

    def body(g_ref, o_ref):
        acc = g_ref[0]
        for d in range(1, N_DEV):
            acc = acc + g_ref[d]
        o_ref[...] = acc

    return pl.pallas_call(
        body, name="sum_devices", grid=(rows // tr,), in_specs=[pl.BlockSpec((N_DEV, tr, cols), lambda i: (0, i, 0))],
        out_specs=pl.BlockSpec((tr, cols), lambda i: (i, 0)), out_shape=jax.ShapeDtypeStruct((rows, cols), F32),
        compiler_params=_params(("parallel",)),
    )(g8)


def _place():
    return lax.axis_index("x"), lax.axis_index("y"), lax.axis_index("c")


def _other_chips(x, y):
    return [(1 - x, y), (x, 1 - y), (1 - x, 1 - y)]


def _remote(src, dst, send_sem, recv_sem, to):
    return pltpu.make_async_remote_copy(src_ref=src, dst_ref=dst, send_sem=send_sem, recv_sem=recv_sem, device_id=to,
                                        device_id_type=MESH)


def _comm_call(body, name, operands, out_shapes, n_remote, n_local, aliases=None):
    return pl.pallas_call(
        body, name=name, out_shape=tuple(out_shapes), in_specs=[ANY] * len(operands), out_specs=tuple(ANY for _ in out_shapes),
        scratch_shapes=[pltpu.SemaphoreType.DMA((n_remote,)), pltpu.SemaphoreType.DMA((n_remote,)),
                        pltpu.SemaphoreType.DMA((max(n_local, 1),))],
        input_output_aliases=aliases or {},
    )(*operands)


def _in_place(arrays):
    return [jax.ShapeDtypeStruct(a.shape, a.dtype) for a in arrays], {i: i for i in range(len(arrays))}


def _allgather8(arrs, name):
    n = len(arrs)

    def body(*refs):
        ins, outs = refs[:n], refs[n:2 * n]
        send, recv, lsem = refs[2 * n:]
        x, y, c = _place()
        me, sib = (x, y, c), (x, y, 1 - c)
        chips = _other_chips(x, y)

        def slot(t, px, py, pc):
            return outs[t].at[4 * px + 2 * py + pc]

        def cp(t, k, block, to, from_input=False):
            src = ins[t] if from_input else slot(t, *block)
            return _remote(src, slot(t, *block), send.at[7 * t + k], recv.at[7 * t + k], to)

        mine = [pltpu.make_async_copy(ins[t], slot(t, *me), lsem.at[t]) for t in range(n)]
        for cpy in mine:
            cpy.start()
        first = []
        for t in range(n):
            first.append(cp(t, 0, me, sib, True))
            first += [cp(t, 1 + j, me, (*chip, c), True) for j, chip in enumerate(chips)]
        for cpy in first:
            cpy.start()
        passed = []
        for t in range(n):
            for j, chip in enumerate(chips):
                cp(t, 1 + j, (*chip, c), me).wait_recv()
                fwd = cp(t, 4 + j, (*chip, c), sib)
                fwd.start()
                passed.append(fwd)
        for t in range(n):
            cp(t, 0, sib, me).wait_recv()
            for j, chip in enumerate(chips):
                cp(t, 4 + j, (*chip, 1 - c), me).wait_recv()
        for cpy in first + passed:
            cpy.wait_send()
        for cpy in mine:
            cpy.wait()

    outs = _comm_call(body, name, arrs, [jax.ShapeDtypeStruct((N_DEV,) + a.shape, a.dtype) for a in arrs], 7 * n, n)
    return list(outs)


def _gather_weights(bufs):
    n_u = len(bufs)

    def body(*refs):
        bufs_ = refs[n_u:2 * n_u]
        send, recv, _ = refs[2 * n_u:]
        x, y, c = _place()
        me, sib, q = (x, y, c), (x, y, 1 - c), 2 * x + y
        chips = _other_chips(x, y)
        sent = []
        for u in range(n_u):
            half = bufs_[u].shape[1] // 2
            mine = bufs_[u].at[q, pl.ds(c * half, half)]
            for j, chip in enumerate(chips):
                cpy = _remote(mine, mine, send.at[6 * u + j], recv.at[6 * u + j], (*chip, c))
                cpy.start()
                sent.append(cpy)
        for u in range(n_u):
            half = bufs_[u].shape[1] // 2
            for j, chip in enumerate(chips):
                landed = bufs_[u].at[2 * chip[0] + chip[1], pl.ds(c * half, half)]
                _remote(landed, landed, send.at[6 * u + j], recv.at[6 * u + j], me).wait_recv()
                cpy = _remote(landed, landed, send.at[6 * u + 3 + j], recv.at[6 * u + 3 + j], sib)
                cpy.start()
                sent.append(cpy)
        for u in range(n_u):
            half = bufs_[u].shape[1] // 2
            for j, chip in enumerate(chips):
                landed = bufs_[u].at[2 * chip[0] + chip[1], pl.ds((1 - c) * half, half)]
                _remote(landed, landed, send.at[6 * u + 3 + j], recv.at[6 * u + 3 + j], me).wait_recv()
        for cpy in sent:
            cpy.wait_send()

    shapes, aliases = _in_place(bufs)
    return list(_comm_call(body, "gather_weights", bufs, shapes, 6 * n_u, 0, aliases))


def _exchange_halves(grads):
    n = len(grads)

    def body(*refs):
        ins, outs = refs[:n], refs[n:2 * n]
        send, recv, _ = refs[2 * n:]
        x, y, c = _place()
        sib = (x, y, 1 - c)
        sent = []
        for k in range(n):
            half = ins[k].shape[1] // 2
            cpy = _remote(ins[k].at[pl.ds(0, N_CHIPS), pl.ds((1 - c) * half, half)], outs[k], send.at[k], recv.at[k], sib)
            cpy.start()
            sent.append(cpy)
        for cpy in sent:
            cpy.wait()

    shapes = [jax.ShapeDtypeStruct((N_CHIPS, g.shape[1] // 2, g.shape[2]), g.dtype) for g in grads]
    return list(_comm_call(body, "exchange_halves", grads, shapes, n, 0))


def _chips_all_to_all(sums):
    n = len(sums)

    def body(*refs):
        ins, outs = refs[:n], refs[n:2 * n]
        send, recv, _ = refs[2 * n:]
        x, y, c = _place()
        sent = []
        for k in range(n):
            for j, chip in enumerate(_other_chips(x, y)):
                cpy = _remote(ins[k].at[2 * chip[0] + chip[1]], outs[k].at[j], send.at[3 * k + j], recv.at[3 * k + j], (*chip, c))
                cpy.start()
                sent.append(cpy)
        for cpy in sent:
            cpy.wait()

    shapes = [jax.ShapeDtypeStruct((N_CHIPS - 1,) + s.shape[1:], s.dtype) for s in sums]
    return list(_comm_call(body, "chips_all_to_all", sums, shapes, 3 * n, 0))


def _join_halves(bufs):
    n = len(bufs)
    units = [(k, layer) for k in range(n) for layer in range(bufs[k].shape[0])]

    def body(*refs):
        bufs_ = refs[n:2 * n]
        send, recv, _ = refs[2 * n:]
        x, y, c = _place()
        sent = []
        for u, (k, layer) in enumerate(units):
            half = bufs_[k].shape[1] // 2
            mine = bufs_[k].at[layer, pl.ds(c * half, half)]
            cpy = _remote(mine, mine, send.at[u], recv.at[u], (x, y, 1 - c))
            cpy.start()
            sent.append(cpy)
        for u, (k, layer) in enumerate(units):
            half = bufs_[k].shape[1] // 2
            theirs = bufs_[k].at[layer, pl.ds((1 - c) * half, half)]
            _remote(theirs, theirs, send.at[u], recv.at[u], (x, y, c)).wait_recv()
        for cpy in sent:
            cpy.wait_send()

    shapes, aliases = _in_place(bufs)
    return list(_comm_call(body, "join_halves", bufs, shapes, len(units), 0, aliases))


def _add_halves(grad, other, place):
    _, rows, cols = grad.shape
    half = rows // 2
    tr = _rows_tile(half, cols, itemsize=2, budget=1024 * 1024)
    per_half = half // tr

    def body(place_ref, g_ref, o_ref, s_ref):
        s_ref[...] = (g_ref[...].astype(F32) + o_ref[...].astype(F32)).astype(s_ref.dtype)

    return pl.pallas_call(
        body, name="add_halves", out_shape=jax.ShapeDtypeStruct((N_CHIPS, half, cols), grad.dtype),
        grid_spec=pltpu.PrefetchScalarGridSpec(
            num_scalar_prefetch=1, grid=(N_CHIPS, per_half),
            in_specs=[pl.BlockSpec((None, tr, cols), lambda k, i, pr: (k, pr[1] * per_half + i, 0)),
                      pl.BlockSpec((None, tr, cols), lambda k, i, pr: (k, i, 0))],
            out_specs=pl.BlockSpec((None, tr, cols), lambda k, i, pr: (k, i, 0))),
        compiler_params=_params(("parallel", "parallel")),
    )(place, grad, other)


def _add_chips(sums, others, place, dest, layer, n_layers):
    _, half, cols = sums.shape
    tr = _rows_tile(half, cols, itemsize=4, budget=1024 * 1024)
    per_half = half // tr

    def body(place_ref, s_ref, o_ref, *rest):
        acc = s_ref[...].astype(F32)
        for j in range(N_CHIPS - 1):
            acc = acc + o_ref[j].astype(F32)
        rest[-1][...] = acc

    operands = [place, sums, others] + ([] if dest is None else [dest])
    return pl.pallas_call(
        body, name="add_chips", out_shape=jax.ShapeDtypeStruct((n_layers, 2 * half, cols), F32),
        grid_spec=pltpu.PrefetchScalarGridSpec(
            num_scalar_prefetch=1, grid=(per_half,),
            in_specs=[pl.BlockSpec((None, tr, cols), lambda i, pr: (pr[0], i, 0)),
                      pl.BlockSpec((N_CHIPS - 1, tr, cols), lambda i, pr: (0, i, 0))] + ([] if dest is None else [ANY]),
            out_specs=pl.BlockSpec((None, tr, cols), lambda i, pr: (layer, pr[1] * per_half + i, 0))),
        input_output_aliases={} if dest is None else {3: 0},
        compiler_params=_params(("parallel",)),
    )(*operands)


HBM = pl.BlockSpec(memory_space=pltpu.HBM)
SEM = pl.BlockSpec(memory_space=pltpu.SEMAPHORE)
DATAFLOW = pltpu.SideEffectType.DATAFLOW_SIDE_EFFECTING


def _split_start(name, bufs, copies, n_copies, after=None):
    n = len(bufs)
    extra = 0 if after is None else 1

    def body(*refs):
        for cpy in copies(refs[:n], refs[n + extra], refs[n + extra + 1]):
            cpy.start()
        refs[-1][...] = jnp.zeros_like(refs[-1])

    outs = pl.pallas_call(
        body, name=name,
        out_shape=(pltpu.SemaphoreType.DMA((n_copies,)), pltpu.SemaphoreType.DMA((n_copies,)),
                   *[pltpu.HBM(b.shape, b.dtype) for b in bufs], jax.ShapeDtypeStruct((8, LANES), F32)),
        in_specs=[HBM] * n + [ANY] * extra,
        out_specs=(SEM, SEM, *[HBM] * n, pl.BlockSpec(memory_space=pltpu.VMEM)),
        input_output_aliases={i: 2 + i for i in range(n)},
        compiler_params=pltpu.CompilerParams(has_side_effects=DATAFLOW),
    )(*[pltpu.with_memory_space_constraint(b, pltpu.HBM) for b in bufs], *([] if after is None else [after]))
    return outs[0], outs[1], list(outs[2:2 + n]), outs[-1]


def _split_wait(name, bufs, send, recv, copies, after):
    n = len(bufs)

    def body(*refs):
        for cpy in copies(refs[:n], refs[n], refs[n + 1]):
            cpy.wait_send()
            cpy.wait_recv()

    return list(pl.pallas_call(
        body, name=name, out_shape=tuple(pltpu.HBM(b.shape, b.dtype) for b in bufs),
        in_specs=[HBM] * n + [SEM, SEM, ANY], out_specs=tuple([HBM] * n),
        input_output_aliases={i: i for i in range(n)},
        compiler_params=pltpu.CompilerParams(has_side_effects=DATAFLOW),
    )(*bufs, send, recv, after))


def _gather_copies(bufs, send, recv):
    x, y, c = _place()
    out = []
    for u, buf in enumerate(bufs):
        half = buf.shape[1] // 2
        mine = buf.at[2 * x + y, pl.ds(c * half, half)]
        out += [_remote(mine, mine, send.at[3 * u + j], recv.at[3 * u + j], (*chip, c))
                for j, chip in enumerate(_other_chips(x, y))]
    return out


def _all_to_all_copies(bufs, send, recv):
    x, y, c = _place()
    n = len(bufs) // 2
    return [_remote(bufs[k].at[2 * chip[0] + chip[1]], bufs[n + k].at[j], send.at[3 * k + j], recv.at[3 * k + j], (*chip, c))
            for k in range(n) for j, chip in enumerate(_other_chips(x, y))]


def _forward_halves(bufs, name):
    n_u = len(bufs)

    def body(*refs):
        bufs_ = refs[n_u:2 * n_u]
        send, recv, _ = refs[2 * n_u:]
        x, y, c = _place()
        chips = _other_chips(x, y)
        sent = []
        for u in range(n_u):
            half = bufs_[u].shape[1] // 2
            for j, chip in enumerate(chips):
                landed = bufs_[u].at[2 * chip[0] + chip[1], pl.ds(c * half, half)]
                cpy = _remote(landed, landed, send.at[3 * u + j], recv.at[3 * u + j], (x, y, 1 - c))
                cpy.start()
                sent.append(cpy)
        for u in range(n_u):
            half = bufs_[u].shape[1] // 2
            for j, chip in enumerate(chips):
                theirs = bufs_[u].at[2 * chip[0] + chip[1], pl.ds((1 - c) * half, half)]
                _remote(theirs, theirs, send.at[3 * u + j], recv.at[3 * u + j], (x, y, c)).wait_recv()
        for cpy in sent:
            cpy.wait_send()

    shapes, aliases = _in_place(bufs)
    return list(_comm_call(body, name, bufs, shapes, 3 * n_u, 0, aliases))


FWD_GROUPS = {'ffn0': ('ff_in0', 'ff_out0'), 'l1': ('gm_in', 'gm_out', 'ff_in1', 'ff_out1')}
GRAD_LAYOUT = {'ff_in0': (0, 0), 'ff_in1': (0, 1), 'ff_out0': (1, 0), 'ff_out1': (1, 1), 'ar_in': (2, 0), 'ar_out': (3, 0),
               'gm_in': (4, 0), 'gm_out': (5, 0)}


class _MeshLink:
    def __init__(self, place, shards):
        self.place = place
        ar = _gather_weights([shards['ar_in'], shards['ar_out']])
        self.ready = {'ar_in': ar[0], 'ar_out': ar[1]}
        self.pending, after = {}, ar[1]
        for group, names in FWD_GROUPS.items():
            send, recv, bufs, token = _split_start(f"gather_{group}_start", [shards[n] for n in names], _gather_copies,
                                                   3 * len(names), after)
            self.pending[group] = (names, send, recv, bufs)
            after = token
        self.start_token = after[0, 0]
        self.sent, self.reduced = {}, {}

    def weights(self, group, after):
        if group in self.pending:
            names, send, recv, bufs = self.pending.pop(group)
            bufs = _split_wait(f"gather_{group}_wait", bufs, send, recv, _gather_copies, after)
            self.ready.update(zip(names, _forward_halves(bufs, f"gather_{group}_forward")))
        return self.ready

    def gradients(self, group, grads):
        names = list(grads)
        received = _exchange_halves([grads[n] for n in names])
        sums = [_add_halves(grads[n], r, self.place) for n, r in zip(names, received)]
        if group == 'ar':
            self.reduced.update(zip(names, zip(sums, _chips_all_to_all(sums))))
            return 0.0
        landing = [lax.empty((N_CHIPS - 1,) + s.shape[1:], s.dtype) for s in sums]
        send, recv, bufs, token = _split_start(f"grads_{group}_start", sums + landing, _all_to_all_copies, 3 * len(names))
        self.sent[group] = (names, send, recv, bufs)
        return token[0, 0]

    def finish(self, after):
        for group, (names, send, recv, bufs) in self.sent.items():
            bufs = _split_wait(f"grads_{group}_wait", bufs, send, recv, _all_to_all_copies, after)
            self.reduced.update(zip(names, zip(bufs[:len(names)], bufs[len(names):])))
        n_layers = {p: 1 + max(l for pp, l in GRAD_LAYOUT.values() if pp == p) for p, _ in GRAD_LAYOUT.values()}
        out = {}
        for name, (p, layer) in GRAD_LAYOUT.items():
            sums, others = self.reduced[name]
            out[p] = _add_chips(sums, others, self.place, out.get(p), layer, n_layers[p])
        return _join_halves([out[p] for p in sorted(out)])


def _rope_tables(n):
    t = jnp.arange(n)
    freqs = ROPE_THETA ** (-jnp.arange(ROPE_PAIRS, dtype=F32) / ROPE_PAIRS)
    ang_r = (t // GRID_W).astype(F32)[:, None] * freqs
    ang_c = (t % GRID_W).astype(F32)[:, None] * freqs
    cos = jnp.concatenate([jnp.cos(ang_r), jnp.cos(ang_r), jnp.cos(ang_c), jnp.cos(ang_c)], axis=-1)
    sin = jnp.concatenate([-jnp.sin(ang_r), jnp.sin(ang_r), -jnp.sin(ang_c), jnp.sin(ang_c)], axis=-1)
    return cos, sin


def _ffn_fwd(h2, w1, w2, tag):
    r, a = _matmul(h2, w1, kind='nn', b_split='n', out_dtype=BF16, epilogue='relu2', name=f"ffn_in_{tag}")
    f = _matmul(a, w2, kind='nn', b_split='k', out_dtype=F32, name=f"ffn_out_{tag}")
    return r, a, f


def _ffn_bwd(d_f, h2, r, a, w1, w2, tag):
    d_u = _matmul(d_f, w2, kind='nt', b_split='k', out_dtype=BF16, epilogue='times2x', extra=r, name=f"ffn_out_dx_{tag}")
    d_w2 = _matmul(a, d_f, kind='tn', out_split='k', out_dtype=BF16, name=f"ffn_out_dw_{tag}")
    d_w1 = _matmul(h2, d_u, kind='tn', out_split='n', out_dtype=BF16, name=f"ffn_in_dw_{tag}")
    d_h2 = _matmul(d_u, w1, kind='nt', b_split='n', out_dtype=F32, name=f"ffn_in_dx_{tag}")
    return d_h2, d_w1, d_w2


class _LocalLink:
    def __init__(self, big):
        self.big, self.grads, self.start_token = big, {}, 0.0

    def weights(self, group, after):
        return self.big

    def gradients(self, group, grads):
        self.grads.update(grads)
        return 0.0


def _local_step(xl0, xc0, target, ml, mc0, sp, link):
    n_lat, n_ctx = xl0.shape[0], xc0.shape[0]
    one = lambda v: 1.0 + v
    g = [[sp['norm_g'][i, k][None, :] for k in range(4)] for i in range(2)]

    sh1, sc1, gt1, sh2, sc2, gt2 = ml[0]
    big = link.weights('ar', None)
    sh1 = sh1 + link.start_token
    hl = _norm_fwd(xl0, g[0][0], one(sc1), b=sh1, out_dtype=BF16, name="l0_mod1")
    hc = _norm_fwd(xc0, g[0][0], one(mc0[1]), b=mc0[0], out_dtype=BF16, name="l0_mod1_ctx")
    proj_l = _matmul(hl, big['ar_in'], kind='nn', b_split='n', out_dtype=F32, name="ar_in_lat")
    proj_c = _matmul(hc, big['ar_in'], kind='nn', b_split='n', out_dtype=F32, name="ar_in_ctx")
    cos_l, sin_l = _rope_tables(n_lat)
    cos_c, sin_c = jnp.ones((n_ctx, HEAD_DIM), F32), jnp.zeros((n_ctx, HEAD_DIM), F32)
    q_g, k_g = sp['q_g'], sp['k_g']
    q_l, k_l, v_l = _qk_fwd(proj_l, q_g, k_g, cos_l, sin_l, name="qk_fwd_lat")
    _, k_c, v_c = _qk_fwd(proj_c, q_g, k_g, cos_c, sin_c, name="qk_fwd_ctx")
    k_all = jnp.concatenate([k_c, k_l], axis=0)
    v_all = jnp.concatenate([v_c, v_l], axis=0)
    attn, lse = _attn_fwd(q_l, k_all, v_all)
    conv_l = _conv_fwd(proj_l, sp['conv_w'], sp['conv_b'], name="conv_fwd_lat")
    conv_c = _conv_fwd(proj_c, sp['conv_w'], sp['conv_b'], name="conv_fwd_ctx")
    xs_f = jnp.concatenate([conv_c, conv_l], axis=0)
    xs_r = jnp.concatenate([conv_l, conv_c], axis=0)
    rnn_w = [(sp['wa'][d], sp['ba'][d][None, :], sp['wx'][d], sp['bx'][d][None, :], sp['lam'][d][None, :]) for d in range(2)]
    h_f, hp_f = _rglru_fwd(xs_f, *rnn_w[0], reverse=False, name="rglru_fwd_f")
    h_r, hp_r = _rglru_fwd(xs_r, *rnn_w[1], reverse=True, name="rglru_fwd_r")
    rnn = _rnn_out_fwd(h_f, h_r, proj_l, n_ctx, 0)
    cat = jnp.concatenate([attn, rnn], axis=1)
    ol0 = _matmul(cat, big['ar_out'], kind='nn', b_split='k', out_dtype=F32, name="ar_out")
    xm0 = _norm_fwd(ol0, g[0][1], gt1, res=xl0, out_dtype=F32, name="l0_res1")
    h2_0 = _norm_fwd(xm0, g[0][2], one(sc2), b=sh2, out_dtype=BF16, name="l0_mod2")
    w_f0 = link.weights('ffn0', h2_0)
    r0, a0, f0 = _ffn_fwd(h2_0, w_f0['ff_in0'], w_f0['ff_out0'], "l0")
    xl1 = _norm_fwd(f0, g[0][3], gt2, res=xm0, out_dtype=F32, name="l0_res2")

    th1, tc1, tg1, th2, tc2, tg2 = ml[1]
    w_l1 = link.weights('l1', xl1)
    hl1 = _norm_fwd(xl1, g[1][0], one(tc1), b=th1, out_dtype=BF16, name="l1_mod1")
    z = _matmul(hl1, w_l1['gm_in'], kind='nn', b_split='n', bias=sp['gm_b_in'], out_dtype=F32, name="gm_in")
    b_sp_t = sp['gm_b_sp'].T
    gated = _gmlp_fwd(z, sp['gm_v_g'], sp['gm_v_b'], sp['gm_w_sp'], b_sp_t)
    ol1 = _matmul(gated, w_l1['gm_out'], kind='nn', b_split='k', out_dtype=F32, name="gm_out")
    xm1 = _norm_fwd(ol1, g[1][1], tg1, res=xl1, out_dtype=F32, name="l1_res1")
    h2_1 = _norm_fwd(xm1, g[1][2], one(tc2), b=th2, out_dtype=BF16, name="l1_mod2")
    r1, a1, f1 = _ffn_fwd(h2_1, w_l1['ff_in1'], w_l1['ff_out1'], "l1")
    y = _norm_fwd(f1, g[1][3], tg2, res=xm1, out_dtype=F32, name="l1_res2")

    dy, loss = _loss_head(y, target)

    d_f1, dg13, d_tg2, _ = _norm_bwd(dy, f1, g[1][3], tg2, out_dtype=BF16, name="l1_res2_bwd")
    d_h2, dw_ff_in1, dw_ff_out1 = _ffn_bwd(d_f1, h2_1, r1, a1, w_l1['ff_in1'], w_l1['ff_out1'], "l1")
    tok = link.gradients('ffn1', {'ff_in1': dw_ff_in1, 'ff_out1': dw_ff_out1})
    dxm1, dg12, d_tc2, d_th2 = _norm_bwd(d_h2, xm1, g[1][2], one(tc2) + tok, extra=dy, out_dtype=F32, name="l1_mod2_bwd")
    d_ol1, dg11, d_tg1, _ = _norm_bwd(dxm1, ol1, g[1][1], tg1, out_dtype=BF16, name="l1_res1_bwd")
    d_gated = _matmul(d_ol1, w_l1['gm_out'], kind='nt', b_split='k', out_dtype=F32, name="gm_out_dx")
    dw_gm_out = _matmul(gated, d_ol1, kind='tn', out_split='k', out_dtype=BF16, name="gm_out_dw")
    d_z, d_gm_b_in, d_vg, d_vb, d_wsp, d_bsp_t = _gmlp_bwd(z, d_gated, sp['gm_v_g'], sp['gm_v_b'], sp['gm_w_sp'], b_sp_t)
    dw_gm_in = _matmul(hl1, d_z, kind='tn', out_split='n', out_dtype=BF16, name="gm_in_dw")
    d_hl1 = _matmul(d_z, w_l1['gm_in'], kind='nt', b_split='n', out_dtype=F32, name="gm_in_dx")
    tok = link.gradients('gm', {'gm_in': dw_gm_in, 'gm_out': dw_gm_out})
    dxl1, dg10, d_tc1, d_th1 = _norm_bwd(d_hl1, xl1, g[1][0], one(tc1) + tok, extra=dxm1, out_dtype=F32, name="l1_mod1_bwd")

    d_f0, dg03, d_gt2, _ = _norm_bwd(dxl1, f0, g[0][3], gt2, out_dtype=BF16, name="l0_res2_bwd")
    d_h2, dw_ff_in0, dw_ff_out0 = _ffn_bwd(d_f0, h2_0, r0, a0, w_f0['ff_in0'], w_f0['ff_out0'], "l0")
    tok = link.gradients('ffn0', {'ff_in0': dw_ff_in0, 'ff_out0': dw_ff_out0})
    dxm0, dg02, d_sc2, d_sh2 = _norm_bwd(d_h2, xm0, g[0][2], one(sc2) + tok, extra=dxl1, out_dtype=F32, name="l0_mod2_bwd")
    d_ol0, dg01, d_gt1, _ = _norm_bwd(dxm0, ol0, g[0][1], gt1, out_dtype=BF16, name="l0_res1_bwd")
    d_cat = _matmul(d_ol0, big['ar_out'], kind='nt', b_split='k', out_dtype=F32, name="ar_out_dx")
    dw_ar_out = _matmul(cat, d_ol0, kind='tn', out_split='k', out_dtype=BF16, name="ar_out_dw")
    dq, dk_all, dv_all = _attn_bwd(q_l, k_all, v_all, attn, lse, d_cat)
    d_h, d_gate = _rnn_out_bwd(d_cat, h_f, h_r, proj_l, n_ctx, 0)
    zeros_c = jnp.zeros((n_ctx, D_RNN), F32)
    dxs_f, d_wa0, d_ba0, d_wx0, d_bx0, d_lam0 = _rglru_bwd(
        xs_f, hp_f, jnp.concatenate([zeros_c, d_h], axis=0), *rnn_w[0], reverse=False, name="rglru_bwd_f")
    dxs_r, d_wa1, d_ba1, d_wx1, d_bx1, d_lam1 = _rglru_bwd(
        xs_r, hp_r, jnp.concatenate([d_h, zeros_c], axis=0), *rnn_w[1], reverse=True, name="rglru_bwd_r")
    d_xr_l, d_cw_l, d_cb_l = _conv_bwd(dxs_f[n_ctx:], dxs_r[:n_lat], proj_l, sp['conv_w'], name="conv_bwd_lat")
    d_xr_c, d_cw_c, d_cb_c = _conv_bwd(dxs_f[:n_ctx], dxs_r[n_lat:], proj_c, sp['conv_w'], name="conv_bwd_ctx")
    dp_qk_l, d_qg, d_kg_l = _qk_bwd(dq, dk_all[n_ctx:], proj_l, q_g, k_g, cos_l, sin_l, name="qk_bwd_lat")
    dp_qk_c, _, d_kg_c = _qk_bwd(None, dk_all[:n_ctx], proj_c, q_g, k_g, cos_c, sin_c, name="qk_bwd_ctx")
    dv_b = dv_all.astype(BF16)
    d_proj_l = jnp.concatenate([dp_qk_l, dv_b[n_ctx:], d_xr_l, d_gate], axis=1)
    d_proj_c = jnp.concatenate([dp_qk_c, dv_b[:n_ctx], d_xr_c, jnp.zeros((n_ctx, D_RNN), BF16)], axis=1)
    dw_ar_in = _matmul(jnp.concatenate([hc, hl], axis=0), jnp.concatenate([d_proj_c, d_proj_l], axis=0), kind='tn',
                       out_split='n', out_dtype=BF16, name="ar_in_dw")
    d_hl = _matmul(d_proj_l, big['ar_in'], kind='nt', b_split='n', out_dtype=F32, name="ar_in_dx_lat")
    d_hc = _matmul(d_proj_c, big['ar_in'], kind='nt', b_split='n', out_dtype=F32, name="ar_in_dx_ctx")
    grad_x, dg00, d_sc1, d_sh1 = _norm_bwd(d_hl, xl0, g[0][0], one(sc1), extra=dxm0, out_dtype=F32, name="l0_mod1_bwd")
    _, dg00c, d_mc_scale, d_mc_shift = _norm_bwd(d_hc, xc0, g[0][0], one(mc0[1]), out_dtype=BF16, name="l0_mod1_ctx_bwd")

    zeros_d = jnp.zeros_like(d_sh1)
    small = {
        'd_ml0': jnp.concatenate([d_sh1, d_sc1, d_gt1, d_sh2, d_sc2, d_gt2], axis=1),
        'd_ml1': jnp.concatenate([d_th1, d_tc1, d_tg1, d_th2, d_tc2, d_tg2], axis=1),
        'd_mc0': jnp.concatenate([d_mc_shift, d_mc_scale] + [zeros_d] * 4, axis=1),
        'norm_g': jnp.stack([jnp.concatenate([dg00 + dg00c, dg01, dg02, dg03], axis=0),
                             jnp.concatenate([dg10, dg11, dg12, dg13], axis=0)]),
        'q_g': d_qg, 'k_g': d_kg_l + d_kg_c, 'conv_w': d_cw_l + d_cw_c, 'conv_b': d_cb_l + d_cb_c,
        'wa': jnp.stack([d_wa0, d_wa1]), 'ba': jnp.concatenate([d_ba0, d_ba1], axis=0),
        'wx': jnp.stack([d_wx0, d_wx1]), 'bx': jnp.concatenate([d_bx0, d_bx1], axis=0),
        'lam': jnp.concatenate([d_lam0, d_lam1], axis=0),
        'gm_b_in': d_gm_b_in, 'gm_v_g': d_vg, 'gm_v_b': d_vb, 'gm_w_sp': d_wsp, 'gm_b_sp': d_bsp_t.T,
        'loss': loss,
    }
    link.gradients('ar', {'ar_in': dw_ar_in, 'ar_out': dw_ar_out})
    return grad_x, small


MOD_ROWS = 16
SMALL_ORDER = ('d_ml0', 'd_ml1', 'd_mc0', 'norm_g', 'q_g', 'k_g', 'conv_w', 'conv_b', 'wa', 'ba', 'wx', 'bx', 'lam',
               'gm_b_in', 'gm_v_g', 'gm_v_b', 'gm_w_sp', 'gm_b_sp', 'loss')


def _silu(v):
    return v * _sigmoid(v)


def _chip_concat(gathered, axis):
    return jnp.concatenate([gathered[2 * q] for q in range(N_CHIPS)], axis=axis)


def kernel(x, c, ctx, c_ctx, w_mod, b_mod, norm_g, w_ff_in, w_ff_out, ar_w_in, ar_q_g, ar_k_g, ar_conv_w, ar_conv_b, ar_wa, ar_ba, ar_wx, ar_bx, ar_lambda, ar_w_out, gm_w_in, gm_b_in, gm_v_g, gm_v_b, gm_w_sp, gm_b_sp, gm_w_out, loss_target, m_c_ctx, m_w_mod, m_b_mod, m_norm_g, m_w_ff_in, m_w_ff_out, m_ar_w_in, m_ar_q_g, m_ar_k_g, m_ar_conv_w, m_ar_conv_b, m_ar_wa, m_ar_ba, m_ar_wx, m_ar_bx, m_ar_lambda, m_ar_w_out, m_gm_w_in, m_gm_b_in, m_gm_v_g, m_gm_v_b, m_gm_w_sp, m_gm_b_sp, m_gm_w_out, v_c_ctx, v_w_mod, v_b_mod, v_norm_g, v_w_ff_in, v_w_ff_out, v_ar_w_in, v_ar_q_g, v_ar_k_g, v_ar_conv_w, v_ar_conv_b, v_ar_wa, v_ar_ba, v_ar_wx, v_ar_bx, v_ar_lambda, v_ar_w_out, v_gm_w_in, v_gm_b_in, v_gm_v_g, v_gm_v_b, v_gm_w_sp, v_gm_b_sp, v_gm_w_out):
    weights = dict(c_ctx=c_ctx, w_mod=w_mod, b_mod=b_mod, norm_g=norm_g, w_ff_in=w_ff_in, w_ff_out=w_ff_out, ar_w_in=ar_w_in,
                   ar_q_g=ar_q_g, ar_k_g=ar_k_g, ar_conv_w=ar_conv_w, ar_conv_b=ar_conv_b, ar_wa=ar_wa, ar_ba=ar_ba, ar_wx=ar_wx,
                   ar_bx=ar_bx, ar_lambda=ar_lambda, ar_w_out=ar_w_out, gm_w_in=gm_w_in, gm_b_in=gm_b_in, gm_v_g=gm_v_g,
                   gm_v_b=gm_v_b, gm_w_sp=gm_w_sp, gm_b_sp=gm_b_sp, gm_w_out=gm_w_out)
    m_in = dict(c_ctx=m_c_ctx, w_mod=m_w_mod, b_mod=m_b_mod, norm_g=m_norm_g, w_ff_in=m_w_ff_in, w_ff_out=m_w_ff_out,
                ar_w_in=m_ar_w_in, ar_q_g=m_ar_q_g, ar_k_g=m_ar_k_g, ar_conv_w=m_ar_conv_w, ar_conv_b=m_ar_conv_b, ar_wa=m_ar_wa,
                ar_ba=m_ar_ba, ar_wx=m_ar_wx, ar_bx=m_ar_bx, ar_lambda=m_ar_lambda, ar_w_out=m_ar_w_out, gm_w_in=m_gm_w_in,
                gm_b_in=m_gm_b_in, gm_v_g=m_gm_v_g, gm_v_b=m_gm_v_b, gm_w_sp=m_gm_w_sp, gm_b_sp=m_gm_b_sp, gm_w_out=m_gm_w_out)
    v_in = dict(c_ctx=v_c_ctx, w_mod=v_w_mod, b_mod=v_b_mod, norm_g=v_norm_g, w_ff_in=v_w_ff_in, w_ff_out=v_w_ff_out,
                ar_w_in=v_ar_w_in, ar_q_g=v_ar_q_g, ar_k_g=v_ar_k_g, ar_conv_w=v_ar_conv_w, ar_conv_b=v_ar_conv_b, ar_wa=v_ar_wa,
                ar_ba=v_ar_ba, ar_wx=v_ar_wx, ar_bx=v_ar_bx, ar_lambda=v_ar_lambda, ar_w_out=v_ar_w_out, gm_w_in=v_gm_w_in,
                gm_b_in=v_gm_b_in, gm_v_g=v_gm_v_g, gm_v_b=v_gm_v_b, gm_w_sp=v_gm_w_sp, gm_b_sp=v_gm_b_sp, gm_w_out=v_gm_w_out)

    xi, yi, ci = lax.axis_index("x"), lax.axis_index("y"), lax.axis_index("c")
    chip = 2 * xi + yi
    dev = 4 * xi + 2 * yi + ci
    place = jnp.stack([chip, ci]).astype(jnp.int32)
    n_lat, d = x.shape[1], x.shape[2]
    d6 = 6 * d
    cols_mod = w_mod.shape[2]

    mine = [c, norm_g, ar_conv_w[0], ar_ba[0], ar_bx[0], ar_lambda[0], gm_b_in, gm_v_g, gm_v_b]
    gathered = _allgather8([_pack(mine)], "gather_small_params")[0]
    parts = _unpack_devices(gathered, [a.shape for a in mine])
    c_all = parts[0].reshape(N_DEV, d)
    sp = {'norm_g': _chip_concat(parts[1], 2), 'q_g': ar_q_g, 'k_g': ar_k_g, 'conv_w': _chip_concat(parts[2], 1),
          'conv_b': ar_conv_b, 'wa': ar_wa[0], 'ba': _chip_concat(parts[3], 1), 'wx': ar_wx[0], 'bx': _chip_concat(parts[4], 1),
          'lam': _chip_concat(parts[5], 1), 'gm_b_in': _chip_concat(parts[6], 1), 'gm_v_g': _chip_concat(parts[7], 1),
          'gm_v_b': _chip_concat(parts[8], 1), 'gm_w_sp': gm_w_sp[0], 'gm_b_sp': gm_b_sp[0]}

    def mod_operand(c_rows, cc):
        row = lax.broadcasted_iota(jnp.int32, (MOD_ROWS - N_DEV, d), 0)
        lower = jnp.where(row == 0, jnp.broadcast_to(_silu(cc), (MOD_ROWS - N_DEV, d)), 0.0)
        sig = _sigmoid(cc)
        return jnp.concatenate([_silu(c_rows), lower], axis=0), sig * (1.0 + cc * (1.0 - sig))

    s_mod, dsilu_ctx = _small(mod_operand, [((MOD_ROWS, d), F32), ((1, d), F32)], c_all, c_ctx[None, :], name="mod_operand")
    b_mod_mine = lax.dynamic_slice(b_mod, (0, chip * cols_mod), (2, cols_mod))
    mod = [_matmul(s_mod, w_mod[i], kind='nn', bias=b_mod_mine[i][None, :], out_dtype=F32, name=f"mod_fwd_{i}") for i in range(2)]
    mod_all = _allgather8([jnp.concatenate(mod, axis=0)], "gather_mod")[0]
    mod_all = _chip_concat(mod_all, 1).reshape(2, MOD_ROWS, d6)
    ml = [jnp.split(lax.dynamic_slice(mod_all[i], (dev, 0), (1, d6)), 6, axis=1) for i in range(2)]
    mc0 = jnp.split(mod_all[0, N_DEV:N_DEV + 1], 6, axis=1)[:2]

    names = ('w_ff_in', 'w_ff_out', 'ar_w_in', 'ar_w_out', 'gm_w_in', 'gm_w_out')
    keys = {'w_ff_in': ('ff_in0', 'ff_in1'), 'w_ff_out': ('ff_out0', 'ff_out1'), 'ar_w_in': ('ar_in',), 'ar_w_out': ('ar_out',),
            'gm_w_in': ('gm_in',), 'gm_w_out': ('gm_out',)}
    shards = {key: _cast_shard(weights[n], place, layer, f"cast_{key}") for n in names for layer, key in enumerate(keys[n])}
    link = _MeshLink(place, shards)

    grad_x, small = _local_step(x[0], ctx[0], loss_target[0], ml, mc0, sp, link)
    reduced = dict(zip(names, link.finish(grad_x)))

    small_list = [small[k] for k in SMALL_ORDER]
    small8 = _allgather8([_pack(small_list)], "gather_small_grads")[0]
    total = _unpack(_sum_devices(small8).reshape(-1), [a.shape for a in small_list])
    total = dict(zip(SMALL_ORDER, total))
    per_dev = _unpack_devices(small8, [(d6,), (d6,)])
    pad_rows = jnp.zeros((MOD_ROWS - N_DEV - 1, d6), F32)
    d_mod = [jnp.concatenate([per_dev[0], total['d_mc0'], pad_rows], axis=0),
             jnp.concatenate([per_dev[1], jnp.zeros((MOD_ROWS - N_DEV, d6), F32)], axis=0)]
    d_mod_mine = [lax.dynamic_slice(dm, (0, chip * cols_mod), (MOD_ROWS, cols_mod)) for dm in d_mod]
    g_w_mod = jnp.stack([_matmul(s_mod, d_mod_mine[i], kind='tn', out_dtype=F32, name=f"mod_dw_{i}") for i in range(2)])
    d_s_part = _matmul(d_mod_mine[0], w_mod[0], kind='nt', out_dtype=F32, name="mod_ds")
    d_s_all = _allgather8([d_s_part[N_DEV:]], "gather_mod_ds")[0]

    def c_ctx_grad(parts_, dsilu):
        acc = parts_[0, 0:1]
        for q in range(1, N_CHIPS):
            acc = acc + parts_[2 * q, 0:1]
        return (acc * dsilu,)

    g_c_ctx = _small(c_ctx_grad, [((1, d), F32)], d_s_all, dsilu_ctx, name="c_ctx_grad")[0].reshape(d)

    def mine_of(full_grad, axis, n_shard):
        return lax.dynamic_slice_in_dim(full_grad, chip * n_shard, n_shard, axis=axis)

    grads_out = {
        'c_ctx': g_c_ctx, 'w_mod': g_w_mod,
        'b_mod': jnp.stack([total['d_ml0'][0] + total['d_mc0'][0], total['d_ml1'][0]]),
        'norm_g': mine_of(total['norm_g'], 2, norm_g.shape[2]),
        'w_ff_in': reduced['w_ff_in'], 'w_ff_out': reduced['w_ff_out'], 'ar_w_in': reduced['ar_w_in'],
        'ar_q_g': total['q_g'], 'ar_k_g': total['k_g'], 'ar_conv_w': mine_of(total['conv_w'], 1, ar_conv_w.shape[2])[None],
        'ar_conv_b': total['conv_b'], 'ar_wa': total['wa'][None], 'ar_ba': mine_of(total['ba'], 1, ar_ba.shape[2])[None],
        'ar_wx': total['wx'][None], 'ar_bx': mine_of(total['bx'], 1, ar_bx.shape[2])[None],
        'ar_lambda': mine_of(total['lam'], 1, ar_lambda.shape[2])[None], 'ar_w_out': reduced['ar_w_out'],
        'gm_w_in': reduced['gm_w_in'], 'gm_b_in': mine_of(total['gm_b_in'], 1, gm_b_in.shape[1]),
        'gm_v_g': mine_of(total['gm_v_g'], 1, gm_v_g.shape[1]), 'gm_v_b': mine_of(total['gm_v_b'], 1, gm_v_b.shape[1]),
        'gm_w_sp': total['gm_w_sp'][None], 'gm_b_sp': total['gm_b_sp'][None], 'gm_w_out': reduced['gm_w_out'],
    }
    order = list(weights)
    stepped = [_adamw(weights[n], grads_out[n].reshape(weights[n].shape), m_in[n], v_in[n], f"adamw_{n}") for n in order]
    loss = total['loss'].reshape(())
    return (loss, grad_x[None], *[s[0] for s in stepped], *[s[1] for s in stepped], *[s[2] for s in stepped],
            *[s[3] for s in stepped])
```

```python
import functools
import math

import jax
import jax.numpy as jnp
from jax import lax
from jax.experimental import pallas as pl
from jax.experimental.pallas import tpu as pltpu

F32 = jnp.float32
BF16 = jnp.bfloat16
MESH = pl.DeviceIdType.MESH
ANY = pl.BlockSpec(memory_space=pl.ANY)

VMEM_LIMIT_BYTES = 52 * 1024 * 1024
LANES = 128
N_CHIPS = 4
N_DEV = 8

HEAD_DIM = 128
N_HEADS = 8
N_KV = 2
GROUP = N_HEADS // N_KV
ATTN_W = N_HEADS * HEAD_DIM
KV_W = N_KV * HEAD_DIM
D_RNN = 1024
RNN_BLOCKS = 8
RNN_BW = D_RNN // RNN_BLOCKS
CONV_W = 4
RG_C = 8.0
GRID_W = 64
ROPE_THETA = 10000.0
ROPE_PAIRS = HEAD_DIM // 4
GM_GROUPS = 16
CHUNK = 128
EPS = 1e-6
ADAM_LR, ADAM_B1, ADAM_B2, ADAM_EPS, ADAM_WD, ADAM_STEP = 0.001, 0.9, 0.999, 1e-08, 0.01, 10
GELU_C = math.sqrt(2.0 / math.pi)
LOG2E = math.log2(math.e)


def _params(sem=None):
    return pltpu.CompilerParams(dimension_semantics=sem, vmem_limit_bytes=VMEM_LIMIT_BYTES)


def _tile(dim, pref, unit):
    best = None
    t = unit
    while t <= min(dim, pref):
        if dim % t == 0:
            best = t
        t += unit
    return best if best is not None else dim


def _full(shape):
    nd = len(shape)
    return pl.BlockSpec(shape, lambda *_: (0,) * nd)


def _blocked_map(split, per_q):
    if split == 'n':
        return lambda r, c: (c // per_q, r, c % per_q)
    if split == 'k':
        return lambda r, c: (r // per_q, r % per_q, c)
    return lambda r, c: (r, c)


def _logical_shape(arr, split):
    if split == 'n':
        return arr.shape[1], arr.shape[0] * arr.shape[2]
    if split == 'k':
        return arr.shape[0] * arr.shape[1], arr.shape[2]
    return arr.shape


def _matmul(a, b, *, kind, name, out_dtype, b_split=None, out_split=None, bias=None, epilogue=None, extra=None,
            a_rows=None, pref=(1024, 1024, 2048)):
    b_rows, b_cols = _logical_shape(b, b_split)
    row0 = 0
    if kind == 'nn':
        m, kc = a.shape
        n = b_cols
        assert b_rows == kc
    elif kind == 'nt':
        m, kc = a.shape
        n = b_rows
        assert b_cols == kc
    if a_rows is not None:
        assert kind != 'tn'
        row0, m = a_rows
    if kind == 'tn':
        kc, m = a.shape
        n = b_cols
        assert b_rows == kc
    b_row_ext = b.shape[1] if b_split == 'k' else b_rows
    b_col_ext = b.shape[2] if b_split == 'n' else b_cols
    out_row_ext = m // N_CHIPS if out_split == 'k' else m
    out_col_ext = n // N_CHIPS if out_split == 'n' else n
    if kind == 'nn':
        ti = _tile(math.gcd(min(m, out_row_ext), row0), pref[0], 16)
        tj = _tile(math.gcd(b_col_ext, out_col_ext), pref[1], LANES)
        tl = _tile(b_row_ext, pref[2], LANES)
        a_spec = pl.BlockSpec((ti, tl), lambda i, j, l: (i + row0 // ti, l))
        b_tile, b_rc = (tl, tj), (lambda i, j, l: (l, j))
        dims = (((1,), (0,)), ((), ()))
    elif kind == 'nt':
        ti = _tile(math.gcd(min(m, out_row_ext), row0), pref[0], 16)
        tj = _tile(math.gcd(b_row_ext, out_col_ext), pref[1], LANES)
        tl = _tile(b_col_ext, pref[2], LANES)
        a_spec = pl.BlockSpec((ti, tl), lambda i, j, l: (i + row0 // ti, l))
        b_tile, b_rc = (tj, tl), (lambda i, j, l: (j, l))
        dims = (((1,), (1,)), ((), ()))
    else:
        ti = _tile(out_row_ext, pref[0], LANES)
        tj = _tile(math.gcd(b_col_ext, out_col_ext), pref[1], LANES)
        tl = _tile(b_row_ext, pref[2], 16)
        a_spec = pl.BlockSpec((tl, ti), lambda i, j, l: (l, i))
        b_tile, b_rc = (tl, tj), (lambda i, j, l: (l, j))
        dims = (((0,), (0,)), ((), ()))
    grid = (m // ti, n // tj, kc // tl)
    n_l = grid[2]

    if b_split is None:
        b_spec = pl.BlockSpec(b_tile, b_rc)
    else:
        per_q = (b.shape[2] // b_tile[1]) if b_split == 'n' else (b.shape[1] // b_tile[0])
        bmap = _blocked_map(b_split, per_q)
        b_spec = pl.BlockSpec((None,) + b_tile, lambda i, j, l: bmap(*b_rc(i, j, l)))
    if out_split is None:
        out_shape2 = (m, n)
        o_spec = pl.BlockSpec((ti, tj), lambda i, j, l: (i, j))
    else:
        out_shape2 = (N_CHIPS, m // N_CHIPS, n) if out_split == 'k' else (N_CHIPS, m, n // N_CHIPS)
        per_q = (out_shape2[2] // tj) if out_split == 'n' else (out_shape2[1] // ti)
        omap = _blocked_map(out_split, per_q)
        o_spec = pl.BlockSpec((None, ti, tj), lambda i, j, l: omap(i, j))

    in_specs = [a_spec, b_spec]
    operands = [a, b]
    if bias is not None:
        in_specs.append(pl.BlockSpec((1, tj), lambda i, j, l: (0, j)))
        operands.append(bias)
    if extra is not None:
        in_specs.append(pl.BlockSpec((ti, tj), lambda i, j, l: (i, j)))
        operands.append(extra)
    if epilogue == 'relu2':
        out_shape = (jax.ShapeDtypeStruct(out_shape2, out_dtype), jax.ShapeDtypeStruct(out_shape2, out_dtype))
        out_specs = (o_spec, o_spec)
    else:
        out_shape = jax.ShapeDtypeStruct(out_shape2, out_dtype)
        out_specs = o_spec
    has_bias, has_extra = bias is not None, extra is not None

    def body(*refs):
        a_ref, b_ref = refs[0], refs[1]
        pos = 2
        bias_ref = extra_ref = None
        if has_bias:
            bias_ref = refs[pos]
            pos += 1
        if has_extra:
            extra_ref = refs[pos]
            pos += 1
        outs = refs[pos:] if n_l == 1 else refs[pos:-1]

        def finish(acc):
            if has_bias:
                acc = acc + bias_ref[...]
            if epilogue == 'relu2':
                r = jnp.maximum(acc, 0.0)
                outs[0][...] = r.astype(outs[0].dtype)
                outs[1][...] = (r * r).astype(outs[1].dtype)
            elif epilogue == 'times2x':
                outs[0][...] = (acc * (2.0 * extra_ref[...].astype(F32))).astype(outs[0].dtype)
            else:
                outs[0][...] = acc.astype(outs[0].dtype)

        def product():
            return lax.dot_general(a_ref[...].astype(BF16), b_ref[...].astype(BF16), dims, preferred_element_type=F32)

        if n_l == 1:
            finish(product())
            return
        acc_ref = refs[-1]
        step = pl.program_id(2)

        @pl.when(step == 0)
        def _():
            acc_ref[...] = jnp.zeros_like(acc_ref)

        acc_ref[...] += product()

        @pl.when(step == n_l - 1)
        def _():
            finish(acc_ref[...])

    return pl.pallas_call(
        body, name=name, grid=grid, in_specs=in_specs, out_specs=out_specs, out_shape=out_shape,
        scratch_shapes=[] if n_l == 1 else [pltpu.VMEM((ti, tj), F32)],
        compiler_params=_params(("parallel", "parallel", "arbitrary")),
    )(*operands)


def _small(fn, out_shapes, *arrays, name):
    n_in = len(arrays)

    def body(*refs):
        res = fn(*[r[...] for r in refs[:n_in]])
        for o_ref, v in zip(refs[n_in:], res):
            o_ref[...] = v.astype(o_ref.dtype)

    return pl.pallas_call(
        body, name=name, out_shape=tuple(jax.ShapeDtypeStruct(s, d) for s, d in out_shapes),
        in_specs=[_full(a.shape) for a in arrays], out_specs=tuple(_full(s) for s, _ in out_shapes), grid=(1,),
        compiler_params=_params(("arbitrary",)),
    )(*arrays)


def _rows_tile(rows, cols, itemsize=4, budget=2 * 1024 * 1024):
    return _tile(rows, max(16, budget // (cols * itemsize)), 16)


def _rowwise(fn, out_dtypes, *arrays, name):
    rows, cols = arrays[0].shape
    tr = _rows_tile(rows, cols)
    n_in = len(arrays)

    def body(*refs):
        res = fn(*[r[...] for r in refs[:n_in]])
        for o_ref, v in zip(refs[n_in:], res):
            o_ref[...] = v.astype(o_ref.dtype)

    spec = pl.BlockSpec((tr, cols), lambda i: (i, 0))
    return pl.pallas_call(
        body, name=name, grid=(rows // tr,), in_specs=[spec] * n_in, out_specs=tuple(spec for _ in out_dtypes),
        out_shape=tuple(jax.ShapeDtypeStruct((rows, cols), d) for d in out_dtypes),
        compiler_params=_params(("parallel",)),
    )(*arrays)


def _as2d(a):
    return a.reshape(1, a.size) if a.ndim < 2 else a.reshape(-1, a.shape[-1])


def _cast_shard(w, place, layer, name):
    _, rows, cols = w.shape
    tr = _rows_tile(rows, cols)

    def body(place_ref, w_ref, o_ref):
        o_ref[...] = w_ref[...].astype(o_ref.dtype)

    return pl.pallas_call(
        body, name=name, out_shape=jax.ShapeDtypeStruct((N_CHIPS, rows, cols), BF16),
        grid_spec=pltpu.PrefetchScalarGridSpec(
            num_scalar_prefetch=1, grid=(rows // tr,),
            in_specs=[pl.BlockSpec((None, tr, cols), lambda i, pr: (layer, i, 0))],
            out_specs=pl.BlockSpec((None, tr, cols), lambda i, pr: (pr[0], i, 0))),
        compiler_params=_params(("parallel",)),
    )(place, w)


def _norm_fwd(x, g, a, b=None, res=None, *, out_dtype, name):
    rows, d = x.shape
    tr = _rows_tile(rows, d)
    has_b, has_res = b is not None, res is not None

    def body(*refs):
        x_ref, g_ref, a_ref = refs[:3]
        pos = 3
        xv = x_ref[...]
        rstd = lax.rsqrt(jnp.mean(xv * xv, axis=-1, keepdims=True) + EPS)
        y = (xv * rstd * g_ref[...]) * a_ref[...]
        if has_b:
            y = y + refs[pos][...]
            pos += 1
        if has_res:
            y = y + refs[pos][...]
            pos += 1
        refs[pos][...] = y.astype(refs[pos].dtype)

    row = pl.BlockSpec((tr, d), lambda i: (i, 0))
    vec = pl.BlockSpec((1, d), lambda i: (0, 0))
    operands, specs = [x, g, a], [row, vec, vec]
    if has_b:
        operands.append(b)
        specs.append(vec)
    if has_res:
        operands.append(res)
        specs.append(row)
    return pl.pallas_call(
        body, name=name, grid=(rows // tr,), in_specs=specs, out_specs=row,
        out_shape=jax.ShapeDtypeStruct((rows, d), out_dtype), compiler_params=_params(("parallel",)),
    )(*operands)


def _norm_bwd(dy, x, g, a, extra=None, *, out_dtype, name):
    rows, d = x.shape
    tr = _rows_tile(rows, d)
    has_extra = extra is not None

    def body(*refs):
        dy_ref, x_ref, g_ref, a_ref = refs[:4]
        pos = 4
        extra_ref = None
        if has_extra:
            extra_ref = refs[pos]
            pos += 1
        dx_ref, dg_ref, da_ref, db_ref = refs[pos:pos + 4]

        @pl.when(pl.program_id(0) == 0)
        def _():
            dg_ref[...] = jnp.zeros_like(dg_ref)
            da_ref[...] = jnp.zeros_like(da_ref)
            db_ref[...] = jnp.zeros_like(db_ref)

        xv = x_ref[...]
        dyv = dy_ref[...].astype(F32)
        rstd = lax.rsqrt(jnp.mean(xv * xv, axis=-1, keepdims=True) + EPS)
        nrm = xv * rstd
        gv = g_ref[...]
        da_ref[...] += jnp.sum(dyv * (nrm * gv), axis=0, keepdims=True)
        db_ref[...] += jnp.sum(dyv, axis=0, keepdims=True)
        dt = dyv * a_ref[...]
        dg_ref[...] += jnp.sum(dt * nrm, axis=0, keepdims=True)
        dn = dt * gv
        dx = rstd * (dn - nrm * jnp.mean(dn * nrm, axis=-1, keepdims=True))
        if has_extra:
            dx = dx + extra_ref[...]
        dx_ref[...] = dx.astype(dx_ref.dtype)

    row = pl.BlockSpec((tr, d), lambda i: (i, 0))
    vec = pl.BlockSpec((1, d), lambda i: (0, 0))
    operands, specs = [dy, x, g, a], [row, row, vec, vec]
    if has_extra:
        operands.append(extra)
        specs.append(row)
    vshape = jax.ShapeDtypeStruct((1, d), F32)
    return pl.pallas_call(
        body, name=name, grid=(rows // tr,), in_specs=specs, out_specs=(row, vec, vec, vec),
        out_shape=(jax.ShapeDtypeStruct((rows, d), out_dtype), vshape, vshape, vshape),
        compiler_params=_params(("arbitrary",)),
    )(*operands)


def _loss_head(y, target):
    rows, d = y.shape
    tr = _rows_tile(rows, d)

    def body(y_ref, t_ref, dy_ref, loss_ref):
        @pl.when(pl.program_id(0) == 0)
        def _():
            loss_ref[...] = jnp.zeros_like(loss_ref)

        err = y_ref[...] - t_ref[...]
        dy_ref[...] = err * (1.0 / d)
        loss_ref[...] += jnp.sum(jnp.sum(err * err, axis=-1, keepdims=True), axis=0, keepdims=True) * (0.5 / d)

    row = pl.BlockSpec((tr, d), lambda i: (i, 0))
    return pl.pallas_call(
        body, name="loss_head", grid=(rows // tr,), in_specs=[row, row], out_specs=(row, _full((1, 1))),
        out_shape=(jax.ShapeDtypeStruct((rows, d), F32), jax.ShapeDtypeStruct((1, 1), F32)),
        compiler_params=_params(("arbitrary",)),
    )(y, target)


def _rope_partner(v):
    lane = lax.broadcasted_iota(jnp.int32, v.shape, 1)
    up = pltpu.roll(v, HEAD_DIM - ROPE_PAIRS, 1)
    down = pltpu.roll(v, ROPE_PAIRS, 1)
    return jnp.where((lane % (2 * ROPE_PAIRS)) < ROPE_PAIRS, up, down)


def _qk_fwd(proj, q_g, k_g, cos, sin, *, name):
    rows = proj.shape[0]
    tr = _tile(rows, 256, 16)
    width = ATTN_W + 2 * KV_W

    def body(p_ref, qg_ref, kg_ref, cos_ref, sin_ref, q_ref, k_ref, v_ref):
        cosv, sinv = cos_ref[...], sin_ref[...]
        for h in range(N_HEADS + N_KV):
            xv = p_ref[:, h * HEAD_DIM:(h + 1) * HEAD_DIM]
            gain = qg_ref[...] if h < N_HEADS else kg_ref[...]
            t = xv * lax.rsqrt(jnp.mean(xv * xv, axis=-1, keepdims=True) + EPS) * gain
            y = t * cosv + _rope_partner(t) * sinv
            if h < N_HEADS:
                q_ref[:, h * HEAD_DIM:(h + 1) * HEAD_DIM] = y.astype(BF16)
            else:
                k_ref[:, (h - N_HEADS) * HEAD_DIM:(h - N_HEADS + 1) * HEAD_DIM] = y.astype(BF16)
        v_ref[...] = p_ref[:, ATTN_W + KV_W:width].astype(BF16)

    vec = _full((1, HEAD_DIM))
    tab = pl.BlockSpec((tr, HEAD_DIM), lambda i: (i, 0))
    return pl.pallas_call(
        body, name=name, grid=(rows // tr,),
        in_specs=[pl.BlockSpec((tr, width), lambda i: (i, 0)), vec, vec, tab, tab],
        out_specs=(pl.BlockSpec((tr, ATTN_W), lambda i: (i, 0)), pl.BlockSpec((tr, KV_W), lambda i: (i, 0)),
                   pl.BlockSpec((tr, KV_W), lambda i: (i, 0))),
        out_shape=(jax.ShapeDtypeStruct((rows, ATTN_W), BF16), jax.ShapeDtypeStruct((rows, KV_W), BF16),
                   jax.ShapeDtypeStruct((rows, KV_W), BF16)),
        compiler_params=_params(("parallel",)),
    )(proj, q_g, k_g, cos, sin)


def _qk_bwd(dq, dk, proj, q_g, k_g, cos, sin, *, name):
    rows = proj.shape[0]
    tr = _tile(rows, 256, 16)
    width = ATTN_W + KV_W
    has_q = dq is not None

    def body(*refs):
        pos = 0
        dq_ref = None
        if has_q:
            dq_ref = refs[0]
            pos = 1
        dk_ref, p_ref, qg_ref, kg_ref, cos_ref, sin_ref, dp_ref, dqg_ref, dkg_ref = refs[pos:pos + 9]

        @pl.when(pl.program_id(0) == 0)
        def _():
            dqg_ref[...] = jnp.zeros_like(dqg_ref)
            dkg_ref[...] = jnp.zeros_like(dkg_ref)

        cosv, sinv = cos_ref[...], sin_ref[...]
        for h in range(N_HEADS + N_KV):
            cols = slice(h * HEAD_DIM, (h + 1) * HEAD_DIM)
            if h < N_HEADS and not has_q:
                dp_ref[:, cols] = jnp.zeros((tr, HEAD_DIM), dp_ref.dtype)
                continue
            if h < N_HEADS:
                dyv, gain, dgain_ref = dq_ref[:, cols], qg_ref[...], dqg_ref
            else:
                hk = h - N_HEADS
                dyv, gain, dgain_ref = dk_ref[:, hk * HEAD_DIM:(hk + 1) * HEAD_DIM], kg_ref[...], dkg_ref
            dyv = dyv.astype(F32)
            dt = dyv * cosv + _rope_partner(dyv * sinv)
            xv = p_ref[:, cols]
            rstd = lax.rsqrt(jnp.mean(xv * xv, axis=-1, keepdims=True) + EPS)
            nrm = xv * rstd
            dgain_ref[...] += jnp.sum(dt * nrm, axis=0, keepdims=True)
            dn = dt * gain
            dp_ref[:, cols] = (rstd * (dn - nrm * jnp.mean(dn * nrm, axis=-1, keepdims=True))).astype(dp_ref.dtype)

    vec = _full((1, HEAD_DIM))
    tab = pl.BlockSpec((tr, HEAD_DIM), lambda i: (i, 0))
    operands = ([dq] if has_q else []) + [dk, proj, q_g, k_g, cos, sin]
    specs = ([pl.BlockSpec((tr, ATTN_W), lambda i: (i, 0))] if has_q else []) + [
        pl.BlockSpec((tr, KV_W), lambda i: (i, 0)), pl.BlockSpec((tr, width), lambda i: (i, 0)), vec, vec, tab, tab]
    return pl.pallas_call(
        body, name=name, grid=(rows // tr,), in_specs=specs,
        out_specs=(pl.BlockSpec((tr, width), lambda i: (i, 0)), vec, vec),
        out_shape=(jax.ShapeDtypeStruct((rows, width), BF16), jax.ShapeDtypeStruct((1, HEAD_DIM), F32),
                   jax.ShapeDtypeStruct((1, HEAD_DIM), F32)),
        compiler_params=_params(("arbitrary",)),
    )(*operands)


def _attn_fwd(q, k, v):
    n_q, n_k = q.shape[0], k.shape[0]
    tq = _tile(n_q, 256, 16)
    gw = GROUP * HEAD_DIM
    scale = HEAD_DIM ** -0.5

    def body(q_ref, k_ref, v_ref, o_ref, lse_ref):
        kv, vv = k_ref[...], v_ref[...]
        for g in range(GROUP):
            cols = slice(g * HEAD_DIM, (g + 1) * HEAD_DIM)
            s = lax.dot_general(q_ref[:, cols], kv, (((1,), (1,)), ((), ())), preferred_element_type=F32) * (scale * LOG2E)
            m = jnp.max(s, axis=-1, keepdims=True)
            p = jnp.exp2(s - m)
            l = jnp.sum(p, axis=-1, keepdims=True)
            o = jnp.dot(p.astype(BF16), vv, preferred_element_type=F32) / l
            o_ref[:, cols] = o.astype(o_ref.dtype)
            lse_ref[:, g:g + 1] = m + jnp.log(l) * LOG2E

    return pl.pallas_call(
        body, name="attn_fwd", grid=(N_KV, n_q // tq),
        in_specs=[pl.BlockSpec((tq, gw), lambda h, i: (i, h)), pl.BlockSpec((n_k, HEAD_DIM), lambda h, i: (0, h)),
                  pl.BlockSpec((n_k, HEAD_DIM), lambda h, i: (0, h))],
        out_specs=(pl.BlockSpec((tq, gw), lambda h, i: (i, h)), pl.BlockSpec((None, tq, GROUP), lambda h, i: (h, i, 0))),
        out_shape=(jax.ShapeDtypeStruct((n_q, ATTN_W), BF16), jax.ShapeDtypeStruct((N_KV, n_q, GROUP), F32)),
        compiler_params=_params(("parallel", "parallel")),
    )(q, k, v)


def _attn_bwd(q, k, v, o, lse, do):
    n_q, n_k = q.shape[0], k.shape[0]
    tq = _tile(n_q, 256, 16)
    gw = GROUP * HEAD_DIM
    scale = HEAD_DIM ** -0.5

    def body(q_ref, k_ref, v_ref, o_ref, lse_ref, do_ref, dq_ref, dk_ref, dv_ref):
        @pl.when(pl.program_id(1) == 0)
        def _():
            dk_ref[...] = jnp.zeros_like(dk_ref)
            dv_ref[...] = jnp.zeros_like(dv_ref)

        kv, vv = k_ref[...], v_ref[...]
        for g in range(GROUP):
            cols = slice(g * HEAD_DIM, (g + 1) * HEAD_DIM)
            qg = q_ref[:, cols]
            dof = do_ref[:, cols].astype(F32)
            dog = dof.astype(BF16)
            s = lax.dot_general(qg, kv, (((1,), (1,)), ((), ())), preferred_element_type=F32) * (scale * LOG2E)
            p = jnp.exp2(s - lse_ref[:, g:g + 1])
            delta = jnp.sum(dof * o_ref[:, cols].astype(F32), axis=-1, keepdims=True)
            dp = lax.dot_general(dog, vv, (((1,), (1,)), ((), ())), preferred_element_type=F32)
            ds = (p * (dp - delta) * scale).astype(BF16)
            pb = p.astype(BF16)
            dq_ref[:, cols] = jnp.dot(ds, kv, preferred_element_type=F32)
            dk_ref[...] += lax.dot_general(ds, qg, (((0,), (0,)), ((), ())), preferred_element_type=F32)
            dv_ref[...] += lax.dot_general(pb, dog, (((0,), (0,)), ((), ())), preferred_element_type=F32)

    qspec = pl.BlockSpec((tq, gw), lambda h, i: (i, h))
    kspec = pl.BlockSpec((n_k, HEAD_DIM), lambda h, i: (0, h))
    return pl.pallas_call(
        body, name="attn_bwd", grid=(N_KV, n_q // tq),
        in_specs=[qspec, kspec, kspec, qspec, pl.BlockSpec((None, tq, GROUP), lambda h, i: (h, i, 0)), qspec],
        out_specs=(qspec, kspec, kspec),
        out_shape=(jax.ShapeDtypeStruct((n_q, ATTN_W), F32), jax.ShapeDtypeStruct((n_k, KV_W), F32),
                   jax.ShapeDtypeStruct((n_k, KV_W), F32)),
        compiler_params=_params(("parallel", "arbitrary")),
    )(q, k, v, o, lse, do)


CONV_COLS = 256
XR_COL0 = ATTN_W + 2 * KV_W


def _shift_rows(v, off):
    if off == 0:
        return v
    n = v.shape[0]
    rolled = pltpu.roll(v, (-off) % n, 0)
    t = lax.broadcasted_iota(jnp.int32, v.shape, 0)
    keep = (t + off >= 0) & (t + off < n)
    return jnp.where(keep, rolled, 0.0)


def _conv_fwd(proj, w, b, *, name):
    rows = proj.shape[0]
    blk0 = XR_COL0 // CONV_COLS

    def body(x_ref, w_ref, b_ref, y_ref):
        xv = x_ref[...]
        y = b_ref[...] + jnp.zeros_like(xv)
        for j in range(CONV_W):
            y = y + _shift_rows(xv, j - CONV_W // 2) * w_ref[j:j + 1, :]
        y_ref[...] = y

    return pl.pallas_call(
        body, name=name, grid=(D_RNN // CONV_COLS,),
        in_specs=[pl.BlockSpec((rows, CONV_COLS), lambda i: (0, blk0 + i)),
                  pl.BlockSpec((CONV_W, CONV_COLS), lambda i: (0, i)), pl.BlockSpec((1, CONV_COLS), lambda i: (0, i))],
        out_specs=pl.BlockSpec((rows, CONV_COLS), lambda i: (0, i)),
        out_shape=jax.ShapeDtypeStruct((rows, D_RNN), F32), compiler_params=_params(("parallel",)),
    )(proj, w, b)


def _conv_bwd(d1, d2, proj, w, *, name):
    rows = proj.shape[0]
    blk0 = XR_COL0 // CONV_COLS

    def body(d1_ref, d2_ref, x_ref, w_ref, dx_ref, dw_ref, db_ref):
        dv = d1_ref[...] + d2_ref[...]
        xv = x_ref[...]
        dx = jnp.zeros_like(dv)
        for j in range(CONV_W):
            off = j - CONV_W // 2
            dx = dx + _shift_rows(dv, -off) * w_ref[j:j + 1, :]
            dw_ref[j:j + 1, :] = jnp.sum(dv * _shift_rows(xv, off), axis=0, keepdims=True)
        dx_ref[...] = dx.astype(dx_ref.dtype)
        db_ref[...] = jnp.sum(dv, axis=0, keepdims=True)

    col = pl.BlockSpec((rows, CONV_COLS), lambda i: (0, i))
    return pl.pallas_call(
        body, name=name, grid=(D_RNN // CONV_COLS,),
        in_specs=[col, col, pl.BlockSpec((rows, CONV_COLS), lambda i: (0, blk0 + i)),
                  pl.BlockSpec((CONV_W, CONV_COLS), lambda i: (0, i))],
        out_specs=(col, pl.BlockSpec((CONV_W, CONV_COLS), lambda i: (0, i)), pl.BlockSpec((1, CONV_COLS), lambda i: (0, i))),
        out_shape=(jax.ShapeDtypeStruct((rows, D_RNN), BF16), jax.ShapeDtypeStruct((CONV_W, D_RNN), F32),
                   jax.ShapeDtypeStruct((1, D_RNN), F32)),
        compiler_params=_params(("parallel",)),
    )(d1, d2, proj, w)


RNN_TB = 256
SCAN_ROWS = 8


def _sigmoid(z):
    return 1.0 / (1.0 + jnp.exp(-z))


def _softplus(z):
    return jnp.maximum(z, 0.0) + jnp.log(1.0 + jnp.exp(-jnp.abs(z)))


def _one_minus_exp(y):
    series = -y * (1.0 + y * (0.5 + y * (1.0 / 6.0 + y * (1.0 / 24.0))))
    return jnp.where(y > -0.03, series, 1.0 - jnp.exp(y))


def _rglru_gates(xv, wa_ref, ba_ref, wx_ref, bx_ref, lam_ref):
    xb = xv.astype(BF16)
    zr = jnp.concatenate([jnp.dot(xb[:, n * RNN_BW:(n + 1) * RNN_BW], wa_ref[n].astype(BF16),
                                  preferred_element_type=F32) for n in range(RNN_BLOCKS)], axis=-1) + ba_ref[...]
    zi = jnp.concatenate([jnp.dot(xb[:, n * RNN_BW:(n + 1) * RNN_BW], wx_ref[n].astype(BF16),
                                  preferred_element_type=F32) for n in range(RNN_BLOCKS)], axis=-1) + bx_ref[...]
    r = _sigmoid(zr)
    gi = _sigmoid(zi)
    sp = _softplus(-lam_ref[...])
    log_a = -RG_C * r * sp
    a = jnp.exp(log_a)
    s = jnp.sqrt(_one_minus_exp(2.0 * log_a))
    return r, gi, sp, a, s


def _scan_rows(n_rows, reverse, step_fn, carry):
    groups = n_rows // SCAN_ROWS

    def trip(gidx, carry):
        gi = (groups - 1 - gidx) if reverse else gidx
        base = pl.multiple_of(gi * SCAN_ROWS, SCAN_ROWS)
        return step_fn(base, carry)

    return lax.fori_loop(0, groups, trip, carry)


def _rglru_fwd(xs, wa, ba, wx, bx, lam, *, reverse, name):
    rows = xs.shape[0]
    tb = _tile(rows, RNN_TB, SCAN_ROWS)
    nb = rows // tb
    order = (lambda i: (nb - 1 - i, 0)) if reverse else (lambda i: (i, 0))

    def body(x_ref, wa_ref, ba_ref, wx_ref, bx_ref, lam_ref, h_ref, hp_ref, a_s, b_s, state):
        @pl.when(pl.program_id(0) == 0)
        def _():
            state[...] = jnp.zeros_like(state)

        xv = x_ref[...]
        _, gi, _, a, s = _rglru_gates(xv, wa_ref, ba_ref, wx_ref, bx_ref, lam_ref)
        a_s[...] = a
        b_s[...] = s * (gi * xv)

        def group(base, h):
            av = a_s[pl.ds(base, SCAN_ROWS), :]
            bv = b_s[pl.ds(base, SCAN_ROWS), :]
            outs, prevs = [None] * SCAN_ROWS, [None] * SCAN_ROWS
            for k in range(SCAN_ROWS):
                r_ = SCAN_ROWS - 1 - k if reverse else k
                prevs[r_] = h
                h = av[r_:r_ + 1, :] * h + bv[r_:r_ + 1, :]
                outs[r_] = h
            h_ref[pl.ds(base, SCAN_ROWS), :] = jnp.concatenate(outs, axis=0)
            hp_ref[pl.ds(base, SCAN_ROWS), :] = jnp.concatenate(prevs, axis=0)
            return h

        state[0:1, :] = _scan_rows(tb, reverse, group, state[0:1, :])

    blk = pl.BlockSpec((tb, D_RNN), order)
    wspec = _full((RNN_BLOCKS, RNN_BW, RNN_BW))
    vec = _full((1, D_RNN))
    return pl.pallas_call(
        body, name=name, grid=(nb,), in_specs=[blk, wspec, vec, wspec, vec, vec], out_specs=(blk, blk),
        out_shape=(jax.ShapeDtypeStruct((rows, D_RNN), F32), jax.ShapeDtypeStruct((rows, D_RNN), F32)),
        scratch_shapes=[pltpu.VMEM((tb, D_RNN), F32), pltpu.VMEM((tb, D_RNN), F32), pltpu.VMEM((SCAN_ROWS, D_RNN), F32)],
        compiler_params=_params(("arbitrary",)),
    )(xs, wa, ba, wx, bx, lam)


def _rglru_bwd(xs, h_prev, dh, wa, ba, wx, bx, lam, *, reverse, name):
    rows = xs.shape[0]
    tb = _tile(rows, RNN_TB, SCAN_ROWS)
    nb = rows // tb
    back = not reverse
    order = (lambda i: (nb - 1 - i, 0)) if back else (lambda i: (i, 0))

    def body(x_ref, hp_ref, dh_ref, wa_ref, ba_ref, wx_ref, bx_ref, lam_ref,
             dx_ref, dwa_ref, dba_ref, dwx_ref, dbx_ref, dlam_ref, a_s, g_s, state):
        @pl.when(pl.program_id(0) == 0)
        def _():
            state[...] = jnp.zeros_like(state)
            dwa_ref[...] = jnp.zeros_like(dwa_ref)
            dwx_ref[...] = jnp.zeros_like(dwx_ref)
            dba_ref[...] = jnp.zeros_like(dba_ref)
            dbx_ref[...] = jnp.zeros_like(dbx_ref)
            dlam_ref[...] = jnp.zeros_like(dlam_ref)

        xv = x_ref[...]
        r, gi, sp, a, s = _rglru_gates(xv, wa_ref, ba_ref, wx_ref, bx_ref, lam_ref)
        a_s[...] = a

        def group(base, carry):
            av = a_s[pl.ds(base, SCAN_ROWS), :]
            dv = dh_ref[pl.ds(base, SCAN_ROWS), :]
            outs = [None] * SCAN_ROWS
            for k in range(SCAN_ROWS):
                r_ = SCAN_ROWS - 1 - k if back else k
                gt = dv[r_:r_ + 1, :] + carry
                outs[r_] = gt
                carry = av[r_:r_ + 1, :] * gt
            g_s[pl.ds(base, SCAN_ROWS), :] = jnp.concatenate(outs, axis=0)
            return carry

        state[0:1, :] = _scan_rows(tb, back, group, state[0:1, :])

        gv = g_s[...]
        d_a = gv * hp_ref[...]
        d_s = gv * (gi * xv)
        d_gi = gv * (s * xv)
        dx = gv * (s * gi)
        d_log_a = d_a * a - d_s * (a * a) / s
        d_r = d_log_a * (-RG_C * sp)
        lamv = lam_ref[...]
        d_sp = jnp.sum(d_log_a * (-RG_C * r), axis=0, keepdims=True)
        dlam_ref[...] += d_sp * (-_sigmoid(-lamv))
        d_zr = d_r * r * (1.0 - r)
        d_zi = d_gi * gi * (1.0 - gi)
        dba_ref[...] += jnp.sum(d_zr, axis=0, keepdims=True)
        dbx_ref[...] += jnp.sum(d_zi, axis=0, keepdims=True)
        xb = xv.astype(BF16)
        zrb, zib = d_zr.astype(BF16), d_zi.astype(BF16)
        parts = []
        for n in range(RNN_BLOCKS):
            cols = slice(n * RNN_BW, (n + 1) * RNN_BW)
            dwa_ref[n] += lax.dot_general(xb[:, cols], zrb[:, cols], (((0,), (0,)), ((), ())), preferred_element_type=F32)
            dwx_ref[n] += lax.dot_general(xb[:, cols], zib[:, cols], (((0,), (0,)), ((), ())), preferred_element_type=F32)
            parts.append(
                lax.dot_general(zrb[:, cols], wa_ref[n].astype(BF16), (((1,), (1,)), ((), ())), preferred_element_type=F32)
                + lax.dot_general(zib[:, cols], wx_ref[n].astype(BF16), (((1,), (1,)), ((), ())), preferred_element_type=F32))
        dx_ref[...] = dx + jnp.concatenate(parts, axis=-1)

    blk = pl.BlockSpec((tb, D_RNN), order)
    wspec = _full((RNN_BLOCKS, RNN_BW, RNN_BW))
    vec = _full((1, D_RNN))
    wshape = jax.ShapeDtypeStruct((RNN_BLOCKS, RNN_BW, RNN_BW), F32)
    vshape = jax.ShapeDtypeStruct((1, D_RNN), F32)
    return pl.pallas_call(
        body, name=name, grid=(nb,), in_specs=[blk, blk, blk, wspec, vec, wspec, vec, vec],
        out_specs=(blk, wspec, vec, wspec, vec, vec),
        out_shape=(jax.ShapeDtypeStruct((rows, D_RNN), F32), wshape, vshape, wshape, vshape, vshape),
        scratch_shapes=[pltpu.VMEM((tb, D_RNN), F32), pltpu.VMEM((tb, D_RNN), F32), pltpu.VMEM((SCAN_ROWS, D_RNN), F32)],
        compiler_params=_params(("arbitrary",)),
    )(xs, h_prev, dh, wa, ba, wx, bx, lam)


def _assemble_d_proj(dp_qk_l, dp_qk_c, dv_all, d_xr_l, d_xr_c, d_gate):
    n_lat, n_ctx = dp_qk_l.shape[0], dp_qk_c.shape[0]
    tr = _tile(math.gcd(n_lat, n_ctx), 256, 16)
    nb_l, nb_c = n_lat // tr, n_ctx // tr
    w_qk = ATTN_W + KV_W

    def body(ql_ref, qc_ref, dv_ref, xl_ref, xc_ref, g_ref, o_ref):
        i = pl.program_id(0)
        o_ref[:, w_qk:XR_COL0] = dv_ref[...].astype(o_ref.dtype)

        @pl.when(i < nb_l)
        def _():
            o_ref[:, :w_qk] = ql_ref[...]
            o_ref[:, XR_COL0:GATE_COL0] = xl_ref[...]
            o_ref[:, GATE_COL0:] = g_ref[...]

        @pl.when(i >= nb_l)
        def _():
            o_ref[:, :w_qk] = qc_ref[...]
            o_ref[:, XR_COL0:GATE_COL0] = xc_ref[...]
            o_ref[:, GATE_COL0:] = jnp.zeros((tr, D_RNN), o_ref.dtype)

    lat = lambda i: (jnp.minimum(i, nb_l - 1), 0)
    ctx = lambda i: (jnp.maximum(i - nb_l, 0), 0)
    return pl.pallas_call(
        body, name="assemble_d_proj", grid=(nb_l + nb_c,),
        in_specs=[pl.BlockSpec((tr, w_qk), lat), pl.BlockSpec((tr, w_qk), ctx),
                  pl.BlockSpec((tr, KV_W), lambda i: (jnp.where(i < nb_l, i + nb_c, i - nb_l), 0)),
                  pl.BlockSpec((tr, D_RNN), lat), pl.BlockSpec((tr, D_RNN), ctx), pl.BlockSpec((tr, D_RNN), lat)],
        out_specs=pl.BlockSpec((tr, GATE_COL0 + D_RNN), lambda i: (i, 0)),
        out_shape=jax.ShapeDtypeStruct((n_lat + n_ctx, GATE_COL0 + D_RNN), BF16),
        compiler_params=_params(("parallel",)),
    )(dp_qk_l, dp_qk_c, dv_all, d_xr_l, d_xr_c, d_gate)


def _gelu(z):
    return 0.5 * z * (1.0 + jnp.tanh(GELU_C * (z + 0.044715 * z * z * z)))


def _gelu_grad(z):
    t = jnp.tanh(GELU_C * (z + 0.044715 * z * z * z))
    return 0.5 * (1.0 + t) + 0.5 * z * (1.0 - t * t) * (GELU_C * (1.0 + 3.0 * 0.044715 * z * z))


GATE_COL0 = XR_COL0 + D_RNN


RNN_OUT_COLS = 512


def _rnn_out_specs(rows, hf_off, hb_off):
    tr = _tile(rows, 256, 16)
    assert hf_off % tr == 0 and hb_off % tr == 0 and GATE_COL0 % RNN_OUT_COLS == 0
    fo, bo, go = hf_off // tr, hb_off // tr, GATE_COL0 // RNN_OUT_COLS
    hf_spec = pl.BlockSpec((tr, RNN_OUT_COLS), lambda i, j: (i + fo, j))
    hb_spec = pl.BlockSpec((tr, RNN_OUT_COLS), lambda i, j: (i + bo, j))
    gate_spec = pl.BlockSpec((tr, RNN_OUT_COLS), lambda i, j: (i, j + go))
    out_spec = pl.BlockSpec((tr, RNN_OUT_COLS), lambda i, j: (i, j))
    return (rows // tr, D_RNN // RNN_OUT_COLS), hf_spec, hb_spec, gate_spec, out_spec


def _rnn_out_fwd(hf, hb, proj, hf_off, hb_off):
    rows = proj.shape[0]
    grid, hf_spec, hb_spec, gate_spec, out_spec = _rnn_out_specs(rows, hf_off, hb_off)

    def body(hf_ref, hb_ref, g_ref, o_ref):
        o_ref[...] = ((hf_ref[...] + hb_ref[...]) * _gelu(g_ref[...])).astype(o_ref.dtype)

    return pl.pallas_call(
        body, name="rnn_out_fwd", grid=grid, in_specs=[hf_spec, hb_spec, gate_spec], out_specs=out_spec,
        out_shape=jax.ShapeDtypeStruct((rows, D_RNN), BF16), compiler_params=_params(("parallel", "parallel")),
    )(hf, hb, proj)


def _rnn_out_bwd(d_cat, hf, hb, proj, hf_off, hb_off):
    rows = proj.shape[0]
    grid, hf_spec, hb_spec, gate_spec, out_spec = _rnn_out_specs(rows, hf_off, hb_off)
    do = ATTN_W // RNN_OUT_COLS

    def body(d_ref, hf_ref, hb_ref, g_ref, dh_ref, dg_ref):
        dv, gv = d_ref[...].astype(F32), g_ref[...]
        dh_ref[...] = dv * _gelu(gv)
        dg_ref[...] = (dv * (hf_ref[...] + hb_ref[...]) * _gelu_grad(gv)).astype(dg_ref.dtype)

    tr = out_spec.block_shape[0]
    return pl.pallas_call(
        body, name="rnn_out_bwd", grid=grid,
        in_specs=[pl.BlockSpec((tr, RNN_OUT_COLS), lambda i, j: (i, j + do)), hf_spec, hb_spec, gate_spec],
        out_specs=(out_spec, out_spec),
        out_shape=(jax.ShapeDtypeStruct((rows, D_RNN), F32), jax.ShapeDtypeStruct((rows, D_RNN), BF16)),
        compiler_params=_params(("parallel", "parallel")),
    )(d_cat, hf, hb, proj)


def _gmlp_parts(z_ref, vg_ref, vb_ref, d_gm):
    zu, zv = z_ref[:, :d_gm], z_ref[:, d_gm:]
    u = _gelu(zu)
    v = _gelu(zv)
    mu = jnp.mean(v, axis=-1, keepdims=True)
    vc = v - mu
    rstd = lax.rsqrt(jnp.mean(vc * vc, axis=-1, keepdims=True) + EPS)
    vhat = vc * rstd
    vn = vhat * vg_ref[...] + vb_ref[...]
    return zu, zv, u, vhat, rstd, vn


def _gmlp_fwd(z, v_g, v_b, w_sp, b_sp_t):
    rows, d_gm = z.shape[0], z.shape[1] // 2
    tr = _tile(rows, 256, CHUNK)
    gwid = d_gm // GM_GROUPS

    def body(z_ref, vg_ref, vb_ref, w_ref, b_ref, o_ref):
        _, _, u, _, _, vn = _gmlp_parts(z_ref, vg_ref, vb_ref, d_gm)
        vnb = vn.astype(BF16)
        for g in range(GM_GROUPS):
            wg = w_ref[g].astype(BF16)
            for c in range(tr // CHUNK):
                rs, cs = slice(c * CHUNK, (c + 1) * CHUNK), slice(g * gwid, (g + 1) * gwid)
                sv = jnp.dot(wg, vnb[rs, cs], preferred_element_type=F32) + b_ref[:, g:g + 1]
                o_ref[rs, cs] = (u[rs, cs] * sv).astype(o_ref.dtype)

    return pl.pallas_call(
        body, name="gmlp_fwd", grid=(rows // tr,),
        in_specs=[pl.BlockSpec((tr, 2 * d_gm), lambda i: (i, 0)), _full((1, d_gm)), _full((1, d_gm)),
                  _full(w_sp.shape), _full(b_sp_t.shape)],
        out_specs=pl.BlockSpec((tr, d_gm), lambda i: (i, 0)),
        out_shape=jax.ShapeDtypeStruct((rows, d_gm), BF16), compiler_params=_params(("parallel",)),
    )(z, v_g, v_b, w_sp, b_sp_t)


def _gmlp_bwd(z, dgate, v_g, v_b, w_sp, b_sp_t):
    rows, d_gm = z.shape[0], z.shape[1] // 2
    tr = _tile(rows, 256, CHUNK)
    gwid = d_gm // GM_GROUPS

    def body(z_ref, dg_ref, vg_ref, vb_ref, w_ref, b_ref, dz_ref, dbin_ref, dvg_ref, dvb_ref, dw_ref, dbs_ref, dvn_s):
        @pl.when(pl.program_id(0) == 0)
        def _():
            dbin_ref[...] = jnp.zeros_like(dbin_ref)
            dvg_ref[...] = jnp.zeros_like(dvg_ref)
            dvb_ref[...] = jnp.zeros_like(dvb_ref)
            dw_ref[...] = jnp.zeros_like(dw_ref)
            dbs_ref[...] = jnp.zeros_like(dbs_ref)

        zu, zv, u, vhat, rstd, vn = _gmlp_parts(z_ref, vg_ref, vb_ref, d_gm)
        vnb = vn.astype(BF16)
        dgv = dg_ref[...].astype(F32)
        dsv = dgv * u
        dsvb = dsv.astype(BF16)
        for g in range(GM_GROUPS):
            wg = w_ref[g].astype(BF16)
            cs = slice(g * gwid, (g + 1) * gwid)
            for c in range(tr // CHUNK):
                rs = slice(c * CHUNK, (c + 1) * CHUNK)
                sv = jnp.dot(wg, vnb[rs, cs], preferred_element_type=F32) + b_ref[:, g:g + 1]
                dz_ref[rs, cs] = (dgv[rs, cs] * sv * _gelu_grad(zu[rs, cs])).astype(dz_ref.dtype)
                dw_ref[g] += lax.dot_general(dsvb[rs, cs], vnb[rs, cs], (((1,), (1,)), ((), ())),
                                             preferred_element_type=F32)
                dbs_ref[:, g:g + 1] += jnp.sum(dsv[rs, cs], axis=-1, keepdims=True)
                dvn_s[rs, cs] = lax.dot_general(wg, dsvb[rs, cs], (((0,), (0,)), ((), ())), preferred_element_type=F32)
        dvn = dvn_s[...]
        dvg_ref[...] += jnp.sum(dvn * vhat, axis=0, keepdims=True)
        dvb_ref[...] += jnp.sum(dvn, axis=0, keepdims=True)
        dvh = dvn * vg_ref[...]
        dv = rstd * (dvh - jnp.mean(dvh, axis=-1, keepdims=True) - vhat * jnp.mean(dvh * vhat, axis=-1, keepdims=True))
        dzv = dv * _gelu_grad(zv)
        dz_ref[:, d_gm:] = dzv.astype(dz_ref.dtype)
        dbin_ref[:, d_gm:] += jnp.sum(dzv, axis=0, keepdims=True)
        dbin_ref[:, :d_gm] += jnp.sum(dz_ref[:, :d_gm].astype(F32), axis=0, keepdims=True)

    return pl.pallas_call(
        body, name="gmlp_bwd", grid=(rows // tr,),
        in_specs=[pl.BlockSpec((tr, 2 * d_gm), lambda i: (i, 0)), pl.BlockSpec((tr, d_gm), lambda i: (i, 0)),
                  _full((1, d_gm)), _full((1, d_gm)), _full(w_sp.shape), _full(b_sp_t.shape)],
        out_specs=(pl.BlockSpec((tr, 2 * d_gm), lambda i: (i, 0)), _full((1, 2 * d_gm)), _full((1, d_gm)),
                   _full((1, d_gm)), _full(w_sp.shape), _full(b_sp_t.shape)),
        out_shape=(jax.ShapeDtypeStruct((rows, 2 * d_gm), BF16), jax.ShapeDtypeStruct((1, 2 * d_gm), F32),
                   jax.ShapeDtypeStruct((1, d_gm), F32), jax.ShapeDtypeStruct((1, d_gm), F32),
                   jax.ShapeDtypeStruct(w_sp.shape, F32), jax.ShapeDtypeStruct(b_sp_t.shape, F32)),
        scratch_shapes=[pltpu.VMEM((tr, d_gm), F32)],
        compiler_params=_params(("arbitrary",)),
    )(z, dgate, v_g, v_b, w_sp, b_sp_t)


def _adamw_math(w, g, m, v):
    m = ADAM_B1 * m + (1.0 - ADAM_B1) * g
    v = ADAM_B2 * v + (1.0 - ADAM_B2) * (g * g)
    m_hat = m / (1.0 - ADAM_B1 ** ADAM_STEP)
    v_hat = v / (1.0 - ADAM_B2 ** ADAM_STEP)
    delta = -ADAM_LR * (m_hat / (jnp.sqrt(v_hat) + ADAM_EPS) + ADAM_WD * w)
    return delta, m, v


def _adamw(w, g, m, v, name):
    shape = w.shape
    outs = _rowwise(_adamw_math, (F32, F32, F32), _as2d(w), _as2d(g), _as2d(m), _as2d(v), name=name)
    return (g.reshape(shape),) + tuple(o.reshape(shape) for o in outs)


PACK_COLS = 1024


def _pack(arrays):
    flat = jnp.concatenate([a.reshape(-1).astype(F32) for a in arrays])
    pad = (-flat.size) % (8 * PACK_COLS)
    return jnp.pad(flat, (0, pad)).reshape(-1, PACK_COLS)


def _unpack(flat, shapes):
    out, pos = [], 0
    for shp in shapes:
        n = math.prod(shp)
        out.append(flat[pos:pos + n].reshape(shp))
        pos += n
    return out


def _unpack_devices(packed8, shapes):
    flat8 = packed8.reshape(N_DEV, -1)
    out, pos = [], 0
    for shp in shapes:
        n = math.prod(shp)
        out.append(flat8[:, pos:pos + n].reshape((N_DEV,) + tuple(shp)))
        pos += n
    return out


def _sum_devices(g8):
    _, rows, cols = g8.shape
    tr = _rows_tile(rows, cols, budget=256 * 1024)

    def body(g_ref, o_ref):
        acc = g_ref[0]
        for d in range(1, N_DEV):
            acc = acc + g_ref[d]
        o_ref[...] = acc

    return pl.pallas_call(
        body, name="sum_devices", grid=(rows // tr,), in_specs=[pl.BlockSpec((N_DEV, tr, cols), lambda i: (0, i, 0))],
        out_specs=pl.BlockSpec((tr, cols), lambda i: (i, 0)), out_shape=jax.ShapeDtypeStruct((rows, cols), F32),
        compiler_params=_params(("parallel",)),
    )(g8)


def _place():
    return lax.axis_index("x"), lax.axis_index("y"), lax.axis_index("c")


def _other_chips(x, y):
    return [(1 - x, y), (x, 1 - y), (1 - x, 1 - y)]


def _remote(src, dst, send_sem, recv_sem, to):
    return pltpu.make_async_remote_copy(src_ref=src, dst_ref=dst, send_sem=send_sem, recv_sem=recv_sem, device_id=to,
                                        device_id_type=MESH)


def _comm_call(body, name, operands, out_shapes, n_remote, n_local, aliases=None):
    return pl.pallas_call(
        body, name=name, out_shape=tuple(out_shapes), in_specs=[ANY] * len(operands), out_specs=tuple(ANY for _ in out_shapes),
        scratch_shapes=[pltpu.SemaphoreType.DMA((n_remote,)), pltpu.SemaphoreType.DMA((n_remote,)),
                        pltpu.SemaphoreType.DMA((max(n_local, 1),))],
        input_output_aliases=aliases or {},
    )(*operands)


def _in_place(arrays):
    return [jax.ShapeDtypeStruct(a.shape, a.dtype) for a in arrays], {i: i for i in range(len(arrays))}


def _allgather8(arrs, name):
    n = len(arrs)

    def body(*refs):
        ins, outs = refs[:n], refs[n:2 * n]
        send, recv, lsem = refs[2 * n:]
        x, y, c = _place()
        me, sib = (x, y, c), (x, y, 1 - c)
        chips = _other_chips(x, y)

        def slot(t, px, py, pc):
            return outs[t].at[4 * px + 2 * py + pc]

        def cp(t, k, block, to, from_input=False):
            src = ins[t] if from_input else slot(t, *block)
            return _remote(src, slot(t, *block), send.at[7 * t + k], recv.at[7 * t + k], to)

        mine = [pltpu.make_async_copy(ins[t], slot(t, *me), lsem.at[t]) for t in range(n)]
        for cpy in mine:
            cpy.start()
        first = []
        for t in range(n):
            first.append(cp(t, 0, me, sib, True))
            first += [cp(t, 1 + j, me, (*chip, c), True) for j, chip in enumerate(chips)]
        for cpy in first:
            cpy.start()
        passed = []
        for t in range(n):
            for j, chip in enumerate(chips):
                cp(t, 1 + j, (*chip, c), me).wait_recv()
                fwd = cp(t, 4 + j, (*chip, c), sib)
                fwd.start()
                passed.append(fwd)
        for t in range(n):
            cp(t, 0, sib, me).wait_recv()
            for j, chip in enumerate(chips):
                cp(t, 4 + j, (*chip, 1 - c), me).wait_recv()
        for cpy in first + passed:
            cpy.wait_send()
        for cpy in mine:
            cpy.wait()

    outs = _comm_call(body, name, arrs, [jax.ShapeDtypeStruct((N_DEV,) + a.shape, a.dtype) for a in arrs], 7 * n, n)
    return list(outs)


def _gather_weights(bufs):
    n_u = len(bufs)

    def body(*refs):
        bufs_ = refs[n_u:2 * n_u]
        send, recv, _ = refs[2 * n_u:]
        x, y, c = _place()
        me, sib, q = (x, y, c), (x, y, 1 - c), 2 * x + y
        chips = _other_chips(x, y)
        sent = []
        for u in range(n_u):
            half = bufs_[u].shape[1] // 2
            mine = bufs_[u].at[q, pl.ds(c * half, half)]
            for j, chip in enumerate(chips):
                cpy = _remote(mine, mine, send.at[6 * u + j], recv.at[6 * u + j], (*chip, c))
                cpy.start()
                sent.append(cpy)
        for u in range(n_u):
            half = bufs_[u].shape[1] // 2
            for j, chip in enumerate(chips):
                landed = bufs_[u].at[2 * chip[0] + chip[1], pl.ds(c * half, half)]
                _remote(landed, landed, send.at[6 * u + j], recv.at[6 * u + j], me).wait_recv()
                cpy = _remote(landed, landed, send.at[6 * u + 3 + j], recv.at[6 * u + 3 + j], sib)
                cpy.start()
                sent.append(cpy)
        for u in range(n_u):
            half = bufs_[u].shape[1] // 2
            for j, chip in enumerate(chips):
                landed = bufs_[u].at[2 * chip[0] + chip[1], pl.ds((1 - c) * half, half)]
                _remote(landed, landed, send.at[6 * u + 3 + j], recv.at[6 * u + 3 + j], me).wait_recv()
        for cpy in sent:
            cpy.wait_send()

    shapes, aliases = _in_place(bufs)
    return list(_comm_call(body, "gather_weights", bufs, shapes, 6 * n_u, 0, aliases))


def _exchange_halves(grads):
    n = len(grads)

    def body(*refs):
        ins, outs = refs[:n], refs[n:2 * n]
        send, recv, _ = refs[2 * n:]
        x, y, c = _place()
        sib = (x, y, 1 - c)
        sent = []
        for k in range(n):
            half = ins[k].shape[1] // 2
            cpy = _remote(ins[k].at[pl.ds(0, N_CHIPS), pl.ds((1 - c) * half, half)], outs[k], send.at[k], recv.at[k], sib)
            cpy.start()
            sent.append(cpy)
        for cpy in sent:
            cpy.wait()

    shapes = [jax.ShapeDtypeStruct((N_CHIPS, g.shape[1] // 2, g.shape[2]), g.dtype) for g in grads]
    return list(_comm_call(body, "exchange_halves", grads, shapes, n, 0))


def _chips_all_to_all(sums):
    n = len(sums)

    def body(*refs):
        ins, outs = refs[:n], refs[n:2 * n]
        send, recv, _ = refs[2 * n:]
        x, y, c = _place()
        sent = []
        for k in range(n):
            for j, chip in enumerate(_other_chips(x, y)):
                cpy = _remote(ins[k].at[2 * chip[0] + chip[1]], outs[k].at[j], send.at[3 * k + j], recv.at[3 * k + j], (*chip, c))
                cpy.start()
                sent.append(cpy)
        for cpy in sent:
            cpy.wait()

    shapes = [jax.ShapeDtypeStruct((N_CHIPS - 1,) + s.shape[1:], s.dtype) for s in sums]
    return list(_comm_call(body, "chips_all_to_all", sums, shapes, 3 * n, 0))


def _join_halves(bufs):
    n = len(bufs)
    units = [(k, layer) for k in range(n) for layer in range(bufs[k].shape[0])]

    def body(*refs):
        bufs_ = refs[n:2 * n]
        send, recv, _ = refs[2 * n:]
        x, y, c = _place()
        sent = []
        for u, (k, layer) in enumerate(units):
            half = bufs_[k].shape[1] // 2
            mine = bufs_[k].at[layer, pl.ds(c * half, half)]
            cpy = _remote(mine, mine, send.at[u], recv.at[u], (x, y, 1 - c))
            cpy.start()
            sent.append(cpy)
        for u, (k, layer) in enumerate(units):
            half = bufs_[k].shape[1] // 2
            theirs = bufs_[k].at[layer, pl.ds((1 - c) * half, half)]
            _remote(theirs, theirs, send.at[u], recv.at[u], (x, y, c)).wait_recv()
        for cpy in sent:
            cpy.wait_send()

    shapes, aliases = _in_place(bufs)
    return list(_comm_call(body, "join_halves", bufs, shapes, len(units), 0, aliases))


def _add_halves(grad, other, place):
    _, rows, cols = grad.shape
    half = rows // 2
    tr = _rows_tile(half, cols, itemsize=2, budget=1024 * 1024)
    per_half = half // tr

    def body(place_ref, g_ref, o_ref, s_ref):
        s_ref[...] = (g_ref[...].astype(F32) + o_ref[...].astype(F32)).astype(s_ref.dtype)

    return pl.pallas_call(
        body, name="add_halves", out_shape=jax.ShapeDtypeStruct((N_CHIPS, half, cols), grad.dtype),
        grid_spec=pltpu.PrefetchScalarGridSpec(
            num_scalar_prefetch=1, grid=(N_CHIPS, per_half),
            in_specs=[pl.BlockSpec((None, tr, cols), lambda k, i, pr: (k, pr[1] * per_half + i, 0)),
                      pl.BlockSpec((None, tr, cols), lambda k, i, pr: (k, i, 0))],
            out_specs=pl.BlockSpec((None, tr, cols), lambda k, i, pr: (k, i, 0))),
        compiler_params=_params(("parallel", "parallel")),
    )(place, grad, other)


def _add_chips(sums, others, place, dest, layer, n_layers):
    _, half, cols = sums.shape
    tr = _rows_tile(half, cols, itemsize=4, budget=1024 * 1024)
    per_half = half // tr

    def body(place_ref, s_ref, o_ref, *rest):
        acc = s_ref[...].astype(F32)
        for j in range(N_CHIPS - 1):
            acc = acc + o_ref[j].astype(F32)
        rest[-1][...] = acc

    operands = [place, sums, others] + ([] if dest is None else [dest])
    return pl.pallas_call(
        body, name="add_chips", out_shape=jax.ShapeDtypeStruct((n_layers, 2 * half, cols), F32),
        grid_spec=pltpu.PrefetchScalarGridSpec(
            num_scalar_prefetch=1, grid=(per_half,),
            in_specs=[pl.BlockSpec((None, tr, cols), lambda i, pr: (pr[0], i, 0)),
                      pl.BlockSpec((N_CHIPS - 1, tr, cols), lambda i, pr: (0, i, 0))] + ([] if dest is None else [ANY]),
            out_specs=pl.BlockSpec((None, tr, cols), lambda i, pr: (layer, pr[1] * per_half + i, 0))),
        input_output_aliases={} if dest is None else {3: 0},
        compiler_params=_params(("parallel",)),
    )(*operands)


HBM = pl.BlockSpec(memory_space=pltpu.HBM)
SEM = pl.BlockSpec(memory_space=pltpu.SEMAPHORE)
DATAFLOW = pltpu.SideEffectType.DATAFLOW_SIDE_EFFECTING


def _split_start(name, bufs, copies, n_copies, after=None):
    n = len(bufs)
    extra = 0 if after is None else 1

    def body(*refs):
        for cpy in copies(refs[:n], refs[n + extra], refs[n + extra + 1]):
            cpy.start()
        refs[-1][...] = jnp.zeros_like(refs[-1])

    outs = pl.pallas_call(
        body, name=name,
        out_shape=(pltpu.SemaphoreType.DMA((n_copies,)), pltpu.SemaphoreType.DMA((n_copies,)),
                   *[pltpu.HBM(b.shape, b.dtype) for b in bufs], jax.ShapeDtypeStruct((8, LANES), F32)),
        in_specs=[HBM] * n + [ANY] * extra,
        out_specs=(SEM, SEM, *[HBM] * n, pl.BlockSpec(memory_space=pltpu.VMEM)),
        input_output_aliases={i: 2 + i for i in range(n)},
        compiler_params=pltpu.CompilerParams(has_side_effects=DATAFLOW),
    )(*[pltpu.with_memory_space_constraint(b, pltpu.HBM) for b in bufs], *([] if after is None else [after]))
    return outs[0], outs[1], list(outs[2:2 + n]), outs[-1]


def _split_wait(name, bufs, send, recv, copies, after):
    n = len(bufs)

    def body(*refs):
        for cpy in copies(refs[:n], refs[n], refs[n + 1]):
            cpy.wait_send()
            cpy.wait_recv()

    return list(pl.pallas_call(
        body, name=name, out_shape=tuple(pltpu.HBM(b.shape, b.dtype) for b in bufs),
        in_specs=[HBM] * n + [SEM, SEM, ANY], out_specs=tuple([HBM] * n),
        input_output_aliases={i: i for i in range(n)},
        compiler_params=pltpu.CompilerParams(has_side_effects=DATAFLOW),
    )(*bufs, send, recv, after))


def _gather_copies(bufs, send, recv):
    x, y, c = _place()
    out = []
    for u, buf in enumerate(bufs):
        half = buf.shape[1] // 2
        mine = buf.at[2 * x + y, pl.ds(c * half, half)]
        out += [_remote(mine, mine, send.at[3 * u + j], recv.at[3 * u + j], (*chip, c))
                for j, chip in enumerate(_other_chips(x, y))]
    return out


def _exchange_copies(bufs, send, recv):
    x, y, c = _place()
    n = len(bufs) // 2
    out = []
    for k in range(n):
        half = bufs[k].shape[1] // 2
        theirs = bufs[k].at[pl.ds(0, N_CHIPS), pl.ds((1 - c) * half, half)]
        out.append(_remote(theirs, bufs[n + k], send.at[k], recv.at[k], (x, y, 1 - c)))
    return out


def _all_to_all_copies(bufs, send, recv):
    x, y, c = _place()
    n = len(bufs) // 2
    return [_remote(bufs[k].at[2 * chip[0] + chip[1]], bufs[n + k].at[j], send.at[3 * k + j], recv.at[3 * k + j], (*chip, c))
            for k in range(n) for j, chip in enumerate(_other_chips(x, y))]


def _forward_copies(bufs, send, recv):
    x, y, c = _place()
    out = []
    for u, buf in enumerate(bufs):
        half = buf.shape[1] // 2
        for j, chip in enumerate(_other_chips(x, y)):
            landed = buf.at[2 * chip[0] + chip[1], pl.ds(c * half, half)]
            out.append(_remote(landed, landed, send.at[3 * u + j], recv.at[3 * u + j], (x, y, 1 - c)))
    return out


def _gather8_copies(bufs, send, recv):
    x, y, c = _place()
    mine = bufs[0].at[4 * x + 2 * y + c]
    targets = [(x, y, 1 - c)] + [(*chip, c) for chip in _other_chips(x, y)]
    return [_remote(mine, mine, send.at[k], recv.at[k], to) for k, to in enumerate(targets)]


def _forward_slots(buf, name):
    def body(_, buf_, send, recv, __):
        x, y, c = _place()
        chips = _other_chips(x, y)
        sent = []
        for j, chip in enumerate(chips):
            slot = buf_.at[4 * chip[0] + 2 * chip[1] + c]
            cpy = _remote(slot, slot, send.at[j], recv.at[j], (x, y, 1 - c))
            cpy.start()
            sent.append(cpy)
        for j, chip in enumerate(chips):
            slot = buf_.at[4 * chip[0] + 2 * chip[1] + 1 - c]
            _remote(slot, slot, send.at[j], recv.at[j], (x, y, c)).wait_recv()
        for cpy in sent:
            cpy.wait_send()

    shapes, aliases = _in_place([buf])
    return _comm_call(body, name, [buf], shapes, N_CHIPS - 1, 0, aliases)[0]


FWD_GROUPS = {'ffn0': ('ff_in0', 'ff_out0'), 'l1': ('gm_in', 'gm_out', 'ff_in1', 'ff_out1')}
GRAD_LAYOUT = {'ff_in0': (0, 0), 'ff_in1': (0, 1), 'ff_out0': (1, 0), 'ff_out1': (1, 1), 'ar_in': (2, 0), 'ar_out': (3, 0),
               'gm_in': (4, 0), 'gm_out': (5, 0)}


class _MeshLink:
    def __init__(self, place, shards):
        self.place = place
        ar = _gather_weights([shards['ar_in'], shards['ar_out']])
        self.ready = {'ar_in': ar[0], 'ar_out': ar[1]}
        self.pending, after = {}, ar[1]
        for group, names in FWD_GROUPS.items():
            send, recv, bufs, token = _split_start(f"gather_{group}_start", [shards[n] for n in names], _gather_copies,
                                                   3 * len(names), after)
            self.pending[group] = (names, send, recv, bufs)
            after = token
        self.start_token = after[0, 0]
        self.forwarding, self.exchanging, self.sent, self.last_token = {}, {}, {}, None

    def prefetch(self, group, after):
        names, send, recv, bufs = self.pending.pop(group)
        bufs = _split_wait(f"gather_{group}_wait", bufs, send, recv, _gather_copies, after)
        send, recv, bufs, token = _split_start(f"forward_{group}_start", bufs, _forward_copies, 3 * len(names))
        self.forwarding[group] = (names, send, recv, bufs)
        return token[0, 0]

    def weights(self, group, after):
        if group in self.forwarding:
            names, send, recv, bufs = self.forwarding.pop(group)
            self.ready.update(zip(names, _split_wait(f"forward_{group}_wait", bufs, send, recv, _forward_copies, after)))
        return self.ready

    def gradients(self, group, grads, after=None):
        tok = self.poll(next(iter(grads.values())))
        names, mine = list(grads), list(grads.values())
        landing = [lax.empty((N_CHIPS, g.shape[1] // 2, g.shape[2]), g.dtype) for g in mine]
        send, recv, bufs, token = _split_start(f"exchange_{group}_start", mine + landing, _exchange_copies, len(names), after)
        self.exchanging[group] = (names, send, recv, bufs)
        self.last_token = token
        return token[0, 0] + tok

    def poll(self, after):
        tok = 0.0
        for group in list(self.exchanging):
            names, send, recv, bufs = self.exchanging.pop(group)
            bufs = _split_wait(f"exchange_{group}_wait", bufs, send, recv, _exchange_copies, after)
            sums = [_add_halves(g, r, self.place) for g, r in zip(bufs[:len(names)], bufs[len(names):])]
            landing = [lax.empty((N_CHIPS - 1,) + s.shape[1:], s.dtype) for s in sums]
            send, recv, bufs, token = _split_start(f"grads_{group}_start", sums + landing, _all_to_all_copies, 3 * len(names))
            self.sent[group] = (names, send, recv, bufs)
            self.last_token = token
            tok = tok + token[0, 0]
        return tok

    def reduce(self, groups, after):
        units = {}
        for group in groups:
            names, send, recv, bufs = self.sent.pop(group)
            bufs = _split_wait(f"grads_{group}_wait", bufs, send, recv, _all_to_all_copies, after)
            units.update(zip(names, zip(bufs[:len(names)], bufs[len(names):])))
        n_layers = {p: 1 + max(l for pp, l in GRAD_LAYOUT.values() if pp == p) for p, _ in GRAD_LAYOUT.values()}
        out = {}
        for name, (p, layer) in GRAD_LAYOUT.items():
            if name in units:
                out[p] = _add_chips(*units[name], self.place, out.get(p), layer, n_layers[p])
        params = sorted(out)
        return dict(zip(params, _join_halves([out[p] for p in params])))


def _rope_tables(n):
    t = jnp.arange(n)
    freqs = ROPE_THETA ** (-jnp.arange(ROPE_PAIRS, dtype=F32) / ROPE_PAIRS)
    ang_r = (t // GRID_W).astype(F32)[:, None] * freqs
    ang_c = (t % GRID_W).astype(F32)[:, None] * freqs
    cos = jnp.concatenate([jnp.cos(ang_r), jnp.cos(ang_r), jnp.cos(ang_c), jnp.cos(ang_c)], axis=-1)
    sin = jnp.concatenate([-jnp.sin(ang_r), jnp.sin(ang_r), -jnp.sin(ang_c), jnp.sin(ang_c)], axis=-1)
    return cos, sin


def _ffn_fwd(h2, w1, w2, tag):
    r, a = _matmul(h2, w1, kind='nn', b_split='n', out_dtype=BF16, epilogue='relu2', name=f"ffn_in_{tag}")
    f = _matmul(a, w2, kind='nn', b_split='k', out_dtype=F32, name=f"ffn_out_{tag}")
    return r, a, f


def _ffn_bwd(d_f, h2, r, a, w1, w2, tag):
    d_u = _matmul(d_f, w2, kind='nt', b_split='k', out_dtype=BF16, epilogue='times2x', extra=r, name=f"ffn_out_dx_{tag}")
    d_w2 = _matmul(a, d_f, kind='tn', out_split='k', out_dtype=BF16, name=f"ffn_out_dw_{tag}")
    d_w1 = _matmul(h2, d_u, kind='tn', out_split='n', out_dtype=BF16, name=f"ffn_in_dw_{tag}")
    d_h2 = _matmul(d_u, w1, kind='nt', b_split='n', out_dtype=F32, name=f"ffn_in_dx_{tag}")
    return d_h2, d_w1, d_w2


class _LocalLink:
    def __init__(self, big):
        self.big, self.grads, self.start_token = big, {}, 0.0

    def prefetch(self, group, after):
        return 0.0

    def poll(self, after):
        return 0.0

    def weights(self, group, after):
        return self.big

    def gradients(self, group, grads):
        self.grads.update(grads)
        return 0.0


def _local_step(xl0, xc0, target, ml, mc0, sp, link):
    n_lat, n_ctx = xl0.shape[0], xc0.shape[0]
    one = lambda v: 1.0 + v
    g = [[sp['norm_g'][i, k][None, :] for k in range(4)] for i in range(2)]

    sh1, sc1, gt1, sh2, sc2, gt2 = ml[0]
    big = link.weights('ar', None)
    sh1 = sh1 + link.start_token
    hl = _norm_fwd(xl0, g[0][0], one(sc1), b=sh1, out_dtype=BF16, name="l0_mod1")
    hc = _norm_fwd(xc0, g[0][0], one(mc0[1]), b=mc0[0], out_dtype=BF16, name="l0_mod1_ctx")
    proj_l = _matmul(hl, big['ar_in'], kind='nn', b_split='n', out_dtype=F32, name="ar_in_lat")
    proj_c = _matmul(hc, big['ar_in'], kind='nn', b_split='n', out_dtype=F32, name="ar_in_ctx")
    cos_l, sin_l = _rope_tables(n_lat)
    cos_c, sin_c = jnp.ones((n_ctx, HEAD_DIM), F32), jnp.zeros((n_ctx, HEAD_DIM), F32)
    q_g, k_g = sp['q_g'], sp['k_g']
    q_l, k_l, v_l = _qk_fwd(proj_l, q_g, k_g, cos_l, sin_l, name="qk_fwd_lat")
    _, k_c, v_c = _qk_fwd(proj_c, q_g, k_g, cos_c, sin_c, name="qk_fwd_ctx")
    k_all = jnp.concatenate([k_c, k_l], axis=0)
    v_all = jnp.concatenate([v_c, v_l], axis=0)
    attn, lse = _attn_fwd(q_l, k_all, v_all)
    conv_b = sp['conv_b'] + link.prefetch('ffn0', attn)
    conv_l = _conv_fwd(proj_l, sp['conv_w'], conv_b, name="conv_fwd_lat")
    conv_c = _conv_fwd(proj_c, sp['conv_w'], conv_b, name="conv_fwd_ctx")
    xs_f = jnp.concatenate([conv_c, conv_l], axis=0)
    xs_r = jnp.concatenate([conv_l, conv_c], axis=0)
    rnn_w = [(sp['wa'][d], sp['ba'][d][None, :], sp['wx'][d], sp['bx'][d][None, :], sp['lam'][d][None, :]) for d in range(2)]
    h_f, hp_f = _rglru_fwd(xs_f, *rnn_w[0], reverse=False, name="rglru_fwd_f")
    h_r, hp_r = _rglru_fwd(xs_r, *rnn_w[1], reverse=True, name="rglru_fwd_r")
    rnn = _rnn_out_fwd(h_f, h_r, proj_l, n_ctx, 0)
    cat = jnp.concatenate([attn, rnn], axis=1)
    ol0 = _matmul(cat, big['ar_out'], kind='nn', b_split='k', out_dtype=F32, name="ar_out")
    xm0 = _norm_fwd(ol0, g[0][1], gt1, res=xl0, out_dtype=F32, name="l0_res1")
    h2_0 = _norm_fwd(xm0, g[0][2], one(sc2), b=sh2, out_dtype=BF16, name="l0_mod2")
    w_f0 = link.weights('ffn0', h2_0)
    r0, a0, f0 = _ffn_fwd(h2_0, w_f0['ff_in0'], w_f0['ff_out0'], "l0")
    xl1 = _norm_fwd(f0, g[0][3], gt2 + link.prefetch('l1', f0), res=xm0, out_dtype=F32, name="l0_res2")

    th1, tc1, tg1, th2, tc2, tg2 = ml[1]
    w_l1 = link.weights('l1', xl1)
    hl1 = _norm_fwd(xl1, g[1][0], one(tc1), b=th1, out_dtype=BF16, name="l1_mod1")
    z = _matmul(hl1, w_l1['gm_in'], kind='nn', b_split='n', bias=sp['gm_b_in'], out_dtype=F32, name="gm_in")
    b_sp_t = sp['gm_b_sp'].T
    gated = _gmlp_fwd(z, sp['gm_v_g'], sp['gm_v_b'], sp['gm_w_sp'], b_sp_t)
    ol1 = _matmul(gated, w_l1['gm_out'], kind='nn', b_split='k', out_dtype=F32, name="gm_out")
    xm1 = _norm_fwd(ol1, g[1][1], tg1, res=xl1, out_dtype=F32, name="l1_res1")
    h2_1 = _norm_fwd(xm1, g[1][2], one(tc2), b=th2, out_dtype=BF16, name="l1_mod2")
    r1, a1, f1 = _ffn_fwd(h2_1, w_l1['ff_in1'], w_l1['ff_out1'], "l1")
    y = _norm_fwd(f1, g[1][3], tg2, res=xm1, out_dtype=F32, name="l1_res2")

    dy, loss = _loss_head(y, target)

    d_f1, dg13, d_tg2, _ = _norm_bwd(dy, f1, g[1][3], tg2, out_dtype=BF16, name="l1_res2_bwd")
    d_h2, dw_ff_in1, dw_ff_out1 = _ffn_bwd(d_f1, h2_1, r1, a1, w_l1['ff_in1'], w_l1['ff_out1'], "l1")
    tok = link.gradients('ffn1', {'ff_in1': dw_ff_in1, 'ff_out1': dw_ff_out1})
    dxm1, dg12, d_tc2, d_th2 = _norm_bwd(d_h2, xm1, g[1][2], one(tc2) + tok, extra=dy, out_dtype=F32, name="l1_mod2_bwd")
    d_ol1, dg11, d_tg1, _ = _norm_bwd(dxm1, ol1, g[1][1], tg1, out_dtype=BF16, name="l1_res1_bwd")
    d_gated = _matmul(d_ol1, w_l1['gm_out'], kind='nt', b_split='k', out_dtype=F32, name="gm_out_dx")
    dw_gm_out = _matmul(gated, d_ol1, kind='tn', out_split='k', out_dtype=BF16, name="gm_out_dw")
    d_z, d_gm_b_in, d_vg, d_vb, d_wsp, d_bsp_t = _gmlp_bwd(z, d_gated, sp['gm_v_g'], sp['gm_v_b'], sp['gm_w_sp'], b_sp_t)
    dw_gm_in = _matmul(hl1, d_z, kind='tn', out_split='n', out_dtype=BF16, name="gm_in_dw")
    d_hl1 = _matmul(d_z, w_l1['gm_in'], kind='nt', b_split='n', out_dtype=F32, name="gm_in_dx")
    tok = link.gradients('gm', {'gm_in': dw_gm_in, 'gm_out': dw_gm_out})
    dxl1, dg10, d_tc1, d_th1 = _norm_bwd(d_hl1, xl1, g[1][0], one(tc1) + tok, extra=dxm1, out_dtype=F32, name="l1_mod1_bwd")

    d_f0, dg03, d_gt2, _ = _norm_bwd(dxl1, f0, g[0][3], gt2, out_dtype=BF16, name="l0_res2_bwd")
    d_h2, dw_ff_in0, dw_ff_out0 = _ffn_bwd(d_f0, h2_0, r0, a0, w_f0['ff_in0'], w_f0['ff_out0'], "l0")
    tok = link.gradients('ffn0', {'ff_in0': dw_ff_in0, 'ff_out0': dw_ff_out0})
    dxm0, dg02, d_sc2, d_sh2 = _norm_bwd(d_h2, xm0, g[0][2], one(sc2) + tok, extra=dxl1, out_dtype=F32, name="l0_mod2_bwd")
    d_ol0, dg01, d_gt1, _ = _norm_bwd(dxm0, ol0, g[0][1], gt1, out_dtype=BF16, name="l0_res1_bwd")
    d_cat = _matmul(d_ol0, big['ar_out'], kind='nt', b_split='k', out_dtype=F32, name="ar_out_dx")
    dw_ar_out = _matmul(cat, d_ol0, kind='tn', out_split='k', out_dtype=BF16, name="ar_out_dw")
    dq, dk_all, dv_all = _attn_bwd(q_l, k_all, v_all, attn, lse, d_cat)
    tok = link.poll(dq)
    d_h, d_gate = _rnn_out_bwd(d_cat, h_f, h_r, proj_l, n_ctx, 0)
    zeros_c = jnp.zeros((n_ctx, D_RNN), F32)
    rnn_wb = [(wa_, ba_ + tok, wx_, bx_, lam_) for wa_, ba_, wx_, bx_, lam_ in rnn_w]
    dxs_f, d_wa0, d_ba0, d_wx0, d_bx0, d_lam0 = _rglru_bwd(
        xs_f, hp_f, jnp.concatenate([zeros_c, d_h], axis=0), *rnn_wb[0], reverse=False, name="rglru_bwd_f")
    dxs_r, d_wa1, d_ba1, d_wx1, d_bx1, d_lam1 = _rglru_bwd(
        xs_r, hp_r, jnp.concatenate([d_h, zeros_c], axis=0), *rnn_wb[1], reverse=True, name="rglru_bwd_r")
    d_xr_l, d_cw_l, d_cb_l = _conv_bwd(dxs_f[n_ctx:], dxs_r[:n_lat], proj_l, sp['conv_w'], name="conv_bwd_lat")
    d_xr_c, d_cw_c, d_cb_c = _conv_bwd(dxs_f[:n_ctx], dxs_r[n_lat:], proj_c, sp['conv_w'], name="conv_bwd_ctx")
    dp_qk_l, d_qg, d_kg_l = _qk_bwd(dq, dk_all[n_ctx:], proj_l, q_g, k_g, cos_l, sin_l, name="qk_bwd_lat")
    dp_qk_c, _, d_kg_c = _qk_bwd(None, dk_all[:n_ctx], proj_c, q_g, k_g, cos_c, sin_c, name="qk_bwd_ctx")
    d_proj = _assemble_d_proj(dp_qk_l, dp_qk_c, dv_all, d_xr_l, d_xr_c, d_gate)
    dw_ar_in = _matmul(jnp.concatenate([hl, hc], axis=0), d_proj, kind='tn', out_split='n', out_dtype=BF16, name="ar_in_dw")
    d_hl = _matmul(d_proj, big['ar_in'], kind='nt', b_split='n', out_dtype=F32, a_rows=(0, n_lat), name="ar_in_dx_lat")
    d_hc = _matmul(d_proj, big['ar_in'], kind='nt', b_split='n', out_dtype=F32, a_rows=(n_lat, n_ctx), name="ar_in_dx_ctx")
    grad_x, dg00, d_sc1, d_sh1 = _norm_bwd(d_hl, xl0, g[0][0], one(sc1), extra=dxm0, out_dtype=F32, name="l0_mod1_bwd")
    _, dg00c, d_mc_scale, d_mc_shift = _norm_bwd(d_hc, xc0, g[0][0], one(mc0[1]), out_dtype=BF16, name="l0_mod1_ctx_bwd")

    zeros_d = jnp.zeros_like(d_sh1)
    small = {
        'd_ml0': jnp.concatenate([d_sh1, d_sc1, d_gt1, d_sh2, d_sc2, d_gt2], axis=1),
        'd_ml1': jnp.concatenate([d_th1, d_tc1, d_tg1, d_th2, d_tc2, d_tg2], axis=1),
        'd_mc0': jnp.concatenate([d_mc_shift, d_mc_scale] + [zeros_d] * 4, axis=1),
        'norm_g': jnp.stack([jnp.concatenate([dg00 + dg00c, dg01, dg02, dg03], axis=0),
                             jnp.concatenate([dg10, dg11, dg12, dg13], axis=0)]),
        'q_g': d_qg, 'k_g': d_kg_l + d_kg_c, 'conv_w': d_cw_l + d_cw_c, 'conv_b': d_cb_l + d_cb_c,
        'wa': jnp.stack([d_wa0, d_wa1]), 'ba': jnp.concatenate([d_ba0, d_ba1], axis=0),
        'wx': jnp.stack([d_wx0, d_wx1]), 'bx': jnp.concatenate([d_bx0, d_bx1], axis=0),
        'lam': jnp.concatenate([d_lam0, d_lam1], axis=0),
        'gm_b_in': d_gm_b_in, 'gm_v_g': d_vg, 'gm_v_b': d_vb, 'gm_w_sp': d_wsp, 'gm_b_sp': d_bsp_t.T,
        'loss': loss,
    }
    return grad_x, small, {'ar_in': dw_ar_in, 'ar_out': dw_ar_out}


MOD_ROWS = 16
SMALL_ORDER = ('d_ml0', 'd_ml1', 'd_mc0', 'norm_g', 'q_g', 'k_g', 'conv_w', 'conv_b', 'wa', 'ba', 'wx', 'bx', 'lam',
               'gm_b_in', 'gm_v_g', 'gm_v_b', 'gm_w_sp', 'gm_b_sp', 'loss')


def _silu(v):
    return v * _sigmoid(v)


def _chip_concat(gathered, axis):
    return jnp.concatenate([gathered[2 * q] for q in range(N_CHIPS)], axis=axis)


def kernel(x, c, ctx, c_ctx, w_mod, b_mod, norm_g, w_ff_in, w_ff_out, ar_w_in, ar_q_g, ar_k_g, ar_conv_w, ar_conv_b, ar_wa, ar_ba, ar_wx, ar_bx, ar_lambda, ar_w_out, gm_w_in, gm_b_in, gm_v_g, gm_v_b, gm_w_sp, gm_b_sp, gm_w_out, loss_target, m_c_ctx, m_w_mod, m_b_mod, m_norm_g, m_w_ff_in, m_w_ff_out, m_ar_w_in, m_ar_q_g, m_ar_k_g, m_ar_conv_w, m_ar_conv_b, m_ar_wa, m_ar_ba, m_ar_wx, m_ar_bx, m_ar_lambda, m_ar_w_out, m_gm_w_in, m_gm_b_in, m_gm_v_g, m_gm_v_b, m_gm_w_sp, m_gm_b_sp, m_gm_w_out, v_c_ctx, v_w_mod, v_b_mod, v_norm_g, v_w_ff_in, v_w_ff_out, v_ar_w_in, v_ar_q_g, v_ar_k_g, v_ar_conv_w, v_ar_conv_b, v_ar_wa, v_ar_ba, v_ar_wx, v_ar_bx, v_ar_lambda, v_ar_w_out, v_gm_w_in, v_gm_b_in, v_gm_v_g, v_gm_v_b, v_gm_w_sp, v_gm_b_sp, v_gm_w_out):
    weights = dict(c_ctx=c_ctx, w_mod=w_mod, b_mod=b_mod, norm_g=norm_g, w_ff_in=w_ff_in, w_ff_out=w_ff_out, ar_w_in=ar_w_in,
                   ar_q_g=ar_q_g, ar_k_g=ar_k_g, ar_conv_w=ar_conv_w, ar_conv_b=ar_conv_b, ar_wa=ar_wa, ar_ba=ar_ba, ar_wx=ar_wx,
                   ar_bx=ar_bx, ar_lambda=ar_lambda, ar_w_out=ar_w_out, gm_w_in=gm_w_in, gm_b_in=gm_b_in, gm_v_g=gm_v_g,
                   gm_v_b=gm_v_b, gm_w_sp=gm_w_sp, gm_b_sp=gm_b_sp, gm_w_out=gm_w_out)
    m_in = dict(c_ctx=m_c_ctx, w_mod=m_w_mod, b_mod=m_b_mod, norm_g=m_norm_g, w_ff_in=m_w_ff_in, w_ff_out=m_w_ff_out,
                ar_w_in=m_ar_w_in, ar_q_g=m_ar_q_g, ar_k_g=m_ar_k_g, ar_conv_w=m_ar_conv_w, ar_conv_b=m_ar_conv_b, ar_wa=m_ar_wa,
                ar_ba=m_ar_ba, ar_wx=m_ar_wx, ar_bx=m_ar_bx, ar_lambda=m_ar_lambda, ar_w_out=m_ar_w_out, gm_w_in=m_gm_w_in,
                gm_b_in=m_gm_b_in, gm_v_g=m_gm_v_g, gm_v_b=m_gm_v_b, gm_w_sp=m_gm_w_sp, gm_b_sp=m_gm_b_sp, gm_w_out=m_gm_w_out)
    v_in = dict(c_ctx=v_c_ctx, w_mod=v_w_mod, b_mod=v_b_mod, norm_g=v_norm_g, w_ff_in=v_w_ff_in, w_ff_out=v_w_ff_out,
                ar_w_in=v_ar_w_in, ar_q_g=v_ar_q_g, ar_k_g=v_ar_k_g, ar_conv_w=v_ar_conv_w, ar_conv_b=v_ar_conv_b, ar_wa=v_ar_wa,
                ar_ba=v_ar_ba, ar_wx=v_ar_wx, ar_bx=v_ar_bx, ar_lambda=v_ar_lambda, ar_w_out=v_ar_w_out, gm_w_in=v_gm_w_in,
                gm_b_in=v_gm_b_in, gm_v_g=v_gm_v_g, gm_v_b=v_gm_v_b, gm_w_sp=v_gm_w_sp, gm_b_sp=v_gm_b_sp, gm_w_out=v_gm_w_out)

    xi, yi, ci = lax.axis_index("x"), lax.axis_index("y"), lax.axis_index("c")
    chip = 2 * xi + yi
    dev = 4 * xi + 2 * yi + ci
    place = jnp.stack([chip, ci]).astype(jnp.int32)
    n_lat, d = x.shape[1], x.shape[2]
    d6 = 6 * d
    cols_mod = w_mod.shape[2]

    mine = [c, norm_g, ar_conv_w[0], ar_ba[0], ar_bx[0], ar_lambda[0], gm_b_in, gm_v_g, gm_v_b]
    gathered = _allgather8([_pack(mine)], "gather_small_params")[0]
    parts = _unpack_devices(gathered, [a.shape for a in mine])
    c_all = parts[0].reshape(N_DEV, d)
    sp = {'norm_g': _chip_concat(parts[1], 2), 'q_g': ar_q_g, 'k_g': ar_k_g, 'conv_w': _chip_concat(parts[2], 1),
          'conv_b': ar_conv_b, 'wa': ar_wa[0], 'ba': _chip_concat(parts[3], 1), 'wx': ar_wx[0], 'bx': _chip_concat(parts[4], 1),
          'lam': _chip_concat(parts[5], 1), 'gm_b_in': _chip_concat(parts[6], 1), 'gm_v_g': _chip_concat(parts[7], 1),
          'gm_v_b': _chip_concat(parts[8], 1), 'gm_w_sp': gm_w_sp[0], 'gm_b_sp': gm_b_sp[0]}

    def mod_operand(c_rows, cc):
        row = lax.broadcasted_iota(jnp.int32, (MOD_ROWS - N_DEV, d), 0)
        lower = jnp.where(row == 0, jnp.broadcast_to(_silu(cc), (MOD_ROWS - N_DEV, d)), 0.0)
        sig = _sigmoid(cc)
        return jnp.concatenate([_silu(c_rows), lower], axis=0), sig * (1.0 + cc * (1.0 - sig))

    s_mod, dsilu_ctx = _small(mod_operand, [((MOD_ROWS, d), F32), ((1, d), F32)], c_all, c_ctx[None, :], name="mod_operand")
    b_mod_mine = lax.dynamic_slice(b_mod, (0, chip * cols_mod), (2, cols_mod))
    mod = [_matmul(s_mod, w_mod[i], kind='nn', bias=b_mod_mine[i][None, :], out_dtype=F32, name=f"mod_fwd_{i}") for i in range(2)]
    mod_all = _allgather8([jnp.concatenate(mod, axis=0)], "gather_mod")[0]
    mod_all = _chip_concat(mod_all, 1).reshape(2, MOD_ROWS, d6)
    ml = [jnp.split(lax.dynamic_slice(mod_all[i], (dev, 0), (1, d6)), 6, axis=1) for i in range(2)]
    mc0 = jnp.split(mod_all[0, N_DEV:N_DEV + 1], 6, axis=1)[:2]

    names = ('w_ff_in', 'w_ff_out', 'ar_w_in', 'ar_w_out', 'gm_w_in', 'gm_w_out')
    keys = {'w_ff_in': ('ff_in0', 'ff_in1'), 'w_ff_out': ('ff_out0', 'ff_out1'), 'ar_w_in': ('ar_in',), 'ar_w_out': ('ar_out',),
            'gm_w_in': ('gm_in',), 'gm_w_out': ('gm_out',)}
    shards = {key: _cast_shard(weights[n], place, layer, f"cast_{key}") for n in names for layer, key in enumerate(keys[n])}
    link = _MeshLink(place, shards)

    grad_x, small, last_grads = _local_step(x[0], ctx[0], loss_target[0], ml, mc0, sp, link)

    def step(n, grad):
        return _adamw(weights[n], grad.reshape(weights[n].shape), m_in[n], v_in[n], f"adamw_{n}")

    small_list = [small[k] for k in SMALL_ORDER]
    pack = _pack(small_list)
    slots = lax.dynamic_update_slice(lax.empty((N_DEV,) + pack.shape, F32), pack[None], (dev, 0, 0))
    s_send, s_recv, slots, s_token = _split_start("small_grads_start", [slots], _gather8_copies, N_CHIPS, grad_x)
    link.gradients('ar', last_grads, s_token)
    link.poll(link.last_token)
    reduced = link.reduce(('ffn1', 'gm', 'ffn0'), link.last_token)
    stepped = {n: step(n, reduced[names.index(n)]) for n in ('w_ff_in', 'w_ff_out', 'gm_w_in', 'gm_w_out')}
    reduced = link.reduce(('ar',), stepped['gm_w_out'][1])
    stepped.update({n: step(n, reduced[names.index(n)]) for n in ('ar_w_in', 'ar_w_out')})
    slots = _split_wait("small_grads_wait", slots, s_send, s_recv, _gather8_copies, stepped['ar_w_out'][1])
    small8 = _forward_slots(slots[0], "small_grads_forward")
    total = _unpack(_sum_devices(small8).reshape(-1), [a.shape for a in small_list])
    total = dict(zip(SMALL_ORDER, total))
    per_dev = _unpack_devices(small8, [(d6,), (d6,)])
    pad_rows = jnp.zeros((MOD_ROWS - N_DEV - 1, d6), F32)
    d_mod = [jnp.concatenate([per_dev[0], total['d_mc0'], pad_rows], axis=0),
             jnp.concatenate([per_dev[1], jnp.zeros((MOD_ROWS - N_DEV, d6), F32)], axis=0)]
    d_mod_mine = [lax.dynamic_slice(dm, (0, chip * cols_mod), (MOD_ROWS, cols_mod)) for dm in d_mod]
    g_w_mod = jnp.stack([_matmul(s_mod, d_mod_mine[i], kind='tn', out_dtype=F32, name=f"mod_dw_{i}") for i in range(2)])
    d_s_part = _matmul(d_mod_mine[0], w_mod[0], kind='nt', out_dtype=F32, name="mod_ds")
    d_s_all = _allgather8([d_s_part[N_DEV:]], "gather_mod_ds")[0]

    def c_ctx_grad(parts_, dsilu):
        acc = parts_[0, 0:1]
        for q in range(1, N_CHIPS):
            acc = acc + parts_[2 * q, 0:1]
        return (acc * dsilu,)

    g_c_ctx = _small(c_ctx_grad, [((1, d), F32)], d_s_all, dsilu_ctx, name="c_ctx_grad")[0].reshape(d)

    def mine_of(full_grad, axis, n_shard):
        return lax.dynamic_slice_in_dim(full_grad, chip * n_shard, n_shard, axis=axis)

    grads_out = {
        'c_ctx': g_c_ctx, 'w_mod': g_w_mod,
        'b_mod': jnp.stack([total['d_ml0'][0] + total['d_mc0'][0], total['d_ml1'][0]]),
        'norm_g': mine_of(total['norm_g'], 2, norm_g.shape[2]),
        'ar_q_g': total['q_g'], 'ar_k_g': total['k_g'], 'ar_conv_w': mine_of(total['conv_w'], 1, ar_conv_w.shape[2])[None],
        'ar_conv_b': total['conv_b'], 'ar_wa': total['wa'][None], 'ar_ba': mine_of(total['ba'], 1, ar_ba.shape[2])[None],
        'ar_wx': total['wx'][None], 'ar_bx': mine_of(total['bx'], 1, ar_bx.shape[2])[None],
        'ar_lambda': mine_of(total['lam'], 1, ar_lambda.shape[2])[None],
        'gm_b_in': mine_of(total['gm_b_in'], 1, gm_b_in.shape[1]),
        'gm_v_g': mine_of(total['gm_v_g'], 1, gm_v_g.shape[1]), 'gm_v_b': mine_of(total['gm_v_b'], 1, gm_v_b.shape[1]),
        'gm_w_sp': total['gm_w_sp'][None], 'gm_b_sp': total['gm_b_sp'][None],
    }
    stepped.update({n: step(n, grad) for n, grad in grads_out.items()})
    stepped = [stepped[n] for n in weights]
    loss = total['loss'].reshape(())
    return (loss, grad_x[None], *[s[0] for s in stepped], *[s[1] for s in stepped], *[s[2] for s in stepped],
            *[s[3] for s in stepped])
```

```python
import functools
import math

import jax
import jax.numpy as jnp
from jax import lax
from jax.experimental import pallas as pl
from jax.experimental.pallas import tpu as pltpu

F32 = jnp.float32
BF16 = jnp.bfloat16
MESH = pl.DeviceIdType.MESH
ANY = pl.BlockSpec(memory_space=pl.ANY)

VMEM_LIMIT_BYTES = 52 * 1024 * 1024
LANES = 128
N_CHIPS = 4
N_DEV = 8

HEAD_DIM = 128
N_HEADS = 8
N_KV = 2
GROUP = N_HEADS // N_KV
ATTN_W = N_HEADS * HEAD_DIM
KV_W = N_KV * HEAD_DIM
D_RNN = 1024
RNN_BLOCKS = 8
RNN_BW = D_RNN // RNN_BLOCKS
CONV_W = 4
RG_C = 8.0
GRID_W = 64
ROPE_THETA = 10000.0
ROPE_PAIRS = HEAD_DIM // 4
GM_GROUPS = 16
CHUNK = 128
EPS = 1e-6
ADAM_LR, ADAM_B1, ADAM_B2, ADAM_EPS, ADAM_WD, ADAM_STEP = 0.001, 0.9, 0.999, 1e-08, 0.01, 10
GELU_C = math.sqrt(2.0 / math.pi)
LOG2E = math.log2(math.e)


def _params(sem=None):
    return pltpu.CompilerParams(dimension_semantics=sem, vmem_limit_bytes=VMEM_LIMIT_BYTES)


def _tile(dim, pref, unit):
    best = None
    t = unit
    while t <= min(dim, pref):
        if dim % t == 0:
            best = t
        t += unit
    return best if best is not None else dim


def _full(shape):
    nd = len(shape)
    return pl.BlockSpec(shape, lambda *_: (0,) * nd)


def _blocked_map(split, per_q):
    if split == 'n':
        return lambda r, c: (c // per_q, r, c % per_q)
    if split == 'k':
        return lambda r, c: (r // per_q, r % per_q, c)
    return lambda r, c: (r, c)


def _logical_shape(arr, split):
    if split == 'n':
        return arr.shape[1], arr.shape[0] * arr.shape[2]
    if split == 'k':
        return arr.shape[0] * arr.shape[1], arr.shape[2]
    return arr.shape


def _matmul(a, b, *, kind, name, out_dtype, b_split=None, out_split=None, bias=None, epilogue=None, extra=None,
            a_rows=None, pref=(1024, 1024, 2048)):
    b_rows, b_cols = _logical_shape(b, b_split)
    row0 = 0
    if kind == 'nn':
        m, kc = a.shape
        n = b_cols
        assert b_rows == kc
    elif kind == 'nt':
        m, kc = a.shape
        n = b_rows
        assert b_cols == kc
    if a_rows is not None:
        assert kind != 'tn'
        row0, m = a_rows
    if kind == 'tn':
        kc, m = a.shape
        n = b_cols
        assert b_rows == kc
    b_row_ext = b.shape[1] if b_split == 'k' else b_rows
    b_col_ext = b.shape[2] if b_split == 'n' else b_cols
    out_row_ext = m // N_CHIPS if out_split == 'k' else m
    out_col_ext = n // N_CHIPS if out_split == 'n' else n
    if kind == 'nn':
        ti = _tile(math.gcd(min(m, out_row_ext), row0), pref[0], 16)
        tj = _tile(math.gcd(b_col_ext, out_col_ext), pref[1], LANES)
        tl = _tile(b_row_ext, pref[2], LANES)
        a_spec = pl.BlockSpec((ti, tl), lambda i, j, l: (i + row0 // ti, l))
        b_tile, b_rc = (tl, tj), (lambda i, j, l: (l, j))
        dims = (((1,), (0,)), ((), ()))
    elif kind == 'nt':
        ti = _tile(math.gcd(min(m, out_row_ext), row0), pref[0], 16)
        tj = _tile(math.gcd(b_row_ext, out_col_ext), pref[1], LANES)
        tl = _tile(b_col_ext, pref[2], LANES)
        a_spec = pl.BlockSpec((ti, tl), lambda i, j, l: (i + row0 // ti, l))
        b_tile, b_rc = (tj, tl), (lambda i, j, l: (j, l))
        dims = (((1,), (1,)), ((), ()))
    else:
        ti = _tile(out_row_ext, pref[0], LANES)
        tj = _tile(math.gcd(b_col_ext, out_col_ext), pref[1], LANES)
        tl = _tile(b_row_ext, pref[2], 16)
        a_spec = pl.BlockSpec((tl, ti), lambda i, j, l: (l, i))
        b_tile, b_rc = (tl, tj), (lambda i, j, l: (l, j))
        dims = (((0,), (0,)), ((), ()))
    grid = (m // ti, n // tj, kc // tl)
    n_l = grid[2]

    if b_split is None:
        b_spec = pl.BlockSpec(b_tile, b_rc)
    else:
        per_q = (b.shape[2] // b_tile[1]) if b_split == 'n' else (b.shape[1] // b_tile[0])
        bmap = _blocked_map(b_split, per_q)
        b_spec = pl.BlockSpec((None,) + b_tile, lambda i, j, l: bmap(*b_rc(i, j, l)))
    if out_split is None:
        out_shape2 = (m, n)
        o_spec = pl.BlockSpec((ti, tj), lambda i, j, l: (i, j))
    else:
        out_shape2 = (N_CHIPS, m // N_CHIPS, n) if out_split == 'k' else (N_CHIPS, m, n // N_CHIPS)
        per_q = (out_shape2[2] // tj) if out_split == 'n' else (out_shape2[1] // ti)
        omap = _blocked_map(out_split, per_q)
        o_spec = pl.BlockSpec((None, ti, tj), lambda i, j, l: omap(i, j))

    in_specs = [a_spec, b_spec]
    operands = [a, b]
    if bias is not None:
        in_specs.append(pl.BlockSpec((1, tj), lambda i, j, l: (0, j)))
        operands.append(bias)
    if extra is not None:
        in_specs.append(pl.BlockSpec((ti, tj), lambda i, j, l: (i, j)))
        operands.append(extra)
    if epilogue == 'relu2':
        out_shape = (jax.ShapeDtypeStruct(out_shape2, out_dtype), jax.ShapeDtypeStruct(out_shape2, out_dtype))
        out_specs = (o_spec, o_spec)
    else:
        out_shape = jax.ShapeDtypeStruct(out_shape2, out_dtype)
        out_specs = o_spec
    has_bias, has_extra = bias is not None, extra is not None

    def body(*refs):
        a_ref, b_ref = refs[0], refs[1]
        pos = 2
        bias_ref = extra_ref = None
        if has_bias:
            bias_ref = refs[pos]
            pos += 1
        if has_extra:
            extra_ref = refs[pos]
            pos += 1
        outs = refs[pos:] if n_l == 1 else refs[pos:-1]

        def finish(acc):
            if has_bias:
                acc = acc + bias_ref[...]
            if epilogue == 'relu2':
                r = jnp.maximum(acc, 0.0)
                outs[0][...] = r.astype(outs[0].dtype)
                outs[1][...] = (r * r).astype(outs[1].dtype)
            elif epilogue == 'times2x':
                outs[0][...] = (acc * (2.0 * extra_ref[...].astype(F32))).astype(outs[0].dtype)
            else:
                outs[0][...] = acc.astype(outs[0].dtype)

        def product():
            return lax.dot_general(a_ref[...].astype(BF16), b_ref[...].astype(BF16), dims, preferred_element_type=F32)

        if n_l == 1:
            finish(product())
            return
        acc_ref = refs[-1]
        step = pl.program_id(2)

        @pl.when(step == 0)
        def _():
            acc_ref[...] = jnp.zeros_like(acc_ref)

        acc_ref[...] += product()

        @pl.when(step == n_l - 1)
        def _():
            finish(acc_ref[...])

    return pl.pallas_call(
        body, name=name, grid=grid, in_specs=in_specs, out_specs=out_specs, out_shape=out_shape,
        scratch_shapes=[] if n_l == 1 else [pltpu.VMEM((ti, tj), F32)],
        compiler_params=_params(("parallel", "parallel", "arbitrary")),
    )(*operands)


def _small(fn, out_shapes, *arrays, name):
    n_in = len(arrays)

    def body(*refs):
        res = fn(*[r[...] for r in refs[:n_in]])
        for o_ref, v in zip(refs[n_in:], res):
            o_ref[...] = v.astype(o_ref.dtype)

    return pl.pallas_call(
        body, name=name, out_shape=tuple(jax.ShapeDtypeStruct(s, d) for s, d in out_shapes),
        in_specs=[_full(a.shape) for a in arrays], out_specs=tuple(_full(s) for s, _ in out_shapes), grid=(1,),
        compiler_params=_params(("arbitrary",)),
    )(*arrays)


def _rows_tile(rows, cols, itemsize=4, budget=2 * 1024 * 1024):
    return _tile(rows, max(16, budget // (cols * itemsize)), 16)


def _rowwise(fn, out_dtypes, *arrays, name):
    rows, cols = arrays[0].shape
    tr = _rows_tile(rows, cols)
    n_in = len(arrays)

    def body(*refs):
        res = fn(*[r[...] for r in refs[:n_in]])
        for o_ref, v in zip(refs[n_in:], res):
            o_ref[...] = v.astype(o_ref.dtype)

    spec = pl.BlockSpec((tr, cols), lambda i: (i, 0))
    return pl.pallas_call(
        body, name=name, grid=(rows // tr,), in_specs=[spec] * n_in, out_specs=tuple(spec for _ in out_dtypes),
        out_shape=tuple(jax.ShapeDtypeStruct((rows, cols), d) for d in out_dtypes),
        compiler_params=_params(("parallel",)),
    )(*arrays)


def _as2d(a):
    return a.reshape(1, a.size) if a.ndim < 2 else a.reshape(-1, a.shape[-1])


def _cast_shard(w, place, layer, name):
    _, rows, cols = w.shape
    tr = _rows_tile(rows, cols)

    def body(place_ref, w_ref, o_ref):
        o_ref[...] = w_ref[...].astype(o_ref.dtype)

    return pl.pallas_call(
        body, name=name, out_shape=jax.ShapeDtypeStruct((N_CHIPS, rows, cols), BF16),
        grid_spec=pltpu.PrefetchScalarGridSpec(
            num_scalar_prefetch=1, grid=(rows // tr,),
            in_specs=[pl.BlockSpec((None, tr, cols), lambda i, pr: (layer, i, 0))],
            out_specs=pl.BlockSpec((None, tr, cols), lambda i, pr: (pr[0], i, 0))),
        compiler_params=_params(("parallel",)),
    )(place, w)


def _norm_fwd(x, g, a, b=None, res=None, *, out_dtype, name):
    rows, d = x.shape
    tr = _rows_tile(rows, d, budget=4 * 1024 * 1024)
    has_b, has_res = b is not None, res is not None

    def body(*refs):
        x_ref, g_ref, a_ref = refs[:3]
        pos = 3
        xv = x_ref[...]
        rstd = lax.rsqrt(jnp.mean(xv * xv, axis=-1, keepdims=True) + EPS)
        y = (xv * rstd * g_ref[...]) * a_ref[...]
        if has_b:
            y = y + refs[pos][...]
            pos += 1
        if has_res:
            y = y + refs[pos][...]
            pos += 1
        refs[pos][...] = y.astype(refs[pos].dtype)

    row = pl.BlockSpec((tr, d), lambda i: (i, 0))
    vec = pl.BlockSpec((1, d), lambda i: (0, 0))
    operands, specs = [x, g, a], [row, vec, vec]
    if has_b:
        operands.append(b)
        specs.append(vec)
    if has_res:
        operands.append(res)
        specs.append(row)
    return pl.pallas_call(
        body, name=name, grid=(rows // tr,), in_specs=specs, out_specs=row,
        out_shape=jax.ShapeDtypeStruct((rows, d), out_dtype), compiler_params=_params(("parallel",)),
    )(*operands)


def _norm_bwd(dy, x, g, a, extra=None, *, out_dtype, name):
    rows, d = x.shape
    tr = _rows_tile(rows, d)
    has_extra = extra is not None

    def body(*refs):
        dy_ref, x_ref, g_ref, a_ref = refs[:4]
        pos = 4
        extra_ref = None
        if has_extra:
            extra_ref = refs[pos]
            pos += 1
        dx_ref, dg_ref, da_ref, db_ref = refs[pos:pos + 4]

        @pl.when(pl.program_id(0) == 0)
        def _():
            dg_ref[...] = jnp.zeros_like(dg_ref)
            da_ref[...] = jnp.zeros_like(da_ref)
            db_ref[...] = jnp.zeros_like(db_ref)

        xv = x_ref[...]
        dyv = dy_ref[...].astype(F32)
        rstd = lax.rsqrt(jnp.mean(xv * xv, axis=-1, keepdims=True) + EPS)
        nrm = xv * rstd
        gv = g_ref[...]
        da_ref[...] += jnp.sum(dyv * (nrm * gv), axis=0, keepdims=True)
        db_ref[...] += jnp.sum(dyv, axis=0, keepdims=True)
        dt = dyv * a_ref[...]
        dg_ref[...] += jnp.sum(dt * nrm, axis=0, keepdims=True)
        dn = dt * gv
        dx = rstd * (dn - nrm * jnp.mean(dn * nrm, axis=-1, keepdims=True))
        if has_extra:
            dx = dx + extra_ref[...]
        dx_ref[...] = dx.astype(dx_ref.dtype)

    row = pl.BlockSpec((tr, d), lambda i: (i, 0))
    vec = pl.BlockSpec((1, d), lambda i: (0, 0))
    operands, specs = [dy, x, g, a], [row, row, vec, vec]
    if has_extra:
        operands.append(extra)
        specs.append(row)
    vshape = jax.ShapeDtypeStruct((1, d), F32)
    return pl.pallas_call(
        body, name=name, grid=(rows // tr,), in_specs=specs, out_specs=(row, vec, vec, vec),
        out_shape=(jax.ShapeDtypeStruct((rows, d), out_dtype), vshape, vshape, vshape),
        compiler_params=_params(("arbitrary",)),
    )(*operands)


def _loss_head(y, target):
    rows, d = y.shape
    tr = _rows_tile(rows, d)

    def body(y_ref, t_ref, dy_ref, loss_ref):
        @pl.when(pl.program_id(0) == 0)
        def _():
            loss_ref[...] = jnp.zeros_like(loss_ref)

        err = y_ref[...] - t_ref[...]
        dy_ref[...] = err * (1.0 / d)
        loss_ref[...] += jnp.sum(jnp.sum(err * err, axis=-1, keepdims=True), axis=0, keepdims=True) * (0.5 / d)

    row = pl.BlockSpec((tr, d), lambda i: (i, 0))
    return pl.pallas_call(
        body, name="loss_head", grid=(rows // tr,), in_specs=[row, row], out_specs=(row, _full((1, 1))),
        out_shape=(jax.ShapeDtypeStruct((rows, d), F32), jax.ShapeDtypeStruct((1, 1), F32)),
        compiler_params=_params(("arbitrary",)),
    )(y, target)


def _rope_partner(v):
    lane = lax.broadcasted_iota(jnp.int32, v.shape, 1)
    up = pltpu.roll(v, HEAD_DIM - ROPE_PAIRS, 1)
    down = pltpu.roll(v, ROPE_PAIRS, 1)
    return jnp.where((lane % (2 * ROPE_PAIRS)) < ROPE_PAIRS, up, down)


def _qk_fwd(proj, q_g, k_g, cos, sin, *, name):
    rows = proj.shape[0]
    tr = _tile(rows, 256, 16)
    width = ATTN_W + 2 * KV_W

    def body(p_ref, qg_ref, kg_ref, cos_ref, sin_ref, q_ref, k_ref, v_ref):
        cosv, sinv = cos_ref[...], sin_ref[...]
        for h in range(N_HEADS + N_KV):
            xv = p_ref[:, h * HEAD_DIM:(h + 1) * HEAD_DIM]
            gain = qg_ref[...] if h < N_HEADS else kg_ref[...]
            t = xv * lax.rsqrt(jnp.mean(xv * xv, axis=-1, keepdims=True) + EPS) * gain
            y = t * cosv + _rope_partner(t) * sinv
            if h < N_HEADS:
                q_ref[:, h * HEAD_DIM:(h + 1) * HEAD_DIM] = y.astype(BF16)
            else:
                k_ref[:, (h - N_HEADS) * HEAD_DIM:(h - N_HEADS + 1) * HEAD_DIM] = y.astype(BF16)
        v_ref[...] = p_ref[:, ATTN_W + KV_W:width].astype(BF16)

    vec = _full((1, HEAD_DIM))
    tab = pl.BlockSpec((tr, HEAD_DIM), lambda i: (i, 0))
    return pl.pallas_call(
        body, name=name, grid=(rows // tr,),
        in_specs=[pl.BlockSpec((tr, width), lambda i: (i, 0)), vec, vec, tab, tab],
        out_specs=(pl.BlockSpec((tr, ATTN_W), lambda i: (i, 0)), pl.BlockSpec((tr, KV_W), lambda i: (i, 0)),
                   pl.BlockSpec((tr, KV_W), lambda i: (i, 0))),
        out_shape=(jax.ShapeDtypeStruct((rows, ATTN_W), BF16), jax.ShapeDtypeStruct((rows, KV_W), BF16),
                   jax.ShapeDtypeStruct((rows, KV_W), BF16)),
        compiler_params=_params(("parallel",)),
    )(proj, q_g, k_g, cos, sin)


def _qk_bwd(dq, dk, proj, q_g, k_g, cos, sin, *, name):
    rows = proj.shape[0]
    tr = _tile(rows, 256, 16)
    width = ATTN_W + KV_W
    has_q = dq is not None

    def body(*refs):
        pos = 0
        dq_ref = None
        if has_q:
            dq_ref = refs[0]
            pos = 1
        dk_ref, p_ref, qg_ref, kg_ref, cos_ref, sin_ref, dp_ref, dqg_ref, dkg_ref = refs[pos:pos + 9]

        @pl.when(pl.program_id(0) == 0)
        def _():
            dqg_ref[...] = jnp.zeros_like(dqg_ref)
            dkg_ref[...] = jnp.zeros_like(dkg_ref)

        cosv, sinv = cos_ref[...], sin_ref[...]
        for h in range(N_HEADS + N_KV):
            cols = slice(h * HEAD_DIM, (h + 1) * HEAD_DIM)
            if h < N_HEADS and not has_q:
                dp_ref[:, cols] = jnp.zeros((tr, HEAD_DIM), dp_ref.dtype)
                continue
            if h < N_HEADS:
                dyv, gain, dgain_ref = dq_ref[:, cols], qg_ref[...], dqg_ref
            else:
                hk = h - N_HEADS
                dyv, gain, dgain_ref = dk_ref[:, hk * HEAD_DIM:(hk + 1) * HEAD_DIM], kg_ref[...], dkg_ref
            dyv = dyv.astype(F32)
            dt = dyv * cosv + _rope_partner(dyv * sinv)
            xv = p_ref[:, cols]
            rstd = lax.rsqrt(jnp.mean(xv * xv, axis=-1, keepdims=True) + EPS)
            nrm = xv * rstd
            dgain_ref[...] += jnp.sum(dt * nrm, axis=0, keepdims=True)
            dn = dt * gain
            dp_ref[:, cols] = (rstd * (dn - nrm * jnp.mean(dn * nrm, axis=-1, keepdims=True))).astype(dp_ref.dtype)

    vec = _full((1, HEAD_DIM))
    tab = pl.BlockSpec((tr, HEAD_DIM), lambda i: (i, 0))
    operands = ([dq] if has_q else []) + [dk, proj, q_g, k_g, cos, sin]
    specs = ([pl.BlockSpec((tr, ATTN_W), lambda i: (i, 0))] if has_q else []) + [
        pl.BlockSpec((tr, KV_W), lambda i: (i, 0)), pl.BlockSpec((tr, width), lambda i: (i, 0)), vec, vec, tab, tab]
    return pl.pallas_call(
        body, name=name, grid=(rows // tr,), in_specs=specs,
        out_specs=(pl.BlockSpec((tr, width), lambda i: (i, 0)), vec, vec),
        out_shape=(jax.ShapeDtypeStruct((rows, width), BF16), jax.ShapeDtypeStruct((1, HEAD_DIM), F32),
                   jax.ShapeDtypeStruct((1, HEAD_DIM), F32)),
        compiler_params=_params(("arbitrary",)),
    )(*operands)


def _attn_fwd(q, k, v):
    n_q, n_k = q.shape[0], k.shape[0]
    tq = _tile(n_q, 256, 16)
    gw = GROUP * HEAD_DIM
    scale = HEAD_DIM ** -0.5

    def body(q_ref, k_ref, v_ref, o_ref, lse_ref):
        kv, vv = k_ref[...], v_ref[...]
        for g in range(GROUP):
            cols = slice(g * HEAD_DIM, (g + 1) * HEAD_DIM)
            s = lax.dot_general(q_ref[:, cols], kv, (((1,), (1,)), ((), ())), preferred_element_type=F32) * (scale * LOG2E)
            m = jnp.max(s, axis=-1, keepdims=True)
            p = jnp.exp2(s - m)
            l = jnp.sum(p, axis=-1, keepdims=True)
            o = jnp.dot(p.astype(BF16), vv, preferred_element_type=F32) / l
            o_ref[:, cols] = o.astype(o_ref.dtype)
            lse_ref[:, g:g + 1] = m + jnp.log(l) * LOG2E

    return pl.pallas_call(
        body, name="attn_fwd", grid=(N_KV, n_q // tq),
        in_specs=[pl.BlockSpec((tq, gw), lambda h, i: (i, h)), pl.BlockSpec((n_k, HEAD_DIM), lambda h, i: (0, h)),
                  pl.BlockSpec((n_k, HEAD_DIM), lambda h, i: (0, h))],
        out_specs=(pl.BlockSpec((tq, gw), lambda h, i: (i, h)), pl.BlockSpec((None, tq, GROUP), lambda h, i: (h, i, 0))),
        out_shape=(jax.ShapeDtypeStruct((n_q, ATTN_W), BF16), jax.ShapeDtypeStruct((N_KV, n_q, GROUP), F32)),
        compiler_params=_params(("parallel", "parallel")),
    )(q, k, v)


def _attn_bwd(q, k, v, o, lse, do):
    n_q, n_k = q.shape[0], k.shape[0]
    tq = _tile(n_q, 256, 16)
    gw = GROUP * HEAD_DIM
    scale = HEAD_DIM ** -0.5

    def body(q_ref, k_ref, v_ref, o_ref, lse_ref, do_ref, dq_ref, dk_ref, dv_ref):
        @pl.when(pl.program_id(1) == 0)
        def _():
            dk_ref[...] = jnp.zeros_like(dk_ref)
            dv_ref[...] = jnp.zeros_like(dv_ref)

        kv, vv = k_ref[...], v_ref[...]
        for g in range(GROUP):
            cols = slice(g * HEAD_DIM, (g + 1) * HEAD_DIM)
            qg = q_ref[:, cols]
            dof = do_ref[:, cols].astype(F32)
            dog = dof.astype(BF16)
            s = lax.dot_general(qg, kv, (((1,), (1,)), ((), ())), preferred_element_type=F32) * (scale * LOG2E)
            p = jnp.exp2(s - lse_ref[:, g:g + 1])
            delta = jnp.sum(dof * o_ref[:, cols].astype(F32), axis=-1, keepdims=True)
            dp = lax.dot_general(dog, vv, (((1,), (1,)), ((), ())), preferred_element_type=F32)
            ds = (p * (dp - delta) * scale).astype(BF16)
            pb = p.astype(BF16)
            dq_ref[:, cols] = jnp.dot(ds, kv, preferred_element_type=F32)
            dk_ref[...] += lax.dot_general(ds, qg, (((0,), (0,)), ((), ())), preferred_element_type=F32)
            dv_ref[...] += lax.dot_general(pb, dog, (((0,), (0,)), ((), ())), preferred_element_type=F32)

    qspec = pl.BlockSpec((tq, gw), lambda h, i: (i, h))
    kspec = pl.BlockSpec((n_k, HEAD_DIM), lambda h, i: (0, h))
    return pl.pallas_call(
        body, name="attn_bwd", grid=(N_KV, n_q // tq),
        in_specs=[qspec, kspec, kspec, qspec, pl.BlockSpec((None, tq, GROUP), lambda h, i: (h, i, 0)), qspec],
        out_specs=(qspec, kspec, kspec),
        out_shape=(jax.ShapeDtypeStruct((n_q, ATTN_W), F32), jax.ShapeDtypeStruct((n_k, KV_W), F32),
                   jax.ShapeDtypeStruct((n_k, KV_W), F32)),
        compiler_params=_params(("parallel", "arbitrary")),
    )(q, k, v, o, lse, do)


CONV_COLS = 256
XR_COL0 = ATTN_W + 2 * KV_W


def _shift_rows(v, off):
    if off == 0:
        return v
    n = v.shape[0]
    rolled = pltpu.roll(v, (-off) % n, 0)
    t = lax.broadcasted_iota(jnp.int32, v.shape, 0)
    keep = (t + off >= 0) & (t + off < n)
    return jnp.where(keep, rolled, 0.0)


def _conv_fwd(proj, w, b, *, name):
    rows = proj.shape[0]
    blk0 = XR_COL0 // CONV_COLS

    def body(x_ref, w_ref, b_ref, y_ref):
        xv = x_ref[...]
        y = b_ref[...] + jnp.zeros_like(xv)
        for j in range(CONV_W):
            y = y + _shift_rows(xv, j - CONV_W // 2) * w_ref[j:j + 1, :]
        y_ref[...] = y

    return pl.pallas_call(
        body, name=name, grid=(D_RNN // CONV_COLS,),
        in_specs=[pl.BlockSpec((rows, CONV_COLS), lambda i: (0, blk0 + i)),
                  pl.BlockSpec((CONV_W, CONV_COLS), lambda i: (0, i)), pl.BlockSpec((1, CONV_COLS), lambda i: (0, i))],
        out_specs=pl.BlockSpec((rows, CONV_COLS), lambda i: (0, i)),
        out_shape=jax.ShapeDtypeStruct((rows, D_RNN), F32), compiler_params=_params(("parallel",)),
    )(proj, w, b)


def _conv_bwd(d1, d2, proj, w, *, name):
    rows = proj.shape[0]
    blk0 = XR_COL0 // CONV_COLS

    def body(d1_ref, d2_ref, x_ref, w_ref, dx_ref, dw_ref, db_ref):
        dv = d1_ref[...] + d2_ref[...]
        xv = x_ref[...]
        dx = jnp.zeros_like(dv)
        for j in range(CONV_W):
            off = j - CONV_W // 2
            dx = dx + _shift_rows(dv, -off) * w_ref[j:j + 1, :]
            dw_ref[j:j + 1, :] = jnp.sum(dv * _shift_rows(xv, off), axis=0, keepdims=True)
        dx_ref[...] = dx.astype(dx_ref.dtype)
        db_ref[...] = jnp.sum(dv, axis=0, keepdims=True)

    col = pl.BlockSpec((rows, CONV_COLS), lambda i: (0, i))
    return pl.pallas_call(
        body, name=name, grid=(D_RNN // CONV_COLS,),
        in_specs=[col, col, pl.BlockSpec((rows, CONV_COLS), lambda i: (0, blk0 + i)),
                  pl.BlockSpec((CONV_W, CONV_COLS), lambda i: (0, i))],
        out_specs=(col, pl.BlockSpec((CONV_W, CONV_COLS), lambda i: (0, i)), pl.BlockSpec((1, CONV_COLS), lambda i: (0, i))),
        out_shape=(jax.ShapeDtypeStruct((rows, D_RNN), BF16), jax.ShapeDtypeStruct((CONV_W, D_RNN), F32),
                   jax.ShapeDtypeStruct((1, D_RNN), F32)),
        compiler_params=_params(("parallel",)),
    )(d1, d2, proj, w)


RNN_TB = 256
SCAN_ROWS = 8


def _sigmoid(z):
    return 1.0 / (1.0 + jnp.exp(-z))


def _softplus(z):
    return jnp.maximum(z, 0.0) + jnp.log(1.0 + jnp.exp(-jnp.abs(z)))


def _one_minus_exp(y):
    series = -y * (1.0 + y * (0.5 + y * (1.0 / 6.0 + y * (1.0 / 24.0))))
    return jnp.where(y > -0.03, series, 1.0 - jnp.exp(y))


def _rglru_gates(xv, wa_ref, ba_ref, wx_ref, bx_ref, lam_ref):
    xb = xv.astype(BF16)
    zr = jnp.concatenate([jnp.dot(xb[:, n * RNN_BW:(n + 1) * RNN_BW], wa_ref[n].astype(BF16),
                                  preferred_element_type=F32) for n in range(RNN_BLOCKS)], axis=-1) + ba_ref[...]
    zi = jnp.concatenate([jnp.dot(xb[:, n * RNN_BW:(n + 1) * RNN_BW], wx_ref[n].astype(BF16),
                                  preferred_element_type=F32) for n in range(RNN_BLOCKS)], axis=-1) + bx_ref[...]
    r = _sigmoid(zr)
    gi = _sigmoid(zi)
    sp = _softplus(-lam_ref[...])
    log_a = -RG_C * r * sp
    a = jnp.exp(log_a)
    s = jnp.sqrt(_one_minus_exp(2.0 * log_a))
    return r, gi, sp, a, s


def _scan_rows(n_rows, reverse, step_fn, carry):
    groups = n_rows // SCAN_ROWS

    def trip(gidx, carry):
        gi = (groups - 1 - gidx) if reverse else gidx
        base = pl.multiple_of(gi * SCAN_ROWS, SCAN_ROWS)
        return step_fn(base, carry)

    return lax.fori_loop(0, groups, trip, carry)


def _rglru_fwd(xs, wa, ba, wx, bx, lam, *, reverse, name):
    rows = xs.shape[0]
    tb = _tile(rows, RNN_TB, SCAN_ROWS)
    nb = rows // tb
    order = (lambda i: (nb - 1 - i, 0)) if reverse else (lambda i: (i, 0))

    def body(x_ref, wa_ref, ba_ref, wx_ref, bx_ref, lam_ref, h_ref, hp_ref, a_s, b_s, state):
        @pl.when(pl.program_id(0) == 0)
        def _():
            state[...] = jnp.zeros_like(state)

        xv = x_ref[...]
        _, gi, _, a, s = _rglru_gates(xv, wa_ref, ba_ref, wx_ref, bx_ref, lam_ref)
        a_s[...] = a
        b_s[...] = s * (gi * xv)

        def group(base, h):
            av = a_s[pl.ds(base, SCAN_ROWS), :]
            bv = b_s[pl.ds(base, SCAN_ROWS), :]
            outs, prevs = [None] * SCAN_ROWS, [None] * SCAN_ROWS
            for k in range(SCAN_ROWS):
                r_ = SCAN_ROWS - 1 - k if reverse else k
                prevs[r_] = h
                h = av[r_:r_ + 1, :] * h + bv[r_:r_ + 1, :]
                outs[r_] = h
            h_ref[pl.ds(base, SCAN_ROWS), :] = jnp.concatenate(outs, axis=0)
            hp_ref[pl.ds(base, SCAN_ROWS), :] = jnp.concatenate(prevs, axis=0)
            return h

        state[0:1, :] = _scan_rows(tb, reverse, group, state[0:1, :])

    blk = pl.BlockSpec((tb, D_RNN), order)
    wspec = _full((RNN_BLOCKS, RNN_BW, RNN_BW))
    vec = _full((1, D_RNN))
    return pl.pallas_call(
        body, name=name, grid=(nb,), in_specs=[blk, wspec, vec, wspec, vec, vec], out_specs=(blk, blk),
        out_shape=(jax.ShapeDtypeStruct((rows, D_RNN), F32), jax.ShapeDtypeStruct((rows, D_RNN), F32)),
        scratch_shapes=[pltpu.VMEM((tb, D_RNN), F32), pltpu.VMEM((tb, D_RNN), F32), pltpu.VMEM((SCAN_ROWS, D_RNN), F32)],
        compiler_params=_params(("arbitrary",)),
    )(xs, wa, ba, wx, bx, lam)


def _rglru_bwd(xs, h_prev, dh, wa, ba, wx, bx, lam, *, reverse, name):
    rows = xs.shape[0]
    tb = _tile(rows, RNN_TB, SCAN_ROWS)
    nb = rows // tb
    back = not reverse
    order = (lambda i: (nb - 1 - i, 0)) if back else (lambda i: (i, 0))

    def body(x_ref, hp_ref, dh_ref, wa_ref, ba_ref, wx_ref, bx_ref, lam_ref,
             dx_ref, dwa_ref, dba_ref, dwx_ref, dbx_ref, dlam_ref, a_s, g_s, state):
        @pl.when(pl.program_id(0) == 0)
        def _():
            state[...] = jnp.zeros_like(state)
            dwa_ref[...] = jnp.zeros_like(dwa_ref)
            dwx_ref[...] = jnp.zeros_like(dwx_ref)
            dba_ref[...] = jnp.zeros_like(dba_ref)
            dbx_ref[...] = jnp.zeros_like(dbx_ref)
            dlam_ref[...] = jnp.zeros_like(dlam_ref)

        xv = x_ref[...]
        r, gi, sp, a, s = _rglru_gates(xv, wa_ref, ba_ref, wx_ref, bx_ref, lam_ref)
        a_s[...] = a

        def group(base, carry):
            av = a_s[pl.ds(base, SCAN_ROWS), :]
            dv = dh_ref[pl.ds(base, SCAN_ROWS), :]
            outs = [None] * SCAN_ROWS
            for k in range(SCAN_ROWS):
                r_ = SCAN_ROWS - 1 - k if back else k
                gt = dv[r_:r_ + 1, :] + carry
                outs[r_] = gt
                carry = av[r_:r_ + 1, :] * gt
            g_s[pl.ds(base, SCAN_ROWS), :] = jnp.concatenate(outs, axis=0)
            return carry

        state[0:1, :] = _scan_rows(tb, back, group, state[0:1, :])

        gv = g_s[...]
        d_a = gv * hp_ref[...]
        d_s = gv * (gi * xv)
        d_gi = gv * (s * xv)
        dx = gv * (s * gi)
        d_log_a = d_a * a - d_s * (a * a) / s
        d_r = d_log_a * (-RG_C * sp)
        lamv = lam_ref[...]
        d_sp = jnp.sum(d_log_a * (-RG_C * r), axis=0, keepdims=True)
        dlam_ref[...] += d_sp * (-_sigmoid(-lamv))
        d_zr = d_r * r * (1.0 - r)
        d_zi = d_gi * gi * (1.0 - gi)
        dba_ref[...] += jnp.sum(d_zr, axis=0, keepdims=True)
        dbx_ref[...] += jnp.sum(d_zi, axis=0, keepdims=True)
        xb = xv.astype(BF16)
        zrb, zib = d_zr.astype(BF16), d_zi.astype(BF16)
        parts = []
        for n in range(RNN_BLOCKS):
            cols = slice(n * RNN_BW, (n + 1) * RNN_BW)
            dwa_ref[n] += lax.dot_general(xb[:, cols], zrb[:, cols], (((0,), (0,)), ((), ())), preferred_element_type=F32)
            dwx_ref[n] += lax.dot_general(xb[:, cols], zib[:, cols], (((0,), (0,)), ((), ())), preferred_element_type=F32)
            parts.append(
                lax.dot_general(zrb[:, cols], wa_ref[n].astype(BF16), (((1,), (1,)), ((), ())), preferred_element_type=F32)
                + lax.dot_general(zib[:, cols], wx_ref[n].astype(BF16), (((1,), (1,)), ((), ())), preferred_element_type=F32))
        dx_ref[...] = dx + jnp.concatenate(parts, axis=-1)

    blk = pl.BlockSpec((tb, D_RNN), order)
    wspec = _full((RNN_BLOCKS, RNN_BW, RNN_BW))
    vec = _full((1, D_RNN))
    wshape = jax.ShapeDtypeStruct((RNN_BLOCKS, RNN_BW, RNN_BW), F32)
    vshape = jax.ShapeDtypeStruct((1, D_RNN), F32)
    return pl.pallas_call(
        body, name=name, grid=(nb,), in_specs=[blk, blk, blk, wspec, vec, wspec, vec, vec],
        out_specs=(blk, wspec, vec, wspec, vec, vec),
        out_shape=(jax.ShapeDtypeStruct((rows, D_RNN), F32), wshape, vshape, wshape, vshape, vshape),
        scratch_shapes=[pltpu.VMEM((tb, D_RNN), F32), pltpu.VMEM((tb, D_RNN), F32), pltpu.VMEM((SCAN_ROWS, D_RNN), F32)],
        compiler_params=_params(("arbitrary",)),
    )(xs, h_prev, dh, wa, ba, wx, bx, lam)


def _assemble_d_proj(dp_qk_l, dp_qk_c, dv_all, d_xr_l, d_xr_c, d_gate):
    n_lat, n_ctx = dp_qk_l.shape[0], dp_qk_c.shape[0]
    tr = _tile(math.gcd(n_lat, n_ctx), 256, 16)
    nb_l, nb_c = n_lat // tr, n_ctx // tr
    w_qk = ATTN_W + KV_W

    def body(ql_ref, qc_ref, dv_ref, xl_ref, xc_ref, g_ref, o_ref):
        i = pl.program_id(0)
        o_ref[:, w_qk:XR_COL0] = dv_ref[...].astype(o_ref.dtype)

        @pl.when(i < nb_l)
        def _():
            o_ref[:, :w_qk] = ql_ref[...]
            o_ref[:, XR_COL0:GATE_COL0] = xl_ref[...]
            o_ref[:, GATE_COL0:] = g_ref[...]

        @pl.when(i >= nb_l)
        def _():
            o_ref[:, :w_qk] = qc_ref[...]
            o_ref[:, XR_COL0:GATE_COL0] = xc_ref[...]
            o_ref[:, GATE_COL0:] = jnp.zeros((tr, D_RNN), o_ref.dtype)

    lat = lambda i: (jnp.minimum(i, nb_l - 1), 0)
    ctx = lambda i: (jnp.maximum(i - nb_l, 0), 0)
    return pl.pallas_call(
        body, name="assemble_d_proj", grid=(nb_l + nb_c,),
        in_specs=[pl.BlockSpec((tr, w_qk), lat), pl.BlockSpec((tr, w_qk), ctx),
                  pl.BlockSpec((tr, KV_W), lambda i: (jnp.where(i < nb_l, i + nb_c, i - nb_l), 0)),
                  pl.BlockSpec((tr, D_RNN), lat), pl.BlockSpec((tr, D_RNN), ctx), pl.BlockSpec((tr, D_RNN), lat)],
        out_specs=pl.BlockSpec((tr, GATE_COL0 + D_RNN), lambda i: (i, 0)),
        out_shape=jax.ShapeDtypeStruct((n_lat + n_ctx, GATE_COL0 + D_RNN), BF16),
        compiler_params=_params(("parallel",)),
    )(dp_qk_l, dp_qk_c, dv_all, d_xr_l, d_xr_c, d_gate)


def _gelu(z):
    return 0.5 * z * (1.0 + jnp.tanh(GELU_C * (z + 0.044715 * z * z * z)))


def _gelu_grad(z):
    t = jnp.tanh(GELU_C * (z + 0.044715 * z * z * z))
    return 0.5 * (1.0 + t) + 0.5 * z * (1.0 - t * t) * (GELU_C * (1.0 + 3.0 * 0.044715 * z * z))


GATE_COL0 = XR_COL0 + D_RNN


RNN_OUT_COLS = 512


def _rnn_out_specs(rows, hf_off, hb_off):
    tr = _tile(rows, 256, 16)
    assert hf_off % tr == 0 and hb_off % tr == 0 and GATE_COL0 % RNN_OUT_COLS == 0
    fo, bo, go = hf_off // tr, hb_off // tr, GATE_COL0 // RNN_OUT_COLS
    hf_spec = pl.BlockSpec((tr, RNN_OUT_COLS), lambda i, j: (i + fo, j))
    hb_spec = pl.BlockSpec((tr, RNN_OUT_COLS), lambda i, j: (i + bo, j))
    gate_spec = pl.BlockSpec((tr, RNN_OUT_COLS), lambda i, j: (i, j + go))
    out_spec = pl.BlockSpec((tr, RNN_OUT_COLS), lambda i, j: (i, j))
    return (rows // tr, D_RNN // RNN_OUT_COLS), hf_spec, hb_spec, gate_spec, out_spec


def _rnn_out_fwd(hf, hb, proj, hf_off, hb_off):
    rows = proj.shape[0]
    grid, hf_spec, hb_spec, gate_spec, out_spec = _rnn_out_specs(rows, hf_off, hb_off)

    def body(hf_ref, hb_ref, g_ref, o_ref):
        o_ref[...] = ((hf_ref[...] + hb_ref[...]) * _gelu(g_ref[...])).astype(o_ref.dtype)

    return pl.pallas_call(
        body, name="rnn_out_fwd", grid=grid, in_specs=[hf_spec, hb_spec, gate_spec], out_specs=out_spec,
        out_shape=jax.ShapeDtypeStruct((rows, D_RNN), BF16), compiler_params=_params(("parallel", "parallel")),
    )(hf, hb, proj)


def _rnn_out_bwd(d_cat, hf, hb, proj, hf_off, hb_off):
    rows = proj.shape[0]
    grid, hf_spec, hb_spec, gate_spec, out_spec = _rnn_out_specs(rows, hf_off, hb_off)
    do = ATTN_W // RNN_OUT_COLS

    def body(d_ref, hf_ref, hb_ref, g_ref, dh_ref, dg_ref):
        dv, gv = d_ref[...].astype(F32), g_ref[...]
        dh_ref[...] = dv * _gelu(gv)
        dg_ref[...] = (dv * (hf_ref[...] + hb_ref[...]) * _gelu_grad(gv)).astype(dg_ref.dtype)

    tr = out_spec.block_shape[0]
    return pl.pallas_call(
        body, name="rnn_out_bwd", grid=grid,
        in_specs=[pl.BlockSpec((tr, RNN_OUT_COLS), lambda i, j: (i, j + do)), hf_spec, hb_spec, gate_spec],
        out_specs=(out_spec, out_spec),
        out_shape=(jax.ShapeDtypeStruct((rows, D_RNN), F32), jax.ShapeDtypeStruct((rows, D_RNN), BF16)),
        compiler_params=_params(("parallel", "parallel")),
    )(d_cat, hf, hb, proj)


def _gmlp_parts(z_ref, vg_ref, vb_ref, d_gm):
    zu, zv = z_ref[:, :d_gm], z_ref[:, d_gm:]
    u = _gelu(zu)
    v = _gelu(zv)
    mu = jnp.mean(v, axis=-1, keepdims=True)
    vc = v - mu
    rstd = lax.rsqrt(jnp.mean(vc * vc, axis=-1, keepdims=True) + EPS)
    vhat = vc * rstd
    vn = vhat * vg_ref[...] + vb_ref[...]
    return zu, zv, u, vhat, rstd, vn


def _gmlp_fwd(z, v_g, v_b, w_sp, b_sp_t):
    rows, d_gm = z.shape[0], z.shape[1] // 2
    tr = _tile(rows, 256, CHUNK)
    gwid = d_gm // GM_GROUPS

    def body(z_ref, vg_ref, vb_ref, w_ref, b_ref, o_ref):
        _, _, u, _, _, vn = _gmlp_parts(z_ref, vg_ref, vb_ref, d_gm)
        vnb = vn.astype(BF16)
        for g in range(GM_GROUPS):
            wg = w_ref[g].astype(BF16)
            for c in range(tr // CHUNK):
                rs, cs = slice(c * CHUNK, (c + 1) * CHUNK), slice(g * gwid, (g + 1) * gwid)
                sv = jnp.dot(wg, vnb[rs, cs], preferred_element_type=F32) + b_ref[:, g:g + 1]
                o_ref[rs, cs] = (u[rs, cs] * sv).astype(o_ref.dtype)

    return pl.pallas_call(
        body, name="gmlp_fwd", grid=(rows // tr,),
        in_specs=[pl.BlockSpec((tr, 2 * d_gm), lambda i: (i, 0)), _full((1, d_gm)), _full((1, d_gm)),
                  _full(w_sp.shape), _full(b_sp_t.shape)],
        out_specs=pl.BlockSpec((tr, d_gm), lambda i: (i, 0)),
        out_shape=jax.ShapeDtypeStruct((rows, d_gm), BF16), compiler_params=_params(("parallel",)),
    )(z, v_g, v_b, w_sp, b_sp_t)


def _gmlp_bwd(z, dgate, v_g, v_b, w_sp, b_sp_t):
    rows, d_gm = z.shape[0], z.shape[1] // 2
    tr = _tile(rows, 256, CHUNK)
    gwid = d_gm // GM_GROUPS

    def body(z_ref, dg_ref, vg_ref, vb_ref, w_ref, b_ref, dz_ref, dbin_ref, dvg_ref, dvb_ref, dw_ref, dbs_ref, dvn_s):
        @pl.when(pl.program_id(0) == 0)
        def _():
            dbin_ref[...] = jnp.zeros_like(dbin_ref)
            dvg_ref[...] = jnp.zeros_like(dvg_ref)
            dvb_ref[...] = jnp.zeros_like(dvb_ref)
            dw_ref[...] = jnp.zeros_like(dw_ref)
            dbs_ref[...] = jnp.zeros_like(dbs_ref)

        zu, zv, u, vhat, rstd, vn = _gmlp_parts(z_ref, vg_ref, vb_ref, d_gm)
        vnb = vn.astype(BF16)
        dgv = dg_ref[...].astype(F32)
        dsv = dgv * u
        dsvb = dsv.astype(BF16)
        for g in range(GM_GROUPS):
            wg = w_ref[g].astype(BF16)
            cs = slice(g * gwid, (g + 1) * gwid)
            for c in range(tr // CHUNK):
                rs = slice(c * CHUNK, (c + 1) * CHUNK)
                sv = jnp.dot(wg, vnb[rs, cs], preferred_element_type=F32) + b_ref[:, g:g + 1]
                dz_ref[rs, cs] = (dgv[rs, cs] * sv * _gelu_grad(zu[rs, cs])).astype(dz_ref.dtype)
                dw_ref[g] += lax.dot_general(dsvb[rs, cs], vnb[rs, cs], (((1,), (1,)), ((), ())),
                                             preferred_element_type=F32)
                dbs_ref[:, g:g + 1] += jnp.sum(dsv[rs, cs], axis=-1, keepdims=True)
                dvn_s[rs, cs] = lax.dot_general(wg, dsvb[rs, cs], (((0,), (0,)), ((), ())), preferred_element_type=F32)
        dvn = dvn_s[...]
        dvg_ref[...] += jnp.sum(dvn * vhat, axis=0, keepdims=True)
        dvb_ref[...] += jnp.sum(dvn, axis=0, keepdims=True)
        dvh = dvn * vg_ref[...]
        dv = rstd * (dvh - jnp.mean(dvh, axis=-1, keepdims=True) - vhat * jnp.mean(dvh * vhat, axis=-1, keepdims=True))
        dzv = dv * _gelu_grad(zv)
        dz_ref[:, d_gm:] = dzv.astype(dz_ref.dtype)
        dbin_ref[:, d_gm:] += jnp.sum(dzv, axis=0, keepdims=True)
        dbin_ref[:, :d_gm] += jnp.sum(dz_ref[:, :d_gm].astype(F32), axis=0, keepdims=True)

    return pl.pallas_call(
        body, name="gmlp_bwd", grid=(rows // tr,),
        in_specs=[pl.BlockSpec((tr, 2 * d_gm), lambda i: (i, 0)), pl.BlockSpec((tr, d_gm), lambda i: (i, 0)),
                  _full((1, d_gm)), _full((1, d_gm)), _full(w_sp.shape), _full(b_sp_t.shape)],
        out_specs=(pl.BlockSpec((tr, 2 * d_gm), lambda i: (i, 0)), _full((1, 2 * d_gm)), _full((1, d_gm)),
                   _full((1, d_gm)), _full(w_sp.shape), _full(b_sp_t.shape)),
        out_shape=(jax.ShapeDtypeStruct((rows, 2 * d_gm), BF16), jax.ShapeDtypeStruct((1, 2 * d_gm), F32),
                   jax.ShapeDtypeStruct((1, d_gm), F32), jax.ShapeDtypeStruct((1, d_gm), F32),
                   jax.ShapeDtypeStruct(w_sp.shape, F32), jax.ShapeDtypeStruct(b_sp_t.shape, F32)),
        scratch_shapes=[pltpu.VMEM((tr, d_gm), F32)],
        compiler_params=_params(("arbitrary",)),
    )(z, dgate, v_g, v_b, w_sp, b_sp_t)


def _adamw_math(w, g, m, v):
    m = ADAM_B1 * m + (1.0 - ADAM_B1) * g
    v = ADAM_B2 * v + (1.0 - ADAM_B2) * (g * g)
    m_hat = m / (1.0 - ADAM_B1 ** ADAM_STEP)
    v_hat = v / (1.0 - ADAM_B2 ** ADAM_STEP)
    delta = -ADAM_LR * (m_hat / (jnp.sqrt(v_hat) + ADAM_EPS) + ADAM_WD * w)
    return delta, m, v


def _adamw(w, g, m, v, name):
    shape = w.shape
    outs = _rowwise(_adamw_math, (F32, F32, F32), _as2d(w), _as2d(g), _as2d(m), _as2d(v), name=name)
    return (g.reshape(shape),) + tuple(o.reshape(shape) for o in outs)


PACK_COLS = 1024


def _pack(arrays, dtype=F32):
    flat = jnp.concatenate([a.reshape(-1).astype(dtype) for a in arrays])
    pad = (-flat.size) % (16 * PACK_COLS)
    return jnp.pad(flat, (0, pad)).reshape(-1, PACK_COLS)


def _into_slot(pack, dev, name):
    rows, cols = pack.shape
    tr = _rows_tile(rows, cols, budget=512 * 1024)

    def body(dev_ref, p_ref, o_ref):
        o_ref[...] = p_ref[...]

    return pl.pallas_call(
        body, name=name, out_shape=jax.ShapeDtypeStruct((N_DEV, rows, cols), pack.dtype),
        grid_spec=pltpu.PrefetchScalarGridSpec(
            num_scalar_prefetch=1, grid=(rows // tr,), in_specs=[pl.BlockSpec((tr, cols), lambda i, dv: (i, 0))],
            out_specs=pl.BlockSpec((None, tr, cols), lambda i, dv: (dv[0], i, 0))),
        compiler_params=_params(("parallel",)),
    )(dev, pack)


def _unpack(flat, shapes):
    out, pos = [], 0
    for shp in shapes:
        n = math.prod(shp)
        out.append(flat[pos:pos + n].reshape(shp))
        pos += n
    return out


def _unpack_devices(packed8, shapes):
    flat8 = packed8.reshape(N_DEV, -1)
    out, pos = [], 0
    for shp in shapes:
        n = math.prod(shp)
        out.append(flat8[:, pos:pos + n].reshape((N_DEV,) + tuple(shp)))
        pos += n
    return out


def _sum_devices(g8):
    _, rows, cols = g8.shape
    tr = _rows_tile(rows, cols, budget=256 * 1024)

    def body(g_ref, o_ref):
        acc = g_ref[0].astype(F32)
        for d in range(1, N_DEV):
            acc = acc + g_ref[d].astype(F32)
        o_ref[...] = acc

    return pl.pallas_call(
        body, name="sum_devices", grid=(rows // tr,), in_specs=[pl.BlockSpec((N_DEV, tr, cols), lambda i: (0, i, 0))],
        out_specs=pl.BlockSpec((tr, cols), lambda i: (i, 0)), out_shape=jax.ShapeDtypeStruct((rows, cols), F32),
        compiler_params=_params(("parallel",)),
    )(g8)


def _place():
    return lax.axis_index("x"), lax.axis_index("y"), lax.axis_index("c")


def _other_chips(x, y):
    return [(1 - x, y), (x, 1 - y), (1 - x, 1 - y)]


def _remote(src, dst, send_sem, recv_sem, to):
    return pltpu.make_async_remote_copy(src_ref=src, dst_ref=dst, send_sem=send_sem, recv_sem=recv_sem, device_id=to,
                                        device_id_type=MESH)


def _comm_call(body, name, operands, out_shapes, n_remote, n_local, aliases=None):
    return pl.pallas_call(
        body, name=name, out_shape=tuple(out_shapes), in_specs=[ANY] * len(operands), out_specs=tuple(ANY for _ in out_shapes),
        scratch_shapes=[pltpu.SemaphoreType.DMA((n_remote,)), pltpu.SemaphoreType.DMA((n_remote,)),
                        pltpu.SemaphoreType.DMA((max(n_local, 1),))],
        input_output_aliases=aliases or {},
    )(*operands)


def _in_place(arrays):
    return [jax.ShapeDtypeStruct(a.shape, a.dtype) for a in arrays], {i: i for i in range(len(arrays))}


def _allgather8(arrs, name):
    n = len(arrs)

    def body(*refs):
        ins, outs = refs[:n], refs[n:2 * n]
        send, recv, lsem = refs[2 * n:]
        x, y, c = _place()
        me, sib = (x, y, c), (x, y, 1 - c)
        chips = _other_chips(x, y)

        def slot(t, px, py, pc):
            return outs[t].at[4 * px + 2 * py + pc]

        def cp(t, k, block, to, from_input=False):
            src = ins[t] if from_input else slot(t, *block)
            return _remote(src, slot(t, *block), send.at[7 * t + k], recv.at[7 * t + k], to)

        mine = [pltpu.make_async_copy(ins[t], slot(t, *me), lsem.at[t]) for t in range(n)]
        for cpy in mine:
            cpy.start()
        first = []
        for t in range(n):
            first.append(cp(t, 0, me, sib, True))
            first += [cp(t, 1 + j, me, (*chip, c), True) for j, chip in enumerate(chips)]
        for cpy in first:
            cpy.start()
        passed = []
        for t in range(n):
            for j, chip in enumerate(chips):
                cp(t, 1 + j, (*chip, c), me).wait_recv()
                fwd = cp(t, 4 + j, (*chip, c), sib)
                fwd.start()
                passed.append(fwd)
        for t in range(n):
            cp(t, 0, sib, me).wait_recv()
            for j, chip in enumerate(chips):
                cp(t, 4 + j, (*chip, 1 - c), me).wait_recv()
        for cpy in first + passed:
            cpy.wait_send()
        for cpy in mine:
            cpy.wait()

    outs = _comm_call(body, name, arrs, [jax.ShapeDtypeStruct((N_DEV,) + a.shape, a.dtype) for a in arrs], 7 * n, n)
    return list(outs)


def _gather_weights(bufs):
    n_u = len(bufs)

    def body(*refs):
        bufs_ = refs[n_u:2 * n_u]
        send, recv, _ = refs[2 * n_u:]
        x, y, c = _place()
        me, sib, q = (x, y, c), (x, y, 1 - c), 2 * x + y
        chips = _other_chips(x, y)
        sent = []
        for u in range(n_u):
            half = bufs_[u].shape[1] // 2
            mine = bufs_[u].at[q, pl.ds(c * half, half)]
            for j, chip in enumerate(chips):
                cpy = _remote(mine, mine, send.at[6 * u + j], recv.at[6 * u + j], (*chip, c))
                cpy.start()
                sent.append(cpy)
        for u in range(n_u):
            half = bufs_[u].shape[1] // 2
            for j, chip in enumerate(chips):
                landed = bufs_[u].at[2 * chip[0] + chip[1], pl.ds(c * half, half)]
                _remote(landed, landed, send.at[6 * u + j], recv.at[6 * u + j], me).wait_recv()
                cpy = _remote(landed, landed, send.at[6 * u + 3 + j], recv.at[6 * u + 3 + j], sib)
                cpy.start()
                sent.append(cpy)
        for u in range(n_u):
            half = bufs_[u].shape[1] // 2
            for j, chip in enumerate(chips):
                landed = bufs_[u].at[2 * chip[0] + chip[1], pl.ds((1 - c) * half, half)]
                _remote(landed, landed, send.at[6 * u + 3 + j], recv.at[6 * u + 3 + j], me).wait_recv()
        for cpy in sent:
            cpy.wait_send()

    shapes, aliases = _in_place(bufs)
    return list(_comm_call(body, "gather_weights", bufs, shapes, 6 * n_u, 0, aliases))


def _exchange_halves(grads):
    n = len(grads)

    def body(*refs):
        ins, outs = refs[:n], refs[n:2 * n]
        send, recv, _ = refs[2 * n:]
        x, y, c = _place()
        sib = (x, y, 1 - c)
        sent = []
        for k in range(n):
            half = ins[k].shape[1] // 2
            cpy = _remote(ins[k].at[pl.ds(0, N_CHIPS), pl.ds((1 - c) * half, half)], outs[k], send.at[k], recv.at[k], sib)
            cpy.start()
            sent.append(cpy)
        for cpy in sent:
            cpy.wait()

    shapes = [jax.ShapeDtypeStruct((N_CHIPS, g.shape[1] // 2, g.shape[2]), g.dtype) for g in grads]
    return list(_comm_call(body, "exchange_halves", grads, shapes, n, 0))


def _chips_all_to_all(sums):
    n = len(sums)

    def body(*refs):
        ins, outs = refs[:n], refs[n:2 * n]
        send, recv, _ = refs[2 * n:]
        x, y, c = _place()
        sent = []
        for k in range(n):
            for j, chip in enumerate(_other_chips(x, y)):
                cpy = _remote(ins[k].at[2 * chip[0] + chip[1]], outs[k].at[j], send.at[3 * k + j], recv.at[3 * k + j], (*chip, c))
                cpy.start()
                sent.append(cpy)
        for cpy in sent:
            cpy.wait()

    shapes = [jax.ShapeDtypeStruct((N_CHIPS - 1,) + s.shape[1:], s.dtype) for s in sums]
    return list(_comm_call(body, "chips_all_to_all", sums, shapes, 3 * n, 0))


def _join_halves(bufs):
    n = len(bufs)
    units = [(k, layer) for k in range(n) for layer in range(bufs[k].shape[0])]

    def body(*refs):
        bufs_ = refs[n:2 * n]
        send, recv, _ = refs[2 * n:]
        x, y, c = _place()
        sent = []
        for u, (k, layer) in enumerate(units):
            half = bufs_[k].shape[1] // 2
            mine = bufs_[k].at[layer, pl.ds(c * half, half)]
            cpy = _remote(mine, mine, send.at[u], recv.at[u], (x, y, 1 - c))
            cpy.start()
            sent.append(cpy)
        for u, (k, layer) in enumerate(units):
            half = bufs_[k].shape[1] // 2
            theirs = bufs_[k].at[layer, pl.ds((1 - c) * half, half)]
            _remote(theirs, theirs, send.at[u], recv.at[u], (x, y, c)).wait_recv()
        for cpy in sent:
            cpy.wait_send()

    shapes, aliases = _in_place(bufs)
    return list(_comm_call(body, "join_halves", bufs, shapes, len(units), 0, aliases))


def _add_halves(grad, other, place):
    _, rows, cols = grad.shape
    half = rows // 2
    tr = _rows_tile(half, cols, itemsize=2, budget=1024 * 1024)
    per_half = half // tr

    def body(place_ref, g_ref, o_ref, s_ref):
        s_ref[...] = (g_ref[...].astype(F32) + o_ref[...].astype(F32)).astype(s_ref.dtype)

    return pl.pallas_call(
        body, name="add_halves", out_shape=jax.ShapeDtypeStruct((N_CHIPS, half, cols), grad.dtype),
        grid_spec=pltpu.PrefetchScalarGridSpec(
            num_scalar_prefetch=1, grid=(N_CHIPS, per_half),
            in_specs=[pl.BlockSpec((None, tr, cols), lambda k, i, pr: (k, pr[1] * per_half + i, 0)),
                      pl.BlockSpec((None, tr, cols), lambda k, i, pr: (k, i, 0))],
            out_specs=pl.BlockSpec((None, tr, cols), lambda k, i, pr: (k, i, 0))),
        compiler_params=_params(("parallel", "parallel")),
    )(place, grad, other)


def _add_chips(sums, others, place, dest, layer, n_layers):
    _, half, cols = sums.shape
    tr = _rows_tile(half, cols, itemsize=4, budget=1024 * 1024)
    per_half = half // tr

    def body(place_ref, s_ref, o_ref, *rest):
        acc = s_ref[...].astype(F32)
        for j in range(N_CHIPS - 1):
            acc = acc + o_ref[j].astype(F32)
        rest[-1][...] = acc

    operands = [place, sums, others] + ([] if dest is None else [dest])
    return pl.pallas_call(
        body, name="add_chips", out_shape=jax.ShapeDtypeStruct((n_layers, 2 * half, cols), F32),
        grid_spec=pltpu.PrefetchScalarGridSpec(
            num_scalar_prefetch=1, grid=(per_half,),
            in_specs=[pl.BlockSpec((None, tr, cols), lambda i, pr: (pr[0], i, 0)),
                      pl.BlockSpec((N_CHIPS - 1, tr, cols), lambda i, pr: (0, i, 0))] + ([] if dest is None else [ANY]),
            out_specs=pl.BlockSpec((None, tr, cols), lambda i, pr: (layer, pr[1] * per_half + i, 0))),
        input_output_aliases={} if dest is None else {3: 0},
        compiler_params=_params(("parallel",)),
    )(*operands)


HBM = pl.BlockSpec(memory_space=pltpu.HBM)
SEM = pl.BlockSpec(memory_space=pltpu.SEMAPHORE)
DATAFLOW = pltpu.SideEffectType.DATAFLOW_SIDE_EFFECTING


def _split_start(name, bufs, copies, n_copies, after=None):
    n = len(bufs)
    extra = 0 if after is None else 1

    def body(*refs):
        for cpy in copies(refs[:n], refs[n + extra], refs[n + extra + 1]):
            cpy.start()
        refs[-1][...] = jnp.zeros_like(refs[-1])

    outs = pl.pallas_call(
        body, name=name,
        out_shape=(pltpu.SemaphoreType.DMA((n_copies,)), pltpu.SemaphoreType.DMA((n_copies,)),
                   *[pltpu.HBM(b.shape, b.dtype) for b in bufs], jax.ShapeDtypeStruct((8, LANES), F32)),
        in_specs=[HBM] * n + [ANY] * extra,
        out_specs=(SEM, SEM, *[HBM] * n, pl.BlockSpec(memory_space=pltpu.VMEM)),
        input_output_aliases={i: 2 + i for i in range(n)},
        compiler_params=pltpu.CompilerParams(has_side_effects=DATAFLOW),
    )(*[pltpu.with_memory_space_constraint(b, pltpu.HBM) for b in bufs], *([] if after is None else [after]))
    return outs[0], outs[1], list(outs[2:2 + n]), outs[-1]


def _split_wait(name, bufs, send, recv, copies, after):
    n = len(bufs)

    def body(*refs):
        for cpy in copies(refs[:n], refs[n], refs[n + 1]):
            cpy.wait_send()
            cpy.wait_recv()

    return list(pl.pallas_call(
        body, name=name, out_shape=tuple(pltpu.HBM(b.shape, b.dtype) for b in bufs),
        in_specs=[HBM] * n + [SEM, SEM, ANY], out_specs=tuple([HBM] * n),
        input_output_aliases={i: i for i in range(n)},
        compiler_params=pltpu.CompilerParams(has_side_effects=DATAFLOW),
    )(*bufs, send, recv, after))


def _gather_copies(bufs, send, recv):
    x, y, c = _place()
    out = []
    for u, buf in enumerate(bufs):
        half = buf.shape[1] // 2
        mine = buf.at[2 * x + y, pl.ds(c * half, half)]
        out += [_remote(mine, mine, send.at[3 * u + j], recv.at[3 * u + j], (*chip, c))
                for j, chip in enumerate(_other_chips(x, y))]
    return out


def _exchange_copies(bufs, send, recv):
    x, y, c = _place()
    n = len(bufs) // 2
    out = []
    for k in range(n):
        half = bufs[k].shape[1] // 2
        theirs = bufs[k].at[pl.ds(0, N_CHIPS), pl.ds((1 - c) * half, half)]
        out.append(_remote(theirs, bufs[n + k], send.at[k], recv.at[k], (x, y, 1 - c)))
    return out


def _all_to_all_copies(bufs, send, recv):
    x, y, c = _place()
    n = len(bufs) // 2
    return [_remote(bufs[k].at[2 * chip[0] + chip[1]], bufs[n + k].at[j], send.at[3 * k + j], recv.at[3 * k + j], (*chip, c))
            for k in range(n) for j, chip in enumerate(_other_chips(x, y))]


def _forward_copies(bufs, send, recv):
    x, y, c = _place()
    out = []
    for u, buf in enumerate(bufs):
        half = buf.shape[1] // 2
        for j, chip in enumerate(_other_chips(x, y)):
            landed = buf.at[2 * chip[0] + chip[1], pl.ds(c * half, half)]
            out.append(_remote(landed, landed, send.at[3 * u + j], recv.at[3 * u + j], (x, y, 1 - c)))
    return out


def _gather8_copies(bufs, send, recv):
    x, y, c = _place()
    targets = [(x, y, 1 - c)] + [(*chip, c) for chip in _other_chips(x, y)]
    out = []
    for b, buf in enumerate(bufs):
        mine = buf.at[4 * x + 2 * y + c]
        out += [_remote(mine, mine, send.at[N_CHIPS * b + k], recv.at[N_CHIPS * b + k], to) for k, to in enumerate(targets)]
    return out


def _forward_slots(bufs, name):
    n = len(bufs)

    def body(*refs):
        bufs_ = refs[n:2 * n]
        send, recv, _ = refs[2 * n:]
        x, y, c = _place()
        chips = _other_chips(x, y)
        sent = []
        for b in range(n):
            for j, chip in enumerate(chips):
                slot = bufs_[b].at[4 * chip[0] + 2 * chip[1] + c]
                cpy = _remote(slot, slot, send.at[3 * b + j], recv.at[3 * b + j], (x, y, 1 - c))
                cpy.start()
                sent.append(cpy)
        for b in range(n):
            for j, chip in enumerate(chips):
                slot = bufs_[b].at[4 * chip[0] + 2 * chip[1] + 1 - c]
                _remote(slot, slot, send.at[3 * b + j], recv.at[3 * b + j], (x, y, c)).wait_recv()
        for cpy in sent:
            cpy.wait_send()

    shapes, aliases = _in_place(bufs)
    return list(_comm_call(body, name, bufs, shapes, (N_CHIPS - 1) * n, 0, aliases))


FWD_GROUPS = {'mix': ('ar_out', 'ff_in0', 'ff_out0'), 'l1': ('gm_in', 'gm_out', 'ff_in1', 'ff_out1')}
GRAD_LAYOUT = {'ff_in0': (0, 0), 'ff_in1': (0, 1), 'ff_out0': (1, 0), 'ff_out1': (1, 1), 'ar_in': (2, 0), 'ar_out': (3, 0),
               'gm_in': (4, 0), 'gm_out': (5, 0)}


class _MeshLink:
    def __init__(self, place, shards):
        self.place = place
        first = _gather_weights([shards['ar_in']])
        self.ready = {'ar_in': first[0]}
        self.pending, after = {}, first[0]
        for group, names in FWD_GROUPS.items():
            send, recv, bufs, token = _split_start(f"gather_{group}_start", [shards[n] for n in names], _gather_copies,
                                                   3 * len(names), after)
            self.pending[group] = (names, send, recv, bufs)
            after = token
        self.start_token = after[0, 0]
        self.forwarding, self.exchanging, self.sent, self.last_token = {}, {}, {}, None

    def prefetch(self, group, after):
        names, send, recv, bufs = self.pending.pop(group)
        bufs = _split_wait(f"gather_{group}_wait", bufs, send, recv, _gather_copies, after)
        send, recv, bufs, token = _split_start(f"forward_{group}_start", bufs, _forward_copies, 3 * len(names))
        self.forwarding[group] = (names, send, recv, bufs)
        return token[0, 0]

    def weights(self, group, after):
        if group in self.forwarding:
            names, send, recv, bufs = self.forwarding.pop(group)
            self.ready.update(zip(names, _split_wait(f"forward_{group}_wait", bufs, send, recv, _forward_copies, after)))
        return self.ready

    def gradients(self, group, grads, after=None):
        tok = self.poll(next(iter(grads.values())))
        names, mine = list(grads), list(grads.values())
        landing = [lax.empty((N_CHIPS, g.shape[1] // 2, g.shape[2]), g.dtype) for g in mine]
        send, recv, bufs, token = _split_start(f"exchange_{group}_start", mine + landing, _exchange_copies, len(names), after)
        self.exchanging[group] = (names, send, recv, bufs)
        self.last_token = token
        return token[0, 0] + tok

    def poll(self, after):
        tok = 0.0
        for group in list(self.exchanging):
            names, send, recv, bufs = self.exchanging.pop(group)
            bufs = _split_wait(f"exchange_{group}_wait", bufs, send, recv, _exchange_copies, after)
            sums = [_add_halves(g, r, self.place) for g, r in zip(bufs[:len(names)], bufs[len(names):])]
            landing = [lax.empty((N_CHIPS - 1,) + s.shape[1:], s.dtype) for s in sums]
            send, recv, bufs, token = _split_start(f"grads_{group}_start", sums + landing, _all_to_all_copies, 3 * len(names))
            self.sent[group] = (names, send, recv, bufs)
            self.last_token = token
            tok = tok + token[0, 0]
        return tok

    def reduce(self, groups, after):
        units = {}
        for group in groups:
            names, send, recv, bufs = self.sent.pop(group)
            bufs = _split_wait(f"grads_{group}_wait", bufs, send, recv, _all_to_all_copies, after)
            units.update(zip(names, zip(bufs[:len(names)], bufs[len(names):])))
        n_layers = {p: 1 + max(l for pp, l in GRAD_LAYOUT.values() if pp == p) for p, _ in GRAD_LAYOUT.values()}
        out = {}
        for name, (p, layer) in GRAD_LAYOUT.items():
            if name in units:
                out[p] = _add_chips(*units[name], self.place, out.get(p), layer, n_layers[p])
        params = sorted(out)
        return dict(zip(params, _join_halves([out[p] for p in params])))


def _rope_tables(n):
    t = jnp.arange(n)
    freqs = ROPE_THETA ** (-jnp.arange(ROPE_PAIRS, dtype=F32) / ROPE_PAIRS)
    ang_r = (t // GRID_W).astype(F32)[:, None] * freqs
    ang_c = (t % GRID_W).astype(F32)[:, None] * freqs
    cos = jnp.concatenate([jnp.cos(ang_r), jnp.cos(ang_r), jnp.cos(ang_c), jnp.cos(ang_c)], axis=-1)
    sin = jnp.concatenate([-jnp.sin(ang_r), jnp.sin(ang_r), -jnp.sin(ang_c), jnp.sin(ang_c)], axis=-1)
    return cos, sin


def _ffn_fwd(h2, w1, w2, tag):
    r, a = _matmul(h2, w1, kind='nn', b_split='n', out_dtype=BF16, epilogue='relu2', name=f"ffn_in_{tag}")
    f = _matmul(a, w2, kind='nn', b_split='k', out_dtype=F32, name=f"ffn_out_{tag}")
    return r, a, f


def _ffn_bwd(d_f, h2, r, a, w1, w2, tag):
    d_u = _matmul(d_f, w2, kind='nt', b_split='k', out_dtype=BF16, epilogue='times2x', extra=r, name=f"ffn_out_dx_{tag}")
    d_w2 = _matmul(a, d_f, kind='tn', out_split='k', out_dtype=BF16, name=f"ffn_out_dw_{tag}")
    d_w1 = _matmul(h2, d_u, kind='tn', out_split='n', out_dtype=BF16, name=f"ffn_in_dw_{tag}")
    d_h2 = _matmul(d_u, w1, kind='nt', b_split='n', out_dtype=F32, name=f"ffn_in_dx_{tag}")
    return d_h2, d_w1, d_w2


class _LocalLink:
    def __init__(self, big):
        self.big, self.grads, self.start_token = big, {}, 0.0

    def prefetch(self, group, after):
        return 0.0

    def poll(self, after):
        return 0.0

    def weights(self, group, after):
        return self.big

    def gradients(self, group, grads):
        self.grads.update(grads)
        return 0.0


def _local_step(xl0, xc0, target, ml, mc0, sp, link):
    n_lat, n_ctx = xl0.shape[0], xc0.shape[0]
    one = lambda v: 1.0 + v
    g = [[sp['norm_g'][i, k][None, :] for k in range(4)] for i in range(2)]

    sh1, sc1, gt1, sh2, sc2, gt2 = ml[0]
    big = link.weights('ar', None)
    sh1 = sh1 + link.start_token
    hl = _norm_fwd(xl0, g[0][0], one(sc1), b=sh1, out_dtype=BF16, name="l0_mod1")
    hc = _norm_fwd(xc0, g[0][0], one(mc0[1]), b=mc0[0], out_dtype=BF16, name="l0_mod1_ctx")
    proj_l = _matmul(hl, big['ar_in'], kind='nn', b_split='n', out_dtype=F32, name="ar_in_lat")
    proj_c = _matmul(hc, big['ar_in'], kind='nn', b_split='n', out_dtype=F32, name="ar_in_ctx")
    cos_l, sin_l = _rope_tables(n_lat)
    cos_c, sin_c = jnp.ones((n_ctx, HEAD_DIM), F32), jnp.zeros((n_ctx, HEAD_DIM), F32)
    q_g, k_g = sp['q_g'], sp['k_g']
    q_l, k_l, v_l = _qk_fwd(proj_l, q_g, k_g, cos_l, sin_l, name="qk_fwd_lat")
    _, k_c, v_c = _qk_fwd(proj_c, q_g, k_g, cos_c, sin_c, name="qk_fwd_ctx")
    k_all = jnp.concatenate([k_c, k_l], axis=0)
    v_all = jnp.concatenate([v_c, v_l], axis=0)
    attn, lse = _attn_fwd(q_l, k_all, v_all)
    conv_b = sp['conv_b'] + link.prefetch('mix', attn)
    conv_l = _conv_fwd(proj_l, sp['conv_w'], conv_b, name="conv_fwd_lat")
    conv_c = _conv_fwd(proj_c, sp['conv_w'], conv_b, name="conv_fwd_ctx")
    xs_f = jnp.concatenate([conv_c, conv_l], axis=0)
    xs_r = jnp.concatenate([conv_l, conv_c], axis=0)
    rnn_w = [(sp['wa'][d], sp['ba'][d][None, :], sp['wx'][d], sp['bx'][d][None, :], sp['lam'][d][None, :]) for d in range(2)]
    h_f, hp_f = _rglru_fwd(xs_f, *rnn_w[0], reverse=False, name="rglru_fwd_f")
    h_r, hp_r = _rglru_fwd(xs_r, *rnn_w[1], reverse=True, name="rglru_fwd_r")
    rnn = _rnn_out_fwd(h_f, h_r, proj_l, n_ctx, 0)
    w_mix = link.weights('mix', rnn)
    cat = jnp.concatenate([attn, rnn], axis=1)
    ol0 = _matmul(cat, w_mix['ar_out'], kind='nn', b_split='k', out_dtype=F32, name="ar_out")
    xm0 = _norm_fwd(ol0, g[0][1], gt1, res=xl0, out_dtype=F32, name="l0_res1")
    h2_0 = _norm_fwd(xm0, g[0][2], one(sc2), b=sh2, out_dtype=BF16, name="l0_mod2")
    r0, a0, f0 = _ffn_fwd(h2_0, w_mix['ff_in0'], w_mix['ff_out0'], "l0")
    xl1 = _norm_fwd(f0, g[0][3], gt2 + link.prefetch('l1', f0), res=xm0, out_dtype=F32, name="l0_res2")

    th1, tc1, tg1, th2, tc2, tg2 = ml[1]
    w_l1 = link.weights('l1', xl1)
    hl1 = _norm_fwd(xl1, g[1][0], one(tc1), b=th1, out_dtype=BF16, name="l1_mod1")
    z = _matmul(hl1, w_l1['gm_in'], kind='nn', b_split='n', bias=sp['gm_b_in'], out_dtype=F32, name="gm_in")
    b_sp_t = sp['gm_b_sp'].T
    gated = _gmlp_fwd(z, sp['gm_v_g'], sp['gm_v_b'], sp['gm_w_sp'], b_sp_t)
    ol1 = _matmul(gated, w_l1['gm_out'], kind='nn', b_split='k', out_dtype=F32, name="gm_out")
    xm1 = _norm_fwd(ol1, g[1][1], tg1, res=xl1, out_dtype=F32, name="l1_res1")
    h2_1 = _norm_fwd(xm1, g[1][2], one(tc2), b=th2, out_dtype=BF16, name="l1_mod2")
    r1, a1, f1 = _ffn_fwd(h2_1, w_l1['ff_in1'], w_l1['ff_out1'], "l1")
    y = _norm_fwd(f1, g[1][3], tg2, res=xm1, out_dtype=F32, name="l1_res2")

    dy, loss = _loss_head(y, target)

    d_f1, dg13, d_tg2, _ = _norm_bwd(dy, f1, g[1][3], tg2, out_dtype=BF16, name="l1_res2_bwd")
    d_h2, dw_ff_in1, dw_ff_out1 = _ffn_bwd(d_f1, h2_1, r1, a1, w_l1['ff_in1'], w_l1['ff_out1'], "l1")
    tok = link.gradients('ffn1', {'ff_in1': dw_ff_in1, 'ff_out1': dw_ff_out1})
    dxm1, dg12, d_tc2, d_th2 = _norm_bwd(d_h2, xm1, g[1][2], one(tc2) + tok, extra=dy, out_dtype=F32, name="l1_mod2_bwd")
    d_ol1, dg11, d_tg1, _ = _norm_bwd(dxm1, ol1, g[1][1], tg1, out_dtype=BF16, name="l1_res1_bwd")
    d_gated = _matmul(d_ol1, w_l1['gm_out'], kind='nt', b_split='k', out_dtype=F32, name="gm_out_dx")
    dw_gm_out = _matmul(gated, d_ol1, kind='tn', out_split='k', out_dtype=BF16, name="gm_out_dw")
    d_z, d_gm_b_in, d_vg, d_vb, d_wsp, d_bsp_t = _gmlp_bwd(z, d_gated, sp['gm_v_g'], sp['gm_v_b'], sp['gm_w_sp'], b_sp_t)
    dw_gm_in = _matmul(hl1, d_z, kind='tn', out_split='n', out_dtype=BF16, name="gm_in_dw")
    d_hl1 = _matmul(d_z, w_l1['gm_in'], kind='nt', b_split='n', out_dtype=F32, name="gm_in_dx")
    tok = link.gradients('gm', {'gm_in': dw_gm_in, 'gm_out': dw_gm_out})
    dxl1, dg10, d_tc1, d_th1 = _norm_bwd(d_hl1, xl1, g[1][0], one(tc1) + tok, extra=dxm1, out_dtype=F32, name="l1_mod1_bwd")

    d_f0, dg03, d_gt2, _ = _norm_bwd(dxl1, f0, g[0][3], gt2, out_dtype=BF16, name="l0_res2_bwd")
    d_h2, dw_ff_in0, dw_ff_out0 = _ffn_bwd(d_f0, h2_0, r0, a0, w_mix['ff_in0'], w_mix['ff_out0'], "l0")
    tok = link.gradients('ffn0', {'ff_in0': dw_ff_in0, 'ff_out0': dw_ff_out0})
    dxm0, dg02, d_sc2, d_sh2 = _norm_bwd(d_h2, xm0, g[0][2], one(sc2) + tok, extra=dxl1, out_dtype=F32, name="l0_mod2_bwd")
    d_ol0, dg01, d_gt1, _ = _norm_bwd(dxm0, ol0, g[0][1], gt1, out_dtype=BF16, name="l0_res1_bwd")
    d_cat = _matmul(d_ol0, w_mix['ar_out'], kind='nt', b_split='k', out_dtype=F32, name="ar_out_dx")
    dw_ar_out = _matmul(cat, d_ol0, kind='tn', out_split='k', out_dtype=BF16, name="ar_out_dw")
    dq, dk_all, dv_all = _attn_bwd(q_l, k_all, v_all, attn, lse, d_cat)
    tok = link.poll(dq)
    d_h, d_gate = _rnn_out_bwd(d_cat, h_f, h_r, proj_l, n_ctx, 0)
    zeros_c = jnp.zeros((n_ctx, D_RNN), F32)
    rnn_wb = [(wa_, ba_ + tok, wx_, bx_, lam_) for wa_, ba_, wx_, bx_, lam_ in rnn_w]
    dxs_f, d_wa0, d_ba0, d_wx0, d_bx0, d_lam0 = _rglru_bwd(
        xs_f, hp_f, jnp.concatenate([zeros_c, d_h], axis=0), *rnn_wb[0], reverse=False, name="rglru_bwd_f")
    dxs_r, d_wa1, d_ba1, d_wx1, d_bx1, d_lam1 = _rglru_bwd(
        xs_r, hp_r, jnp.concatenate([d_h, zeros_c], axis=0), *rnn_wb[1], reverse=True, name="rglru_bwd_r")
    d_xr_l, d_cw_l, d_cb_l = _conv_bwd(dxs_f[n_ctx:], dxs_r[:n_lat], proj_l, sp['conv_w'], name="conv_bwd_lat")
    d_xr_c, d_cw_c, d_cb_c = _conv_bwd(dxs_f[:n_ctx], dxs_r[n_lat:], proj_c, sp['conv_w'], name="conv_bwd_ctx")
    dp_qk_l, d_qg, d_kg_l = _qk_bwd(dq, dk_all[n_ctx:], proj_l, q_g, k_g, cos_l, sin_l, name="qk_bwd_lat")
    dp_qk_c, _, d_kg_c = _qk_bwd(None, dk_all[:n_ctx], proj_c, q_g, k_g, cos_c, sin_c, name="qk_bwd_ctx")
    d_proj = _assemble_d_proj(dp_qk_l, dp_qk_c, dv_all, d_xr_l, d_xr_c, d_gate)
    dw_ar_in = _matmul(jnp.concatenate([hl, hc], axis=0), d_proj, kind='tn', out_split='n', out_dtype=BF16, name="ar_in_dw")
    d_hl = _matmul(d_proj, big['ar_in'], kind='nt', b_split='n', out_dtype=F32, a_rows=(0, n_lat), name="ar_in_dx_lat")
    d_hc = _matmul(d_proj, big['ar_in'], kind='nt', b_split='n', out_dtype=F32, a_rows=(n_lat, n_ctx), name="ar_in_dx_ctx")
    grad_x, dg00, d_sc1, d_sh1 = _norm_bwd(d_hl, xl0, g[0][0], one(sc1), extra=dxm0, out_dtype=F32, name="l0_mod1_bwd")
    _, dg00c, d_mc_scale, d_mc_shift = _norm_bwd(d_hc, xc0, g[0][0], one(mc0[1]), out_dtype=BF16, name="l0_mod1_ctx_bwd")

    zeros_d = jnp.zeros_like(d_sh1)
    small = {
        'd_ml0': jnp.concatenate([d_sh1, d_sc1, d_gt1, d_sh2, d_sc2, d_gt2], axis=1),
        'd_ml1': jnp.concatenate([d_th1, d_tc1, d_tg1, d_th2, d_tc2, d_tg2], axis=1),
        'd_mc0': jnp.concatenate([d_mc_shift, d_mc_scale] + [zeros_d] * 4, axis=1),
        'norm_g': jnp.stack([jnp.concatenate([dg00 + dg00c, dg01, dg02, dg03], axis=0),
                             jnp.concatenate([dg10, dg11, dg12, dg13], axis=0)]),
        'q_g': d_qg, 'k_g': d_kg_l + d_kg_c, 'conv_w': d_cw_l + d_cw_c, 'conv_b': d_cb_l + d_cb_c,
        'wa': jnp.stack([d_wa0, d_wa1]), 'ba': jnp.concatenate([d_ba0, d_ba1], axis=0),
        'wx': jnp.stack([d_wx0, d_wx1]), 'bx': jnp.concatenate([d_bx0, d_bx1], axis=0),
        'lam': jnp.concatenate([d_lam0, d_lam1], axis=0),
        'gm_b_in': d_gm_b_in, 'gm_v_g': d_vg, 'gm_v_b': d_vb, 'gm_w_sp': d_wsp, 'gm_b_sp': d_bsp_t.T,
        'loss': loss,
    }
    return grad_x, small, {'ar_in': dw_ar_in, 'ar_out': dw_ar_out}


MOD_ROWS = 16
SMALL_F32 = ('d_ml0', 'd_ml1', 'd_mc0', 'norm_g', 'q_g', 'k_g', 'conv_w', 'conv_b', 'ba', 'bx', 'lam', 'gm_b_in', 'gm_v_g',
             'gm_v_b', 'gm_b_sp', 'loss')
SMALL_BF16 = ('wa', 'wx', 'gm_w_sp')


def _silu(v):
    return v * _sigmoid(v)


def _chip_concat(gathered, axis):
    return jnp.concatenate([gathered[2 * q] for q in range(N_CHIPS)], axis=axis)


def kernel(x, c, ctx, c_ctx, w_mod, b_mod, norm_g, w_ff_in, w_ff_out, ar_w_in, ar_q_g, ar_k_g, ar_conv_w, ar_conv_b, ar_wa, ar_ba, ar_wx, ar_bx, ar_lambda, ar_w_out, gm_w_in, gm_b_in, gm_v_g, gm_v_b, gm_w_sp, gm_b_sp, gm_w_out, loss_target, m_c_ctx, m_w_mod, m_b_mod, m_norm_g, m_w_ff_in, m_w_ff_out, m_ar_w_in, m_ar_q_g, m_ar_k_g, m_ar_conv_w, m_ar_conv_b, m_ar_wa, m_ar_ba, m_ar_wx, m_ar_bx, m_ar_lambda, m_ar_w_out, m_gm_w_in, m_gm_b_in, m_gm_v_g, m_gm_v_b, m_gm_w_sp, m_gm_b_sp, m_gm_w_out, v_c_ctx, v_w_mod, v_b_mod, v_norm_g, v_w_ff_in, v_w_ff_out, v_ar_w_in, v_ar_q_g, v_ar_k_g, v_ar_conv_w, v_ar_conv_b, v_ar_wa, v_ar_ba, v_ar_wx, v_ar_bx, v_ar_lambda, v_ar_w_out, v_gm_w_in, v_gm_b_in, v_gm_v_g, v_gm_v_b, v_gm_w_sp, v_gm_b_sp, v_gm_w_out):
    weights = dict(c_ctx=c_ctx, w_mod=w_mod, b_mod=b_mod, norm_g=norm_g, w_ff_in=w_ff_in, w_ff_out=w_ff_out, ar_w_in=ar_w_in,
                   ar_q_g=ar_q_g, ar_k_g=ar_k_g, ar_conv_w=ar_conv_w, ar_conv_b=ar_conv_b, ar_wa=ar_wa, ar_ba=ar_ba, ar_wx=ar_wx,
                   ar_bx=ar_bx, ar_lambda=ar_lambda, ar_w_out=ar_w_out, gm_w_in=gm_w_in, gm_b_in=gm_b_in, gm_v_g=gm_v_g,
                   gm_v_b=gm_v_b, gm_w_sp=gm_w_sp, gm_b_sp=gm_b_sp, gm_w_out=gm_w_out)
    m_in = dict(c_ctx=m_c_ctx, w_mod=m_w_mod, b_mod=m_b_mod, norm_g=m_norm_g, w_ff_in=m_w_ff_in, w_ff_out=m_w_ff_out,
                ar_w_in=m_ar_w_in, ar_q_g=m_ar_q_g, ar_k_g=m_ar_k_g, ar_conv_w=m_ar_conv_w, ar_conv_b=m_ar_conv_b, ar_wa=m_ar_wa,
                ar_ba=m_ar_ba, ar_wx=m_ar_wx, ar_bx=m_ar_bx, ar_lambda=m_ar_lambda, ar_w_out=m_ar_w_out, gm_w_in=m_gm_w_in,
                gm_b_in=m_gm_b_in, gm_v_g=m_gm_v_g, gm_v_b=m_gm_v_b, gm_w_sp=m_gm_w_sp, gm_b_sp=m_gm_b_sp, gm_w_out=m_gm_w_out)
    v_in = dict(c_ctx=v_c_ctx, w_mod=v_w_mod, b_mod=v_b_mod, norm_g=v_norm_g, w_ff_in=v_w_ff_in, w_ff_out=v_w_ff_out,
                ar_w_in=v_ar_w_in, ar_q_g=v_ar_q_g, ar_k_g=v_ar_k_g, ar_conv_w=v_ar_conv_w, ar_conv_b=v_ar_conv_b, ar_wa=v_ar_wa,
                ar_ba=v_ar_ba, ar_wx=v_ar_wx, ar_bx=v_ar_bx, ar_lambda=v_ar_lambda, ar_w_out=v_ar_w_out, gm_w_in=v_gm_w_in,
                gm_b_in=v_gm_b_in, gm_v_g=v_gm_v_g, gm_v_b=v_gm_v_b, gm_w_sp=v_gm_w_sp, gm_b_sp=v_gm_b_sp, gm_w_out=v_gm_w_out)

    xi, yi, ci = lax.axis_index("x"), lax.axis_index("y"), lax.axis_index("c")
    chip = 2 * xi + yi
    dev = 4 * xi + 2 * yi + ci
    place = jnp.stack([chip, ci]).astype(jnp.int32)
    n_lat, d = x.shape[1], x.shape[2]
    d6 = 6 * d
    cols_mod = w_mod.shape[2]

    mine = [c, norm_g, ar_conv_w[0], ar_ba[0], ar_bx[0], ar_lambda[0], gm_b_in, gm_v_g, gm_v_b]
    gathered = _allgather8([_pack(mine)], "gather_small_params")[0]
    parts = _unpack_devices(gathered, [a.shape for a in mine])
    c_all = parts[0].reshape(N_DEV, d)
    sp = {'norm_g': _chip_concat(parts[1], 2), 'q_g': ar_q_g, 'k_g': ar_k_g, 'conv_w': _chip_concat(parts[2], 1),
          'conv_b': ar_conv_b, 'wa': ar_wa[0], 'ba': _chip_concat(parts[3], 1), 'wx': ar_wx[0], 'bx': _chip_concat(parts[4], 1),
          'lam': _chip_concat(parts[5], 1), 'gm_b_in': _chip_concat(parts[6], 1), 'gm_v_g': _chip_concat(parts[7], 1),
          'gm_v_b': _chip_concat(parts[8], 1), 'gm_w_sp': gm_w_sp[0], 'gm_b_sp': gm_b_sp[0]}

    def mod_operand(c_rows, cc):
        row = lax.broadcasted_iota(jnp.int32, (MOD_ROWS - N_DEV, d), 0)
        lower = jnp.where(row == 0, jnp.broadcast_to(_silu(cc), (MOD_ROWS - N_DEV, d)), 0.0)
        sig = _sigmoid(cc)
        return jnp.concatenate([_silu(c_rows), lower], axis=0), sig * (1.0 + cc * (1.0 - sig))

    s_mod, dsilu_ctx = _small(mod_operand, [((MOD_ROWS, d), F32), ((1, d), F32)], c_all, c_ctx[None, :], name="mod_operand")
    b_mod_mine = lax.dynamic_slice(b_mod, (0, chip * cols_mod), (2, cols_mod))
    mod = [_matmul(s_mod, w_mod[i], kind='nn', bias=b_mod_mine[i][None, :], out_dtype=F32, name=f"mod_fwd_{i}") for i in range(2)]
    mod_all = _allgather8([jnp.concatenate(mod, axis=0)], "gather_mod")[0]
    mod_all = _chip_concat(mod_all, 1).reshape(2, MOD_ROWS, d6)
    ml = [jnp.split(lax.dynamic_slice(mod_all[i], (dev, 0), (1, d6)), 6, axis=1) for i in range(2)]
    mc0 = jnp.split(mod_all[0, N_DEV:N_DEV + 1], 6, axis=1)[:2]

    names = ('w_ff_in', 'w_ff_out', 'ar_w_in', 'ar_w_out', 'gm_w_in', 'gm_w_out')
    keys = {'w_ff_in': ('ff_in0', 'ff_in1'), 'w_ff_out': ('ff_out0', 'ff_out1'), 'ar_w_in': ('ar_in',), 'ar_w_out': ('ar_out',),
            'gm_w_in': ('gm_in',), 'gm_w_out': ('gm_out',)}
    shards = {key: _cast_shard(weights[n], place, layer, f"cast_{key}") for n in names for layer, key in enumerate(keys[n])}
    link = _MeshLink(place, shards)

    grad_x, small, last_grads = _local_step(x[0], ctx[0], loss_target[0], ml, mc0, sp, link)

    def step(n, grad):
        return _adamw(weights[n], grad.reshape(weights[n].shape), m_in[n], v_in[n], f"adamw_{n}")

    small_f32, small_bf16 = [small[k] for k in SMALL_F32], [small[k] for k in SMALL_BF16]
    dev_arr = dev.astype(jnp.int32)[None]
    slots = [_into_slot(_pack(small_f32), dev_arr, "small_grads_slot_f32"),
             _into_slot(_pack(small_bf16, BF16), dev_arr, "small_grads_slot_bf16")]
    s_send, s_recv, slots, s_token = _split_start("small_grads_start", slots, _gather8_copies, 2 * N_CHIPS, grad_x)
    link.gradients('ar', last_grads, s_token)
    link.poll(link.last_token)
    reduced = link.reduce(('ffn1', 'gm', 'ffn0'), link.last_token)
    stepped = {n: step(n, reduced[names.index(n)]) for n in ('w_ff_in', 'w_ff_out', 'gm_w_in', 'gm_w_out')}
    reduced = link.reduce(('ar',), stepped['gm_w_out'][1])
    stepped.update({n: step(n, reduced[names.index(n)]) for n in ('ar_w_in', 'ar_w_out')})
    slots = _split_wait("small_grads_wait", slots, s_send, s_recv, _gather8_copies, stepped['ar_w_out'][1])
    small8, small8_bf16 = _forward_slots(slots, "small_grads_forward")
    total = dict(zip(SMALL_F32, _unpack(_sum_devices(small8).reshape(-1), [a.shape for a in small_f32])))
    total.update(zip(SMALL_BF16, _unpack(_sum_devices(small8_bf16).reshape(-1), [a.shape for a in small_bf16])))
    per_dev = _unpack_devices(small8, [(d6,), (d6,)])
    pad_rows = jnp.zeros((MOD_ROWS - N_DEV - 1, d6), F32)
    d_mod = [jnp.concatenate([per_dev[0], total['d_mc0'], pad_rows], axis=0),
             jnp.concatenate([per_dev[1], jnp.zeros((MOD_ROWS - N_DEV, d6), F32)], axis=0)]
    d_mod_mine = [lax.dynamic_slice(dm, (0, chip * cols_mod), (MOD_ROWS, cols_mod)) for dm in d_mod]
    g_w_mod = jnp.stack([_matmul(s_mod, d_mod_mine[i], kind='tn', out_dtype=F32, name=f"mod_dw_{i}") for i in range(2)])
    d_s_part = _matmul(d_mod_mine[0], w_mod[0], kind='nt', out_dtype=F32, name="mod_ds")
    d_s_all = _allgather8([d_s_part[N_DEV:]], "gather_mod_ds")[0]

    def c_ctx_grad(parts_, dsilu):
        acc = parts_[0, 0:1]
        for q in range(1, N_CHIPS):
            acc = acc + parts_[2 * q, 0:1]
        return (acc * dsilu,)

    g_c_ctx = _small(c_ctx_grad, [((1, d), F32)], d_s_all, dsilu_ctx, name="c_ctx_grad")[0].reshape(d)

    def mine_of(full_grad, axis, n_shard):
        return lax.dynamic_slice_in_dim(full_grad, chip * n_shard, n_shard, axis=axis)

    grads_out = {
        'c_ctx': g_c_ctx, 'w_mod': g_w_mod,
        'b_mod': jnp.stack([total['d_ml0'][0] + total['d_mc0'][0], total['d_ml1'][0]]),
        'norm_g': mine_of(total['norm_g'], 2, norm_g.shape[2]),
        'ar_q_g': total['q_g'], 'ar_k_g': total['k_g'], 'ar_conv_w': mine_of(total['conv_w'], 1, ar_conv_w.shape[2])[None],
        'ar_conv_b': total['conv_b'], 'ar_wa': total['wa'][None], 'ar_ba': mine_of(total['ba'], 1, ar_ba.shape[2])[None],
        'ar_wx': total['wx'][None], 'ar_bx': mine_of(total['bx'], 1, ar_bx.shape[2])[None],
        'ar_lambda': mine_of(total['lam'], 1, ar_lambda.shape[2])[None],
        'gm_b_in': mine_of(total['gm_b_in'], 1, gm_b_in.shape[1]),
        'gm_v_g': mine_of(total['gm_v_g'], 1, gm_v_g.shape[1]), 'gm_v_b': mine_of(total['gm_v_b'], 1, gm_v_b.shape[1]),
        'gm_w_sp': total['gm_w_sp'][None], 'gm_b_sp': total['gm_b_sp'][None],
    }
    stepped.update({n: step(n, grad) for n, grad in grads_out.items()})
    stepped = [stepped[n] for n in weights]
    loss = total['loss'].reshape(())
    return (loss, grad_x[None], *[s[0] for s in stepped], *[s[1] for s in stepped], *[s[2] for s in stepped],
            *[s[3] for s in stepped])
```

```python
import functools
import math

import jax
import jax.numpy as jnp
from jax import lax
from jax.experimental import pallas as pl
from jax.experimental.pallas import tpu as pltpu

F32 = jnp.float32
BF16 = jnp.bfloat16
MESH = pl.DeviceIdType.MESH
ANY = pl.BlockSpec(memory_space=pl.ANY)

VMEM_LIMIT_BYTES = 52 * 1024 * 1024
LANES = 128
N_CHIPS = 4
N_DEV = 8

HEAD_DIM = 128
N_HEADS = 8
N_KV = 2
GROUP = N_HEADS // N_KV
ATTN_W = N_HEADS * HEAD_DIM
KV_W = N_KV * HEAD_DIM
D_RNN = 1024
RNN_BLOCKS = 8
RNN_BW = D_RNN // RNN_BLOCKS
CONV_W = 4
RG_C = 8.0
GRID_W = 64
ROPE_THETA = 10000.0
ROPE_PAIRS = HEAD_DIM // 4
GM_GROUPS = 16
CHUNK = 128
EPS = 1e-6
ADAM_LR, ADAM_B1, ADAM_B2, ADAM_EPS, ADAM_WD, ADAM_STEP = 0.001, 0.9, 0.999, 1e-08, 0.01, 10
GELU_C = math.sqrt(2.0 / math.pi)
LOG2E = math.log2(math.e)


def _params(sem=None):
    return pltpu.CompilerParams(dimension_semantics=sem, vmem_limit_bytes=VMEM_LIMIT_BYTES)


def _tile(dim, pref, unit):
    best = None
    t = unit
    while t <= min(dim, pref):
        if dim % t == 0:
            best = t
        t += unit
    return best if best is not None else dim


def _full(shape):
    nd = len(shape)
    return pl.BlockSpec(shape, lambda *_: (0,) * nd)


def _blocked_map(split, per_q):
    if split == 'n':
        return lambda r, c: (c // per_q, r, c % per_q)
    if split == 'k':
        return lambda r, c: (r // per_q, r % per_q, c)
    return lambda r, c: (r, c)


def _logical_shape(arr, split):
    if split == 'n':
        return arr.shape[1], arr.shape[0] * arr.shape[2]
    if split == 'k':
        return arr.shape[0] * arr.shape[1], arr.shape[2]
    return arr.shape


def _matmul(a, b, *, kind, name, out_dtype, b_split=None, out_split=None, bias=None, epilogue=None, extra=None,
            a_rows=None, out_stack=None, pref=(1024, 1024, 2048)):
    b_rows, b_cols = _logical_shape(b, b_split)
    row0 = 0
    if kind == 'nn':
        m, kc = a.shape
        n = b_cols
        assert b_rows == kc
    elif kind == 'nt':
        m, kc = a.shape
        n = b_rows
        assert b_cols == kc
    if a_rows is not None:
        assert kind != 'tn'
        row0, m = a_rows
    if kind == 'tn':
        kc, m = a.shape
        n = b_cols
        assert b_rows == kc
    b_row_ext = b.shape[1] if b_split == 'k' else b_rows
    b_col_ext = b.shape[2] if b_split == 'n' else b_cols
    out_row_ext = m // N_CHIPS if out_split == 'k' else m
    out_col_ext = n // N_CHIPS if out_split == 'n' else n
    if kind == 'nn':
        ti = _tile(math.gcd(min(m, out_row_ext), row0), pref[0], 16)
        tj = _tile(math.gcd(b_col_ext, out_col_ext), pref[1], LANES)
        tl = _tile(b_row_ext, pref[2], LANES)
        a_spec = pl.BlockSpec((ti, tl), lambda i, j, l: (i + row0 // ti, l))
        b_tile, b_rc = (tl, tj), (lambda i, j, l: (l, j))
        dims = (((1,), (0,)), ((), ()))
    elif kind == 'nt':
        ti = _tile(math.gcd(min(m, out_row_ext), row0), pref[0], 16)
        tj = _tile(math.gcd(b_row_ext, out_col_ext), pref[1], LANES)
        tl = _tile(b_col_ext, pref[2], LANES)
        a_spec = pl.BlockSpec((ti, tl), lambda i, j, l: (i + row0 // ti, l))
        b_tile, b_rc = (tj, tl), (lambda i, j, l: (j, l))
        dims = (((1,), (1,)), ((), ()))
    else:
        ti = _tile(out_row_ext, pref[0], LANES)
        tj = _tile(math.gcd(b_col_ext, out_col_ext), pref[1], LANES)
        tl = _tile(b_row_ext, pref[2], 16)
        a_spec = pl.BlockSpec((tl, ti), lambda i, j, l: (l, i))
        b_tile, b_rc = (tl, tj), (lambda i, j, l: (l, j))
        dims = (((0,), (0,)), ((), ()))
    grid = (m // ti, n // tj, kc // tl)
    n_l = grid[2]

    if b_split is None:
        b_spec = pl.BlockSpec(b_tile, b_rc)
    else:
        per_q = (b.shape[2] // b_tile[1]) if b_split == 'n' else (b.shape[1] // b_tile[0])
        bmap = _blocked_map(b_split, per_q)
        b_spec = pl.BlockSpec((None,) + b_tile, lambda i, j, l: bmap(*b_rc(i, j, l)))
    if out_stack is not None:
        layer, n_layers, _ = out_stack
        out_shape2 = (n_layers, m, n)
        o_spec = pl.BlockSpec((None, ti, tj), lambda i, j, l: (layer, i, j))
    elif out_split is None:
        out_shape2 = (m, n)
        o_spec = pl.BlockSpec((ti, tj), lambda i, j, l: (i, j))
    else:
        out_shape2 = (N_CHIPS, m // N_CHIPS, n) if out_split == 'k' else (N_CHIPS, m, n // N_CHIPS)
        per_q = (out_shape2[2] // tj) if out_split == 'n' else (out_shape2[1] // ti)
        omap = _blocked_map(out_split, per_q)
        o_spec = pl.BlockSpec((None, ti, tj), lambda i, j, l: omap(i, j))

    in_specs = [a_spec, b_spec]
    operands = [a, b]
    if bias is not None:
        in_specs.append(pl.BlockSpec((1, tj), lambda i, j, l: (0, j)))
        operands.append(bias)
    if extra is not None:
        in_specs.append(pl.BlockSpec((ti, tj), lambda i, j, l: (i, j)))
        operands.append(extra)
    if epilogue == 'relu2':
        out_shape = (jax.ShapeDtypeStruct(out_shape2, out_dtype), jax.ShapeDtypeStruct(out_shape2, out_dtype))
        out_specs = (o_spec, o_spec)
    else:
        out_shape = jax.ShapeDtypeStruct(out_shape2, out_dtype)
        out_specs = o_spec
    has_bias, has_extra = bias is not None, extra is not None
    has_dest = out_stack is not None and out_stack[2] is not None
    if has_dest:
        in_specs.append(ANY)
        operands.append(out_stack[2])

    def body(*refs):
        a_ref, b_ref = refs[0], refs[1]
        pos = 2
        bias_ref = extra_ref = None
        if has_bias:
            bias_ref = refs[pos]
            pos += 1
        if has_extra:
            extra_ref = refs[pos]
            pos += 1
        if has_dest:
            pos += 1
        outs = refs[pos:] if n_l == 1 else refs[pos:-1]

        def finish(acc):
            if has_bias:
                acc = acc + bias_ref[...]
            if epilogue == 'relu2':
                r = jnp.maximum(acc, 0.0)
                outs[0][...] = r.astype(outs[0].dtype)
                outs[1][...] = (r * r).astype(outs[1].dtype)
            elif epilogue == 'times2x':
                outs[0][...] = (acc * (2.0 * extra_ref[...].astype(F32))).astype(outs[0].dtype)
            else:
                outs[0][...] = acc.astype(outs[0].dtype)

        def product():
            return lax.dot_general(a_ref[...].astype(BF16), b_ref[...].astype(BF16), dims, preferred_element_type=F32)

        if n_l == 1:
            finish(product())
            return
        acc_ref = refs[-1]
        step = pl.program_id(2)

        @pl.when(step == 0)
        def _():
            acc_ref[...] = jnp.zeros_like(acc_ref)

        acc_ref[...] += product()

        @pl.when(step == n_l - 1)
        def _():
            finish(acc_ref[...])

    return pl.pallas_call(
        body, name=name, grid=grid, in_specs=in_specs, out_specs=out_specs, out_shape=out_shape,
        input_output_aliases={len(operands) - 1: 0} if has_dest else {},
        scratch_shapes=[] if n_l == 1 else [pltpu.VMEM((ti, tj), F32)],
        compiler_params=_params(("parallel", "parallel", "arbitrary")),
    )(*operands)


def _small(fn, out_shapes, *arrays, name):
    n_in = len(arrays)

    def body(*refs):
        res = fn(*[r[...] for r in refs[:n_in]])
        for o_ref, v in zip(refs[n_in:], res):
            o_ref[...] = v.astype(o_ref.dtype)

    return pl.pallas_call(
        body, name=name, out_shape=tuple(jax.ShapeDtypeStruct(s, d) for s, d in out_shapes),
        in_specs=[_full(a.shape) for a in arrays], out_specs=tuple(_full(s) for s, _ in out_shapes), grid=(1,),
        compiler_params=_params(("arbitrary",)),
    )(*arrays)


def _rows_tile(rows, cols, itemsize=4, budget=2 * 1024 * 1024):
    return _tile(rows, max(16, budget // (cols * itemsize)), 16)


def _rowwise(fn, out_dtypes, *arrays, name):
    rows, cols = arrays[0].shape
    tr = _rows_tile(rows, cols)
    n_in = len(arrays)

    def body(*refs):
        res = fn(*[r[...] for r in refs[:n_in]])
        for o_ref, v in zip(refs[n_in:], res):
            o_ref[...] = v.astype(o_ref.dtype)

    spec = pl.BlockSpec((tr, cols), lambda i: (i, 0))
    return pl.pallas_call(
        body, name=name, grid=(rows // tr,), in_specs=[spec] * n_in, out_specs=tuple(spec for _ in out_dtypes),
        out_shape=tuple(jax.ShapeDtypeStruct((rows, cols), d) for d in out_dtypes),
        compiler_params=_params(("parallel",)),
    )(*arrays)


def _as2d(a):
    return a.reshape(1, a.size) if a.ndim < 2 else a.reshape(-1, a.shape[-1])


def _cast_shard(w, place, layer, name):
    _, rows, cols = w.shape
    tr = _rows_tile(rows, cols)

    def body(place_ref, w_ref, o_ref):
        o_ref[...] = w_ref[...].astype(o_ref.dtype)

    return pl.pallas_call(
        body, name=name, out_shape=jax.ShapeDtypeStruct((N_CHIPS, rows, cols), BF16),
        grid_spec=pltpu.PrefetchScalarGridSpec(
            num_scalar_prefetch=1, grid=(rows // tr,),
            in_specs=[pl.BlockSpec((None, tr, cols), lambda i, pr: (layer, i, 0))],
            out_specs=pl.BlockSpec((None, tr, cols), lambda i, pr: (pr[0], i, 0))),
        compiler_params=_params(("parallel",)),
    )(place, w)


def _norm_fwd(x, g, a, b=None, res=None, *, out_dtype, name, into=None):
    rows, d = x.shape
    row0, total, dest = into if into is not None else (0, rows, None)
    tr = _rows_tile(math.gcd(rows, row0), d, budget=4 * 1024 * 1024)
    has_b, has_res = b is not None, res is not None

    def body(*refs):
        x_ref, g_ref, a_ref = refs[:3]
        pos = 3
        xv = x_ref[...]
        rstd = lax.rsqrt(jnp.mean(xv * xv, axis=-1, keepdims=True) + EPS)
        y = (xv * rstd * g_ref[...]) * a_ref[...]
        if has_b:
            y = y + refs[pos][...]
            pos += 1
        if has_res:
            y = y + refs[pos][...]
            pos += 1
        refs[-1][...] = y.astype(refs[-1].dtype)

    row = pl.BlockSpec((tr, d), lambda i: (i, 0))
    vec = pl.BlockSpec((1, d), lambda i: (0, 0))
    operands, specs = [x, g, a], [row, vec, vec]
    if has_b:
        operands.append(b)
        specs.append(vec)
    if has_res:
        operands.append(res)
        specs.append(row)
    if dest is not None:
        operands.append(dest)
        specs.append(ANY)
    return pl.pallas_call(
        body, name=name, grid=(rows // tr,), in_specs=specs, out_specs=pl.BlockSpec((tr, d), lambda i: (i + row0 // tr, 0)),
        out_shape=jax.ShapeDtypeStruct((total, d), out_dtype), compiler_params=_params(("parallel",)),
        input_output_aliases={} if dest is None else {len(operands) - 1: 0},
    )(*operands)


def _norm_bwd(dy, x, g, a, extra=None, *, out_dtype, name):
    rows, d = x.shape
    tr = _rows_tile(rows, d)
    has_extra = extra is not None

    def body(*refs):
        dy_ref, x_ref, g_ref, a_ref = refs[:4]
        pos = 4
        extra_ref = None
        if has_extra:
            extra_ref = refs[pos]
            pos += 1
        dx_ref, dg_ref, da_ref, db_ref = refs[pos:pos + 4]

        @pl.when(pl.program_id(0) == 0)
        def _():
            dg_ref[...] = jnp.zeros_like(dg_ref)
            da_ref[...] = jnp.zeros_like(da_ref)
            db_ref[...] = jnp.zeros_like(db_ref)

        xv = x_ref[...]
        dyv = dy_ref[...].astype(F32)
        rstd = lax.rsqrt(jnp.mean(xv * xv, axis=-1, keepdims=True) + EPS)
        nrm = xv * rstd
        gv = g_ref[...]
        da_ref[...] += jnp.sum(dyv * (nrm * gv), axis=0, keepdims=True)
        db_ref[...] += jnp.sum(dyv, axis=0, keepdims=True)
        dt = dyv * a_ref[...]
        dg_ref[...] += jnp.sum(dt * nrm, axis=0, keepdims=True)
        dn = dt * gv
        dx = rstd * (dn - nrm * jnp.mean(dn * nrm, axis=-1, keepdims=True))
        if has_extra:
            dx = dx + extra_ref[...]
        dx_ref[...] = dx.astype(dx_ref.dtype)

    row = pl.BlockSpec((tr, d), lambda i: (i, 0))
    vec = pl.BlockSpec((1, d), lambda i: (0, 0))
    operands, specs = [dy, x, g, a], [row, row, vec, vec]
    if has_extra:
        operands.append(extra)
        specs.append(row)
    vshape = jax.ShapeDtypeStruct((1, d), F32)
    return pl.pallas_call(
        body, name=name, grid=(rows // tr,), in_specs=specs, out_specs=(row, vec, vec, vec),
        out_shape=(jax.ShapeDtypeStruct((rows, d), out_dtype), vshape, vshape, vshape),
        compiler_params=_params(("arbitrary",)),
    )(*operands)


def _loss_head(y, target):
    rows, d = y.shape
    tr = _rows_tile(rows, d)

    def body(y_ref, t_ref, dy_ref, loss_ref):
        @pl.when(pl.program_id(0) == 0)
        def _():
            loss_ref[...] = jnp.zeros_like(loss_ref)

        err = y_ref[...] - t_ref[...]
        dy_ref[...] = err * (1.0 / d)
        loss_ref[...] += jnp.sum(jnp.sum(err * err, axis=-1, keepdims=True), axis=0, keepdims=True) * (0.5 / d)

    row = pl.BlockSpec((tr, d), lambda i: (i, 0))
    return pl.pallas_call(
        body, name="loss_head", grid=(rows // tr,), in_specs=[row, row], out_specs=(row, _full((1, 1))),
        out_shape=(jax.ShapeDtypeStruct((rows, d), F32), jax.ShapeDtypeStruct((1, 1), F32)),
        compiler_params=_params(("arbitrary",)),
    )(y, target)


def _rope_partner(v):
    lane = lax.broadcasted_iota(jnp.int32, v.shape, 1)
    up = pltpu.roll(v, HEAD_DIM - ROPE_PAIRS, 1)
    down = pltpu.roll(v, ROPE_PAIRS, 1)
    return jnp.where((lane % (2 * ROPE_PAIRS)) < ROPE_PAIRS, up, down)


def _qk_fwd(proj, q_g, k_g, cos, sin, *, name, kv_into=None):
    rows = proj.shape[0]
    row0, total, kv_dest = kv_into if kv_into is not None else (0, rows, None)
    tr = _tile(math.gcd(rows, row0), 256, 16)
    width = ATTN_W + 2 * KV_W

    def body(p_ref, qg_ref, kg_ref, cos_ref, sin_ref, *rest):
        q_ref, k_ref, v_ref = rest[-3:]
        cosv, sinv = cos_ref[...], sin_ref[...]
        for h in range(N_HEADS + N_KV):
            xv = p_ref[:, h * HEAD_DIM:(h + 1) * HEAD_DIM]
            gain = qg_ref[...] if h < N_HEADS else kg_ref[...]
            t = xv * lax.rsqrt(jnp.mean(xv * xv, axis=-1, keepdims=True) + EPS) * gain
            y = t * cosv + _rope_partner(t) * sinv
            if h < N_HEADS:
                q_ref[:, h * HEAD_DIM:(h + 1) * HEAD_DIM] = y.astype(BF16)
            else:
                k_ref[:, (h - N_HEADS) * HEAD_DIM:(h - N_HEADS + 1) * HEAD_DIM] = y.astype(BF16)
        v_ref[...] = p_ref[:, ATTN_W + KV_W:width].astype(BF16)

    vec = _full((1, HEAD_DIM))
    tab = pl.BlockSpec((tr, HEAD_DIM), lambda i: (i, 0))
    kv_spec = pl.BlockSpec((tr, KV_W), lambda i: (i + row0 // tr, 0))
    kv_shape = jax.ShapeDtypeStruct((total, KV_W), BF16)
    return pl.pallas_call(
        body, name=name, grid=(rows // tr,),
        in_specs=[pl.BlockSpec((tr, width), lambda i: (i, 0)), vec, vec, tab, tab] + ([] if kv_dest is None else [ANY, ANY]),
        out_specs=(pl.BlockSpec((tr, ATTN_W), lambda i: (i, 0)), kv_spec, kv_spec),
        out_shape=(jax.ShapeDtypeStruct((rows, ATTN_W), BF16), kv_shape, kv_shape),
        input_output_aliases={} if kv_dest is None else {5: 1, 6: 2},
        compiler_params=_params(("parallel",)),
    )(proj, q_g, k_g, cos, sin, *([] if kv_dest is None else kv_dest))


def _qk_bwd(dq, dk, proj, q_g, k_g, cos, sin, *, name, dk_row0=0):
    rows = proj.shape[0]
    tr = _tile(math.gcd(rows, dk_row0), 256, 16)
    width = ATTN_W + KV_W
    has_q = dq is not None

    def body(*refs):
        pos = 0
        dq_ref = None
        if has_q:
            dq_ref = refs[0]
            pos = 1
        dk_ref, p_ref, qg_ref, kg_ref, cos_ref, sin_ref, dp_ref, dqg_ref, dkg_ref = refs[pos:pos + 9]

        @pl.when(pl.program_id(0) == 0)
        def _():
            dqg_ref[...] = jnp.zeros_like(dqg_ref)
            dkg_ref[...] = jnp.zeros_like(dkg_ref)

        cosv, sinv = cos_ref[...], sin_ref[...]
        for h in range(N_HEADS + N_KV):
            cols = slice(h * HEAD_DIM, (h + 1) * HEAD_DIM)
            if h < N_HEADS and not has_q:
                dp_ref[:, cols] = jnp.zeros((tr, HEAD_DIM), dp_ref.dtype)
                continue
            if h < N_HEADS:
                dyv, gain, dgain_ref = dq_ref[:, cols], qg_ref[...], dqg_ref
            else:
                hk = h - N_HEADS
                dyv, gain, dgain_ref = dk_ref[:, hk * HEAD_DIM:(hk + 1) * HEAD_DIM], kg_ref[...], dkg_ref
            dyv = dyv.astype(F32)
            dt = dyv * cosv + _rope_partner(dyv * sinv)
            xv = p_ref[:, cols]
            rstd = lax.rsqrt(jnp.mean(xv * xv, axis=-1, keepdims=True) + EPS)
            nrm = xv * rstd
            dgain_ref[...] += jnp.sum(dt * nrm, axis=0, keepdims=True)
            dn = dt * gain
            dp_ref[:, cols] = (rstd * (dn - nrm * jnp.mean(dn * nrm, axis=-1, keepdims=True))).astype(dp_ref.dtype)

    vec = _full((1, HEAD_DIM))
    tab = pl.BlockSpec((tr, HEAD_DIM), lambda i: (i, 0))
    operands = ([dq] if has_q else []) + [dk, proj, q_g, k_g, cos, sin]
    specs = ([pl.BlockSpec((tr, ATTN_W), lambda i: (i, 0))] if has_q else []) + [
        pl.BlockSpec((tr, KV_W), lambda i: (i + dk_row0 // tr, 0)), pl.BlockSpec((tr, width), lambda i: (i, 0)), vec, vec, tab, tab]
    return pl.pallas_call(
        body, name=name, grid=(rows // tr,), in_specs=specs,
        out_specs=(pl.BlockSpec((tr, width), lambda i: (i, 0)), vec, vec),
        out_shape=(jax.ShapeDtypeStruct((rows, width), BF16), jax.ShapeDtypeStruct((1, HEAD_DIM), F32),
                   jax.ShapeDtypeStruct((1, HEAD_DIM), F32)),
        compiler_params=_params(("arbitrary",)),
    )(*operands)


def _attn_fwd(q, k, v):
    n_q, n_k = q.shape[0], k.shape[0]
    tq = _tile(n_q, 256, 16)
    gw = GROUP * HEAD_DIM
    scale = HEAD_DIM ** -0.5

    def body(q_ref, k_ref, v_ref, o_ref, lse_ref):
        kv, vv = k_ref[...], v_ref[...]
        for g in range(GROUP):
            cols = slice(g * HEAD_DIM, (g + 1) * HEAD_DIM)
            s = lax.dot_general(q_ref[:, cols], kv, (((1,), (1,)), ((), ())), preferred_element_type=F32) * (scale * LOG2E)
            m = jnp.max(s, axis=-1, keepdims=True)
            p = jnp.exp2(s - m)
            l = jnp.sum(p, axis=-1, keepdims=True)
            o = jnp.dot(p.astype(BF16), vv, preferred_element_type=F32) / l
            o_ref[:, cols] = o.astype(o_ref.dtype)
            lse_ref[:, g:g + 1] = m + jnp.log(l) * LOG2E

    return pl.pallas_call(
        body, name="attn_fwd", grid=(N_KV, n_q // tq),
        in_specs=[pl.BlockSpec((tq, gw), lambda h, i: (i, h)), pl.BlockSpec((n_k, HEAD_DIM), lambda h, i: (0, h)),
                  pl.BlockSpec((n_k, HEAD_DIM), lambda h, i: (0, h))],
        out_specs=(pl.BlockSpec((tq, gw), lambda h, i: (i, h)), pl.BlockSpec((None, tq, GROUP), lambda h, i: (h, i, 0))),
        out_shape=(jax.ShapeDtypeStruct((n_q, ATTN_W + D_RNN), BF16), jax.ShapeDtypeStruct((N_KV, n_q, GROUP), F32)),
        compiler_params=_params(("parallel", "parallel")),
    )(q, k, v)


def _attn_bwd(q, k, v, o, lse, do):
    n_q, n_k = q.shape[0], k.shape[0]
    tq = _tile(n_q, 256, 16)
    gw = GROUP * HEAD_DIM
    scale = HEAD_DIM ** -0.5

    def body(q_ref, k_ref, v_ref, o_ref, lse_ref, do_ref, dq_ref, dk_ref, dv_ref):
        @pl.when(pl.program_id(1) == 0)
        def _():
            dk_ref[...] = jnp.zeros_like(dk_ref)
            dv_ref[...] = jnp.zeros_like(dv_ref)

        kv, vv = k_ref[...], v_ref[...]
        for g in range(GROUP):
            cols = slice(g * HEAD_DIM, (g + 1) * HEAD_DIM)
            qg = q_ref[:, cols]
            dof = do_ref[:, cols].astype(F32)
            dog = dof.astype(BF16)
            s = lax.dot_general(qg, kv, (((1,), (1,)), ((), ())), preferred_element_type=F32) * (scale * LOG2E)
            p = jnp.exp2(s - lse_ref[:, g:g + 1])
            delta = jnp.sum(dof * o_ref[:, cols].astype(F32), axis=-1, keepdims=True)
            dp = lax.dot_general(dog, vv, (((1,), (1,)), ((), ())), preferred_element_type=F32)
            ds = (p * (dp - delta) * scale).astype(BF16)
            pb = p.astype(BF16)
            dq_ref[:, cols] = jnp.dot(ds, kv, preferred_element_type=F32)
            dk_ref[...] += lax.dot_general(ds, qg, (((0,), (0,)), ((), ())), preferred_element_type=F32)
            dv_ref[...] += lax.dot_general(pb, dog, (((0,), (0,)), ((), ())), preferred_element_type=F32)

    qspec = pl.BlockSpec((tq, gw), lambda h, i: (i, h))
    kspec = pl.BlockSpec((n_k, HEAD_DIM), lambda h, i: (0, h))
    return pl.pallas_call(
        body, name="attn_bwd", grid=(N_KV, n_q // tq),
        in_specs=[qspec, kspec, kspec, qspec, pl.BlockSpec((None, tq, GROUP), lambda h, i: (h, i, 0)), qspec],
        out_specs=(qspec, kspec, kspec),
        out_shape=(jax.ShapeDtypeStruct((n_q, ATTN_W), F32), jax.ShapeDtypeStruct((n_k, KV_W), F32),
                   jax.ShapeDtypeStruct((n_k, KV_W), F32)),
        compiler_params=_params(("parallel", "arbitrary")),
    )(q, k, v, o, lse, do)


CONV_COLS = 256
XR_COL0 = ATTN_W + 2 * KV_W


def _shift_rows(v, off):
    if off == 0:
        return v
    n = v.shape[0]
    rolled = pltpu.roll(v, (-off) % n, 0)
    t = lax.broadcasted_iota(jnp.int32, v.shape, 0)
    keep = (t + off >= 0) & (t + off < n)
    return jnp.where(keep, rolled, 0.0)


def _conv_fwd(proj_l, proj_c, w, b):
    n_lat, n_ctx = proj_l.shape[0], proj_c.shape[0]
    blk0 = XR_COL0 // CONV_COLS

    def body(xl_ref, xc_ref, w_ref, b_ref, y_ref):
        for x_ref, rows in ((xc_ref, slice(0, n_ctx)), (xl_ref, slice(n_ctx, n_ctx + n_lat))):
            xv = x_ref[...]
            y = b_ref[...] + jnp.zeros_like(xv)
            for j in range(CONV_W):
                y = y + _shift_rows(xv, j - CONV_W // 2) * w_ref[j:j + 1, :]
            y_ref[rows, :] = y

    return pl.pallas_call(
        body, name="conv_fwd", grid=(D_RNN // CONV_COLS,),
        in_specs=[pl.BlockSpec((n_lat, CONV_COLS), lambda i: (0, blk0 + i)), pl.BlockSpec((n_ctx, CONV_COLS), lambda i: (0, blk0 + i)),
                  pl.BlockSpec((CONV_W, CONV_COLS), lambda i: (0, i)), pl.BlockSpec((1, CONV_COLS), lambda i: (0, i))],
        out_specs=pl.BlockSpec((n_ctx + n_lat, CONV_COLS), lambda i: (0, i)),
        out_shape=jax.ShapeDtypeStruct((n_ctx + n_lat, D_RNN), F32), compiler_params=_params(("parallel",)),
    )(proj_l, proj_c, w, b)


def _conv_bwd(d1, d2, proj_l, proj_c, w):
    n_lat, n_ctx = proj_l.shape[0], proj_c.shape[0]
    blk0 = XR_COL0 // CONV_COLS

    def body(d1_ref, d2_ref, xl_ref, xc_ref, w_ref, dxl_ref, dxc_ref, dw_ref, db_ref):
        dw = [0.0] * CONV_W
        db = 0.0
        for x_ref, dx_ref, rows in ((xc_ref, dxc_ref, slice(0, n_ctx)), (xl_ref, dxl_ref, slice(n_ctx, n_ctx + n_lat))):
            dv = d1_ref[rows, :] + d2_ref[rows, :]
            xv = x_ref[...]
            dx = jnp.zeros_like(dv)
            for j in range(CONV_W):
                off = j - CONV_W // 2
                dx = dx + _shift_rows(dv, -off) * w_ref[j:j + 1, :]
                dw[j] = dw[j] + jnp.sum(dv * _shift_rows(xv, off), axis=0, keepdims=True)
            dx_ref[...] = dx.astype(dx_ref.dtype)
            db = db + jnp.sum(dv, axis=0, keepdims=True)
        for j in range(CONV_W):
            dw_ref[j:j + 1, :] = dw[j]
        db_ref[...] = db

    both = pl.BlockSpec((n_ctx + n_lat, CONV_COLS), lambda i: (0, i))
    return pl.pallas_call(
        body, name="conv_bwd", grid=(D_RNN // CONV_COLS,),
        in_specs=[both, both, pl.BlockSpec((n_lat, CONV_COLS), lambda i: (0, blk0 + i)),
                  pl.BlockSpec((n_ctx, CONV_COLS), lambda i: (0, blk0 + i)), pl.BlockSpec((CONV_W, CONV_COLS), lambda i: (0, i))],
        out_specs=(pl.BlockSpec((n_lat, CONV_COLS), lambda i: (0, i)), pl.BlockSpec((n_ctx, CONV_COLS), lambda i: (0, i)),
                   pl.BlockSpec((CONV_W, CONV_COLS), lambda i: (0, i)), pl.BlockSpec((1, CONV_COLS), lambda i: (0, i))),
        out_shape=(jax.ShapeDtypeStruct((n_lat, D_RNN), BF16), jax.ShapeDtypeStruct((n_ctx, D_RNN), BF16),
                   jax.ShapeDtypeStruct((CONV_W, D_RNN), F32), jax.ShapeDtypeStruct((1, D_RNN), F32)),
        compiler_params=_params(("parallel",)),
    )(d1, d2, proj_l, proj_c, w)


RNN_TB = 256
SCAN_ROWS = 8


def _sigmoid(z):
    return 1.0 / (1.0 + jnp.exp(-z))


def _softplus(z):
    return jnp.maximum(z, 0.0) + jnp.log(1.0 + jnp.exp(-jnp.abs(z)))


def _one_minus_exp(y):
    series = -y * (1.0 + y * (0.5 + y * (1.0 / 6.0 + y * (1.0 / 24.0))))
    return jnp.where(y > -0.03, series, 1.0 - jnp.exp(y))


def _rglru_gates(xv, wa_ref, ba_ref, wx_ref, bx_ref, lam_ref):
    xb = xv.astype(BF16)
    zr = jnp.concatenate([jnp.dot(xb[:, n * RNN_BW:(n + 1) * RNN_BW], wa_ref[n].astype(BF16),
                                  preferred_element_type=F32) for n in range(RNN_BLOCKS)], axis=-1) + ba_ref[...]
    zi = jnp.concatenate([jnp.dot(xb[:, n * RNN_BW:(n + 1) * RNN_BW], wx_ref[n].astype(BF16),
                                  preferred_element_type=F32) for n in range(RNN_BLOCKS)], axis=-1) + bx_ref[...]
    r = _sigmoid(zr)
    gi = _sigmoid(zi)
    sp = _softplus(-lam_ref[...])
    log_a = -RG_C * r * sp
    a = jnp.exp(log_a)
    s = jnp.sqrt(_one_minus_exp(2.0 * log_a))
    return r, gi, sp, a, s


def _scan_rows(n_rows, reverse, step_fn, carry):
    groups = n_rows // SCAN_ROWS

    def trip(gidx, carry):
        gi = (groups - 1 - gidx) if reverse else gidx
        base = pl.multiple_of(gi * SCAN_ROWS, SCAN_ROWS)
        return step_fn(base, carry)

    return lax.fori_loop(0, groups, trip, carry)


def _scan_block_order(nb, nb_c, reverse, adjoint):
    if not reverse:
        return (lambda i: nb - 1 - i) if adjoint else (lambda i: i)
    if adjoint:
        return lambda i: jnp.where(i < nb - nb_c, nb_c + i, i - (nb - nb_c))
    return lambda i: jnp.where(i < nb_c, nb_c - 1 - i, nb + nb_c - 1 - i)


def _rglru_fwd(xs, wa, ba, wx, bx, lam, *, reverse, n_ctx, name):
    rows = xs.shape[0]
    tb = _tile(math.gcd(rows, n_ctx), RNN_TB, SCAN_ROWS)
    nb = rows // tb
    block_of = _scan_block_order(nb, n_ctx // tb, reverse, False)
    order = lambda i: (block_of(i), 0)

    def body(x_ref, wa_ref, ba_ref, wx_ref, bx_ref, lam_ref, h_ref, hp_ref, a_s, b_s, state):
        @pl.when(pl.program_id(0) == 0)
        def _():
            state[...] = jnp.zeros_like(state)

        xv = x_ref[...]
        _, gi, _, a, s = _rglru_gates(xv, wa_ref, ba_ref, wx_ref, bx_ref, lam_ref)
        a_s[...] = a
        b_s[...] = s * (gi * xv)

        def group(base, h):
            av = a_s[pl.ds(base, SCAN_ROWS), :]
            bv = b_s[pl.ds(base, SCAN_ROWS), :]
            outs, prevs = [None] * SCAN_ROWS, [None] * SCAN_ROWS
            for k in range(SCAN_ROWS):
                r_ = SCAN_ROWS - 1 - k if reverse else k
                prevs[r_] = h
                h = av[r_:r_ + 1, :] * h + bv[r_:r_ + 1, :]
                outs[r_] = h
            h_ref[pl.ds(base, SCAN_ROWS), :] = jnp.concatenate(outs, axis=0)
            hp_ref[pl.ds(base, SCAN_ROWS), :] = jnp.concatenate(prevs, axis=0)
            return h

        state[0:1, :] = _scan_rows(tb, reverse, group, state[0:1, :])

    blk = pl.BlockSpec((tb, D_RNN), order)
    wspec = _full((RNN_BLOCKS, RNN_BW, RNN_BW))
    vec = _full((1, D_RNN))
    return pl.pallas_call(
        body, name=name, grid=(nb,), in_specs=[blk, wspec, vec, wspec, vec, vec], out_specs=(blk, blk),
        out_shape=(jax.ShapeDtypeStruct((rows, D_RNN), F32), jax.ShapeDtypeStruct((rows, D_RNN), F32)),
        scratch_shapes=[pltpu.VMEM((tb, D_RNN), F32), pltpu.VMEM((tb, D_RNN), F32), pltpu.VMEM((SCAN_ROWS, D_RNN), F32)],
        compiler_params=_params(("arbitrary",)),
    )(xs, wa, ba, wx, bx, lam)


def _rglru_bwd(xs, h_prev, dh, wa, ba, wx, bx, lam, *, reverse, n_ctx, name):
    rows = xs.shape[0]
    tb = _tile(math.gcd(rows, n_ctx), RNN_TB, SCAN_ROWS)
    nb, nb_c = rows // tb, n_ctx // tb
    back = not reverse
    block_of = _scan_block_order(nb, nb_c, reverse, True)
    order = lambda i: (block_of(i), 0)

    def body(x_ref, hp_ref, dh_ref, wa_ref, ba_ref, wx_ref, bx_ref, lam_ref,
             dx_ref, dwa_ref, dba_ref, dwx_ref, dbx_ref, dlam_ref, a_s, g_s, state):
        @pl.when(pl.program_id(0) == 0)
        def _():
            state[...] = jnp.zeros_like(state)
            dwa_ref[...] = jnp.zeros_like(dwa_ref)
            dwx_ref[...] = jnp.zeros_like(dwx_ref)
            dba_ref[...] = jnp.zeros_like(dba_ref)
            dbx_ref[...] = jnp.zeros_like(dbx_ref)
            dlam_ref[...] = jnp.zeros_like(dlam_ref)

        xv = x_ref[...]
        r, gi, sp, a, s = _rglru_gates(xv, wa_ref, ba_ref, wx_ref, bx_ref, lam_ref)
        a_s[...] = a

        is_latent = block_of(pl.program_id(0)) >= nb_c

        def group(base, carry):
            av = a_s[pl.ds(base, SCAN_ROWS), :]
            dv = jnp.where(is_latent, dh_ref[pl.ds(base, SCAN_ROWS), :], 0.0)
            outs = [None] * SCAN_ROWS
            for k in range(SCAN_ROWS):
                r_ = SCAN_ROWS - 1 - k if back else k
                gt = dv[r_:r_ + 1, :] + carry
                outs[r_] = gt
                carry = av[r_:r_ + 1, :] * gt
            g_s[pl.ds(base, SCAN_ROWS), :] = jnp.concatenate(outs, axis=0)
            return carry

        state[0:1, :] = _scan_rows(tb, back, group, state[0:1, :])

        gv = g_s[...]
        d_a = gv * hp_ref[...]
        d_s = gv * (gi * xv)
        d_gi = gv * (s * xv)
        dx = gv * (s * gi)
        d_log_a = d_a * a - d_s * (a * a) / s
        d_r = d_log_a * (-RG_C * sp)
        lamv = lam_ref[...]
        d_sp = jnp.sum(d_log_a * (-RG_C * r), axis=0, keepdims=True)
        dlam_ref[...] += d_sp * (-_sigmoid(-lamv))
        d_zr = d_r * r * (1.0 - r)
        d_zi = d_gi * gi * (1.0 - gi)
        dba_ref[...] += jnp.sum(d_zr, axis=0, keepdims=True)
        dbx_ref[...] += jnp.sum(d_zi, axis=0, keepdims=True)
        xb = xv.astype(BF16)
        zrb, zib = d_zr.astype(BF16), d_zi.astype(BF16)
        parts = []
        for n in range(RNN_BLOCKS):
            cols = slice(n * RNN_BW, (n + 1) * RNN_BW)
            dwa_ref[n] += lax.dot_general(xb[:, cols], zrb[:, cols], (((0,), (0,)), ((), ())), preferred_element_type=F32)
            dwx_ref[n] += lax.dot_general(xb[:, cols], zib[:, cols], (((0,), (0,)), ((), ())), preferred_element_type=F32)
            parts.append(
                lax.dot_general(zrb[:, cols], wa_ref[n].astype(BF16), (((1,), (1,)), ((), ())), preferred_element_type=F32)
                + lax.dot_general(zib[:, cols], wx_ref[n].astype(BF16), (((1,), (1,)), ((), ())), preferred_element_type=F32))
        dx_ref[...] = dx + jnp.concatenate(parts, axis=-1)

    blk = pl.BlockSpec((tb, D_RNN), order)
    wspec = _full((RNN_BLOCKS, RNN_BW, RNN_BW))
    vec = _full((1, D_RNN))
    wshape = jax.ShapeDtypeStruct((RNN_BLOCKS, RNN_BW, RNN_BW), F32)
    vshape = jax.ShapeDtypeStruct((1, D_RNN), F32)
    dh_blk = pl.BlockSpec((tb, D_RNN), lambda i: (jnp.maximum(block_of(i) - nb_c, 0), 0))
    return pl.pallas_call(
        body, name=name, grid=(nb,), in_specs=[blk, blk, dh_blk, wspec, vec, wspec, vec, vec],
        out_specs=(blk, wspec, vec, wspec, vec, vec),
        out_shape=(jax.ShapeDtypeStruct((rows, D_RNN), F32), wshape, vshape, wshape, vshape, vshape),
        scratch_shapes=[pltpu.VMEM((tb, D_RNN), F32), pltpu.VMEM((tb, D_RNN), F32), pltpu.VMEM((SCAN_ROWS, D_RNN), F32)],
        compiler_params=_params(("arbitrary",)),
    )(xs, h_prev, dh, wa, ba, wx, bx, lam)


def _assemble_d_proj(dp_qk_l, dp_qk_c, dv_all, d_xr_l, d_xr_c, d_gate):
    n_lat, n_ctx = dp_qk_l.shape[0], dp_qk_c.shape[0]
    tr = _tile(math.gcd(n_lat, n_ctx), 256, 16)
    nb_l, nb_c = n_lat // tr, n_ctx // tr
    w_qk = ATTN_W + KV_W

    def body(ql_ref, qc_ref, dv_ref, xl_ref, xc_ref, g_ref, o_ref):
        i = pl.program_id(0)
        o_ref[:, w_qk:XR_COL0] = dv_ref[...].astype(o_ref.dtype)

        @pl.when(i < nb_l)
        def _():
            o_ref[:, :w_qk] = ql_ref[...]
            o_ref[:, XR_COL0:GATE_COL0] = xl_ref[...]
            o_ref[:, GATE_COL0:] = g_ref[...]

        @pl.when(i >= nb_l)
        def _():
            o_ref[:, :w_qk] = qc_ref[...]
            o_ref[:, XR_COL0:GATE_COL0] = xc_ref[...]
            o_ref[:, GATE_COL0:] = jnp.zeros((tr, D_RNN), o_ref.dtype)

    lat = lambda i: (jnp.minimum(i, nb_l - 1), 0)
    ctx = lambda i: (jnp.maximum(i - nb_l, 0), 0)
    return pl.pallas_call(
        body, name="assemble_d_proj", grid=(nb_l + nb_c,),
        in_specs=[pl.BlockSpec((tr, w_qk), lat), pl.BlockSpec((tr, w_qk), ctx),
                  pl.BlockSpec((tr, KV_W), lambda i: (jnp.where(i < nb_l, i + nb_c, i - nb_l), 0)),
                  pl.BlockSpec((tr, D_RNN), lat), pl.BlockSpec((tr, D_RNN), ctx), pl.BlockSpec((tr, D_RNN), lat)],
        out_specs=pl.BlockSpec((tr, GATE_COL0 + D_RNN), lambda i: (i, 0)),
        out_shape=jax.ShapeDtypeStruct((n_lat + n_ctx, GATE_COL0 + D_RNN), BF16),
        compiler_params=_params(("parallel",)),
    )(dp_qk_l, dp_qk_c, dv_all, d_xr_l, d_xr_c, d_gate)


def _gelu(z):
    return 0.5 * z * (1.0 + jnp.tanh(GELU_C * (z + 0.044715 * z * z * z)))


def _gelu_grad(z):
    t = jnp.tanh(GELU_C * (z + 0.044715 * z * z * z))
    return 0.5 * (1.0 + t) + 0.5 * z * (1.0 - t * t) * (GELU_C * (1.0 + 3.0 * 0.044715 * z * z))


GATE_COL0 = XR_COL0 + D_RNN


RNN_OUT_COLS = 512


def _rnn_out_specs(rows, hf_off, hb_off):
    tr = _tile(rows, 256, 16)
    assert hf_off % tr == 0 and hb_off % tr == 0 and GATE_COL0 % RNN_OUT_COLS == 0
    fo, bo, go = hf_off // tr, hb_off // tr, GATE_COL0 // RNN_OUT_COLS
    hf_spec = pl.BlockSpec((tr, RNN_OUT_COLS), lambda i, j: (i + fo, j))
    hb_spec = pl.BlockSpec((tr, RNN_OUT_COLS), lambda i, j: (i + bo, j))
    gate_spec = pl.BlockSpec((tr, RNN_OUT_COLS), lambda i, j: (i, j + go))
    out_spec = pl.BlockSpec((tr, RNN_OUT_COLS), lambda i, j: (i, j))
    return (rows // tr, D_RNN // RNN_OUT_COLS), hf_spec, hb_spec, gate_spec, out_spec


def _rnn_out_fwd(hf, hb, proj, hf_off, hb_off, cat):
    rows = proj.shape[0]
    grid, hf_spec, hb_spec, gate_spec, out_spec = _rnn_out_specs(rows, hf_off, hb_off)
    tr, col0 = out_spec.block_shape[0], ATTN_W // RNN_OUT_COLS

    def body(hf_ref, hb_ref, g_ref, _, o_ref):
        o_ref[...] = ((hf_ref[...] + hb_ref[...]) * _gelu(g_ref[...])).astype(o_ref.dtype)

    return pl.pallas_call(
        body, name="rnn_out_fwd", grid=grid, in_specs=[hf_spec, hb_spec, gate_spec, ANY],
        out_specs=pl.BlockSpec((tr, RNN_OUT_COLS), lambda i, j: (i, j + col0)),
        out_shape=jax.ShapeDtypeStruct(cat.shape, cat.dtype), input_output_aliases={3: 0},
        compiler_params=_params(("parallel", "parallel")),
    )(hf, hb, proj, cat)


def _rnn_out_bwd(d_cat, hf, hb, proj, hf_off, hb_off):
    rows = proj.shape[0]
    grid, hf_spec, hb_spec, gate_spec, out_spec = _rnn_out_specs(rows, hf_off, hb_off)
    do = ATTN_W // RNN_OUT_COLS

    def body(d_ref, hf_ref, hb_ref, g_ref, dh_ref, dg_ref):
        dv, gv = d_ref[...].astype(F32), g_ref[...]
        dh_ref[...] = dv * _gelu(gv)
        dg_ref[...] = (dv * (hf_ref[...] + hb_ref[...]) * _gelu_grad(gv)).astype(dg_ref.dtype)

    tr = out_spec.block_shape[0]
    return pl.pallas_call(
        body, name="rnn_out_bwd", grid=grid,
        in_specs=[pl.BlockSpec((tr, RNN_OUT_COLS), lambda i, j: (i, j + do)), hf_spec, hb_spec, gate_spec],
        out_specs=(out_spec, out_spec),
        out_shape=(jax.ShapeDtypeStruct((rows, D_RNN), F32), jax.ShapeDtypeStruct((rows, D_RNN), BF16)),
        compiler_params=_params(("parallel", "parallel")),
    )(d_cat, hf, hb, proj)


def _gmlp_parts(z_ref, vg_ref, vb_ref, d_gm):
    zu, zv = z_ref[:, :d_gm], z_ref[:, d_gm:]
    u = _gelu(zu)
    v = _gelu(zv)
    mu = jnp.mean(v, axis=-1, keepdims=True)
    vc = v - mu
    rstd = lax.rsqrt(jnp.mean(vc * vc, axis=-1, keepdims=True) + EPS)
    vhat = vc * rstd
    vn = vhat * vg_ref[...] + vb_ref[...]
    return zu, zv, u, vhat, rstd, vn


def _gmlp_fwd(z, v_g, v_b, w_sp, b_sp_t):
    rows, d_gm = z.shape[0], z.shape[1] // 2
    tr = _tile(rows, 256, CHUNK)
    gwid = d_gm // GM_GROUPS

    def body(z_ref, vg_ref, vb_ref, w_ref, b_ref, o_ref):
        _, _, u, _, _, vn = _gmlp_parts(z_ref, vg_ref, vb_ref, d_gm)
        vnb = vn.astype(BF16)
        for g in range(GM_GROUPS):
            wg = w_ref[g].astype(BF16)
            for c in range(tr // CHUNK):
                rs, cs = slice(c * CHUNK, (c + 1) * CHUNK), slice(g * gwid, (g + 1) * gwid)
                sv = jnp.dot(wg, vnb[rs, cs], preferred_element_type=F32) + b_ref[:, g:g + 1]
                o_ref[rs, cs] = (u[rs, cs] * sv).astype(o_ref.dtype)

    return pl.pallas_call(
        body, name="gmlp_fwd", grid=(rows // tr,),
        in_specs=[pl.BlockSpec((tr, 2 * d_gm), lambda i: (i, 0)), _full((1, d_gm)), _full((1, d_gm)),
                  _full(w_sp.shape), _full(b_sp_t.shape)],
        out_specs=pl.BlockSpec((tr, d_gm), lambda i: (i, 0)),
        out_shape=jax.ShapeDtypeStruct((rows, d_gm), BF16), compiler_params=_params(("parallel",)),
    )(z, v_g, v_b, w_sp, b_sp_t)


def _gmlp_bwd(z, dgate, v_g, v_b, w_sp, b_sp_t):
    rows, d_gm = z.shape[0], z.shape[1] // 2
    tr = _tile(rows, 256, CHUNK)
    gwid = d_gm // GM_GROUPS

    def body(z_ref, dg_ref, vg_ref, vb_ref, w_ref, b_ref, dz_ref, dbin_ref, dvg_ref, dvb_ref, dw_ref, dbs_ref, dvn_s):
        @pl.when(pl.program_id(0) == 0)
        def _():
            dbin_ref[...] = jnp.zeros_like(dbin_ref)
            dvg_ref[...] = jnp.zeros_like(dvg_ref)
            dvb_ref[...] = jnp.zeros_like(dvb_ref)
            dw_ref[...] = jnp.zeros_like(dw_ref)
            dbs_ref[...] = jnp.zeros_like(dbs_ref)

        zu, zv, u, vhat, rstd, vn = _gmlp_parts(z_ref, vg_ref, vb_ref, d_gm)
        vnb = vn.astype(BF16)
        dgv = dg_ref[...].astype(F32)
        dsv = dgv * u
        dsvb = dsv.astype(BF16)
        for g in range(GM_GROUPS):
            wg = w_ref[g].astype(BF16)
            cs = slice(g * gwid, (g + 1) * gwid)
            for c in range(tr // CHUNK):
                rs = slice(c * CHUNK, (c + 1) * CHUNK)
                sv = jnp.dot(wg, vnb[rs, cs], preferred_element_type=F32) + b_ref[:, g:g + 1]
                dz_ref[rs, cs] = (dgv[rs, cs] * sv * _gelu_grad(zu[rs, cs])).astype(dz_ref.dtype)
                dw_ref[g] += lax.dot_general(dsvb[rs, cs], vnb[rs, cs], (((1,), (1,)), ((), ())),
                                             preferred_element_type=F32)
                dbs_ref[:, g:g + 1] += jnp.sum(dsv[rs, cs], axis=-1, keepdims=True)
                dvn_s[rs, cs] = lax.dot_general(wg, dsvb[rs, cs], (((0,), (0,)), ((), ())), preferred_element_type=F32)
        dvn = dvn_s[...]
        dvg_ref[...] += jnp.sum(dvn * vhat, axis=0, keepdims=True)
        dvb_ref[...] += jnp.sum(dvn, axis=0, keepdims=True)
        dvh = dvn * vg_ref[...]
        dv = rstd * (dvh - jnp.mean(dvh, axis=-1, keepdims=True) - vhat * jnp.mean(dvh * vhat, axis=-1, keepdims=True))
        dzv = dv * _gelu_grad(zv)
        dz_ref[:, d_gm:] = dzv.astype(dz_ref.dtype)
        dbin_ref[:, d_gm:] += jnp.sum(dzv, axis=0, keepdims=True)
        dbin_ref[:, :d_gm] += jnp.sum(dz_ref[:, :d_gm].astype(F32), axis=0, keepdims=True)

    return pl.pallas_call(
        body, name="gmlp_bwd", grid=(rows // tr,),
        in_specs=[pl.BlockSpec((tr, 2 * d_gm), lambda i: (i, 0)), pl.BlockSpec((tr, d_gm), lambda i: (i, 0)),
                  _full((1, d_gm)), _full((1, d_gm)), _full(w_sp.shape), _full(b_sp_t.shape)],
        out_specs=(pl.BlockSpec((tr, 2 * d_gm), lambda i: (i, 0)), _full((1, 2 * d_gm)), _full((1, d_gm)),
                   _full((1, d_gm)), _full(w_sp.shape), _full(b_sp_t.shape)),
        out_shape=(jax.ShapeDtypeStruct((rows, 2 * d_gm), BF16), jax.ShapeDtypeStruct((1, 2 * d_gm), F32),
                   jax.ShapeDtypeStruct((1, d_gm), F32), jax.ShapeDtypeStruct((1, d_gm), F32),
                   jax.ShapeDtypeStruct(w_sp.shape, F32), jax.ShapeDtypeStruct(b_sp_t.shape, F32)),
        scratch_shapes=[pltpu.VMEM((tr, d_gm), F32)],
        compiler_params=_params(("arbitrary",)),
    )(z, dgate, v_g, v_b, w_sp, b_sp_t)


def _adamw_math(w, g, m, v):
    m = ADAM_B1 * m + (1.0 - ADAM_B1) * g
    v = ADAM_B2 * v + (1.0 - ADAM_B2) * (g * g)
    m_hat = m / (1.0 - ADAM_B1 ** ADAM_STEP)
    v_hat = v / (1.0 - ADAM_B2 ** ADAM_STEP)
    delta = -ADAM_LR * (m_hat / (jnp.sqrt(v_hat) + ADAM_EPS) + ADAM_WD * w)
    return delta, m, v


def _adamw(w, g, m, v, name):
    shape = w.shape
    outs = _rowwise(_adamw_math, (F32, F32, F32), _as2d(w), _as2d(g), _as2d(m), _as2d(v), name=name)
    return (g.reshape(shape),) + tuple(o.reshape(shape) for o in outs)


PACK_COLS = 1024


def _pack(arrays, dtype=F32):
    flat = jnp.concatenate([a.reshape(-1).astype(dtype) for a in arrays])
    pad = (-flat.size) % (16 * PACK_COLS)
    return jnp.pad(flat, (0, pad)).reshape(-1, PACK_COLS)


def _into_slot(pack, dev, name):
    rows, cols = pack.shape
    tr = _rows_tile(rows, cols, budget=512 * 1024)

    def body(dev_ref, p_ref, o_ref):
        o_ref[...] = p_ref[...]

    return pl.pallas_call(
        body, name=name, out_shape=jax.ShapeDtypeStruct((N_DEV, rows, cols), pack.dtype),
        grid_spec=pltpu.PrefetchScalarGridSpec(
            num_scalar_prefetch=1, grid=(rows // tr,), in_specs=[pl.BlockSpec((tr, cols), lambda i, dv: (i, 0))],
            out_specs=pl.BlockSpec((None, tr, cols), lambda i, dv: (dv[0], i, 0))),
        compiler_params=_params(("parallel",)),
    )(dev, pack)


def _unpack(flat, shapes):
    out, pos = [], 0
    for shp in shapes:
        n = math.prod(shp)
        out.append(flat[pos:pos + n].reshape(shp))
        pos += n
    return out


def _unpack_devices(packed8, shapes):
    flat8 = packed8.reshape(N_DEV, -1)
    out, pos = [], 0
    for shp in shapes:
        n = math.prod(shp)
        out.append(flat8[:, pos:pos + n].reshape((N_DEV,) + tuple(shp)))
        pos += n
    return out


def _sum_devices(g8):
    _, rows, cols = g8.shape
    tr = _rows_tile(rows, cols, budget=256 * 1024)

    def body(g_ref, o_ref):
        acc = g_ref[0].astype(F32)
        for d in range(1, N_DEV):
            acc = acc + g_ref[d].astype(F32)
        o_ref[...] = acc

    return pl.pallas_call(
        body, name="sum_devices", grid=(rows // tr,), in_specs=[pl.BlockSpec((N_DEV, tr, cols), lambda i: (0, i, 0))],
        out_specs=pl.BlockSpec((tr, cols), lambda i: (i, 0)), out_shape=jax.ShapeDtypeStruct((rows, cols), F32),
        compiler_params=_params(("parallel",)),
    )(g8)


def _place():
    return lax.axis_index("x"), lax.axis_index("y"), lax.axis_index("c")


def _other_chips(x, y):
    return [(1 - x, y), (x, 1 - y), (1 - x, 1 - y)]


def _remote(src, dst, send_sem, recv_sem, to):
    return pltpu.make_async_remote_copy(src_ref=src, dst_ref=dst, send_sem=send_sem, recv_sem=recv_sem, device_id=to,
                                        device_id_type=MESH)


def _comm_call(body, name, operands, out_shapes, n_remote, n_local, aliases=None):
    return pl.pallas_call(
        body, name=name, out_shape=tuple(out_shapes), in_specs=[ANY] * len(operands), out_specs=tuple(ANY for _ in out_shapes),
        scratch_shapes=[pltpu.SemaphoreType.DMA((n_remote,)), pltpu.SemaphoreType.DMA((n_remote,)),
                        pltpu.SemaphoreType.DMA((max(n_local, 1),))],
        input_output_aliases=aliases or {},
    )(*operands)


def _in_place(arrays):
    return [jax.ShapeDtypeStruct(a.shape, a.dtype) for a in arrays], {i: i for i in range(len(arrays))}


def _allgather8(arrs, name):
    n = len(arrs)

    def body(*refs):
        ins, outs = refs[:n], refs[n:2 * n]
        send, recv, lsem = refs[2 * n:]
        x, y, c = _place()
        me, sib = (x, y, c), (x, y, 1 - c)
        chips = _other_chips(x, y)

        def slot(t, px, py, pc):
            return outs[t].at[4 * px + 2 * py + pc]

        def cp(t, k, block, to, from_input=False):
            src = ins[t] if from_input else slot(t, *block)
            return _remote(src, slot(t, *block), send.at[7 * t + k], recv.at[7 * t + k], to)

        mine = [pltpu.make_async_copy(ins[t], slot(t, *me), lsem.at[t]) for t in range(n)]
        for cpy in mine:
            cpy.start()
        first = []
        for t in range(n):
            first.append(cp(t, 0, me, sib, True))
            first += [cp(t, 1 + j, me, (*chip, c), True) for j, chip in enumerate(chips)]
        for cpy in first:
            cpy.start()
        passed = []
        for t in range(n):
            for j, chip in enumerate(chips):
                cp(t, 1 + j, (*chip, c), me).wait_recv()
                fwd = cp(t, 4 + j, (*chip, c), sib)
                fwd.start()
                passed.append(fwd)
        for t in range(n):
            cp(t, 0, sib, me).wait_recv()
            for j, chip in enumerate(chips):
                cp(t, 4 + j, (*chip, 1 - c), me).wait_recv()
        for cpy in first + passed:
            cpy.wait_send()
        for cpy in mine:
            cpy.wait()

    outs = _comm_call(body, name, arrs, [jax.ShapeDtypeStruct((N_DEV,) + a.shape, a.dtype) for a in arrs], 7 * n, n)
    return list(outs)


def _gather_weights(bufs):
    n_u = len(bufs)

    def body(*refs):
        bufs_ = refs[n_u:2 * n_u]
        send, recv, _ = refs[2 * n_u:]
        x, y, c = _place()
        me, sib, q = (x, y, c), (x, y, 1 - c), 2 * x + y
        chips = _other_chips(x, y)
        sent = []
        for u in range(n_u):
            half = bufs_[u].shape[1] // 2
            mine = bufs_[u].at[q, pl.ds(c * half, half)]
            for j, chip in enumerate(chips):
                cpy = _remote(mine, mine, send.at[6 * u + j], recv.at[6 * u + j], (*chip, c))
                cpy.start()
                sent.append(cpy)
        for u in range(n_u):
            half = bufs_[u].shape[1] // 2
            for j, chip in enumerate(chips):
                landed = bufs_[u].at[2 * chip[0] + chip[1], pl.ds(c * half, half)]
                _remote(landed, landed, send.at[6 * u + j], recv.at[6 * u + j], me).wait_recv()
                cpy = _remote(landed, landed, send.at[6 * u + 3 + j], recv.at[6 * u + 3 + j], sib)
                cpy.start()
                sent.append(cpy)
        for u in range(n_u):
            half = bufs_[u].shape[1] // 2
            for j, chip in enumerate(chips):
                landed = bufs_[u].at[2 * chip[0] + chip[1], pl.ds((1 - c) * half, half)]
                _remote(landed, landed, send.at[6 * u + 3 + j], recv.at[6 * u + 3 + j], me).wait_recv()
        for cpy in sent:
            cpy.wait_send()

    shapes, aliases = _in_place(bufs)
    return list(_comm_call(body, "gather_weights", bufs, shapes, 6 * n_u, 0, aliases))


def _exchange_halves(grads):
    n = len(grads)

    def body(*refs):
        ins, outs = refs[:n], refs[n:2 * n]
        send, recv, _ = refs[2 * n:]
        x, y, c = _place()
        sib = (x, y, 1 - c)
        sent = []
        for k in range(n):
            half = ins[k].shape[1] // 2
            cpy = _remote(ins[k].at[pl.ds(0, N_CHIPS), pl.ds((1 - c) * half, half)], outs[k], send.at[k], recv.at[k], sib)
            cpy.start()
            sent.append(cpy)
        for cpy in sent:
            cpy.wait()

    shapes = [jax.ShapeDtypeStruct((N_CHIPS, g.shape[1] // 2, g.shape[2]), g.dtype) for g in grads]
    return list(_comm_call(body, "exchange_halves", grads, shapes, n, 0))


def _chips_all_to_all(sums):
    n = len(sums)

    def body(*refs):
        ins, outs = refs[:n], refs[n:2 * n]
        send, recv, _ = refs[2 * n:]
        x, y, c = _place()
        sent = []
        for k in range(n):
            for j, chip in enumerate(_other_chips(x, y)):
                cpy = _remote(ins[k].at[2 * chip[0] + chip[1]], outs[k].at[j], send.at[3 * k + j], recv.at[3 * k + j], (*chip, c))
                cpy.start()
                sent.append(cpy)
        for cpy in sent:
            cpy.wait()

    shapes = [jax.ShapeDtypeStruct((N_CHIPS - 1,) + s.shape[1:], s.dtype) for s in sums]
    return list(_comm_call(body, "chips_all_to_all", sums, shapes, 3 * n, 0))


def _join_halves(bufs):
    n = len(bufs)
    units = [(k, layer) for k in range(n) for layer in range(bufs[k].shape[0])]

    def body(*refs):
        bufs_ = refs[n:2 * n]
        send, recv, _ = refs[2 * n:]
        x, y, c = _place()
        sent = []
        for u, (k, layer) in enumerate(units):
            half = bufs_[k].shape[1] // 2
            mine = bufs_[k].at[layer, pl.ds(c * half, half)]
            cpy = _remote(mine, mine, send.at[u], recv.at[u], (x, y, 1 - c))
            cpy.start()
            sent.append(cpy)
        for u, (k, layer) in enumerate(units):
            half = bufs_[k].shape[1] // 2
            theirs = bufs_[k].at[layer, pl.ds((1 - c) * half, half)]
            _remote(theirs, theirs, send.at[u], recv.at[u], (x, y, c)).wait_recv()
        for cpy in sent:
            cpy.wait_send()

    shapes, aliases = _in_place(bufs)
    return list(_comm_call(body, "join_halves", bufs, shapes, len(units), 0, aliases))


def _add_halves(grad, other, place):
    _, rows, cols = grad.shape
    half = rows // 2
    tr = _rows_tile(half, cols, itemsize=2, budget=1024 * 1024)
    per_half = half // tr

    def body(place_ref, g_ref, o_ref, s_ref):
        s_ref[...] = (g_ref[...].astype(F32) + o_ref[...].astype(F32)).astype(s_ref.dtype)

    return pl.pallas_call(
        body, name="add_halves", out_shape=jax.ShapeDtypeStruct((N_CHIPS, half, cols), grad.dtype),
        grid_spec=pltpu.PrefetchScalarGridSpec(
            num_scalar_prefetch=1, grid=(N_CHIPS, per_half),
            in_specs=[pl.BlockSpec((None, tr, cols), lambda k, i, pr: (k, pr[1] * per_half + i, 0)),
                      pl.BlockSpec((None, tr, cols), lambda k, i, pr: (k, i, 0))],
            out_specs=pl.BlockSpec((None, tr, cols), lambda k, i, pr: (k, i, 0))),
        compiler_params=_params(("parallel", "parallel")),
    )(place, grad, other)


def _add_chips(sums, others, place, dest, layer, n_layers):
    _, half, cols = sums.shape
    tr = _rows_tile(half, cols, itemsize=4, budget=1024 * 1024)
    per_half = half // tr

    def body(place_ref, s_ref, o_ref, *rest):
        acc = s_ref[...].astype(F32)
        for j in range(N_CHIPS - 1):
            acc = acc + o_ref[j].astype(F32)
        rest[-1][...] = acc

    operands = [place, sums, others] + ([] if dest is None else [dest])
    return pl.pallas_call(
        body, name="add_chips", out_shape=jax.ShapeDtypeStruct((n_layers, 2 * half, cols), F32),
        grid_spec=pltpu.PrefetchScalarGridSpec(
            num_scalar_prefetch=1, grid=(per_half,),
            in_specs=[pl.BlockSpec((None, tr, cols), lambda i, pr: (pr[0], i, 0)),
                      pl.BlockSpec((N_CHIPS - 1, tr, cols), lambda i, pr: (0, i, 0))] + ([] if dest is None else [ANY]),
            out_specs=pl.BlockSpec((None, tr, cols), lambda i, pr: (layer, pr[1] * per_half + i, 0))),
        input_output_aliases={} if dest is None else {3: 0},
        compiler_params=_params(("parallel",)),
    )(*operands)


HBM = pl.BlockSpec(memory_space=pltpu.HBM)
SEM = pl.BlockSpec(memory_space=pltpu.SEMAPHORE)
DATAFLOW = pltpu.SideEffectType.DATAFLOW_SIDE_EFFECTING


def _split_start(name, bufs, copies, n_copies, after=None):
    n = len(bufs)
    extra = 0 if after is None else 1

    def body(*refs):
        for cpy in copies(refs[:n], refs[n + extra], refs[n + extra + 1]):
            cpy.start()
        refs[-1][...] = jnp.zeros_like(refs[-1])

    outs = pl.pallas_call(
        body, name=name,
        out_shape=(pltpu.SemaphoreType.DMA((n_copies,)), pltpu.SemaphoreType.DMA((n_copies,)),
                   *[pltpu.HBM(b.shape, b.dtype) for b in bufs], jax.ShapeDtypeStruct((8, LANES), F32)),
        in_specs=[HBM] * n + [ANY] * extra,
        out_specs=(SEM, SEM, *[HBM] * n, pl.BlockSpec(memory_space=pltpu.VMEM)),
        input_output_aliases={i: 2 + i for i in range(n)},
        compiler_params=pltpu.CompilerParams(has_side_effects=DATAFLOW),
    )(*[pltpu.with_memory_space_constraint(b, pltpu.HBM) for b in bufs], *([] if after is None else [after]))
    return outs[0], outs[1], list(outs[2:2 + n]), outs[-1]


def _split_wait(name, bufs, send, recv, copies, after):
    n = len(bufs)

    def body(*refs):
        for cpy in copies(refs[:n], refs[n], refs[n + 1]):
            cpy.wait_send()
            cpy.wait_recv()

    return list(pl.pallas_call(
        body, name=name, out_shape=tuple(pltpu.HBM(b.shape, b.dtype) for b in bufs),
        in_specs=[HBM] * n + [SEM, SEM, ANY], out_specs=tuple([HBM] * n),
        input_output_aliases={i: i for i in range(n)},
        compiler_params=pltpu.CompilerParams(has_side_effects=DATAFLOW),
    )(*bufs, send, recv, after))


def _gather_copies(bufs, send, recv):
    x, y, c = _place()
    out = []
    for u, buf in enumerate(bufs):
        half = buf.shape[1] // 2
        mine = buf.at[2 * x + y, pl.ds(c * half, half)]
        out += [_remote(mine, mine, send.at[3 * u + j], recv.at[3 * u + j], (*chip, c))
                for j, chip in enumerate(_other_chips(x, y))]
    return out


def _exchange_copies(bufs, send, recv):
    x, y, c = _place()
    n = len(bufs) // 2
    out = []
    for k in range(n):
        half = bufs[k].shape[1] // 2
        theirs = bufs[k].at[pl.ds(0, N_CHIPS), pl.ds((1 - c) * half, half)]
        out.append(_remote(theirs, bufs[n + k], send.at[k], recv.at[k], (x, y, 1 - c)))
    return out


def _all_to_all_copies(bufs, send, recv):
    x, y, c = _place()
    n = len(bufs) // 2
    return [_remote(bufs[k].at[2 * chip[0] + chip[1]], bufs[n + k].at[j], send.at[3 * k + j], recv.at[3 * k + j], (*chip, c))
            for k in range(n) for j, chip in enumerate(_other_chips(x, y))]


def _forward_copies(bufs, send, recv):
    x, y, c = _place()
    out = []
    for u, buf in enumerate(bufs):
        half = buf.shape[1] // 2
        for j, chip in enumerate(_other_chips(x, y)):
            landed = buf.at[2 * chip[0] + chip[1], pl.ds(c * half, half)]
            out.append(_remote(landed, landed, send.at[3 * u + j], recv.at[3 * u + j], (x, y, 1 - c)))
    return out


def _gather8_copies(bufs, send, recv):
    x, y, c = _place()
    targets = [(x, y, 1 - c)] + [(*chip, c) for chip in _other_chips(x, y)]
    out = []
    for b, buf in enumerate(bufs):
        mine = buf.at[4 * x + 2 * y + c]
        out += [_remote(mine, mine, send.at[N_CHIPS * b + k], recv.at[N_CHIPS * b + k], to) for k, to in enumerate(targets)]
    return out


def _forward_slots(bufs, name):
    n = len(bufs)

    def body(*refs):
        bufs_ = refs[n:2 * n]
        send, recv, _ = refs[2 * n:]
        x, y, c = _place()
        chips = _other_chips(x, y)
        sent = []
        for b in range(n):
            for j, chip in enumerate(chips):
                slot = bufs_[b].at[4 * chip[0] + 2 * chip[1] + c]
                cpy = _remote(slot, slot, send.at[3 * b + j], recv.at[3 * b + j], (x, y, 1 - c))
                cpy.start()
                sent.append(cpy)
        for b in range(n):
            for j, chip in enumerate(chips):
                slot = bufs_[b].at[4 * chip[0] + 2 * chip[1] + 1 - c]
                _remote(slot, slot, send.at[3 * b + j], recv.at[3 * b + j], (x, y, c)).wait_recv()
        for cpy in sent:
            cpy.wait_send()

    shapes, aliases = _in_place(bufs)
    return list(_comm_call(body, name, bufs, shapes, (N_CHIPS - 1) * n, 0, aliases))


FWD_GROUPS = {'mix': ('ar_out', 'ff_in0', 'ff_out0'), 'l1': ('gm_in', 'gm_out', 'ff_in1', 'ff_out1')}
GRAD_LAYOUT = {'ff_in0': (0, 0), 'ff_in1': (0, 1), 'ff_out0': (1, 0), 'ff_out1': (1, 1), 'ar_in': (2, 0), 'ar_out': (3, 0),
               'gm_in': (4, 0), 'gm_out': (5, 0)}


class _MeshLink:
    def __init__(self, place, shards):
        self.place = place
        first = _gather_weights([shards['ar_in']])
        self.ready = {'ar_in': first[0]}
        self.pending, after = {}, first[0]
        for group, names in FWD_GROUPS.items():
            send, recv, bufs, token = _split_start(f"gather_{group}_start", [shards[n] for n in names], _gather_copies,
                                                   3 * len(names), after)
            self.pending[group] = (names, send, recv, bufs)
            after = token
        self.start_token = after[0, 0]
        self.forwarding, self.exchanging, self.sent, self.last_token = {}, {}, {}, None

    def prefetch(self, group, after):
        names, send, recv, bufs = self.pending.pop(group)
        bufs = _split_wait(f"gather_{group}_wait", bufs, send, recv, _gather_copies, after)
        send, recv, bufs, token = _split_start(f"forward_{group}_start", bufs, _forward_copies, 3 * len(names))
        self.forwarding[group] = (names, send, recv, bufs)
        return token[0, 0]

    def weights(self, group, after):
        if group in self.forwarding:
            names, send, recv, bufs = self.forwarding.pop(group)
            self.ready.update(zip(names, _split_wait(f"forward_{group}_wait", bufs, send, recv, _forward_copies, after)))
        return self.ready

    def gradients(self, group, grads, after=None):
        tok = self.poll(next(iter(grads.values())))
        names, mine = list(grads), list(grads.values())
        landing = [lax.empty((N_CHIPS, g.shape[1] // 2, g.shape[2]), g.dtype) for g in mine]
        send, recv, bufs, token = _split_start(f"exchange_{group}_start", mine + landing, _exchange_copies, len(names), after)
        self.exchanging[group] = (names, send, recv, bufs)
        self.last_token = token
        return token[0, 0] + tok

    def poll(self, after):
        tok = 0.0
        for group in list(self.exchanging):
            names, send, recv, bufs = self.exchanging.pop(group)
            bufs = _split_wait(f"exchange_{group}_wait", bufs, send, recv, _exchange_copies, after)
            sums = [_add_halves(g, r, self.place) for g, r in zip(bufs[:len(names)], bufs[len(names):])]
            landing = [lax.empty((N_CHIPS - 1,) + s.shape[1:], s.dtype) for s in sums]
            send, recv, bufs, token = _split_start(f"grads_{group}_start", sums + landing, _all_to_all_copies, 3 * len(names))
            self.sent[group] = (names, send, recv, bufs)
            self.last_token = token
            tok = tok + token[0, 0]
        return tok

    def reduce(self, groups, after):
        units = {}
        for group in groups:
            names, send, recv, bufs = self.sent.pop(group)
            bufs = _split_wait(f"grads_{group}_wait", bufs, send, recv, _all_to_all_copies, after)
            units.update(zip(names, zip(bufs[:len(names)], bufs[len(names):])))
        n_layers = {p: 1 + max(l for pp, l in GRAD_LAYOUT.values() if pp == p) for p, _ in GRAD_LAYOUT.values()}
        out = {}
        for name, (p, layer) in GRAD_LAYOUT.items():
            if name in units:
                out[p] = _add_chips(*units[name], self.place, out.get(p), layer, n_layers[p])
        params = sorted(out)
        return dict(zip(params, _join_halves([out[p] for p in params])))


def _rope_tables(n):
    n_rows = n // GRID_W
    freqs = ROPE_THETA ** (-jnp.arange(ROPE_PAIRS, dtype=F32) / ROPE_PAIRS)
    ang_r = jnp.arange(n_rows, dtype=F32)[:, None] * freqs
    ang_c = jnp.arange(GRID_W, dtype=F32)[:, None] * freqs

    def per_token(of_row, of_col):
        r = jnp.broadcast_to(of_row[:, None, :], (n_rows, GRID_W, ROPE_PAIRS)).reshape(n, ROPE_PAIRS)
        c = jnp.broadcast_to(of_col[None, :, :], (n_rows, GRID_W, ROPE_PAIRS)).reshape(n, ROPE_PAIRS)
        return r, c

    cos_r, cos_c = per_token(jnp.cos(ang_r), jnp.cos(ang_c))
    sin_r, sin_c = per_token(jnp.sin(ang_r), jnp.sin(ang_c))
    cos = jnp.concatenate([cos_r, cos_r, cos_c, cos_c], axis=-1)
    sin = jnp.concatenate([-sin_r, sin_r, -sin_c, sin_c], axis=-1)
    return cos, sin


def _ffn_fwd(h2, w1, w2, tag):
    r, a = _matmul(h2, w1, kind='nn', b_split='n', out_dtype=BF16, epilogue='relu2', name=f"ffn_in_{tag}")
    f = _matmul(a, w2, kind='nn', b_split='k', out_dtype=F32, name=f"ffn_out_{tag}")
    return r, a, f


def _ffn_bwd(d_f, h2, r, a, w1, w2, tag):
    d_u = _matmul(d_f, w2, kind='nt', b_split='k', out_dtype=BF16, epilogue='times2x', extra=r, name=f"ffn_out_dx_{tag}")
    d_w2 = _matmul(a, d_f, kind='tn', out_split='k', out_dtype=BF16, name=f"ffn_out_dw_{tag}")
    d_w1 = _matmul(h2, d_u, kind='tn', out_split='n', out_dtype=BF16, name=f"ffn_in_dw_{tag}")
    d_h2 = _matmul(d_u, w1, kind='nt', b_split='n', out_dtype=F32, name=f"ffn_in_dx_{tag}")
    return d_h2, d_w1, d_w2


class _LocalLink:
    def __init__(self, big):
        self.big, self.grads, self.start_token = big, {}, 0.0

    def prefetch(self, group, after):
        return 0.0

    def poll(self, after):
        return 0.0

    def weights(self, group, after):
        return self.big

    def gradients(self, group, grads):
        self.grads.update(grads)
        return 0.0


def _local_step(xl0, xc0, target, ml, mc0, sp, link):
    n_lat, n_ctx = xl0.shape[0], xc0.shape[0]
    one = lambda v: 1.0 + v
    g = [[sp['norm_g'][i, k][None, :] for k in range(4)] for i in range(2)]

    sh1, sc1, gt1, sh2, sc2, gt2 = ml[0]
    big = link.weights('ar', None)
    sh1 = sh1 + link.start_token
    n_all = n_lat + n_ctx
    h_all = _norm_fwd(xl0, g[0][0], one(sc1), b=sh1, out_dtype=BF16, name="l0_mod1", into=(0, n_all, None))
    h_all = _norm_fwd(xc0, g[0][0], one(mc0[1]), b=mc0[0], out_dtype=BF16, name="l0_mod1_ctx", into=(n_lat, n_all, h_all))
    proj_l = _matmul(h_all, big['ar_in'], kind='nn', b_split='n', out_dtype=F32, a_rows=(0, n_lat), name="ar_in_lat")
    proj_c = _matmul(h_all, big['ar_in'], kind='nn', b_split='n', out_dtype=F32, a_rows=(n_lat, n_ctx), name="ar_in_ctx")
    cos_l, sin_l = _rope_tables(n_lat)
    cos_c, sin_c = jnp.ones((n_ctx, HEAD_DIM), F32), jnp.zeros((n_ctx, HEAD_DIM), F32)
    q_g, k_g = sp['q_g'], sp['k_g']
    _, k_all, v_all = _qk_fwd(proj_c, q_g, k_g, cos_c, sin_c, name="qk_fwd_ctx", kv_into=(0, n_all, None))
    q_l, k_all, v_all = _qk_fwd(proj_l, q_g, k_g, cos_l, sin_l, name="qk_fwd_lat", kv_into=(n_ctx, n_all, (k_all, v_all)))
    cat, lse = _attn_fwd(q_l, k_all, v_all)
    conv_b = sp['conv_b'] + link.prefetch('mix', cat)
    xs = _conv_fwd(proj_l, proj_c, sp['conv_w'], conv_b)
    rnn_w = [(sp['wa'][d], sp['ba'][d][None, :], sp['wx'][d], sp['bx'][d][None, :], sp['lam'][d][None, :]) for d in range(2)]
    h_f, hp_f = _rglru_fwd(xs, *rnn_w[0], reverse=False, n_ctx=n_ctx, name="rglru_fwd_f")
    h_r, hp_r = _rglru_fwd(xs, *rnn_w[1], reverse=True, n_ctx=n_ctx, name="rglru_fwd_r")
    cat = _rnn_out_fwd(h_f, h_r, proj_l, n_ctx, n_ctx, cat)
    w_mix = link.weights('mix', cat)
    ol0 = _matmul(cat, w_mix['ar_out'], kind='nn', b_split='k', out_dtype=F32, name="ar_out")
    xm0 = _norm_fwd(ol0, g[0][1], gt1, res=xl0, out_dtype=F32, name="l0_res1")
    h2_0 = _norm_fwd(xm0, g[0][2], one(sc2), b=sh2, out_dtype=BF16, name="l0_mod2")
    r0, a0, f0 = _ffn_fwd(h2_0, w_mix['ff_in0'], w_mix['ff_out0'], "l0")
    xl1 = _norm_fwd(f0, g[0][3], gt2 + link.prefetch('l1', f0), res=xm0, out_dtype=F32, name="l0_res2")

    th1, tc1, tg1, th2, tc2, tg2 = ml[1]
    w_l1 = link.weights('l1', xl1)
    hl1 = _norm_fwd(xl1, g[1][0], one(tc1), b=th1, out_dtype=BF16, name="l1_mod1")
    z = _matmul(hl1, w_l1['gm_in'], kind='nn', b_split='n', bias=sp['gm_b_in'], out_dtype=F32, name="gm_in")
    b_sp_t = sp['gm_b_sp'].T
    gated = _gmlp_fwd(z, sp['gm_v_g'], sp['gm_v_b'], sp['gm_w_sp'], b_sp_t)
    ol1 = _matmul(gated, w_l1['gm_out'], kind='nn', b_split='k', out_dtype=F32, name="gm_out")
    xm1 = _norm_fwd(ol1, g[1][1], tg1, res=xl1, out_dtype=F32, name="l1_res1")
    h2_1 = _norm_fwd(xm1, g[1][2], one(tc2), b=th2, out_dtype=BF16, name="l1_mod2")
    r1, a1, f1 = _ffn_fwd(h2_1, w_l1['ff_in1'], w_l1['ff_out1'], "l1")
    y = _norm_fwd(f1, g[1][3], tg2, res=xm1, out_dtype=F32, name="l1_res2")

    dy, loss = _loss_head(y, target)

    d_f1, dg13, d_tg2, _ = _norm_bwd(dy, f1, g[1][3], tg2, out_dtype=BF16, name="l1_res2_bwd")
    d_h2, dw_ff_in1, dw_ff_out1 = _ffn_bwd(d_f1, h2_1, r1, a1, w_l1['ff_in1'], w_l1['ff_out1'], "l1")
    tok = link.gradients('ffn1', {'ff_in1': dw_ff_in1, 'ff_out1': dw_ff_out1})
    dxm1, dg12, d_tc2, d_th2 = _norm_bwd(d_h2, xm1, g[1][2], one(tc2) + tok, extra=dy, out_dtype=F32, name="l1_mod2_bwd")
    d_ol1, dg11, d_tg1, _ = _norm_bwd(dxm1, ol1, g[1][1], tg1, out_dtype=BF16, name="l1_res1_bwd")
    d_gated = _matmul(d_ol1, w_l1['gm_out'], kind='nt', b_split='k', out_dtype=F32, name="gm_out_dx")
    dw_gm_out = _matmul(gated, d_ol1, kind='tn', out_split='k', out_dtype=BF16, name="gm_out_dw")
    d_z, d_gm_b_in, d_vg, d_vb, d_wsp, d_bsp_t = _gmlp_bwd(z, d_gated, sp['gm_v_g'], sp['gm_v_b'], sp['gm_w_sp'], b_sp_t)
    dw_gm_in = _matmul(hl1, d_z, kind='tn', out_split='n', out_dtype=BF16, name="gm_in_dw")
    d_hl1 = _matmul(d_z, w_l1['gm_in'], kind='nt', b_split='n', out_dtype=F32, name="gm_in_dx")
    tok = link.gradients('gm', {'gm_in': dw_gm_in, 'gm_out': dw_gm_out})
    dxl1, dg10, d_tc1, d_th1 = _norm_bwd(d_hl1, xl1, g[1][0], one(tc1) + tok, extra=dxm1, out_dtype=F32, name="l1_mod1_bwd")

    d_f0, dg03, d_gt2, _ = _norm_bwd(dxl1, f0, g[0][3], gt2, out_dtype=BF16, name="l0_res2_bwd")
    d_h2, dw_ff_in0, dw_ff_out0 = _ffn_bwd(d_f0, h2_0, r0, a0, w_mix['ff_in0'], w_mix['ff_out0'], "l0")
    tok = link.gradients('ffn0', {'ff_in0': dw_ff_in0, 'ff_out0': dw_ff_out0})
    dxm0, dg02, d_sc2, d_sh2 = _norm_bwd(d_h2, xm0, g[0][2], one(sc2) + tok, extra=dxl1, out_dtype=F32, name="l0_mod2_bwd")
    d_ol0, dg01, d_gt1, _ = _norm_bwd(dxm0, ol0, g[0][1], gt1, out_dtype=BF16, name="l0_res1_bwd")
    d_cat = _matmul(d_ol0, w_mix['ar_out'], kind='nt', b_split='k', out_dtype=F32, name="ar_out_dx")
    dw_ar_out = _matmul(cat, d_ol0, kind='tn', out_split='k', out_dtype=BF16, name="ar_out_dw")
    dq, dk_all, dv_all = _attn_bwd(q_l, k_all, v_all, cat, lse, d_cat)
    tok = link.poll(dq)
    d_h, d_gate = _rnn_out_bwd(d_cat, h_f, h_r, proj_l, n_ctx, n_ctx)
    rnn_wb = [(wa_, ba_ + tok, wx_, bx_, lam_) for wa_, ba_, wx_, bx_, lam_ in rnn_w]
    dxs_f, d_wa0, d_ba0, d_wx0, d_bx0, d_lam0 = _rglru_bwd(
        xs, hp_f, d_h, *rnn_wb[0], reverse=False, n_ctx=n_ctx, name="rglru_bwd_f")
    dxs_r, d_wa1, d_ba1, d_wx1, d_bx1, d_lam1 = _rglru_bwd(
        xs, hp_r, d_h, *rnn_wb[1], reverse=True, n_ctx=n_ctx, name="rglru_bwd_r")
    d_xr_l, d_xr_c, d_cw, d_cb = _conv_bwd(dxs_f, dxs_r, proj_l, proj_c, sp['conv_w'])
    dp_qk_l, d_qg, d_kg_l = _qk_bwd(dq, dk_all, proj_l, q_g, k_g, cos_l, sin_l, name="qk_bwd_lat", dk_row0=n_ctx)
    dp_qk_c, _, d_kg_c = _qk_bwd(None, dk_all, proj_c, q_g, k_g, cos_c, sin_c, name="qk_bwd_ctx")
    d_proj = _assemble_d_proj(dp_qk_l, dp_qk_c, dv_all, d_xr_l, d_xr_c, d_gate)
    dw_ar_in = _matmul(h_all, d_proj, kind='tn', out_split='n', out_dtype=BF16, name="ar_in_dw")
    d_hl = _matmul(d_proj, big['ar_in'], kind='nt', b_split='n', out_dtype=F32, a_rows=(0, n_lat), name="ar_in_dx_lat")
    d_hc = _matmul(d_proj, big['ar_in'], kind='nt', b_split='n', out_dtype=F32, a_rows=(n_lat, n_ctx), name="ar_in_dx_ctx")
    grad_x, dg00, d_sc1, d_sh1 = _norm_bwd(d_hl, xl0, g[0][0], one(sc1), extra=dxm0, out_dtype=F32, name="l0_mod1_bwd")
    _, dg00c, d_mc_scale, d_mc_shift = _norm_bwd(d_hc, xc0, g[0][0], one(mc0[1]), out_dtype=BF16, name="l0_mod1_ctx_bwd")

    zeros_d = jnp.zeros_like(d_sh1)
    small = {
        'd_ml0': jnp.concatenate([d_sh1, d_sc1, d_gt1, d_sh2, d_sc2, d_gt2], axis=1),
        'd_ml1': jnp.concatenate([d_th1, d_tc1, d_tg1, d_th2, d_tc2, d_tg2], axis=1),
        'd_mc0': jnp.concatenate([d_mc_shift, d_mc_scale] + [zeros_d] * 4, axis=1),
        'norm_g': jnp.stack([jnp.concatenate([dg00 + dg00c, dg01, dg02, dg03], axis=0),
                             jnp.concatenate([dg10, dg11, dg12, dg13], axis=0)]),
        'q_g': d_qg, 'k_g': d_kg_l + d_kg_c, 'conv_w': d_cw, 'conv_b': d_cb,
        'wa': jnp.stack([d_wa0, d_wa1]), 'ba': jnp.concatenate([d_ba0, d_ba1], axis=0),
        'wx': jnp.stack([d_wx0, d_wx1]), 'bx': jnp.concatenate([d_bx0, d_bx1], axis=0),
        'lam': jnp.concatenate([d_lam0, d_lam1], axis=0),
        'gm_b_in': d_gm_b_in, 'gm_v_g': d_vg, 'gm_v_b': d_vb, 'gm_w_sp': d_wsp, 'gm_b_sp': d_bsp_t.T,
        'loss': loss,
    }
    return grad_x, small, {'ar_in': dw_ar_in, 'ar_out': dw_ar_out}


MOD_ROWS = 16
SMALL_F32 = ('d_ml0', 'd_ml1', 'd_mc0', 'norm_g', 'q_g', 'k_g', 'conv_w', 'conv_b', 'ba', 'bx', 'lam', 'gm_b_in', 'gm_v_g',
             'gm_v_b', 'gm_b_sp', 'loss')
SMALL_BF16 = ('wa', 'wx', 'gm_w_sp')


def _silu(v):
    return v * _sigmoid(v)


def _chip_concat(gathered, axis):
    return jnp.concatenate([gathered[2 * q] for q in range(N_CHIPS)], axis=axis)


def kernel(x, c, ctx, c_ctx, w_mod, b_mod, norm_g, w_ff_in, w_ff_out, ar_w_in, ar_q_g, ar_k_g, ar_conv_w, ar_conv_b, ar_wa, ar_ba, ar_wx, ar_bx, ar_lambda, ar_w_out, gm_w_in, gm_b_in, gm_v_g, gm_v_b, gm_w_sp, gm_b_sp, gm_w_out, loss_target, m_c_ctx, m_w_mod, m_b_mod, m_norm_g, m_w_ff_in, m_w_ff_out, m_ar_w_in, m_ar_q_g, m_ar_k_g, m_ar_conv_w, m_ar_conv_b, m_ar_wa, m_ar_ba, m_ar_wx, m_ar_bx, m_ar_lambda, m_ar_w_out, m_gm_w_in, m_gm_b_in, m_gm_v_g, m_gm_v_b, m_gm_w_sp, m_gm_b_sp, m_gm_w_out, v_c_ctx, v_w_mod, v_b_mod, v_norm_g, v_w_ff_in, v_w_ff_out, v_ar_w_in, v_ar_q_g, v_ar_k_g, v_ar_conv_w, v_ar_conv_b, v_ar_wa, v_ar_ba, v_ar_wx, v_ar_bx, v_ar_lambda, v_ar_w_out, v_gm_w_in, v_gm_b_in, v_gm_v_g, v_gm_v_b, v_gm_w_sp, v_gm_b_sp, v_gm_w_out):
    weights = dict(c_ctx=c_ctx, w_mod=w_mod, b_mod=b_mod, norm_g=norm_g, w_ff_in=w_ff_in, w_ff_out=w_ff_out, ar_w_in=ar_w_in,
                   ar_q_g=ar_q_g, ar_k_g=ar_k_g, ar_conv_w=ar_conv_w, ar_conv_b=ar_conv_b, ar_wa=ar_wa, ar_ba=ar_ba, ar_wx=ar_wx,
                   ar_bx=ar_bx, ar_lambda=ar_lambda, ar_w_out=ar_w_out, gm_w_in=gm_w_in, gm_b_in=gm_b_in, gm_v_g=gm_v_g,
                   gm_v_b=gm_v_b, gm_w_sp=gm_w_sp, gm_b_sp=gm_b_sp, gm_w_out=gm_w_out)
    m_in = dict(c_ctx=m_c_ctx, w_mod=m_w_mod, b_mod=m_b_mod, norm_g=m_norm_g, w_ff_in=m_w_ff_in, w_ff_out=m_w_ff_out,
                ar_w_in=m_ar_w_in, ar_q_g=m_ar_q_g, ar_k_g=m_ar_k_g, ar_conv_w=m_ar_conv_w, ar_conv_b=m_ar_conv_b, ar_wa=m_ar_wa,
                ar_ba=m_ar_ba, ar_wx=m_ar_wx, ar_bx=m_ar_bx, ar_lambda=m_ar_lambda, ar_w_out=m_ar_w_out, gm_w_in=m_gm_w_in,
                gm_b_in=m_gm_b_in, gm_v_g=m_gm_v_g, gm_v_b=m_gm_v_b, gm_w_sp=m_gm_w_sp, gm_b_sp=m_gm_b_sp, gm_w_out=m_gm_w_out)
    v_in = dict(c_ctx=v_c_ctx, w_mod=v_w_mod, b_mod=v_b_mod, norm_g=v_norm_g, w_ff_in=v_w_ff_in, w_ff_out=v_w_ff_out,
                ar_w_in=v_ar_w_in, ar_q_g=v_ar_q_g, ar_k_g=v_ar_k_g, ar_conv_w=v_ar_conv_w, ar_conv_b=v_ar_conv_b, ar_wa=v_ar_wa,
                ar_ba=v_ar_ba, ar_wx=v_ar_wx, ar_bx=v_ar_bx, ar_lambda=v_ar_lambda, ar_w_out=v_ar_w_out, gm_w_in=v_gm_w_in,
                gm_b_in=v_gm_b_in, gm_v_g=v_gm_v_g, gm_v_b=v_gm_v_b, gm_w_sp=v_gm_w_sp, gm_b_sp=v_gm_b_sp, gm_w_out=v_gm_w_out)

    xi, yi, ci = lax.axis_index("x"), lax.axis_index("y"), lax.axis_index("c")
    chip = 2 * xi + yi
    dev = 4 * xi + 2 * yi + ci
    place = jnp.stack([chip, ci]).astype(jnp.int32)
    n_lat, d = x.shape[1], x.shape[2]
    d6 = 6 * d
    cols_mod = w_mod.shape[2]

    mine = [c, norm_g, ar_conv_w[0], ar_ba[0], ar_bx[0], ar_lambda[0], gm_b_in, gm_v_g, gm_v_b]
    gathered = _allgather8([_pack(mine)], "gather_small_params")[0]
    parts = _unpack_devices(gathered, [a.shape for a in mine])
    c_all = parts[0].reshape(N_DEV, d)
    sp = {'norm_g': _chip_concat(parts[1], 2), 'q_g': ar_q_g, 'k_g': ar_k_g, 'conv_w': _chip_concat(parts[2], 1),
          'conv_b': ar_conv_b, 'wa': ar_wa[0], 'ba': _chip_concat(parts[3], 1), 'wx': ar_wx[0], 'bx': _chip_concat(parts[4], 1),
          'lam': _chip_concat(parts[5], 1), 'gm_b_in': _chip_concat(parts[6], 1), 'gm_v_g': _chip_concat(parts[7], 1),
          'gm_v_b': _chip_concat(parts[8], 1), 'gm_w_sp': gm_w_sp[0], 'gm_b_sp': gm_b_sp[0]}

    def mod_operand(c_rows, cc):
        row = lax.broadcasted_iota(jnp.int32, (MOD_ROWS - N_DEV, d), 0)
        lower = jnp.where(row == 0, jnp.broadcast_to(_silu(cc), (MOD_ROWS - N_DEV, d)), 0.0)
        sig = _sigmoid(cc)
        return jnp.concatenate([_silu(c_rows), lower], axis=0), sig * (1.0 + cc * (1.0 - sig))

    s_mod, dsilu_ctx = _small(mod_operand, [((MOD_ROWS, d), F32), ((1, d), F32)], c_all, c_ctx[None, :], name="mod_operand")
    b_mod_mine = lax.dynamic_slice(b_mod, (0, chip * cols_mod), (2, cols_mod))
    mod = [_matmul(s_mod, w_mod[i], kind='nn', bias=b_mod_mine[i][None, :], out_dtype=F32, name=f"mod_fwd_{i}") for i in range(2)]
    mod_all = _allgather8([jnp.concatenate(mod, axis=0)], "gather_mod")[0]
    mod_all = _chip_concat(mod_all, 1).reshape(2, MOD_ROWS, d6)
    ml = [jnp.split(lax.dynamic_slice(mod_all[i], (dev, 0), (1, d6)), 6, axis=1) for i in range(2)]
    mc0 = jnp.split(mod_all[0, N_DEV:N_DEV + 1], 6, axis=1)[:2]

    names = ('w_ff_in', 'w_ff_out', 'ar_w_in', 'ar_w_out', 'gm_w_in', 'gm_w_out')
    keys = {'w_ff_in': ('ff_in0', 'ff_in1'), 'w_ff_out': ('ff_out0', 'ff_out1'), 'ar_w_in': ('ar_in',), 'ar_w_out': ('ar_out',),
            'gm_w_in': ('gm_in',), 'gm_w_out': ('gm_out',)}
    shards = {key: _cast_shard(weights[n], place, layer, f"cast_{key}") for n in names for layer, key in enumerate(keys[n])}
    link = _MeshLink(place, shards)

    grad_x, small, last_grads = _local_step(x[0], ctx[0], loss_target[0], ml, mc0, sp, link)

    def step(n, grad):
        return _adamw(weights[n], grad.reshape(weights[n].shape), m_in[n], v_in[n], f"adamw_{n}")

    small_f32, small_bf16 = [small[k] for k in SMALL_F32], [small[k] for k in SMALL_BF16]
    dev_arr = dev.astype(jnp.int32)[None]
    slots = [_into_slot(_pack(small_f32), dev_arr, "small_grads_slot_f32"),
             _into_slot(_pack(small_bf16, BF16), dev_arr, "small_grads_slot_bf16")]
    s_send, s_recv, slots, s_token = _split_start("small_grads_start", slots, _gather8_copies, 2 * N_CHIPS, grad_x)
    link.gradients('ar', last_grads, s_token)
    link.poll(link.last_token)
    reduced = link.reduce(('ffn1', 'gm', 'ffn0'), link.last_token)
    stepped = {n: step(n, reduced[names.index(n)]) for n in ('w_ff_in', 'w_ff_out', 'gm_w_in', 'gm_w_out')}
    reduced = link.reduce(('ar',), stepped['gm_w_out'][1])
    stepped.update({n: step(n, reduced[names.index(n)]) for n in ('ar_w_in', 'ar_w_out')})
    slots = _split_wait("small_grads_wait", slots, s_send, s_recv, _gather8_copies, stepped['ar_w_out'][1])
    small8, small8_bf16 = _forward_slots(slots, "small_grads_forward")
    total = dict(zip(SMALL_F32, _unpack(_sum_devices(small8).reshape(-1), [a.shape for a in small_f32])))
    total.update(zip(SMALL_BF16, _unpack(_sum_devices(small8_bf16).reshape(-1), [a.shape for a in small_bf16])))
    per_dev = _unpack_devices(small8, [(d6,), (d6,)])
    pad_rows = jnp.zeros((MOD_ROWS - N_DEV - 1, d6), F32)
    d_mod = [jnp.concatenate([per_dev[0], total['d_mc0'], pad_rows], axis=0),
             jnp.concatenate([per_dev[1], jnp.zeros((MOD_ROWS - N_DEV, d6), F32)], axis=0)]
    d_mod_mine = [lax.dynamic_slice(dm, (0, chip * cols_mod), (MOD_ROWS, cols_mod)) for dm in d_mod]
    g_w_mod = None
    for i in range(2):
        g_w_mod = _matmul(s_mod, d_mod_mine[i], kind='tn', out_dtype=F32, out_stack=(i, 2, g_w_mod), name=f"mod_dw_{i}")
    d_s_part = _matmul(d_mod_mine[0], w_mod[0], kind='nt', out_dtype=F32, name="mod_ds")
    d_s_all = _allgather8([d_s_part[N_DEV:]], "gather_mod_ds")[0]

    def c_ctx_grad(parts_, dsilu):
        acc = parts_[0, 0:1]
        for q in range(1, N_CHIPS):
            acc = acc + parts_[2 * q, 0:1]
        return (acc * dsilu,)

    g_c_ctx = _small(c_ctx_grad, [((1, d), F32)], d_s_all, dsilu_ctx, name="c_ctx_grad")[0].reshape(d)

    def mine_of(full_grad, axis, n_shard):
        return lax.dynamic_slice_in_dim(full_grad, chip * n_shard, n_shard, axis=axis)

    grads_out = {
        'c_ctx': g_c_ctx, 'w_mod': g_w_mod,
        'b_mod': jnp.stack([total['d_ml0'][0] + total['d_mc0'][0], total['d_ml1'][0]]),
        'norm_g': mine_of(total['norm_g'], 2, norm_g.shape[2]),
        'ar_q_g': total['q_g'], 'ar_k_g': total['k_g'], 'ar_conv_w': mine_of(total['conv_w'], 1, ar_conv_w.shape[2])[None],
        'ar_conv_b': total['conv_b'], 'ar_wa': total['wa'][None], 'ar_ba': mine_of(total['ba'], 1, ar_ba.shape[2])[None],
        'ar_wx': total['wx'][None], 'ar_bx': mine_of(total['bx'], 1, ar_bx.shape[2])[None],
        'ar_lambda': mine_of(total['lam'], 1, ar_lambda.shape[2])[None],
        'gm_b_in': mine_of(total['gm_b_in'], 1, gm_b_in.shape[1]),
        'gm_v_g': mine_of(total['gm_v_g'], 1, gm_v_g.shape[1]), 'gm_v_b': mine_of(total['gm_v_b'], 1, gm_v_b.shape[1]),
        'gm_w_sp': total['gm_w_sp'][None], 'gm_b_sp': total['gm_b_sp'][None],
    }
    stepped.update({n: step(n, grad) for n, grad in grads_out.items()})
    stepped = [stepped[n] for n in weights]
    loss = total['loss'].reshape(())
    return (loss, grad_x[None], *[s[0] for s in stepped], *[s[1] for s in stepped], *[s[2] for s in stepped],
            *[s[3] for s in stepped])
```

```python
import functools
import math

import jax
import jax.numpy as jnp
from jax import lax
from jax.experimental import pallas as pl
from jax.experimental.pallas import tpu as pltpu

F32 = jnp.float32
BF16 = jnp.bfloat16
MESH = pl.DeviceIdType.MESH
ANY = pl.BlockSpec(memory_space=pl.ANY)

VMEM_LIMIT_BYTES = 52 * 1024 * 1024
LANES = 128
N_CHIPS = 4
N_DEV = 8

HEAD_DIM = 128
N_HEADS = 8
N_KV = 2
GROUP = N_HEADS // N_KV
ATTN_W = N_HEADS * HEAD_DIM
KV_W = N_KV * HEAD_DIM
D_RNN = 1024
RNN_BLOCKS = 8
RNN_BW = D_RNN // RNN_BLOCKS
CONV_W = 4
RG_C = 8.0
GRID_W = 64
ROPE_THETA = 10000.0
ROPE_PAIRS = HEAD_DIM // 4
GM_GROUPS = 16
CHUNK = 128
EPS = 1e-6
ADAM_LR, ADAM_B1, ADAM_B2, ADAM_EPS, ADAM_WD, ADAM_STEP = 0.001, 0.9, 0.999, 1e-08, 0.01, 10
GELU_C = math.sqrt(2.0 / math.pi)
LOG2E = math.log2(math.e)


def _params(sem=None):
    return pltpu.CompilerParams(dimension_semantics=sem, vmem_limit_bytes=VMEM_LIMIT_BYTES)


def _tile(dim, pref, unit):
    best = None
    t = unit
    while t <= min(dim, pref):
        if dim % t == 0:
            best = t
        t += unit
    return best if best is not None else dim


def _full(shape):
    nd = len(shape)
    return pl.BlockSpec(shape, lambda *_: (0,) * nd)


def _blocked_map(split, per_q):
    if split == 'n':
        return lambda r, c: (c // per_q, r, c % per_q)
    if split == 'k':
        return lambda r, c: (r // per_q, r % per_q, c)
    return lambda r, c: (r, c)


def _logical_shape(arr, split):
    if split == 'n':
        return arr.shape[1], arr.shape[0] * arr.shape[2]
    if split == 'k':
        return arr.shape[0] * arr.shape[1], arr.shape[2]
    return arr.shape[-2:]


def _matmul(a, b, *, kind, name, out_dtype, b_split=None, out_split=None, bias=None, epilogue=None, extra=None,
            a_rows=None, b_layer=None, out_stack=None, pref=(1024, 1024, 2048)):
    b_rows, b_cols = _logical_shape(b, b_split)
    row0 = 0
    if kind == 'nn':
        m, kc = a.shape
        n = b_cols
        assert b_rows == kc
    elif kind == 'nt':
        m, kc = a.shape
        n = b_rows
        assert b_cols == kc
    if a_rows is not None:
        assert kind != 'tn'
        row0, m = a_rows
    if kind == 'tn':
        kc, m = a.shape
        n = b_cols
        assert b_rows == kc
    b_row_ext = b.shape[1] if b_split == 'k' else b_rows
    b_col_ext = b.shape[2] if b_split == 'n' else b_cols
    out_row_ext = m // N_CHIPS if out_split == 'k' else m
    out_col_ext = n // N_CHIPS if out_split == 'n' else n
    if kind == 'nn':
        ti = _tile(math.gcd(min(m, out_row_ext), row0), pref[0], 16)
        tj = _tile(math.gcd(b_col_ext, out_col_ext), pref[1], LANES)
        tl = _tile(b_row_ext, pref[2], LANES)
        a_spec = pl.BlockSpec((ti, tl), lambda i, j, l: (i + row0 // ti, l))
        b_tile, b_rc = (tl, tj), (lambda i, j, l: (l, j))
        dims = (((1,), (0,)), ((), ()))
    elif kind == 'nt':
        ti = _tile(math.gcd(min(m, out_row_ext), row0), pref[0], 16)
        tj = _tile(math.gcd(b_row_ext, out_col_ext), pref[1], LANES)
        tl = _tile(b_col_ext, pref[2], LANES)
        a_spec = pl.BlockSpec((ti, tl), lambda i, j, l: (i + row0 // ti, l))
        b_tile, b_rc = (tj, tl), (lambda i, j, l: (j, l))
        dims = (((1,), (1,)), ((), ()))
    else:
        ti = _tile(out_row_ext, pref[0], LANES)
        tj = _tile(math.gcd(b_col_ext, out_col_ext), pref[1], LANES)
        tl = _tile(b_row_ext, pref[2], 16)
        a_spec = pl.BlockSpec((tl, ti), lambda i, j, l: (l, i))
        b_tile, b_rc = (tl, tj), (lambda i, j, l: (l, j))
        dims = (((0,), (0,)), ((), ()))
    grid = (m // ti, n // tj, kc // tl)
    n_l = grid[2]

    if b_layer is not None:
        b_spec = pl.BlockSpec((None,) + b_tile, lambda i, j, l: (b_layer, *b_rc(i, j, l)))
    elif b_split is None:
        b_spec = pl.BlockSpec(b_tile, b_rc)
    else:
        per_q = (b.shape[2] // b_tile[1]) if b_split == 'n' else (b.shape[1] // b_tile[0])
        bmap = _blocked_map(b_split, per_q)
        b_spec = pl.BlockSpec((None,) + b_tile, lambda i, j, l: bmap(*b_rc(i, j, l)))
    if out_stack is not None:
        layer, n_layers, _ = out_stack
        out_shape2 = (n_layers, m, n)
        o_spec = pl.BlockSpec((None, ti, tj), lambda i, j, l: (layer, i, j))
    elif out_split is None:
        out_shape2 = (m, n)
        o_spec = pl.BlockSpec((ti, tj), lambda i, j, l: (i, j))
    else:
        out_shape2 = (N_CHIPS, m // N_CHIPS, n) if out_split == 'k' else (N_CHIPS, m, n // N_CHIPS)
        per_q = (out_shape2[2] // tj) if out_split == 'n' else (out_shape2[1] // ti)
        omap = _blocked_map(out_split, per_q)
        o_spec = pl.BlockSpec((None, ti, tj), lambda i, j, l: omap(i, j))

    in_specs = [a_spec, b_spec]
    operands = [a, b]
    if bias is not None:
        in_specs.append(pl.BlockSpec((1, tj), lambda i, j, l: (0, j)))
        operands.append(bias)
    if extra is not None:
        in_specs.append(pl.BlockSpec((ti, tj), lambda i, j, l: (i, j)))
        operands.append(extra)
    if epilogue == 'relu2':
        out_shape = (jax.ShapeDtypeStruct(out_shape2, out_dtype), jax.ShapeDtypeStruct(out_shape2, out_dtype))
        out_specs = (o_spec, o_spec)
    else:
        out_shape = jax.ShapeDtypeStruct(out_shape2, out_dtype)
        out_specs = o_spec
    has_bias, has_extra = bias is not None, extra is not None
    has_dest = out_stack is not None and out_stack[2] is not None
    if has_dest:
        in_specs.append(ANY)
        operands.append(out_stack[2])

    def body(*refs):
        a_ref, b_ref = refs[0], refs[1]
        pos = 2
        bias_ref = extra_ref = None
        if has_bias:
            bias_ref = refs[pos]
            pos += 1
        if has_extra:
            extra_ref = refs[pos]
            pos += 1
        if has_dest:
            pos += 1
        outs = refs[pos:] if n_l == 1 else refs[pos:-1]

        def finish(acc):
            if has_bias:
                acc = acc + bias_ref[...]
            if epilogue == 'relu2':
                r = jnp.maximum(acc, 0.0)
                outs[0][...] = r.astype(outs[0].dtype)
                outs[1][...] = (r * r).astype(outs[1].dtype)
            elif epilogue == 'times2x':
                outs[0][...] = (acc * (2.0 * extra_ref[...].astype(F32))).astype(outs[0].dtype)
            else:
                outs[0][...] = acc.astype(outs[0].dtype)

        def product():
            return lax.dot_general(a_ref[...].astype(BF16), b_ref[...].astype(BF16), dims, preferred_element_type=F32)

        if n_l == 1:
            finish(product())
            return
        acc_ref = refs[-1]
        step = pl.program_id(2)

        @pl.when(step == 0)
        def _():
            acc_ref[...] = jnp.zeros_like(acc_ref)

        acc_ref[...] += product()

        @pl.when(step == n_l - 1)
        def _():
            finish(acc_ref[...])

    return pl.pallas_call(
        body, name=name, grid=grid, in_specs=in_specs, out_specs=out_specs, out_shape=out_shape,
        input_output_aliases={len(operands) - 1: 0} if has_dest else {},
        scratch_shapes=[] if n_l == 1 else [pltpu.VMEM((ti, tj), F32)],
        compiler_params=_params(("parallel", "parallel", "arbitrary")),
    )(*operands)


def _small(fn, out_shapes, *arrays, name):
    n_in = len(arrays)

    def body(*refs):
        res = fn(*[r[...] for r in refs[:n_in]])
        for o_ref, v in zip(refs[n_in:], res):
            o_ref[...] = v.astype(o_ref.dtype)

    return pl.pallas_call(
        body, name=name, out_shape=tuple(jax.ShapeDtypeStruct(s, d) for s, d in out_shapes),
        in_specs=[_full(a.shape) for a in arrays], out_specs=tuple(_full(s) for s, _ in out_shapes), grid=(1,),
        compiler_params=_params(("arbitrary",)),
    )(*arrays)


def _rows_tile(rows, cols, itemsize=4, budget=2 * 1024 * 1024):
    return _tile(rows, max(16, budget // (cols * itemsize)), 16)


def _rowwise(fn, out_dtypes, *arrays, name):
    rows, cols = arrays[0].shape
    tr = _rows_tile(rows, cols)
    n_in = len(arrays)

    def body(*refs):
        res = fn(*[r[...] for r in refs[:n_in]])
        for o_ref, v in zip(refs[n_in:], res):
            o_ref[...] = v.astype(o_ref.dtype)

    spec = pl.BlockSpec((tr, cols), lambda i: (i, 0))
    return pl.pallas_call(
        body, name=name, grid=(rows // tr,), in_specs=[spec] * n_in, out_specs=tuple(spec for _ in out_dtypes),
        out_shape=tuple(jax.ShapeDtypeStruct((rows, cols), d) for d in out_dtypes),
        compiler_params=_params(("parallel",)),
    )(*arrays)


def _as2d(a):
    return a.reshape(1, a.size) if a.ndim < 2 else a.reshape(-1, a.shape[-1])


def _cast_shard(w, place, layer, name):
    _, rows, cols = w.shape
    tr = _rows_tile(rows, cols)

    def body(place_ref, w_ref, o_ref):
        o_ref[...] = w_ref[...].astype(o_ref.dtype)

    return pl.pallas_call(
        body, name=name, out_shape=jax.ShapeDtypeStruct((N_CHIPS, rows, cols), BF16),
        grid_spec=pltpu.PrefetchScalarGridSpec(
            num_scalar_prefetch=1, grid=(rows // tr,),
            in_specs=[pl.BlockSpec((None, tr, cols), lambda i, pr: (layer, i, 0))],
            out_specs=pl.BlockSpec((None, tr, cols), lambda i, pr: (pr[0], i, 0))),
        compiler_params=_params(("parallel",)),
    )(place, w)


def _norm_fwd(x, g, a, b=None, res=None, *, out_dtype, name, into=None):
    rows, d = x.shape
    row0, total, dest = into if into is not None else (0, rows, None)
    tr = _rows_tile(math.gcd(rows, row0), d, budget=4 * 1024 * 1024)
    has_b, has_res = b is not None, res is not None

    def body(*refs):
        x_ref, g_ref, a_ref = refs[:3]
        pos = 3
        xv = x_ref[...]
        rstd = lax.rsqrt(jnp.mean(xv * xv, axis=-1, keepdims=True) + EPS)
        y = (xv * rstd * g_ref[...]) * a_ref[...]
        if has_b:
            y = y + refs[pos][...]
            pos += 1
        if has_res:
            y = y + refs[pos][...]
            pos += 1
        refs[-1][...] = y.astype(refs[-1].dtype)

    row = pl.BlockSpec((tr, d), lambda i: (i, 0))
    vec = pl.BlockSpec((1, d), lambda i: (0, 0))
    operands, specs = [x, g, a], [row, vec, vec]
    if has_b:
        operands.append(b)
        specs.append(vec)
    if has_res:
        operands.append(res)
        specs.append(row)
    if dest is not None:
        operands.append(dest)
        specs.append(ANY)
    return pl.pallas_call(
        body, name=name, grid=(rows // tr,), in_specs=specs, out_specs=pl.BlockSpec((tr, d), lambda i: (i + row0 // tr, 0)),
        out_shape=jax.ShapeDtypeStruct((total, d), out_dtype), compiler_params=_params(("parallel",)),
        input_output_aliases={} if dest is None else {len(operands) - 1: 0},
    )(*operands)


def _rstd(v):
    return lax.rsqrt(jnp.mean(v * v, axis=-1, keepdims=True) + EPS)


def _res_mod_fwd(o, x, g_res, gate, g_mod, a_mod, b_mod, *, name):
    rows, d = x.shape
    tr = _rows_tile(rows, d)

    def body(o_ref, x_ref, gr_ref, gate_ref, gm_ref, a_ref, b_ref, xm_ref, h_ref):
        ov = o_ref[...]
        xm = x_ref[...] + (ov * _rstd(ov) * gr_ref[...]) * gate_ref[...]
        xm_ref[...] = xm
        h_ref[...] = ((xm * _rstd(xm) * gm_ref[...]) * a_ref[...] + b_ref[...]).astype(h_ref.dtype)

    row = pl.BlockSpec((tr, d), lambda i: (i, 0))
    vec = pl.BlockSpec((1, d), lambda i: (0, 0))
    return pl.pallas_call(
        body, name=name, grid=(rows // tr,), in_specs=[row, row, vec, vec, vec, vec, vec], out_specs=(row, row),
        out_shape=(jax.ShapeDtypeStruct((rows, d), F32), jax.ShapeDtypeStruct((rows, d), BF16)),
        compiler_params=_params(("parallel",)),
    )(o, x, g_res, gate, g_mod, a_mod, b_mod)


def _mod_res_bwd(d_h, xm, g_mod, a_mod, extra, o, g_res, gate, *, name):
    rows, d = xm.shape
    tr = _rows_tile(rows, d)

    def body(dh_ref, xm_ref, gm_ref, a_ref, ex_ref, o_ref, gr_ref, gate_ref,
             dxm_ref, do_ref, dgm_ref, da_ref, db_ref, dgr_ref, dgate_ref):
        @pl.when(pl.program_id(0) == 0)
        def _():
            for ref in (dgm_ref, da_ref, db_ref, dgr_ref, dgate_ref):
                ref[...] = jnp.zeros_like(ref)

        def norm_adjoint(dy, xv, gain, scale, dgain_ref, dscale_ref):
            rstd = _rstd(xv)
            nrm = xv * rstd
            dscale_ref[...] += jnp.sum(dy * (nrm * gain), axis=0, keepdims=True)
            dt = dy * scale
            dgain_ref[...] += jnp.sum(dt * nrm, axis=0, keepdims=True)
            dn = dt * gain
            return rstd * (dn - nrm * jnp.mean(dn * nrm, axis=-1, keepdims=True))

        dhv = dh_ref[...].astype(F32)
        db_ref[...] += jnp.sum(dhv, axis=0, keepdims=True)
        dxm = norm_adjoint(dhv, xm_ref[...], gm_ref[...], a_ref[...], dgm_ref, da_ref) + ex_ref[...]
        dxm_ref[...] = dxm
        do_ref[...] = norm_adjoint(dxm, o_ref[...], gr_ref[...], gate_ref[...], dgr_ref, dgate_ref).astype(do_ref.dtype)

    row = pl.BlockSpec((tr, d), lambda i: (i, 0))
    vec = pl.BlockSpec((1, d), lambda i: (0, 0))
    vshape = jax.ShapeDtypeStruct((1, d), F32)
    return pl.pallas_call(
        body, name=name, grid=(rows // tr,), in_specs=[row, row, vec, vec, row, row, vec, vec],
        out_specs=(row, row, vec, vec, vec, vec, vec),
        out_shape=(jax.ShapeDtypeStruct((rows, d), F32), jax.ShapeDtypeStruct((rows, d), BF16)) + (vshape,) * 5,
        compiler_params=_params(("arbitrary",)),
    )(d_h, xm, g_mod, a_mod, extra, o, g_res, gate)


def _norm_bwd(dy, x, g, a, extra=None, *, out_dtype, name):
    rows, d = x.shape
    tr = _rows_tile(rows, d)
    has_extra = extra is not None

    def body(*refs):
        dy_ref, x_ref, g_ref, a_ref = refs[:4]
        pos = 4
        extra_ref = None
        if has_extra:
            extra_ref = refs[pos]
            pos += 1
        dx_ref, dg_ref, da_ref, db_ref = refs[pos:pos + 4]

        @pl.when(pl.program_id(0) == 0)
        def _():
            dg_ref[...] = jnp.zeros_like(dg_ref)
            da_ref[...] = jnp.zeros_like(da_ref)
            db_ref[...] = jnp.zeros_like(db_ref)

        xv = x_ref[...]
        dyv = dy_ref[...].astype(F32)
        rstd = lax.rsqrt(jnp.mean(xv * xv, axis=-1, keepdims=True) + EPS)
        nrm = xv * rstd
        gv = g_ref[...]
        da_ref[...] += jnp.sum(dyv * (nrm * gv), axis=0, keepdims=True)
        db_ref[...] += jnp.sum(dyv, axis=0, keepdims=True)
        dt = dyv * a_ref[...]
        dg_ref[...] += jnp.sum(dt * nrm, axis=0, keepdims=True)
        dn = dt * gv
        dx = rstd * (dn - nrm * jnp.mean(dn * nrm, axis=-1, keepdims=True))
        if has_extra:
            dx = dx + extra_ref[...]
        dx_ref[...] = dx.astype(dx_ref.dtype)

    row = pl.BlockSpec((tr, d), lambda i: (i, 0))
    vec = pl.BlockSpec((1, d), lambda i: (0, 0))
    operands, specs = [dy, x, g, a], [row, row, vec, vec]
    if has_extra:
        operands.append(extra)
        specs.append(row)
    vshape = jax.ShapeDtypeStruct((1, d), F32)
    return pl.pallas_call(
        body, name=name, grid=(rows // tr,), in_specs=specs, out_specs=(row, vec, vec, vec),
        out_shape=(jax.ShapeDtypeStruct((rows, d), out_dtype), vshape, vshape, vshape),
        compiler_params=_params(("arbitrary",)),
    )(*operands)


def _loss_head(y, target):
    rows, d = y.shape
    tr = _rows_tile(rows, d)

    def body(y_ref, t_ref, dy_ref, loss_ref):
        @pl.when(pl.program_id(0) == 0)
        def _():
            loss_ref[...] = jnp.zeros_like(loss_ref)

        err = y_ref[...] - t_ref[...]
        dy_ref[...] = err * (1.0 / d)
        loss_ref[...] += jnp.sum(jnp.sum(err * err, axis=-1, keepdims=True), axis=0, keepdims=True) * (0.5 / d)

    row = pl.BlockSpec((tr, d), lambda i: (i, 0))
    return pl.pallas_call(
        body, name="loss_head", grid=(rows // tr,), in_specs=[row, row], out_specs=(row, _full((1, 1))),
        out_shape=(jax.ShapeDtypeStruct((rows, d), F32), jax.ShapeDtypeStruct((1, 1), F32)),
        compiler_params=_params(("arbitrary",)),
    )(y, target)


def _rope_partner(v):
    lane = lax.broadcasted_iota(jnp.int32, v.shape, 1)
    up = pltpu.roll(v, HEAD_DIM - ROPE_PAIRS, 1)
    down = pltpu.roll(v, ROPE_PAIRS, 1)
    return jnp.where((lane % (2 * ROPE_PAIRS)) < ROPE_PAIRS, up, down)


def _qk_fwd(proj, q_g, k_g, cos, sin, *, name, kv_into=None):
    rows = proj.shape[0]
    row0, total, kv_dest = kv_into if kv_into is not None else (0, rows, None)
    tr = _tile(math.gcd(rows, row0), 256, 16)
    width = ATTN_W + 2 * KV_W

    def body(p_ref, qg_ref, kg_ref, cos_ref, sin_ref, *rest):
        q_ref, k_ref, v_ref = rest[-3:]
        cosv, sinv = cos_ref[...], sin_ref[...]
        for h in range(N_HEADS + N_KV):
            xv = p_ref[:, h * HEAD_DIM:(h + 1) * HEAD_DIM]
            gain = qg_ref[...] if h < N_HEADS else kg_ref[...]
            t = xv * lax.rsqrt(jnp.mean(xv * xv, axis=-1, keepdims=True) + EPS) * gain
            y = t * cosv + _rope_partner(t) * sinv
            if h < N_HEADS:
                q_ref[:, h * HEAD_DIM:(h + 1) * HEAD_DIM] = y.astype(BF16)
            else:
                k_ref[:, (h - N_HEADS) * HEAD_DIM:(h - N_HEADS + 1) * HEAD_DIM] = y.astype(BF16)
        v_ref[...] = p_ref[:, ATTN_W + KV_W:width].astype(BF16)

    vec = _full((1, HEAD_DIM))
    tab = pl.BlockSpec((tr, HEAD_DIM), lambda i: (i, 0))
    kv_spec = pl.BlockSpec((tr, KV_W), lambda i: (i + row0 // tr, 0))
    kv_shape = jax.ShapeDtypeStruct((total, KV_W), BF16)
    return pl.pallas_call(
        body, name=name, grid=(rows // tr,),
        in_specs=[pl.BlockSpec((tr, width), lambda i: (i, 0)), vec, vec, tab, tab] + ([] if kv_dest is None else [ANY, ANY]),
        out_specs=(pl.BlockSpec((tr, ATTN_W), lambda i: (i, 0)), kv_spec, kv_spec),
        out_shape=(jax.ShapeDtypeStruct((rows, ATTN_W), BF16), kv_shape, kv_shape),
        input_output_aliases={} if kv_dest is None else {5: 1, 6: 2},
        compiler_params=_params(("parallel",)),
    )(proj, q_g, k_g, cos, sin, *([] if kv_dest is None else kv_dest))


def _qk_bwd(dq, dk, proj, q_g, k_g, cos, sin, *, name, dk_row0=0):
    rows = proj.shape[0]
    tr = _tile(math.gcd(rows, dk_row0), 256, 16)
    width = ATTN_W + KV_W
    has_q = dq is not None

    def body(*refs):
        pos = 0
        dq_ref = None
        if has_q:
            dq_ref = refs[0]
            pos = 1
        dk_ref, p_ref, qg_ref, kg_ref, cos_ref, sin_ref, dp_ref, dqg_ref, dkg_ref = refs[pos:pos + 9]

        @pl.when(pl.program_id(0) == 0)
        def _():
            dqg_ref[...] = jnp.zeros_like(dqg_ref)
            dkg_ref[...] = jnp.zeros_like(dkg_ref)

        cosv, sinv = cos_ref[...], sin_ref[...]
        for h in range(N_HEADS + N_KV):
            cols = slice(h * HEAD_DIM, (h + 1) * HEAD_DIM)
            if h < N_HEADS and not has_q:
                dp_ref[:, cols] = jnp.zeros((tr, HEAD_DIM), dp_ref.dtype)
                continue
            if h < N_HEADS:
                dyv, gain, dgain_ref = dq_ref[:, cols], qg_ref[...], dqg_ref
            else:
                hk = h - N_HEADS
                dyv, gain, dgain_ref = dk_ref[:, hk * HEAD_DIM:(hk + 1) * HEAD_DIM], kg_ref[...], dkg_ref
            dyv = dyv.astype(F32)
            dt = dyv * cosv + _rope_partner(dyv * sinv)
            xv = p_ref[:, cols]
            rstd = lax.rsqrt(jnp.mean(xv * xv, axis=-1, keepdims=True) + EPS)
            nrm = xv * rstd
            dgain_ref[...] += jnp.sum(dt * nrm, axis=0, keepdims=True)
            dn = dt * gain
            dp_ref[:, cols] = (rstd * (dn - nrm * jnp.mean(dn * nrm, axis=-1, keepdims=True))).astype(dp_ref.dtype)

    vec = _full((1, HEAD_DIM))
    tab = pl.BlockSpec((tr, HEAD_DIM), lambda i: (i, 0))
    operands = ([dq] if has_q else []) + [dk, proj, q_g, k_g, cos, sin]
    specs = ([pl.BlockSpec((tr, ATTN_W), lambda i: (i, 0))] if has_q else []) + [
        pl.BlockSpec((tr, KV_W), lambda i: (i + dk_row0 // tr, 0)), pl.BlockSpec((tr, width), lambda i: (i, 0)), vec, vec, tab, tab]
    return pl.pallas_call(
        body, name=name, grid=(rows // tr,), in_specs=specs,
        out_specs=(pl.BlockSpec((tr, width), lambda i: (i, 0)), vec, vec),
        out_shape=(jax.ShapeDtypeStruct((rows, width), BF16), jax.ShapeDtypeStruct((1, HEAD_DIM), F32),
                   jax.ShapeDtypeStruct((1, HEAD_DIM), F32)),
        compiler_params=_params(("arbitrary",)),
    )(*operands)


def _attn_fwd(q, k, v):
    n_q, n_k = q.shape[0], k.shape[0]
    tq = _tile(n_q, 512, 16)
    gw = GROUP * HEAD_DIM
    scale = HEAD_DIM ** -0.5

    def body(q_ref, k_ref, v_ref, o_ref, lse_ref):
        kv, vv = k_ref[...], v_ref[...]
        for g in range(GROUP):
            cols = slice(g * HEAD_DIM, (g + 1) * HEAD_DIM)
            s = lax.dot_general(q_ref[:, cols], kv, (((1,), (1,)), ((), ())), preferred_element_type=F32) * (scale * LOG2E)
            m = jnp.max(s, axis=-1, keepdims=True)
            p = jnp.exp2(s - m)
            l = jnp.sum(p, axis=-1, keepdims=True)
            o = jnp.dot(p.astype(BF16), vv, preferred_element_type=F32) / l
            o_ref[:, cols] = o.astype(o_ref.dtype)
            lse_ref[:, g:g + 1] = m + jnp.log(l) * LOG2E

    return pl.pallas_call(
        body, name="attn_fwd", grid=(N_KV, n_q // tq),
        in_specs=[pl.BlockSpec((tq, gw), lambda h, i: (i, h)), pl.BlockSpec((n_k, HEAD_DIM), lambda h, i: (0, h)),
                  pl.BlockSpec((n_k, HEAD_DIM), lambda h, i: (0, h))],
        out_specs=(pl.BlockSpec((tq, gw), lambda h, i: (i, h)), pl.BlockSpec((None, tq, GROUP), lambda h, i: (h, i, 0))),
        out_shape=(jax.ShapeDtypeStruct((n_q, ATTN_W + D_RNN), BF16), jax.ShapeDtypeStruct((N_KV, n_q, GROUP), F32)),
        compiler_params=_params(("parallel", "parallel")),
    )(q, k, v)


def _attn_bwd(q, k, v, o, lse, do):
    n_q, n_k = q.shape[0], k.shape[0]
    tq = _tile(n_q, 256, 16)
    gw = GROUP * HEAD_DIM
    scale = HEAD_DIM ** -0.5

    def body(q_ref, k_ref, v_ref, o_ref, lse_ref, do_ref, dq_ref, dk_ref, dv_ref):
        @pl.when(pl.program_id(1) == 0)
        def _():
            dk_ref[...] = jnp.zeros_like(dk_ref)
            dv_ref[...] = jnp.zeros_like(dv_ref)

        kv, vv = k_ref[...], v_ref[...]
        for g in range(GROUP):
            cols = slice(g * HEAD_DIM, (g + 1) * HEAD_DIM)
            qg = q_ref[:, cols]
            dof = do_ref[:, cols].astype(F32)
            dog = dof.astype(BF16)
            s = lax.dot_general(qg, kv, (((1,), (1,)), ((), ())), preferred_element_type=F32) * (scale * LOG2E)
            p = jnp.exp2(s - lse_ref[:, g:g + 1])
            delta = jnp.sum(dof * o_ref[:, cols].astype(F32), axis=-1, keepdims=True)
            dp = lax.dot_general(dog, vv, (((1,), (1,)), ((), ())), preferred_element_type=F32)
            ds = (p * (dp - delta) * scale).astype(BF16)
            pb = p.astype(BF16)
            dq_ref[:, cols] = jnp.dot(ds, kv, preferred_element_type=F32)
            dk_ref[...] += lax.dot_general(ds, qg, (((0,), (0,)), ((), ())), preferred_element_type=F32)
            dv_ref[...] += lax.dot_general(pb, dog, (((0,), (0,)), ((), ())), preferred_element_type=F32)

    qspec = pl.BlockSpec((tq, gw), lambda h, i: (i, h))
    kspec = pl.BlockSpec((n_k, HEAD_DIM), lambda h, i: (0, h))
    return pl.pallas_call(
        body, name="attn_bwd", grid=(N_KV, n_q // tq),
        in_specs=[qspec, kspec, kspec, qspec, pl.BlockSpec((None, tq, GROUP), lambda h, i: (h, i, 0)), qspec],
        out_specs=(qspec, kspec, kspec),
        out_shape=(jax.ShapeDtypeStruct((n_q, ATTN_W), F32), jax.ShapeDtypeStruct((n_k, KV_W), F32),
                   jax.ShapeDtypeStruct((n_k, KV_W), F32)),
        compiler_params=_params(("parallel", "arbitrary")),
    )(q, k, v, o, lse, do)


CONV_COLS = 256
XR_COL0 = ATTN_W + 2 * KV_W


def _shift_rows(v, off):
    if off == 0:
        return v
    n = v.shape[0]
    rolled = pltpu.roll(v, (-off) % n, 0)
    t = lax.broadcasted_iota(jnp.int32, v.shape, 0)
    keep = (t + off >= 0) & (t + off < n)
    return jnp.where(keep, rolled, 0.0)


def _conv_fwd(proj_l, proj_c, w, b):
    n_lat, n_ctx = proj_l.shape[0], proj_c.shape[0]
    blk0 = XR_COL0 // CONV_COLS

    def body(xl_ref, xc_ref, w_ref, b_ref, y_ref):
        for x_ref, rows in ((xc_ref, slice(0, n_ctx)), (xl_ref, slice(n_ctx, n_ctx + n_lat))):
            xv = x_ref[...]
            y = b_ref[...] + jnp.zeros_like(xv)
            for j in range(CONV_W):
                y = y + _shift_rows(xv, j - CONV_W // 2) * w_ref[j:j + 1, :]
            y_ref[rows, :] = y

    return pl.pallas_call(
        body, name="conv_fwd", grid=(D_RNN // CONV_COLS,),
        in_specs=[pl.BlockSpec((n_lat, CONV_COLS), lambda i: (0, blk0 + i)), pl.BlockSpec((n_ctx, CONV_COLS), lambda i: (0, blk0 + i)),
                  pl.BlockSpec((CONV_W, CONV_COLS), lambda i: (0, i)), pl.BlockSpec((1, CONV_COLS), lambda i: (0, i))],
        out_specs=pl.BlockSpec((n_ctx + n_lat, CONV_COLS), lambda i: (0, i)),
        out_shape=jax.ShapeDtypeStruct((n_ctx + n_lat, D_RNN), F32), compiler_params=_params(("parallel",)),
    )(proj_l, proj_c, w, b)


def _conv_bwd(d1, d2, proj_l, proj_c, w):
    n_lat, n_ctx = proj_l.shape[0], proj_c.shape[0]
    blk0 = XR_COL0 // CONV_COLS

    def body(d1_ref, d2_ref, xl_ref, xc_ref, w_ref, dxl_ref, dxc_ref, dw_ref, db_ref):
        dw = [0.0] * CONV_W
        db = 0.0
        for x_ref, dx_ref, rows in ((xc_ref, dxc_ref, slice(0, n_ctx)), (xl_ref, dxl_ref, slice(n_ctx, n_ctx + n_lat))):
            dv = d1_ref[rows, :] + d2_ref[rows, :]
            xv = x_ref[...]
            dx = jnp.zeros_like(dv)
            for j in range(CONV_W):
                off = j - CONV_W // 2
                dx = dx + _shift_rows(dv, -off) * w_ref[j:j + 1, :]
                dw[j] = dw[j] + jnp.sum(dv * _shift_rows(xv, off), axis=0, keepdims=True)
            dx_ref[...] = dx.astype(dx_ref.dtype)
            db = db + jnp.sum(dv, axis=0, keepdims=True)
        for j in range(CONV_W):
            dw_ref[j:j + 1, :] = dw[j]
        db_ref[...] = db

    both = pl.BlockSpec((n_ctx + n_lat, CONV_COLS), lambda i: (0, i))
    return pl.pallas_call(
        body, name="conv_bwd", grid=(D_RNN // CONV_COLS,),
        in_specs=[both, both, pl.BlockSpec((n_lat, CONV_COLS), lambda i: (0, blk0 + i)),
                  pl.BlockSpec((n_ctx, CONV_COLS), lambda i: (0, blk0 + i)), pl.BlockSpec((CONV_W, CONV_COLS), lambda i: (0, i))],
        out_specs=(pl.BlockSpec((n_lat, CONV_COLS), lambda i: (0, i)), pl.BlockSpec((n_ctx, CONV_COLS), lambda i: (0, i)),
                   pl.BlockSpec((CONV_W, CONV_COLS), lambda i: (0, i)), pl.BlockSpec((1, CONV_COLS), lambda i: (0, i))),
        out_shape=(jax.ShapeDtypeStruct((n_lat, D_RNN), BF16), jax.ShapeDtypeStruct((n_ctx, D_RNN), BF16),
                   jax.ShapeDtypeStruct((CONV_W, D_RNN), F32), jax.ShapeDtypeStruct((1, D_RNN), F32)),
        compiler_params=_params(("parallel",)),
    )(d1, d2, proj_l, proj_c, w)


RNN_TB = 256
SCAN_ROWS = 8


def _sigmoid(z):
    return 1.0 / (1.0 + jnp.exp(-z))


def _softplus(z):
    return jnp.maximum(z, 0.0) + jnp.log(1.0 + jnp.exp(-jnp.abs(z)))


def _one_minus_exp(y):
    series = -y * (1.0 + y * (0.5 + y * (1.0 / 6.0 + y * (1.0 / 24.0))))
    return jnp.where(y > -0.03, series, 1.0 - jnp.exp(y))


def _rglru_gates(xv, wa_ref, ba_ref, wx_ref, bx_ref, lam_ref):
    xb = xv.astype(BF16)
    zr = jnp.concatenate([jnp.dot(xb[:, n * RNN_BW:(n + 1) * RNN_BW], wa_ref[n].astype(BF16),
                                  preferred_element_type=F32) for n in range(RNN_BLOCKS)], axis=-1) + ba_ref[...]
    zi = jnp.concatenate([jnp.dot(xb[:, n * RNN_BW:(n + 1) * RNN_BW], wx_ref[n].astype(BF16),
                                  preferred_element_type=F32) for n in range(RNN_BLOCKS)], axis=-1) + bx_ref[...]
    r = _sigmoid(zr)
    gi = _sigmoid(zi)
    sp = _softplus(-lam_ref[...])
    log_a = -RG_C * r * sp
    a = jnp.exp(log_a)
    s = jnp.sqrt(_one_minus_exp(2.0 * log_a))
    return r, gi, sp, a, s


def _scan_rows(n_rows, reverse, step_fn, carry):
    groups = n_rows // SCAN_ROWS

    def trip(gidx, carry):
        gi = (groups - 1 - gidx) if reverse else gidx
        base = pl.multiple_of(gi * SCAN_ROWS, SCAN_ROWS)
        return step_fn(base, carry)

    return lax.fori_loop(0, groups, trip, carry)


def _scan_block_order(nb, nb_c, reverse, adjoint):
    if not reverse:
        return (lambda i: nb - 1 - i) if adjoint else (lambda i: i)
    if adjoint:
        return lambda i: jnp.where(i < nb - nb_c, nb_c + i, i - (nb - nb_c))
    return lambda i: jnp.where(i < nb_c, nb_c - 1 - i, nb + nb_c - 1 - i)


def _rglru_fwd(xs, wa, ba, wx, bx, lam, *, reverse, n_ctx, name):
    rows = xs.shape[0]
    tb = _tile(math.gcd(rows, n_ctx), RNN_TB, SCAN_ROWS)
    nb = rows // tb
    block_of = _scan_block_order(nb, n_ctx // tb, reverse, False)
    order = lambda i: (block_of(i), 0)

    def body(x_ref, wa_ref, ba_ref, wx_ref, bx_ref, lam_ref, h_ref, hp_ref, a_s, b_s, state):
        @pl.when(pl.program_id(0) == 0)
        def _():
            state[...] = jnp.zeros_like(state)

        xv = x_ref[...]
        _, gi, _, a, s = _rglru_gates(xv, wa_ref, ba_ref, wx_ref, bx_ref, lam_ref)
        a_s[...] = a
        b_s[...] = s * (gi * xv)

        def group(base, h):
            av = a_s[pl.ds(base, SCAN_ROWS), :]
            bv = b_s[pl.ds(base, SCAN_ROWS), :]
            outs, prevs = [None] * SCAN_ROWS, [None] * SCAN_ROWS
            for k in range(SCAN_ROWS):
                r_ = SCAN_ROWS - 1 - k if reverse else k
                prevs[r_] = h
                h = av[r_:r_ + 1, :] * h + bv[r_:r_ + 1, :]
                outs[r_] = h
            h_ref[pl.ds(base, SCAN_ROWS), :] = jnp.concatenate(outs, axis=0)
            hp_ref[pl.ds(base, SCAN_ROWS), :] = jnp.concatenate(prevs, axis=0)
            return h

        state[0:1, :] = _scan_rows(tb, reverse, group, state[0:1, :])

    blk = pl.BlockSpec((tb, D_RNN), order)
    wspec = _full((RNN_BLOCKS, RNN_BW, RNN_BW))
    vec = _full((1, D_RNN))
    return pl.pallas_call(
        body, name=name, grid=(nb,), in_specs=[blk, wspec, vec, wspec, vec, vec], out_specs=(blk, blk),
        out_shape=(jax.ShapeDtypeStruct((rows, D_RNN), F32), jax.ShapeDtypeStruct((rows, D_RNN), F32)),
        scratch_shapes=[pltpu.VMEM((tb, D_RNN), F32), pltpu.VMEM((tb, D_RNN), F32), pltpu.VMEM((SCAN_ROWS, D_RNN), F32)],
        compiler_params=_params(("arbitrary",)),
    )(xs, wa, ba, wx, bx, lam)


def _rglru_bwd(xs, h_prev, dh, wa, ba, wx, bx, lam, *, reverse, n_ctx, name):
    rows = xs.shape[0]
    tb = _tile(math.gcd(rows, n_ctx), RNN_TB, SCAN_ROWS)
    nb, nb_c = rows // tb, n_ctx // tb
    back = not reverse
    block_of = _scan_block_order(nb, nb_c, reverse, True)
    order = lambda i: (block_of(i), 0)

    def body(x_ref, hp_ref, dh_ref, wa_ref, ba_ref, wx_ref, bx_ref, lam_ref,
             dx_ref, dwa_ref, dba_ref, dwx_ref, dbx_ref, dlam_ref, a_s, g_s, state):
        @pl.when(pl.program_id(0) == 0)
        def _():
            state[...] = jnp.zeros_like(state)
            dwa_ref[...] = jnp.zeros_like(dwa_ref)
            dwx_ref[...] = jnp.zeros_like(dwx_ref)
            dba_ref[...] = jnp.zeros_like(dba_ref)
            dbx_ref[...] = jnp.zeros_like(dbx_ref)
            dlam_ref[...] = jnp.zeros_like(dlam_ref)

        xv = x_ref[...]
        r, gi, sp, a, s = _rglru_gates(xv, wa_ref, ba_ref, wx_ref, bx_ref, lam_ref)
        a_s[...] = a

        is_latent = block_of(pl.program_id(0)) >= nb_c

        def group(base, carry):
            av = a_s[pl.ds(base, SCAN_ROWS), :]
            dv = jnp.where(is_latent, dh_ref[pl.ds(base, SCAN_ROWS), :], 0.0)
            outs = [None] * SCAN_ROWS
            for k in range(SCAN_ROWS):
                r_ = SCAN_ROWS - 1 - k if back else k
                gt = dv[r_:r_ + 1, :] + carry
                outs[r_] = gt
                carry = av[r_:r_ + 1, :] * gt
            g_s[pl.ds(base, SCAN_ROWS), :] = jnp.concatenate(outs, axis=0)
            return carry

        state[0:1, :] = _scan_rows(tb, back, group, state[0:1, :])

        gv = g_s[...]
        d_a = gv * hp_ref[...]
        d_s = gv * (gi * xv)
        d_gi = gv * (s * xv)
        dx = gv * (s * gi)
        d_log_a = d_a * a - d_s * (a * a) / s
        d_r = d_log_a * (-RG_C * sp)
        lamv = lam_ref[...]
        d_sp = jnp.sum(d_log_a * (-RG_C * r), axis=0, keepdims=True)
        dlam_ref[...] += d_sp * (-_sigmoid(-lamv))
        d_zr = d_r * r * (1.0 - r)
        d_zi = d_gi * gi * (1.0 - gi)
        dba_ref[...] += jnp.sum(d_zr, axis=0, keepdims=True)
        dbx_ref[...] += jnp.sum(d_zi, axis=0, keepdims=True)
        xb = xv.astype(BF16)
        zrb, zib = d_zr.astype(BF16), d_zi.astype(BF16)
        parts = []
        for n in range(RNN_BLOCKS):
            cols = slice(n * RNN_BW, (n + 1) * RNN_BW)
            dwa_ref[n] += lax.dot_general(xb[:, cols], zrb[:, cols], (((0,), (0,)), ((), ())), preferred_element_type=F32)
            dwx_ref[n] += lax.dot_general(xb[:, cols], zib[:, cols], (((0,), (0,)), ((), ())), preferred_element_type=F32)
            parts.append(
                lax.dot_general(zrb[:, cols], wa_ref[n].astype(BF16), (((1,), (1,)), ((), ())), preferred_element_type=F32)
                + lax.dot_general(zib[:, cols], wx_ref[n].astype(BF16), (((1,), (1,)), ((), ())), preferred_element_type=F32))
        dx_ref[...] = dx + jnp.concatenate(parts, axis=-1)

    blk = pl.BlockSpec((tb, D_RNN), order)
    wspec = _full((RNN_BLOCKS, RNN_BW, RNN_BW))
    vec = _full((1, D_RNN))
    wshape = jax.ShapeDtypeStruct((RNN_BLOCKS, RNN_BW, RNN_BW), F32)
    vshape = jax.ShapeDtypeStruct((1, D_RNN), F32)
    dh_blk = pl.BlockSpec((tb, D_RNN), lambda i: (jnp.maximum(block_of(i) - nb_c, 0), 0))
    return pl.pallas_call(
        body, name=name, grid=(nb,), in_specs=[blk, blk, dh_blk, wspec, vec, wspec, vec, vec],
        out_specs=(blk, wspec, vec, wspec, vec, vec),
        out_shape=(jax.ShapeDtypeStruct((rows, D_RNN), F32), wshape, vshape, wshape, vshape, vshape),
        scratch_shapes=[pltpu.VMEM((tb, D_RNN), F32), pltpu.VMEM((tb, D_RNN), F32), pltpu.VMEM((SCAN_ROWS, D_RNN), F32)],
        compiler_params=_params(("arbitrary",)),
    )(xs, h_prev, dh, wa, ba, wx, bx, lam)


def _assemble_d_proj(dp_qk_l, dp_qk_c, dv_all, d_xr_l, d_xr_c, d_gate):
    n_lat, n_ctx = dp_qk_l.shape[0], dp_qk_c.shape[0]
    tr = _tile(math.gcd(n_lat, n_ctx), 256, 16)
    nb_l, nb_c = n_lat // tr, n_ctx // tr
    w_qk = ATTN_W + KV_W

    def body(ql_ref, qc_ref, dv_ref, xl_ref, xc_ref, g_ref, o_ref):
        i = pl.program_id(0)
        o_ref[:, w_qk:XR_COL0] = dv_ref[...].astype(o_ref.dtype)

        @pl.when(i < nb_l)
        def _():
            o_ref[:, :w_qk] = ql_ref[...]
            o_ref[:, XR_COL0:GATE_COL0] = xl_ref[...]
            o_ref[:, GATE_COL0:] = g_ref[...]

        @pl.when(i >= nb_l)
        def _():
            o_ref[:, :w_qk] = qc_ref[...]
            o_ref[:, XR_COL0:GATE_COL0] = xc_ref[...]
            o_ref[:, GATE_COL0:] = jnp.zeros((tr, D_RNN), o_ref.dtype)

    lat = lambda i: (jnp.minimum(i, nb_l - 1), 0)
    ctx = lambda i: (jnp.maximum(i - nb_l, 0), 0)
    return pl.pallas_call(
        body, name="assemble_d_proj", grid=(nb_l + nb_c,),
        in_specs=[pl.BlockSpec((tr, w_qk), lat), pl.BlockSpec((tr, w_qk), ctx),
                  pl.BlockSpec((tr, KV_W), lambda i: (jnp.where(i < nb_l, i + nb_c, i - nb_l), 0)),
                  pl.BlockSpec((tr, D_RNN), lat), pl.BlockSpec((tr, D_RNN), ctx), pl.BlockSpec((tr, D_RNN), lat)],
        out_specs=pl.BlockSpec((tr, GATE_COL0 + D_RNN), lambda i: (i, 0)),
        out_shape=jax.ShapeDtypeStruct((n_lat + n_ctx, GATE_COL0 + D_RNN), BF16),
        compiler_params=_params(("parallel",)),
    )(dp_qk_l, dp_qk_c, dv_all, d_xr_l, d_xr_c, d_gate)


def _gelu(z):
    return 0.5 * z * (1.0 + jnp.tanh(GELU_C * (z + 0.044715 * z * z * z)))


def _gelu_grad(z):
    t = jnp.tanh(GELU_C * (z + 0.044715 * z * z * z))
    return 0.5 * (1.0 + t) + 0.5 * z * (1.0 - t * t) * (GELU_C * (1.0 + 3.0 * 0.044715 * z * z))


GATE_COL0 = XR_COL0 + D_RNN


RNN_OUT_COLS = 512


def _rnn_out_specs(rows, hf_off, hb_off):
    tr = _tile(rows, 256, 16)
    assert hf_off % tr == 0 and hb_off % tr == 0 and GATE_COL0 % RNN_OUT_COLS == 0
    fo, bo, go = hf_off // tr, hb_off // tr, GATE_COL0 // RNN_OUT_COLS
    hf_spec = pl.BlockSpec((tr, RNN_OUT_COLS), lambda i, j: (i + fo, j))
    hb_spec = pl.BlockSpec((tr, RNN_OUT_COLS), lambda i, j: (i + bo, j))
    gate_spec = pl.BlockSpec((tr, RNN_OUT_COLS), lambda i, j: (i, j + go))
    out_spec = pl.BlockSpec((tr, RNN_OUT_COLS), lambda i, j: (i, j))
    return (rows // tr, D_RNN // RNN_OUT_COLS), hf_spec, hb_spec, gate_spec, out_spec


def _rnn_out_fwd(hf, hb, proj, hf_off, hb_off, cat):
    rows = proj.shape[0]
    grid, hf_spec, hb_spec, gate_spec, out_spec = _rnn_out_specs(rows, hf_off, hb_off)
    tr, col0 = out_spec.block_shape[0], ATTN_W // RNN_OUT_COLS

    def body(hf_ref, hb_ref, g_ref, _, o_ref):
        o_ref[...] = ((hf_ref[...] + hb_ref[...]) * _gelu(g_ref[...])).astype(o_ref.dtype)

    return pl.pallas_call(
        body, name="rnn_out_fwd", grid=grid, in_specs=[hf_spec, hb_spec, gate_spec, ANY],
        out_specs=pl.BlockSpec((tr, RNN_OUT_COLS), lambda i, j: (i, j + col0)),
        out_shape=jax.ShapeDtypeStruct(cat.shape, cat.dtype), input_output_aliases={3: 0},
        compiler_params=_params(("parallel", "parallel")),
    )(hf, hb, proj, cat)


def _rnn_out_bwd(d_cat, hf, hb, proj, hf_off, hb_off):
    rows = proj.shape[0]
    grid, hf_spec, hb_spec, gate_spec, out_spec = _rnn_out_specs(rows, hf_off, hb_off)
    do = ATTN_W // RNN_OUT_COLS

    def body(d_ref, hf_ref, hb_ref, g_ref, dh_ref, dg_ref):
        dv, gv = d_ref[...].astype(F32), g_ref[...]
        dh_ref[...] = dv * _gelu(gv)
        dg_ref[...] = (dv * (hf_ref[...] + hb_ref[...]) * _gelu_grad(gv)).astype(dg_ref.dtype)

    tr = out_spec.block_shape[0]
    return pl.pallas_call(
        body, name="rnn_out_bwd", grid=grid,
        in_specs=[pl.BlockSpec((tr, RNN_OUT_COLS), lambda i, j: (i, j + do)), hf_spec, hb_spec, gate_spec],
        out_specs=(out_spec, out_spec),
        out_shape=(jax.ShapeDtypeStruct((rows, D_RNN), F32), jax.ShapeDtypeStruct((rows, D_RNN), BF16)),
        compiler_params=_params(("parallel", "parallel")),
    )(d_cat, hf, hb, proj)


def _gmlp_parts(z_ref, vg_ref, vb_ref, d_gm):
    zu, zv = z_ref[:, :d_gm], z_ref[:, d_gm:]
    u = _gelu(zu)
    v = _gelu(zv)
    mu = jnp.mean(v, axis=-1, keepdims=True)
    vc = v - mu
    rstd = lax.rsqrt(jnp.mean(vc * vc, axis=-1, keepdims=True) + EPS)
    vhat = vc * rstd
    vn = vhat * vg_ref[...] + vb_ref[...]
    return zu, zv, u, vhat, rstd, vn


def _gmlp_fwd(z, v_g, v_b, w_sp, b_sp_t):
    rows, d_gm = z.shape[0], z.shape[1] // 2
    tr = _tile(rows, 256, CHUNK)
    gwid = d_gm // GM_GROUPS

    def body(z_ref, vg_ref, vb_ref, w_ref, b_ref, o_ref):
        _, _, u, _, _, vn = _gmlp_parts(z_ref, vg_ref, vb_ref, d_gm)
        vnb = vn.astype(BF16)
        for g in range(GM_GROUPS):
            wg = w_ref[g].astype(BF16)
            for c in range(tr // CHUNK):
                rs, cs = slice(c * CHUNK, (c + 1) * CHUNK), slice(g * gwid, (g + 1) * gwid)
                sv = jnp.dot(wg, vnb[rs, cs], preferred_element_type=F32) + b_ref[:, g:g + 1]
                o_ref[rs, cs] = (u[rs, cs] * sv).astype(o_ref.dtype)

    return pl.pallas_call(
        body, name="gmlp_fwd", grid=(rows // tr,),
        in_specs=[pl.BlockSpec((tr, 2 * d_gm), lambda i: (i, 0)), _full((1, d_gm)), _full((1, d_gm)),
                  _full(w_sp.shape), _full(b_sp_t.shape)],
        out_specs=pl.BlockSpec((tr, d_gm), lambda i: (i, 0)),
        out_shape=jax.ShapeDtypeStruct((rows, d_gm), BF16), compiler_params=_params(("parallel",)),
    )(z, v_g, v_b, w_sp, b_sp_t)


def _gmlp_bwd(z, dgate, v_g, v_b, w_sp, b_sp_t):
    rows, d_gm = z.shape[0], z.shape[1] // 2
    tr = _tile(rows, 256, CHUNK)
    gwid = d_gm // GM_GROUPS

    def body(z_ref, dg_ref, vg_ref, vb_ref, w_ref, b_ref, dz_ref, dbin_ref, dvg_ref, dvb_ref, dw_ref, dbs_ref, dvn_s):
        @pl.when(pl.program_id(0) == 0)
        def _():
            dbin_ref[...] = jnp.zeros_like(dbin_ref)
            dvg_ref[...] = jnp.zeros_like(dvg_ref)
            dvb_ref[...] = jnp.zeros_like(dvb_ref)
            dw_ref[...] = jnp.zeros_like(dw_ref)
            dbs_ref[...] = jnp.zeros_like(dbs_ref)

        zu, zv, u, vhat, rstd, vn = _gmlp_parts(z_ref, vg_ref, vb_ref, d_gm)
        vnb = vn.astype(BF16)
        dgv = dg_ref[...].astype(F32)
        dsv = dgv * u
        dsvb = dsv.astype(BF16)
        for g in range(GM_GROUPS):
            wg = w_ref[g].astype(BF16)
            cs = slice(g * gwid, (g + 1) * gwid)
            for c in range(tr // CHUNK):
                rs = slice(c * CHUNK, (c + 1) * CHUNK)
                sv = jnp.dot(wg, vnb[rs, cs], preferred_element_type=F32) + b_ref[:, g:g + 1]
                dz_ref[rs, cs] = (dgv[rs, cs] * sv * _gelu_grad(zu[rs, cs])).astype(dz_ref.dtype)
                dw_ref[g] += lax.dot_general(dsvb[rs, cs], vnb[rs, cs], (((1,), (1,)), ((), ())),
                                             preferred_element_type=F32)
                dbs_ref[:, g:g + 1] += jnp.sum(dsv[rs, cs], axis=-1, keepdims=True)
                dvn_s[rs, cs] = lax.dot_general(wg, dsvb[rs, cs], (((0,), (0,)), ((), ())), preferred_element_type=F32)
        dvn = dvn_s[...]
        dvg_ref[...] += jnp.sum(dvn * vhat, axis=0, keepdims=True)
        dvb_ref[...] += jnp.sum(dvn, axis=0, keepdims=True)
        dvh = dvn * vg_ref[...]
        dv = rstd * (dvh - jnp.mean(dvh, axis=-1, keepdims=True) - vhat * jnp.mean(dvh * vhat, axis=-1, keepdims=True))
        dzv = dv * _gelu_grad(zv)
        dz_ref[:, d_gm:] = dzv.astype(dz_ref.dtype)
        dbin_ref[:, d_gm:] += jnp.sum(dzv, axis=0, keepdims=True)
        dbin_ref[:, :d_gm] += jnp.sum(dz_ref[:, :d_gm].astype(F32), axis=0, keepdims=True)

    return pl.pallas_call(
        body, name="gmlp_bwd", grid=(rows // tr,),
        in_specs=[pl.BlockSpec((tr, 2 * d_gm), lambda i: (i, 0)), pl.BlockSpec((tr, d_gm), lambda i: (i, 0)),
                  _full((1, d_gm)), _full((1, d_gm)), _full(w_sp.shape), _full(b_sp_t.shape)],
        out_specs=(pl.BlockSpec((tr, 2 * d_gm), lambda i: (i, 0)), _full((1, 2 * d_gm)), _full((1, d_gm)),
                   _full((1, d_gm)), _full(w_sp.shape), _full(b_sp_t.shape)),
        out_shape=(jax.ShapeDtypeStruct((rows, 2 * d_gm), BF16), jax.ShapeDtypeStruct((1, 2 * d_gm), F32),
                   jax.ShapeDtypeStruct((1, d_gm), F32), jax.ShapeDtypeStruct((1, d_gm), F32),
                   jax.ShapeDtypeStruct(w_sp.shape, F32), jax.ShapeDtypeStruct(b_sp_t.shape, F32)),
        scratch_shapes=[pltpu.VMEM((tr, d_gm), F32)],
        compiler_params=_params(("arbitrary",)),
    )(z, dgate, v_g, v_b, w_sp, b_sp_t)


def _adamw_math(w, g, m, v):
    m = ADAM_B1 * m + (1.0 - ADAM_B1) * g
    v = ADAM_B2 * v + (1.0 - ADAM_B2) * (g * g)
    m_hat = m / (1.0 - ADAM_B1 ** ADAM_STEP)
    v_hat = v / (1.0 - ADAM_B2 ** ADAM_STEP)
    delta = -ADAM_LR * (m_hat / (jnp.sqrt(v_hat) + ADAM_EPS) + ADAM_WD * w)
    return delta, m, v


def _adamw(w, g, m, v, name, rewrite_grad=False):
    shape = w.shape
    if rewrite_grad:
        outs = _rowwise(lambda w_, g_, m_, v_: (g_,) + _adamw_math(w_, g_, m_, v_), (F32,) * 4, _as2d(w), _as2d(g), _as2d(m),
                        _as2d(v), name=name)
        return tuple(o.reshape(shape) for o in outs)
    outs = _rowwise(_adamw_math, (F32, F32, F32), _as2d(w), _as2d(g), _as2d(m), _as2d(v), name=name)
    return (g.reshape(shape),) + tuple(o.reshape(shape) for o in outs)


PACK_COLS = 1024


def _pack(arrays, dtype=F32):
    flat = jnp.concatenate([a.reshape(-1).astype(dtype) for a in arrays])
    pad = (-flat.size) % (16 * PACK_COLS)
    return jnp.pad(flat, (0, pad)).reshape(-1, PACK_COLS)


def _into_slot(pack, dev, name):
    rows, cols = pack.shape
    tr = _rows_tile(rows, cols, budget=512 * 1024)

    def body(dev_ref, p_ref, o_ref):
        o_ref[...] = p_ref[...]

    return pl.pallas_call(
        body, name=name, out_shape=jax.ShapeDtypeStruct((N_DEV, rows, cols), pack.dtype),
        grid_spec=pltpu.PrefetchScalarGridSpec(
            num_scalar_prefetch=1, grid=(rows // tr,), in_specs=[pl.BlockSpec((tr, cols), lambda i, dv: (i, 0))],
            out_specs=pl.BlockSpec((None, tr, cols), lambda i, dv: (dv[0], i, 0))),
        compiler_params=_params(("parallel",)),
    )(dev, pack)


def _unpack(flat, shapes):
    out, pos = [], 0
    for shp in shapes:
        n = math.prod(shp)
        out.append(flat[pos:pos + n].reshape(shp))
        pos += n
    return out


def _unpack_devices(packed8, shapes):
    flat8 = packed8.reshape(N_DEV, -1)
    out, pos = [], 0
    for shp in shapes:
        n = math.prod(shp)
        out.append(flat8[:, pos:pos + n].reshape((N_DEV,) + tuple(shp)))
        pos += n
    return out


def _sum_devices(g8):
    _, rows, cols = g8.shape
    tr = _rows_tile(rows, cols, budget=256 * 1024)

    def body(g_ref, o_ref):
        acc = g_ref[0].astype(F32)
        for d in range(1, N_DEV):
            acc = acc + g_ref[d].astype(F32)
        o_ref[...] = acc

    return pl.pallas_call(
        body, name="sum_devices", grid=(rows // tr,), in_specs=[pl.BlockSpec((N_DEV, tr, cols), lambda i: (0, i, 0))],
        out_specs=pl.BlockSpec((tr, cols), lambda i: (i, 0)), out_shape=jax.ShapeDtypeStruct((rows, cols), F32),
        compiler_params=_params(("parallel",)),
    )(g8)


def _place():
    return lax.axis_index("x"), lax.axis_index("y"), lax.axis_index("c")


def _other_chips(x, y):
    return [(1 - x, y), (x, 1 - y), (1 - x, 1 - y)]


def _remote(src, dst, send_sem, recv_sem, to):
    return pltpu.make_async_remote_copy(src_ref=src, dst_ref=dst, send_sem=send_sem, recv_sem=recv_sem, device_id=to,
                                        device_id_type=MESH)


def _comm_call(body, name, operands, out_shapes, n_remote, n_local, aliases=None):
    return pl.pallas_call(
        body, name=name, out_shape=tuple(out_shapes), in_specs=[ANY] * len(operands), out_specs=tuple(ANY for _ in out_shapes),
        scratch_shapes=[pltpu.SemaphoreType.DMA((n_remote,)), pltpu.SemaphoreType.DMA((n_remote,)),
                        pltpu.SemaphoreType.DMA((max(n_local, 1),))],
        input_output_aliases=aliases or {},
    )(*operands)


def _in_place(arrays):
    return [jax.ShapeDtypeStruct(a.shape, a.dtype) for a in arrays], {i: i for i in range(len(arrays))}


def _allgather8(arrs, name):
    n = len(arrs)

    def body(*refs):
        ins, outs = refs[:n], refs[n:2 * n]
        send, recv, lsem = refs[2 * n:]
        x, y, c = _place()
        me, sib = (x, y, c), (x, y, 1 - c)
        chips = _other_chips(x, y)

        def slot(t, px, py, pc):
            return outs[t].at[4 * px + 2 * py + pc]

        def cp(t, k, block, to, from_input=False):
            src = ins[t] if from_input else slot(t, *block)
            return _remote(src, slot(t, *block), send.at[7 * t + k], recv.at[7 * t + k], to)

        mine = [pltpu.make_async_copy(ins[t], slot(t, *me), lsem.at[t]) for t in range(n)]
        for cpy in mine:
            cpy.start()
        first = []
        for t in range(n):
            first.append(cp(t, 0, me, sib, True))
            first += [cp(t, 1 + j, me, (*chip, c), True) for j, chip in enumerate(chips)]
        for cpy in first:
            cpy.start()
        passed = []
        for t in range(n):
            for j, chip in enumerate(chips):
                cp(t, 1 + j, (*chip, c), me).wait_recv()
                fwd = cp(t, 4 + j, (*chip, c), sib)
                fwd.start()
                passed.append(fwd)
        for t in range(n):
            cp(t, 0, sib, me).wait_recv()
            for j, chip in enumerate(chips):
                cp(t, 4 + j, (*chip, 1 - c), me).wait_recv()
        for cpy in first + passed:
            cpy.wait_send()
        for cpy in mine:
            cpy.wait()

    outs = _comm_call(body, name, arrs, [jax.ShapeDtypeStruct((N_DEV,) + a.shape, a.dtype) for a in arrs], 7 * n, n)
    return list(outs)


def _gather_weights(bufs):
    n_u = len(bufs)

    def body(*refs):
        bufs_ = refs[n_u:2 * n_u]
        send, recv, _ = refs[2 * n_u:]
        x, y, c = _place()
        me, sib, q = (x, y, c), (x, y, 1 - c), 2 * x + y
        chips = _other_chips(x, y)
        sent = []
        for u in range(n_u):
            half = bufs_[u].shape[1] // 2
            mine = bufs_[u].at[q, pl.ds(c * half, half)]
            for j, chip in enumerate(chips):
                cpy = _remote(mine, mine, send.at[6 * u + j], recv.at[6 * u + j], (*chip, c))
                cpy.start()
                sent.append(cpy)
        for u in range(n_u):
            half = bufs_[u].shape[1] // 2
            for j, chip in enumerate(chips):
                landed = bufs_[u].at[2 * chip[0] + chip[1], pl.ds(c * half, half)]
                _remote(landed, landed, send.at[6 * u + j], recv.at[6 * u + j], me).wait_recv()
                cpy = _remote(landed, landed, send.at[6 * u + 3 + j], recv.at[6 * u + 3 + j], sib)
                cpy.start()
                sent.append(cpy)
        for u in range(n_u):
            half = bufs_[u].shape[1] // 2
            for j, chip in enumerate(chips):
                landed = bufs_[u].at[2 * chip[0] + chip[1], pl.ds((1 - c) * half, half)]
                _remote(landed, landed, send.at[6 * u + 3 + j], recv.at[6 * u + 3 + j], me).wait_recv()
        for cpy in sent:
            cpy.wait_send()

    shapes, aliases = _in_place(bufs)
    return list(_comm_call(body, "gather_weights", bufs, shapes, 6 * n_u, 0, aliases))


def _exchange_halves(grads):
    n = len(grads)

    def body(*refs):
        ins, outs = refs[:n], refs[n:2 * n]
        send, recv, _ = refs[2 * n:]
        x, y, c = _place()
        sib = (x, y, 1 - c)
        sent = []
        for k in range(n):
            half = ins[k].shape[1] // 2
            cpy = _remote(ins[k].at[pl.ds(0, N_CHIPS), pl.ds((1 - c) * half, half)], outs[k], send.at[k], recv.at[k], sib)
            cpy.start()
            sent.append(cpy)
        for cpy in sent:
            cpy.wait()

    shapes = [jax.ShapeDtypeStruct((N_CHIPS, g.shape[1] // 2, g.shape[2]), g.dtype) for g in grads]
    return list(_comm_call(body, "exchange_halves", grads, shapes, n, 0))


def _chips_all_to_all(sums):
    n = len(sums)

    def body(*refs):
        ins, outs = refs[:n], refs[n:2 * n]
        send, recv, _ = refs[2 * n:]
        x, y, c = _place()
        sent = []
        for k in range(n):
            for j, chip in enumerate(_other_chips(x, y)):
                cpy = _remote(ins[k].at[2 * chip[0] + chip[1]], outs[k].at[j], send.at[3 * k + j], recv.at[3 * k + j], (*chip, c))
                cpy.start()
                sent.append(cpy)
        for cpy in sent:
            cpy.wait()

    shapes = [jax.ShapeDtypeStruct((N_CHIPS - 1,) + s.shape[1:], s.dtype) for s in sums]
    return list(_comm_call(body, "chips_all_to_all", sums, shapes, 3 * n, 0))


def _join_halves(bufs):
    n = len(bufs)
    units = [(k, layer) for k in range(n) for layer in range(bufs[k].shape[0])]

    def body(*refs):
        bufs_ = refs[n:2 * n]
        send, recv, _ = refs[2 * n:]
        x, y, c = _place()
        sent = []
        for u, (k, layer) in enumerate(units):
            half = bufs_[k].shape[1] // 2
            mine = bufs_[k].at[layer, pl.ds(c * half, half)]
            cpy = _remote(mine, mine, send.at[u], recv.at[u], (x, y, 1 - c))
            cpy.start()
            sent.append(cpy)
        for u, (k, layer) in enumerate(units):
            half = bufs_[k].shape[1] // 2
            theirs = bufs_[k].at[layer, pl.ds((1 - c) * half, half)]
            _remote(theirs, theirs, send.at[u], recv.at[u], (x, y, c)).wait_recv()
        for cpy in sent:
            cpy.wait_send()

    shapes, aliases = _in_place(bufs)
    return list(_comm_call(body, "join_halves", bufs, shapes, len(units), 0, aliases))


def _add_halves(grad, other, place):
    _, rows, cols = grad.shape
    half = rows // 2
    tr = _rows_tile(half, cols, itemsize=2, budget=2 * 1024 * 1024)
    per_half = half // tr

    def body(place_ref, g_ref, o_ref, s_ref):
        s_ref[...] = (g_ref[...].astype(F32) + o_ref[...].astype(F32)).astype(s_ref.dtype)

    return pl.pallas_call(
        body, name="add_halves", out_shape=jax.ShapeDtypeStruct((N_CHIPS, half, cols), grad.dtype),
        grid_spec=pltpu.PrefetchScalarGridSpec(
            num_scalar_prefetch=1, grid=(N_CHIPS, per_half),
            in_specs=[pl.BlockSpec((None, tr, cols), lambda k, i, pr: (k, pr[1] * per_half + i, 0)),
                      pl.BlockSpec((None, tr, cols), lambda k, i, pr: (k, i, 0))],
            out_specs=pl.BlockSpec((None, tr, cols), lambda k, i, pr: (k, i, 0))),
        compiler_params=_params(("parallel", "parallel")),
    )(place, grad, other)


def _add_chips(sums, others, place, dest, layer, n_layers):
    _, half, cols = sums.shape
    tr = _rows_tile(half, cols, itemsize=4, budget=2 * 1024 * 1024)
    per_half = half // tr

    def body(place_ref, s_ref, o_ref, *rest):
        acc = s_ref[...].astype(F32)
        for j in range(N_CHIPS - 1):
            acc = acc + o_ref[j].astype(F32)
        rest[-1][...] = acc

    operands = [place, sums, others] + ([] if dest is None else [dest])
    return pl.pallas_call(
        body, name="add_chips", out_shape=jax.ShapeDtypeStruct((n_layers, 2 * half, cols), F32),
        grid_spec=pltpu.PrefetchScalarGridSpec(
            num_scalar_prefetch=1, grid=(per_half,),
            in_specs=[pl.BlockSpec((None, tr, cols), lambda i, pr: (pr[0], i, 0)),
                      pl.BlockSpec((N_CHIPS - 1, tr, cols), lambda i, pr: (0, i, 0))] + ([] if dest is None else [ANY]),
            out_specs=pl.BlockSpec((None, tr, cols), lambda i, pr: (layer, pr[1] * per_half + i, 0))),
        input_output_aliases={} if dest is None else {3: 0},
        compiler_params=_params(("parallel",)),
    )(*operands)


HBM = pl.BlockSpec(memory_space=pltpu.HBM)
SEM = pl.BlockSpec(memory_space=pltpu.SEMAPHORE)
DATAFLOW = pltpu.SideEffectType.DATAFLOW_SIDE_EFFECTING


def _split_start(name, bufs, copies, n_copies, after=None):
    n = len(bufs)
    extra = 0 if after is None else 1

    def body(*refs):
        for cpy in copies(refs[:n], refs[n + extra], refs[n + extra + 1]):
            cpy.start()
        refs[-1][...] = jnp.zeros_like(refs[-1])

    outs = pl.pallas_call(
        body, name=name,
        out_shape=(pltpu.SemaphoreType.DMA((n_copies,)), pltpu.SemaphoreType.DMA((n_copies,)),
                   *[pltpu.HBM(b.shape, b.dtype) for b in bufs], jax.ShapeDtypeStruct((8, LANES), F32)),
        in_specs=[HBM] * n + [ANY] * extra,
        out_specs=(SEM, SEM, *[HBM] * n, pl.BlockSpec(memory_space=pltpu.VMEM)),
        input_output_aliases={i: 2 + i for i in range(n)},
        compiler_params=pltpu.CompilerParams(has_side_effects=DATAFLOW),
    )(*[pltpu.with_memory_space_constraint(b, pltpu.HBM) for b in bufs], *([] if after is None else [after]))
    return outs[0], outs[1], list(outs[2:2 + n]), outs[-1]


def _split_wait(name, bufs, send, recv, copies, after):
    n = len(bufs)

    def body(*refs):
        for cpy in copies(refs[:n], refs[n], refs[n + 1]):
            cpy.wait_send()
            cpy.wait_recv()

    return list(pl.pallas_call(
        body, name=name, out_shape=tuple(pltpu.HBM(b.shape, b.dtype) for b in bufs),
        in_specs=[HBM] * n + [SEM, SEM, ANY], out_specs=tuple([HBM] * n),
        input_output_aliases={i: i for i in range(n)},
        compiler_params=pltpu.CompilerParams(has_side_effects=DATAFLOW),
    )(*bufs, send, recv, after))


def _gather_copies(bufs, send, recv):
    x, y, c = _place()
    out = []
    for u, buf in enumerate(bufs):
        half = buf.shape[1] // 2
        mine = buf.at[2 * x + y, pl.ds(c * half, half)]
        out += [_remote(mine, mine, send.at[3 * u + j], recv.at[3 * u + j], (*chip, c))
                for j, chip in enumerate(_other_chips(x, y))]
    return out


def _exchange_copies(bufs, send, recv):
    x, y, c = _place()
    n = len(bufs) // 2
    out = []
    for k in range(n):
        half = bufs[k].shape[1] // 2
        theirs = bufs[k].at[pl.ds(0, N_CHIPS), pl.ds((1 - c) * half, half)]
        out.append(_remote(theirs, bufs[n + k], send.at[k], recv.at[k], (x, y, 1 - c)))
    return out


def _all_to_all_copies(bufs, send, recv):
    x, y, c = _place()
    n = len(bufs) // 2
    return [_remote(bufs[k].at[2 * chip[0] + chip[1]], bufs[n + k].at[j], send.at[3 * k + j], recv.at[3 * k + j], (*chip, c))
            for k in range(n) for j, chip in enumerate(_other_chips(x, y))]


def _forward_copies(bufs, send, recv):
    x, y, c = _place()
    out = []
    for u, buf in enumerate(bufs):
        half = buf.shape[1] // 2
        for j, chip in enumerate(_other_chips(x, y)):
            landed = buf.at[2 * chip[0] + chip[1], pl.ds(c * half, half)]
            out.append(_remote(landed, landed, send.at[3 * u + j], recv.at[3 * u + j], (x, y, 1 - c)))
    return out


def _gather8_copies(bufs, send, recv):
    x, y, c = _place()
    targets = [(x, y, 1 - c)] + [(*chip, c) for chip in _other_chips(x, y)]
    out = []
    for b, buf in enumerate(bufs):
        mine = buf.at[4 * x + 2 * y + c]
        out += [_remote(mine, mine, send.at[N_CHIPS * b + k], recv.at[N_CHIPS * b + k], to) for k, to in enumerate(targets)]
    return out


def _forward_slots(bufs, name):
    n = len(bufs)

    def body(*refs):
        bufs_ = refs[n:2 * n]
        send, recv, _ = refs[2 * n:]
        x, y, c = _place()
        chips = _other_chips(x, y)
        sent = []
        for b in range(n):
            for j, chip in enumerate(chips):
                slot = bufs_[b].at[4 * chip[0] + 2 * chip[1] + c]
                cpy = _remote(slot, slot, send.at[3 * b + j], recv.at[3 * b + j], (x, y, 1 - c))
                cpy.start()
                sent.append(cpy)
        for b in range(n):
            for j, chip in enumerate(chips):
                slot = bufs_[b].at[4 * chip[0] + 2 * chip[1] + 1 - c]
                _remote(slot, slot, send.at[3 * b + j], recv.at[3 * b + j], (x, y, c)).wait_recv()
        for cpy in sent:
            cpy.wait_send()

    shapes, aliases = _in_place(bufs)
    return list(_comm_call(body, name, bufs, shapes, (N_CHIPS - 1) * n, 0, aliases))


FWD_GROUPS = {'mix': ('ar_out', 'ff_in0', 'ff_out0'), 'l1': ('gm_in', 'gm_out', 'ff_in1', 'ff_out1')}
GRAD_LAYOUT = {'ff_in0': (0, 0), 'ff_in1': (0, 1), 'ff_out0': (1, 0), 'ff_out1': (1, 1), 'ar_in': (2, 0), 'ar_out': (3, 0),
               'gm_in': (4, 0), 'gm_out': (5, 0)}


class _MeshLink:
    def __init__(self, place, shards):
        self.place = place
        first = _gather_weights([shards['ar_in']])
        self.ready = {'ar_in': first[0]}
        self.pending, after = {}, first[0]
        for group, names in FWD_GROUPS.items():
            send, recv, bufs, token = _split_start(f"gather_{group}_start", [shards[n] for n in names], _gather_copies,
                                                   3 * len(names), after)
            self.pending[group] = (names, send, recv, bufs)
            after = token
        self.start_token = after[0, 0]
        self.forwarding, self.exchanging, self.sent, self.last_token = {}, {}, {}, None

    def prefetch(self, group, after):
        names, send, recv, bufs = self.pending.pop(group)
        bufs = _split_wait(f"gather_{group}_wait", bufs, send, recv, _gather_copies, after)
        send, recv, bufs, token = _split_start(f"forward_{group}_start", bufs, _forward_copies, 3 * len(names))
        self.forwarding[group] = (names, send, recv, bufs)
        return token[0, 0]

    def weights(self, group, after):
        if group in self.forwarding:
            names, send, recv, bufs = self.forwarding.pop(group)
            self.ready.update(zip(names, _split_wait(f"forward_{group}_wait", bufs, send, recv, _forward_copies, after)))
        return self.ready

    def gradients(self, group, grads, after=None):
        tok = self.poll(next(iter(grads.values())))
        names, mine = list(grads), list(grads.values())
        landing = [lax.empty((N_CHIPS, g.shape[1] // 2, g.shape[2]), g.dtype) for g in mine]
        send, recv, bufs, token = _split_start(f"exchange_{group}_start", mine + landing, _exchange_copies, len(names), after)
        self.exchanging[group] = (names, send, recv, bufs)
        self.last_token = token
        return token[0, 0] + tok

    def poll(self, after):
        tok = 0.0
        for group in list(self.exchanging):
            names, send, recv, bufs = self.exchanging.pop(group)
            bufs = _split_wait(f"exchange_{group}_wait", bufs, send, recv, _exchange_copies, after)
            sums = [_add_halves(g, r, self.place) for g, r in zip(bufs[:len(names)], bufs[len(names):])]
            landing = [lax.empty((N_CHIPS - 1,) + s.shape[1:], s.dtype) for s in sums]
            send, recv, bufs, token = _split_start(f"grads_{group}_start", sums + landing, _all_to_all_copies, 3 * len(names))
            self.sent[group] = (names, send, recv, bufs)
            self.last_token = token
            tok = tok + token[0, 0]
        return tok

    def reduce(self, groups, after):
        units = {}
        for group in groups:
            names, send, recv, bufs = self.sent.pop(group)
            bufs = _split_wait(f"grads_{group}_wait", bufs, send, recv, _all_to_all_copies, after)
            units.update(zip(names, zip(bufs[:len(names)], bufs[len(names):])))
        n_layers = {p: 1 + max(l for pp, l in GRAD_LAYOUT.values() if pp == p) for p, _ in GRAD_LAYOUT.values()}
        out = {}
        for name, (p, layer) in GRAD_LAYOUT.items():
            if name in units:
                out[p] = _add_chips(*units[name], self.place, out.get(p), layer, n_layers[p])
        params = sorted(out)
        return dict(zip(params, _join_halves([out[p] for p in params])))


def _rope_tables(n):
    n_rows = n // GRID_W
    freqs = ROPE_THETA ** (-jnp.arange(ROPE_PAIRS, dtype=F32) / ROPE_PAIRS)
    ang_r = jnp.arange(n_rows, dtype=F32)[:, None] * freqs
    ang_c = jnp.arange(GRID_W, dtype=F32)[:, None] * freqs

    def per_token(of_row, of_col):
        r = jnp.broadcast_to(of_row[:, None, :], (n_rows, GRID_W, ROPE_PAIRS)).reshape(n, ROPE_PAIRS)
        c = jnp.broadcast_to(of_col[None, :, :], (n_rows, GRID_W, ROPE_PAIRS)).reshape(n, ROPE_PAIRS)
        return r, c

    cos_r, cos_c = per_token(jnp.cos(ang_r), jnp.cos(ang_c))
    sin_r, sin_c = per_token(jnp.sin(ang_r), jnp.sin(ang_c))
    cos = jnp.concatenate([cos_r, cos_r, cos_c, cos_c], axis=-1)
    sin = jnp.concatenate([-sin_r, sin_r, -sin_c, sin_c], axis=-1)
    return cos, sin


def _ffn_fwd(h2, w1, w2, tag):
    r, a = _matmul(h2, w1, kind='nn', b_split='n', out_dtype=BF16, epilogue='relu2', name=f"ffn_in_{tag}")
    f = _matmul(a, w2, kind='nn', b_split='k', out_dtype=F32, name=f"ffn_out_{tag}")
    return r, a, f


def _ffn_bwd(d_f, h2, r, a, w1, w2, tag):
    d_u = _matmul(d_f, w2, kind='nt', b_split='k', out_dtype=BF16, epilogue='times2x', extra=r, name=f"ffn_out_dx_{tag}")
    d_w2 = _matmul(a, d_f, kind='tn', out_split='k', out_dtype=BF16, name=f"ffn_out_dw_{tag}")
    d_w1 = _matmul(h2, d_u, kind='tn', out_split='n', out_dtype=BF16, name=f"ffn_in_dw_{tag}")
    d_h2 = _matmul(d_u, w1, kind='nt', b_split='n', out_dtype=F32, name=f"ffn_in_dx_{tag}")
    return d_h2, d_w1, d_w2


class _LocalLink:
    def __init__(self, big):
        self.big, self.grads, self.start_token = big, {}, 0.0

    def prefetch(self, group, after):
        return 0.0

    def poll(self, after):
        return 0.0

    def weights(self, group, after):
        return self.big

    def gradients(self, group, grads):
        self.grads.update(grads)
        return 0.0


def _local_step(xl0, xc0, target, ml, mc0, sp, link):
    n_lat, n_ctx = xl0.shape[0], xc0.shape[0]
    one = lambda v: 1.0 + v
    g = [[sp['norm_g'][i, k][None, :] for k in range(4)] for i in range(2)]

    sh1, sc1, gt1, sh2, sc2, gt2 = ml[0]
    big = link.weights('ar', None)
    sh1 = sh1 + link.start_token
    n_all = n_lat + n_ctx
    h_all = _norm_fwd(xl0, g[0][0], one(sc1), b=sh1, out_dtype=BF16, name="l0_mod1", into=(0, n_all, None))
    h_all = _norm_fwd(xc0, g[0][0], one(mc0[1]), b=mc0[0], out_dtype=BF16, name="l0_mod1_ctx", into=(n_lat, n_all, h_all))
    proj_l = _matmul(h_all, big['ar_in'], kind='nn', b_split='n', out_dtype=F32, a_rows=(0, n_lat), name="ar_in_lat")
    proj_c = _matmul(h_all, big['ar_in'], kind='nn', b_split='n', out_dtype=F32, a_rows=(n_lat, n_ctx), name="ar_in_ctx")
    cos_l, sin_l = _rope_tables(n_lat)
    cos_c, sin_c = jnp.ones((n_ctx, HEAD_DIM), F32), jnp.zeros((n_ctx, HEAD_DIM), F32)
    q_g, k_g = sp['q_g'], sp['k_g']
    _, k_all, v_all = _qk_fwd(proj_c, q_g, k_g, cos_c, sin_c, name="qk_fwd_ctx", kv_into=(0, n_all, None))
    q_l, k_all, v_all = _qk_fwd(proj_l, q_g, k_g, cos_l, sin_l, name="qk_fwd_lat", kv_into=(n_ctx, n_all, (k_all, v_all)))
    cat, lse = _attn_fwd(q_l, k_all, v_all)
    conv_b = sp['conv_b'] + link.prefetch('mix', cat)
    xs = _conv_fwd(proj_l, proj_c, sp['conv_w'], conv_b)
    rnn_w = [(sp['wa'][d], sp['ba'][d][None, :], sp['wx'][d], sp['bx'][d][None, :], sp['lam'][d][None, :]) for d in range(2)]
    h_f, hp_f = _rglru_fwd(xs, *rnn_w[0], reverse=False, n_ctx=n_ctx, name="rglru_fwd_f")
    h_r, hp_r = _rglru_fwd(xs, *rnn_w[1], reverse=True, n_ctx=n_ctx, name="rglru_fwd_r")
    cat = _rnn_out_fwd(h_f, h_r, proj_l, n_ctx, n_ctx, cat)
    w_mix = link.weights('mix', cat)
    ol0 = _matmul(cat, w_mix['ar_out'], kind='nn', b_split='k', out_dtype=F32, name="ar_out")
    xm0, h2_0 = _res_mod_fwd(ol0, xl0, g[0][1], gt1, g[0][2], one(sc2), sh2, name="l0_res1_mod2")
    r0, a0, f0 = _ffn_fwd(h2_0, w_mix['ff_in0'], w_mix['ff_out0'], "l0")
    th1, tc1, tg1, th2, tc2, tg2 = ml[1]
    xl1, hl1 = _res_mod_fwd(f0, xm0, g[0][3], gt2 + link.prefetch('l1', f0), g[1][0], one(tc1), th1, name="l0_res2_l1_mod1")
    w_l1 = link.weights('l1', xl1)
    z = _matmul(hl1, w_l1['gm_in'], kind='nn', b_split='n', bias=sp['gm_b_in'], out_dtype=F32, name="gm_in")
    b_sp_t = sp['gm_b_sp'].T
    gated = _gmlp_fwd(z, sp['gm_v_g'], sp['gm_v_b'], sp['gm_w_sp'], b_sp_t)
    ol1 = _matmul(gated, w_l1['gm_out'], kind='nn', b_split='k', out_dtype=F32, name="gm_out")
    xm1, h2_1 = _res_mod_fwd(ol1, xl1, g[1][1], tg1, g[1][2], one(tc2), th2, name="l1_res1_mod2")
    r1, a1, f1 = _ffn_fwd(h2_1, w_l1['ff_in1'], w_l1['ff_out1'], "l1")
    y = _norm_fwd(f1, g[1][3], tg2, res=xm1, out_dtype=F32, name="l1_res2")

    dy, loss = _loss_head(y, target)

    d_f1, dg13, d_tg2, _ = _norm_bwd(dy, f1, g[1][3], tg2, out_dtype=BF16, name="l1_res2_bwd")
    d_h2, dw_ff_in1, dw_ff_out1 = _ffn_bwd(d_f1, h2_1, r1, a1, w_l1['ff_in1'], w_l1['ff_out1'], "l1")
    tok = link.gradients('ffn1', {'ff_in1': dw_ff_in1, 'ff_out1': dw_ff_out1})
    dxm1, d_ol1, dg12, d_tc2, d_th2, dg11, d_tg1 = _mod_res_bwd(d_h2, xm1, g[1][2], one(tc2) + tok, dy, ol1, g[1][1], tg1,
                                                                name="l1_mod2_res1_bwd")
    d_gated = _matmul(d_ol1, w_l1['gm_out'], kind='nt', b_split='k', out_dtype=F32, name="gm_out_dx")
    dw_gm_out = _matmul(gated, d_ol1, kind='tn', out_split='k', out_dtype=BF16, name="gm_out_dw")
    d_z, d_gm_b_in, d_vg, d_vb, d_wsp, d_bsp_t = _gmlp_bwd(z, d_gated, sp['gm_v_g'], sp['gm_v_b'], sp['gm_w_sp'], b_sp_t)
    dw_gm_in = _matmul(hl1, d_z, kind='tn', out_split='n', out_dtype=BF16, name="gm_in_dw")
    d_hl1 = _matmul(d_z, w_l1['gm_in'], kind='nt', b_split='n', out_dtype=F32, name="gm_in_dx")
    tok = link.gradients('gm', {'gm_in': dw_gm_in, 'gm_out': dw_gm_out})

    dxl1, d_f0, dg10, d_tc1, d_th1, dg03, d_gt2 = _mod_res_bwd(d_hl1, xl1, g[1][0], one(tc1) + tok, dxm1, f0, g[0][3], gt2,
                                                               name="l1_mod1_l0_res2_bwd")
    d_h2, dw_ff_in0, dw_ff_out0 = _ffn_bwd(d_f0, h2_0, r0, a0, w_mix['ff_in0'], w_mix['ff_out0'], "l0")
    tok = link.gradients('ffn0', {'ff_in0': dw_ff_in0, 'ff_out0': dw_ff_out0})
    dxm0, d_ol0, dg02, d_sc2, d_sh2, dg01, d_gt1 = _mod_res_bwd(d_h2, xm0, g[0][2], one(sc2) + tok, dxl1, ol0, g[0][1], gt1,
                                                                name="l0_mod2_res1_bwd")
    d_cat = _matmul(d_ol0, w_mix['ar_out'], kind='nt', b_split='k', out_dtype=F32, name="ar_out_dx")
    dw_ar_out = _matmul(cat, d_ol0, kind='tn', out_split='k', out_dtype=BF16, name="ar_out_dw")
    dq, dk_all, dv_all = _attn_bwd(q_l, k_all, v_all, cat, lse, d_cat)
    tok = link.poll(dq)
    d_h, d_gate = _rnn_out_bwd(d_cat, h_f, h_r, proj_l, n_ctx, n_ctx)
    rnn_wb = [(wa_, ba_ + tok, wx_, bx_, lam_) for wa_, ba_, wx_, bx_, lam_ in rnn_w]
    dxs_f, d_wa0, d_ba0, d_wx0, d_bx0, d_lam0 = _rglru_bwd(
        xs, hp_f, d_h, *rnn_wb[0], reverse=False, n_ctx=n_ctx, name="rglru_bwd_f")
    dxs_r, d_wa1, d_ba1, d_wx1, d_bx1, d_lam1 = _rglru_bwd(
        xs, hp_r, d_h, *rnn_wb[1], reverse=True, n_ctx=n_ctx, name="rglru_bwd_r")
    d_xr_l, d_xr_c, d_cw, d_cb = _conv_bwd(dxs_f, dxs_r, proj_l, proj_c, sp['conv_w'])
    dp_qk_l, d_qg, d_kg_l = _qk_bwd(dq, dk_all, proj_l, q_g, k_g, cos_l, sin_l, name="qk_bwd_lat", dk_row0=n_ctx)
    dp_qk_c, _, d_kg_c = _qk_bwd(None, dk_all, proj_c, q_g, k_g, cos_c, sin_c, name="qk_bwd_ctx")
    d_proj = _assemble_d_proj(dp_qk_l, dp_qk_c, dv_all, d_xr_l, d_xr_c, d_gate)
    dw_ar_in = _matmul(h_all, d_proj, kind='tn', out_split='n', out_dtype=BF16, name="ar_in_dw")
    d_hl = _matmul(d_proj, big['ar_in'], kind='nt', b_split='n', out_dtype=F32, a_rows=(0, n_lat), name="ar_in_dx_lat")
    d_hc = _matmul(d_proj, big['ar_in'], kind='nt', b_split='n', out_dtype=F32, a_rows=(n_lat, n_ctx), name="ar_in_dx_ctx")
    grad_x, dg00, d_sc1, d_sh1 = _norm_bwd(d_hl, xl0, g[0][0], one(sc1), extra=dxm0, out_dtype=F32, name="l0_mod1_bwd")
    _, dg00c, d_mc_scale, d_mc_shift = _norm_bwd(d_hc, xc0, g[0][0], one(mc0[1]), out_dtype=BF16, name="l0_mod1_ctx_bwd")

    zeros_d = jnp.zeros_like(d_sh1)
    small = {
        'd_ml0': jnp.concatenate([d_sh1, d_sc1, d_gt1, d_sh2, d_sc2, d_gt2], axis=1),
        'd_ml1': jnp.concatenate([d_th1, d_tc1, d_tg1, d_th2, d_tc2, d_tg2], axis=1),
        'd_mc0': jnp.concatenate([d_mc_shift, d_mc_scale] + [zeros_d] * 4, axis=1),
        'norm_g': jnp.stack([jnp.concatenate([dg00 + dg00c, dg01, dg02, dg03], axis=0),
                             jnp.concatenate([dg10, dg11, dg12, dg13], axis=0)]),
        'q_g': d_qg, 'k_g': d_kg_l + d_kg_c, 'conv_w': d_cw, 'conv_b': d_cb,
        'wa': jnp.stack([d_wa0, d_wa1]), 'ba': jnp.concatenate([d_ba0, d_ba1], axis=0),
        'wx': jnp.stack([d_wx0, d_wx1]), 'bx': jnp.concatenate([d_bx0, d_bx1], axis=0),
        'lam': jnp.concatenate([d_lam0, d_lam1], axis=0),
        'gm_b_in': d_gm_b_in, 'gm_v_g': d_vg, 'gm_v_b': d_vb, 'gm_w_sp': d_wsp, 'gm_b_sp': d_bsp_t.T,
        'loss': loss,
    }
    return grad_x, small, {'ar_in': dw_ar_in, 'ar_out': dw_ar_out}


MOD_ROWS = 16
SMALL_F32 = ('d_ml0', 'd_ml1', 'd_mc0', 'norm_g', 'q_g', 'k_g', 'conv_w', 'conv_b', 'ba', 'bx', 'lam', 'gm_b_in', 'gm_v_g',
             'gm_v_b', 'gm_b_sp', 'loss')
SMALL_BF16 = ('wa', 'wx', 'gm_w_sp')


def _silu(v):
    return v * _sigmoid(v)


def _chip_concat(gathered, axis):
    return jnp.concatenate([gathered[2 * q] for q in range(N_CHIPS)], axis=axis)


def kernel(x, c, ctx, c_ctx, w_mod, b_mod, norm_g, w_ff_in, w_ff_out, ar_w_in, ar_q_g, ar_k_g, ar_conv_w, ar_conv_b, ar_wa, ar_ba, ar_wx, ar_bx, ar_lambda, ar_w_out, gm_w_in, gm_b_in, gm_v_g, gm_v_b, gm_w_sp, gm_b_sp, gm_w_out, loss_target, m_c_ctx, m_w_mod, m_b_mod, m_norm_g, m_w_ff_in, m_w_ff_out, m_ar_w_in, m_ar_q_g, m_ar_k_g, m_ar_conv_w, m_ar_conv_b, m_ar_wa, m_ar_ba, m_ar_wx, m_ar_bx, m_ar_lambda, m_ar_w_out, m_gm_w_in, m_gm_b_in, m_gm_v_g, m_gm_v_b, m_gm_w_sp, m_gm_b_sp, m_gm_w_out, v_c_ctx, v_w_mod, v_b_mod, v_norm_g, v_w_ff_in, v_w_ff_out, v_ar_w_in, v_ar_q_g, v_ar_k_g, v_ar_conv_w, v_ar_conv_b, v_ar_wa, v_ar_ba, v_ar_wx, v_ar_bx, v_ar_lambda, v_ar_w_out, v_gm_w_in, v_gm_b_in, v_gm_v_g, v_gm_v_b, v_gm_w_sp, v_gm_b_sp, v_gm_w_out):
    weights = dict(c_ctx=c_ctx, w_mod=w_mod, b_mod=b_mod, norm_g=norm_g, w_ff_in=w_ff_in, w_ff_out=w_ff_out, ar_w_in=ar_w_in,
                   ar_q_g=ar_q_g, ar_k_g=ar_k_g, ar_conv_w=ar_conv_w, ar_conv_b=ar_conv_b, ar_wa=ar_wa, ar_ba=ar_ba, ar_wx=ar_wx,
                   ar_bx=ar_bx, ar_lambda=ar_lambda, ar_w_out=ar_w_out, gm_w_in=gm_w_in, gm_b_in=gm_b_in, gm_v_g=gm_v_g,
                   gm_v_b=gm_v_b, gm_w_sp=gm_w_sp, gm_b_sp=gm_b_sp, gm_w_out=gm_w_out)
    m_in = dict(c_ctx=m_c_ctx, w_mod=m_w_mod, b_mod=m_b_mod, norm_g=m_norm_g, w_ff_in=m_w_ff_in, w_ff_out=m_w_ff_out,
                ar_w_in=m_ar_w_in, ar_q_g=m_ar_q_g, ar_k_g=m_ar_k_g, ar_conv_w=m_ar_conv_w, ar_conv_b=m_ar_conv_b, ar_wa=m_ar_wa,
                ar_ba=m_ar_ba, ar_wx=m_ar_wx, ar_bx=m_ar_bx, ar_lambda=m_ar_lambda, ar_w_out=m_ar_w_out, gm_w_in=m_gm_w_in,
                gm_b_in=m_gm_b_in, gm_v_g=m_gm_v_g, gm_v_b=m_gm_v_b, gm_w_sp=m_gm_w_sp, gm_b_sp=m_gm_b_sp, gm_w_out=m_gm_w_out)
    v_in = dict(c_ctx=v_c_ctx, w_mod=v_w_mod, b_mod=v_b_mod, norm_g=v_norm_g, w_ff_in=v_w_ff_in, w_ff_out=v_w_ff_out,
                ar_w_in=v_ar_w_in, ar_q_g=v_ar_q_g, ar_k_g=v_ar_k_g, ar_conv_w=v_ar_conv_w, ar_conv_b=v_ar_conv_b, ar_wa=v_ar_wa,
                ar_ba=v_ar_ba, ar_wx=v_ar_wx, ar_bx=v_ar_bx, ar_lambda=v_ar_lambda, ar_w_out=v_ar_w_out, gm_w_in=v_gm_w_in,
                gm_b_in=v_gm_b_in, gm_v_g=v_gm_v_g, gm_v_b=v_gm_v_b, gm_w_sp=v_gm_w_sp, gm_b_sp=v_gm_b_sp, gm_w_out=v_gm_w_out)

    xi, yi, ci = lax.axis_index("x"), lax.axis_index("y"), lax.axis_index("c")
    chip = 2 * xi + yi
    dev = 4 * xi + 2 * yi + ci
    place = jnp.stack([chip, ci]).astype(jnp.int32)
    n_lat, d = x.shape[1], x.shape[2]
    d6 = 6 * d
    cols_mod = w_mod.shape[2]

    mine = [c, norm_g, ar_conv_w[0], ar_ba[0], ar_bx[0], ar_lambda[0], gm_b_in, gm_v_g, gm_v_b]
    gathered = _allgather8([_pack(mine)], "gather_small_params")[0]
    parts = _unpack_devices(gathered, [a.shape for a in mine])
    c_all = parts[0].reshape(N_DEV, d)
    sp = {'norm_g': _chip_concat(parts[1], 2), 'q_g': ar_q_g, 'k_g': ar_k_g, 'conv_w': _chip_concat(parts[2], 1),
          'conv_b': ar_conv_b, 'wa': ar_wa[0], 'ba': _chip_concat(parts[3], 1), 'wx': ar_wx[0], 'bx': _chip_concat(parts[4], 1),
          'lam': _chip_concat(parts[5], 1), 'gm_b_in': _chip_concat(parts[6], 1), 'gm_v_g': _chip_concat(parts[7], 1),
          'gm_v_b': _chip_concat(parts[8], 1), 'gm_w_sp': gm_w_sp[0], 'gm_b_sp': gm_b_sp[0]}

    def mod_operand(c_rows, cc):
        row = lax.broadcasted_iota(jnp.int32, (MOD_ROWS - N_DEV, d), 0)
        lower = jnp.where(row == 0, jnp.broadcast_to(_silu(cc), (MOD_ROWS - N_DEV, d)), 0.0)
        sig = _sigmoid(cc)
        return jnp.concatenate([_silu(c_rows), lower], axis=0), sig * (1.0 + cc * (1.0 - sig))

    s_mod, dsilu_ctx = _small(mod_operand, [((MOD_ROWS, d), F32), ((1, d), F32)], c_all, c_ctx[None, :], name="mod_operand")
    b_mod_mine = lax.dynamic_slice(b_mod, (0, chip * cols_mod), (2, cols_mod))
    mod = [_matmul(s_mod, w_mod, kind='nn', b_layer=i, bias=b_mod_mine[i][None, :], out_dtype=F32, name=f"mod_fwd_{i}")
           for i in range(2)]
    mod_all = _allgather8([jnp.concatenate(mod, axis=0)], "gather_mod")[0]
    mod_all = _chip_concat(mod_all, 1).reshape(2, MOD_ROWS, d6)
    ml = [jnp.split(lax.dynamic_slice(mod_all[i], (dev, 0), (1, d6)), 6, axis=1) for i in range(2)]
    mc0 = jnp.split(mod_all[0, N_DEV:N_DEV + 1], 6, axis=1)[:2]

    names = ('w_ff_in', 'w_ff_out', 'ar_w_in', 'ar_w_out', 'gm_w_in', 'gm_w_out')
    keys = {'w_ff_in': ('ff_in0', 'ff_in1'), 'w_ff_out': ('ff_out0', 'ff_out1'), 'ar_w_in': ('ar_in',), 'ar_w_out': ('ar_out',),
            'gm_w_in': ('gm_in',), 'gm_w_out': ('gm_out',)}
    shards = {key: _cast_shard(weights[n], place, layer, f"cast_{key}") for n in names for layer, key in enumerate(keys[n])}
    link = _MeshLink(place, shards)

    grad_x, small, last_grads = _local_step(x[0], ctx[0], loss_target[0], ml, mc0, sp, link)

    def step(n, grad):
        return _adamw(weights[n], grad.reshape(weights[n].shape), m_in[n], v_in[n], f"adamw_{n}", rewrite_grad=n in names)

    small_f32, small_bf16 = [small[k] for k in SMALL_F32], [small[k] for k in SMALL_BF16]
    dev_arr = dev.astype(jnp.int32)[None]
    slots = [_into_slot(_pack(small_f32), dev_arr, "small_grads_slot_f32"),
             _into_slot(_pack(small_bf16, BF16), dev_arr, "small_grads_slot_bf16")]
    s_send, s_recv, slots, s_token = _split_start("small_grads_start", slots, _gather8_copies, 2 * N_CHIPS, grad_x)
    link.gradients('ar', last_grads, s_token)
    link.poll(link.last_token)
    reduced = link.reduce(('ffn1', 'gm', 'ffn0'), link.last_token)
    stepped = {n: step(n, reduced[names.index(n)]) for n in ('w_ff_in', 'w_ff_out', 'gm_w_in', 'gm_w_out')}
    reduced = link.reduce(('ar',), stepped['gm_w_out'][1])
    stepped.update({n: step(n, reduced[names.index(n)]) for n in ('ar_w_in', 'ar_w_out')})
    slots = _split_wait("small_grads_wait", slots, s_send, s_recv, _gather8_copies, stepped['ar_w_out'][1])
    small8, small8_bf16 = _forward_slots(slots, "small_grads_forward")
    total = dict(zip(SMALL_F32, _unpack(_sum_devices(small8).reshape(-1), [a.shape for a in small_f32])))
    total.update(zip(SMALL_BF16, _unpack(_sum_devices(small8_bf16).reshape(-1), [a.shape for a in small_bf16])))
    per_dev = _unpack_devices(small8, [(d6,), (d6,)])
    pad_rows = jnp.zeros((MOD_ROWS - N_DEV - 1, d6), F32)
    d_mod = [jnp.concatenate([per_dev[0], total['d_mc0'], pad_rows], axis=0),
             jnp.concatenate([per_dev[1], jnp.zeros((MOD_ROWS - N_DEV, d6), F32)], axis=0)]
    d_mod_mine = [lax.dynamic_slice(dm, (0, chip * cols_mod), (MOD_ROWS, cols_mod)) for dm in d_mod]
    g_w_mod = None
    for i in range(2):
        g_w_mod = _matmul(s_mod, d_mod_mine[i], kind='tn', out_dtype=F32, out_stack=(i, 2, g_w_mod), name=f"mod_dw_{i}")
    d_s_part = _matmul(d_mod_mine[0], w_mod, kind='nt', b_layer=0, out_dtype=F32, name="mod_ds")
    d_s_all = _allgather8([d_s_part[N_DEV:]], "gather_mod_ds")[0]

    def c_ctx_grad(parts_, dsilu):
        acc = parts_[0, 0:1]
        for q in range(1, N_CHIPS):
            acc = acc + parts_[2 * q, 0:1]
        return (acc * dsilu,)

    g_c_ctx = _small(c_ctx_grad, [((1, d), F32)], d_s_all, dsilu_ctx, name="c_ctx_grad")[0].reshape(d)

    def mine_of(full_grad, axis, n_shard):
        return lax.dynamic_slice_in_dim(full_grad, chip * n_shard, n_shard, axis=axis)

    grads_out = {
        'c_ctx': g_c_ctx, 'w_mod': g_w_mod,
        'b_mod': jnp.stack([total['d_ml0'][0] + total['d_mc0'][0], total['d_ml1'][0]]),
        'norm_g': mine_of(total['norm_g'], 2, norm_g.shape[2]),
        'ar_q_g': total['q_g'], 'ar_k_g': total['k_g'], 'ar_conv_w': mine_of(total['conv_w'], 1, ar_conv_w.shape[2])[None],
        'ar_conv_b': total['conv_b'], 'ar_wa': total['wa'][None], 'ar_ba': mine_of(total['ba'], 1, ar_ba.shape[2])[None],
        'ar_wx': total['wx'][None], 'ar_bx': mine_of(total['bx'], 1, ar_bx.shape[2])[None],
        'ar_lambda': mine_of(total['lam'], 1, ar_lambda.shape[2])[None],
        'gm_b_in': mine_of(total['gm_b_in'], 1, gm_b_in.shape[1]),
        'gm_v_g': mine_of(total['gm_v_g'], 1, gm_v_g.shape[1]), 'gm_v_b': mine_of(total['gm_v_b'], 1, gm_v_b.shape[1]),
        'gm_w_sp': total['gm_w_sp'][None], 'gm_b_sp': total['gm_b_sp'][None],
    }
    stepped.update({n: step(n, grad) for n, grad in grads_out.items()})
    stepped = [stepped[n] for n in weights]
    loss = total['loss'].reshape(())
    return (loss, grad_x[None], *[s[0] for s in stepped], *[s[1] for s in stepped], *[s[2] for s in stepped],
            *[s[3] for s in stepped])
```

```python
import functools
import math

import jax
import jax.numpy as jnp
from jax import lax
from jax.experimental import pallas as pl
from jax.experimental.pallas import tpu as pltpu

F32 = jnp.float32
BF16 = jnp.bfloat16
MESH = pl.DeviceIdType.MESH
ANY = pl.BlockSpec(memory_space=pl.ANY)

VMEM_LIMIT_BYTES = 52 * 1024 * 1024
LANES = 128
N_CHIPS = 4
N_DEV = 8

HEAD_DIM = 128
N_HEADS = 8
N_KV = 2
GROUP = N_HEADS // N_KV
ATTN_W = N_HEADS * HEAD_DIM
KV_W = N_KV * HEAD_DIM
D_RNN = 1024
RNN_BLOCKS = 8
RNN_BW = D_RNN // RNN_BLOCKS
CONV_W = 4
RG_C = 8.0
GRID_W = 64
ROPE_THETA = 10000.0
ROPE_PAIRS = HEAD_DIM // 4
GM_GROUPS = 16
CHUNK = 128
EPS = 1e-6
ADAM_LR, ADAM_B1, ADAM_B2, ADAM_EPS, ADAM_WD, ADAM_STEP = 0.001, 0.9, 0.999, 1e-08, 0.01, 10
GELU_C = math.sqrt(2.0 / math.pi)
LOG2E = math.log2(math.e)


def _params(sem=None):
    return pltpu.CompilerParams(dimension_semantics=sem, vmem_limit_bytes=VMEM_LIMIT_BYTES)


def _tile(dim, pref, unit):
    best = None
    t = unit
    while t <= min(dim, pref):
        if dim % t == 0:
            best = t
        t += unit
    return best if best is not None else dim


def _full(shape):
    nd = len(shape)
    return pl.BlockSpec(shape, lambda *_: (0,) * nd)


def _blocked_map(split, per_q):
    if split == 'n':
        return lambda r, c: (c // per_q, r, c % per_q)
    if split == 'k':
        return lambda r, c: (r // per_q, r % per_q, c)
    return lambda r, c: (r, c)


def _logical_shape(arr, split):
    if split == 'n':
        return arr.shape[1], arr.shape[0] * arr.shape[2]
    if split == 'k':
        return arr.shape[0] * arr.shape[1], arr.shape[2]
    return arr.shape[-2:]


def _matmul(a, b, *, kind, name, out_dtype, b_split=None, out_split=None, bias=None, epilogue=None, extra=None,
            a_rows=None, b_layer=None, out_stack=None, pref=(1024, 1024, 2048)):
    if b_split == 'k':
        b, b_split = b.reshape(-1, b.shape[-1]), None
    blocked_rows_out = out_split == 'k'
    if blocked_rows_out:
        assert epilogue != 'relu2'
        out_split = None
    b_rows, b_cols = _logical_shape(b, b_split)
    row0 = 0
    if kind == 'nn':
        m, kc = a.shape
        n = b_cols
        assert b_rows == kc
    elif kind == 'nt':
        m, kc = a.shape
        n = b_rows
        assert b_cols == kc
    if a_rows is not None:
        assert kind != 'tn'
        row0, m = a_rows
    if kind == 'tn':
        kc, m = a.shape
        n = b_cols
        assert b_rows == kc
    b_row_ext = b.shape[1] if b_split == 'k' else b_rows
    b_col_ext = b.shape[2] if b_split == 'n' else b_cols
    out_row_ext = m // N_CHIPS if out_split == 'k' else m
    out_col_ext = n // N_CHIPS if out_split == 'n' else n
    if kind == 'nn':
        ti = _tile(math.gcd(min(m, out_row_ext), row0), pref[0], 16)
        tj = _tile(math.gcd(b_col_ext, out_col_ext), pref[1], LANES)
        tl = _tile(b_row_ext, pref[2], LANES)
        a_spec = pl.BlockSpec((ti, tl), lambda i, j, l: (i + row0 // ti, l))
        b_tile, b_rc = (tl, tj), (lambda i, j, l: (l, j))
        dims = (((1,), (0,)), ((), ()))
    elif kind == 'nt':
        ti = _tile(math.gcd(min(m, out_row_ext), row0), pref[0], 16)
        tj = _tile(math.gcd(b_row_ext, out_col_ext), pref[1], LANES)
        tl = _tile(b_col_ext, pref[2], LANES)
        a_spec = pl.BlockSpec((ti, tl), lambda i, j, l: (i + row0 // ti, l))
        b_tile, b_rc = (tj, tl), (lambda i, j, l: (j, l))
        dims = (((1,), (1,)), ((), ()))
    else:
        ti = _tile(out_row_ext, pref[0], LANES)
        tj = _tile(math.gcd(b_col_ext, out_col_ext), pref[1], LANES)
        tl = _tile(b_row_ext, pref[2], 16)
        a_spec = pl.BlockSpec((tl, ti), lambda i, j, l: (l, i))
        b_tile, b_rc = (tl, tj), (lambda i, j, l: (l, j))
        dims = (((0,), (0,)), ((), ()))
    grid = (m // ti, n // tj, kc // tl)
    n_l = grid[2]

    if b_layer is not None:
        b_spec = pl.BlockSpec((None,) + b_tile, lambda i, j, l: (b_layer, *b_rc(i, j, l)))
    elif b_split is None:
        b_spec = pl.BlockSpec(b_tile, b_rc)
    else:
        per_q = (b.shape[2] // b_tile[1]) if b_split == 'n' else (b.shape[1] // b_tile[0])
        bmap = _blocked_map(b_split, per_q)
        b_spec = pl.BlockSpec((None,) + b_tile, lambda i, j, l: bmap(*b_rc(i, j, l)))
    if out_stack is not None:
        layer, n_layers, _ = out_stack
        out_shape2 = (n_layers, m, n)
        o_spec = pl.BlockSpec((None, ti, tj), lambda i, j, l: (layer, i, j))
    elif out_split is None:
        out_shape2 = (m, n)
        o_spec = pl.BlockSpec((ti, tj), lambda i, j, l: (i, j))
    else:
        out_shape2 = (N_CHIPS, m // N_CHIPS, n) if out_split == 'k' else (N_CHIPS, m, n // N_CHIPS)
        per_q = (out_shape2[2] // tj) if out_split == 'n' else (out_shape2[1] // ti)
        omap = _blocked_map(out_split, per_q)
        o_spec = pl.BlockSpec((None, ti, tj), lambda i, j, l: omap(i, j))

    in_specs = [a_spec, b_spec]
    operands = [a, b]
    if bias is not None:
        in_specs.append(pl.BlockSpec((1, tj), lambda i, j, l: (0, j)))
        operands.append(bias)
    if extra is not None:
        in_specs.append(pl.BlockSpec((ti, tj), lambda i, j, l: (i, j)))
        operands.append(extra)
    if epilogue == 'relu2':
        out_shape = (jax.ShapeDtypeStruct(out_shape2, out_dtype), jax.ShapeDtypeStruct(out_shape2, out_dtype))
        out_specs = (o_spec, o_spec)
    else:
        out_shape = jax.ShapeDtypeStruct(out_shape2, out_dtype)
        out_specs = o_spec
    has_bias, has_extra = bias is not None, extra is not None
    has_dest = out_stack is not None and out_stack[2] is not None
    if has_dest:
        in_specs.append(ANY)
        operands.append(out_stack[2])

    def body(*refs):
        a_ref, b_ref = refs[0], refs[1]
        pos = 2
        bias_ref = extra_ref = None
        if has_bias:
            bias_ref = refs[pos]
            pos += 1
        if has_extra:
            extra_ref = refs[pos]
            pos += 1
        if has_dest:
            pos += 1
        outs = refs[pos:] if n_l == 1 else refs[pos:-1]

        def finish(acc):
            if has_bias:
                acc = acc + bias_ref[...]
            if epilogue == 'relu2':
                r = jnp.maximum(acc, 0.0)
                outs[0][...] = r.astype(outs[0].dtype)
                outs[1][...] = (r * r).astype(outs[1].dtype)
            elif epilogue == 'times2x':
                outs[0][...] = (acc * (2.0 * extra_ref[...].astype(F32))).astype(outs[0].dtype)
            else:
                outs[0][...] = acc.astype(outs[0].dtype)

        def product():
            return lax.dot_general(a_ref[...].astype(BF16), b_ref[...].astype(BF16), dims, preferred_element_type=F32)

        if n_l == 1:
            finish(product())
            return
        acc_ref = refs[-1]
        step = pl.program_id(2)

        @pl.when(step == 0)
        def _():
            acc_ref[...] = jnp.zeros_like(acc_ref)

        acc_ref[...] += product()

        @pl.when(step == n_l - 1)
        def _():
            finish(acc_ref[...])

    result = pl.pallas_call(
        body, name=name, grid=grid, in_specs=in_specs, out_specs=out_specs, out_shape=out_shape,
        input_output_aliases={len(operands) - 1: 0} if has_dest else {},
        scratch_shapes=[] if n_l == 1 else [pltpu.VMEM((ti, tj), F32)],
        compiler_params=_params(("parallel", "parallel", "arbitrary")),
    )(*operands)
    return result.reshape(N_CHIPS, m // N_CHIPS, n) if blocked_rows_out else result


def _small(fn, out_shapes, *arrays, name):
    n_in = len(arrays)

    def body(*refs):
        res = fn(*[r[...] for r in refs[:n_in]])
        for o_ref, v in zip(refs[n_in:], res):
            o_ref[...] = v.astype(o_ref.dtype)

    return pl.pallas_call(
        body, name=name, out_shape=tuple(jax.ShapeDtypeStruct(s, d) for s, d in out_shapes),
        in_specs=[_full(a.shape) for a in arrays], out_specs=tuple(_full(s) for s, _ in out_shapes), grid=(1,),
        compiler_params=_params(("arbitrary",)),
    )(*arrays)


def _rows_tile(rows, cols, itemsize=4, budget=2 * 1024 * 1024):
    return _tile(rows, max(16, budget // (cols * itemsize)), 16)


def _rowwise(fn, out_dtypes, *arrays, name):
    rows, cols = arrays[0].shape
    tr = _rows_tile(rows, cols)
    n_in = len(arrays)

    def body(*refs):
        res = fn(*[r[...] for r in refs[:n_in]])
        for o_ref, v in zip(refs[n_in:], res):
            o_ref[...] = v.astype(o_ref.dtype)

    spec = pl.BlockSpec((tr, cols), lambda i: (i, 0))
    return pl.pallas_call(
        body, name=name, grid=(rows // tr,), in_specs=[spec] * n_in, out_specs=tuple(spec for _ in out_dtypes),
        out_shape=tuple(jax.ShapeDtypeStruct((rows, cols), d) for d in out_dtypes),
        compiler_params=_params(("parallel",)),
    )(*arrays)


def _as2d(a):
    return a.reshape(1, a.size) if a.ndim < 2 else a.reshape(-1, a.shape[-1])


def _cast_shard(w, place, layer, name):
    _, rows, cols = w.shape
    tr = _rows_tile(rows, cols)

    def body(place_ref, w_ref, o_ref):
        o_ref[...] = w_ref[...].astype(o_ref.dtype)

    return pl.pallas_call(
        body, name=name, out_shape=jax.ShapeDtypeStruct((N_CHIPS, rows, cols), BF16),
        grid_spec=pltpu.PrefetchScalarGridSpec(
            num_scalar_prefetch=1, grid=(rows // tr,),
            in_specs=[pl.BlockSpec((None, tr, cols), lambda i, pr: (layer, i, 0))],
            out_specs=pl.BlockSpec((None, tr, cols), lambda i, pr: (pr[0], i, 0))),
        compiler_params=_params(("parallel",)),
    )(place, w)


def _norm_fwd(x, g, a, b=None, res=None, *, out_dtype, name, into=None):
    rows, d = x.shape
    row0, total, dest = into if into is not None else (0, rows, None)
    tr = _rows_tile(math.gcd(rows, row0), d, budget=4 * 1024 * 1024)
    has_b, has_res = b is not None, res is not None

    def body(*refs):
        x_ref, g_ref, a_ref = refs[:3]
        pos = 3
        xv = x_ref[...]
        rstd = lax.rsqrt(jnp.mean(xv * xv, axis=-1, keepdims=True) + EPS)
        y = (xv * rstd * g_ref[...]) * a_ref[...]
        if has_b:
            y = y + refs[pos][...]
            pos += 1
        if has_res:
            y = y + refs[pos][...]
            pos += 1
        refs[-1][...] = y.astype(refs[-1].dtype)

    row = pl.BlockSpec((tr, d), lambda i: (i, 0))
    vec = pl.BlockSpec((1, d), lambda i: (0, 0))
    operands, specs = [x, g, a], [row, vec, vec]
    if has_b:
        operands.append(b)
        specs.append(vec)
    if has_res:
        operands.append(res)
        specs.append(row)
    if dest is not None:
        operands.append(dest)
        specs.append(ANY)
    return pl.pallas_call(
        body, name=name, grid=(rows // tr,), in_specs=specs, out_specs=pl.BlockSpec((tr, d), lambda i: (i + row0 // tr, 0)),
        out_shape=jax.ShapeDtypeStruct((total, d), out_dtype), compiler_params=_params(("parallel",)),
        input_output_aliases={} if dest is None else {len(operands) - 1: 0},
    )(*operands)


def _rstd(v):
    return lax.rsqrt(jnp.mean(v * v, axis=-1, keepdims=True) + EPS)


def _res_mod_fwd(o, x, g_res, gate, g_mod, a_mod, b_mod, *, name):
    rows, d = x.shape
    tr = _rows_tile(rows, d)

    def body(o_ref, x_ref, gr_ref, gate_ref, gm_ref, a_ref, b_ref, xm_ref, h_ref):
        ov = o_ref[...]
        xm = x_ref[...] + (ov * _rstd(ov) * gr_ref[...]) * gate_ref[...]
        xm_ref[...] = xm
        h_ref[...] = ((xm * _rstd(xm) * gm_ref[...]) * a_ref[...] + b_ref[...]).astype(h_ref.dtype)

    row = pl.BlockSpec((tr, d), lambda i: (i, 0))
    vec = pl.BlockSpec((1, d), lambda i: (0, 0))
    return pl.pallas_call(
        body, name=name, grid=(rows // tr,), in_specs=[row, row, vec, vec, vec, vec, vec], out_specs=(row, row),
        out_shape=(jax.ShapeDtypeStruct((rows, d), F32), jax.ShapeDtypeStruct((rows, d), BF16)),
        compiler_params=_params(("parallel",)),
    )(o, x, g_res, gate, g_mod, a_mod, b_mod)


def _mod_res_bwd(d_h, xm, g_mod, a_mod, extra, o, g_res, gate, *, name):
    rows, d = xm.shape
    tr = _rows_tile(rows, d)

    def body(dh_ref, xm_ref, gm_ref, a_ref, ex_ref, o_ref, gr_ref, gate_ref,
             dxm_ref, do_ref, dgm_ref, da_ref, db_ref, dgr_ref, dgate_ref):
        @pl.when(pl.program_id(0) == 0)
        def _():
            for ref in (dgm_ref, da_ref, db_ref, dgr_ref, dgate_ref):
                ref[...] = jnp.zeros_like(ref)

        def norm_adjoint(dy, xv, gain, scale, dgain_ref, dscale_ref):
            rstd = _rstd(xv)
            nrm = xv * rstd
            dscale_ref[...] += jnp.sum(dy * (nrm * gain), axis=0, keepdims=True)
            dt = dy * scale
            dgain_ref[...] += jnp.sum(dt * nrm, axis=0, keepdims=True)
            dn = dt * gain
            return rstd * (dn - nrm * jnp.mean(dn * nrm, axis=-1, keepdims=True))

        dhv = dh_ref[...].astype(F32)
        db_ref[...] += jnp.sum(dhv, axis=0, keepdims=True)
        dxm = norm_adjoint(dhv, xm_ref[...], gm_ref[...], a_ref[...], dgm_ref, da_ref) + ex_ref[...]
        dxm_ref[...] = dxm
        do_ref[...] = norm_adjoint(dxm, o_ref[...], gr_ref[...], gate_ref[...], dgr_ref, dgate_ref).astype(do_ref.dtype)

    row = pl.BlockSpec((tr, d), lambda i: (i, 0))
    vec = pl.BlockSpec((1, d), lambda i: (0, 0))
    vshape = jax.ShapeDtypeStruct((1, d), F32)
    return pl.pallas_call(
        body, name=name, grid=(rows // tr,), in_specs=[row, row, vec, vec, row, row, vec, vec],
        out_specs=(row, row, vec, vec, vec, vec, vec),
        out_shape=(jax.ShapeDtypeStruct((rows, d), F32), jax.ShapeDtypeStruct((rows, d), BF16)) + (vshape,) * 5,
        compiler_params=_params(("arbitrary",)),
    )(d_h, xm, g_mod, a_mod, extra, o, g_res, gate)


def _norm_bwd(dy, x, g, a, extra=None, *, out_dtype, name):
    rows, d = x.shape
    tr = _rows_tile(rows, d)
    has_extra = extra is not None

    def body(*refs):
        dy_ref, x_ref, g_ref, a_ref = refs[:4]
        pos = 4
        extra_ref = None
        if has_extra:
            extra_ref = refs[pos]
            pos += 1
        dx_ref, dg_ref, da_ref, db_ref = refs[pos:pos + 4]

        @pl.when(pl.program_id(0) == 0)
        def _():
            dg_ref[...] = jnp.zeros_like(dg_ref)
            da_ref[...] = jnp.zeros_like(da_ref)
            db_ref[...] = jnp.zeros_like(db_ref)

        xv = x_ref[...]
        dyv = dy_ref[...].astype(F32)
        rstd = lax.rsqrt(jnp.mean(xv * xv, axis=-1, keepdims=True) + EPS)
        nrm = xv * rstd
        gv = g_ref[...]
        da_ref[...] += jnp.sum(dyv * (nrm * gv), axis=0, keepdims=True)
        db_ref[...] += jnp.sum(dyv, axis=0, keepdims=True)
        dt = dyv * a_ref[...]
        dg_ref[...] += jnp.sum(dt * nrm, axis=0, keepdims=True)
        dn = dt * gv
        dx = rstd * (dn - nrm * jnp.mean(dn * nrm, axis=-1, keepdims=True))
        if has_extra:
            dx = dx + extra_ref[...]
        dx_ref[...] = dx.astype(dx_ref.dtype)

    row = pl.BlockSpec((tr, d), lambda i: (i, 0))
    vec = pl.BlockSpec((1, d), lambda i: (0, 0))
    operands, specs = [dy, x, g, a], [row, row, vec, vec]
    if has_extra:
        operands.append(extra)
        specs.append(row)
    vshape = jax.ShapeDtypeStruct((1, d), F32)
    return pl.pallas_call(
        body, name=name, grid=(rows // tr,), in_specs=specs, out_specs=(row, vec, vec, vec),
        out_shape=(jax.ShapeDtypeStruct((rows, d), out_dtype), vshape, vshape, vshape),
        compiler_params=_params(("arbitrary",)),
    )(*operands)


def _loss_head(y, target):
    rows, d = y.shape
    tr = _rows_tile(rows, d)

    def body(y_ref, t_ref, dy_ref, loss_ref):
        @pl.when(pl.program_id(0) == 0)
        def _():
            loss_ref[...] = jnp.zeros_like(loss_ref)

        err = y_ref[...] - t_ref[...]
        dy_ref[...] = err * (1.0 / d)
        loss_ref[...] += jnp.sum(jnp.sum(err * err, axis=-1, keepdims=True), axis=0, keepdims=True) * (0.5 / d)

    row = pl.BlockSpec((tr, d), lambda i: (i, 0))
    return pl.pallas_call(
        body, name="loss_head", grid=(rows // tr,), in_specs=[row, row], out_specs=(row, _full((1, 1))),
        out_shape=(jax.ShapeDtypeStruct((rows, d), F32), jax.ShapeDtypeStruct((1, 1), F32)),
        compiler_params=_params(("arbitrary",)),
    )(y, target)


def _rope_partner(v):
    lane = lax.broadcasted_iota(jnp.int32, v.shape, 1)
    up = pltpu.roll(v, HEAD_DIM - ROPE_PAIRS, 1)
    down = pltpu.roll(v, ROPE_PAIRS, 1)
    return jnp.where((lane % (2 * ROPE_PAIRS)) < ROPE_PAIRS, up, down)


def _qk_fwd(proj, q_g, k_g, cos, sin, *, name, kv_into=None):
    rows = proj.shape[0]
    row0, total, kv_dest = kv_into if kv_into is not None else (0, rows, None)
    tr = _tile(math.gcd(rows, row0), 256, 16)
    width = ATTN_W + 2 * KV_W

    def body(p_ref, qg_ref, kg_ref, cos_ref, sin_ref, *rest):
        q_ref, k_ref, v_ref = rest[-3:]
        cosv, sinv = cos_ref[...], sin_ref[...]
        for h in range(N_HEADS + N_KV):
            xv = p_ref[:, h * HEAD_DIM:(h + 1) * HEAD_DIM]
            gain = qg_ref[...] if h < N_HEADS else kg_ref[...]
            t = xv * lax.rsqrt(jnp.mean(xv * xv, axis=-1, keepdims=True) + EPS) * gain
            y = t * cosv + _rope_partner(t) * sinv
            if h < N_HEADS:
                q_ref[:, h * HEAD_DIM:(h + 1) * HEAD_DIM] = y.astype(BF16)
            else:
                k_ref[:, (h - N_HEADS) * HEAD_DIM:(h - N_HEADS + 1) * HEAD_DIM] = y.astype(BF16)
        v_ref[...] = p_ref[:, ATTN_W + KV_W:width].astype(BF16)

    vec = _full((1, HEAD_DIM))
    tab = pl.BlockSpec((tr, HEAD_DIM), lambda i: (i, 0))
    kv_spec = pl.BlockSpec((tr, KV_W), lambda i: (i + row0 // tr, 0))
    kv_shape = jax.ShapeDtypeStruct((total, KV_W), BF16)
    return pl.pallas_call(
        body, name=name, grid=(rows // tr,),
        in_specs=[pl.BlockSpec((tr, width), lambda i: (i, 0)), vec, vec, tab, tab] + ([] if kv_dest is None else [ANY, ANY]),
        out_specs=(pl.BlockSpec((tr, ATTN_W), lambda i: (i, 0)), kv_spec, kv_spec),
        out_shape=(jax.ShapeDtypeStruct((rows, ATTN_W), BF16), kv_shape, kv_shape),
        input_output_aliases={} if kv_dest is None else {5: 1, 6: 2},
        compiler_params=_params(("parallel",)),
    )(proj, q_g, k_g, cos, sin, *([] if kv_dest is None else kv_dest))


def _qk_bwd(dq, dk, proj, q_g, k_g, cos, sin, *, name, dk_row0=0):
    rows = proj.shape[0]
    tr = _tile(math.gcd(rows, dk_row0), 256, 16)
    width = ATTN_W + KV_W
    has_q = dq is not None

    def body(*refs):
        pos = 0
        dq_ref = None
        if has_q:
            dq_ref = refs[0]
            pos = 1
        dk_ref, p_ref, qg_ref, kg_ref, cos_ref, sin_ref, dp_ref, dqg_ref, dkg_ref = refs[pos:pos + 9]

        @pl.when(pl.program_id(0) == 0)
        def _():
            dqg_ref[...] = jnp.zeros_like(dqg_ref)
            dkg_ref[...] = jnp.zeros_like(dkg_ref)

        cosv, sinv = cos_ref[...], sin_ref[...]
        for h in range(N_HEADS + N_KV):
            cols = slice(h * HEAD_DIM, (h + 1) * HEAD_DIM)
            if h < N_HEADS and not has_q:
                dp_ref[:, cols] = jnp.zeros((tr, HEAD_DIM), dp_ref.dtype)
                continue
            if h < N_HEADS:
                dyv, gain, dgain_ref = dq_ref[:, cols], qg_ref[...], dqg_ref
            else:
                hk = h - N_HEADS
                dyv, gain, dgain_ref = dk_ref[:, hk * HEAD_DIM:(hk + 1) * HEAD_DIM], kg_ref[...], dkg_ref
            dyv = dyv.astype(F32)
            dt = dyv * cosv + _rope_partner(dyv * sinv)
            xv = p_ref[:, cols]
            rstd = lax.rsqrt(jnp.mean(xv * xv, axis=-1, keepdims=True) + EPS)
            nrm = xv * rstd
            dgain_ref[...] += jnp.sum(dt * nrm, axis=0, keepdims=True)
            dn = dt * gain
            dp_ref[:, cols] = (rstd * (dn - nrm * jnp.mean(dn * nrm, axis=-1, keepdims=True))).astype(dp_ref.dtype)

    vec = _full((1, HEAD_DIM))
    tab = pl.BlockSpec((tr, HEAD_DIM), lambda i: (i, 0))
    operands = ([dq] if has_q else []) + [dk, proj, q_g, k_g, cos, sin]
    specs = ([pl.BlockSpec((tr, ATTN_W), lambda i: (i, 0))] if has_q else []) + [
        pl.BlockSpec((tr, KV_W), lambda i: (i + dk_row0 // tr, 0)), pl.BlockSpec((tr, width), lambda i: (i, 0)), vec, vec, tab, tab]
    return pl.pallas_call(
        body, name=name, grid=(rows // tr,), in_specs=specs,
        out_specs=(pl.BlockSpec((tr, width), lambda i: (i, 0)), vec, vec),
        out_shape=(jax.ShapeDtypeStruct((rows, width), BF16), jax.ShapeDtypeStruct((1, HEAD_DIM), F32),
                   jax.ShapeDtypeStruct((1, HEAD_DIM), F32)),
        compiler_params=_params(("arbitrary",)),
    )(*operands)


def _attn_fwd(q, k, v):
    n_q, n_k = q.shape[0], k.shape[0]
    tq = _tile(n_q, 512, 16)
    gw = GROUP * HEAD_DIM
    scale = HEAD_DIM ** -0.5

    def body(q_ref, k_ref, v_ref, o_ref, lse_ref):
        kv, vv = k_ref[...], v_ref[...]
        for g in range(GROUP):
            cols = slice(g * HEAD_DIM, (g + 1) * HEAD_DIM)
            s = lax.dot_general(q_ref[:, cols], kv, (((1,), (1,)), ((), ())), preferred_element_type=F32) * (scale * LOG2E)
            m = jnp.max(s, axis=-1, keepdims=True)
            p = jnp.exp2(s - m)
            l = jnp.sum(p, axis=-1, keepdims=True)
            o = jnp.dot(p.astype(BF16), vv, preferred_element_type=F32) / l
            o_ref[:, cols] = o.astype(o_ref.dtype)
            lse_ref[:, g:g + 1] = m + jnp.log(l) * LOG2E

    return pl.pallas_call(
        body, name="attn_fwd", grid=(N_KV, n_q // tq),
        in_specs=[pl.BlockSpec((tq, gw), lambda h, i: (i, h)), pl.BlockSpec((n_k, HEAD_DIM), lambda h, i: (0, h)),
                  pl.BlockSpec((n_k, HEAD_DIM), lambda h, i: (0, h))],
        out_specs=(pl.BlockSpec((tq, gw), lambda h, i: (i, h)), pl.BlockSpec((None, tq, GROUP), lambda h, i: (h, i, 0))),
        out_shape=(jax.ShapeDtypeStruct((n_q, ATTN_W + D_RNN), BF16), jax.ShapeDtypeStruct((N_KV, n_q, GROUP), F32)),
        compiler_params=_params(("parallel", "parallel")),
    )(q, k, v)


def _attn_bwd(q, k, v, o, lse, do):
    n_q, n_k = q.shape[0], k.shape[0]
    tq = _tile(n_q, 256, 16)
    gw = GROUP * HEAD_DIM
    scale = HEAD_DIM ** -0.5

    def body(q_ref, k_ref, v_ref, o_ref, lse_ref, do_ref, dq_ref, dk_ref, dv_ref):
        @pl.when(pl.program_id(1) == 0)
        def _():
            dk_ref[...] = jnp.zeros_like(dk_ref)
            dv_ref[...] = jnp.zeros_like(dv_ref)

        kv, vv = k_ref[...], v_ref[...]
        for g in range(GROUP):
            cols = slice(g * HEAD_DIM, (g + 1) * HEAD_DIM)
            qg = q_ref[:, cols]
            dof = do_ref[:, cols].astype(F32)
            dog = dof.astype(BF16)
            s = lax.dot_general(qg, kv, (((1,), (1,)), ((), ())), preferred_element_type=F32) * (scale * LOG2E)
            p = jnp.exp2(s - lse_ref[:, g:g + 1])
            delta = jnp.sum(dof * o_ref[:, cols].astype(F32), axis=-1, keepdims=True)
            dp = lax.dot_general(dog, vv, (((1,), (1,)), ((), ())), preferred_element_type=F32)
            ds = (p * (dp - delta) * scale).astype(BF16)
            pb = p.astype(BF16)
            dq_ref[:, cols] = jnp.dot(ds, kv, preferred_element_type=F32)
            dk_ref[...] += lax.dot_general(ds, qg, (((0,), (0,)), ((), ())), preferred_element_type=F32)
            dv_ref[...] += lax.dot_general(pb, dog, (((0,), (0,)), ((), ())), preferred_element_type=F32)

    qspec = pl.BlockSpec((tq, gw), lambda h, i: (i, h))
    kspec = pl.BlockSpec((n_k, HEAD_DIM), lambda h, i: (0, h))
    return pl.pallas_call(
        body, name="attn_bwd", grid=(N_KV, n_q // tq),
        in_specs=[qspec, kspec, kspec, qspec, pl.BlockSpec((None, tq, GROUP), lambda h, i: (h, i, 0)), qspec],
        out_specs=(qspec, kspec, kspec),
        out_shape=(jax.ShapeDtypeStruct((n_q, ATTN_W), F32), jax.ShapeDtypeStruct((n_k, KV_W), F32),
                   jax.ShapeDtypeStruct((n_k, KV_W), F32)),
        compiler_params=_params(("parallel", "arbitrary")),
    )(q, k, v, o, lse, do)


CONV_COLS = 256
XR_COL0 = ATTN_W + 2 * KV_W


def _shift_rows(v, off):
    if off == 0:
        return v
    n = v.shape[0]
    rolled = pltpu.roll(v, (-off) % n, 0)
    t = lax.broadcasted_iota(jnp.int32, v.shape, 0)
    keep = (t + off >= 0) & (t + off < n)
    return jnp.where(keep, rolled, 0.0)


def _conv_fwd(proj_l, proj_c, w, b):
    n_lat, n_ctx = proj_l.shape[0], proj_c.shape[0]
    blk0 = XR_COL0 // CONV_COLS

    def body(xl_ref, xc_ref, w_ref, b_ref, y_ref):
        for x_ref, rows in ((xc_ref, slice(0, n_ctx)), (xl_ref, slice(n_ctx, n_ctx + n_lat))):
            xv = x_ref[...]
            y = b_ref[...] + jnp.zeros_like(xv)
            for j in range(CONV_W):
                y = y + _shift_rows(xv, j - CONV_W // 2) * w_ref[j:j + 1, :]
            y_ref[rows, :] = y

    return pl.pallas_call(
        body, name="conv_fwd", grid=(D_RNN // CONV_COLS,),
        in_specs=[pl.BlockSpec((n_lat, CONV_COLS), lambda i: (0, blk0 + i)), pl.BlockSpec((n_ctx, CONV_COLS), lambda i: (0, blk0 + i)),
                  pl.BlockSpec((CONV_W, CONV_COLS), lambda i: (0, i)), pl.BlockSpec((1, CONV_COLS), lambda i: (0, i))],
        out_specs=pl.BlockSpec((n_ctx + n_lat, CONV_COLS), lambda i: (0, i)),
        out_shape=jax.ShapeDtypeStruct((n_ctx + n_lat, D_RNN), F32), compiler_params=_params(("parallel",)),
    )(proj_l, proj_c, w, b)


def _conv_bwd(d1, d2, proj_l, proj_c, w):
    n_lat, n_ctx = proj_l.shape[0], proj_c.shape[0]
    blk0 = XR_COL0 // CONV_COLS

    def body(d1_ref, d2_ref, xl_ref, xc_ref, w_ref, dxl_ref, dxc_ref, dw_ref, db_ref):
        dw = [0.0] * CONV_W
        db = 0.0
        for x_ref, dx_ref, rows in ((xc_ref, dxc_ref, slice(0, n_ctx)), (xl_ref, dxl_ref, slice(n_ctx, n_ctx + n_lat))):
            dv = d1_ref[rows, :] + d2_ref[rows, :]
            xv = x_ref[...]
            dx = jnp.zeros_like(dv)
            for j in range(CONV_W):
                off = j - CONV_W // 2
                dx = dx + _shift_rows(dv, -off) * w_ref[j:j + 1, :]
                dw[j] = dw[j] + jnp.sum(dv * _shift_rows(xv, off), axis=0, keepdims=True)
            dx_ref[...] = dx.astype(dx_ref.dtype)
            db = db + jnp.sum(dv, axis=0, keepdims=True)
        for j in range(CONV_W):
            dw_ref[j:j + 1, :] = dw[j]
        db_ref[...] = db

    both = pl.BlockSpec((n_ctx + n_lat, CONV_COLS), lambda i: (0, i))
    return pl.pallas_call(
        body, name="conv_bwd", grid=(D_RNN // CONV_COLS,),
        in_specs=[both, both, pl.BlockSpec((n_lat, CONV_COLS), lambda i: (0, blk0 + i)),
                  pl.BlockSpec((n_ctx, CONV_COLS), lambda i: (0, blk0 + i)), pl.BlockSpec((CONV_W, CONV_COLS), lambda i: (0, i))],
        out_specs=(pl.BlockSpec((n_lat, CONV_COLS), lambda i: (0, i)), pl.BlockSpec((n_ctx, CONV_COLS), lambda i: (0, i)),
                   pl.BlockSpec((CONV_W, CONV_COLS), lambda i: (0, i)), pl.BlockSpec((1, CONV_COLS), lambda i: (0, i))),
        out_shape=(jax.ShapeDtypeStruct((n_lat, D_RNN), BF16), jax.ShapeDtypeStruct((n_ctx, D_RNN), BF16),
                   jax.ShapeDtypeStruct((CONV_W, D_RNN), F32), jax.ShapeDtypeStruct((1, D_RNN), F32)),
        compiler_params=_params(("parallel",)),
    )(d1, d2, proj_l, proj_c, w)


RNN_TB = 256
SCAN_ROWS = 8


def _sigmoid(z):
    return 1.0 / (1.0 + jnp.exp(-z))


def _softplus(z):
    return jnp.maximum(z, 0.0) + jnp.log(1.0 + jnp.exp(-jnp.abs(z)))


def _one_minus_exp(y):
    series = -y * (1.0 + y * (0.5 + y * (1.0 / 6.0 + y * (1.0 / 24.0))))
    return jnp.where(y > -0.03, series, 1.0 - jnp.exp(y))


def _rglru_gates(xv, wa_ref, ba_ref, wx_ref, bx_ref, lam_ref):
    xb = xv.astype(BF16)
    zr = jnp.concatenate([jnp.dot(xb[:, n * RNN_BW:(n + 1) * RNN_BW], wa_ref[n].astype(BF16),
                                  preferred_element_type=F32) for n in range(RNN_BLOCKS)], axis=-1) + ba_ref[...]
    zi = jnp.concatenate([jnp.dot(xb[:, n * RNN_BW:(n + 1) * RNN_BW], wx_ref[n].astype(BF16),
                                  preferred_element_type=F32) for n in range(RNN_BLOCKS)], axis=-1) + bx_ref[...]
    r = _sigmoid(zr)
    gi = _sigmoid(zi)
    sp = _softplus(-lam_ref[...])
    log_a = -RG_C * r * sp
    a = jnp.exp(log_a)
    s = jnp.sqrt(_one_minus_exp(2.0 * log_a))
    return r, gi, sp, a, s


def _scan_rows(n_rows, reverse, step_fn, carry):
    groups = n_rows // SCAN_ROWS

    def trip(gidx, carry):
        gi = (groups - 1 - gidx) if reverse else gidx
        base = pl.multiple_of(gi * SCAN_ROWS, SCAN_ROWS)
        return step_fn(base, carry)

    return lax.fori_loop(0, groups, trip, carry)


def _scan_block_order(nb, nb_c, reverse, adjoint):
    if not reverse:
        return (lambda i: nb - 1 - i) if adjoint else (lambda i: i)
    if adjoint:
        return lambda i: jnp.where(i < nb - nb_c, nb_c + i, i - (nb - nb_c))
    return lambda i: jnp.where(i < nb_c, nb_c - 1 - i, nb + nb_c - 1 - i)


def _rglru_fwd(xs, wa, ba, wx, bx, lam, *, reverse, n_ctx, name):
    rows = xs.shape[0]
    tb = _tile(math.gcd(rows, n_ctx), RNN_TB, SCAN_ROWS)
    nb = rows // tb
    block_of = _scan_block_order(nb, n_ctx // tb, reverse, False)
    order = lambda i: (block_of(i), 0)

    def body(x_ref, wa_ref, ba_ref, wx_ref, bx_ref, lam_ref, h_ref, hp_ref, a_s, b_s, state):
        @pl.when(pl.program_id(0) == 0)
        def _():
            state[...] = jnp.zeros_like(state)

        xv = x_ref[...]
        _, gi, _, a, s = _rglru_gates(xv, wa_ref, ba_ref, wx_ref, bx_ref, lam_ref)
        a_s[...] = a
        b_s[...] = s * (gi * xv)

        def group(base, h):
            av = a_s[pl.ds(base, SCAN_ROWS), :]
            bv = b_s[pl.ds(base, SCAN_ROWS), :]
            outs, prevs = [None] * SCAN_ROWS, [None] * SCAN_ROWS
            for k in range(SCAN_ROWS):
                r_ = SCAN_ROWS - 1 - k if reverse else k
                prevs[r_] = h
                h = av[r_:r_ + 1, :] * h + bv[r_:r_ + 1, :]
                outs[r_] = h
            h_ref[pl.ds(base, SCAN_ROWS), :] = jnp.concatenate(outs, axis=0)
            hp_ref[pl.ds(base, SCAN_ROWS), :] = jnp.concatenate(prevs, axis=0)
            return h

        state[0:1, :] = _scan_rows(tb, reverse, group, state[0:1, :])

    blk = pl.BlockSpec((tb, D_RNN), order)
    wspec = _full((RNN_BLOCKS, RNN_BW, RNN_BW))
    vec = _full((1, D_RNN))
    return pl.pallas_call(
        body, name=name, grid=(nb,), in_specs=[blk, wspec, vec, wspec, vec, vec], out_specs=(blk, blk),
        out_shape=(jax.ShapeDtypeStruct((rows, D_RNN), F32), jax.ShapeDtypeStruct((rows, D_RNN), F32)),
        scratch_shapes=[pltpu.VMEM((tb, D_RNN), F32), pltpu.VMEM((tb, D_RNN), F32), pltpu.VMEM((SCAN_ROWS, D_RNN), F32)],
        compiler_params=_params(("arbitrary",)),
    )(xs, wa, ba, wx, bx, lam)


def _rglru_bwd(xs, h_prev, dh, wa, ba, wx, bx, lam, *, reverse, n_ctx, name):
    rows = xs.shape[0]
    tb = _tile(math.gcd(rows, n_ctx), RNN_TB, SCAN_ROWS)
    nb, nb_c = rows // tb, n_ctx // tb
    back = not reverse
    block_of = _scan_block_order(nb, nb_c, reverse, True)
    order = lambda i: (block_of(i), 0)

    def body(x_ref, hp_ref, dh_ref, wa_ref, ba_ref, wx_ref, bx_ref, lam_ref,
             dx_ref, dwa_ref, dba_ref, dwx_ref, dbx_ref, dlam_ref, a_s, g_s, state):
        @pl.when(pl.program_id(0) == 0)
        def _():
            state[...] = jnp.zeros_like(state)
            dwa_ref[...] = jnp.zeros_like(dwa_ref)
            dwx_ref[...] = jnp.zeros_like(dwx_ref)
            dba_ref[...] = jnp.zeros_like(dba_ref)
            dbx_ref[...] = jnp.zeros_like(dbx_ref)
            dlam_ref[...] = jnp.zeros_like(dlam_ref)

        xv = x_ref[...]
        r, gi, sp, a, s = _rglru_gates(xv, wa_ref, ba_ref, wx_ref, bx_ref, lam_ref)
        a_s[...] = a

        is_latent = block_of(pl.program_id(0)) >= nb_c

        def group(base, carry):
            av = a_s[pl.ds(base, SCAN_ROWS), :]
            dv = jnp.where(is_latent, dh_ref[pl.ds(base, SCAN_ROWS), :], 0.0)
            outs = [None] * SCAN_ROWS
            for k in range(SCAN_ROWS):
                r_ = SCAN_ROWS - 1 - k if back else k
                gt = dv[r_:r_ + 1, :] + carry
                outs[r_] = gt
                carry = av[r_:r_ + 1, :] * gt
            g_s[pl.ds(base, SCAN_ROWS), :] = jnp.concatenate(outs, axis=0)
            return carry

        state[0:1, :] = _scan_rows(tb, back, group, state[0:1, :])

        gv = g_s[...]
        d_a = gv * hp_ref[...]
        d_s = gv * (gi * xv)
        d_gi = gv * (s * xv)
        dx = gv * (s * gi)
        d_log_a = d_a * a - d_s * (a * a) / s
        d_r = d_log_a * (-RG_C * sp)
        lamv = lam_ref[...]
        d_sp = jnp.sum(d_log_a * (-RG_C * r), axis=0, keepdims=True)
        dlam_ref[...] += d_sp * (-_sigmoid(-lamv))
        d_zr = d_r * r * (1.0 - r)
        d_zi = d_gi * gi * (1.0 - gi)
        dba_ref[...] += jnp.sum(d_zr, axis=0, keepdims=True)
        dbx_ref[...] += jnp.sum(d_zi, axis=0, keepdims=True)
        xb = xv.astype(BF16)
        zrb, zib = d_zr.astype(BF16), d_zi.astype(BF16)
        parts = []
        for n in range(RNN_BLOCKS):
            cols = slice(n * RNN_BW, (n + 1) * RNN_BW)
            dwa_ref[n] += lax.dot_general(xb[:, cols], zrb[:, cols], (((0,), (0,)), ((), ())), preferred_element_type=F32)
            dwx_ref[n] += lax.dot_general(xb[:, cols], zib[:, cols], (((0,), (0,)), ((), ())), preferred_element_type=F32)
            parts.append(
                lax.dot_general(zrb[:, cols], wa_ref[n].astype(BF16), (((1,), (1,)), ((), ())), preferred_element_type=F32)
                + lax.dot_general(zib[:, cols], wx_ref[n].astype(BF16), (((1,), (1,)), ((), ())), preferred_element_type=F32))
        dx_ref[...] = dx + jnp.concatenate(parts, axis=-1)

    blk = pl.BlockSpec((tb, D_RNN), order)
    wspec = _full((RNN_BLOCKS, RNN_BW, RNN_BW))
    vec = _full((1, D_RNN))
    wshape = jax.ShapeDtypeStruct((RNN_BLOCKS, RNN_BW, RNN_BW), F32)
    vshape = jax.ShapeDtypeStruct((1, D_RNN), F32)
    dh_blk = pl.BlockSpec((tb, D_RNN), lambda i: (jnp.maximum(block_of(i) - nb_c, 0), 0))
    return pl.pallas_call(
        body, name=name, grid=(nb,), in_specs=[blk, blk, dh_blk, wspec, vec, wspec, vec, vec],
        out_specs=(blk, wspec, vec, wspec, vec, vec),
        out_shape=(jax.ShapeDtypeStruct((rows, D_RNN), F32), wshape, vshape, wshape, vshape, vshape),
        scratch_shapes=[pltpu.VMEM((tb, D_RNN), F32), pltpu.VMEM((tb, D_RNN), F32), pltpu.VMEM((SCAN_ROWS, D_RNN), F32)],
        compiler_params=_params(("arbitrary",)),
    )(xs, h_prev, dh, wa, ba, wx, bx, lam)


def _assemble_d_proj(dp_qk_l, dp_qk_c, dv_all, d_xr_l, d_xr_c, d_gate):
    n_lat, n_ctx = dp_qk_l.shape[0], dp_qk_c.shape[0]
    tr = _tile(math.gcd(n_lat, n_ctx), 256, 16)
    nb_l, nb_c = n_lat // tr, n_ctx // tr
    w_qk = ATTN_W + KV_W

    def body(ql_ref, qc_ref, dv_ref, xl_ref, xc_ref, g_ref, o_ref):
        i = pl.program_id(0)
        o_ref[:, w_qk:XR_COL0] = dv_ref[...].astype(o_ref.dtype)

        @pl.when(i < nb_l)
        def _():
            o_ref[:, :w_qk] = ql_ref[...]
            o_ref[:, XR_COL0:GATE_COL0] = xl_ref[...]
            o_ref[:, GATE_COL0:] = g_ref[...]

        @pl.when(i >= nb_l)
        def _():
            o_ref[:, :w_qk] = qc_ref[...]
            o_ref[:, XR_COL0:GATE_COL0] = xc_ref[...]
            o_ref[:, GATE_COL0:] = jnp.zeros((tr, D_RNN), o_ref.dtype)

    lat = lambda i: (jnp.minimum(i, nb_l - 1), 0)
    ctx = lambda i: (jnp.maximum(i - nb_l, 0), 0)
    return pl.pallas_call(
        body, name="assemble_d_proj", grid=(nb_l + nb_c,),
        in_specs=[pl.BlockSpec((tr, w_qk), lat), pl.BlockSpec((tr, w_qk), ctx),
                  pl.BlockSpec((tr, KV_W), lambda i: (jnp.where(i < nb_l, i + nb_c, i - nb_l), 0)),
                  pl.BlockSpec((tr, D_RNN), lat), pl.BlockSpec((tr, D_RNN), ctx), pl.BlockSpec((tr, D_RNN), lat)],
        out_specs=pl.BlockSpec((tr, GATE_COL0 + D_RNN), lambda i: (i, 0)),
        out_shape=jax.ShapeDtypeStruct((n_lat + n_ctx, GATE_COL0 + D_RNN), BF16),
        compiler_params=_params(("parallel",)),
    )(dp_qk_l, dp_qk_c, dv_all, d_xr_l, d_xr_c, d_gate)


def _gelu(z):
    return 0.5 * z * (1.0 + jnp.tanh(GELU_C * (z + 0.044715 * z * z * z)))


def _gelu_grad(z):
    t = jnp.tanh(GELU_C * (z + 0.044715 * z * z * z))
    return 0.5 * (1.0 + t) + 0.5 * z * (1.0 - t * t) * (GELU_C * (1.0 + 3.0 * 0.044715 * z * z))


GATE_COL0 = XR_COL0 + D_RNN


RNN_OUT_COLS = 512


def _rnn_out_specs(rows, hf_off, hb_off):
    tr = _tile(rows, 256, 16)
    assert hf_off % tr == 0 and hb_off % tr == 0 and GATE_COL0 % RNN_OUT_COLS == 0
    fo, bo, go = hf_off // tr, hb_off // tr, GATE_COL0 // RNN_OUT_COLS
    hf_spec = pl.BlockSpec((tr, RNN_OUT_COLS), lambda i, j: (i + fo, j))
    hb_spec = pl.BlockSpec((tr, RNN_OUT_COLS), lambda i, j: (i + bo, j))
    gate_spec = pl.BlockSpec((tr, RNN_OUT_COLS), lambda i, j: (i, j + go))
    out_spec = pl.BlockSpec((tr, RNN_OUT_COLS), lambda i, j: (i, j))
    return (rows // tr, D_RNN // RNN_OUT_COLS), hf_spec, hb_spec, gate_spec, out_spec


def _rnn_out_fwd(hf, hb, proj, hf_off, hb_off, cat):
    rows = proj.shape[0]
    grid, hf_spec, hb_spec, gate_spec, out_spec = _rnn_out_specs(rows, hf_off, hb_off)
    tr, col0 = out_spec.block_shape[0], ATTN_W // RNN_OUT_COLS

    def body(hf_ref, hb_ref, g_ref, _, o_ref):
        o_ref[...] = ((hf_ref[...] + hb_ref[...]) * _gelu(g_ref[...])).astype(o_ref.dtype)

    return pl.pallas_call(
        body, name="rnn_out_fwd", grid=grid, in_specs=[hf_spec, hb_spec, gate_spec, ANY],
        out_specs=pl.BlockSpec((tr, RNN_OUT_COLS), lambda i, j: (i, j + col0)),
        out_shape=jax.ShapeDtypeStruct(cat.shape, cat.dtype), input_output_aliases={3: 0},
        compiler_params=_params(("parallel", "parallel")),
    )(hf, hb, proj, cat)


def _rnn_out_bwd(d_cat, hf, hb, proj, hf_off, hb_off):
    rows = proj.shape[0]
    grid, hf_spec, hb_spec, gate_spec, out_spec = _rnn_out_specs(rows, hf_off, hb_off)
    do = ATTN_W // RNN_OUT_COLS

    def body(d_ref, hf_ref, hb_ref, g_ref, dh_ref, dg_ref):
        dv, gv = d_ref[...].astype(F32), g_ref[...]
        dh_ref[...] = dv * _gelu(gv)
        dg_ref[...] = (dv * (hf_ref[...] + hb_ref[...]) * _gelu_grad(gv)).astype(dg_ref.dtype)

    tr = out_spec.block_shape[0]
    return pl.pallas_call(
        body, name="rnn_out_bwd", grid=grid,
        in_specs=[pl.BlockSpec((tr, RNN_OUT_COLS), lambda i, j: (i, j + do)), hf_spec, hb_spec, gate_spec],
        out_specs=(out_spec, out_spec),
        out_shape=(jax.ShapeDtypeStruct((rows, D_RNN), F32), jax.ShapeDtypeStruct((rows, D_RNN), BF16)),
        compiler_params=_params(("parallel", "parallel")),
    )(d_cat, hf, hb, proj)


def _gmlp_parts(z_ref, vg_ref, vb_ref, d_gm):
    zu, zv = z_ref[:, :d_gm], z_ref[:, d_gm:]
    u = _gelu(zu)
    v = _gelu(zv)
    mu = jnp.mean(v, axis=-1, keepdims=True)
    vc = v - mu
    rstd = lax.rsqrt(jnp.mean(vc * vc, axis=-1, keepdims=True) + EPS)
    vhat = vc * rstd
    vn = vhat * vg_ref[...] + vb_ref[...]
    return zu, zv, u, vhat, rstd, vn


def _gmlp_fwd(z, v_g, v_b, w_sp, b_sp_t):
    rows, d_gm = z.shape[0], z.shape[1] // 2
    tr = _tile(rows, 256, CHUNK)
    gwid = d_gm // GM_GROUPS

    def body(z_ref, vg_ref, vb_ref, w_ref, b_ref, o_ref):
        _, _, u, _, _, vn = _gmlp_parts(z_ref, vg_ref, vb_ref, d_gm)
        vnb = vn.astype(BF16)
        for g in range(GM_GROUPS):
            wg = w_ref[g].astype(BF16)
            for c in range(tr // CHUNK):
                rs, cs = slice(c * CHUNK, (c + 1) * CHUNK), slice(g * gwid, (g + 1) * gwid)
                sv = jnp.dot(wg, vnb[rs, cs], preferred_element_type=F32) + b_ref[:, g:g + 1]
                o_ref[rs, cs] = (u[rs, cs] * sv).astype(o_ref.dtype)

    return pl.pallas_call(
        body, name="gmlp_fwd", grid=(rows // tr,),
        in_specs=[pl.BlockSpec((tr, 2 * d_gm), lambda i: (i, 0)), _full((1, d_gm)), _full((1, d_gm)),
                  _full(w_sp.shape), _full(b_sp_t.shape)],
        out_specs=pl.BlockSpec((tr, d_gm), lambda i: (i, 0)),
        out_shape=jax.ShapeDtypeStruct((rows, d_gm), BF16), compiler_params=_params(("parallel",)),
    )(z, v_g, v_b, w_sp, b_sp_t)


def _gmlp_bwd(z, dgate, v_g, v_b, w_sp, b_sp_t):
    rows, d_gm = z.shape[0], z.shape[1] // 2
    tr = _tile(rows, 256, CHUNK)
    gwid = d_gm // GM_GROUPS

    def body(z_ref, dg_ref, vg_ref, vb_ref, w_ref, b_ref, dz_ref, dbin_ref, dvg_ref, dvb_ref, dw_ref, dbs_ref, dvn_s):
        @pl.when(pl.program_id(0) == 0)
        def _():
            dbin_ref[...] = jnp.zeros_like(dbin_ref)
            dvg_ref[...] = jnp.zeros_like(dvg_ref)
            dvb_ref[...] = jnp.zeros_like(dvb_ref)
            dw_ref[...] = jnp.zeros_like(dw_ref)
            dbs_ref[...] = jnp.zeros_like(dbs_ref)

        zu, zv, u, vhat, rstd, vn = _gmlp_parts(z_ref, vg_ref, vb_ref, d_gm)
        vnb = vn.astype(BF16)
        dgv = dg_ref[...].astype(F32)
        dsv = dgv * u
        dsvb = dsv.astype(BF16)
        for g in range(GM_GROUPS):
            wg = w_ref[g].astype(BF16)
            cs = slice(g * gwid, (g + 1) * gwid)
            for c in range(tr // CHUNK):
                rs = slice(c * CHUNK, (c + 1) * CHUNK)
                sv = jnp.dot(wg, vnb[rs, cs], preferred_element_type=F32) + b_ref[:, g:g + 1]
                dz_ref[rs, cs] = (dgv[rs, cs] * sv * _gelu_grad(zu[rs, cs])).astype(dz_ref.dtype)
                dw_ref[g] += lax.dot_general(dsvb[rs, cs], vnb[rs, cs], (((1,), (1,)), ((), ())),
                                             preferred_element_type=F32)
                dbs_ref[:, g:g + 1] += jnp.sum(dsv[rs, cs], axis=-1, keepdims=True)
                dvn_s[rs, cs] = lax.dot_general(wg, dsvb[rs, cs], (((0,), (0,)), ((), ())), preferred_element_type=F32)
        dvn = dvn_s[...]
        dvg_ref[...] += jnp.sum(dvn * vhat, axis=0, keepdims=True)
        dvb_ref[...] += jnp.sum(dvn, axis=0, keepdims=True)
        dvh = dvn * vg_ref[...]
        dv = rstd * (dvh - jnp.mean(dvh, axis=-1, keepdims=True) - vhat * jnp.mean(dvh * vhat, axis=-1, keepdims=True))
        dzv = dv * _gelu_grad(zv)
        dz_ref[:, d_gm:] = dzv.astype(dz_ref.dtype)
        dbin_ref[:, d_gm:] += jnp.sum(dzv, axis=0, keepdims=True)
        dbin_ref[:, :d_gm] += jnp.sum(dz_ref[:, :d_gm].astype(F32), axis=0, keepdims=True)

    return pl.pallas_call(
        body, name="gmlp_bwd", grid=(rows // tr,),
        in_specs=[pl.BlockSpec((tr, 2 * d_gm), lambda i: (i, 0)), pl.BlockSpec((tr, d_gm), lambda i: (i, 0)),
                  _full((1, d_gm)), _full((1, d_gm)), _full(w_sp.shape), _full(b_sp_t.shape)],
        out_specs=(pl.BlockSpec((tr, 2 * d_gm), lambda i: (i, 0)), _full((1, 2 * d_gm)), _full((1, d_gm)),
                   _full((1, d_gm)), _full(w_sp.shape), _full(b_sp_t.shape)),
        out_shape=(jax.ShapeDtypeStruct((rows, 2 * d_gm), BF16), jax.ShapeDtypeStruct((1, 2 * d_gm), F32),
                   jax.ShapeDtypeStruct((1, d_gm), F32), jax.ShapeDtypeStruct((1, d_gm), F32),
                   jax.ShapeDtypeStruct(w_sp.shape, F32), jax.ShapeDtypeStruct(b_sp_t.shape, F32)),
        scratch_shapes=[pltpu.VMEM((tr, d_gm), F32)],
        compiler_params=_params(("arbitrary",)),
    )(z, dgate, v_g, v_b, w_sp, b_sp_t)


def _adamw_math(w, g, m, v):
    m = ADAM_B1 * m + (1.0 - ADAM_B1) * g
    v = ADAM_B2 * v + (1.0 - ADAM_B2) * (g * g)
    m_hat = m / (1.0 - ADAM_B1 ** ADAM_STEP)
    v_hat = v / (1.0 - ADAM_B2 ** ADAM_STEP)
    delta = -ADAM_LR * (m_hat / (jnp.sqrt(v_hat) + ADAM_EPS) + ADAM_WD * w)
    return delta, m, v


def _adamw(w, g, m, v, name, rewrite_grad=False):
    shape = w.shape
    if rewrite_grad:
        outs = _rowwise(lambda w_, g_, m_, v_: (g_,) + _adamw_math(w_, g_, m_, v_), (F32,) * 4, _as2d(w), _as2d(g), _as2d(m),
                        _as2d(v), name=name)
        return tuple(o.reshape(shape) for o in outs)
    outs = _rowwise(_adamw_math, (F32, F32, F32), _as2d(w), _as2d(g), _as2d(m), _as2d(v), name=name)
    return (g.reshape(shape),) + tuple(o.reshape(shape) for o in outs)


PACK_COLS = 1024


def _pack(arrays, dtype=F32):
    flat = jnp.concatenate([a.reshape(-1).astype(dtype) for a in arrays])
    pad = (-flat.size) % (16 * PACK_COLS)
    return jnp.pad(flat, (0, pad)).reshape(-1, PACK_COLS)


def _into_slot(pack, dev, name):
    rows, cols = pack.shape
    tr = _rows_tile(rows, cols, budget=512 * 1024)

    def body(dev_ref, p_ref, o_ref):
        o_ref[...] = p_ref[...]

    return pl.pallas_call(
        body, name=name, out_shape=jax.ShapeDtypeStruct((N_DEV, rows, cols), pack.dtype),
        grid_spec=pltpu.PrefetchScalarGridSpec(
            num_scalar_prefetch=1, grid=(rows // tr,), in_specs=[pl.BlockSpec((tr, cols), lambda i, dv: (i, 0))],
            out_specs=pl.BlockSpec((None, tr, cols), lambda i, dv: (dv[0], i, 0))),
        compiler_params=_params(("parallel",)),
    )(dev, pack)


def _unpack(flat, shapes):
    out, pos = [], 0
    for shp in shapes:
        n = math.prod(shp)
        out.append(flat[pos:pos + n].reshape(shp))
        pos += n
    return out


def _unpack_devices(packed8, shapes):
    flat8 = packed8.reshape(N_DEV, -1)
    out, pos = [], 0
    for shp in shapes:
        n = math.prod(shp)
        out.append(flat8[:, pos:pos + n].reshape((N_DEV,) + tuple(shp)))
        pos += n
    return out


def _sum_devices(g8):
    _, rows, cols = g8.shape
    tr = _rows_tile(rows, cols, budget=256 * 1024)

    def body(g_ref, o_ref):
        acc = g_ref[0].astype(F32)
        for d in range(1, N_DEV):
            acc = acc + g_ref[d].astype(F32)
        o_ref[...] = acc

    return pl.pallas_call(
        body, name="sum_devices", grid=(rows // tr,), in_specs=[pl.BlockSpec((N_DEV, tr, cols), lambda i: (0, i, 0))],
        out_specs=pl.BlockSpec((tr, cols), lambda i: (i, 0)), out_shape=jax.ShapeDtypeStruct((rows, cols), F32),
        compiler_params=_params(("parallel",)),
    )(g8)


def _place():
    return lax.axis_index("x"), lax.axis_index("y"), lax.axis_index("c")


def _other_chips(x, y):
    return [(1 - x, y), (x, 1 - y), (1 - x, 1 - y)]


def _remote(src, dst, send_sem, recv_sem, to):
    return pltpu.make_async_remote_copy(src_ref=src, dst_ref=dst, send_sem=send_sem, recv_sem=recv_sem, device_id=to,
                                        device_id_type=MESH)


def _comm_call(body, name, operands, out_shapes, n_remote, n_local, aliases=None):
    return pl.pallas_call(
        body, name=name, out_shape=tuple(out_shapes), in_specs=[ANY] * len(operands), out_specs=tuple(ANY for _ in out_shapes),
        scratch_shapes=[pltpu.SemaphoreType.DMA((n_remote,)), pltpu.SemaphoreType.DMA((n_remote,)),
                        pltpu.SemaphoreType.DMA((max(n_local, 1),))],
        input_output_aliases=aliases or {},
    )(*operands)


def _in_place(arrays):
    return [jax.ShapeDtypeStruct(a.shape, a.dtype) for a in arrays], {i: i for i in range(len(arrays))}


def _allgather8(arrs, name):
    n = len(arrs)

    def body(*refs):
        ins, outs = refs[:n], refs[n:2 * n]
        send, recv, lsem = refs[2 * n:]
        x, y, c = _place()
        me, sib = (x, y, c), (x, y, 1 - c)
        chips = _other_chips(x, y)

        def slot(t, px, py, pc):
            return outs[t].at[4 * px + 2 * py + pc]

        def cp(t, k, block, to, from_input=False):
            src = ins[t] if from_input else slot(t, *block)
            return _remote(src, slot(t, *block), send.at[7 * t + k], recv.at[7 * t + k], to)

        mine = [pltpu.make_async_copy(ins[t], slot(t, *me), lsem.at[t]) for t in range(n)]
        for cpy in mine:
            cpy.start()
        first = []
        for t in range(n):
            first.append(cp(t, 0, me, sib, True))
            first += [cp(t, 1 + j, me, (*chip, c), True) for j, chip in enumerate(chips)]
        for cpy in first:
            cpy.start()
        passed = []
        for t in range(n):
            for j, chip in enumerate(chips):
                cp(t, 1 + j, (*chip, c), me).wait_recv()
                fwd = cp(t, 4 + j, (*chip, c), sib)
                fwd.start()
                passed.append(fwd)
        for t in range(n):
            cp(t, 0, sib, me).wait_recv()
            for j, chip in enumerate(chips):
                cp(t, 4 + j, (*chip, 1 - c), me).wait_recv()
        for cpy in first + passed:
            cpy.wait_send()
        for cpy in mine:
            cpy.wait()

    outs = _comm_call(body, name, arrs, [jax.ShapeDtypeStruct((N_DEV,) + a.shape, a.dtype) for a in arrs], 7 * n, n)
    return list(outs)


def _gather_weights(bufs):
    n_u = len(bufs)

    def body(*refs):
        bufs_ = refs[n_u:2 * n_u]
        send, recv, _ = refs[2 * n_u:]
        x, y, c = _place()
        me, sib, q = (x, y, c), (x, y, 1 - c), 2 * x + y
        chips = _other_chips(x, y)
        sent = []
        for u in range(n_u):
            half = bufs_[u].shape[1] // 2
            mine = bufs_[u].at[q, pl.ds(c * half, half)]
            for j, chip in enumerate(chips):
                cpy = _remote(mine, mine, send.at[6 * u + j], recv.at[6 * u + j], (*chip, c))
                cpy.start()
                sent.append(cpy)
        for u in range(n_u):
            half = bufs_[u].shape[1] // 2
            for j, chip in enumerate(chips):
                landed = bufs_[u].at[2 * chip[0] + chip[1], pl.ds(c * half, half)]
                _remote(landed, landed, send.at[6 * u + j], recv.at[6 * u + j], me).wait_recv()
                cpy = _remote(landed, landed, send.at[6 * u + 3 + j], recv.at[6 * u + 3 + j], sib)
                cpy.start()
                sent.append(cpy)
        for u in range(n_u):
            half = bufs_[u].shape[1] // 2
            for j, chip in enumerate(chips):
                landed = bufs_[u].at[2 * chip[0] + chip[1], pl.ds((1 - c) * half, half)]
                _remote(landed, landed, send.at[6 * u + 3 + j], recv.at[6 * u + 3 + j], me).wait_recv()
        for cpy in sent:
            cpy.wait_send()

    shapes, aliases = _in_place(bufs)
    return list(_comm_call(body, "gather_weights", bufs, shapes, 6 * n_u, 0, aliases))


def _exchange_halves(grads):
    n = len(grads)

    def body(*refs):
        ins, outs = refs[:n], refs[n:2 * n]
        send, recv, _ = refs[2 * n:]
        x, y, c = _place()
        sib = (x, y, 1 - c)
        sent = []
        for k in range(n):
            half = ins[k].shape[1] // 2
            cpy = _remote(ins[k].at[pl.ds(0, N_CHIPS), pl.ds((1 - c) * half, half)], outs[k], send.at[k], recv.at[k], sib)
            cpy.start()
            sent.append(cpy)
        for cpy in sent:
            cpy.wait()

    shapes = [jax.ShapeDtypeStruct((N_CHIPS, g.shape[1] // 2, g.shape[2]), g.dtype) for g in grads]
    return list(_comm_call(body, "exchange_halves", grads, shapes, n, 0))


def _chips_all_to_all(sums):
    n = len(sums)

    def body(*refs):
        ins, outs = refs[:n], refs[n:2 * n]
        send, recv, _ = refs[2 * n:]
        x, y, c = _place()
        sent = []
        for k in range(n):
            for j, chip in enumerate(_other_chips(x, y)):
                cpy = _remote(ins[k].at[2 * chip[0] + chip[1]], outs[k].at[j], send.at[3 * k + j], recv.at[3 * k + j], (*chip, c))
                cpy.start()
                sent.append(cpy)
        for cpy in sent:
            cpy.wait()

    shapes = [jax.ShapeDtypeStruct((N_CHIPS - 1,) + s.shape[1:], s.dtype) for s in sums]
    return list(_comm_call(body, "chips_all_to_all", sums, shapes, 3 * n, 0))


def _join_halves(bufs):
    n = len(bufs)
    units = [(k, layer) for k in range(n) for layer in range(bufs[k].shape[0])]

    def body(*refs):
        bufs_ = refs[n:2 * n]
        send, recv, _ = refs[2 * n:]
        x, y, c = _place()
        sent = []
        for u, (k, layer) in enumerate(units):
            half = bufs_[k].shape[1] // 2
            mine = bufs_[k].at[layer, pl.ds(c * half, half)]
            cpy = _remote(mine, mine, send.at[u], recv.at[u], (x, y, 1 - c))
            cpy.start()
            sent.append(cpy)
        for u, (k, layer) in enumerate(units):
            half = bufs_[k].shape[1] // 2
            theirs = bufs_[k].at[layer, pl.ds((1 - c) * half, half)]
            _remote(theirs, theirs, send.at[u], recv.at[u], (x, y, c)).wait_recv()
        for cpy in sent:
            cpy.wait_send()

    shapes, aliases = _in_place(bufs)
    return list(_comm_call(body, "join_halves", bufs, shapes, len(units), 0, aliases))


def _add_halves(grad, other, place):
    _, rows, cols = grad.shape
    half = rows // 2
    tr = _rows_tile(half, cols, itemsize=2, budget=2 * 1024 * 1024)
    per_half = half // tr

    def body(place_ref, g_ref, o_ref, s_ref):
        s_ref[...] = (g_ref[...].astype(F32) + o_ref[...].astype(F32)).astype(s_ref.dtype)

    return pl.pallas_call(
        body, name="add_halves", out_shape=jax.ShapeDtypeStruct((N_CHIPS, half, cols), grad.dtype),
        grid_spec=pltpu.PrefetchScalarGridSpec(
            num_scalar_prefetch=1, grid=(N_CHIPS, per_half),
            in_specs=[pl.BlockSpec((None, tr, cols), lambda k, i, pr: (k, pr[1] * per_half + i, 0)),
                      pl.BlockSpec((None, tr, cols), lambda k, i, pr: (k, i, 0))],
            out_specs=pl.BlockSpec((None, tr, cols), lambda k, i, pr: (k, i, 0))),
        compiler_params=_params(("parallel", "parallel")),
    )(place, grad, other)


def _add_chips(sums, others, place, dest, layer, n_layers):
    _, half, cols = sums.shape
    tr = _rows_tile(half, cols, itemsize=4, budget=2 * 1024 * 1024)
    per_half = half // tr

    def body(place_ref, s_ref, o_ref, *rest):
        acc = s_ref[...].astype(F32)
        for j in range(N_CHIPS - 1):
            acc = acc + o_ref[j].astype(F32)
        rest[-1][...] = acc

    operands = [place, sums, others] + ([] if dest is None else [dest])
    return pl.pallas_call(
        body, name="add_chips", out_shape=jax.ShapeDtypeStruct((n_layers, 2 * half, cols), F32),
        grid_spec=pltpu.PrefetchScalarGridSpec(
            num_scalar_prefetch=1, grid=(per_half,),
            in_specs=[pl.BlockSpec((None, tr, cols), lambda i, pr: (pr[0], i, 0)),
                      pl.BlockSpec((N_CHIPS - 1, tr, cols), lambda i, pr: (0, i, 0))] + ([] if dest is None else [ANY]),
            out_specs=pl.BlockSpec((None, tr, cols), lambda i, pr: (layer, pr[1] * per_half + i, 0))),
        input_output_aliases={} if dest is None else {3: 0},
        compiler_params=_params(("parallel",)),
    )(*operands)


HBM = pl.BlockSpec(memory_space=pltpu.HBM)
SEM = pl.BlockSpec(memory_space=pltpu.SEMAPHORE)
DATAFLOW = pltpu.SideEffectType.DATAFLOW_SIDE_EFFECTING


def _split_start(name, bufs, copies, n_copies, after=None):
    n = len(bufs)
    extra = 0 if after is None else 1

    def body(*refs):
        for cpy in copies(refs[:n], refs[n + extra], refs[n + extra + 1]):
            cpy.start()
        refs[-1][...] = jnp.zeros_like(refs[-1])

    outs = pl.pallas_call(
        body, name=name,
        out_shape=(pltpu.SemaphoreType.DMA((n_copies,)), pltpu.SemaphoreType.DMA((n_copies,)),
                   *[pltpu.HBM(b.shape, b.dtype) for b in bufs], jax.ShapeDtypeStruct((8, LANES), F32)),
        in_specs=[HBM] * n + [ANY] * extra,
        out_specs=(SEM, SEM, *[HBM] * n, pl.BlockSpec(memory_space=pltpu.VMEM)),
        input_output_aliases={i: 2 + i for i in range(n)},
        compiler_params=pltpu.CompilerParams(has_side_effects=DATAFLOW),
    )(*[pltpu.with_memory_space_constraint(b, pltpu.HBM) for b in bufs], *([] if after is None else [after]))
    return outs[0], outs[1], list(outs[2:2 + n]), outs[-1]


def _split_wait(name, bufs, send, recv, copies, after):
    n = len(bufs)

    def body(*refs):
        for cpy in copies(refs[:n], refs[n], refs[n + 1]):
            cpy.wait_send()
            cpy.wait_recv()

    return list(pl.pallas_call(
        body, name=name, out_shape=tuple(pltpu.HBM(b.shape, b.dtype) for b in bufs),
        in_specs=[HBM] * n + [SEM, SEM, ANY], out_specs=tuple([HBM] * n),
        input_output_aliases={i: i for i in range(n)},
        compiler_params=pltpu.CompilerParams(has_side_effects=DATAFLOW),
    )(*bufs, send, recv, after))


def _gather_copies(bufs, send, recv):
    x, y, c = _place()
    out = []
    for u, buf in enumerate(bufs):
        half = buf.shape[1] // 2
        mine = buf.at[2 * x + y, pl.ds(c * half, half)]
        out += [_remote(mine, mine, send.at[3 * u + j], recv.at[3 * u + j], (*chip, c))
                for j, chip in enumerate(_other_chips(x, y))]
    return out


def _exchange_copies(bufs, send, recv):
    x, y, c = _place()
    n = len(bufs) // 2
    out = []
    for k in range(n):
        half = bufs[k].shape[1] // 2
        theirs = bufs[k].at[pl.ds(0, N_CHIPS), pl.ds((1 - c) * half, half)]
        out.append(_remote(theirs, bufs[n + k], send.at[k], recv.at[k], (x, y, 1 - c)))
    return out


def _all_to_all_copies(bufs, send, recv):
    x, y, c = _place()
    n = len(bufs) // 2
    return [_remote(bufs[k].at[2 * chip[0] + chip[1]], bufs[n + k].at[j], send.at[3 * k + j], recv.at[3 * k + j], (*chip, c))
            for k in range(n) for j, chip in enumerate(_other_chips(x, y))]


def _forward_copies(bufs, send, recv):
    x, y, c = _place()
    out = []
    for u, buf in enumerate(bufs):
        half = buf.shape[1] // 2
        for j, chip in enumerate(_other_chips(x, y)):
            landed = buf.at[2 * chip[0] + chip[1], pl.ds(c * half, half)]
            out.append(_remote(landed, landed, send.at[3 * u + j], recv.at[3 * u + j], (x, y, 1 - c)))
    return out


def _gather8_copies(bufs, send, recv):
    x, y, c = _place()
    targets = [(x, y, 1 - c)] + [(*chip, c) for chip in _other_chips(x, y)]
    out = []
    for b, buf in enumerate(bufs):
        mine = buf.at[4 * x + 2 * y + c]
        out += [_remote(mine, mine, send.at[N_CHIPS * b + k], recv.at[N_CHIPS * b + k], to) for k, to in enumerate(targets)]
    return out


def _forward_slots(bufs, name):
    n = len(bufs)

    def body(*refs):
        bufs_ = refs[n:2 * n]
        send, recv, _ = refs[2 * n:]
        x, y, c = _place()
        chips = _other_chips(x, y)
        sent = []
        for b in range(n):
            for j, chip in enumerate(chips):
                slot = bufs_[b].at[4 * chip[0] + 2 * chip[1] + c]
                cpy = _remote(slot, slot, send.at[3 * b + j], recv.at[3 * b + j], (x, y, 1 - c))
                cpy.start()
                sent.append(cpy)
        for b in range(n):
            for j, chip in enumerate(chips):
                slot = bufs_[b].at[4 * chip[0] + 2 * chip[1] + 1 - c]
                _remote(slot, slot, send.at[3 * b + j], recv.at[3 * b + j], (x, y, c)).wait_recv()
        for cpy in sent:
            cpy.wait_send()

    shapes, aliases = _in_place(bufs)
    return list(_comm_call(body, name, bufs, shapes, (N_CHIPS - 1) * n, 0, aliases))


FWD_GROUPS = {'mix': ('ar_out', 'ff_in0', 'ff_out0'), 'l1': ('gm_in', 'gm_out', 'ff_in1', 'ff_out1')}
GRAD_LAYOUT = {'ff_in0': (0, 0), 'ff_in1': (0, 1), 'ff_out0': (1, 0), 'ff_out1': (1, 1), 'ar_in': (2, 0), 'ar_out': (3, 0),
               'gm_in': (4, 0), 'gm_out': (5, 0)}


class _MeshLink:
    def __init__(self, place, shards):
        self.place = place
        first = _gather_weights([shards['ar_in']])
        self.ready = {'ar_in': first[0]}
        self.pending, after = {}, first[0]
        for group, names in FWD_GROUPS.items():
            send, recv, bufs, token = _split_start(f"gather_{group}_start", [shards[n] for n in names], _gather_copies,
                                                   3 * len(names), after)
            self.pending[group] = (names, send, recv, bufs)
            after = token
        self.start_token = after[0, 0]
        self.forwarding, self.exchanging, self.sent, self.last_token = {}, {}, {}, None

    def prefetch(self, group, after):
        names, send, recv, bufs = self.pending.pop(group)
        bufs = _split_wait(f"gather_{group}_wait", bufs, send, recv, _gather_copies, after)
        send, recv, bufs, token = _split_start(f"forward_{group}_start", bufs, _forward_copies, 3 * len(names))
        self.forwarding[group] = (names, send, recv, bufs)
        return token[0, 0]

    def weights(self, group, after):
        if group in self.forwarding:
            names, send, recv, bufs = self.forwarding.pop(group)
            self.ready.update(zip(names, _split_wait(f"forward_{group}_wait", bufs, send, recv, _forward_copies, after)))
        return self.ready

    def gradients(self, group, grads, after=None):
        tok = self.poll(next(iter(grads.values())))
        names, mine = list(grads), list(grads.values())
        landing = [lax.empty((N_CHIPS, g.shape[1] // 2, g.shape[2]), g.dtype) for g in mine]
        send, recv, bufs, token = _split_start(f"exchange_{group}_start", mine + landing, _exchange_copies, len(names), after)
        self.exchanging[group] = (names, send, recv, bufs)
        self.last_token = token
        return token[0, 0] + tok

    def poll(self, after):
        tok = 0.0
        for group in list(self.exchanging):
            names, send, recv, bufs = self.exchanging.pop(group)
            bufs = _split_wait(f"exchange_{group}_wait", bufs, send, recv, _exchange_copies, after)
            sums = [_add_halves(g, r, self.place) for g, r in zip(bufs[:len(names)], bufs[len(names):])]
            landing = [lax.empty((N_CHIPS - 1,) + s.shape[1:], s.dtype) for s in sums]
            send, recv, bufs, token = _split_start(f"grads_{group}_start", sums + landing, _all_to_all_copies, 3 * len(names))
            self.sent[group] = (names, send, recv, bufs)
            self.last_token = token
            tok = tok + token[0, 0]
        return tok

    def reduce(self, groups, after):
        units = {}
        for group in groups:
            names, send, recv, bufs = self.sent.pop(group)
            bufs = _split_wait(f"grads_{group}_wait", bufs, send, recv, _all_to_all_copies, after)
            units.update(zip(names, zip(bufs[:len(names)], bufs[len(names):])))
        n_layers = {p: 1 + max(l for pp, l in GRAD_LAYOUT.values() if pp == p) for p, _ in GRAD_LAYOUT.values()}
        out = {}
        for name, (p, layer) in GRAD_LAYOUT.items():
            if name in units:
                out[p] = _add_chips(*units[name], self.place, out.get(p), layer, n_layers[p])
        params = sorted(out)
        return dict(zip(params, _join_halves([out[p] for p in params])))


def _rope_tables(n):
    n_rows = n // GRID_W
    freqs = ROPE_THETA ** (-jnp.arange(ROPE_PAIRS, dtype=F32) / ROPE_PAIRS)
    ang_r = jnp.arange(n_rows, dtype=F32)[:, None] * freqs
    ang_c = jnp.arange(GRID_W, dtype=F32)[:, None] * freqs

    def per_token(of_row, of_col):
        r = jnp.broadcast_to(of_row[:, None, :], (n_rows, GRID_W, ROPE_PAIRS)).reshape(n, ROPE_PAIRS)
        c = jnp.broadcast_to(of_col[None, :, :], (n_rows, GRID_W, ROPE_PAIRS)).reshape(n, ROPE_PAIRS)
        return r, c

    cos_r, cos_c = per_token(jnp.cos(ang_r), jnp.cos(ang_c))
    sin_r, sin_c = per_token(jnp.sin(ang_r), jnp.sin(ang_c))
    cos = jnp.concatenate([cos_r, cos_r, cos_c, cos_c], axis=-1)
    sin = jnp.concatenate([-sin_r, sin_r, -sin_c, sin_c], axis=-1)
    return cos, sin


def _ffn_fwd(h2, w1, w2, tag):
    r, a = _matmul(h2, w1, kind='nn', b_split='n', out_dtype=BF16, epilogue='relu2', name=f"ffn_in_{tag}")
    f = _matmul(a, w2, kind='nn', b_split='k', out_dtype=F32, name=f"ffn_out_{tag}")
    return r, a, f


def _ffn_bwd(d_f, h2, r, a, w1, w2, tag):
    d_u = _matmul(d_f, w2, kind='nt', b_split='k', out_dtype=BF16, epilogue='times2x', extra=r, name=f"ffn_out_dx_{tag}")
    d_w2 = _matmul(a, d_f, kind='tn', out_split='k', out_dtype=BF16, name=f"ffn_out_dw_{tag}")
    d_w1 = _matmul(h2, d_u, kind='tn', out_split='n', out_dtype=BF16, name=f"ffn_in_dw_{tag}")
    d_h2 = _matmul(d_u, w1, kind='nt', b_split='n', out_dtype=F32, name=f"ffn_in_dx_{tag}")
    return d_h2, d_w1, d_w2


class _LocalLink:
    def __init__(self, big):
        self.big, self.grads, self.start_token = big, {}, 0.0

    def prefetch(self, group, after):
        return 0.0

    def poll(self, after):
        return 0.0

    def weights(self, group, after):
        return self.big

    def gradients(self, group, grads):
        self.grads.update(grads)
        return 0.0


def _local_step(xl0, xc0, target, ml, mc0, sp, link):
    n_lat, n_ctx = xl0.shape[0], xc0.shape[0]
    one = lambda v: 1.0 + v
    g = [[sp['norm_g'][i, k][None, :] for k in range(4)] for i in range(2)]

    sh1, sc1, gt1, sh2, sc2, gt2 = ml[0]
    big = link.weights('ar', None)
    sh1 = sh1 + link.start_token
    n_all = n_lat + n_ctx
    h_all = _norm_fwd(xl0, g[0][0], one(sc1), b=sh1, out_dtype=BF16, name="l0_mod1", into=(0, n_all, None))
    h_all = _norm_fwd(xc0, g[0][0], one(mc0[1]), b=mc0[0], out_dtype=BF16, name="l0_mod1_ctx", into=(n_lat, n_all, h_all))
    proj_l = _matmul(h_all, big['ar_in'], kind='nn', b_split='n', out_dtype=F32, a_rows=(0, n_lat), name="ar_in_lat")
    proj_c = _matmul(h_all, big['ar_in'], kind='nn', b_split='n', out_dtype=F32, a_rows=(n_lat, n_ctx), name="ar_in_ctx")
    cos_l, sin_l = _rope_tables(n_lat)
    cos_c, sin_c = jnp.ones((n_ctx, HEAD_DIM), F32), jnp.zeros((n_ctx, HEAD_DIM), F32)
    q_g, k_g = sp['q_g'], sp['k_g']
    _, k_all, v_all = _qk_fwd(proj_c, q_g, k_g, cos_c, sin_c, name="qk_fwd_ctx", kv_into=(0, n_all, None))
    q_l, k_all, v_all = _qk_fwd(proj_l, q_g, k_g, cos_l, sin_l, name="qk_fwd_lat", kv_into=(n_ctx, n_all, (k_all, v_all)))
    cat, lse = _attn_fwd(q_l, k_all, v_all)
    conv_b = sp['conv_b'] + link.prefetch('mix', cat)
    xs = _conv_fwd(proj_l, proj_c, sp['conv_w'], conv_b)
    rnn_w = [(sp['wa'][d], sp['ba'][d][None, :], sp['wx'][d], sp['bx'][d][None, :], sp['lam'][d][None, :]) for d in range(2)]
    h_f, hp_f = _rglru_fwd(xs, *rnn_w[0], reverse=False, n_ctx=n_ctx, name="rglru_fwd_f")
    h_r, hp_r = _rglru_fwd(xs, *rnn_w[1], reverse=True, n_ctx=n_ctx, name="rglru_fwd_r")
    cat = _rnn_out_fwd(h_f, h_r, proj_l, n_ctx, n_ctx, cat)
    w_mix = link.weights('mix', cat)
    ol0 = _matmul(cat, w_mix['ar_out'], kind='nn', b_split='k', out_dtype=F32, name="ar_out")
    xm0, h2_0 = _res_mod_fwd(ol0, xl0, g[0][1], gt1, g[0][2], one(sc2), sh2, name="l0_res1_mod2")
    r0, a0, f0 = _ffn_fwd(h2_0, w_mix['ff_in0'], w_mix['ff_out0'], "l0")
    th1, tc1, tg1, th2, tc2, tg2 = ml[1]
    xl1, hl1 = _res_mod_fwd(f0, xm0, g[0][3], gt2 + link.prefetch('l1', f0), g[1][0], one(tc1), th1, name="l0_res2_l1_mod1")
    w_l1 = link.weights('l1', xl1)
    z = _matmul(hl1, w_l1['gm_in'], kind='nn', b_split='n', bias=sp['gm_b_in'], out_dtype=F32, name="gm_in")
    b_sp_t = sp['gm_b_sp'].T
    gated = _gmlp_fwd(z, sp['gm_v_g'], sp['gm_v_b'], sp['gm_w_sp'], b_sp_t)
    ol1 = _matmul(gated, w_l1['gm_out'], kind='nn', b_split='k', out_dtype=F32, name="gm_out")
    xm1, h2_1 = _res_mod_fwd(ol1, xl1, g[1][1], tg1, g[1][2], one(tc2), th2, name="l1_res1_mod2")
    r1, a1, f1 = _ffn_fwd(h2_1, w_l1['ff_in1'], w_l1['ff_out1'], "l1")
    y = _norm_fwd(f1, g[1][3], tg2, res=xm1, out_dtype=F32, name="l1_res2")

    dy, loss = _loss_head(y, target)

    d_f1, dg13, d_tg2, _ = _norm_bwd(dy, f1, g[1][3], tg2, out_dtype=BF16, name="l1_res2_bwd")
    d_h2, dw_ff_in1, dw_ff_out1 = _ffn_bwd(d_f1, h2_1, r1, a1, w_l1['ff_in1'], w_l1['ff_out1'], "l1")
    tok = link.gradients('ffn1', {'ff_in1': dw_ff_in1, 'ff_out1': dw_ff_out1})
    dxm1, d_ol1, dg12, d_tc2, d_th2, dg11, d_tg1 = _mod_res_bwd(d_h2, xm1, g[1][2], one(tc2) + tok, dy, ol1, g[1][1], tg1,
                                                                name="l1_mod2_res1_bwd")
    d_gated = _matmul(d_ol1, w_l1['gm_out'], kind='nt', b_split='k', out_dtype=F32, name="gm_out_dx")
    dw_gm_out = _matmul(gated, d_ol1, kind='tn', out_split='k', out_dtype=BF16, name="gm_out_dw")
    d_z, d_gm_b_in, d_vg, d_vb, d_wsp, d_bsp_t = _gmlp_bwd(z, d_gated, sp['gm_v_g'], sp['gm_v_b'], sp['gm_w_sp'], b_sp_t)
    dw_gm_in = _matmul(hl1, d_z, kind='tn', out_split='n', out_dtype=BF16, name="gm_in_dw")
    d_hl1 = _matmul(d_z, w_l1['gm_in'], kind='nt', b_split='n', out_dtype=F32, name="gm_in_dx")
    tok = link.gradients('gm', {'gm_in': dw_gm_in, 'gm_out': dw_gm_out})

    dxl1, d_f0, dg10, d_tc1, d_th1, dg03, d_gt2 = _mod_res_bwd(d_hl1, xl1, g[1][0], one(tc1) + tok, dxm1, f0, g[0][3], gt2,
                                                               name="l1_mod1_l0_res2_bwd")
    d_h2, dw_ff_in0, dw_ff_out0 = _ffn_bwd(d_f0, h2_0, r0, a0, w_mix['ff_in0'], w_mix['ff_out0'], "l0")
    tok = link.gradients('ffn0', {'ff_in0': dw_ff_in0, 'ff_out0': dw_ff_out0})
    dxm0, d_ol0, dg02, d_sc2, d_sh2, dg01, d_gt1 = _mod_res_bwd(d_h2, xm0, g[0][2], one(sc2) + tok, dxl1, ol0, g[0][1], gt1,
                                                                name="l0_mod2_res1_bwd")
    d_cat = _matmul(d_ol0, w_mix['ar_out'], kind='nt', b_split='k', out_dtype=F32, name="ar_out_dx")
    dw_ar_out = _matmul(cat, d_ol0, kind='tn', out_split='k', out_dtype=BF16, name="ar_out_dw")
    dq, dk_all, dv_all = _attn_bwd(q_l, k_all, v_all, cat, lse, d_cat)
    tok = link.poll(dq)
    d_h, d_gate = _rnn_out_bwd(d_cat, h_f, h_r, proj_l, n_ctx, n_ctx)
    rnn_wb = [(wa_, ba_ + tok, wx_, bx_, lam_) for wa_, ba_, wx_, bx_, lam_ in rnn_w]
    dxs_f, d_wa0, d_ba0, d_wx0, d_bx0, d_lam0 = _rglru_bwd(
        xs, hp_f, d_h, *rnn_wb[0], reverse=False, n_ctx=n_ctx, name="rglru_bwd_f")
    dxs_r, d_wa1, d_ba1, d_wx1, d_bx1, d_lam1 = _rglru_bwd(
        xs, hp_r, d_h, *rnn_wb[1], reverse=True, n_ctx=n_ctx, name="rglru_bwd_r")
    d_xr_l, d_xr_c, d_cw, d_cb = _conv_bwd(dxs_f, dxs_r, proj_l, proj_c, sp['conv_w'])
    dp_qk_l, d_qg, d_kg_l = _qk_bwd(dq, dk_all, proj_l, q_g, k_g, cos_l, sin_l, name="qk_bwd_lat", dk_row0=n_ctx)
    dp_qk_c, _, d_kg_c = _qk_bwd(None, dk_all, proj_c, q_g, k_g, cos_c, sin_c, name="qk_bwd_ctx")
    d_proj = _assemble_d_proj(dp_qk_l, dp_qk_c, dv_all, d_xr_l, d_xr_c, d_gate)
    dw_ar_in = _matmul(h_all, d_proj, kind='tn', out_split='n', out_dtype=BF16, name="ar_in_dw")
    d_hl = _matmul(d_proj, big['ar_in'], kind='nt', b_split='n', out_dtype=F32, a_rows=(0, n_lat), name="ar_in_dx_lat")
    d_hc = _matmul(d_proj, big['ar_in'], kind='nt', b_split='n', out_dtype=F32, a_rows=(n_lat, n_ctx), name="ar_in_dx_ctx")
    grad_x, dg00, d_sc1, d_sh1 = _norm_bwd(d_hl, xl0, g[0][0], one(sc1), extra=dxm0, out_dtype=F32, name="l0_mod1_bwd")
    _, dg00c, d_mc_scale, d_mc_shift = _norm_bwd(d_hc, xc0, g[0][0], one(mc0[1]), out_dtype=BF16, name="l0_mod1_ctx_bwd")

    zeros_d = jnp.zeros_like(d_sh1)
    small = {
        'd_ml0': jnp.concatenate([d_sh1, d_sc1, d_gt1, d_sh2, d_sc2, d_gt2], axis=1),
        'd_ml1': jnp.concatenate([d_th1, d_tc1, d_tg1, d_th2, d_tc2, d_tg2], axis=1),
        'd_mc0': jnp.concatenate([d_mc_shift, d_mc_scale] + [zeros_d] * 4, axis=1),
        'norm_g': jnp.stack([jnp.concatenate([dg00 + dg00c, dg01, dg02, dg03], axis=0),
                             jnp.concatenate([dg10, dg11, dg12, dg13], axis=0)]),
        'q_g': d_qg, 'k_g': d_kg_l + d_kg_c, 'conv_w': d_cw, 'conv_b': d_cb,
        'wa': jnp.stack([d_wa0, d_wa1]), 'ba': jnp.concatenate([d_ba0, d_ba1], axis=0),
        'wx': jnp.stack([d_wx0, d_wx1]), 'bx': jnp.concatenate([d_bx0, d_bx1], axis=0),
        'lam': jnp.concatenate([d_lam0, d_lam1], axis=0),
        'gm_b_in': d_gm_b_in, 'gm_v_g': d_vg, 'gm_v_b': d_vb, 'gm_w_sp': d_wsp, 'gm_b_sp': d_bsp_t.T,
        'loss': loss,
    }
    return grad_x, small, {'ar_in': dw_ar_in, 'ar_out': dw_ar_out}


MOD_ROWS = 16
SMALL_F32 = ('d_ml0', 'd_ml1', 'd_mc0', 'norm_g', 'q_g', 'k_g', 'conv_w', 'conv_b', 'ba', 'bx', 'lam', 'gm_b_in', 'gm_v_g',
             'gm_v_b', 'gm_b_sp', 'loss')
SMALL_BF16 = ('wa', 'wx', 'gm_w_sp')


def _silu(v):
    return v * _sigmoid(v)


def _chip_concat(gathered, axis):
    return jnp.concatenate([gathered[2 * q] for q in range(N_CHIPS)], axis=axis)


def kernel(x, c, ctx, c_ctx, w_mod, b_mod, norm_g, w_ff_in, w_ff_out, ar_w_in, ar_q_g, ar_k_g, ar_conv_w, ar_conv_b, ar_wa, ar_ba, ar_wx, ar_bx, ar_lambda, ar_w_out, gm_w_in, gm_b_in, gm_v_g, gm_v_b, gm_w_sp, gm_b_sp, gm_w_out, loss_target, m_c_ctx, m_w_mod, m_b_mod, m_norm_g, m_w_ff_in, m_w_ff_out, m_ar_w_in, m_ar_q_g, m_ar_k_g, m_ar_conv_w, m_ar_conv_b, m_ar_wa, m_ar_ba, m_ar_wx, m_ar_bx, m_ar_lambda, m_ar_w_out, m_gm_w_in, m_gm_b_in, m_gm_v_g, m_gm_v_b, m_gm_w_sp, m_gm_b_sp, m_gm_w_out, v_c_ctx, v_w_mod, v_b_mod, v_norm_g, v_w_ff_in, v_w_ff_out, v_ar_w_in, v_ar_q_g, v_ar_k_g, v_ar_conv_w, v_ar_conv_b, v_ar_wa, v_ar_ba, v_ar_wx, v_ar_bx, v_ar_lambda, v_ar_w_out, v_gm_w_in, v_gm_b_in, v_gm_v_g, v_gm_v_b, v_gm_w_sp, v_gm_b_sp, v_gm_w_out):
    weights = dict(c_ctx=c_ctx, w_mod=w_mod, b_mod=b_mod, norm_g=norm_g, w_ff_in=w_ff_in, w_ff_out=w_ff_out, ar_w_in=ar_w_in,
                   ar_q_g=ar_q_g, ar_k_g=ar_k_g, ar_conv_w=ar_conv_w, ar_conv_b=ar_conv_b, ar_wa=ar_wa, ar_ba=ar_ba, ar_wx=ar_wx,
                   ar_bx=ar_bx, ar_lambda=ar_lambda, ar_w_out=ar_w_out, gm_w_in=gm_w_in, gm_b_in=gm_b_in, gm_v_g=gm_v_g,
                   gm_v_b=gm_v_b, gm_w_sp=gm_w_sp, gm_b_sp=gm_b_sp, gm_w_out=gm_w_out)
    m_in = dict(c_ctx=m_c_ctx, w_mod=m_w_mod, b_mod=m_b_mod, norm_g=m_norm_g, w_ff_in=m_w_ff_in, w_ff_out=m_w_ff_out,
                ar_w_in=m_ar_w_in, ar_q_g=m_ar_q_g, ar_k_g=m_ar_k_g, ar_conv_w=m_ar_conv_w, ar_conv_b=m_ar_conv_b, ar_wa=m_ar_wa,
                ar_ba=m_ar_ba, ar_wx=m_ar_wx, ar_bx=m_ar_bx, ar_lambda=m_ar_lambda, ar_w_out=m_ar_w_out, gm_w_in=m_gm_w_in,
                gm_b_in=m_gm_b_in, gm_v_g=m_gm_v_g, gm_v_b=m_gm_v_b, gm_w_sp=m_gm_w_sp, gm_b_sp=m_gm_b_sp, gm_w_out=m_gm_w_out)
    v_in = dict(c_ctx=v_c_ctx, w_mod=v_w_mod, b_mod=v_b_mod, norm_g=v_norm_g, w_ff_in=v_w_ff_in, w_ff_out=v_w_ff_out,
                ar_w_in=v_ar_w_in, ar_q_g=v_ar_q_g, ar_k_g=v_ar_k_g, ar_conv_w=v_ar_conv_w, ar_conv_b=v_ar_conv_b, ar_wa=v_ar_wa,
                ar_ba=v_ar_ba, ar_wx=v_ar_wx, ar_bx=v_ar_bx, ar_lambda=v_ar_lambda, ar_w_out=v_ar_w_out, gm_w_in=v_gm_w_in,
                gm_b_in=v_gm_b_in, gm_v_g=v_gm_v_g, gm_v_b=v_gm_v_b, gm_w_sp=v_gm_w_sp, gm_b_sp=v_gm_b_sp, gm_w_out=v_gm_w_out)

    xi, yi, ci = lax.axis_index("x"), lax.axis_index("y"), lax.axis_index("c")
    chip = 2 * xi + yi
    dev = 4 * xi + 2 * yi + ci
    place = jnp.stack([chip, ci]).astype(jnp.int32)
    n_lat, d = x.shape[1], x.shape[2]
    d6 = 6 * d
    cols_mod = w_mod.shape[2]

    mine = [c, norm_g, ar_conv_w[0], ar_ba[0], ar_bx[0], ar_lambda[0], gm_b_in, gm_v_g, gm_v_b]
    gathered = _allgather8([_pack(mine)], "gather_small_params")[0]
    parts = _unpack_devices(gathered, [a.shape for a in mine])
    c_all = parts[0].reshape(N_DEV, d)
    sp = {'norm_g': _chip_concat(parts[1], 2), 'q_g': ar_q_g, 'k_g': ar_k_g, 'conv_w': _chip_concat(parts[2], 1),
          'conv_b': ar_conv_b, 'wa': ar_wa[0], 'ba': _chip_concat(parts[3], 1), 'wx': ar_wx[0], 'bx': _chip_concat(parts[4], 1),
          'lam': _chip_concat(parts[5], 1), 'gm_b_in': _chip_concat(parts[6], 1), 'gm_v_g': _chip_concat(parts[7], 1),
          'gm_v_b': _chip_concat(parts[8], 1), 'gm_w_sp': gm_w_sp[0], 'gm_b_sp': gm_b_sp[0]}

    def mod_operand(c_rows, cc):
        row = lax.broadcasted_iota(jnp.int32, (MOD_ROWS - N_DEV, d), 0)
        lower = jnp.where(row == 0, jnp.broadcast_to(_silu(cc), (MOD_ROWS - N_DEV, d)), 0.0)
        sig = _sigmoid(cc)
        return jnp.concatenate([_silu(c_rows), lower], axis=0), sig * (1.0 + cc * (1.0 - sig))

    s_mod, dsilu_ctx = _small(mod_operand, [((MOD_ROWS, d), F32), ((1, d), F32)], c_all, c_ctx[None, :], name="mod_operand")
    b_mod_mine = lax.dynamic_slice(b_mod, (0, chip * cols_mod), (2, cols_mod))
    mod = [_matmul(s_mod, w_mod, kind='nn', b_layer=i, bias=b_mod_mine[i][None, :], out_dtype=F32, name=f"mod_fwd_{i}")
           for i in range(2)]
    mod_all = _allgather8([jnp.concatenate(mod, axis=0)], "gather_mod")[0]
    mod_all = _chip_concat(mod_all, 1).reshape(2, MOD_ROWS, d6)
    ml = [jnp.split(lax.dynamic_slice(mod_all[i], (dev, 0), (1, d6)), 6, axis=1) for i in range(2)]
    mc0 = jnp.split(mod_all[0, N_DEV:N_DEV + 1], 6, axis=1)[:2]

    names = ('w_ff_in', 'w_ff_out', 'ar_w_in', 'ar_w_out', 'gm_w_in', 'gm_w_out')
    keys = {'w_ff_in': ('ff_in0', 'ff_in1'), 'w_ff_out': ('ff_out0', 'ff_out1'), 'ar_w_in': ('ar_in',), 'ar_w_out': ('ar_out',),
            'gm_w_in': ('gm_in',), 'gm_w_out': ('gm_out',)}
    shards = {key: _cast_shard(weights[n], place, layer, f"cast_{key}") for n in names for layer, key in enumerate(keys[n])}
    link = _MeshLink(place, shards)

    grad_x, small, last_grads = _local_step(x[0], ctx[0], loss_target[0], ml, mc0, sp, link)

    def step(n, grad):
        return _adamw(weights[n], grad.reshape(weights[n].shape), m_in[n], v_in[n], f"adamw_{n}", rewrite_grad=n in names)

    small_f32, small_bf16 = [small[k] for k in SMALL_F32], [small[k] for k in SMALL_BF16]
    dev_arr = dev.astype(jnp.int32)[None]
    slots = [_into_slot(_pack(small_f32), dev_arr, "small_grads_slot_f32"),
             _into_slot(_pack(small_bf16, BF16), dev_arr, "small_grads_slot_bf16")]
    s_send, s_recv, slots, s_token = _split_start("small_grads_start", slots, _gather8_copies, 2 * N_CHIPS, grad_x)
    link.gradients('ar', last_grads, s_token)
    link.poll(link.last_token)
    reduced = link.reduce(('ffn1', 'gm', 'ffn0'), link.last_token)
    stepped = {n: step(n, reduced[names.index(n)]) for n in ('w_ff_in', 'w_ff_out', 'gm_w_in', 'gm_w_out')}
    reduced = link.reduce(('ar',), stepped['gm_w_out'][1])
    stepped.update({n: step(n, reduced[names.index(n)]) for n in ('ar_w_in', 'ar_w_out')})
    slots = _split_wait("small_grads_wait", slots, s_send, s_recv, _gather8_copies, stepped['ar_w_out'][1])
    small8, small8_bf16 = _forward_slots(slots, "small_grads_forward")
    total = dict(zip(SMALL_F32, _unpack(_sum_devices(small8).reshape(-1), [a.shape for a in small_f32])))
    total.update(zip(SMALL_BF16, _unpack(_sum_devices(small8_bf16).reshape(-1), [a.shape for a in small_bf16])))
    per_dev = _unpack_devices(small8, [(d6,), (d6,)])
    pad_rows = jnp.zeros((MOD_ROWS - N_DEV - 1, d6), F32)
    d_mod = [jnp.concatenate([per_dev[0], total['d_mc0'], pad_rows], axis=0),
             jnp.concatenate([per_dev[1], jnp.zeros((MOD_ROWS - N_DEV, d6), F32)], axis=0)]
    d_mod_mine = [lax.dynamic_slice(dm, (0, chip * cols_mod), (MOD_ROWS, cols_mod)) for dm in d_mod]
    g_w_mod = None
    for i in range(2):
        g_w_mod = _matmul(s_mod, d_mod_mine[i], kind='tn', out_dtype=F32, out_stack=(i, 2, g_w_mod), name=f"mod_dw_{i}")
    d_s_part = _matmul(d_mod_mine[0], w_mod, kind='nt', b_layer=0, out_dtype=F32, name="mod_ds")
    d_s_all = _allgather8([d_s_part[N_DEV:]], "gather_mod_ds")[0]

    def c_ctx_grad(parts_, dsilu):
        acc = parts_[0, 0:1]
        for q in range(1, N_CHIPS):
            acc = acc + parts_[2 * q, 0:1]
        return (acc * dsilu,)

    g_c_ctx = _small(c_ctx_grad, [((1, d), F32)], d_s_all, dsilu_ctx, name="c_ctx_grad")[0].reshape(d)

    def mine_of(full_grad, axis, n_shard):
        return lax.dynamic_slice_in_dim(full_grad, chip * n_shard, n_shard, axis=axis)

    grads_out = {
        'c_ctx': g_c_ctx, 'w_mod': g_w_mod,
        'b_mod': jnp.stack([total['d_ml0'][0] + total['d_mc0'][0], total['d_ml1'][0]]),
        'norm_g': mine_of(total['norm_g'], 2, norm_g.shape[2]),
        'ar_q_g': total['q_g'], 'ar_k_g': total['k_g'], 'ar_conv_w': mine_of(total['conv_w'], 1, ar_conv_w.shape[2])[None],
        'ar_conv_b': total['conv_b'], 'ar_wa': total['wa'][None], 'ar_ba': mine_of(total['ba'], 1, ar_ba.shape[2])[None],
        'ar_wx': total['wx'][None], 'ar_bx': mine_of(total['bx'], 1, ar_bx.shape[2])[None],
        'ar_lambda': mine_of(total['lam'], 1, ar_lambda.shape[2])[None],
        'gm_b_in': mine_of(total['gm_b_in'], 1, gm_b_in.shape[1]),
        'gm_v_g': mine_of(total['gm_v_g'], 1, gm_v_g.shape[1]), 'gm_v_b': mine_of(total['gm_v_b'], 1, gm_v_b.shape[1]),
        'gm_w_sp': total['gm_w_sp'][None], 'gm_b_sp': total['gm_b_sp'][None],
    }
    stepped.update({n: step(n, grad) for n, grad in grads_out.items()})
    stepped = [stepped[n] for n in weights]
    loss = total['loss'].reshape(())
    return (loss, grad_x[None], *[s[0] for s in stepped], *[s[1] for s in stepped], *[s[2] for s in stepped],
            *[s[3] for s in stepped])
```

```python
import functools
import math

import jax
import jax.numpy as jnp
from jax import lax
from jax.experimental import pallas as pl
from jax.experimental.pallas import tpu as pltpu

F32 = jnp.float32
BF16 = jnp.bfloat16
MESH = pl.DeviceIdType.MESH
ANY = pl.BlockSpec(memory_space=pl.ANY)

VMEM_LIMIT_BYTES = 52 * 1024 * 1024
LANES = 128
N_CHIPS = 4
N_DEV = 8

HEAD_DIM = 128
N_HEADS = 8
N_KV = 2
GROUP = N_HEADS // N_KV
ATTN_W = N_HEADS * HEAD_DIM
KV_W = N_KV * HEAD_DIM
D_RNN = 1024
RNN_BLOCKS = 8
RNN_BW = D_RNN // RNN_BLOCKS
CONV_W = 4
RG_C = 8.0
GRID_W = 64
ROPE_THETA = 10000.0
ROPE_PAIRS = HEAD_DIM // 4
GM_GROUPS = 16
CHUNK = 128
EPS = 1e-6
ADAM_LR, ADAM_B1, ADAM_B2, ADAM_EPS, ADAM_WD, ADAM_STEP = 0.001, 0.9, 0.999, 1e-08, 0.01, 10
GELU_C = math.sqrt(2.0 / math.pi)
LOG2E = math.log2(math.e)


def _params(sem=None):
    return pltpu.CompilerParams(dimension_semantics=sem, vmem_limit_bytes=VMEM_LIMIT_BYTES)


def _tile(dim, pref, unit):
    best = None
    t = unit
    while t <= min(dim, pref):
        if dim % t == 0:
            best = t
        t += unit
    return best if best is not None else dim


def _full(shape):
    nd = len(shape)
    return pl.BlockSpec(shape, lambda *_: (0,) * nd)


def _blocked_map(split, per_q):
    assert split == 'n'
    return lambda r, c: (c // per_q, r, c % per_q)


def _logical_shape(arr, split):
    if split == 'n':
        return arr.shape[1], arr.shape[0] * arr.shape[2]
    if split == 'k':
        return arr.shape[0] * arr.shape[1], arr.shape[2]
    return arr.shape[-2:]


def _matmul(a, b, *, kind, name, out_dtype, b_split=None, out_split=None, bias=None, epilogue=None, extra=None,
            a_rows=None, b_layer=None, out_stack=None, pref=(1024, 1024, 2048)):
    if b_split == 'k':
        b, b_split = b.reshape(-1, b.shape[-1]), None
    blocked_rows_out = out_split == 'k'
    if blocked_rows_out:
        assert epilogue != 'relu2'
        out_split = None
    b_rows, b_cols = _logical_shape(b, b_split)
    row0 = 0
    if kind == 'nn':
        m, kc = a.shape
        n = b_cols
        assert b_rows == kc
    elif kind == 'nt':
        m, kc = a.shape
        n = b_rows
        assert b_cols == kc
    if a_rows is not None:
        assert kind != 'tn'
        row0, m = a_rows
    if kind == 'tn':
        kc, m = a.shape
        n = b_cols
        assert b_rows == kc
    b_row_ext = b.shape[1] if b_split == 'k' else b_rows
    b_col_ext = b.shape[2] if b_split == 'n' else b_cols
    out_row_ext = m // N_CHIPS if out_split == 'k' else m
    out_col_ext = n // N_CHIPS if out_split == 'n' else n
    if kind == 'nn':
        ti = _tile(math.gcd(min(m, out_row_ext), row0), pref[0], 16)
        tj = _tile(math.gcd(b_col_ext, out_col_ext), pref[1], LANES)
        tl = _tile(b_row_ext, pref[2], LANES)
        a_spec = pl.BlockSpec((ti, tl), lambda i, j, l: (i + row0 // ti, l))
        b_tile, b_rc = (tl, tj), (lambda i, j, l: (l, j))
        dims = (((1,), (0,)), ((), ()))
    elif kind == 'nt':
        ti = _tile(math.gcd(min(m, out_row_ext), row0), pref[0], 16)
        tj = _tile(math.gcd(b_row_ext, out_col_ext), pref[1], LANES)
        tl = _tile(b_col_ext, pref[2], LANES)
        a_spec = pl.BlockSpec((ti, tl), lambda i, j, l: (i + row0 // ti, l))
        b_tile, b_rc = (tj, tl), (lambda i, j, l: (j, l))
        dims = (((1,), (1,)), ((), ()))
    else:
        ti = _tile(out_row_ext, pref[0], LANES)
        tj = _tile(math.gcd(b_col_ext, out_col_ext), pref[1], LANES)
        tl = _tile(b_row_ext, pref[2], 16)
        a_spec = pl.BlockSpec((tl, ti), lambda i, j, l: (l, i))
        b_tile, b_rc = (tl, tj), (lambda i, j, l: (l, j))
        dims = (((0,), (0,)), ((), ()))
    grid = (m // ti, n // tj, kc // tl)
    n_l = grid[2]

    if b_layer is not None:
        b_spec = pl.BlockSpec((None,) + b_tile, lambda i, j, l: (b_layer, *b_rc(i, j, l)))
    elif b_split is None:
        b_spec = pl.BlockSpec(b_tile, b_rc)
    else:
        per_q = (b.shape[2] // b_tile[1]) if b_split == 'n' else (b.shape[1] // b_tile[0])
        bmap = _blocked_map(b_split, per_q)
        b_spec = pl.BlockSpec((None,) + b_tile, lambda i, j, l: bmap(*b_rc(i, j, l)))
    if out_stack is not None:
        layer, n_layers, _ = out_stack
        out_shape2 = (n_layers, m, n)
        o_spec = pl.BlockSpec((None, ti, tj), lambda i, j, l: (layer, i, j))
    elif out_split is None:
        out_shape2 = (m, n)
        o_spec = pl.BlockSpec((ti, tj), lambda i, j, l: (i, j))
    else:
        out_shape2 = (N_CHIPS, m // N_CHIPS, n) if out_split == 'k' else (N_CHIPS, m, n // N_CHIPS)
        per_q = (out_shape2[2] // tj) if out_split == 'n' else (out_shape2[1] // ti)
        omap = _blocked_map(out_split, per_q)
        o_spec = pl.BlockSpec((None, ti, tj), lambda i, j, l: omap(i, j))

    in_specs = [a_spec, b_spec]
    operands = [a, b]
    if bias is not None:
        in_specs.append(pl.BlockSpec((1, tj), lambda i, j, l: (0, j)))
        operands.append(bias)
    if extra is not None:
        in_specs.append(pl.BlockSpec((ti, tj), lambda i, j, l: (i, j)))
        operands.append(extra)
    if epilogue == 'relu2':
        out_shape = (jax.ShapeDtypeStruct(out_shape2, out_dtype), jax.ShapeDtypeStruct(out_shape2, out_dtype))
        out_specs = (o_spec, o_spec)
    else:
        out_shape = jax.ShapeDtypeStruct(out_shape2, out_dtype)
        out_specs = o_spec
    has_bias, has_extra = bias is not None, extra is not None
    has_dest = out_stack is not None and out_stack[2] is not None
    if has_dest:
        in_specs.append(ANY)
        operands.append(out_stack[2])

    def body(*refs):
        a_ref, b_ref = refs[0], refs[1]
        pos = 2
        bias_ref = extra_ref = None
        if has_bias:
            bias_ref = refs[pos]
            pos += 1
        if has_extra:
            extra_ref = refs[pos]
            pos += 1
        if has_dest:
            pos += 1
        outs = refs[pos:] if n_l == 1 else refs[pos:-1]

        def finish(acc):
            if has_bias:
                acc = acc + bias_ref[...]
            if epilogue == 'relu2':
                r = jnp.maximum(acc, 0.0)
                outs[0][...] = r.astype(outs[0].dtype)
                outs[1][...] = (r * r).astype(outs[1].dtype)
            elif epilogue == 'times2x':
                outs[0][...] = (acc * (2.0 * extra_ref[...].astype(F32))).astype(outs[0].dtype)
            else:
                outs[0][...] = acc.astype(outs[0].dtype)

        def product():
            return lax.dot_general(a_ref[...].astype(BF16), b_ref[...].astype(BF16), dims, preferred_element_type=F32)

        if n_l == 1:
            finish(product())
            return
        acc_ref = refs[-1]
        step = pl.program_id(2)

        @pl.when(step == 0)
        def _():
            acc_ref[...] = jnp.zeros_like(acc_ref)

        acc_ref[...] += product()

        @pl.when(step == n_l - 1)
        def _():
            finish(acc_ref[...])

    result = pl.pallas_call(
        body, name=name, grid=grid, in_specs=in_specs, out_specs=out_specs, out_shape=out_shape,
        input_output_aliases={len(operands) - 1: 0} if has_dest else {},
        scratch_shapes=[] if n_l == 1 else [pltpu.VMEM((ti, tj), F32)],
        compiler_params=_params(("parallel", "parallel", "arbitrary")),
    )(*operands)
    return result.reshape(N_CHIPS, m // N_CHIPS, n) if blocked_rows_out else result


def _small(fn, out_shapes, *arrays, name):
    n_in = len(arrays)

    def body(*refs):
        res = fn(*[r[...] for r in refs[:n_in]])
        for o_ref, v in zip(refs[n_in:], res):
            o_ref[...] = v.astype(o_ref.dtype)

    return pl.pallas_call(
        body, name=name, out_shape=tuple(jax.ShapeDtypeStruct(s, d) for s, d in out_shapes),
        in_specs=[_full(a.shape) for a in arrays], out_specs=tuple(_full(s) for s, _ in out_shapes), grid=(1,),
        compiler_params=_params(("arbitrary",)),
    )(*arrays)


def _rows_tile(rows, cols, itemsize=4, budget=2 * 1024 * 1024):
    return _tile(rows, max(16, budget // (cols * itemsize)), 16)


def _rowwise(fn, out_dtypes, *arrays, name):
    rows, cols = arrays[0].shape
    tr = _rows_tile(rows, cols)
    n_in = len(arrays)

    def body(*refs):
        res = fn(*[r[...] for r in refs[:n_in]])
        for o_ref, v in zip(refs[n_in:], res):
            o_ref[...] = v.astype(o_ref.dtype)

    spec = pl.BlockSpec((tr, cols), lambda i: (i, 0))
    return pl.pallas_call(
        body, name=name, grid=(rows // tr,), in_specs=[spec] * n_in, out_specs=tuple(spec for _ in out_dtypes),
        out_shape=tuple(jax.ShapeDtypeStruct((rows, cols), d) for d in out_dtypes),
        compiler_params=_params(("parallel",)),
    )(*arrays)


def _as2d(a):
    return a.reshape(1, a.size) if a.ndim < 2 else a.reshape(-1, a.shape[-1])


def _cast_shard(w, place, layer, name, after=None):
    _, rows, cols = w.shape
    tr = _rows_tile(rows, cols)

    def body(place_ref, w_ref, *rest):
        rest[-1][...] = w_ref[...].astype(rest[-1].dtype)

    return pl.pallas_call(
        body, name=name, out_shape=jax.ShapeDtypeStruct((N_CHIPS, rows, cols), BF16),
        grid_spec=pltpu.PrefetchScalarGridSpec(
            num_scalar_prefetch=1, grid=(rows // tr,),
            in_specs=[pl.BlockSpec((None, tr, cols), lambda i, pr: (layer, i, 0))] + ([] if after is None else [ANY]),
            out_specs=pl.BlockSpec((None, tr, cols), lambda i, pr: (pr[0], i, 0))),
        compiler_params=_params(("parallel",)),
    )(place, w, *([] if after is None else [after]))


def _norm_fwd(x, g, a, b=None, res=None, *, out_dtype, name, into=None):
    rows, d = x.shape
    row0, total, dest = into if into is not None else (0, rows, None)
    tr = _rows_tile(math.gcd(rows, row0), d, budget=4 * 1024 * 1024)
    has_b, has_res = b is not None, res is not None

    def body(*refs):
        x_ref, g_ref, a_ref = refs[:3]
        pos = 3
        xv = x_ref[...]
        rstd = lax.rsqrt(jnp.mean(xv * xv, axis=-1, keepdims=True) + EPS)
        y = (xv * rstd * g_ref[...]) * a_ref[...]
        if has_b:
            y = y + refs[pos][...]
            pos += 1
        if has_res:
            y = y + refs[pos][...]
            pos += 1
        refs[-1][...] = y.astype(refs[-1].dtype)

    row = pl.BlockSpec((tr, d), lambda i: (i, 0))
    vec = pl.BlockSpec((1, d), lambda i: (0, 0))
    operands, specs = [x, g, a], [row, vec, vec]
    if has_b:
        operands.append(b)
        specs.append(vec)
    if has_res:
        operands.append(res)
        specs.append(row)
    if dest is not None:
        operands.append(dest)
        specs.append(ANY)
    return pl.pallas_call(
        body, name=name, grid=(rows // tr,), in_specs=specs, out_specs=pl.BlockSpec((tr, d), lambda i: (i + row0 // tr, 0)),
        out_shape=jax.ShapeDtypeStruct((total, d), out_dtype), compiler_params=_params(("parallel",)),
        input_output_aliases={} if dest is None else {len(operands) - 1: 0},
    )(*operands)


def _rstd(v):
    return lax.rsqrt(jnp.mean(v * v, axis=-1, keepdims=True) + EPS)


def _res_mod_fwd(o, x, g_res, gate, g_mod, a_mod, b_mod, *, name):
    rows, d = x.shape
    tr = _rows_tile(rows, d)

    def body(o_ref, x_ref, gr_ref, gate_ref, gm_ref, a_ref, b_ref, xm_ref, h_ref):
        ov = o_ref[...]
        xm = x_ref[...] + (ov * _rstd(ov) * gr_ref[...]) * gate_ref[...]
        xm_ref[...] = xm
        h_ref[...] = ((xm * _rstd(xm) * gm_ref[...]) * a_ref[...] + b_ref[...]).astype(h_ref.dtype)

    row = pl.BlockSpec((tr, d), lambda i: (i, 0))
    vec = pl.BlockSpec((1, d), lambda i: (0, 0))
    return pl.pallas_call(
        body, name=name, grid=(rows // tr,), in_specs=[row, row, vec, vec, vec, vec, vec], out_specs=(row, row),
        out_shape=(jax.ShapeDtypeStruct((rows, d), F32), jax.ShapeDtypeStruct((rows, d), BF16)),
        compiler_params=_params(("parallel",)),
    )(o, x, g_res, gate, g_mod, a_mod, b_mod)


def _mod_res_bwd(d_h, xm, g_mod, a_mod, extra, o, g_res, gate, *, name):
    rows, d = xm.shape
    tr = _rows_tile(rows, d)

    def body(dh_ref, xm_ref, gm_ref, a_ref, ex_ref, o_ref, gr_ref, gate_ref,
             dxm_ref, do_ref, dgm_ref, da_ref, db_ref, dgr_ref, dgate_ref):
        @pl.when(pl.program_id(0) == 0)
        def _():
            for ref in (dgm_ref, da_ref, db_ref, dgr_ref, dgate_ref):
                ref[...] = jnp.zeros_like(ref)

        def norm_adjoint(dy, xv, gain, scale, dgain_ref, dscale_ref):
            rstd = _rstd(xv)
            nrm = xv * rstd
            dscale_ref[...] += jnp.sum(dy * (nrm * gain), axis=0, keepdims=True)
            dt = dy * scale
            dgain_ref[...] += jnp.sum(dt * nrm, axis=0, keepdims=True)
            dn = dt * gain
            return rstd * (dn - nrm * jnp.mean(dn * nrm, axis=-1, keepdims=True))

        dhv = dh_ref[...].astype(F32)
        db_ref[...] += jnp.sum(dhv, axis=0, keepdims=True)
        dxm = norm_adjoint(dhv, xm_ref[...], gm_ref[...], a_ref[...], dgm_ref, da_ref) + ex_ref[...]
        dxm_ref[...] = dxm
        do_ref[...] = norm_adjoint(dxm, o_ref[...], gr_ref[...], gate_ref[...], dgr_ref, dgate_ref).astype(do_ref.dtype)

    row = pl.BlockSpec((tr, d), lambda i: (i, 0))
    vec = pl.BlockSpec((1, d), lambda i: (0, 0))
    vshape = jax.ShapeDtypeStruct((1, d), F32)
    return pl.pallas_call(
        body, name=name, grid=(rows // tr,), in_specs=[row, row, vec, vec, row, row, vec, vec],
        out_specs=(row, row, vec, vec, vec, vec, vec),
        out_shape=(jax.ShapeDtypeStruct((rows, d), F32), jax.ShapeDtypeStruct((rows, d), BF16)) + (vshape,) * 5,
        compiler_params=_params(("arbitrary",)),
    )(d_h, xm, g_mod, a_mod, extra, o, g_res, gate)


def _norm_bwd(dy, x, g, a, extra=None, *, out_dtype, name):
    rows, d = x.shape
    tr = _rows_tile(rows, d)
    has_extra = extra is not None

    def body(*refs):
        dy_ref, x_ref, g_ref, a_ref = refs[:4]
        pos = 4
        extra_ref = None
        if has_extra:
            extra_ref = refs[pos]
            pos += 1
        dx_ref, dg_ref, da_ref, db_ref = refs[pos:pos + 4]

        @pl.when(pl.program_id(0) == 0)
        def _():
            dg_ref[...] = jnp.zeros_like(dg_ref)
            da_ref[...] = jnp.zeros_like(da_ref)
            db_ref[...] = jnp.zeros_like(db_ref)

        xv = x_ref[...]
        dyv = dy_ref[...].astype(F32)
        rstd = lax.rsqrt(jnp.mean(xv * xv, axis=-1, keepdims=True) + EPS)
        nrm = xv * rstd
        gv = g_ref[...]
        da_ref[...] += jnp.sum(dyv * (nrm * gv), axis=0, keepdims=True)
        db_ref[...] += jnp.sum(dyv, axis=0, keepdims=True)
        dt = dyv * a_ref[...]
        dg_ref[...] += jnp.sum(dt * nrm, axis=0, keepdims=True)
        dn = dt * gv
        dx = rstd * (dn - nrm * jnp.mean(dn * nrm, axis=-1, keepdims=True))
        if has_extra:
            dx = dx + extra_ref[...]
        dx_ref[...] = dx.astype(dx_ref.dtype)

    row = pl.BlockSpec((tr, d), lambda i: (i, 0))
    vec = pl.BlockSpec((1, d), lambda i: (0, 0))
    operands, specs = [dy, x, g, a], [row, row, vec, vec]
    if has_extra:
        operands.append(extra)
        specs.append(row)
    vshape = jax.ShapeDtypeStruct((1, d), F32)
    return pl.pallas_call(
        body, name=name, grid=(rows // tr,), in_specs=specs, out_specs=(row, vec, vec, vec),
        out_shape=(jax.ShapeDtypeStruct((rows, d), out_dtype), vshape, vshape, vshape),
        compiler_params=_params(("arbitrary",)),
    )(*operands)


def _loss_head(y, target):
    rows, d = y.shape
    tr = _rows_tile(rows, d)

    def body(y_ref, t_ref, dy_ref, loss_ref):
        @pl.when(pl.program_id(0) == 0)
        def _():
            loss_ref[...] = jnp.zeros_like(loss_ref)

        err = y_ref[...] - t_ref[...]
        dy_ref[...] = err * (1.0 / d)
        loss_ref[...] += jnp.sum(jnp.sum(err * err, axis=-1, keepdims=True), axis=0, keepdims=True) * (0.5 / d)

    row = pl.BlockSpec((tr, d), lambda i: (i, 0))
    return pl.pallas_call(
        body, name="loss_head", grid=(rows // tr,), in_specs=[row, row], out_specs=(row, _full((1, 1))),
        out_shape=(jax.ShapeDtypeStruct((rows, d), F32), jax.ShapeDtypeStruct((1, 1), F32)),
        compiler_params=_params(("arbitrary",)),
    )(y, target)


def _rope_partner(v):
    lane = lax.broadcasted_iota(jnp.int32, v.shape, 1)
    up = pltpu.roll(v, HEAD_DIM - ROPE_PAIRS, 1)
    down = pltpu.roll(v, ROPE_PAIRS, 1)
    return jnp.where((lane % (2 * ROPE_PAIRS)) < ROPE_PAIRS, up, down)


def _qk_fwd(proj, q_g, k_g, cos, sin, *, name, kv_into=None):
    rows = proj.shape[0]
    row0, total, kv_dest = kv_into if kv_into is not None else (0, rows, None)
    tr = _tile(math.gcd(rows, row0), 256, 16)
    width = ATTN_W + 2 * KV_W

    def body(p_ref, qg_ref, kg_ref, cos_ref, sin_ref, *rest):
        q_ref, k_ref, v_ref = rest[-3:]
        cosv, sinv = cos_ref[...], sin_ref[...]
        for h in range(N_HEADS + N_KV):
            xv = p_ref[:, h * HEAD_DIM:(h + 1) * HEAD_DIM]
            gain = qg_ref[...] if h < N_HEADS else kg_ref[...]
            t = xv * lax.rsqrt(jnp.mean(xv * xv, axis=-1, keepdims=True) + EPS) * gain
            y = t * cosv + _rope_partner(t) * sinv
            if h < N_HEADS:
                q_ref[:, h * HEAD_DIM:(h + 1) * HEAD_DIM] = y.astype(BF16)
            else:
                k_ref[:, (h - N_HEADS) * HEAD_DIM:(h - N_HEADS + 1) * HEAD_DIM] = y.astype(BF16)
        v_ref[...] = p_ref[:, ATTN_W + KV_W:width].astype(BF16)

    vec = _full((1, HEAD_DIM))
    tab = pl.BlockSpec((tr, HEAD_DIM), lambda i: (i, 0))
    kv_spec = pl.BlockSpec((tr, KV_W), lambda i: (i + row0 // tr, 0))
    kv_shape = jax.ShapeDtypeStruct((total, KV_W), BF16)
    return pl.pallas_call(
        body, name=name, grid=(rows // tr,),
        in_specs=[pl.BlockSpec((tr, width), lambda i: (i, 0)), vec, vec, tab, tab] + ([] if kv_dest is None else [ANY, ANY]),
        out_specs=(pl.BlockSpec((tr, ATTN_W), lambda i: (i, 0)), kv_spec, kv_spec),
        out_shape=(jax.ShapeDtypeStruct((rows, ATTN_W), BF16), kv_shape, kv_shape),
        input_output_aliases={} if kv_dest is None else {5: 1, 6: 2},
        compiler_params=_params(("parallel",)),
    )(proj, q_g, k_g, cos, sin, *([] if kv_dest is None else kv_dest))


def _qk_bwd(dq, dk, proj, q_g, k_g, cos, sin, *, name, dk_row0=0):
    rows = proj.shape[0]
    tr = _tile(math.gcd(rows, dk_row0), 256, 16)
    width = ATTN_W + KV_W
    has_q = dq is not None

    def body(*refs):
        pos = 0
        dq_ref = None
        if has_q:
            dq_ref = refs[0]
            pos = 1
        dk_ref, p_ref, qg_ref, kg_ref, cos_ref, sin_ref, dp_ref, dqg_ref, dkg_ref = refs[pos:pos + 9]

        @pl.when(pl.program_id(0) == 0)
        def _():
            dqg_ref[...] = jnp.zeros_like(dqg_ref)
            dkg_ref[...] = jnp.zeros_like(dkg_ref)

        cosv, sinv = cos_ref[...], sin_ref[...]
        for h in range(N_HEADS + N_KV):
            cols = slice(h * HEAD_DIM, (h + 1) * HEAD_DIM)
            if h < N_HEADS and not has_q:
                dp_ref[:, cols] = jnp.zeros((tr, HEAD_DIM), dp_ref.dtype)
                continue
            if h < N_HEADS:
                dyv, gain, dgain_ref = dq_ref[:, cols], qg_ref[...], dqg_ref
            else:
                hk = h - N_HEADS
                dyv, gain, dgain_ref = dk_ref[:, hk * HEAD_DIM:(hk + 1) * HEAD_DIM], kg_ref[...], dkg_ref
            dyv = dyv.astype(F32)
            dt = dyv * cosv + _rope_partner(dyv * sinv)
            xv = p_ref[:, cols]
            rstd = lax.rsqrt(jnp.mean(xv * xv, axis=-1, keepdims=True) + EPS)
            nrm = xv * rstd
            dgain_ref[...] += jnp.sum(dt * nrm, axis=0, keepdims=True)
            dn = dt * gain
            dp_ref[:, cols] = (rstd * (dn - nrm * jnp.mean(dn * nrm, axis=-1, keepdims=True))).astype(dp_ref.dtype)

    vec = _full((1, HEAD_DIM))
    tab = pl.BlockSpec((tr, HEAD_DIM), lambda i: (i, 0))
    operands = ([dq] if has_q else []) + [dk, proj, q_g, k_g, cos, sin]
    specs = ([pl.BlockSpec((tr, ATTN_W), lambda i: (i, 0))] if has_q else []) + [
        pl.BlockSpec((tr, KV_W), lambda i: (i + dk_row0 // tr, 0)), pl.BlockSpec((tr, width), lambda i: (i, 0)), vec, vec, tab, tab]
    return pl.pallas_call(
        body, name=name, grid=(rows // tr,), in_specs=specs,
        out_specs=(pl.BlockSpec((tr, width), lambda i: (i, 0)), vec, vec),
        out_shape=(jax.ShapeDtypeStruct((rows, width), BF16), jax.ShapeDtypeStruct((1, HEAD_DIM), F32),
                   jax.ShapeDtypeStruct((1, HEAD_DIM), F32)),
        compiler_params=_params(("arbitrary",)),
    )(*operands)


def _attn_fwd(q, k, v):
    n_q, n_k = q.shape[0], k.shape[0]
    tq = _tile(n_q, 512, 16)
    gw = GROUP * HEAD_DIM
    scale = HEAD_DIM ** -0.5

    def body(q_ref, k_ref, v_ref, o_ref, lse_ref):
        kv, vv = k_ref[...], v_ref[...]
        for g in range(GROUP):
            cols = slice(g * HEAD_DIM, (g + 1) * HEAD_DIM)
            s = lax.dot_general(q_ref[:, cols], kv, (((1,), (1,)), ((), ())), preferred_element_type=F32) * (scale * LOG2E)
            m = jnp.max(s, axis=-1, keepdims=True)
            p = jnp.exp2(s - m)
            l = jnp.sum(p, axis=-1, keepdims=True)
            o = jnp.dot(p.astype(BF16), vv, preferred_element_type=F32) / l
            o_ref[:, cols] = o.astype(o_ref.dtype)
            lse_ref[:, g:g + 1] = m + jnp.log(l) * LOG2E

    return pl.pallas_call(
        body, name="attn_fwd", grid=(N_KV, n_q // tq),
        in_specs=[pl.BlockSpec((tq, gw), lambda h, i: (i, h)), pl.BlockSpec((n_k, HEAD_DIM), lambda h, i: (0, h)),
                  pl.BlockSpec((n_k, HEAD_DIM), lambda h, i: (0, h))],
        out_specs=(pl.BlockSpec((tq, gw), lambda h, i: (i, h)), pl.BlockSpec((None, tq, GROUP), lambda h, i: (h, i, 0))),
        out_shape=(jax.ShapeDtypeStruct((n_q, ATTN_W + D_RNN), BF16), jax.ShapeDtypeStruct((N_KV, n_q, GROUP), F32)),
        compiler_params=_params(("parallel", "parallel")),
    )(q, k, v)


def _attn_bwd(q, k, v, o, lse, do):
    n_q, n_k = q.shape[0], k.shape[0]
    tq = _tile(n_q, 256, 16)
    gw = GROUP * HEAD_DIM
    scale = HEAD_DIM ** -0.5

    def body(q_ref, k_ref, v_ref, o_ref, lse_ref, do_ref, dq_ref, dk_ref, dv_ref):
        @pl.when(pl.program_id(1) == 0)
        def _():
            dk_ref[...] = jnp.zeros_like(dk_ref)
            dv_ref[...] = jnp.zeros_like(dv_ref)

        kv, vv = k_ref[...], v_ref[...]
        for g in range(GROUP):
            cols = slice(g * HEAD_DIM, (g + 1) * HEAD_DIM)
            qg = q_ref[:, cols]
            dof = do_ref[:, cols].astype(F32)
            dog = dof.astype(BF16)
            s = lax.dot_general(qg, kv, (((1,), (1,)), ((), ())), preferred_element_type=F32) * (scale * LOG2E)
            p = jnp.exp2(s - lse_ref[:, g:g + 1])
            delta = jnp.sum(dof * o_ref[:, cols].astype(F32), axis=-1, keepdims=True)
            dp = lax.dot_general(dog, vv, (((1,), (1,)), ((), ())), preferred_element_type=F32)
            ds = (p * (dp - delta) * scale).astype(BF16)
            pb = p.astype(BF16)
            dq_ref[:, cols] = jnp.dot(ds, kv, preferred_element_type=F32)
            dk_ref[...] += lax.dot_general(ds, qg, (((0,), (0,)), ((), ())), preferred_element_type=F32)
            dv_ref[...] += lax.dot_general(pb, dog, (((0,), (0,)), ((), ())), preferred_element_type=F32)

    qspec = pl.BlockSpec((tq, gw), lambda h, i: (i, h))
    kspec = pl.BlockSpec((n_k, HEAD_DIM), lambda h, i: (0, h))
    return pl.pallas_call(
        body, name="attn_bwd", grid=(N_KV, n_q // tq),
        in_specs=[qspec, kspec, kspec, qspec, pl.BlockSpec((None, tq, GROUP), lambda h, i: (h, i, 0)), qspec],
        out_specs=(qspec, kspec, kspec),
        out_shape=(jax.ShapeDtypeStruct((n_q, ATTN_W), F32), jax.ShapeDtypeStruct((n_k, KV_W), F32),
                   jax.ShapeDtypeStruct((n_k, KV_W), F32)),
        compiler_params=_params(("parallel", "arbitrary")),
    )(q, k, v, o, lse, do)


CONV_COLS = 256
XR_COL0 = ATTN_W + 2 * KV_W


def _shift_rows(v, off):
    if off == 0:
        return v
    n = v.shape[0]
    rolled = pltpu.roll(v, (-off) % n, 0)
    t = lax.broadcasted_iota(jnp.int32, v.shape, 0)
    keep = (t + off >= 0) & (t + off < n)
    return jnp.where(keep, rolled, 0.0)


def _conv_fwd(proj_l, proj_c, w, b):
    n_lat, n_ctx = proj_l.shape[0], proj_c.shape[0]
    blk0 = XR_COL0 // CONV_COLS

    def body(xl_ref, xc_ref, w_ref, b_ref, y_ref):
        for x_ref, rows in ((xc_ref, slice(0, n_ctx)), (xl_ref, slice(n_ctx, n_ctx + n_lat))):
            xv = x_ref[...]
            y = b_ref[...] + jnp.zeros_like(xv)
            for j in range(CONV_W):
                y = y + _shift_rows(xv, j - CONV_W // 2) * w_ref[j:j + 1, :]
            y_ref[rows, :] = y

    return pl.pallas_call(
        body, name="conv_fwd", grid=(D_RNN // CONV_COLS,),
        in_specs=[pl.BlockSpec((n_lat, CONV_COLS), lambda i: (0, blk0 + i)), pl.BlockSpec((n_ctx, CONV_COLS), lambda i: (0, blk0 + i)),
                  pl.BlockSpec((CONV_W, CONV_COLS), lambda i: (0, i)), pl.BlockSpec((1, CONV_COLS), lambda i: (0, i))],
        out_specs=pl.BlockSpec((n_ctx + n_lat, CONV_COLS), lambda i: (0, i)),
        out_shape=jax.ShapeDtypeStruct((n_ctx + n_lat, D_RNN), F32), compiler_params=_params(("parallel",)),
    )(proj_l, proj_c, w, b)


def _conv_bwd(d1, d2, proj_l, proj_c, w):
    n_lat, n_ctx = proj_l.shape[0], proj_c.shape[0]
    blk0 = XR_COL0 // CONV_COLS

    def body(d1_ref, d2_ref, xl_ref, xc_ref, w_ref, dxl_ref, dxc_ref, dw_ref, db_ref):
        dw = [0.0] * CONV_W
        db = 0.0
        for x_ref, dx_ref, rows in ((xc_ref, dxc_ref, slice(0, n_ctx)), (xl_ref, dxl_ref, slice(n_ctx, n_ctx + n_lat))):
            dv = d1_ref[rows, :] + d2_ref[rows, :]
            xv = x_ref[...]
            dx = jnp.zeros_like(dv)
            for j in range(CONV_W):
                off = j - CONV_W // 2
                dx = dx + _shift_rows(dv, -off) * w_ref[j:j + 1, :]
                dw[j] = dw[j] + jnp.sum(dv * _shift_rows(xv, off), axis=0, keepdims=True)
            dx_ref[...] = dx.astype(dx_ref.dtype)
            db = db + jnp.sum(dv, axis=0, keepdims=True)
        for j in range(CONV_W):
            dw_ref[j:j + 1, :] = dw[j]
        db_ref[...] = db

    both = pl.BlockSpec((n_ctx + n_lat, CONV_COLS), lambda i: (0, i))
    return pl.pallas_call(
        body, name="conv_bwd", grid=(D_RNN // CONV_COLS,),
        in_specs=[both, both, pl.BlockSpec((n_lat, CONV_COLS), lambda i: (0, blk0 + i)),
                  pl.BlockSpec((n_ctx, CONV_COLS), lambda i: (0, blk0 + i)), pl.BlockSpec((CONV_W, CONV_COLS), lambda i: (0, i))],
        out_specs=(pl.BlockSpec((n_lat, CONV_COLS), lambda i: (0, i)), pl.BlockSpec((n_ctx, CONV_COLS), lambda i: (0, i)),
                   pl.BlockSpec((CONV_W, CONV_COLS), lambda i: (0, i)), pl.BlockSpec((1, CONV_COLS), lambda i: (0, i))),
        out_shape=(jax.ShapeDtypeStruct((n_lat, D_RNN), BF16), jax.ShapeDtypeStruct((n_ctx, D_RNN), BF16),
                   jax.ShapeDtypeStruct((CONV_W, D_RNN), F32), jax.ShapeDtypeStruct((1, D_RNN), F32)),
        compiler_params=_params(("parallel",)),
    )(d1, d2, proj_l, proj_c, w)


RNN_TB = 256
SCAN_ROWS = 8


def _sigmoid(z):
    return 1.0 / (1.0 + jnp.exp(-z))


def _softplus(z):
    return jnp.maximum(z, 0.0) + jnp.log(1.0 + jnp.exp(-jnp.abs(z)))


def _one_minus_exp(y):
    series = -y * (1.0 + y * (0.5 + y * (1.0 / 6.0 + y * (1.0 / 24.0))))
    return jnp.where(y > -0.03, series, 1.0 - jnp.exp(y))


def _rglru_gates(xv, wa_ref, ba_ref, wx_ref, bx_ref, lam_ref):
    xb = xv.astype(BF16)
    zr = jnp.concatenate([jnp.dot(xb[:, n * RNN_BW:(n + 1) * RNN_BW], wa_ref[n].astype(BF16),
                                  preferred_element_type=F32) for n in range(RNN_BLOCKS)], axis=-1) + ba_ref[...]
    zi = jnp.concatenate([jnp.dot(xb[:, n * RNN_BW:(n + 1) * RNN_BW], wx_ref[n].astype(BF16),
                                  preferred_element_type=F32) for n in range(RNN_BLOCKS)], axis=-1) + bx_ref[...]
    r = _sigmoid(zr)
    gi = _sigmoid(zi)
    sp = _softplus(-lam_ref[...])
    log_a = -RG_C * r * sp
    a = jnp.exp(log_a)
    s = jnp.sqrt(_one_minus_exp(2.0 * log_a))
    return r, gi, sp, a, s


def _scan_rows(n_rows, reverse, step_fn, carry):
    groups = n_rows // SCAN_ROWS

    def trip(gidx, carry):
        gi = (groups - 1 - gidx) if reverse else gidx
        base = pl.multiple_of(gi * SCAN_ROWS, SCAN_ROWS)
        return step_fn(base, carry)

    return lax.fori_loop(0, groups, trip, carry)


def _scan_block_order(nb, nb_c, reverse, adjoint):
    if not reverse:
        return (lambda i: nb - 1 - i) if adjoint else (lambda i: i)
    if adjoint:
        return lambda i: jnp.where(i < nb - nb_c, nb_c + i, i - (nb - nb_c))
    return lambda i: jnp.where(i < nb_c, nb_c - 1 - i, nb + nb_c - 1 - i)


def _rglru_fwd(xs, wa, ba, wx, bx, lam, *, reverse, n_ctx, name):
    rows = xs.shape[0]
    tb = _tile(math.gcd(rows, n_ctx), RNN_TB, SCAN_ROWS)
    nb = rows // tb
    block_of = _scan_block_order(nb, n_ctx // tb, reverse, False)
    order = lambda i: (block_of(i), 0)

    def body(x_ref, wa_ref, ba_ref, wx_ref, bx_ref, lam_ref, h_ref, hp_ref, a_s, b_s, state):
        @pl.when(pl.program_id(0) == 0)
        def _():
            state[...] = jnp.zeros_like(state)

        xv = x_ref[...]
        _, gi, _, a, s = _rglru_gates(xv, wa_ref, ba_ref, wx_ref, bx_ref, lam_ref)
        a_s[...] = a
        b_s[...] = s * (gi * xv)

        def group(base, h):
            av = a_s[pl.ds(base, SCAN_ROWS), :]
            bv = b_s[pl.ds(base, SCAN_ROWS), :]
            outs, prevs = [None] * SCAN_ROWS, [None] * SCAN_ROWS
            for k in range(SCAN_ROWS):
                r_ = SCAN_ROWS - 1 - k if reverse else k
                prevs[r_] = h
                h = av[r_:r_ + 1, :] * h + bv[r_:r_ + 1, :]
                outs[r_] = h
            h_ref[pl.ds(base, SCAN_ROWS), :] = jnp.concatenate(outs, axis=0)
            hp_ref[pl.ds(base, SCAN_ROWS), :] = jnp.concatenate(prevs, axis=0)
            return h

        state[0:1, :] = _scan_rows(tb, reverse, group, state[0:1, :])

    blk = pl.BlockSpec((tb, D_RNN), order)
    wspec = _full((RNN_BLOCKS, RNN_BW, RNN_BW))
    vec = _full((1, D_RNN))
    return pl.pallas_call(
        body, name=name, grid=(nb,), in_specs=[blk, wspec, vec, wspec, vec, vec], out_specs=(blk, blk),
        out_shape=(jax.ShapeDtypeStruct((rows, D_RNN), F32), jax.ShapeDtypeStruct((rows, D_RNN), F32)),
        scratch_shapes=[pltpu.VMEM((tb, D_RNN), F32), pltpu.VMEM((tb, D_RNN), F32), pltpu.VMEM((SCAN_ROWS, D_RNN), F32)],
        compiler_params=_params(("arbitrary",)),
    )(xs, wa, ba, wx, bx, lam)


def _rglru_bwd(xs, h_prev, dh, wa, ba, wx, bx, lam, *, reverse, n_ctx, name):
    rows = xs.shape[0]
    tb = _tile(math.gcd(rows, n_ctx), RNN_TB, SCAN_ROWS)
    nb, nb_c = rows // tb, n_ctx // tb
    back = not reverse
    block_of = _scan_block_order(nb, nb_c, reverse, True)
    order = lambda i: (block_of(i), 0)

    def body(x_ref, hp_ref, dh_ref, wa_ref, ba_ref, wx_ref, bx_ref, lam_ref,
             dx_ref, dwa_ref, dba_ref, dwx_ref, dbx_ref, dlam_ref, a_s, g_s, state):
        @pl.when(pl.program_id(0) == 0)
        def _():
            state[...] = jnp.zeros_like(state)
            dwa_ref[...] = jnp.zeros_like(dwa_ref)
            dwx_ref[...] = jnp.zeros_like(dwx_ref)
            dba_ref[...] = jnp.zeros_like(dba_ref)
            dbx_ref[...] = jnp.zeros_like(dbx_ref)
            dlam_ref[...] = jnp.zeros_like(dlam_ref)

        xv = x_ref[...]
        r, gi, sp, a, s = _rglru_gates(xv, wa_ref, ba_ref, wx_ref, bx_ref, lam_ref)
        a_s[...] = a

        is_latent = block_of(pl.program_id(0)) >= nb_c

        def group(base, carry):
            av = a_s[pl.ds(base, SCAN_ROWS), :]
            dv = jnp.where(is_latent, dh_ref[pl.ds(base, SCAN_ROWS), :], 0.0)
            outs = [None] * SCAN_ROWS
            for k in range(SCAN_ROWS):
                r_ = SCAN_ROWS - 1 - k if back else k
                gt = dv[r_:r_ + 1, :] + carry
                outs[r_] = gt
                carry = av[r_:r_ + 1, :] * gt
            g_s[pl.ds(base, SCAN_ROWS), :] = jnp.concatenate(outs, axis=0)
            return carry

        state[0:1, :] = _scan_rows(tb, back, group, state[0:1, :])

        gv = g_s[...]
        d_a = gv * hp_ref[...]
        d_s = gv * (gi * xv)
        d_gi = gv * (s * xv)
        dx = gv * (s * gi)
        d_log_a = d_a * a - d_s * (a * a) / s
        d_r = d_log_a * (-RG_C * sp)
        lamv = lam_ref[...]
        d_sp = jnp.sum(d_log_a * (-RG_C * r), axis=0, keepdims=True)
        dlam_ref[...] += d_sp * (-_sigmoid(-lamv))
        d_zr = d_r * r * (1.0 - r)
        d_zi = d_gi * gi * (1.0 - gi)
        dba_ref[...] += jnp.sum(d_zr, axis=0, keepdims=True)
        dbx_ref[...] += jnp.sum(d_zi, axis=0, keepdims=True)
        xb = xv.astype(BF16)
        zrb, zib = d_zr.astype(BF16), d_zi.astype(BF16)
        parts = []
        for n in range(RNN_BLOCKS):
            cols = slice(n * RNN_BW, (n + 1) * RNN_BW)
            dwa_ref[n] += lax.dot_general(xb[:, cols], zrb[:, cols], (((0,), (0,)), ((), ())), preferred_element_type=F32)
            dwx_ref[n] += lax.dot_general(xb[:, cols], zib[:, cols], (((0,), (0,)), ((), ())), preferred_element_type=F32)
            parts.append(
                lax.dot_general(zrb[:, cols], wa_ref[n].astype(BF16), (((1,), (1,)), ((), ())), preferred_element_type=F32)
                + lax.dot_general(zib[:, cols], wx_ref[n].astype(BF16), (((1,), (1,)), ((), ())), preferred_element_type=F32))
        dx_ref[...] = dx + jnp.concatenate(parts, axis=-1)

    blk = pl.BlockSpec((tb, D_RNN), order)
    wspec = _full((RNN_BLOCKS, RNN_BW, RNN_BW))
    vec = _full((1, D_RNN))
    wshape = jax.ShapeDtypeStruct((RNN_BLOCKS, RNN_BW, RNN_BW), F32)
    vshape = jax.ShapeDtypeStruct((1, D_RNN), F32)
    dh_blk = pl.BlockSpec((tb, D_RNN), lambda i: (jnp.maximum(block_of(i) - nb_c, 0), 0))
    return pl.pallas_call(
        body, name=name, grid=(nb,), in_specs=[blk, blk, dh_blk, wspec, vec, wspec, vec, vec],
        out_specs=(blk, wspec, vec, wspec, vec, vec),
        out_shape=(jax.ShapeDtypeStruct((rows, D_RNN), F32), wshape, vshape, wshape, vshape, vshape),
        scratch_shapes=[pltpu.VMEM((tb, D_RNN), F32), pltpu.VMEM((tb, D_RNN), F32), pltpu.VMEM((SCAN_ROWS, D_RNN), F32)],
        compiler_params=_params(("arbitrary",)),
    )(xs, h_prev, dh, wa, ba, wx, bx, lam)


def _assemble_d_proj(dp_qk_l, dp_qk_c, dv_all, d_xr_l, d_xr_c, d_gate):
    n_lat, n_ctx = dp_qk_l.shape[0], dp_qk_c.shape[0]
    tr = _tile(math.gcd(n_lat, n_ctx), 256, 16)
    nb_l, nb_c = n_lat // tr, n_ctx // tr
    w_qk = ATTN_W + KV_W

    def body(ql_ref, qc_ref, dv_ref, xl_ref, xc_ref, g_ref, o_ref):
        i = pl.program_id(0)
        o_ref[:, w_qk:XR_COL0] = dv_ref[...].astype(o_ref.dtype)

        @pl.when(i < nb_l)
        def _():
            o_ref[:, :w_qk] = ql_ref[...]
            o_ref[:, XR_COL0:GATE_COL0] = xl_ref[...]
            o_ref[:, GATE_COL0:] = g_ref[...]

        @pl.when(i >= nb_l)
        def _():
            o_ref[:, :w_qk] = qc_ref[...]
            o_ref[:, XR_COL0:GATE_COL0] = xc_ref[...]
            o_ref[:, GATE_COL0:] = jnp.zeros((tr, D_RNN), o_ref.dtype)

    lat = lambda i: (jnp.minimum(i, nb_l - 1), 0)
    ctx = lambda i: (jnp.maximum(i - nb_l, 0), 0)
    return pl.pallas_call(
        body, name="assemble_d_proj", grid=(nb_l + nb_c,),
        in_specs=[pl.BlockSpec((tr, w_qk), lat), pl.BlockSpec((tr, w_qk), ctx),
                  pl.BlockSpec((tr, KV_W), lambda i: (jnp.where(i < nb_l, i + nb_c, i - nb_l), 0)),
                  pl.BlockSpec((tr, D_RNN), lat), pl.BlockSpec((tr, D_RNN), ctx), pl.BlockSpec((tr, D_RNN), lat)],
        out_specs=pl.BlockSpec((tr, GATE_COL0 + D_RNN), lambda i: (i, 0)),
        out_shape=jax.ShapeDtypeStruct((n_lat + n_ctx, GATE_COL0 + D_RNN), BF16),
        compiler_params=_params(("parallel",)),
    )(dp_qk_l, dp_qk_c, dv_all, d_xr_l, d_xr_c, d_gate)


def _gelu(z):
    return 0.5 * z * (1.0 + jnp.tanh(GELU_C * (z + 0.044715 * z * z * z)))


def _gelu_grad(z):
    t = jnp.tanh(GELU_C * (z + 0.044715 * z * z * z))
    return 0.5 * (1.0 + t) + 0.5 * z * (1.0 - t * t) * (GELU_C * (1.0 + 3.0 * 0.044715 * z * z))


GATE_COL0 = XR_COL0 + D_RNN


RNN_OUT_COLS = 512


def _rnn_out_specs(rows, hf_off, hb_off):
    tr = _tile(rows, 256, 16)
    assert hf_off % tr == 0 and hb_off % tr == 0 and GATE_COL0 % RNN_OUT_COLS == 0
    fo, bo, go = hf_off // tr, hb_off // tr, GATE_COL0 // RNN_OUT_COLS
    hf_spec = pl.BlockSpec((tr, RNN_OUT_COLS), lambda i, j: (i + fo, j))
    hb_spec = pl.BlockSpec((tr, RNN_OUT_COLS), lambda i, j: (i + bo, j))
    gate_spec = pl.BlockSpec((tr, RNN_OUT_COLS), lambda i, j: (i, j + go))
    out_spec = pl.BlockSpec((tr, RNN_OUT_COLS), lambda i, j: (i, j))
    return (rows // tr, D_RNN // RNN_OUT_COLS), hf_spec, hb_spec, gate_spec, out_spec


def _rnn_out_fwd(hf, hb, proj, hf_off, hb_off, cat):
    rows = proj.shape[0]
    grid, hf_spec, hb_spec, gate_spec, out_spec = _rnn_out_specs(rows, hf_off, hb_off)
    tr, col0 = out_spec.block_shape[0], ATTN_W // RNN_OUT_COLS

    def body(hf_ref, hb_ref, g_ref, _, o_ref):
        o_ref[...] = ((hf_ref[...] + hb_ref[...]) * _gelu(g_ref[...])).astype(o_ref.dtype)

    return pl.pallas_call(
        body, name="rnn_out_fwd", grid=grid, in_specs=[hf_spec, hb_spec, gate_spec, ANY],
        out_specs=pl.BlockSpec((tr, RNN_OUT_COLS), lambda i, j: (i, j + col0)),
        out_shape=jax.ShapeDtypeStruct(cat.shape, cat.dtype), input_output_aliases={3: 0},
        compiler_params=_params(("parallel", "parallel")),
    )(hf, hb, proj, cat)


def _rnn_out_bwd(d_cat, hf, hb, proj, hf_off, hb_off):
    rows = proj.shape[0]
    grid, hf_spec, hb_spec, gate_spec, out_spec = _rnn_out_specs(rows, hf_off, hb_off)
    do = ATTN_W // RNN_OUT_COLS

    def body(d_ref, hf_ref, hb_ref, g_ref, dh_ref, dg_ref):
        dv, gv = d_ref[...].astype(F32), g_ref[...]
        dh_ref[...] = dv * _gelu(gv)
        dg_ref[...] = (dv * (hf_ref[...] + hb_ref[...]) * _gelu_grad(gv)).astype(dg_ref.dtype)

    tr = out_spec.block_shape[0]
    return pl.pallas_call(
        body, name="rnn_out_bwd", grid=grid,
        in_specs=[pl.BlockSpec((tr, RNN_OUT_COLS), lambda i, j: (i, j + do)), hf_spec, hb_spec, gate_spec],
        out_specs=(out_spec, out_spec),
        out_shape=(jax.ShapeDtypeStruct((rows, D_RNN), F32), jax.ShapeDtypeStruct((rows, D_RNN), BF16)),
        compiler_params=_params(("parallel", "parallel")),
    )(d_cat, hf, hb, proj)


def _gmlp_parts(z_ref, vg_ref, vb_ref, d_gm):
    zu, zv = z_ref[:, :d_gm], z_ref[:, d_gm:]
    u = _gelu(zu)
    v = _gelu(zv)
    mu = jnp.mean(v, axis=-1, keepdims=True)
    vc = v - mu
    rstd = lax.rsqrt(jnp.mean(vc * vc, axis=-1, keepdims=True) + EPS)
    vhat = vc * rstd
    vn = vhat * vg_ref[...] + vb_ref[...]
    return zu, zv, u, vhat, rstd, vn


def _gmlp_fwd(z, v_g, v_b, w_sp, b_sp_t):
    rows, d_gm = z.shape[0], z.shape[1] // 2
    tr = _tile(rows, 256, CHUNK)
    gwid = d_gm // GM_GROUPS

    def body(z_ref, vg_ref, vb_ref, w_ref, b_ref, o_ref):
        _, _, u, _, _, vn = _gmlp_parts(z_ref, vg_ref, vb_ref, d_gm)
        vnb = vn.astype(BF16)
        for g in range(GM_GROUPS):
            wg = w_ref[g].astype(BF16)
            for c in range(tr // CHUNK):
                rs, cs = slice(c * CHUNK, (c + 1) * CHUNK), slice(g * gwid, (g + 1) * gwid)
                sv = jnp.dot(wg, vnb[rs, cs], preferred_element_type=F32) + b_ref[:, g:g + 1]
                o_ref[rs, cs] = (u[rs, cs] * sv).astype(o_ref.dtype)

    return pl.pallas_call(
        body, name="gmlp_fwd", grid=(rows // tr,),
        in_specs=[pl.BlockSpec((tr, 2 * d_gm), lambda i: (i, 0)), _full((1, d_gm)), _full((1, d_gm)),
                  _full(w_sp.shape), _full(b_sp_t.shape)],
        out_specs=pl.BlockSpec((tr, d_gm), lambda i: (i, 0)),
        out_shape=jax.ShapeDtypeStruct((rows, d_gm), BF16), compiler_params=_params(("parallel",)),
    )(z, v_g, v_b, w_sp, b_sp_t)


def _gmlp_bwd(z, dgate, v_g, v_b, w_sp, b_sp_t):
    rows, d_gm = z.shape[0], z.shape[1] // 2
    tr = _tile(rows, 256, CHUNK)
    gwid = d_gm // GM_GROUPS

    def body(z_ref, dg_ref, vg_ref, vb_ref, w_ref, b_ref, dz_ref, dbin_ref, dvg_ref, dvb_ref, dw_ref, dbs_ref, dvn_s):
        @pl.when(pl.program_id(0) == 0)
        def _():
            dbin_ref[...] = jnp.zeros_like(dbin_ref)
            dvg_ref[...] = jnp.zeros_like(dvg_ref)
            dvb_ref[...] = jnp.zeros_like(dvb_ref)
            dw_ref[...] = jnp.zeros_like(dw_ref)
            dbs_ref[...] = jnp.zeros_like(dbs_ref)

        zu, zv, u, vhat, rstd, vn = _gmlp_parts(z_ref, vg_ref, vb_ref, d_gm)
        vnb = vn.astype(BF16)
        dgv = dg_ref[...].astype(F32)
        dsv = dgv * u
        dsvb = dsv.astype(BF16)
        for g in range(GM_GROUPS):
            wg = w_ref[g].astype(BF16)
            cs = slice(g * gwid, (g + 1) * gwid)
            for c in range(tr // CHUNK):
                rs = slice(c * CHUNK, (c + 1) * CHUNK)
                sv = jnp.dot(wg, vnb[rs, cs], preferred_element_type=F32) + b_ref[:, g:g + 1]
                dz_ref[rs, cs] = (dgv[rs, cs] * sv * _gelu_grad(zu[rs, cs])).astype(dz_ref.dtype)
                dw_ref[g] += lax.dot_general(dsvb[rs, cs], vnb[rs, cs], (((1,), (1,)), ((), ())),
                                             preferred_element_type=F32)
                dbs_ref[:, g:g + 1] += jnp.sum(dsv[rs, cs], axis=-1, keepdims=True)
                dvn_s[rs, cs] = lax.dot_general(wg, dsvb[rs, cs], (((0,), (0,)), ((), ())), preferred_element_type=F32)
        dvn = dvn_s[...]
        dvg_ref[...] += jnp.sum(dvn * vhat, axis=0, keepdims=True)
        dvb_ref[...] += jnp.sum(dvn, axis=0, keepdims=True)
        dvh = dvn * vg_ref[...]
        dv = rstd * (dvh - jnp.mean(dvh, axis=-1, keepdims=True) - vhat * jnp.mean(dvh * vhat, axis=-1, keepdims=True))
        dzv = dv * _gelu_grad(zv)
        dz_ref[:, d_gm:] = dzv.astype(dz_ref.dtype)
        dbin_ref[:, d_gm:] += jnp.sum(dzv, axis=0, keepdims=True)
        dbin_ref[:, :d_gm] += jnp.sum(dz_ref[:, :d_gm].astype(F32), axis=0, keepdims=True)

    return pl.pallas_call(
        body, name="gmlp_bwd", grid=(rows // tr,),
        in_specs=[pl.BlockSpec((tr, 2 * d_gm), lambda i: (i, 0)), pl.BlockSpec((tr, d_gm), lambda i: (i, 0)),
                  _full((1, d_gm)), _full((1, d_gm)), _full(w_sp.shape), _full(b_sp_t.shape)],
        out_specs=(pl.BlockSpec((tr, 2 * d_gm), lambda i: (i, 0)), _full((1, 2 * d_gm)), _full((1, d_gm)),
                   _full((1, d_gm)), _full(w_sp.shape), _full(b_sp_t.shape)),
        out_shape=(jax.ShapeDtypeStruct((rows, 2 * d_gm), BF16), jax.ShapeDtypeStruct((1, 2 * d_gm), F32),
                   jax.ShapeDtypeStruct((1, d_gm), F32), jax.ShapeDtypeStruct((1, d_gm), F32),
                   jax.ShapeDtypeStruct(w_sp.shape, F32), jax.ShapeDtypeStruct(b_sp_t.shape, F32)),
        scratch_shapes=[pltpu.VMEM((tr, d_gm), F32)],
        compiler_params=_params(("arbitrary",)),
    )(z, dgate, v_g, v_b, w_sp, b_sp_t)


def _adamw_math(w, g, m, v):
    m = ADAM_B1 * m + (1.0 - ADAM_B1) * g
    v = ADAM_B2 * v + (1.0 - ADAM_B2) * (g * g)
    m_hat = m / (1.0 - ADAM_B1 ** ADAM_STEP)
    v_hat = v / (1.0 - ADAM_B2 ** ADAM_STEP)
    delta = -ADAM_LR * (m_hat / (jnp.sqrt(v_hat) + ADAM_EPS) + ADAM_WD * w)
    return delta, m, v


def _adamw(w, g, m, v, name, rewrite_grad=False):
    shape = w.shape
    if rewrite_grad:
        outs = _rowwise(lambda w_, g_, m_, v_: (g_,) + _adamw_math(w_, g_, m_, v_), (F32,) * 4, _as2d(w), _as2d(g), _as2d(m),
                        _as2d(v), name=name)
        return tuple(o.reshape(shape) for o in outs)
    outs = _rowwise(_adamw_math, (F32, F32, F32), _as2d(w), _as2d(g), _as2d(m), _as2d(v), name=name)
    return (g.reshape(shape),) + tuple(o.reshape(shape) for o in outs)


PACK_COLS = 1024


def _pack(arrays, dtype=F32):
    flat = jnp.concatenate([a.reshape(-1).astype(dtype) for a in arrays])
    pad = (-flat.size) % (16 * PACK_COLS)
    return jnp.pad(flat, (0, pad)).reshape(-1, PACK_COLS)


def _into_slot(pack, dev, name):
    rows, cols = pack.shape
    tr = _rows_tile(rows, cols, budget=512 * 1024)

    def body(dev_ref, p_ref, o_ref):
        o_ref[...] = p_ref[...]

    return pl.pallas_call(
        body, name=name, out_shape=jax.ShapeDtypeStruct((N_DEV, rows, cols), pack.dtype),
        grid_spec=pltpu.PrefetchScalarGridSpec(
            num_scalar_prefetch=1, grid=(rows // tr,), in_specs=[pl.BlockSpec((tr, cols), lambda i, dv: (i, 0))],
            out_specs=pl.BlockSpec((None, tr, cols), lambda i, dv: (dv[0], i, 0))),
        compiler_params=_params(("parallel",)),
    )(dev, pack)


def _unpack(flat, shapes):
    out, pos = [], 0
    for shp in shapes:
        n = math.prod(shp)
        out.append(flat[pos:pos + n].reshape(shp))
        pos += n
    return out


def _unpack_devices(packed8, shapes):
    flat8 = packed8.reshape(N_DEV, -1)
    out, pos = [], 0
    for shp in shapes:
        n = math.prod(shp)
        out.append(flat8[:, pos:pos + n].reshape((N_DEV,) + tuple(shp)))
        pos += n
    return out


def _sum_devices(g8):
    _, rows, cols = g8.shape
    tr = _rows_tile(rows, cols, budget=256 * 1024)

    def body(g_ref, o_ref):
        acc = g_ref[0].astype(F32)
        for d in range(1, N_DEV):
            acc = acc + g_ref[d].astype(F32)
        o_ref[...] = acc

    return pl.pallas_call(
        body, name="sum_devices", grid=(rows // tr,), in_specs=[pl.BlockSpec((N_DEV, tr, cols), lambda i: (0, i, 0))],
        out_specs=pl.BlockSpec((tr, cols), lambda i: (i, 0)), out_shape=jax.ShapeDtypeStruct((rows, cols), F32),
        compiler_params=_params(("parallel",)),
    )(g8)


def _place():
    return lax.axis_index("x"), lax.axis_index("y"), lax.axis_index("c")


def _other_chips(x, y):
    return [(1 - x, y), (x, 1 - y), (1 - x, 1 - y)]


def _remote(src, dst, send_sem, recv_sem, to):
    return pltpu.make_async_remote_copy(src_ref=src, dst_ref=dst, send_sem=send_sem, recv_sem=recv_sem, device_id=to,
                                        device_id_type=MESH)


def _comm_call(body, name, operands, out_shapes, n_remote, n_local, aliases=None):
    return pl.pallas_call(
        body, name=name, out_shape=tuple(out_shapes), in_specs=[ANY] * len(operands), out_specs=tuple(ANY for _ in out_shapes),
        scratch_shapes=[pltpu.SemaphoreType.DMA((n_remote,)), pltpu.SemaphoreType.DMA((n_remote,)),
                        pltpu.SemaphoreType.DMA((max(n_local, 1),))],
        input_output_aliases=aliases or {},
    )(*operands)


def _in_place(arrays):
    return [jax.ShapeDtypeStruct(a.shape, a.dtype) for a in arrays], {i: i for i in range(len(arrays))}


def _allgather8(arrs, name):
    n = len(arrs)

    def body(*refs):
        ins, outs = refs[:n], refs[n:2 * n]
        send, recv, lsem = refs[2 * n:]
        x, y, c = _place()
        me, sib = (x, y, c), (x, y, 1 - c)
        chips = _other_chips(x, y)

        def slot(t, px, py, pc):
            return outs[t].at[4 * px + 2 * py + pc]

        def cp(t, k, block, to, from_input=False):
            src = ins[t] if from_input else slot(t, *block)
            return _remote(src, slot(t, *block), send.at[7 * t + k], recv.at[7 * t + k], to)

        mine = [pltpu.make_async_copy(ins[t], slot(t, *me), lsem.at[t]) for t in range(n)]
        for cpy in mine:
            cpy.start()
        first = []
        for t in range(n):
            first.append(cp(t, 0, me, sib, True))
            first += [cp(t, 1 + j, me, (*chip, c), True) for j, chip in enumerate(chips)]
        for cpy in first:
            cpy.start()
        passed = []
        for t in range(n):
            for j, chip in enumerate(chips):
                cp(t, 1 + j, (*chip, c), me).wait_recv()
                fwd = cp(t, 4 + j, (*chip, c), sib)
                fwd.start()
                passed.append(fwd)
        for t in range(n):
            cp(t, 0, sib, me).wait_recv()
            for j, chip in enumerate(chips):
                cp(t, 4 + j, (*chip, 1 - c), me).wait_recv()
        for cpy in first + passed:
            cpy.wait_send()
        for cpy in mine:
            cpy.wait()

    outs = _comm_call(body, name, arrs, [jax.ShapeDtypeStruct((N_DEV,) + a.shape, a.dtype) for a in arrs], 7 * n, n)
    return list(outs)


def _join_halves(bufs):
    n = len(bufs)
    units = [(k, layer) for k in range(n) for layer in range(bufs[k].shape[0])]

    def body(*refs):
        bufs_ = refs[n:2 * n]
        send, recv, _ = refs[2 * n:]
        x, y, c = _place()
        sent = []
        for u, (k, layer) in enumerate(units):
            half = bufs_[k].shape[1] // 2
            mine = bufs_[k].at[layer, pl.ds(c * half, half)]
            cpy = _remote(mine, mine, send.at[u], recv.at[u], (x, y, 1 - c))
            cpy.start()
            sent.append(cpy)
        for u, (k, layer) in enumerate(units):
            half = bufs_[k].shape[1] // 2
            theirs = bufs_[k].at[layer, pl.ds((1 - c) * half, half)]
            _remote(theirs, theirs, send.at[u], recv.at[u], (x, y, c)).wait_recv()
        for cpy in sent:
            cpy.wait_send()

    shapes, aliases = _in_place(bufs)
    return list(_comm_call(body, "join_halves", bufs, shapes, len(units), 0, aliases))


def _add_halves(grad, other, place):
    _, rows, cols = grad.shape
    half = rows // 2
    tr = _rows_tile(half, cols, itemsize=2, budget=2 * 1024 * 1024)
    per_half = half // tr

    def body(place_ref, g_ref, o_ref, s_ref):
        s_ref[...] = (g_ref[...].astype(F32) + o_ref[...].astype(F32)).astype(s_ref.dtype)

    return pl.pallas_call(
        body, name="add_halves", out_shape=jax.ShapeDtypeStruct((N_CHIPS, half, cols), grad.dtype),
        grid_spec=pltpu.PrefetchScalarGridSpec(
            num_scalar_prefetch=1, grid=(N_CHIPS, per_half),
            in_specs=[pl.BlockSpec((None, tr, cols), lambda k, i, pr: (k, pr[1] * per_half + i, 0)),
                      pl.BlockSpec((None, tr, cols), lambda k, i, pr: (k, i, 0))],
            out_specs=pl.BlockSpec((None, tr, cols), lambda k, i, pr: (k, i, 0))),
        compiler_params=_params(("parallel", "parallel")),
    )(place, grad, other)


def _add_chips(sums, others, place, dest, layer, n_layers):
    _, half, cols = sums.shape
    tr = _rows_tile(half, cols, itemsize=4, budget=2 * 1024 * 1024)
    per_half = half // tr

    def body(place_ref, s_ref, o_ref, *rest):
        acc = s_ref[...].astype(F32)
        for j in range(N_CHIPS - 1):
            acc = acc + o_ref[j].astype(F32)
        rest[-1][...] = acc

    operands = [place, sums, others] + ([] if dest is None else [dest])
    return pl.pallas_call(
        body, name="add_chips", out_shape=jax.ShapeDtypeStruct((n_layers, 2 * half, cols), F32),
        grid_spec=pltpu.PrefetchScalarGridSpec(
            num_scalar_prefetch=1, grid=(per_half,),
            in_specs=[pl.BlockSpec((None, tr, cols), lambda i, pr: (pr[0], i, 0)),
                      pl.BlockSpec((N_CHIPS - 1, tr, cols), lambda i, pr: (0, i, 0))] + ([] if dest is None else [ANY]),
            out_specs=pl.BlockSpec((None, tr, cols), lambda i, pr: (layer, pr[1] * per_half + i, 0))),
        input_output_aliases={} if dest is None else {3: 0},
        compiler_params=_params(("parallel",)),
    )(*operands)


HBM = pl.BlockSpec(memory_space=pltpu.HBM)
SEM = pl.BlockSpec(memory_space=pltpu.SEMAPHORE)
DATAFLOW = pltpu.SideEffectType.DATAFLOW_SIDE_EFFECTING


def _split_start(name, bufs, copies, n_copies, after=None):
    n = len(bufs)
    extra = 0 if after is None else 1

    def body(*refs):
        for cpy in copies(refs[:n], refs[n + extra], refs[n + extra + 1]):
            cpy.start()
        refs[-1][...] = jnp.zeros_like(refs[-1])

    outs = pl.pallas_call(
        body, name=name,
        out_shape=(pltpu.SemaphoreType.DMA((n_copies,)), pltpu.SemaphoreType.DMA((n_copies,)),
                   *[pltpu.HBM(b.shape, b.dtype) for b in bufs], jax.ShapeDtypeStruct((8, LANES), F32)),
        in_specs=[HBM] * n + [ANY] * extra,
        out_specs=(SEM, SEM, *[HBM] * n, pl.BlockSpec(memory_space=pltpu.VMEM)),
        input_output_aliases={i: 2 + i for i in range(n)},
        compiler_params=pltpu.CompilerParams(has_side_effects=DATAFLOW),
    )(*[pltpu.with_memory_space_constraint(b, pltpu.HBM) for b in bufs], *([] if after is None else [after]))
    return outs[0], outs[1], list(outs[2:2 + n]), outs[-1]


def _split_wait(name, bufs, send, recv, copies, after):
    n = len(bufs)

    def body(*refs):
        for cpy in copies(refs[:n], refs[n], refs[n + 1]):
            cpy.wait_send()
            cpy.wait_recv()

    return list(pl.pallas_call(
        body, name=name, out_shape=tuple(pltpu.HBM(b.shape, b.dtype) for b in bufs),
        in_specs=[HBM] * n + [SEM, SEM, ANY], out_specs=tuple([HBM] * n),
        input_output_aliases={i: i for i in range(n)},
        compiler_params=pltpu.CompilerParams(has_side_effects=DATAFLOW),
    )(*bufs, send, recv, after))


def _gather_copies(bufs, send, recv):
    x, y, c = _place()
    out = []
    for u, buf in enumerate(bufs):
        half = buf.shape[1] // 2
        mine = buf.at[2 * x + y, pl.ds(c * half, half)]
        out += [_remote(mine, mine, send.at[3 * u + j], recv.at[3 * u + j], (*chip, c))
                for j, chip in enumerate(_other_chips(x, y))]
    return out


def _exchange_copies(bufs, send, recv):
    x, y, c = _place()
    n = len(bufs) // 2
    out = []
    for k in range(n):
        half = bufs[k].shape[1] // 2
        theirs = bufs[k].at[pl.ds(0, N_CHIPS), pl.ds((1 - c) * half, half)]
        out.append(_remote(theirs, bufs[n + k], send.at[k], recv.at[k], (x, y, 1 - c)))
    return out


def _all_to_all_copies(bufs, send, recv):
    x, y, c = _place()
    n = len(bufs) // 2
    return [_remote(bufs[k].at[2 * chip[0] + chip[1]], bufs[n + k].at[j], send.at[3 * k + j], recv.at[3 * k + j], (*chip, c))
            for k in range(n) for j, chip in enumerate(_other_chips(x, y))]


def _forward_copies(bufs, send, recv):
    x, y, c = _place()
    out = []
    for u, buf in enumerate(bufs):
        half = buf.shape[1] // 2
        for j, chip in enumerate(_other_chips(x, y)):
            landed = buf.at[2 * chip[0] + chip[1], pl.ds(c * half, half)]
            out.append(_remote(landed, landed, send.at[3 * u + j], recv.at[3 * u + j], (x, y, 1 - c)))
    return out


def _gather8_copies(bufs, send, recv):
    x, y, c = _place()
    targets = [(x, y, 1 - c)] + [(*chip, c) for chip in _other_chips(x, y)]
    out = []
    for b, buf in enumerate(bufs):
        mine = buf.at[4 * x + 2 * y + c]
        out += [_remote(mine, mine, send.at[N_CHIPS * b + k], recv.at[N_CHIPS * b + k], to) for k, to in enumerate(targets)]
    return out


def _forward_slots(bufs, name):
    n = len(bufs)

    def body(*refs):
        bufs_ = refs[n:2 * n]
        send, recv, _ = refs[2 * n:]
        x, y, c = _place()
        chips = _other_chips(x, y)
        sent = []
        for b in range(n):
            for j, chip in enumerate(chips):
                slot = bufs_[b].at[4 * chip[0] + 2 * chip[1] + c]
                cpy = _remote(slot, slot, send.at[3 * b + j], recv.at[3 * b + j], (x, y, 1 - c))
                cpy.start()
                sent.append(cpy)
        for b in range(n):
            for j, chip in enumerate(chips):
                slot = bufs_[b].at[4 * chip[0] + 2 * chip[1] + 1 - c]
                _remote(slot, slot, send.at[3 * b + j], recv.at[3 * b + j], (x, y, c)).wait_recv()
        for cpy in sent:
            cpy.wait_send()

    shapes, aliases = _in_place(bufs)
    return list(_comm_call(body, name, bufs, shapes, (N_CHIPS - 1) * n, 0, aliases))


FWD_GROUPS = {'mix': ('ar_out', 'ff_in0', 'ff_out0'), 'l1': ('gm_in', 'gm_out', 'ff_in1', 'ff_out1')}
GRAD_LAYOUT = {'ff_in0': (0, 0), 'ff_in1': (0, 1), 'ff_out0': (1, 0), 'ff_out1': (1, 1), 'ar_in': (2, 0), 'ar_out': (3, 0),
               'gm_in': (4, 0), 'gm_out': (5, 0)}


class _MeshLink:
    def __init__(self, place, shards):
        self.place = place
        self.ready = {'ar_in': shards['ar_in']}
        self.pending, after = {}, shards['ar_in']
        for group, names in FWD_GROUPS.items():
            send, recv, bufs, token = _split_start(f"gather_{group}_start", [shards[n] for n in names], _gather_copies,
                                                   3 * len(names), after)
            self.pending[group] = (names, send, recv, bufs)
            after = token
        self.start_token = after[0, 0]
        self.forwarding, self.exchanging, self.sent, self.last_token = {}, {}, {}, None

    def prefetch(self, group, after):
        names, send, recv, bufs = self.pending.pop(group)
        bufs = _split_wait(f"gather_{group}_wait", bufs, send, recv, _gather_copies, after)
        send, recv, bufs, token = _split_start(f"forward_{group}_start", bufs, _forward_copies, 3 * len(names))
        self.forwarding[group] = (names, send, recv, bufs)
        return token[0, 0]

    def weights(self, group, after):
        if group in self.forwarding:
            names, send, recv, bufs = self.forwarding.pop(group)
            self.ready.update(zip(names, _split_wait(f"forward_{group}_wait", bufs, send, recv, _forward_copies, after)))
        return self.ready

    def gradients(self, group, grads, after=None):
        tok = self.poll(next(iter(grads.values())))
        names, mine = list(grads), list(grads.values())
        landing = [lax.empty((N_CHIPS, g.shape[1] // 2, g.shape[2]), g.dtype) for g in mine]
        send, recv, bufs, token = _split_start(f"exchange_{group}_start", mine + landing, _exchange_copies, len(names), after)
        self.exchanging[group] = (names, send, recv, bufs)
        self.last_token = token
        return token[0, 0] + tok

    def poll(self, after):
        tok = 0.0
        for group in list(self.exchanging):
            names, send, recv, bufs = self.exchanging.pop(group)
            bufs = _split_wait(f"exchange_{group}_wait", bufs, send, recv, _exchange_copies, after)
            sums = [_add_halves(g, r, self.place) for g, r in zip(bufs[:len(names)], bufs[len(names):])]
            landing = [lax.empty((N_CHIPS - 1,) + s.shape[1:], s.dtype) for s in sums]
            send, recv, bufs, token = _split_start(f"grads_{group}_start", sums + landing, _all_to_all_copies, 3 * len(names))
            self.sent[group] = (names, send, recv, bufs)
            self.last_token = token
            tok = tok + token[0, 0]
        return tok

    def reduce(self, groups, after):
        units = {}
        for group in groups:
            names, send, recv, bufs = self.sent.pop(group)
            bufs = _split_wait(f"grads_{group}_wait", bufs, send, recv, _all_to_all_copies, after)
            units.update(zip(names, zip(bufs[:len(names)], bufs[len(names):])))
        n_layers = {p: 1 + max(l for pp, l in GRAD_LAYOUT.values() if pp == p) for p, _ in GRAD_LAYOUT.values()}
        out = {}
        for name, (p, layer) in GRAD_LAYOUT.items():
            if name in units:
                out[p] = _add_chips(*units[name], self.place, out.get(p), layer, n_layers[p])
        params = sorted(out)
        return dict(zip(params, _join_halves([out[p] for p in params])))


def _rope_tables(n):
    n_rows = n // GRID_W
    freqs = ROPE_THETA ** (-jnp.arange(ROPE_PAIRS, dtype=F32) / ROPE_PAIRS)
    ang_r = jnp.arange(n_rows, dtype=F32)[:, None] * freqs
    ang_c = jnp.arange(GRID_W, dtype=F32)[:, None] * freqs

    def per_token(of_row, of_col):
        r = jnp.broadcast_to(of_row[:, None, :], (n_rows, GRID_W, ROPE_PAIRS)).reshape(n, ROPE_PAIRS)
        c = jnp.broadcast_to(of_col[None, :, :], (n_rows, GRID_W, ROPE_PAIRS)).reshape(n, ROPE_PAIRS)
        return r, c

    cos_r, cos_c = per_token(jnp.cos(ang_r), jnp.cos(ang_c))
    sin_r, sin_c = per_token(jnp.sin(ang_r), jnp.sin(ang_c))
    cos = jnp.concatenate([cos_r, cos_r, cos_c, cos_c], axis=-1)
    sin = jnp.concatenate([-sin_r, sin_r, -sin_c, sin_c], axis=-1)
    return cos, sin


def _ffn_fwd(h2, w1, w2, tag):
    r, a = _matmul(h2, w1, kind='nn', b_split='n', out_dtype=BF16, epilogue='relu2', name=f"ffn_in_{tag}")
    f = _matmul(a, w2, kind='nn', b_split='k', out_dtype=F32, name=f"ffn_out_{tag}")
    return r, a, f


def _ffn_bwd(d_f, h2, r, a, w1, w2, tag):
    d_u = _matmul(d_f, w2, kind='nt', b_split='k', out_dtype=BF16, epilogue='times2x', extra=r, name=f"ffn_out_dx_{tag}")
    d_w2 = _matmul(a, d_f, kind='tn', out_split='k', out_dtype=BF16, name=f"ffn_out_dw_{tag}")
    d_w1 = _matmul(h2, d_u, kind='tn', out_split='n', out_dtype=BF16, name=f"ffn_in_dw_{tag}")
    d_h2 = _matmul(d_u, w1, kind='nt', b_split='n', out_dtype=F32, name=f"ffn_in_dx_{tag}")
    return d_h2, d_w1, d_w2


class _LocalLink:
    def __init__(self, big):
        self.big, self.grads, self.start_token = big, {}, 0.0

    def prefetch(self, group, after):
        return 0.0

    def poll(self, after):
        return 0.0

    def weights(self, group, after):
        return self.big

    def gradients(self, group, grads):
        self.grads.update(grads)
        return 0.0


def _local_step(xl0, xc0, target, ml, mc0, sp, link):
    n_lat, n_ctx = xl0.shape[0], xc0.shape[0]
    one = lambda v: 1.0 + v
    g = [[sp['norm_g'][i, k][None, :] for k in range(4)] for i in range(2)]

    sh1, sc1, gt1, sh2, sc2, gt2 = ml[0]
    big = link.weights('ar', None)
    sh1 = sh1 + link.start_token
    n_all = n_lat + n_ctx
    h_all = _norm_fwd(xl0, g[0][0], one(sc1), b=sh1, out_dtype=BF16, name="l0_mod1", into=(0, n_all, None))
    h_all = _norm_fwd(xc0, g[0][0], one(mc0[1]), b=mc0[0], out_dtype=BF16, name="l0_mod1_ctx", into=(n_lat, n_all, h_all))
    proj_l = _matmul(h_all, big['ar_in'], kind='nn', b_split='n', out_dtype=F32, a_rows=(0, n_lat), name="ar_in_lat")
    proj_c = _matmul(h_all, big['ar_in'], kind='nn', b_split='n', out_dtype=F32, a_rows=(n_lat, n_ctx), name="ar_in_ctx")
    cos_l, sin_l = _rope_tables(n_lat)
    cos_c, sin_c = jnp.ones((n_ctx, HEAD_DIM), F32), jnp.zeros((n_ctx, HEAD_DIM), F32)
    q_g, k_g = sp['q_g'], sp['k_g']
    _, k_all, v_all = _qk_fwd(proj_c, q_g, k_g, cos_c, sin_c, name="qk_fwd_ctx", kv_into=(0, n_all, None))
    q_l, k_all, v_all = _qk_fwd(proj_l, q_g, k_g, cos_l, sin_l, name="qk_fwd_lat", kv_into=(n_ctx, n_all, (k_all, v_all)))
    cat, lse = _attn_fwd(q_l, k_all, v_all)
    conv_b = sp['conv_b'] + link.prefetch('mix', cat)
    xs = _conv_fwd(proj_l, proj_c, sp['conv_w'], conv_b)
    rnn_w = [(sp['wa'][d], sp['ba'][d][None, :], sp['wx'][d], sp['bx'][d][None, :], sp['lam'][d][None, :]) for d in range(2)]
    h_f, hp_f = _rglru_fwd(xs, *rnn_w[0], reverse=False, n_ctx=n_ctx, name="rglru_fwd_f")
    h_r, hp_r = _rglru_fwd(xs, *rnn_w[1], reverse=True, n_ctx=n_ctx, name="rglru_fwd_r")
    cat = _rnn_out_fwd(h_f, h_r, proj_l, n_ctx, n_ctx, cat)
    w_mix = link.weights('mix', cat)
    ol0 = _matmul(cat, w_mix['ar_out'], kind='nn', b_split='k', out_dtype=F32, name="ar_out")
    xm0, h2_0 = _res_mod_fwd(ol0, xl0, g[0][1], gt1, g[0][2], one(sc2), sh2, name="l0_res1_mod2")
    r0, a0, f0 = _ffn_fwd(h2_0, w_mix['ff_in0'], w_mix['ff_out0'], "l0")
    th1, tc1, tg1, th2, tc2, tg2 = ml[1]
    xl1, hl1 = _res_mod_fwd(f0, xm0, g[0][3], gt2 + link.prefetch('l1', f0), g[1][0], one(tc1), th1, name="l0_res2_l1_mod1")
    w_l1 = link.weights('l1', xl1)
    z = _matmul(hl1, w_l1['gm_in'], kind='nn', b_split='n', bias=sp['gm_b_in'], out_dtype=F32, name="gm_in")
    b_sp_t = sp['gm_b_sp'].T
    gated = _gmlp_fwd(z, sp['gm_v_g'], sp['gm_v_b'], sp['gm_w_sp'], b_sp_t)
    ol1 = _matmul(gated, w_l1['gm_out'], kind='nn', b_split='k', out_dtype=F32, name="gm_out")
    xm1, h2_1 = _res_mod_fwd(ol1, xl1, g[1][1], tg1, g[1][2], one(tc2), th2, name="l1_res1_mod2")
    r1, a1, f1 = _ffn_fwd(h2_1, w_l1['ff_in1'], w_l1['ff_out1'], "l1")
    y = _norm_fwd(f1, g[1][3], tg2, res=xm1, out_dtype=F32, name="l1_res2")

    dy, loss = _loss_head(y, target)

    d_f1, dg13, d_tg2, _ = _norm_bwd(dy, f1, g[1][3], tg2, out_dtype=BF16, name="l1_res2_bwd")
    d_h2, dw_ff_in1, dw_ff_out1 = _ffn_bwd(d_f1, h2_1, r1, a1, w_l1['ff_in1'], w_l1['ff_out1'], "l1")
    tok = link.gradients('ffn1', {'ff_in1': dw_ff_in1, 'ff_out1': dw_ff_out1})
    dxm1, d_ol1, dg12, d_tc2, d_th2, dg11, d_tg1 = _mod_res_bwd(d_h2, xm1, g[1][2], one(tc2) + tok, dy, ol1, g[1][1], tg1,
                                                                name="l1_mod2_res1_bwd")
    d_gated = _matmul(d_ol1, w_l1['gm_out'], kind='nt', b_split='k', out_dtype=F32, name="gm_out_dx")
    dw_gm_out = _matmul(gated, d_ol1, kind='tn', out_split='k', out_dtype=BF16, name="gm_out_dw")
    d_z, d_gm_b_in, d_vg, d_vb, d_wsp, d_bsp_t = _gmlp_bwd(z, d_gated, sp['gm_v_g'], sp['gm_v_b'], sp['gm_w_sp'], b_sp_t)
    dw_gm_in = _matmul(hl1, d_z, kind='tn', out_split='n', out_dtype=BF16, name="gm_in_dw")
    d_hl1 = _matmul(d_z, w_l1['gm_in'], kind='nt', b_split='n', out_dtype=F32, name="gm_in_dx")
    tok = link.gradients('gm', {'gm_in': dw_gm_in, 'gm_out': dw_gm_out})

    dxl1, d_f0, dg10, d_tc1, d_th1, dg03, d_gt2 = _mod_res_bwd(d_hl1, xl1, g[1][0], one(tc1) + tok, dxm1, f0, g[0][3], gt2,
                                                               name="l1_mod1_l0_res2_bwd")
    d_h2, dw_ff_in0, dw_ff_out0 = _ffn_bwd(d_f0, h2_0, r0, a0, w_mix['ff_in0'], w_mix['ff_out0'], "l0")
    tok = link.gradients('ffn0', {'ff_in0': dw_ff_in0, 'ff_out0': dw_ff_out0})
    dxm0, d_ol0, dg02, d_sc2, d_sh2, dg01, d_gt1 = _mod_res_bwd(d_h2, xm0, g[0][2], one(sc2) + tok, dxl1, ol0, g[0][1], gt1,
                                                                name="l0_mod2_res1_bwd")
    d_cat = _matmul(d_ol0, w_mix['ar_out'], kind='nt', b_split='k', out_dtype=F32, name="ar_out_dx")
    dw_ar_out = _matmul(cat, d_ol0, kind='tn', out_split='k', out_dtype=BF16, name="ar_out_dw")
    dq, dk_all, dv_all = _attn_bwd(q_l, k_all, v_all, cat, lse, d_cat)
    tok = link.poll(dq)
    d_h, d_gate = _rnn_out_bwd(d_cat, h_f, h_r, proj_l, n_ctx, n_ctx)
    rnn_wb = [(wa_, ba_ + tok, wx_, bx_, lam_) for wa_, ba_, wx_, bx_, lam_ in rnn_w]
    dxs_f, d_wa0, d_ba0, d_wx0, d_bx0, d_lam0 = _rglru_bwd(
        xs, hp_f, d_h, *rnn_wb[0], reverse=False, n_ctx=n_ctx, name="rglru_bwd_f")
    dxs_r, d_wa1, d_ba1, d_wx1, d_bx1, d_lam1 = _rglru_bwd(
        xs, hp_r, d_h, *rnn_wb[1], reverse=True, n_ctx=n_ctx, name="rglru_bwd_r")
    d_xr_l, d_xr_c, d_cw, d_cb = _conv_bwd(dxs_f, dxs_r, proj_l, proj_c, sp['conv_w'])
    dp_qk_l, d_qg, d_kg_l = _qk_bwd(dq, dk_all, proj_l, q_g, k_g, cos_l, sin_l, name="qk_bwd_lat", dk_row0=n_ctx)
    dp_qk_c, _, d_kg_c = _qk_bwd(None, dk_all, proj_c, q_g, k_g, cos_c, sin_c, name="qk_bwd_ctx")
    d_proj = _assemble_d_proj(dp_qk_l, dp_qk_c, dv_all, d_xr_l, d_xr_c, d_gate)
    dw_ar_in = _matmul(h_all, d_proj, kind='tn', out_split='n', out_dtype=BF16, name="ar_in_dw")
    d_hl = _matmul(d_proj, big['ar_in'], kind='nt', b_split='n', out_dtype=F32, a_rows=(0, n_lat), name="ar_in_dx_lat")
    d_hc = _matmul(d_proj, big['ar_in'], kind='nt', b_split='n', out_dtype=F32, a_rows=(n_lat, n_ctx), name="ar_in_dx_ctx")
    grad_x, dg00, d_sc1, d_sh1 = _norm_bwd(d_hl, xl0, g[0][0], one(sc1), extra=dxm0, out_dtype=F32, name="l0_mod1_bwd")
    _, dg00c, d_mc_scale, d_mc_shift = _norm_bwd(d_hc, xc0, g[0][0], one(mc0[1]), out_dtype=BF16, name="l0_mod1_ctx_bwd")

    zeros_d = jnp.zeros_like(d_sh1)
    small = {
        'd_ml0': jnp.concatenate([d_sh1, d_sc1, d_gt1, d_sh2, d_sc2, d_gt2], axis=1),
        'd_ml1': jnp.concatenate([d_th1, d_tc1, d_tg1, d_th2, d_tc2, d_tg2], axis=1),
        'd_mc0': jnp.concatenate([d_mc_shift, d_mc_scale] + [zeros_d] * 4, axis=1),
        'norm_g': jnp.stack([jnp.concatenate([dg00 + dg00c, dg01, dg02, dg03], axis=0),
                             jnp.concatenate([dg10, dg11, dg12, dg13], axis=0)]),
        'q_g': d_qg, 'k_g': d_kg_l + d_kg_c, 'conv_w': d_cw, 'conv_b': d_cb,
        'wa': jnp.stack([d_wa0, d_wa1]), 'ba': jnp.concatenate([d_ba0, d_ba1], axis=0),
        'wx': jnp.stack([d_wx0, d_wx1]), 'bx': jnp.concatenate([d_bx0, d_bx1], axis=0),
        'lam': jnp.concatenate([d_lam0, d_lam1], axis=0),
        'gm_b_in': d_gm_b_in, 'gm_v_g': d_vg, 'gm_v_b': d_vb, 'gm_w_sp': d_wsp, 'gm_b_sp': d_bsp_t.T,
        'loss': loss,
    }
    return grad_x, small, {'ar_in': dw_ar_in, 'ar_out': dw_ar_out}


MOD_ROWS = 16
SMALL_F32 = ('d_ml0', 'd_ml1', 'd_mc0', 'norm_g', 'q_g', 'k_g', 'conv_w', 'conv_b', 'ba', 'bx', 'lam', 'gm_b_in', 'gm_v_g',
             'gm_v_b', 'gm_b_sp', 'loss')
SMALL_BF16 = ('wa', 'wx', 'gm_w_sp')


def _silu(v):
    return v * _sigmoid(v)


def _chip_concat(gathered, axis):
    return jnp.concatenate([gathered[2 * q] for q in range(N_CHIPS)], axis=axis)


def kernel(x, c, ctx, c_ctx, w_mod, b_mod, norm_g, w_ff_in, w_ff_out, ar_w_in, ar_q_g, ar_k_g, ar_conv_w, ar_conv_b, ar_wa, ar_ba, ar_wx, ar_bx, ar_lambda, ar_w_out, gm_w_in, gm_b_in, gm_v_g, gm_v_b, gm_w_sp, gm_b_sp, gm_w_out, loss_target, m_c_ctx, m_w_mod, m_b_mod, m_norm_g, m_w_ff_in, m_w_ff_out, m_ar_w_in, m_ar_q_g, m_ar_k_g, m_ar_conv_w, m_ar_conv_b, m_ar_wa, m_ar_ba, m_ar_wx, m_ar_bx, m_ar_lambda, m_ar_w_out, m_gm_w_in, m_gm_b_in, m_gm_v_g, m_gm_v_b, m_gm_w_sp, m_gm_b_sp, m_gm_w_out, v_c_ctx, v_w_mod, v_b_mod, v_norm_g, v_w_ff_in, v_w_ff_out, v_ar_w_in, v_ar_q_g, v_ar_k_g, v_ar_conv_w, v_ar_conv_b, v_ar_wa, v_ar_ba, v_ar_wx, v_ar_bx, v_ar_lambda, v_ar_w_out, v_gm_w_in, v_gm_b_in, v_gm_v_g, v_gm_v_b, v_gm_w_sp, v_gm_b_sp, v_gm_w_out):
    weights = dict(c_ctx=c_ctx, w_mod=w_mod, b_mod=b_mod, norm_g=norm_g, w_ff_in=w_ff_in, w_ff_out=w_ff_out, ar_w_in=ar_w_in,
                   ar_q_g=ar_q_g, ar_k_g=ar_k_g, ar_conv_w=ar_conv_w, ar_conv_b=ar_conv_b, ar_wa=ar_wa, ar_ba=ar_ba, ar_wx=ar_wx,
                   ar_bx=ar_bx, ar_lambda=ar_lambda, ar_w_out=ar_w_out, gm_w_in=gm_w_in, gm_b_in=gm_b_in, gm_v_g=gm_v_g,
                   gm_v_b=gm_v_b, gm_w_sp=gm_w_sp, gm_b_sp=gm_b_sp, gm_w_out=gm_w_out)
    m_in = dict(c_ctx=m_c_ctx, w_mod=m_w_mod, b_mod=m_b_mod, norm_g=m_norm_g, w_ff_in=m_w_ff_in, w_ff_out=m_w_ff_out,
                ar_w_in=m_ar_w_in, ar_q_g=m_ar_q_g, ar_k_g=m_ar_k_g, ar_conv_w=m_ar_conv_w, ar_conv_b=m_ar_conv_b, ar_wa=m_ar_wa,
                ar_ba=m_ar_ba, ar_wx=m_ar_wx, ar_bx=m_ar_bx, ar_lambda=m_ar_lambda, ar_w_out=m_ar_w_out, gm_w_in=m_gm_w_in,
                gm_b_in=m_gm_b_in, gm_v_g=m_gm_v_g, gm_v_b=m_gm_v_b, gm_w_sp=m_gm_w_sp, gm_b_sp=m_gm_b_sp, gm_w_out=m_gm_w_out)
    v_in = dict(c_ctx=v_c_ctx, w_mod=v_w_mod, b_mod=v_b_mod, norm_g=v_norm_g, w_ff_in=v_w_ff_in, w_ff_out=v_w_ff_out,
                ar_w_in=v_ar_w_in, ar_q_g=v_ar_q_g, ar_k_g=v_ar_k_g, ar_conv_w=v_ar_conv_w, ar_conv_b=v_ar_conv_b, ar_wa=v_ar_wa,
                ar_ba=v_ar_ba, ar_wx=v_ar_wx, ar_bx=v_ar_bx, ar_lambda=v_ar_lambda, ar_w_out=v_ar_w_out, gm_w_in=v_gm_w_in,
                gm_b_in=v_gm_b_in, gm_v_g=v_gm_v_g, gm_v_b=v_gm_v_b, gm_w_sp=v_gm_w_sp, gm_b_sp=v_gm_b_sp, gm_w_out=v_gm_w_out)

    xi, yi, ci = lax.axis_index("x"), lax.axis_index("y"), lax.axis_index("c")
    chip = 2 * xi + yi
    dev = 4 * xi + 2 * yi + ci
    place = jnp.stack([chip, ci]).astype(jnp.int32)
    n_lat, d = x.shape[1], x.shape[2]
    d6 = 6 * d
    cols_mod = w_mod.shape[2]

    mine = [c, norm_g, ar_conv_w[0], ar_ba[0], ar_bx[0], ar_lambda[0], gm_b_in, gm_v_g, gm_v_b]
    gathered = _allgather8([_pack(mine)], "gather_small_params")[0]
    first = _split_start("gather_first_start", [_cast_shard(ar_w_in, place, 0, "cast_ar_in")], _gather_copies, N_CHIPS - 1, gathered)
    parts = _unpack_devices(gathered, [a.shape for a in mine])
    c_all = parts[0].reshape(N_DEV, d)
    sp = {'norm_g': _chip_concat(parts[1], 2), 'q_g': ar_q_g, 'k_g': ar_k_g, 'conv_w': _chip_concat(parts[2], 1),
          'conv_b': ar_conv_b, 'wa': ar_wa[0], 'ba': _chip_concat(parts[3], 1), 'wx': ar_wx[0], 'bx': _chip_concat(parts[4], 1),
          'lam': _chip_concat(parts[5], 1), 'gm_b_in': _chip_concat(parts[6], 1), 'gm_v_g': _chip_concat(parts[7], 1),
          'gm_v_b': _chip_concat(parts[8], 1), 'gm_w_sp': gm_w_sp[0], 'gm_b_sp': gm_b_sp[0]}

    def mod_operand(c_rows, cc):
        row = lax.broadcasted_iota(jnp.int32, (MOD_ROWS - N_DEV, d), 0)
        lower = jnp.where(row == 0, jnp.broadcast_to(_silu(cc), (MOD_ROWS - N_DEV, d)), 0.0)
        sig = _sigmoid(cc)
        return jnp.concatenate([_silu(c_rows), lower], axis=0), sig * (1.0 + cc * (1.0 - sig))

    s_mod, dsilu_ctx = _small(mod_operand, [((MOD_ROWS, d), F32), ((1, d), F32)], c_all, c_ctx[None, :], name="mod_operand")
    b_mod_mine = lax.dynamic_slice(b_mod, (0, chip * cols_mod), (2, cols_mod))
    mod = [_matmul(s_mod, w_mod, kind='nn', b_layer=i, bias=b_mod_mine[i][None, :], out_dtype=F32, name=f"mod_fwd_{i}")
           for i in range(2)]
    mod_all = _allgather8([jnp.concatenate(mod, axis=0)], "gather_mod")[0]
    mod_all = _chip_concat(mod_all, 1).reshape(2, MOD_ROWS, d6)
    ml = [jnp.split(lax.dynamic_slice(mod_all[i], (dev, 0), (1, d6)), 6, axis=1) for i in range(2)]
    mc0 = jnp.split(mod_all[0, N_DEV:N_DEV + 1], 6, axis=1)[:2]

    names = ('w_ff_in', 'w_ff_out', 'ar_w_in', 'ar_w_out', 'gm_w_in', 'gm_w_out')
    keys = {'w_ff_in': ('ff_in0', 'ff_in1'), 'w_ff_out': ('ff_out0', 'ff_out1'), 'ar_w_in': ('ar_in',), 'ar_w_out': ('ar_out',),
            'gm_w_in': ('gm_in',), 'gm_w_out': ('gm_out',)}
    shards = {key: _cast_shard(weights[n], place, layer, f"cast_{key}", after=first[3]) for n in names
              for layer, key in enumerate(keys[n]) if key != 'ar_in'}
    send, recv, bufs, _ = first
    bufs = _split_wait("gather_first_wait", bufs, send, recv, _gather_copies, mod_all)
    send, recv, bufs, token = _split_start("forward_first_start", bufs, _forward_copies, N_CHIPS - 1)
    shards['ar_in'] = _split_wait("forward_first_wait", bufs, send, recv, _forward_copies, token)[0]
    link = _MeshLink(place, shards)

    grad_x, small, last_grads = _local_step(x[0], ctx[0], loss_target[0], ml, mc0, sp, link)

    def step(n, grad):
        return _adamw(weights[n], grad.reshape(weights[n].shape), m_in[n], v_in[n], f"adamw_{n}", rewrite_grad=n in names)

    small_f32, small_bf16 = [small[k] for k in SMALL_F32], [small[k] for k in SMALL_BF16]
    dev_arr = dev.astype(jnp.int32)[None]
    slots = [_into_slot(_pack(small_f32), dev_arr, "small_grads_slot_f32"),
             _into_slot(_pack(small_bf16, BF16), dev_arr, "small_grads_slot_bf16")]
    s_send, s_recv, slots, s_token = _split_start("small_grads_start", slots, _gather8_copies, 2 * N_CHIPS, grad_x)
    link.gradients('ar', last_grads, s_token)
    link.poll(link.last_token)
    reduced = link.reduce(('ffn1', 'gm', 'ffn0'), link.last_token)
    stepped = {n: step(n, reduced[names.index(n)]) for n in ('w_ff_in', 'w_ff_out', 'gm_w_in', 'gm_w_out')}
    reduced = link.reduce(('ar',), stepped['gm_w_out'][1])
    stepped.update({n: step(n, reduced[names.index(n)]) for n in ('ar_w_in', 'ar_w_out')})
    slots = _split_wait("small_grads_wait", slots, s_send, s_recv, _gather8_copies, stepped['ar_w_out'][1])
    small8, small8_bf16 = _forward_slots(slots, "small_grads_forward")
    total = dict(zip(SMALL_F32, _unpack(_sum_devices(small8).reshape(-1), [a.shape for a in small_f32])))
    total.update(zip(SMALL_BF16, _unpack(_sum_devices(small8_bf16).reshape(-1), [a.shape for a in small_bf16])))
    per_dev = _unpack_devices(small8, [(d6,), (d6,)])
    pad_rows = jnp.zeros((MOD_ROWS - N_DEV - 1, d6), F32)
    d_mod = [jnp.concatenate([per_dev[0], total['d_mc0'], pad_rows], axis=0),
             jnp.concatenate([per_dev[1], jnp.zeros((MOD_ROWS - N_DEV, d6), F32)], axis=0)]
    d_mod_mine = [lax.dynamic_slice(dm, (0, chip * cols_mod), (MOD_ROWS, cols_mod)) for dm in d_mod]
    g_w_mod = None
    for i in range(2):
        g_w_mod = _matmul(s_mod, d_mod_mine[i], kind='tn', out_dtype=F32, out_stack=(i, 2, g_w_mod), name=f"mod_dw_{i}")
    d_s_part = _matmul(d_mod_mine[0], w_mod, kind='nt', b_layer=0, out_dtype=F32, name="mod_ds")
    d_s_all = _allgather8([d_s_part[N_DEV:]], "gather_mod_ds")[0]

    def c_ctx_grad(parts_, dsilu):
        acc = parts_[0, 0:1]
        for q in range(1, N_CHIPS):
            acc = acc + parts_[2 * q, 0:1]
        return (acc * dsilu,)

    g_c_ctx = _small(c_ctx_grad, [((1, d), F32)], d_s_all, dsilu_ctx, name="c_ctx_grad")[0].reshape(d)

    def mine_of(full_grad, axis, n_shard):
        return lax.dynamic_slice_in_dim(full_grad, chip * n_shard, n_shard, axis=axis)

    grads_out = {
        'c_ctx': g_c_ctx, 'w_mod': g_w_mod,
        'b_mod': jnp.stack([total['d_ml0'][0] + total['d_mc0'][0], total['d_ml1'][0]]),
        'norm_g': mine_of(total['norm_g'], 2, norm_g.shape[2]),
        'ar_q_g': total['q_g'], 'ar_k_g': total['k_g'], 'ar_conv_w': mine_of(total['conv_w'], 1, ar_conv_w.shape[2])[None],
        'ar_conv_b': total['conv_b'], 'ar_wa': total['wa'][None], 'ar_ba': mine_of(total['ba'], 1, ar_ba.shape[2])[None],
        'ar_wx': total['wx'][None], 'ar_bx': mine_of(total['bx'], 1, ar_bx.shape[2])[None],
        'ar_lambda': mine_of(total['lam'], 1, ar_lambda.shape[2])[None],
        'gm_b_in': mine_of(total['gm_b_in'], 1, gm_b_in.shape[1]),
        'gm_v_g': mine_of(total['gm_v_g'], 1, gm_v_g.shape[1]), 'gm_v_b': mine_of(total['gm_v_b'], 1, gm_v_b.shape[1]),
        'gm_w_sp': total['gm_w_sp'][None], 'gm_b_sp': total['gm_b_sp'][None],
    }
    stepped.update({n: step(n, grad) for n, grad in grads_out.items()})
    stepped = [stepped[n] for n in weights]
    loss = total['loss'].reshape(())
    return (loss, grad_x[None], *[s[0] for s in stepped], *[s[1] for s in stepped], *[s[2] for s in stepped],
            *[s[3] for s in stepped])
```

```python
import functools
import math

import jax
import jax.numpy as jnp
from jax import lax
from jax.experimental import pallas as pl
from jax.experimental.pallas import tpu as pltpu

F32 = jnp.float32
BF16 = jnp.bfloat16
MESH = pl.DeviceIdType.MESH
ANY = pl.BlockSpec(memory_space=pl.ANY)

VMEM_LIMIT_BYTES = 52 * 1024 * 1024
LANES = 128
N_CHIPS = 4
N_DEV = 8

HEAD_DIM = 128
N_HEADS = 8
N_KV = 2
GROUP = N_HEADS // N_KV
ATTN_W = N_HEADS * HEAD_DIM
KV_W = N_KV * HEAD_DIM
D_RNN = 1024
RNN_BLOCKS = 8
RNN_BW = D_RNN // RNN_BLOCKS
CONV_W = 4
RG_C = 8.0
GRID_W = 64
ROPE_THETA = 10000.0
ROPE_PAIRS = HEAD_DIM // 4
GM_GROUPS = 16
CHUNK = 128
EPS = 1e-6
ADAM_LR, ADAM_B1, ADAM_B2, ADAM_EPS, ADAM_WD, ADAM_STEP = 0.001, 0.9, 0.999, 1e-08, 0.01, 10
GELU_C = math.sqrt(2.0 / math.pi)
LOG2E = math.log2(math.e)


def _params(sem=None):
    return pltpu.CompilerParams(dimension_semantics=sem, vmem_limit_bytes=VMEM_LIMIT_BYTES)


def _tile(dim, pref, unit):
    best = None
    t = unit
    while t <= min(dim, pref):
        if dim % t == 0:
            best = t
        t += unit
    return best if best is not None else dim


def _full(shape):
    nd = len(shape)
    return pl.BlockSpec(shape, lambda *_: (0,) * nd)


def _blocked_map(split, per_q):
    assert split == 'n'
    return lambda r, c: (c // per_q, r, c % per_q)


def _logical_shape(arr, split):
    if split == 'n':
        return arr.shape[1], arr.shape[0] * arr.shape[2]
    if split == 'k':
        return arr.shape[0] * arr.shape[1], arr.shape[2]
    return arr.shape[-2:]


def _matmul(a, b, *, kind, name, out_dtype, b_split=None, out_split=None, bias=None, epilogue=None, extra=None,
            a_rows=None, b_layer=None, out_stack=None, pref=(1024, 1024, 2048)):
    if b_split == 'k':
        b, b_split = b.reshape(-1, b.shape[-1]), None
    blocked_rows_out = out_split == 'k'
    if blocked_rows_out:
        assert epilogue != 'relu2'
        out_split = None
    b_rows, b_cols = _logical_shape(b, b_split)
    row0 = 0
    if kind == 'nn':
        m, kc = a.shape
        n = b_cols
        assert b_rows == kc
    elif kind == 'nt':
        m, kc = a.shape
        n = b_rows
        assert b_cols == kc
    if a_rows is not None:
        assert kind != 'tn'
        row0, m = a_rows
    if kind == 'tn':
        kc, m = a.shape
        n = b_cols
        assert b_rows == kc
    b_row_ext = b.shape[1] if b_split == 'k' else b_rows
    b_col_ext = b.shape[2] if b_split == 'n' else b_cols
    out_row_ext = m // N_CHIPS if out_split == 'k' else m
    out_col_ext = n // N_CHIPS if out_split == 'n' else n
    if kind == 'nn':
        ti = _tile(math.gcd(min(m, out_row_ext), row0), pref[0], 16)
        tj = _tile(math.gcd(b_col_ext, out_col_ext), pref[1], LANES)
        tl = _tile(b_row_ext, pref[2], LANES)
        a_spec = pl.BlockSpec((ti, tl), lambda i, j, l: (i + row0 // ti, l))
        b_tile, b_rc = (tl, tj), (lambda i, j, l: (l, j))
        dims = (((1,), (0,)), ((), ()))
    elif kind == 'nt':
        ti = _tile(math.gcd(min(m, out_row_ext), row0), pref[0], 16)
        tj = _tile(math.gcd(b_row_ext, out_col_ext), pref[1], LANES)
        tl = _tile(b_col_ext, pref[2], LANES)
        a_spec = pl.BlockSpec((ti, tl), lambda i, j, l: (i + row0 // ti, l))
        b_tile, b_rc = (tj, tl), (lambda i, j, l: (j, l))
        dims = (((1,), (1,)), ((), ()))
    else:
        ti = _tile(out_row_ext, pref[0], LANES)
        tj = _tile(math.gcd(b_col_ext, out_col_ext), pref[1], LANES)
        tl = _tile(b_row_ext, pref[2], 16)
        a_spec = pl.BlockSpec((tl, ti), lambda i, j, l: (l, i))
        b_tile, b_rc = (tl, tj), (lambda i, j, l: (l, j))
        dims = (((0,), (0,)), ((), ()))
    grid = (m // ti, n // tj, kc // tl)
    n_l = grid[2]

    if b_layer is not None:
        b_spec = pl.BlockSpec((None,) + b_tile, lambda i, j, l: (b_layer, *b_rc(i, j, l)))
    elif b_split is None:
        b_spec = pl.BlockSpec(b_tile, b_rc)
    else:
        per_q = (b.shape[2] // b_tile[1]) if b_split == 'n' else (b.shape[1] // b_tile[0])
        bmap = _blocked_map(b_split, per_q)
        b_spec = pl.BlockSpec((None,) + b_tile, lambda i, j, l: bmap(*b_rc(i, j, l)))
    if out_stack is not None:
        layer, n_layers, _ = out_stack
        out_shape2 = (n_layers, m, n)
        o_spec = pl.BlockSpec((None, ti, tj), lambda i, j, l: (layer, i, j))
    elif out_split is None:
        out_shape2 = (m, n)
        o_spec = pl.BlockSpec((ti, tj), lambda i, j, l: (i, j))
    else:
        out_shape2 = (N_CHIPS, m // N_CHIPS, n) if out_split == 'k' else (N_CHIPS, m, n // N_CHIPS)
        per_q = (out_shape2[2] // tj) if out_split == 'n' else (out_shape2[1] // ti)
        omap = _blocked_map(out_split, per_q)
        o_spec = pl.BlockSpec((None, ti, tj), lambda i, j, l: omap(i, j))

    in_specs = [a_spec, b_spec]
    operands = [a, b]
    if bias is not None:
        in_specs.append(pl.BlockSpec((1, tj), lambda i, j, l: (0, j)))
        operands.append(bias)
    if extra is not None:
        in_specs.append(pl.BlockSpec((ti, tj), lambda i, j, l: (i, j)))
        operands.append(extra)
    if epilogue == 'relu2':
        out_shape = (jax.ShapeDtypeStruct(out_shape2, out_dtype), jax.ShapeDtypeStruct(out_shape2, out_dtype))
        out_specs = (o_spec, o_spec)
    else:
        out_shape = jax.ShapeDtypeStruct(out_shape2, out_dtype)
        out_specs = o_spec
    has_bias, has_extra = bias is not None, extra is not None
    has_dest = out_stack is not None and out_stack[2] is not None
    if has_dest:
        in_specs.append(ANY)
        operands.append(out_stack[2])

    def body(*refs):
        a_ref, b_ref = refs[0], refs[1]
        pos = 2
        bias_ref = extra_ref = None
        if has_bias:
            bias_ref = refs[pos]
            pos += 1
        if has_extra:
            extra_ref = refs[pos]
            pos += 1
        if has_dest:
            pos += 1
        outs = refs[pos:] if n_l == 1 else refs[pos:-1]

        def finish(acc):
            if has_bias:
                acc = acc + bias_ref[...]
            if epilogue == 'relu2':
                r = jnp.maximum(acc, 0.0)
                outs[0][...] = r.astype(outs[0].dtype)
                outs[1][...] = (r * r).astype(outs[1].dtype)
            elif epilogue == 'times2x':
                outs[0][...] = (acc * (2.0 * extra_ref[...].astype(F32))).astype(outs[0].dtype)
            else:
                outs[0][...] = acc.astype(outs[0].dtype)

        def product():
            return lax.dot_general(a_ref[...].astype(BF16), b_ref[...].astype(BF16), dims, preferred_element_type=F32)

        if n_l == 1:
            finish(product())
            return
        acc_ref = refs[-1]
        step = pl.program_id(2)

        @pl.when(step == 0)
        def _():
            acc_ref[...] = jnp.zeros_like(acc_ref)

        acc_ref[...] += product()

        @pl.when(step == n_l - 1)
        def _():
            finish(acc_ref[...])

    result = pl.pallas_call(
        body, name=name, grid=grid, in_specs=in_specs, out_specs=out_specs, out_shape=out_shape,
        input_output_aliases={len(operands) - 1: 0} if has_dest else {},
        scratch_shapes=[] if n_l == 1 else [pltpu.VMEM((ti, tj), F32)],
        compiler_params=_params(("parallel", "parallel", "arbitrary")),
    )(*operands)
    return result.reshape(N_CHIPS, m // N_CHIPS, n) if blocked_rows_out else result


def _small(fn, out_shapes, *arrays, name):
    n_in = len(arrays)

    def body(*refs):
        res = fn(*[r[...] for r in refs[:n_in]])
        for o_ref, v in zip(refs[n_in:], res):
            o_ref[...] = v.astype(o_ref.dtype)

    return pl.pallas_call(
        body, name=name, out_shape=tuple(jax.ShapeDtypeStruct(s, d) for s, d in out_shapes),
        in_specs=[_full(a.shape) for a in arrays], out_specs=tuple(_full(s) for s, _ in out_shapes), grid=(1,),
        compiler_params=_params(("arbitrary",)),
    )(*arrays)


def _rows_tile(rows, cols, itemsize=4, budget=2 * 1024 * 1024):
    return _tile(rows, max(16, budget // (cols * itemsize)), 16)


def _rowwise(fn, out_dtypes, *arrays, name):
    rows, cols = arrays[0].shape
    tr = _rows_tile(rows, cols)
    n_in = len(arrays)

    def body(*refs):
        res = fn(*[r[...] for r in refs[:n_in]])
        for o_ref, v in zip(refs[n_in:], res):
            o_ref[...] = v.astype(o_ref.dtype)

    spec = pl.BlockSpec((tr, cols), lambda i: (i, 0))
    return pl.pallas_call(
        body, name=name, grid=(rows // tr,), in_specs=[spec] * n_in, out_specs=tuple(spec for _ in out_dtypes),
        out_shape=tuple(jax.ShapeDtypeStruct((rows, cols), d) for d in out_dtypes),
        compiler_params=_params(("parallel",)),
    )(*arrays)


def _as2d(a):
    return a.reshape(1, a.size) if a.ndim < 2 else a.reshape(-1, a.shape[-1])


def _cast_shard(w, place, layer, name, after=None):
    _, rows, cols = w.shape
    tr = _rows_tile(rows, cols)

    def body(place_ref, w_ref, *rest):
        rest[-1][...] = w_ref[...].astype(rest[-1].dtype)

    return pl.pallas_call(
        body, name=name, out_shape=jax.ShapeDtypeStruct((N_CHIPS, rows, cols), BF16),
        grid_spec=pltpu.PrefetchScalarGridSpec(
            num_scalar_prefetch=1, grid=(rows // tr,),
            in_specs=[pl.BlockSpec((None, tr, cols), lambda i, pr: (layer, i, 0))] + ([] if after is None else [ANY]),
            out_specs=pl.BlockSpec((None, tr, cols), lambda i, pr: (pr[0], i, 0))),
        compiler_params=_params(("parallel",)),
    )(place, w, *([] if after is None else [after]))


def _norm_fwd(x, g, a, b=None, res=None, *, out_dtype, name, into=None):
    rows, d = x.shape
    row0, total, dest = into if into is not None else (0, rows, None)
    tr = _rows_tile(math.gcd(rows, row0), d, budget=4 * 1024 * 1024)
    has_b, has_res = b is not None, res is not None

    def body(*refs):
        x_ref, g_ref, a_ref = refs[:3]
        pos = 3
        xv = x_ref[...]
        rstd = lax.rsqrt(jnp.mean(xv * xv, axis=-1, keepdims=True) + EPS)
        y = (xv * rstd * g_ref[...]) * a_ref[...]
        if has_b:
            y = y + refs[pos][...]
            pos += 1
        if has_res:
            y = y + refs[pos][...]
            pos += 1
        refs[-1][...] = y.astype(refs[-1].dtype)

    row = pl.BlockSpec((tr, d), lambda i: (i, 0))
    vec = pl.BlockSpec((1, d), lambda i: (0, 0))
    operands, specs = [x, g, a], [row, vec, vec]
    if has_b:
        operands.append(b)
        specs.append(vec)
    if has_res:
        operands.append(res)
        specs.append(row)
    if dest is not None:
        operands.append(dest)
        specs.append(ANY)
    return pl.pallas_call(
        body, name=name, grid=(rows // tr,), in_specs=specs, out_specs=pl.BlockSpec((tr, d), lambda i: (i + row0 // tr, 0)),
        out_shape=jax.ShapeDtypeStruct((total, d), out_dtype), compiler_params=_params(("parallel",)),
        input_output_aliases={} if dest is None else {len(operands) - 1: 0},
    )(*operands)


def _rstd(v):
    return lax.rsqrt(jnp.mean(v * v, axis=-1, keepdims=True) + EPS)


def _res_mod_fwd(o, x, g_res, gate, g_mod, a_mod, b_mod, *, name):
    rows, d = x.shape
    tr = _rows_tile(rows, d)

    def body(o_ref, x_ref, gr_ref, gate_ref, gm_ref, a_ref, b_ref, xm_ref, h_ref):
        ov = o_ref[...]
        xm = x_ref[...] + (ov * _rstd(ov) * gr_ref[...]) * gate_ref[...]
        xm_ref[...] = xm
        h_ref[...] = ((xm * _rstd(xm) * gm_ref[...]) * a_ref[...] + b_ref[...]).astype(h_ref.dtype)

    row = pl.BlockSpec((tr, d), lambda i: (i, 0))
    vec = pl.BlockSpec((1, d), lambda i: (0, 0))
    return pl.pallas_call(
        body, name=name, grid=(rows // tr,), in_specs=[row, row, vec, vec, vec, vec, vec], out_specs=(row, row),
        out_shape=(jax.ShapeDtypeStruct((rows, d), F32), jax.ShapeDtypeStruct((rows, d), BF16)),
        compiler_params=_params(("parallel",)),
    )(o, x, g_res, gate, g_mod, a_mod, b_mod)


def _mod_res_bwd(d_h, xm, g_mod, a_mod, extra, o, g_res, gate, *, name):
    rows, d = xm.shape
    tr = _rows_tile(rows, d)

    def body(dh_ref, xm_ref, gm_ref, a_ref, ex_ref, o_ref, gr_ref, gate_ref,
             dxm_ref, do_ref, dgm_ref, da_ref, db_ref, dgr_ref, dgate_ref):
        @pl.when(pl.program_id(0) == 0)
        def _():
            for ref in (dgm_ref, da_ref, db_ref, dgr_ref, dgate_ref):
                ref[...] = jnp.zeros_like(ref)

        def norm_adjoint(dy, xv, gain, scale, dgain_ref, dscale_ref):
            rstd = _rstd(xv)
            nrm = xv * rstd
            dscale_ref[...] += jnp.sum(dy * (nrm * gain), axis=0, keepdims=True)
            dt = dy * scale
            dgain_ref[...] += jnp.sum(dt * nrm, axis=0, keepdims=True)
            dn = dt * gain
            return rstd * (dn - nrm * jnp.mean(dn * nrm, axis=-1, keepdims=True))

        dhv = dh_ref[...].astype(F32)
        db_ref[...] += jnp.sum(dhv, axis=0, keepdims=True)
        dxm = norm_adjoint(dhv, xm_ref[...], gm_ref[...], a_ref[...], dgm_ref, da_ref) + ex_ref[...]
        dxm_ref[...] = dxm
        do_ref[...] = norm_adjoint(dxm, o_ref[...], gr_ref[...], gate_ref[...], dgr_ref, dgate_ref).astype(do_ref.dtype)

    row = pl.BlockSpec((tr, d), lambda i: (i, 0))
    vec = pl.BlockSpec((1, d), lambda i: (0, 0))
    vshape = jax.ShapeDtypeStruct((1, d), F32)
    return pl.pallas_call(
        body, name=name, grid=(rows // tr,), in_specs=[row, row, vec, vec, row, row, vec, vec],
        out_specs=(row, row, vec, vec, vec, vec, vec),
        out_shape=(jax.ShapeDtypeStruct((rows, d), F32), jax.ShapeDtypeStruct((rows, d), BF16)) + (vshape,) * 5,
        compiler_params=_params(("arbitrary",)),
    )(d_h, xm, g_mod, a_mod, extra, o, g_res, gate)


def _norm_bwd(dy, x, g, a, extra=None, *, out_dtype, name):
    rows, d = x.shape
    tr = _rows_tile(rows, d)
    has_extra = extra is not None

    def body(*refs):
        dy_ref, x_ref, g_ref, a_ref = refs[:4]
        pos = 4
        extra_ref = None
        if has_extra:
            extra_ref = refs[pos]
            pos += 1
        dx_ref, dg_ref, da_ref, db_ref = refs[pos:pos + 4]

        @pl.when(pl.program_id(0) == 0)
        def _():
            dg_ref[...] = jnp.zeros_like(dg_ref)
            da_ref[...] = jnp.zeros_like(da_ref)
            db_ref[...] = jnp.zeros_like(db_ref)

        xv = x_ref[...]
        dyv = dy_ref[...].astype(F32)
        rstd = lax.rsqrt(jnp.mean(xv * xv, axis=-1, keepdims=True) + EPS)
        nrm = xv * rstd
        gv = g_ref[...]
        da_ref[...] += jnp.sum(dyv * (nrm * gv), axis=0, keepdims=True)
        db_ref[...] += jnp.sum(dyv, axis=0, keepdims=True)
        dt = dyv * a_ref[...]
        dg_ref[...] += jnp.sum(dt * nrm, axis=0, keepdims=True)
        dn = dt * gv
        dx = rstd * (dn - nrm * jnp.mean(dn * nrm, axis=-1, keepdims=True))
        if has_extra:
            dx = dx + extra_ref[...]
        dx_ref[...] = dx.astype(dx_ref.dtype)

    row = pl.BlockSpec((tr, d), lambda i: (i, 0))
    vec = pl.BlockSpec((1, d), lambda i: (0, 0))
    operands, specs = [dy, x, g, a], [row, row, vec, vec]
    if has_extra:
        operands.append(extra)
        specs.append(row)
    vshape = jax.ShapeDtypeStruct((1, d), F32)
    return pl.pallas_call(
        body, name=name, grid=(rows // tr,), in_specs=specs, out_specs=(row, vec, vec, vec),
        out_shape=(jax.ShapeDtypeStruct((rows, d), out_dtype), vshape, vshape, vshape),
        compiler_params=_params(("arbitrary",)),
    )(*operands)


def _final_res_loss(f, x, g, gate, target):
    rows, d = x.shape
    tr = _rows_tile(rows, d)

    def body(f_ref, x_ref, g_ref, gate_ref, t_ref, dy_ref, df_ref, dg_ref, dgate_ref, loss_ref):
        @pl.when(pl.program_id(0) == 0)
        def _():
            for ref in (dg_ref, dgate_ref, loss_ref):
                ref[...] = jnp.zeros_like(ref)

        fv, gv, gatev = f_ref[...], g_ref[...], gate_ref[...]
        rstd = _rstd(fv)
        nrm = fv * rstd
        err = x_ref[...] + (nrm * gv) * gatev - t_ref[...]
        loss_ref[...] += jnp.sum(jnp.sum(err * err, axis=-1, keepdims=True), axis=0, keepdims=True) * (0.5 / d)
        dy = err * (1.0 / d)
        dy_ref[...] = dy
        dgate_ref[...] += jnp.sum(dy * (nrm * gv), axis=0, keepdims=True)
        dt = dy * gatev
        dg_ref[...] += jnp.sum(dt * nrm, axis=0, keepdims=True)
        dn = dt * gv
        df_ref[...] = (rstd * (dn - nrm * jnp.mean(dn * nrm, axis=-1, keepdims=True))).astype(df_ref.dtype)

    row = pl.BlockSpec((tr, d), lambda i: (i, 0))
    vec = pl.BlockSpec((1, d), lambda i: (0, 0))
    vshape = jax.ShapeDtypeStruct((1, d), F32)
    return pl.pallas_call(
        body, name="final_res_loss", grid=(rows // tr,), in_specs=[row, row, vec, vec, row],
        out_specs=(row, row, vec, vec, _full((1, 1))),
        out_shape=(jax.ShapeDtypeStruct((rows, d), F32), jax.ShapeDtypeStruct((rows, d), BF16), vshape, vshape,
                   jax.ShapeDtypeStruct((1, 1), F32)),
        compiler_params=_params(("arbitrary",)),
    )(f, x, g, gate, target)


def _rope_partner(v):
    lane = lax.broadcasted_iota(jnp.int32, v.shape, 1)
    up = pltpu.roll(v, HEAD_DIM - ROPE_PAIRS, 1)
    down = pltpu.roll(v, ROPE_PAIRS, 1)
    return jnp.where((lane % (2 * ROPE_PAIRS)) < ROPE_PAIRS, up, down)


def _qk_fwd(proj, q_g, k_g, cos, sin, *, name, kv_into=None):
    rows = proj.shape[0]
    row0, total, kv_dest = kv_into if kv_into is not None else (0, rows, None)
    tr = _tile(math.gcd(rows, row0), 256, 16)
    width = ATTN_W + 2 * KV_W

    def body(p_ref, qg_ref, kg_ref, cos_ref, sin_ref, *rest):
        q_ref, k_ref, v_ref = rest[-3:]
        cosv, sinv = cos_ref[...], sin_ref[...]
        for h in range(N_HEADS + N_KV):
            xv = p_ref[:, h * HEAD_DIM:(h + 1) * HEAD_DIM]
            gain = qg_ref[...] if h < N_HEADS else kg_ref[...]
            t = xv * lax.rsqrt(jnp.mean(xv * xv, axis=-1, keepdims=True) + EPS) * gain
            y = t * cosv + _rope_partner(t) * sinv
            if h < N_HEADS:
                q_ref[:, h * HEAD_DIM:(h + 1) * HEAD_DIM] = y.astype(BF16)
            else:
                k_ref[:, (h - N_HEADS) * HEAD_DIM:(h - N_HEADS + 1) * HEAD_DIM] = y.astype(BF16)
        v_ref[...] = p_ref[:, ATTN_W + KV_W:width].astype(BF16)

    vec = _full((1, HEAD_DIM))
    tab = pl.BlockSpec((tr, HEAD_DIM), lambda i: (i, 0))
    kv_spec = pl.BlockSpec((tr, KV_W), lambda i: (i + row0 // tr, 0))
    kv_shape = jax.ShapeDtypeStruct((total, KV_W), BF16)
    return pl.pallas_call(
        body, name=name, grid=(rows // tr,),
        in_specs=[pl.BlockSpec((tr, width), lambda i: (i, 0)), vec, vec, tab, tab] + ([] if kv_dest is None else [ANY, ANY]),
        out_specs=(pl.BlockSpec((tr, ATTN_W), lambda i: (i, 0)), kv_spec, kv_spec),
        out_shape=(jax.ShapeDtypeStruct((rows, ATTN_W), BF16), kv_shape, kv_shape),
        input_output_aliases={} if kv_dest is None else {5: 1, 6: 2},
        compiler_params=_params(("parallel",)),
    )(proj, q_g, k_g, cos, sin, *([] if kv_dest is None else kv_dest))


def _qk_bwd(dq, dk, proj, q_g, k_g, cos, sin, *, name, dk_row0=0):
    rows = proj.shape[0]
    tr = _tile(math.gcd(rows, dk_row0), 256, 16)
    width = ATTN_W + KV_W
    has_q = dq is not None

    def body(*refs):
        pos = 0
        dq_ref = None
        if has_q:
            dq_ref = refs[0]
            pos = 1
        dk_ref, p_ref, qg_ref, kg_ref, cos_ref, sin_ref, dp_ref, dqg_ref, dkg_ref = refs[pos:pos + 9]

        @pl.when(pl.program_id(0) == 0)
        def _():
            dqg_ref[...] = jnp.zeros_like(dqg_ref)
            dkg_ref[...] = jnp.zeros_like(dkg_ref)

        cosv, sinv = cos_ref[...], sin_ref[...]
        for h in range(N_HEADS + N_KV):
            cols = slice(h * HEAD_DIM, (h + 1) * HEAD_DIM)
            if h < N_HEADS and not has_q:
                dp_ref[:, cols] = jnp.zeros((tr, HEAD_DIM), dp_ref.dtype)
                continue
            if h < N_HEADS:
                dyv, gain, dgain_ref = dq_ref[:, cols], qg_ref[...], dqg_ref
            else:
                hk = h - N_HEADS
                dyv, gain, dgain_ref = dk_ref[:, hk * HEAD_DIM:(hk + 1) * HEAD_DIM], kg_ref[...], dkg_ref
            dyv = dyv.astype(F32)
            dt = dyv * cosv + _rope_partner(dyv * sinv)
            xv = p_ref[:, cols]
            rstd = lax.rsqrt(jnp.mean(xv * xv, axis=-1, keepdims=True) + EPS)
            nrm = xv * rstd
            dgain_ref[...] += jnp.sum(dt * nrm, axis=0, keepdims=True)
            dn = dt * gain
            dp_ref[:, cols] = (rstd * (dn - nrm * jnp.mean(dn * nrm, axis=-1, keepdims=True))).astype(dp_ref.dtype)

    vec = _full((1, HEAD_DIM))
    tab = pl.BlockSpec((tr, HEAD_DIM), lambda i: (i, 0))
    operands = ([dq] if has_q else []) + [dk, proj, q_g, k_g, cos, sin]
    specs = ([pl.BlockSpec((tr, ATTN_W), lambda i: (i, 0))] if has_q else []) + [
        pl.BlockSpec((tr, KV_W), lambda i: (i + dk_row0 // tr, 0)), pl.BlockSpec((tr, width), lambda i: (i, 0)), vec, vec, tab, tab]
    return pl.pallas_call(
        body, name=name, grid=(rows // tr,), in_specs=specs,
        out_specs=(pl.BlockSpec((tr, width), lambda i: (i, 0)), vec, vec),
        out_shape=(jax.ShapeDtypeStruct((rows, width), BF16), jax.ShapeDtypeStruct((1, HEAD_DIM), F32),
                   jax.ShapeDtypeStruct((1, HEAD_DIM), F32)),
        compiler_params=_params(("arbitrary",)),
    )(*operands)


def _attn_fwd(q, k, v):
    n_q, n_k = q.shape[0], k.shape[0]
    tq = _tile(n_q, 512, 16)
    gw = GROUP * HEAD_DIM
    scale = HEAD_DIM ** -0.5

    def body(q_ref, k_ref, v_ref, o_ref, lse_ref):
        kv, vv = k_ref[...], v_ref[...]
        for g in range(GROUP):
            cols = slice(g * HEAD_DIM, (g + 1) * HEAD_DIM)
            s = lax.dot_general(q_ref[:, cols], kv, (((1,), (1,)), ((), ())), preferred_element_type=F32) * (scale * LOG2E)
            m = jnp.max(s, axis=-1, keepdims=True)
            p = jnp.exp2(s - m)
            l = jnp.sum(p, axis=-1, keepdims=True)
            o = jnp.dot(p.astype(BF16), vv, preferred_element_type=F32) / l
            o_ref[:, cols] = o.astype(o_ref.dtype)
            lse_ref[:, g:g + 1] = m + jnp.log(l) * LOG2E

    return pl.pallas_call(
        body, name="attn_fwd", grid=(N_KV, n_q // tq),
        in_specs=[pl.BlockSpec((tq, gw), lambda h, i: (i, h)), pl.BlockSpec((n_k, HEAD_DIM), lambda h, i: (0, h)),
                  pl.BlockSpec((n_k, HEAD_DIM), lambda h, i: (0, h))],
        out_specs=(pl.BlockSpec((tq, gw), lambda h, i: (i, h)), pl.BlockSpec((None, tq, GROUP), lambda h, i: (h, i, 0))),
        out_shape=(jax.ShapeDtypeStruct((n_q, ATTN_W + D_RNN), BF16), jax.ShapeDtypeStruct((N_KV, n_q, GROUP), F32)),
        compiler_params=_params(("parallel", "parallel")),
    )(q, k, v)


def _attn_bwd(q, k, v, o, lse, do):
    n_q, n_k = q.shape[0], k.shape[0]
    tq = _tile(n_q, 256, 16)
    gw = GROUP * HEAD_DIM
    scale = HEAD_DIM ** -0.5

    def body(q_ref, k_ref, v_ref, o_ref, lse_ref, do_ref, dq_ref, dk_ref, dv_ref):
        @pl.when(pl.program_id(1) == 0)
        def _():
            dk_ref[...] = jnp.zeros_like(dk_ref)
            dv_ref[...] = jnp.zeros_like(dv_ref)

        kv, vv = k_ref[...], v_ref[...]
        for g in range(GROUP):
            cols = slice(g * HEAD_DIM, (g + 1) * HEAD_DIM)
            qg = q_ref[:, cols]
            dof = do_ref[:, cols].astype(F32)
            dog = dof.astype(BF16)
            s = lax.dot_general(qg, kv, (((1,), (1,)), ((), ())), preferred_element_type=F32) * (scale * LOG2E)
            p = jnp.exp2(s - lse_ref[:, g:g + 1])
            delta = jnp.sum(dof * o_ref[:, cols].astype(F32), axis=-1, keepdims=True)
            dp = lax.dot_general(dog, vv, (((1,), (1,)), ((), ())), preferred_element_type=F32)
            ds = (p * (dp - delta) * scale).astype(BF16)
            pb = p.astype(BF16)
            dq_ref[:, cols] = jnp.dot(ds, kv, preferred_element_type=F32)
            dk_ref[...] += lax.dot_general(ds, qg, (((0,), (0,)), ((), ())), preferred_element_type=F32)
            dv_ref[...] += lax.dot_general(pb, dog, (((0,), (0,)), ((), ())), preferred_element_type=F32)

    qspec = pl.BlockSpec((tq, gw), lambda h, i: (i, h))
    kspec = pl.BlockSpec((n_k, HEAD_DIM), lambda h, i: (0, h))
    return pl.pallas_call(
        body, name="attn_bwd", grid=(N_KV, n_q // tq),
        in_specs=[qspec, kspec, kspec, qspec, pl.BlockSpec((None, tq, GROUP), lambda h, i: (h, i, 0)), qspec],
        out_specs=(qspec, kspec, kspec),
        out_shape=(jax.ShapeDtypeStruct((n_q, ATTN_W), F32), jax.ShapeDtypeStruct((n_k, KV_W), F32),
                   jax.ShapeDtypeStruct((n_k, KV_W), F32)),
        compiler_params=_params(("parallel", "arbitrary")),
    )(q, k, v, o, lse, do)


CONV_COLS = 256
XR_COL0 = ATTN_W + 2 * KV_W


def _shift_rows(v, off):
    if off == 0:
        return v
    n = v.shape[0]
    rolled = pltpu.roll(v, (-off) % n, 0)
    t = lax.broadcasted_iota(jnp.int32, v.shape, 0)
    keep = (t + off >= 0) & (t + off < n)
    return jnp.where(keep, rolled, 0.0)


def _conv_fwd(proj_l, proj_c, w, b):
    n_lat, n_ctx = proj_l.shape[0], proj_c.shape[0]
    blk0 = XR_COL0 // CONV_COLS

    def body(xl_ref, xc_ref, w_ref, b_ref, y_ref):
        for x_ref, rows in ((xc_ref, slice(0, n_ctx)), (xl_ref, slice(n_ctx, n_ctx + n_lat))):
            xv = x_ref[...]
            y = b_ref[...] + jnp.zeros_like(xv)
            for j in range(CONV_W):
                y = y + _shift_rows(xv, j - CONV_W // 2) * w_ref[j:j + 1, :]
            y_ref[rows, :] = y

    return pl.pallas_call(
        body, name="conv_fwd", grid=(D_RNN // CONV_COLS,),
        in_specs=[pl.BlockSpec((n_lat, CONV_COLS), lambda i: (0, blk0 + i)), pl.BlockSpec((n_ctx, CONV_COLS), lambda i: (0, blk0 + i)),
                  pl.BlockSpec((CONV_W, CONV_COLS), lambda i: (0, i)), pl.BlockSpec((1, CONV_COLS), lambda i: (0, i))],
        out_specs=pl.BlockSpec((n_ctx + n_lat, CONV_COLS), lambda i: (0, i)),
        out_shape=jax.ShapeDtypeStruct((n_ctx + n_lat, D_RNN), F32), compiler_params=_params(("parallel",)),
    )(proj_l, proj_c, w, b)


def _conv_bwd(d1, d2, proj_l, proj_c, w):
    n_lat, n_ctx = proj_l.shape[0], proj_c.shape[0]
    blk0 = XR_COL0 // CONV_COLS

    def body(d1_ref, d2_ref, xl_ref, xc_ref, w_ref, dxl_ref, dxc_ref, dw_ref, db_ref):
        dw = [0.0] * CONV_W
        db = 0.0
        for x_ref, dx_ref, rows in ((xc_ref, dxc_ref, slice(0, n_ctx)), (xl_ref, dxl_ref, slice(n_ctx, n_ctx + n_lat))):
            dv = d1_ref[rows, :] + d2_ref[rows, :]
            xv = x_ref[...]
            dx = jnp.zeros_like(dv)
            for j in range(CONV_W):
                off = j - CONV_W // 2
                dx = dx + _shift_rows(dv, -off) * w_ref[j:j + 1, :]
                dw[j] = dw[j] + jnp.sum(dv * _shift_rows(xv, off), axis=0, keepdims=True)
            dx_ref[...] = dx.astype(dx_ref.dtype)
            db = db + jnp.sum(dv, axis=0, keepdims=True)
        for j in range(CONV_W):
            dw_ref[j:j + 1, :] = dw[j]
        db_ref[...] = db

    both = pl.BlockSpec((n_ctx + n_lat, CONV_COLS), lambda i: (0, i))
    return pl.pallas_call(
        body, name="conv_bwd", grid=(D_RNN // CONV_COLS,),
        in_specs=[both, both, pl.BlockSpec((n_lat, CONV_COLS), lambda i: (0, blk0 + i)),
                  pl.BlockSpec((n_ctx, CONV_COLS), lambda i: (0, blk0 + i)), pl.BlockSpec((CONV_W, CONV_COLS), lambda i: (0, i))],
        out_specs=(pl.BlockSpec((n_lat, CONV_COLS), lambda i: (0, i)), pl.BlockSpec((n_ctx, CONV_COLS), lambda i: (0, i)),
                   pl.BlockSpec((CONV_W, CONV_COLS), lambda i: (0, i)), pl.BlockSpec((1, CONV_COLS), lambda i: (0, i))),
        out_shape=(jax.ShapeDtypeStruct((n_lat, D_RNN), BF16), jax.ShapeDtypeStruct((n_ctx, D_RNN), BF16),
                   jax.ShapeDtypeStruct((CONV_W, D_RNN), F32), jax.ShapeDtypeStruct((1, D_RNN), F32)),
        compiler_params=_params(("parallel",)),
    )(d1, d2, proj_l, proj_c, w)


RNN_TB = 256
SCAN_ROWS = 8


def _sigmoid(z):
    return 1.0 / (1.0 + jnp.exp(-z))


def _softplus(z):
    return jnp.maximum(z, 0.0) + jnp.log(1.0 + jnp.exp(-jnp.abs(z)))


def _one_minus_exp(y):
    series = -y * (1.0 + y * (0.5 + y * (1.0 / 6.0 + y * (1.0 / 24.0))))
    return jnp.where(y > -0.03, series, 1.0 - jnp.exp(y))


def _rglru_gates(xv, wa_ref, ba_ref, wx_ref, bx_ref, lam_ref):
    xb = xv.astype(BF16)
    zr = jnp.concatenate([jnp.dot(xb[:, n * RNN_BW:(n + 1) * RNN_BW], wa_ref[n].astype(BF16),
                                  preferred_element_type=F32) for n in range(RNN_BLOCKS)], axis=-1) + ba_ref[...]
    zi = jnp.concatenate([jnp.dot(xb[:, n * RNN_BW:(n + 1) * RNN_BW], wx_ref[n].astype(BF16),
                                  preferred_element_type=F32) for n in range(RNN_BLOCKS)], axis=-1) + bx_ref[...]
    r = _sigmoid(zr)
    gi = _sigmoid(zi)
    sp = _softplus(-lam_ref[...])
    log_a = -RG_C * r * sp
    a = jnp.exp(log_a)
    s = jnp.sqrt(_one_minus_exp(2.0 * log_a))
    return r, gi, sp, a, s


def _scan_rows(n_rows, reverse, step_fn, carry):
    groups = n_rows // SCAN_ROWS

    def trip(gidx, carry):
        gi = (groups - 1 - gidx) if reverse else gidx
        base = pl.multiple_of(gi * SCAN_ROWS, SCAN_ROWS)
        return step_fn(base, carry)

    return lax.fori_loop(0, groups, trip, carry)


def _scan_block_order(nb, nb_c, reverse, adjoint):
    if not reverse:
        return (lambda i: nb - 1 - i) if adjoint else (lambda i: i)
    if adjoint:
        return lambda i: jnp.where(i < nb - nb_c, nb_c + i, i - (nb - nb_c))
    return lambda i: jnp.where(i < nb_c, nb_c - 1 - i, nb + nb_c - 1 - i)


def _rglru_fwd(xs, wa, ba, wx, bx, lam, *, reverse, n_ctx, name):
    rows = xs.shape[0]
    tb = _tile(math.gcd(rows, n_ctx), RNN_TB, SCAN_ROWS)
    nb = rows // tb
    block_of = _scan_block_order(nb, n_ctx // tb, reverse, False)
    order = lambda i: (block_of(i), 0)

    def body(x_ref, wa_ref, ba_ref, wx_ref, bx_ref, lam_ref, h_ref, hp_ref, a_s, b_s, state):
        @pl.when(pl.program_id(0) == 0)
        def _():
            state[...] = jnp.zeros_like(state)

        xv = x_ref[...]
        _, gi, _, a, s = _rglru_gates(xv, wa_ref, ba_ref, wx_ref, bx_ref, lam_ref)
        a_s[...] = a
        b_s[...] = s * (gi * xv)

        def group(base, h):
            av = a_s[pl.ds(base, SCAN_ROWS), :]
            bv = b_s[pl.ds(base, SCAN_ROWS), :]
            outs, prevs = [None] * SCAN_ROWS, [None] * SCAN_ROWS
            for k in range(SCAN_ROWS):
                r_ = SCAN_ROWS - 1 - k if reverse else k
                prevs[r_] = h
                h = av[r_:r_ + 1, :] * h + bv[r_:r_ + 1, :]
                outs[r_] = h
            h_ref[pl.ds(base, SCAN_ROWS), :] = jnp.concatenate(outs, axis=0)
            hp_ref[pl.ds(base, SCAN_ROWS), :] = jnp.concatenate(prevs, axis=0)
            return h

        state[0:1, :] = _scan_rows(tb, reverse, group, state[0:1, :])

    blk = pl.BlockSpec((tb, D_RNN), order)
    wspec = _full((RNN_BLOCKS, RNN_BW, RNN_BW))
    vec = _full((1, D_RNN))
    return pl.pallas_call(
        body, name=name, grid=(nb,), in_specs=[blk, wspec, vec, wspec, vec, vec], out_specs=(blk, blk),
        out_shape=(jax.ShapeDtypeStruct((rows, D_RNN), F32), jax.ShapeDtypeStruct((rows, D_RNN), F32)),
        scratch_shapes=[pltpu.VMEM((tb, D_RNN), F32), pltpu.VMEM((tb, D_RNN), F32), pltpu.VMEM((SCAN_ROWS, D_RNN), F32)],
        compiler_params=_params(("arbitrary",)),
    )(xs, wa, ba, wx, bx, lam)


def _rglru_bwd(xs, h_prev, dh, wa, ba, wx, bx, lam, *, reverse, n_ctx, name):
    rows = xs.shape[0]
    tb = _tile(math.gcd(rows, n_ctx), RNN_TB, SCAN_ROWS)
    nb, nb_c = rows // tb, n_ctx // tb
    back = not reverse
    block_of = _scan_block_order(nb, nb_c, reverse, True)
    order = lambda i: (block_of(i), 0)

    def body(x_ref, hp_ref, dh_ref, wa_ref, ba_ref, wx_ref, bx_ref, lam_ref,
             dx_ref, dwa_ref, dba_ref, dwx_ref, dbx_ref, dlam_ref, a_s, g_s, state):
        @pl.when(pl.program_id(0) == 0)
        def _():
            state[...] = jnp.zeros_like(state)
            dwa_ref[...] = jnp.zeros_like(dwa_ref)
            dwx_ref[...] = jnp.zeros_like(dwx_ref)
            dba_ref[...] = jnp.zeros_like(dba_ref)
            dbx_ref[...] = jnp.zeros_like(dbx_ref)
            dlam_ref[...] = jnp.zeros_like(dlam_ref)

        xv = x_ref[...]
        r, gi, sp, a, s = _rglru_gates(xv, wa_ref, ba_ref, wx_ref, bx_ref, lam_ref)
        a_s[...] = a

        is_latent = block_of(pl.program_id(0)) >= nb_c

        def group(base, carry):
            av = a_s[pl.ds(base, SCAN_ROWS), :]
            dv = jnp.where(is_latent, dh_ref[pl.ds(base, SCAN_ROWS), :], 0.0)
            outs = [None] * SCAN_ROWS
            for k in range(SCAN_ROWS):
                r_ = SCAN_ROWS - 1 - k if back else k
                gt = dv[r_:r_ + 1, :] + carry
                outs[r_] = gt
                carry = av[r_:r_ + 1, :] * gt
            g_s[pl.ds(base, SCAN_ROWS), :] = jnp.concatenate(outs, axis=0)
            return carry

        state[0:1, :] = _scan_rows(tb, back, group, state[0:1, :])

        gv = g_s[...]
        d_a = gv * hp_ref[...]
        d_s = gv * (gi * xv)
        d_gi = gv * (s * xv)
        dx = gv * (s * gi)
        d_log_a = d_a * a - d_s * (a * a) / s
        d_r = d_log_a * (-RG_C * sp)
        lamv = lam_ref[...]
        d_sp = jnp.sum(d_log_a * (-RG_C * r), axis=0, keepdims=True)
        dlam_ref[...] += d_sp * (-_sigmoid(-lamv))
        d_zr = d_r * r * (1.0 - r)
        d_zi = d_gi * gi * (1.0 - gi)
        dba_ref[...] += jnp.sum(d_zr, axis=0, keepdims=True)
        dbx_ref[...] += jnp.sum(d_zi, axis=0, keepdims=True)
        xb = xv.astype(BF16)
        zrb, zib = d_zr.astype(BF16), d_zi.astype(BF16)
        parts = []
        for n in range(RNN_BLOCKS):
            cols = slice(n * RNN_BW, (n + 1) * RNN_BW)
            dwa_ref[n] += lax.dot_general(xb[:, cols], zrb[:, cols], (((0,), (0,)), ((), ())), preferred_element_type=F32)
            dwx_ref[n] += lax.dot_general(xb[:, cols], zib[:, cols], (((0,), (0,)), ((), ())), preferred_element_type=F32)
            parts.append(
                lax.dot_general(zrb[:, cols], wa_ref[n].astype(BF16), (((1,), (1,)), ((), ())), preferred_element_type=F32)
                + lax.dot_general(zib[:, cols], wx_ref[n].astype(BF16), (((1,), (1,)), ((), ())), preferred_element_type=F32))
        dx_ref[...] = dx + jnp.concatenate(parts, axis=-1)

    blk = pl.BlockSpec((tb, D_RNN), order)
    wspec = _full((RNN_BLOCKS, RNN_BW, RNN_BW))
    vec = _full((1, D_RNN))
    wshape = jax.ShapeDtypeStruct((RNN_BLOCKS, RNN_BW, RNN_BW), F32)
    vshape = jax.ShapeDtypeStruct((1, D_RNN), F32)
    dh_blk = pl.BlockSpec((tb, D_RNN), lambda i: (jnp.maximum(block_of(i) - nb_c, 0), 0))
    return pl.pallas_call(
        body, name=name, grid=(nb,), in_specs=[blk, blk, dh_blk, wspec, vec, wspec, vec, vec],
        out_specs=(blk, wspec, vec, wspec, vec, vec),
        out_shape=(jax.ShapeDtypeStruct((rows, D_RNN), F32), wshape, vshape, wshape, vshape, vshape),
        scratch_shapes=[pltpu.VMEM((tb, D_RNN), F32), pltpu.VMEM((tb, D_RNN), F32), pltpu.VMEM((SCAN_ROWS, D_RNN), F32)],
        compiler_params=_params(("arbitrary",)),
    )(xs, h_prev, dh, wa, ba, wx, bx, lam)


def _assemble_d_proj(dp_qk_l, dp_qk_c, dv_all, d_xr_l, d_xr_c, d_gate):
    n_lat, n_ctx = dp_qk_l.shape[0], dp_qk_c.shape[0]
    tr = _tile(math.gcd(n_lat, n_ctx), 256, 16)
    nb_l, nb_c = n_lat // tr, n_ctx // tr
    w_qk = ATTN_W + KV_W

    def body(ql_ref, qc_ref, dv_ref, xl_ref, xc_ref, g_ref, o_ref):
        i = pl.program_id(0)
        o_ref[:, w_qk:XR_COL0] = dv_ref[...].astype(o_ref.dtype)

        @pl.when(i < nb_l)
        def _():
            o_ref[:, :w_qk] = ql_ref[...]
            o_ref[:, XR_COL0:GATE_COL0] = xl_ref[...]
            o_ref[:, GATE_COL0:] = g_ref[...]

        @pl.when(i >= nb_l)
        def _():
            o_ref[:, :w_qk] = qc_ref[...]
            o_ref[:, XR_COL0:GATE_COL0] = xc_ref[...]
            o_ref[:, GATE_COL0:] = jnp.zeros((tr, D_RNN), o_ref.dtype)

    lat = lambda i: (jnp.minimum(i, nb_l - 1), 0)
    ctx = lambda i: (jnp.maximum(i - nb_l, 0), 0)
    return pl.pallas_call(
        body, name="assemble_d_proj", grid=(nb_l + nb_c,),
        in_specs=[pl.BlockSpec((tr, w_qk), lat), pl.BlockSpec((tr, w_qk), ctx),
                  pl.BlockSpec((tr, KV_W), lambda i: (jnp.where(i < nb_l, i + nb_c, i - nb_l), 0)),
                  pl.BlockSpec((tr, D_RNN), lat), pl.BlockSpec((tr, D_RNN), ctx), pl.BlockSpec((tr, D_RNN), lat)],
        out_specs=pl.BlockSpec((tr, GATE_COL0 + D_RNN), lambda i: (i, 0)),
        out_shape=jax.ShapeDtypeStruct((n_lat + n_ctx, GATE_COL0 + D_RNN), BF16),
        compiler_params=_params(("parallel",)),
    )(dp_qk_l, dp_qk_c, dv_all, d_xr_l, d_xr_c, d_gate)


def _gelu(z):
    return 0.5 * z * (1.0 + jnp.tanh(GELU_C * (z + 0.044715 * z * z * z)))


def _gelu_grad(z):
    t = jnp.tanh(GELU_C * (z + 0.044715 * z * z * z))
    return 0.5 * (1.0 + t) + 0.5 * z * (1.0 - t * t) * (GELU_C * (1.0 + 3.0 * 0.044715 * z * z))


GATE_COL0 = XR_COL0 + D_RNN


RNN_OUT_COLS = 512


def _rnn_out_specs(rows, hf_off, hb_off):
    tr = _tile(rows, 256, 16)
    assert hf_off % tr == 0 and hb_off % tr == 0 and GATE_COL0 % RNN_OUT_COLS == 0
    fo, bo, go = hf_off // tr, hb_off // tr, GATE_COL0 // RNN_OUT_COLS
    hf_spec = pl.BlockSpec((tr, RNN_OUT_COLS), lambda i, j: (i + fo, j))
    hb_spec = pl.BlockSpec((tr, RNN_OUT_COLS), lambda i, j: (i + bo, j))
    gate_spec = pl.BlockSpec((tr, RNN_OUT_COLS), lambda i, j: (i, j + go))
    out_spec = pl.BlockSpec((tr, RNN_OUT_COLS), lambda i, j: (i, j))
    return (rows // tr, D_RNN // RNN_OUT_COLS), hf_spec, hb_spec, gate_spec, out_spec


def _rnn_out_fwd(hf, hb, proj, hf_off, hb_off, cat):
    rows = proj.shape[0]
    grid, hf_spec, hb_spec, gate_spec, out_spec = _rnn_out_specs(rows, hf_off, hb_off)
    tr, col0 = out_spec.block_shape[0], ATTN_W // RNN_OUT_COLS

    def body(hf_ref, hb_ref, g_ref, _, o_ref):
        o_ref[...] = ((hf_ref[...] + hb_ref[...]) * _gelu(g_ref[...])).astype(o_ref.dtype)

    return pl.pallas_call(
        body, name="rnn_out_fwd", grid=grid, in_specs=[hf_spec, hb_spec, gate_spec, ANY],
        out_specs=pl.BlockSpec((tr, RNN_OUT_COLS), lambda i, j: (i, j + col0)),
        out_shape=jax.ShapeDtypeStruct(cat.shape, cat.dtype), input_output_aliases={3: 0},
        compiler_params=_params(("parallel", "parallel")),
    )(hf, hb, proj, cat)


def _rnn_out_bwd(d_cat, hf, hb, proj, hf_off, hb_off):
    rows = proj.shape[0]
    grid, hf_spec, hb_spec, gate_spec, out_spec = _rnn_out_specs(rows, hf_off, hb_off)
    do = ATTN_W // RNN_OUT_COLS

    def body(d_ref, hf_ref, hb_ref, g_ref, dh_ref, dg_ref):
        dv, gv = d_ref[...].astype(F32), g_ref[...]
        dh_ref[...] = dv * _gelu(gv)
        dg_ref[...] = (dv * (hf_ref[...] + hb_ref[...]) * _gelu_grad(gv)).astype(dg_ref.dtype)

    tr = out_spec.block_shape[0]
    return pl.pallas_call(
        body, name="rnn_out_bwd", grid=grid,
        in_specs=[pl.BlockSpec((tr, RNN_OUT_COLS), lambda i, j: (i, j + do)), hf_spec, hb_spec, gate_spec],
        out_specs=(out_spec, out_spec),
        out_shape=(jax.ShapeDtypeStruct((rows, D_RNN), F32), jax.ShapeDtypeStruct((rows, D_RNN), BF16)),
        compiler_params=_params(("parallel", "parallel")),
    )(d_cat, hf, hb, proj)


def _gmlp_parts(z_ref, vg_ref, vb_ref, d_gm):
    zu, zv = z_ref[:, :d_gm], z_ref[:, d_gm:]
    u = _gelu(zu)
    v = _gelu(zv)
    mu = jnp.mean(v, axis=-1, keepdims=True)
    vc = v - mu
    rstd = lax.rsqrt(jnp.mean(vc * vc, axis=-1, keepdims=True) + EPS)
    vhat = vc * rstd
    vn = vhat * vg_ref[...] + vb_ref[...]
    return zu, zv, u, vhat, rstd, vn


def _gmlp_fwd(z, v_g, v_b, w_sp, b_sp_t):
    rows, d_gm = z.shape[0], z.shape[1] // 2
    tr = _tile(rows, 256, CHUNK)
    gwid = d_gm // GM_GROUPS

    def body(z_ref, vg_ref, vb_ref, w_ref, b_ref, o_ref):
        _, _, u, _, _, vn = _gmlp_parts(z_ref, vg_ref, vb_ref, d_gm)
        vnb = vn.astype(BF16)
        for g in range(GM_GROUPS):
            wg = w_ref[g].astype(BF16)
            for c in range(tr // CHUNK):
                rs, cs = slice(c * CHUNK, (c + 1) * CHUNK), slice(g * gwid, (g + 1) * gwid)
                sv = jnp.dot(wg, vnb[rs, cs], preferred_element_type=F32) + b_ref[:, g:g + 1]
                o_ref[rs, cs] = (u[rs, cs] * sv).astype(o_ref.dtype)

    return pl.pallas_call(
        body, name="gmlp_fwd", grid=(rows // tr,),
        in_specs=[pl.BlockSpec((tr, 2 * d_gm), lambda i: (i, 0)), _full((1, d_gm)), _full((1, d_gm)),
                  _full(w_sp.shape), _full(b_sp_t.shape)],
        out_specs=pl.BlockSpec((tr, d_gm), lambda i: (i, 0)),
        out_shape=jax.ShapeDtypeStruct((rows, d_gm), BF16), compiler_params=_params(("parallel",)),
    )(z, v_g, v_b, w_sp, b_sp_t)


def _gmlp_bwd(z, dgate, v_g, v_b, w_sp, b_sp_t):
    rows, d_gm = z.shape[0], z.shape[1] // 2
    tr = _tile(rows, 256, CHUNK)
    gwid = d_gm // GM_GROUPS

    def body(z_ref, dg_ref, vg_ref, vb_ref, w_ref, b_ref, dz_ref, dbin_ref, dvg_ref, dvb_ref, dw_ref, dbs_ref, dvn_s):
        @pl.when(pl.program_id(0) == 0)
        def _():
            dbin_ref[...] = jnp.zeros_like(dbin_ref)
            dvg_ref[...] = jnp.zeros_like(dvg_ref)
            dvb_ref[...] = jnp.zeros_like(dvb_ref)
            dw_ref[...] = jnp.zeros_like(dw_ref)
            dbs_ref[...] = jnp.zeros_like(dbs_ref)

        zu, zv, u, vhat, rstd, vn = _gmlp_parts(z_ref, vg_ref, vb_ref, d_gm)
        vnb = vn.astype(BF16)
        dgv = dg_ref[...].astype(F32)
        dsv = dgv * u
        dsvb = dsv.astype(BF16)
        for g in range(GM_GROUPS):
            wg = w_ref[g].astype(BF16)
            cs = slice(g * gwid, (g + 1) * gwid)
            for c in range(tr // CHUNK):
                rs = slice(c * CHUNK, (c + 1) * CHUNK)
                sv = jnp.dot(wg, vnb[rs, cs], preferred_element_type=F32) + b_ref[:, g:g + 1]
                dz_ref[rs, cs] = (dgv[rs, cs] * sv * _gelu_grad(zu[rs, cs])).astype(dz_ref.dtype)
                dw_ref[g] += lax.dot_general(dsvb[rs, cs], vnb[rs, cs], (((1,), (1,)), ((), ())),
                                             preferred_element_type=F32)
                dbs_ref[:, g:g + 1] += jnp.sum(dsv[rs, cs], axis=-1, keepdims=True)
                dvn_s[rs, cs] = lax.dot_general(wg, dsvb[rs, cs], (((0,), (0,)), ((), ())), preferred_element_type=F32)
        dvn = dvn_s[...]
        dvg_ref[...] += jnp.sum(dvn * vhat, axis=0, keepdims=True)
        dvb_ref[...] += jnp.sum(dvn, axis=0, keepdims=True)
        dvh = dvn * vg_ref[...]
        dv = rstd * (dvh - jnp.mean(dvh, axis=-1, keepdims=True) - vhat * jnp.mean(dvh * vhat, axis=-1, keepdims=True))
        dzv = dv * _gelu_grad(zv)
        dz_ref[:, d_gm:] = dzv.astype(dz_ref.dtype)
        dbin_ref[:, d_gm:] += jnp.sum(dzv, axis=0, keepdims=True)
        dbin_ref[:, :d_gm] += jnp.sum(dz_ref[:, :d_gm].astype(F32), axis=0, keepdims=True)

    return pl.pallas_call(
        body, name="gmlp_bwd", grid=(rows // tr,),
        in_specs=[pl.BlockSpec((tr, 2 * d_gm), lambda i: (i, 0)), pl.BlockSpec((tr, d_gm), lambda i: (i, 0)),
                  _full((1, d_gm)), _full((1, d_gm)), _full(w_sp.shape), _full(b_sp_t.shape)],
        out_specs=(pl.BlockSpec((tr, 2 * d_gm), lambda i: (i, 0)), _full((1, 2 * d_gm)), _full((1, d_gm)),
                   _full((1, d_gm)), _full(w_sp.shape), _full(b_sp_t.shape)),
        out_shape=(jax.ShapeDtypeStruct((rows, 2 * d_gm), BF16), jax.ShapeDtypeStruct((1, 2 * d_gm), F32),
                   jax.ShapeDtypeStruct((1, d_gm), F32), jax.ShapeDtypeStruct((1, d_gm), F32),
                   jax.ShapeDtypeStruct(w_sp.shape, F32), jax.ShapeDtypeStruct(b_sp_t.shape, F32)),
        scratch_shapes=[pltpu.VMEM((tr, d_gm), F32)],
        compiler_params=_params(("arbitrary",)),
    )(z, dgate, v_g, v_b, w_sp, b_sp_t)


def _adamw_math(w, g, m, v):
    m = ADAM_B1 * m + (1.0 - ADAM_B1) * g
    v = ADAM_B2 * v + (1.0 - ADAM_B2) * (g * g)
    m_hat = m / (1.0 - ADAM_B1 ** ADAM_STEP)
    v_hat = v / (1.0 - ADAM_B2 ** ADAM_STEP)
    delta = -ADAM_LR * (m_hat / (jnp.sqrt(v_hat) + ADAM_EPS) + ADAM_WD * w)
    return delta, m, v


def _adamw(w, g, m, v, name, rewrite_grad=False):
    shape = w.shape
    if rewrite_grad:
        outs = _rowwise(lambda w_, g_, m_, v_: (g_,) + _adamw_math(w_, g_, m_, v_), (F32,) * 4, _as2d(w), _as2d(g), _as2d(m),
                        _as2d(v), name=name)
        return tuple(o.reshape(shape) for o in outs)
    outs = _rowwise(_adamw_math, (F32, F32, F32), _as2d(w), _as2d(g), _as2d(m), _as2d(v), name=name)
    return (g.reshape(shape),) + tuple(o.reshape(shape) for o in outs)


PACK_COLS = 1024


def _pack(arrays, dtype=F32):
    flat = jnp.concatenate([a.reshape(-1).astype(dtype) for a in arrays])
    pad = (-flat.size) % (16 * PACK_COLS)
    return jnp.pad(flat, (0, pad)).reshape(-1, PACK_COLS)


def _into_slot(pack, dev, name):
    rows, cols = pack.shape
    tr = _rows_tile(rows, cols, budget=512 * 1024)

    def body(dev_ref, p_ref, o_ref):
        o_ref[...] = p_ref[...]

    return pl.pallas_call(
        body, name=name, out_shape=jax.ShapeDtypeStruct((N_DEV, rows, cols), pack.dtype),
        grid_spec=pltpu.PrefetchScalarGridSpec(
            num_scalar_prefetch=1, grid=(rows // tr,), in_specs=[pl.BlockSpec((tr, cols), lambda i, dv: (i, 0))],
            out_specs=pl.BlockSpec((None, tr, cols), lambda i, dv: (dv[0], i, 0))),
        compiler_params=_params(("parallel",)),
    )(dev, pack)


def _unpack(flat, shapes):
    out, pos = [], 0
    for shp in shapes:
        n = math.prod(shp)
        out.append(flat[pos:pos + n].reshape(shp))
        pos += n
    return out


def _unpack_devices(packed8, shapes):
    flat8 = packed8.reshape(N_DEV, -1)
    out, pos = [], 0
    for shp in shapes:
        n = math.prod(shp)
        out.append(flat8[:, pos:pos + n].reshape((N_DEV,) + tuple(shp)))
        pos += n
    return out


def _sum_devices(g8):
    _, rows, cols = g8.shape
    tr = _rows_tile(rows, cols, budget=256 * 1024)

    def body(g_ref, o_ref):
        acc = g_ref[0].astype(F32)
        for d in range(1, N_DEV):
            acc = acc + g_ref[d].astype(F32)
        o_ref[...] = acc

    return pl.pallas_call(
        body, name="sum_devices", grid=(rows // tr,), in_specs=[pl.BlockSpec((N_DEV, tr, cols), lambda i: (0, i, 0))],
        out_specs=pl.BlockSpec((tr, cols), lambda i: (i, 0)), out_shape=jax.ShapeDtypeStruct((rows, cols), F32),
        compiler_params=_params(("parallel",)),
    )(g8)


def _place():
    return lax.axis_index("x"), lax.axis_index("y"), lax.axis_index("c")


def _other_chips(x, y):
    return [(1 - x, y), (x, 1 - y), (1 - x, 1 - y)]


def _remote(src, dst, send_sem, recv_sem, to):
    return pltpu.make_async_remote_copy(src_ref=src, dst_ref=dst, send_sem=send_sem, recv_sem=recv_sem, device_id=to,
                                        device_id_type=MESH)


def _comm_call(body, name, operands, out_shapes, n_remote, n_local, aliases=None):
    return pl.pallas_call(
        body, name=name, out_shape=tuple(out_shapes), in_specs=[ANY] * len(operands), out_specs=tuple(ANY for _ in out_shapes),
        scratch_shapes=[pltpu.SemaphoreType.DMA((n_remote,)), pltpu.SemaphoreType.DMA((n_remote,)),
                        pltpu.SemaphoreType.DMA((max(n_local, 1),))],
        input_output_aliases=aliases or {},
    )(*operands)


def _in_place(arrays):
    return [jax.ShapeDtypeStruct(a.shape, a.dtype) for a in arrays], {i: i for i in range(len(arrays))}


def _allgather8(arrs, name):
    n = len(arrs)

    def body(*refs):
        ins, outs = refs[:n], refs[n:2 * n]
        send, recv, lsem = refs[2 * n:]
        x, y, c = _place()
        me, sib = (x, y, c), (x, y, 1 - c)
        chips = _other_chips(x, y)

        def slot(t, px, py, pc):
            return outs[t].at[4 * px + 2 * py + pc]

        def cp(t, k, block, to, from_input=False):
            src = ins[t] if from_input else slot(t, *block)
            return _remote(src, slot(t, *block), send.at[7 * t + k], recv.at[7 * t + k], to)

        mine = [pltpu.make_async_copy(ins[t], slot(t, *me), lsem.at[t]) for t in range(n)]
        for cpy in mine:
            cpy.start()
        first = []
        for t in range(n):
            first.append(cp(t, 0, me, sib, True))
            first += [cp(t, 1 + j, me, (*chip, c), True) for j, chip in enumerate(chips)]
        for cpy in first:
            cpy.start()
        passed = []
        for t in range(n):
            for j, chip in enumerate(chips):
                cp(t, 1 + j, (*chip, c), me).wait_recv()
                fwd = cp(t, 4 + j, (*chip, c), sib)
                fwd.start()
                passed.append(fwd)
        for t in range(n):
            cp(t, 0, sib, me).wait_recv()
            for j, chip in enumerate(chips):
                cp(t, 4 + j, (*chip, 1 - c), me).wait_recv()
        for cpy in first + passed:
            cpy.wait_send()
        for cpy in mine:
            cpy.wait()

    outs = _comm_call(body, name, arrs, [jax.ShapeDtypeStruct((N_DEV,) + a.shape, a.dtype) for a in arrs], 7 * n, n)
    return list(outs)


def _join_halves(bufs):
    n = len(bufs)
    units = [(k, layer) for k in range(n) for layer in range(bufs[k].shape[0])]

    def body(*refs):
        bufs_ = refs[n:2 * n]
        send, recv, _ = refs[2 * n:]
        x, y, c = _place()
        sent = []
        for u, (k, layer) in enumerate(units):
            half = bufs_[k].shape[1] // 2
            mine = bufs_[k].at[layer, pl.ds(c * half, half)]
            cpy = _remote(mine, mine, send.at[u], recv.at[u], (x, y, 1 - c))
            cpy.start()
            sent.append(cpy)
        for u, (k, layer) in enumerate(units):
            half = bufs_[k].shape[1] // 2
            theirs = bufs_[k].at[layer, pl.ds((1 - c) * half, half)]
            _remote(theirs, theirs, send.at[u], recv.at[u], (x, y, c)).wait_recv()
        for cpy in sent:
            cpy.wait_send()

    shapes, aliases = _in_place(bufs)
    return list(_comm_call(body, "join_halves", bufs, shapes, len(units), 0, aliases))


def _add_halves(grad, other, place):
    _, rows, cols = grad.shape
    half = rows // 2
    tr = _rows_tile(half, cols, itemsize=2, budget=2 * 1024 * 1024)
    per_half = half // tr

    def body(place_ref, g_ref, o_ref, s_ref):
        s_ref[...] = (g_ref[...].astype(F32) + o_ref[...].astype(F32)).astype(s_ref.dtype)

    return pl.pallas_call(
        body, name="add_halves", out_shape=jax.ShapeDtypeStruct((N_CHIPS, half, cols), grad.dtype),
        grid_spec=pltpu.PrefetchScalarGridSpec(
            num_scalar_prefetch=1, grid=(N_CHIPS, per_half),
            in_specs=[pl.BlockSpec((None, tr, cols), lambda k, i, pr: (k, pr[1] * per_half + i, 0)),
                      pl.BlockSpec((None, tr, cols), lambda k, i, pr: (k, i, 0))],
            out_specs=pl.BlockSpec((None, tr, cols), lambda k, i, pr: (k, i, 0))),
        compiler_params=_params(("parallel", "parallel")),
    )(place, grad, other)


def _add_chips(sums, others, place, dest, layer, n_layers):
    _, half, cols = sums.shape
    tr = _rows_tile(half, cols, itemsize=4, budget=2 * 1024 * 1024)
    per_half = half // tr

    def body(place_ref, s_ref, o_ref, *rest):
        acc = s_ref[...].astype(F32)
        for j in range(N_CHIPS - 1):
            acc = acc + o_ref[j].astype(F32)
        rest[-1][...] = acc

    operands = [place, sums, others] + ([] if dest is None else [dest])
    return pl.pallas_call(
        body, name="add_chips", out_shape=jax.ShapeDtypeStruct((n_layers, 2 * half, cols), F32),
        grid_spec=pltpu.PrefetchScalarGridSpec(
            num_scalar_prefetch=1, grid=(per_half,),
            in_specs=[pl.BlockSpec((None, tr, cols), lambda i, pr: (pr[0], i, 0)),
                      pl.BlockSpec((N_CHIPS - 1, tr, cols), lambda i, pr: (0, i, 0))] + ([] if dest is None else [ANY]),
            out_specs=pl.BlockSpec((None, tr, cols), lambda i, pr: (layer, pr[1] * per_half + i, 0))),
        input_output_aliases={} if dest is None else {3: 0},
        compiler_params=_params(("parallel",)),
    )(*operands)


HBM = pl.BlockSpec(memory_space=pltpu.HBM)
SEM = pl.BlockSpec(memory_space=pltpu.SEMAPHORE)
DATAFLOW = pltpu.SideEffectType.DATAFLOW_SIDE_EFFECTING


def _split_start(name, bufs, copies, n_copies, after=None):
    n = len(bufs)
    extra = 0 if after is None else 1

    def body(*refs):
        for cpy in copies(refs[:n], refs[n + extra], refs[n + extra + 1]):
            cpy.start()
        refs[-1][...] = jnp.zeros_like(refs[-1])

    outs = pl.pallas_call(
        body, name=name,
        out_shape=(pltpu.SemaphoreType.DMA((n_copies,)), pltpu.SemaphoreType.DMA((n_copies,)),
                   *[pltpu.HBM(b.shape, b.dtype) for b in bufs], jax.ShapeDtypeStruct((8, LANES), F32)),
        in_specs=[HBM] * n + [ANY] * extra,
        out_specs=(SEM, SEM, *[HBM] * n, pl.BlockSpec(memory_space=pltpu.VMEM)),
        input_output_aliases={i: 2 + i for i in range(n)},
        compiler_params=pltpu.CompilerParams(has_side_effects=DATAFLOW),
    )(*[pltpu.with_memory_space_constraint(b, pltpu.HBM) for b in bufs], *([] if after is None else [after]))
    return outs[0], outs[1], list(outs[2:2 + n]), outs[-1]


def _split_wait(name, bufs, send, recv, copies, after):
    n = len(bufs)

    def body(*refs):
        for cpy in copies(refs[:n], refs[n], refs[n + 1]):
            cpy.wait_send()
            cpy.wait_recv()

    return list(pl.pallas_call(
        body, name=name, out_shape=tuple(pltpu.HBM(b.shape, b.dtype) for b in bufs),
        in_specs=[HBM] * n + [SEM, SEM, ANY], out_specs=tuple([HBM] * n),
        input_output_aliases={i: i for i in range(n)},
        compiler_params=pltpu.CompilerParams(has_side_effects=DATAFLOW),
    )(*bufs, send, recv, after))


def _gather_copies(bufs, send, recv):
    x, y, c = _place()
    out = []
    for u, buf in enumerate(bufs):
        half = buf.shape[1] // 2
        mine = buf.at[2 * x + y, pl.ds(c * half, half)]
        out += [_remote(mine, mine, send.at[3 * u + j], recv.at[3 * u + j], (*chip, c))
                for j, chip in enumerate(_other_chips(x, y))]
    return out


def _exchange_copies(bufs, send, recv):
    x, y, c = _place()
    n = len(bufs) // 2
    out = []
    for k in range(n):
        half = bufs[k].shape[1] // 2
        theirs = bufs[k].at[pl.ds(0, N_CHIPS), pl.ds((1 - c) * half, half)]
        out.append(_remote(theirs, bufs[n + k], send.at[k], recv.at[k], (x, y, 1 - c)))
    return out


def _all_to_all_copies(bufs, send, recv):
    x, y, c = _place()
    n = len(bufs) // 2
    return [_remote(bufs[k].at[2 * chip[0] + chip[1]], bufs[n + k].at[j], send.at[3 * k + j], recv.at[3 * k + j], (*chip, c))
            for k in range(n) for j, chip in enumerate(_other_chips(x, y))]


def _forward_copies(bufs, send, recv):
    x, y, c = _place()
    out = []
    for u, buf in enumerate(bufs):
        half = buf.shape[1] // 2
        for j, chip in enumerate(_other_chips(x, y)):
            landed = buf.at[2 * chip[0] + chip[1], pl.ds(c * half, half)]
            out.append(_remote(landed, landed, send.at[3 * u + j], recv.at[3 * u + j], (x, y, 1 - c)))
    return out


def _gather8_copies(bufs, send, recv):
    x, y, c = _place()
    targets = [(x, y, 1 - c)] + [(*chip, c) for chip in _other_chips(x, y)]
    out = []
    for b, buf in enumerate(bufs):
        mine = buf.at[4 * x + 2 * y + c]
        out += [_remote(mine, mine, send.at[N_CHIPS * b + k], recv.at[N_CHIPS * b + k], to) for k, to in enumerate(targets)]
    return out


def _forward_slots(bufs, name):
    n = len(bufs)

    def body(*refs):
        bufs_ = refs[n:2 * n]
        send, recv, _ = refs[2 * n:]
        x, y, c = _place()
        chips = _other_chips(x, y)
        sent = []
        for b in range(n):
            for j, chip in enumerate(chips):
                slot = bufs_[b].at[4 * chip[0] + 2 * chip[1] + c]
                cpy = _remote(slot, slot, send.at[3 * b + j], recv.at[3 * b + j], (x, y, 1 - c))
                cpy.start()
                sent.append(cpy)
        for b in range(n):
            for j, chip in enumerate(chips):
                slot = bufs_[b].at[4 * chip[0] + 2 * chip[1] + 1 - c]
                _remote(slot, slot, send.at[3 * b + j], recv.at[3 * b + j], (x, y, c)).wait_recv()
        for cpy in sent:
            cpy.wait_send()

    shapes, aliases = _in_place(bufs)
    return list(_comm_call(body, name, bufs, shapes, (N_CHIPS - 1) * n, 0, aliases))


FWD_GROUPS = {'mix': ('ar_out', 'ff_in0', 'ff_out0'), 'l1': ('gm_in', 'gm_out', 'ff_in1', 'ff_out1')}
GRAD_LAYOUT = {'ff_in0': (0, 0), 'ff_in1': (0, 1), 'ff_out0': (1, 0), 'ff_out1': (1, 1), 'ar_in': (2, 0), 'ar_out': (3, 0),
               'gm_in': (4, 0), 'gm_out': (5, 0)}


class _MeshLink:
    def __init__(self, place, shards):
        self.place = place
        self.ready = {'ar_in': shards['ar_in']}
        self.pending, after = {}, shards['ar_in']
        for group, names in FWD_GROUPS.items():
            send, recv, bufs, token = _split_start(f"gather_{group}_start", [shards[n] for n in names], _gather_copies,
                                                   3 * len(names), after)
            self.pending[group] = (names, send, recv, bufs)
            after = token
        self.start_token = after[0, 0]
        self.forwarding, self.exchanging, self.sent, self.last_token = {}, {}, {}, None

    def prefetch(self, group, after):
        names, send, recv, bufs = self.pending.pop(group)
        bufs = _split_wait(f"gather_{group}_wait", bufs, send, recv, _gather_copies, after)
        send, recv, bufs, token = _split_start(f"forward_{group}_start", bufs, _forward_copies, 3 * len(names))
        self.forwarding[group] = (names, send, recv, bufs)
        return token[0, 0]

    def weights(self, group, after):
        if group in self.forwarding:
            names, send, recv, bufs = self.forwarding.pop(group)
            self.ready.update(zip(names, _split_wait(f"forward_{group}_wait", bufs, send, recv, _forward_copies, after)))
        return self.ready

    def gradients(self, group, grads, after=None):
        tok = self.poll(next(iter(grads.values())))
        names, mine = list(grads), list(grads.values())
        landing = [lax.empty((N_CHIPS, g.shape[1] // 2, g.shape[2]), g.dtype) for g in mine]
        send, recv, bufs, token = _split_start(f"exchange_{group}_start", mine + landing, _exchange_copies, len(names), after)
        self.exchanging[group] = (names, send, recv, bufs)
        self.last_token = token
        return token[0, 0] + tok

    def poll(self, after):
        tok = 0.0
        for group in list(self.exchanging):
            names, send, recv, bufs = self.exchanging.pop(group)
            bufs = _split_wait(f"exchange_{group}_wait", bufs, send, recv, _exchange_copies, after)
            sums = [_add_halves(g, r, self.place) for g, r in zip(bufs[:len(names)], bufs[len(names):])]
            landing = [lax.empty((N_CHIPS - 1,) + s.shape[1:], s.dtype) for s in sums]
            send, recv, bufs, token = _split_start(f"grads_{group}_start", sums + landing, _all_to_all_copies, 3 * len(names))
            self.sent[group] = (names, send, recv, bufs)
            self.last_token = token
            tok = tok + token[0, 0]
        return tok

    def reduce(self, groups, after):
        units = {}
        for group in groups:
            names, send, recv, bufs = self.sent.pop(group)
            bufs = _split_wait(f"grads_{group}_wait", bufs, send, recv, _all_to_all_copies, after)
            units.update(zip(names, zip(bufs[:len(names)], bufs[len(names):])))
        n_layers = {p: 1 + max(l for pp, l in GRAD_LAYOUT.values() if pp == p) for p, _ in GRAD_LAYOUT.values()}
        out = {}
        for name, (p, layer) in GRAD_LAYOUT.items():
            if name in units:
                out[p] = _add_chips(*units[name], self.place, out.get(p), layer, n_layers[p])
        params = sorted(out)
        return dict(zip(params, _join_halves([out[p] for p in params])))


def _rope_tables(n):
    n_rows = n // GRID_W
    freqs = ROPE_THETA ** (-jnp.arange(ROPE_PAIRS, dtype=F32) / ROPE_PAIRS)
    ang_r = jnp.arange(n_rows, dtype=F32)[:, None] * freqs
    ang_c = jnp.arange(GRID_W, dtype=F32)[:, None] * freqs

    def per_token(of_row, of_col):
        r = jnp.broadcast_to(of_row[:, None, :], (n_rows, GRID_W, ROPE_PAIRS)).reshape(n, ROPE_PAIRS)
        c = jnp.broadcast_to(of_col[None, :, :], (n_rows, GRID_W, ROPE_PAIRS)).reshape(n, ROPE_PAIRS)
        return r, c

    cos_r, cos_c = per_token(jnp.cos(ang_r), jnp.cos(ang_c))
    sin_r, sin_c = per_token(jnp.sin(ang_r), jnp.sin(ang_c))
    cos = jnp.concatenate([cos_r, cos_r, cos_c, cos_c], axis=-1)
    sin = jnp.concatenate([-sin_r, sin_r, -sin_c, sin_c], axis=-1)
    return cos, sin


def _ffn_fwd(h2, w1, w2, tag):
    r, a = _matmul(h2, w1, kind='nn', b_split='n', out_dtype=BF16, epilogue='relu2', name=f"ffn_in_{tag}")
    f = _matmul(a, w2, kind='nn', b_split='k', out_dtype=F32, name=f"ffn_out_{tag}")
    return r, a, f


def _ffn_bwd(d_f, h2, r, a, w1, w2, tag):
    d_u = _matmul(d_f, w2, kind='nt', b_split='k', out_dtype=BF16, epilogue='times2x', extra=r, name=f"ffn_out_dx_{tag}")
    d_w2 = _matmul(a, d_f, kind='tn', out_split='k', out_dtype=BF16, name=f"ffn_out_dw_{tag}")
    d_w1 = _matmul(h2, d_u, kind='tn', out_split='n', out_dtype=BF16, name=f"ffn_in_dw_{tag}")
    d_h2 = _matmul(d_u, w1, kind='nt', b_split='n', out_dtype=F32, name=f"ffn_in_dx_{tag}")
    return d_h2, d_w1, d_w2


class _LocalLink:
    def __init__(self, big):
        self.big, self.grads, self.start_token = big, {}, 0.0

    def prefetch(self, group, after):
        return 0.0

    def poll(self, after):
        return 0.0

    def weights(self, group, after):
        return self.big

    def gradients(self, group, grads):
        self.grads.update(grads)
        return 0.0


def _local_step(xl0, xc0, target, ml, mc0, sp, link):
    n_lat, n_ctx = xl0.shape[0], xc0.shape[0]
    one = lambda v: 1.0 + v
    g = [[sp['norm_g'][i, k][None, :] for k in range(4)] for i in range(2)]

    sh1, sc1, gt1, sh2, sc2, gt2 = ml[0]
    big = link.weights('ar', None)
    sh1 = sh1 + link.start_token
    n_all = n_lat + n_ctx
    h_all = _norm_fwd(xl0, g[0][0], one(sc1), b=sh1, out_dtype=BF16, name="l0_mod1", into=(0, n_all, None))
    h_all = _norm_fwd(xc0, g[0][0], one(mc0[1]), b=mc0[0], out_dtype=BF16, name="l0_mod1_ctx", into=(n_lat, n_all, h_all))
    proj_l = _matmul(h_all, big['ar_in'], kind='nn', b_split='n', out_dtype=F32, a_rows=(0, n_lat), name="ar_in_lat")
    proj_c = _matmul(h_all, big['ar_in'], kind='nn', b_split='n', out_dtype=F32, a_rows=(n_lat, n_ctx), name="ar_in_ctx")
    cos_l, sin_l = _rope_tables(n_lat)
    cos_c, sin_c = jnp.ones((n_ctx, HEAD_DIM), F32), jnp.zeros((n_ctx, HEAD_DIM), F32)
    q_g, k_g = sp['q_g'], sp['k_g']
    _, k_all, v_all = _qk_fwd(proj_c, q_g, k_g, cos_c, sin_c, name="qk_fwd_ctx", kv_into=(0, n_all, None))
    q_l, k_all, v_all = _qk_fwd(proj_l, q_g, k_g, cos_l, sin_l, name="qk_fwd_lat", kv_into=(n_ctx, n_all, (k_all, v_all)))
    cat, lse = _attn_fwd(q_l, k_all, v_all)
    conv_b = sp['conv_b'] + link.prefetch('mix', cat)
    xs = _conv_fwd(proj_l, proj_c, sp['conv_w'], conv_b)
    rnn_w = [(sp['wa'][d], sp['ba'][d][None, :], sp['wx'][d], sp['bx'][d][None, :], sp['lam'][d][None, :]) for d in range(2)]
    h_f, hp_f = _rglru_fwd(xs, *rnn_w[0], reverse=False, n_ctx=n_ctx, name="rglru_fwd_f")
    h_r, hp_r = _rglru_fwd(xs, *rnn_w[1], reverse=True, n_ctx=n_ctx, name="rglru_fwd_r")
    cat = _rnn_out_fwd(h_f, h_r, proj_l, n_ctx, n_ctx, cat)
    w_mix = link.weights('mix', cat)
    ol0 = _matmul(cat, w_mix['ar_out'], kind='nn', b_split='k', out_dtype=F32, name="ar_out")
    xm0, h2_0 = _res_mod_fwd(ol0, xl0, g[0][1], gt1, g[0][2], one(sc2), sh2, name="l0_res1_mod2")
    r0, a0, f0 = _ffn_fwd(h2_0, w_mix['ff_in0'], w_mix['ff_out0'], "l0")
    th1, tc1, tg1, th2, tc2, tg2 = ml[1]
    xl1, hl1 = _res_mod_fwd(f0, xm0, g[0][3], gt2 + link.prefetch('l1', f0), g[1][0], one(tc1), th1, name="l0_res2_l1_mod1")
    w_l1 = link.weights('l1', xl1)
    z = _matmul(hl1, w_l1['gm_in'], kind='nn', b_split='n', bias=sp['gm_b_in'], out_dtype=F32, name="gm_in")
    b_sp_t = sp['gm_b_sp'].T
    gated = _gmlp_fwd(z, sp['gm_v_g'], sp['gm_v_b'], sp['gm_w_sp'], b_sp_t)
    ol1 = _matmul(gated, w_l1['gm_out'], kind='nn', b_split='k', out_dtype=F32, name="gm_out")
    xm1, h2_1 = _res_mod_fwd(ol1, xl1, g[1][1], tg1, g[1][2], one(tc2), th2, name="l1_res1_mod2")
    r1, a1, f1 = _ffn_fwd(h2_1, w_l1['ff_in1'], w_l1['ff_out1'], "l1")

    dy, d_f1, dg13, d_tg2, loss = _final_res_loss(f1, xm1, g[1][3], tg2, target)

    d_h2, dw_ff_in1, dw_ff_out1 = _ffn_bwd(d_f1, h2_1, r1, a1, w_l1['ff_in1'], w_l1['ff_out1'], "l1")
    tok = link.gradients('ffn1', {'ff_in1': dw_ff_in1, 'ff_out1': dw_ff_out1})
    dxm1, d_ol1, dg12, d_tc2, d_th2, dg11, d_tg1 = _mod_res_bwd(d_h2, xm1, g[1][2], one(tc2) + tok, dy, ol1, g[1][1], tg1,
                                                                name="l1_mod2_res1_bwd")
    d_gated = _matmul(d_ol1, w_l1['gm_out'], kind='nt', b_split='k', out_dtype=F32, name="gm_out_dx")
    dw_gm_out = _matmul(gated, d_ol1, kind='tn', out_split='k', out_dtype=BF16, name="gm_out_dw")
    d_z, d_gm_b_in, d_vg, d_vb, d_wsp, d_bsp_t = _gmlp_bwd(z, d_gated, sp['gm_v_g'], sp['gm_v_b'], sp['gm_w_sp'], b_sp_t)
    dw_gm_in = _matmul(hl1, d_z, kind='tn', out_split='n', out_dtype=BF16, name="gm_in_dw")
    d_hl1 = _matmul(d_z, w_l1['gm_in'], kind='nt', b_split='n', out_dtype=F32, name="gm_in_dx")
    tok = link.gradients('gm', {'gm_in': dw_gm_in, 'gm_out': dw_gm_out})

    dxl1, d_f0, dg10, d_tc1, d_th1, dg03, d_gt2 = _mod_res_bwd(d_hl1, xl1, g[1][0], one(tc1) + tok, dxm1, f0, g[0][3], gt2,
                                                               name="l1_mod1_l0_res2_bwd")
    d_h2, dw_ff_in0, dw_ff_out0 = _ffn_bwd(d_f0, h2_0, r0, a0, w_mix['ff_in0'], w_mix['ff_out0'], "l0")
    tok = link.gradients('ffn0', {'ff_in0': dw_ff_in0, 'ff_out0': dw_ff_out0})
    dxm0, d_ol0, dg02, d_sc2, d_sh2, dg01, d_gt1 = _mod_res_bwd(d_h2, xm0, g[0][2], one(sc2) + tok, dxl1, ol0, g[0][1], gt1,
                                                                name="l0_mod2_res1_bwd")
    d_cat = _matmul(d_ol0, w_mix['ar_out'], kind='nt', b_split='k', out_dtype=F32, name="ar_out_dx")
    dw_ar_out = _matmul(cat, d_ol0, kind='tn', out_split='k', out_dtype=BF16, name="ar_out_dw")
    dq, dk_all, dv_all = _attn_bwd(q_l, k_all, v_all, cat, lse, d_cat)
    tok = link.poll(dq)
    d_h, d_gate = _rnn_out_bwd(d_cat, h_f, h_r, proj_l, n_ctx, n_ctx)
    rnn_wb = [(wa_, ba_ + tok, wx_, bx_, lam_) for wa_, ba_, wx_, bx_, lam_ in rnn_w]
    dxs_f, d_wa0, d_ba0, d_wx0, d_bx0, d_lam0 = _rglru_bwd(
        xs, hp_f, d_h, *rnn_wb[0], reverse=False, n_ctx=n_ctx, name="rglru_bwd_f")
    dxs_r, d_wa1, d_ba1, d_wx1, d_bx1, d_lam1 = _rglru_bwd(
        xs, hp_r, d_h, *rnn_wb[1], reverse=True, n_ctx=n_ctx, name="rglru_bwd_r")
    d_xr_l, d_xr_c, d_cw, d_cb = _conv_bwd(dxs_f, dxs_r, proj_l, proj_c, sp['conv_w'])
    dp_qk_l, d_qg, d_kg_l = _qk_bwd(dq, dk_all, proj_l, q_g, k_g, cos_l, sin_l, name="qk_bwd_lat", dk_row0=n_ctx)
    dp_qk_c, _, d_kg_c = _qk_bwd(None, dk_all, proj_c, q_g, k_g, cos_c, sin_c, name="qk_bwd_ctx")
    d_proj = _assemble_d_proj(dp_qk_l, dp_qk_c, dv_all, d_xr_l, d_xr_c, d_gate)
    dw_ar_in = _matmul(h_all, d_proj, kind='tn', out_split='n', out_dtype=BF16, name="ar_in_dw")
    d_hl = _matmul(d_proj, big['ar_in'], kind='nt', b_split='n', out_dtype=F32, a_rows=(0, n_lat), name="ar_in_dx_lat")
    d_hc = _matmul(d_proj, big['ar_in'], kind='nt', b_split='n', out_dtype=F32, a_rows=(n_lat, n_ctx), name="ar_in_dx_ctx")
    grad_x, dg00, d_sc1, d_sh1 = _norm_bwd(d_hl, xl0, g[0][0], one(sc1), extra=dxm0, out_dtype=F32, name="l0_mod1_bwd")
    _, dg00c, d_mc_scale, d_mc_shift = _norm_bwd(d_hc, xc0, g[0][0], one(mc0[1]), out_dtype=BF16, name="l0_mod1_ctx_bwd")

    zeros_d = jnp.zeros_like(d_sh1)
    small = {
        'd_ml0': jnp.concatenate([d_sh1, d_sc1, d_gt1, d_sh2, d_sc2, d_gt2], axis=1),
        'd_ml1': jnp.concatenate([d_th1, d_tc1, d_tg1, d_th2, d_tc2, d_tg2], axis=1),
        'd_mc0': jnp.concatenate([d_mc_shift, d_mc_scale] + [zeros_d] * 4, axis=1),
        'norm_g': jnp.stack([jnp.concatenate([dg00 + dg00c, dg01, dg02, dg03], axis=0),
                             jnp.concatenate([dg10, dg11, dg12, dg13], axis=0)]),
        'q_g': d_qg, 'k_g': d_kg_l + d_kg_c, 'conv_w': d_cw, 'conv_b': d_cb,
        'wa': jnp.stack([d_wa0, d_wa1]), 'ba': jnp.concatenate([d_ba0, d_ba1], axis=0),
        'wx': jnp.stack([d_wx0, d_wx1]), 'bx': jnp.concatenate([d_bx0, d_bx1], axis=0),
        'lam': jnp.concatenate([d_lam0, d_lam1], axis=0),
        'gm_b_in': d_gm_b_in, 'gm_v_g': d_vg, 'gm_v_b': d_vb, 'gm_w_sp': d_wsp, 'gm_b_sp': d_bsp_t.T,
        'loss': loss,
    }
    return grad_x, small, {'ar_in': dw_ar_in, 'ar_out': dw_ar_out}


MOD_ROWS = 16
SMALL_F32 = ('d_ml0', 'd_ml1', 'd_mc0', 'norm_g', 'q_g', 'k_g', 'conv_w', 'conv_b', 'ba', 'bx', 'lam', 'gm_b_in', 'gm_v_g',
             'gm_v_b', 'gm_b_sp', 'loss')
SMALL_BF16 = ('wa', 'wx', 'gm_w_sp')


def _silu(v):
    return v * _sigmoid(v)


def _chip_concat(gathered, axis):
    return jnp.concatenate([gathered[2 * q] for q in range(N_CHIPS)], axis=axis)


def kernel(x, c, ctx, c_ctx, w_mod, b_mod, norm_g, w_ff_in, w_ff_out, ar_w_in, ar_q_g, ar_k_g, ar_conv_w, ar_conv_b, ar_wa, ar_ba, ar_wx, ar_bx, ar_lambda, ar_w_out, gm_w_in, gm_b_in, gm_v_g, gm_v_b, gm_w_sp, gm_b_sp, gm_w_out, loss_target, m_c_ctx, m_w_mod, m_b_mod, m_norm_g, m_w_ff_in, m_w_ff_out, m_ar_w_in, m_ar_q_g, m_ar_k_g, m_ar_conv_w, m_ar_conv_b, m_ar_wa, m_ar_ba, m_ar_wx, m_ar_bx, m_ar_lambda, m_ar_w_out, m_gm_w_in, m_gm_b_in, m_gm_v_g, m_gm_v_b, m_gm_w_sp, m_gm_b_sp, m_gm_w_out, v_c_ctx, v_w_mod, v_b_mod, v_norm_g, v_w_ff_in, v_w_ff_out, v_ar_w_in, v_ar_q_g, v_ar_k_g, v_ar_conv_w, v_ar_conv_b, v_ar_wa, v_ar_ba, v_ar_wx, v_ar_bx, v_ar_lambda, v_ar_w_out, v_gm_w_in, v_gm_b_in, v_gm_v_g, v_gm_v_b, v_gm_w_sp, v_gm_b_sp, v_gm_w_out):
    weights = dict(c_ctx=c_ctx, w_mod=w_mod, b_mod=b_mod, norm_g=norm_g, w_ff_in=w_ff_in, w_ff_out=w_ff_out, ar_w_in=ar_w_in,
                   ar_q_g=ar_q_g, ar_k_g=ar_k_g, ar_conv_w=ar_conv_w, ar_conv_b=ar_conv_b, ar_wa=ar_wa, ar_ba=ar_ba, ar_wx=ar_wx,
                   ar_bx=ar_bx, ar_lambda=ar_lambda, ar_w_out=ar_w_out, gm_w_in=gm_w_in, gm_b_in=gm_b_in, gm_v_g=gm_v_g,
                   gm_v_b=gm_v_b, gm_w_sp=gm_w_sp, gm_b_sp=gm_b_sp, gm_w_out=gm_w_out)
    m_in = dict(c_ctx=m_c_ctx, w_mod=m_w_mod, b_mod=m_b_mod, norm_g=m_norm_g, w_ff_in=m_w_ff_in, w_ff_out=m_w_ff_out,
                ar_w_in=m_ar_w_in, ar_q_g=m_ar_q_g, ar_k_g=m_ar_k_g, ar_conv_w=m_ar_conv_w, ar_conv_b=m_ar_conv_b, ar_wa=m_ar_wa,
                ar_ba=m_ar_ba, ar_wx=m_ar_wx, ar_bx=m_ar_bx, ar_lambda=m_ar_lambda, ar_w_out=m_ar_w_out, gm_w_in=m_gm_w_in,
                gm_b_in=m_gm_b_in, gm_v_g=m_gm_v_g, gm_v_b=m_gm_v_b, gm_w_sp=m_gm_w_sp, gm_b_sp=m_gm_b_sp, gm_w_out=m_gm_w_out)
    v_in = dict(c_ctx=v_c_ctx, w_mod=v_w_mod, b_mod=v_b_mod, norm_g=v_norm_g, w_ff_in=v_w_ff_in, w_ff_out=v_w_ff_out,
                ar_w_in=v_ar_w_in, ar_q_g=v_ar_q_g, ar_k_g=v_ar_k_g, ar_conv_w=v_ar_conv_w, ar_conv_b=v_ar_conv_b, ar_wa=v_ar_wa,
                ar_ba=v_ar_ba, ar_wx=v_ar_wx, ar_bx=v_ar_bx, ar_lambda=v_ar_lambda, ar_w_out=v_ar_w_out, gm_w_in=v_gm_w_in,
                gm_b_in=v_gm_b_in, gm_v_g=v_gm_v_g, gm_v_b=v_gm_v_b, gm_w_sp=v_gm_w_sp, gm_b_sp=v_gm_b_sp, gm_w_out=v_gm_w_out)

    xi, yi, ci = lax.axis_index("x"), lax.axis_index("y"), lax.axis_index("c")
    chip = 2 * xi + yi
    dev = 4 * xi + 2 * yi + ci
    place = jnp.stack([chip, ci]).astype(jnp.int32)
    n_lat, d = x.shape[1], x.shape[2]
    d6 = 6 * d
    cols_mod = w_mod.shape[2]

    mine = [c, norm_g, ar_conv_w[0], ar_ba[0], ar_bx[0], ar_lambda[0], gm_b_in, gm_v_g, gm_v_b]
    gathered = _allgather8([_pack(mine)], "gather_small_params")[0]
    first = _split_start("gather_first_start", [_cast_shard(ar_w_in, place, 0, "cast_ar_in")], _gather_copies, N_CHIPS - 1, gathered)
    parts = _unpack_devices(gathered, [a.shape for a in mine])
    c_all = parts[0].reshape(N_DEV, d)
    sp = {'norm_g': _chip_concat(parts[1], 2), 'q_g': ar_q_g, 'k_g': ar_k_g, 'conv_w': _chip_concat(parts[2], 1),
          'conv_b': ar_conv_b, 'wa': ar_wa[0], 'ba': _chip_concat(parts[3], 1), 'wx': ar_wx[0], 'bx': _chip_concat(parts[4], 1),
          'lam': _chip_concat(parts[5], 1), 'gm_b_in': _chip_concat(parts[6], 1), 'gm_v_g': _chip_concat(parts[7], 1),
          'gm_v_b': _chip_concat(parts[8], 1), 'gm_w_sp': gm_w_sp[0], 'gm_b_sp': gm_b_sp[0]}

    def mod_operand(c_rows, cc):
        row = lax.broadcasted_iota(jnp.int32, (MOD_ROWS - N_DEV, d), 0)
        lower = jnp.where(row == 0, jnp.broadcast_to(_silu(cc), (MOD_ROWS - N_DEV, d)), 0.0)
        sig = _sigmoid(cc)
        return jnp.concatenate([_silu(c_rows), lower], axis=0), sig * (1.0 + cc * (1.0 - sig))

    s_mod, dsilu_ctx = _small(mod_operand, [((MOD_ROWS, d), F32), ((1, d), F32)], c_all, c_ctx[None, :], name="mod_operand")
    b_mod_mine = lax.dynamic_slice(b_mod, (0, chip * cols_mod), (2, cols_mod))
    mod = [_matmul(s_mod, w_mod, kind='nn', b_layer=i, bias=b_mod_mine[i][None, :], out_dtype=F32, name=f"mod_fwd_{i}")
           for i in range(2)]
    mod_all = _allgather8([jnp.concatenate(mod, axis=0)], "gather_mod")[0]
    mod_all = _chip_concat(mod_all, 1).reshape(2, MOD_ROWS, d6)
    ml = [jnp.split(lax.dynamic_slice(mod_all[i], (dev, 0), (1, d6)), 6, axis=1) for i in range(2)]
    mc0 = jnp.split(mod_all[0, N_DEV:N_DEV + 1], 6, axis=1)[:2]

    names = ('w_ff_in', 'w_ff_out', 'ar_w_in', 'ar_w_out', 'gm_w_in', 'gm_w_out')
    keys = {'w_ff_in': ('ff_in0', 'ff_in1'), 'w_ff_out': ('ff_out0', 'ff_out1'), 'ar_w_in': ('ar_in',), 'ar_w_out': ('ar_out',),
            'gm_w_in': ('gm_in',), 'gm_w_out': ('gm_out',)}
    shards = {key: _cast_shard(weights[n], place, layer, f"cast_{key}", after=first[3]) for n in names
              for layer, key in enumerate(keys[n]) if key != 'ar_in'}
    send, recv, bufs, _ = first
    bufs = _split_wait("gather_first_wait", bufs, send, recv, _gather_copies, mod_all)
    send, recv, bufs, token = _split_start("forward_first_start", bufs, _forward_copies, N_CHIPS - 1)
    shards['ar_in'] = _split_wait("forward_first_wait", bufs, send, recv, _forward_copies, token)[0]
    link = _MeshLink(place, shards)

    grad_x, small, last_grads = _local_step(x[0], ctx[0], loss_target[0], ml, mc0, sp, link)

    def step(n, grad):
        return _adamw(weights[n], grad.reshape(weights[n].shape), m_in[n], v_in[n], f"adamw_{n}", rewrite_grad=n in names)

    small_f32, small_bf16 = [small[k] for k in SMALL_F32], [small[k] for k in SMALL_BF16]
    dev_arr = dev.astype(jnp.int32)[None]
    slots = [_into_slot(_pack(small_f32), dev_arr, "small_grads_slot_f32"),
             _into_slot(_pack(small_bf16, BF16), dev_arr, "small_grads_slot_bf16")]
    s_send, s_recv, slots, s_token = _split_start("small_grads_start", slots, _gather8_copies, 2 * N_CHIPS, grad_x)
    link.gradients('ar', last_grads, s_token)
    link.poll(link.last_token)
    reduced = link.reduce(('ffn1', 'gm', 'ffn0'), link.last_token)
    stepped = {n: step(n, reduced[names.index(n)]) for n in ('w_ff_in', 'w_ff_out', 'gm_w_in', 'gm_w_out')}
    reduced = link.reduce(('ar',), stepped['gm_w_out'][1])
    stepped.update({n: step(n, reduced[names.index(n)]) for n in ('ar_w_in', 'ar_w_out')})
    slots = _split_wait("small_grads_wait", slots, s_send, s_recv, _gather8_copies, stepped['ar_w_out'][1])
    small8, small8_bf16 = _forward_slots(slots, "small_grads_forward")
    total = dict(zip(SMALL_F32, _unpack(_sum_devices(small8).reshape(-1), [a.shape for a in small_f32])))
    total.update(zip(SMALL_BF16, _unpack(_sum_devices(small8_bf16).reshape(-1), [a.shape for a in small_bf16])))
    per_dev = _unpack_devices(small8, [(d6,), (d6,)])
    pad_rows = jnp.zeros((MOD_ROWS - N_DEV - 1, d6), F32)
    d_mod = [jnp.concatenate([per_dev[0], total['d_mc0'], pad_rows], axis=0),
             jnp.concatenate([per_dev[1], jnp.zeros((MOD_ROWS - N_DEV, d6), F32)], axis=0)]
    d_mod_mine = [lax.dynamic_slice(dm, (0, chip * cols_mod), (MOD_ROWS, cols_mod)) for dm in d_mod]
    g_w_mod = None
    for i in range(2):
        g_w_mod = _matmul(s_mod, d_mod_mine[i], kind='tn', out_dtype=F32, out_stack=(i, 2, g_w_mod), name=f"mod_dw_{i}")
    d_s_part = _matmul(d_mod_mine[0], w_mod, kind='nt', b_layer=0, out_dtype=F32, name="mod_ds")
    d_s_all = _allgather8([d_s_part[N_DEV:]], "gather_mod_ds")[0]

    def c_ctx_grad(parts_, dsilu):
        acc = parts_[0, 0:1]
        for q in range(1, N_CHIPS):
            acc = acc + parts_[2 * q, 0:1]
        return (acc * dsilu,)

    g_c_ctx = _small(c_ctx_grad, [((1, d), F32)], d_s_all, dsilu_ctx, name="c_ctx_grad")[0].reshape(d)

    def mine_of(full_grad, axis, n_shard):
        return lax.dynamic_slice_in_dim(full_grad, chip * n_shard, n_shard, axis=axis)

    grads_out = {
        'c_ctx': g_c_ctx, 'w_mod': g_w_mod,
        'b_mod': jnp.stack([total['d_ml0'][0] + total['d_mc0'][0], total['d_ml1'][0]]),
        'norm_g': mine_of(total['norm_g'], 2, norm_g.shape[2]),
        'ar_q_g': total['q_g'], 'ar_k_g': total['k_g'], 'ar_conv_w': mine_of(total['conv_w'], 1, ar_conv_w.shape[2])[None],
        'ar_conv_b': total['conv_b'], 'ar_wa': total['wa'][None], 'ar_ba': mine_of(total['ba'], 1, ar_ba.shape[2])[None],
        'ar_wx': total['wx'][None], 'ar_bx': mine_of(total['bx'], 1, ar_bx.shape[2])[None],
        'ar_lambda': mine_of(total['lam'], 1, ar_lambda.shape[2])[None],
        'gm_b_in': mine_of(total['gm_b_in'], 1, gm_b_in.shape[1]),
        'gm_v_g': mine_of(total['gm_v_g'], 1, gm_v_g.shape[1]), 'gm_v_b': mine_of(total['gm_v_b'], 1, gm_v_b.shape[1]),
        'gm_w_sp': total['gm_w_sp'][None], 'gm_b_sp': total['gm_b_sp'][None],
    }
    stepped.update({n: step(n, grad) for n, grad in grads_out.items()})
    stepped = [stepped[n] for n in weights]
    loss = total['loss'].reshape(())
    return (loss, grad_x[None], *[s[0] for s in stepped], *[s[1] for s in stepped], *[s[2] for s in stepped],
            *[s[3] for s in stepped])
```

```python
import functools
import math

import jax
import jax.numpy as jnp
from jax import lax
from jax.experimental import pallas as pl
from jax.experimental.pallas import tpu as pltpu

F32 = jnp.float32
BF16 = jnp.bfloat16
MESH = pl.DeviceIdType.MESH
ANY = pl.BlockSpec(memory_space=pl.ANY)

VMEM_LIMIT_BYTES = 52 * 1024 * 1024
LANES = 128
N_CHIPS = 4
N_DEV = 8

HEAD_DIM = 128
N_HEADS = 8
N_KV = 2
GROUP = N_HEADS // N_KV
ATTN_W = N_HEADS * HEAD_DIM
KV_W = N_KV * HEAD_DIM
D_RNN = 1024
RNN_BLOCKS = 8
RNN_BW = D_RNN // RNN_BLOCKS
CONV_W = 4
RG_C = 8.0
GRID_W = 64
ROPE_THETA = 10000.0
ROPE_PAIRS = HEAD_DIM // 4
GM_GROUPS = 16
CHUNK = 128
EPS = 1e-6
ADAM_LR, ADAM_B1, ADAM_B2, ADAM_EPS, ADAM_WD, ADAM_STEP = 0.001, 0.9, 0.999, 1e-08, 0.01, 10
GELU_C = math.sqrt(2.0 / math.pi)
LOG2E = math.log2(math.e)


def _params(sem=None):
    return pltpu.CompilerParams(dimension_semantics=sem, vmem_limit_bytes=VMEM_LIMIT_BYTES)


def _tile(dim, pref, unit):
    best = None
    t = unit
    while t <= min(dim, pref):
        if dim % t == 0:
            best = t
        t += unit
    return best if best is not None else dim


def _full(shape):
    nd = len(shape)
    return pl.BlockSpec(shape, lambda *_: (0,) * nd)


def _blocked_map(split, per_q):
    assert split == 'n'
    return lambda r, c: (c // per_q, r, c % per_q)


def _logical_shape(arr, split):
    if split == 'n':
        return arr.shape[1], arr.shape[0] * arr.shape[2]
    if split == 'k':
        return arr.shape[0] * arr.shape[1], arr.shape[2]
    return arr.shape[-2:]


def _matmul(a, b, *, kind, name, out_dtype, b_split=None, out_split=None, bias=None, epilogue=None, extra=None,
            a_rows=None, b_layer=None, out_stack=None, pref=(1024, 1024, 2048)):
    if b_split == 'k':
        b, b_split = b.reshape(-1, b.shape[-1]), None
    blocked_rows_out = out_split == 'k'
    if blocked_rows_out:
        assert epilogue != 'relu2'
        out_split = None
    b_rows, b_cols = _logical_shape(b, b_split)
    row0, nt_groups = 0, 1
    if kind == 'nn':
        m, kc = a.shape
        n = b_cols
        assert b_rows == kc
    elif kind == 'nt':
        m, kc = a.shape
        n = b_rows
        assert b_cols == kc
    if a_rows is not None:
        assert kind != 'tn'
        row0, m = a_rows
    if kind == 'tn':
        kc, m = a.shape
        n = b_cols
        assert b_rows == kc
    b_row_ext = b.shape[1] if b_split == 'k' else b_rows
    b_col_ext = b.shape[2] if b_split == 'n' else b_cols
    out_row_ext = m // N_CHIPS if out_split == 'k' else m
    out_col_ext = n // N_CHIPS if out_split == 'n' else n
    if kind == 'nn':
        ti = _tile(math.gcd(min(m, out_row_ext), row0), pref[0], 16)
        tj = _tile(math.gcd(b_col_ext, out_col_ext), pref[1], LANES)
        tl = _tile(b_row_ext, pref[2], LANES)
        a_spec = pl.BlockSpec((ti, tl), lambda i, j, l: (i + row0 // ti, l))
        b_tile, b_rc = (tl, tj), (lambda i, j, l: (l, j))
        dims = (((1,), (0,)), ((), ()))
    elif kind == 'nt':
        ti = _tile(math.gcd(min(m, out_row_ext), row0), pref[0], 16)
        tj = _tile(math.gcd(b_row_ext, out_col_ext), pref[1], LANES)
        if b_split == 'n' and b.shape[2] < pref[2]:
            nt_groups = max(k for k in (1, 2, 4) if k * b.shape[2] <= pref[2])
        tl = nt_groups * b.shape[2] if nt_groups > 1 else _tile(b_col_ext, pref[2], LANES)
        a_spec = pl.BlockSpec((ti, tl), lambda i, j, l: (i + row0 // ti, l))
        b_tile, b_rc = (tj, tl), (lambda i, j, l: (j, l))
        dims = (((1,), (1,)), ((), ()))
    else:
        ti = _tile(out_row_ext, pref[0], LANES)
        tj = _tile(math.gcd(b_col_ext, out_col_ext), pref[1], LANES)
        tl = _tile(b_row_ext, pref[2], 16)
        a_spec = pl.BlockSpec((tl, ti), lambda i, j, l: (l, i))
        b_tile, b_rc = (tl, tj), (lambda i, j, l: (l, j))
        dims = (((0,), (0,)), ((), ()))
    grid = (m // ti, n // tj, kc // tl)
    n_l = grid[2]

    if nt_groups > 1:
        b_spec = pl.BlockSpec((nt_groups, tj, b.shape[2]), lambda i, j, l: (l, j, 0))
    elif b_layer is not None:
        b_spec = pl.BlockSpec((None,) + b_tile, lambda i, j, l: (b_layer, *b_rc(i, j, l)))
    elif b_split is None:
        b_spec = pl.BlockSpec(b_tile, b_rc)
    else:
        per_q = (b.shape[2] // b_tile[1]) if b_split == 'n' else (b.shape[1] // b_tile[0])
        bmap = _blocked_map(b_split, per_q)
        b_spec = pl.BlockSpec((None,) + b_tile, lambda i, j, l: bmap(*b_rc(i, j, l)))
    if out_stack is not None:
        layer, n_layers, _ = out_stack
        out_shape2 = (n_layers, m, n)
        o_spec = pl.BlockSpec((None, ti, tj), lambda i, j, l: (layer, i, j))
    elif out_split is None:
        out_shape2 = (m, n)
        o_spec = pl.BlockSpec((ti, tj), lambda i, j, l: (i, j))
    else:
        out_shape2 = (N_CHIPS, m // N_CHIPS, n) if out_split == 'k' else (N_CHIPS, m, n // N_CHIPS)
        per_q = (out_shape2[2] // tj) if out_split == 'n' else (out_shape2[1] // ti)
        omap = _blocked_map(out_split, per_q)
        o_spec = pl.BlockSpec((None, ti, tj), lambda i, j, l: omap(i, j))

    in_specs = [a_spec, b_spec]
    operands = [a, b]
    if bias is not None:
        in_specs.append(pl.BlockSpec((1, tj), lambda i, j, l: (0, j)))
        operands.append(bias)
    if extra is not None:
        in_specs.append(pl.BlockSpec((ti, tj), lambda i, j, l: (i, j)))
        operands.append(extra)
    if epilogue == 'relu2':
        out_shape = (jax.ShapeDtypeStruct(out_shape2, out_dtype), jax.ShapeDtypeStruct(out_shape2, out_dtype))
        out_specs = (o_spec, o_spec)
    else:
        out_shape = jax.ShapeDtypeStruct(out_shape2, out_dtype)
        out_specs = o_spec
    has_bias, has_extra = bias is not None, extra is not None
    has_dest = out_stack is not None and out_stack[2] is not None
    if has_dest:
        in_specs.append(ANY)
        operands.append(out_stack[2])

    def body(*refs):
        a_ref, b_ref = refs[0], refs[1]
        pos = 2
        bias_ref = extra_ref = None
        if has_bias:
            bias_ref = refs[pos]
            pos += 1
        if has_extra:
            extra_ref = refs[pos]
            pos += 1
        if has_dest:
            pos += 1
        outs = refs[pos:] if n_l == 1 else refs[pos:-1]

        def finish(acc):
            if has_bias:
                acc = acc + bias_ref[...]
            if epilogue == 'relu2':
                r = jnp.maximum(acc, 0.0)
                outs[0][...] = r.astype(outs[0].dtype)
                outs[1][...] = (r * r).astype(outs[1].dtype)
            elif epilogue == 'times2x':
                outs[0][...] = (acc * (2.0 * extra_ref[...].astype(F32))).astype(outs[0].dtype)
            else:
                outs[0][...] = acc.astype(outs[0].dtype)

        def product():
            if nt_groups == 1:
                return lax.dot_general(a_ref[...].astype(BF16), b_ref[...].astype(BF16), dims, preferred_element_type=F32)
            width = b_ref.shape[2]
            return sum(lax.dot_general(a_ref[:, s * width:(s + 1) * width].astype(BF16), b_ref[s].astype(BF16), dims,
                                       preferred_element_type=F32) for s in range(nt_groups))

        if n_l == 1:
            finish(product())
            return
        acc_ref = refs[-1]
        step = pl.program_id(2)

        @pl.when(step == 0)
        def _():
            acc_ref[...] = jnp.zeros_like(acc_ref)

        acc_ref[...] += product()

        @pl.when(step == n_l - 1)
        def _():
            finish(acc_ref[...])

    result = pl.pallas_call(
        body, name=name, grid=grid, in_specs=in_specs, out_specs=out_specs, out_shape=out_shape,
        input_output_aliases={len(operands) - 1: 0} if has_dest else {},
        scratch_shapes=[] if n_l == 1 else [pltpu.VMEM((ti, tj), F32)],
        compiler_params=_params(("parallel", "parallel", "arbitrary")),
    )(*operands)
    return result.reshape(N_CHIPS, m // N_CHIPS, n) if blocked_rows_out else result


def _small(fn, out_shapes, *arrays, name):
    n_in = len(arrays)

    def body(*refs):
        res = fn(*[r[...] for r in refs[:n_in]])
        for o_ref, v in zip(refs[n_in:], res):
            o_ref[...] = v.astype(o_ref.dtype)

    return pl.pallas_call(
        body, name=name, out_shape=tuple(jax.ShapeDtypeStruct(s, d) for s, d in out_shapes),
        in_specs=[_full(a.shape) for a in arrays], out_specs=tuple(_full(s) for s, _ in out_shapes), grid=(1,),
        compiler_params=_params(("arbitrary",)),
    )(*arrays)


def _rows_tile(rows, cols, itemsize=4, budget=2 * 1024 * 1024):
    return _tile(rows, max(16, budget // (cols * itemsize)), 16)


def _rowwise(fn, out_dtypes, *arrays, name):
    rows, cols = arrays[0].shape
    tr = _rows_tile(rows, cols)
    n_in = len(arrays)

    def body(*refs):
        res = fn(*[r[...] for r in refs[:n_in]])
        for o_ref, v in zip(refs[n_in:], res):
            o_ref[...] = v.astype(o_ref.dtype)

    spec = pl.BlockSpec((tr, cols), lambda i: (i, 0))
    return pl.pallas_call(
        body, name=name, grid=(rows // tr,), in_specs=[spec] * n_in, out_specs=tuple(spec for _ in out_dtypes),
        out_shape=tuple(jax.ShapeDtypeStruct((rows, cols), d) for d in out_dtypes),
        compiler_params=_params(("parallel",)),
    )(*arrays)


def _as2d(a):
    return a.reshape(1, a.size) if a.ndim < 2 else a.reshape(-1, a.shape[-1])


def _cast_shard(w, place, layer, name, after=None):
    _, rows, cols = w.shape
    tr = _rows_tile(rows, cols)

    def body(place_ref, w_ref, *rest):
        rest[-1][...] = w_ref[...].astype(rest[-1].dtype)

    return pl.pallas_call(
        body, name=name, out_shape=jax.ShapeDtypeStruct((N_CHIPS, rows, cols), BF16),
        grid_spec=pltpu.PrefetchScalarGridSpec(
            num_scalar_prefetch=1, grid=(rows // tr,),
            in_specs=[pl.BlockSpec((None, tr, cols), lambda i, pr: (layer, i, 0))] + ([] if after is None else [ANY]),
            out_specs=pl.BlockSpec((None, tr, cols), lambda i, pr: (pr[0], i, 0))),
        compiler_params=_params(("parallel",)),
    )(place, w, *([] if after is None else [after]))


def _norm_fwd(x, g, a, b=None, res=None, *, out_dtype, name, into=None):
    rows, d = x.shape
    row0, total, dest = into if into is not None else (0, rows, None)
    tr = _rows_tile(math.gcd(rows, row0), d, budget=4 * 1024 * 1024)
    has_b, has_res = b is not None, res is not None

    def body(*refs):
        x_ref, g_ref, a_ref = refs[:3]
        pos = 3
        xv = x_ref[...]
        rstd = lax.rsqrt(jnp.mean(xv * xv, axis=-1, keepdims=True) + EPS)
        y = (xv * rstd * g_ref[...]) * a_ref[...]
        if has_b:
            y = y + refs[pos][...]
            pos += 1
        if has_res:
            y = y + refs[pos][...]
            pos += 1
        refs[-1][...] = y.astype(refs[-1].dtype)

    row = pl.BlockSpec((tr, d), lambda i: (i, 0))
    vec = pl.BlockSpec((1, d), lambda i: (0, 0))
    operands, specs = [x, g, a], [row, vec, vec]
    if has_b:
        operands.append(b)
        specs.append(vec)
    if has_res:
        operands.append(res)
        specs.append(row)
    if dest is not None:
        operands.append(dest)
        specs.append(ANY)
    return pl.pallas_call(
        body, name=name, grid=(rows // tr,), in_specs=specs, out_specs=pl.BlockSpec((tr, d), lambda i: (i + row0 // tr, 0)),
        out_shape=jax.ShapeDtypeStruct((total, d), out_dtype), compiler_params=_params(("parallel",)),
        input_output_aliases={} if dest is None else {len(operands) - 1: 0},
    )(*operands)


def _rstd(v):
    return lax.rsqrt(jnp.mean(v * v, axis=-1, keepdims=True) + EPS)


def _res_mod_fwd(o, x, g_res, gate, g_mod, a_mod, b_mod, *, name):
    rows, d = x.shape
    tr = _rows_tile(rows, d)

    def body(o_ref, x_ref, gr_ref, gate_ref, gm_ref, a_ref, b_ref, xm_ref, h_ref):
        ov = o_ref[...]
        xm = x_ref[...] + (ov * _rstd(ov) * gr_ref[...]) * gate_ref[...]
        xm_ref[...] = xm
        h_ref[...] = ((xm * _rstd(xm) * gm_ref[...]) * a_ref[...] + b_ref[...]).astype(h_ref.dtype)

    row = pl.BlockSpec((tr, d), lambda i: (i, 0))
    vec = pl.BlockSpec((1, d), lambda i: (0, 0))
    return pl.pallas_call(
        body, name=name, grid=(rows // tr,), in_specs=[row, row, vec, vec, vec, vec, vec], out_specs=(row, row),
        out_shape=(jax.ShapeDtypeStruct((rows, d), F32), jax.ShapeDtypeStruct((rows, d), BF16)),
        compiler_params=_params(("parallel",)),
    )(o, x, g_res, gate, g_mod, a_mod, b_mod)


def _mod_res_bwd(d_h, xm, g_mod, a_mod, extra, o, g_res, gate, *, name):
    rows, d = xm.shape
    tr = _rows_tile(rows, d)

    def body(dh_ref, xm_ref, gm_ref, a_ref, ex_ref, o_ref, gr_ref, gate_ref,
             dxm_ref, do_ref, dgm_ref, da_ref, db_ref, dgr_ref, dgate_ref):
        @pl.when(pl.program_id(0) == 0)
        def _():
            for ref in (dgm_ref, da_ref, db_ref, dgr_ref, dgate_ref):
                ref[...] = jnp.zeros_like(ref)

        def norm_adjoint(dy, xv, gain, scale, dgain_ref, dscale_ref):
            rstd = _rstd(xv)
            nrm = xv * rstd
            dscale_ref[...] += jnp.sum(dy * (nrm * gain), axis=0, keepdims=True)
            dt = dy * scale
            dgain_ref[...] += jnp.sum(dt * nrm, axis=0, keepdims=True)
            dn = dt * gain
            return rstd * (dn - nrm * jnp.mean(dn * nrm, axis=-1, keepdims=True))

        dhv = dh_ref[...].astype(F32)
        db_ref[...] += jnp.sum(dhv, axis=0, keepdims=True)
        dxm = norm_adjoint(dhv, xm_ref[...], gm_ref[...], a_ref[...], dgm_ref, da_ref) + ex_ref[...]
        dxm_ref[...] = dxm
        do_ref[...] = norm_adjoint(dxm, o_ref[...], gr_ref[...], gate_ref[...], dgr_ref, dgate_ref).astype(do_ref.dtype)

    row = pl.BlockSpec((tr, d), lambda i: (i, 0))
    vec = pl.BlockSpec((1, d), lambda i: (0, 0))
    vshape = jax.ShapeDtypeStruct((1, d), F32)
    return pl.pallas_call(
        body, name=name, grid=(rows // tr,), in_specs=[row, row, vec, vec, row, row, vec, vec],
        out_specs=(row, row, vec, vec, vec, vec, vec),
        out_shape=(jax.ShapeDtypeStruct((rows, d), F32), jax.ShapeDtypeStruct((rows, d), BF16)) + (vshape,) * 5,
        compiler_params=_params(("arbitrary",)),
    )(d_h, xm, g_mod, a_mod, extra, o, g_res, gate)


def _norm_bwd(dy, x, g, a, extra=None, *, out_dtype, name):
    rows, d = x.shape
    tr = _rows_tile(rows, d)
    has_extra = extra is not None

    def body(*refs):
        dy_ref, x_ref, g_ref, a_ref = refs[:4]
        pos = 4
        extra_ref = None
        if has_extra:
            extra_ref = refs[pos]
            pos += 1
        dx_ref, dg_ref, da_ref, db_ref = refs[pos:pos + 4]

        @pl.when(pl.program_id(0) == 0)
        def _():
            dg_ref[...] = jnp.zeros_like(dg_ref)
            da_ref[...] = jnp.zeros_like(da_ref)
            db_ref[...] = jnp.zeros_like(db_ref)

        xv = x_ref[...]
        dyv = dy_ref[...].astype(F32)
        rstd = lax.rsqrt(jnp.mean(xv * xv, axis=-1, keepdims=True) + EPS)
        nrm = xv * rstd
        gv = g_ref[...]
        da_ref[...] += jnp.sum(dyv * (nrm * gv), axis=0, keepdims=True)
        db_ref[...] += jnp.sum(dyv, axis=0, keepdims=True)
        dt = dyv * a_ref[...]
        dg_ref[...] += jnp.sum(dt * nrm, axis=0, keepdims=True)
        dn = dt * gv
        dx = rstd * (dn - nrm * jnp.mean(dn * nrm, axis=-1, keepdims=True))
        if has_extra:
            dx = dx + extra_ref[...]
        dx_ref[...] = dx.astype(dx_ref.dtype)

    row = pl.BlockSpec((tr, d), lambda i: (i, 0))
    vec = pl.BlockSpec((1, d), lambda i: (0, 0))
    operands, specs = [dy, x, g, a], [row, row, vec, vec]
    if has_extra:
        operands.append(extra)
        specs.append(row)
    vshape = jax.ShapeDtypeStruct((1, d), F32)
    return pl.pallas_call(
        body, name=name, grid=(rows // tr,), in_specs=specs, out_specs=(row, vec, vec, vec),
        out_shape=(jax.ShapeDtypeStruct((rows, d), out_dtype), vshape, vshape, vshape),
        compiler_params=_params(("arbitrary",)),
    )(*operands)


def _final_res_loss(f, x, g, gate, target):
    rows, d = x.shape
    tr = _rows_tile(rows, d)

    def body(f_ref, x_ref, g_ref, gate_ref, t_ref, dy_ref, df_ref, dg_ref, dgate_ref, loss_ref):
        @pl.when(pl.program_id(0) == 0)
        def _():
            for ref in (dg_ref, dgate_ref, loss_ref):
                ref[...] = jnp.zeros_like(ref)

        fv, gv, gatev = f_ref[...], g_ref[...], gate_ref[...]
        rstd = _rstd(fv)
        nrm = fv * rstd
        err = x_ref[...] + (nrm * gv) * gatev - t_ref[...]
        loss_ref[...] += jnp.sum(jnp.sum(err * err, axis=-1, keepdims=True), axis=0, keepdims=True) * (0.5 / d)
        dy = err * (1.0 / d)
        dy_ref[...] = dy
        dgate_ref[...] += jnp.sum(dy * (nrm * gv), axis=0, keepdims=True)
        dt = dy * gatev
        dg_ref[...] += jnp.sum(dt * nrm, axis=0, keepdims=True)
        dn = dt * gv
        df_ref[...] = (rstd * (dn - nrm * jnp.mean(dn * nrm, axis=-1, keepdims=True))).astype(df_ref.dtype)

    row = pl.BlockSpec((tr, d), lambda i: (i, 0))
    vec = pl.BlockSpec((1, d), lambda i: (0, 0))
    vshape = jax.ShapeDtypeStruct((1, d), F32)
    return pl.pallas_call(
        body, name="final_res_loss", grid=(rows // tr,), in_specs=[row, row, vec, vec, row],
        out_specs=(row, row, vec, vec, _full((1, 1))),
        out_shape=(jax.ShapeDtypeStruct((rows, d), F32), jax.ShapeDtypeStruct((rows, d), BF16), vshape, vshape,
                   jax.ShapeDtypeStruct((1, 1), F32)),
        compiler_params=_params(("arbitrary",)),
    )(f, x, g, gate, target)


def _rope_partner(v):
    lane = lax.broadcasted_iota(jnp.int32, v.shape, 1)
    up = pltpu.roll(v, HEAD_DIM - ROPE_PAIRS, 1)
    down = pltpu.roll(v, ROPE_PAIRS, 1)
    return jnp.where((lane % (2 * ROPE_PAIRS)) < ROPE_PAIRS, up, down)


def _qk_fwd(proj, q_g, k_g, cos, sin, *, name, kv_into=None):
    rows = proj.shape[0]
    row0, total, kv_dest = kv_into if kv_into is not None else (0, rows, None)
    tr = _tile(math.gcd(rows, row0), 256, 16)
    width = ATTN_W + 2 * KV_W

    def body(p_ref, qg_ref, kg_ref, cos_ref, sin_ref, *rest):
        q_ref, k_ref, v_ref = rest[-3:]
        cosv, sinv = cos_ref[...], sin_ref[...]
        for h in range(N_HEADS + N_KV):
            xv = p_ref[:, h * HEAD_DIM:(h + 1) * HEAD_DIM]
            gain = qg_ref[...] if h < N_HEADS else kg_ref[...]
            t = xv * lax.rsqrt(jnp.mean(xv * xv, axis=-1, keepdims=True) + EPS) * gain
            y = t * cosv + _rope_partner(t) * sinv
            if h < N_HEADS:
                q_ref[:, h * HEAD_DIM:(h + 1) * HEAD_DIM] = y.astype(BF16)
            else:
                k_ref[:, (h - N_HEADS) * HEAD_DIM:(h - N_HEADS + 1) * HEAD_DIM] = y.astype(BF16)
        v_ref[...] = p_ref[:, ATTN_W + KV_W:width].astype(BF16)

    vec = _full((1, HEAD_DIM))
    tab = pl.BlockSpec((tr, HEAD_DIM), lambda i: (i, 0))
    kv_spec = pl.BlockSpec((tr, KV_W), lambda i: (i + row0 // tr, 0))
    kv_shape = jax.ShapeDtypeStruct((total, KV_W), BF16)
    return pl.pallas_call(
        body, name=name, grid=(rows // tr,),
        in_specs=[pl.BlockSpec((tr, width), lambda i: (i, 0)), vec, vec, tab, tab] + ([] if kv_dest is None else [ANY, ANY]),
        out_specs=(pl.BlockSpec((tr, ATTN_W), lambda i: (i, 0)), kv_spec, kv_spec),
        out_shape=(jax.ShapeDtypeStruct((rows, ATTN_W), BF16), kv_shape, kv_shape),
        input_output_aliases={} if kv_dest is None else {5: 1, 6: 2},
        compiler_params=_params(("parallel",)),
    )(proj, q_g, k_g, cos, sin, *([] if kv_dest is None else kv_dest))


def _qk_bwd(dq, dk, proj, q_g, k_g, cos, sin, *, name, dk_row0=0):
    rows = proj.shape[0]
    tr = _tile(math.gcd(rows, dk_row0), 256, 16)
    width = ATTN_W + KV_W
    has_q = dq is not None

    def body(*refs):
        pos = 0
        dq_ref = None
        if has_q:
            dq_ref = refs[0]
            pos = 1
        dk_ref, p_ref, qg_ref, kg_ref, cos_ref, sin_ref, dp_ref, dqg_ref, dkg_ref = refs[pos:pos + 9]

        @pl.when(pl.program_id(0) == 0)
        def _():
            dqg_ref[...] = jnp.zeros_like(dqg_ref)
            dkg_ref[...] = jnp.zeros_like(dkg_ref)

        cosv, sinv = cos_ref[...], sin_ref[...]
        for h in range(N_HEADS + N_KV):
            cols = slice(h * HEAD_DIM, (h + 1) * HEAD_DIM)
            if h < N_HEADS and not has_q:
                dp_ref[:, cols] = jnp.zeros((tr, HEAD_DIM), dp_ref.dtype)
                continue
            if h < N_HEADS:
                dyv, gain, dgain_ref = dq_ref[:, cols], qg_ref[...], dqg_ref
            else:
                hk = h - N_HEADS
                dyv, gain, dgain_ref = dk_ref[:, hk * HEAD_DIM:(hk + 1) * HEAD_DIM], kg_ref[...], dkg_ref
            dyv = dyv.astype(F32)
            dt = dyv * cosv + _rope_partner(dyv * sinv)
            xv = p_ref[:, cols]
            rstd = lax.rsqrt(jnp.mean(xv * xv, axis=-1, keepdims=True) + EPS)
            nrm = xv * rstd
            dgain_ref[...] += jnp.sum(dt * nrm, axis=0, keepdims=True)
            dn = dt * gain
            dp_ref[:, cols] = (rstd * (dn - nrm * jnp.mean(dn * nrm, axis=-1, keepdims=True))).astype(dp_ref.dtype)

    vec = _full((1, HEAD_DIM))
    tab = pl.BlockSpec((tr, HEAD_DIM), lambda i: (i, 0))
    operands = ([dq] if has_q else []) + [dk, proj, q_g, k_g, cos, sin]
    specs = ([pl.BlockSpec((tr, ATTN_W), lambda i: (i, 0))] if has_q else []) + [
        pl.BlockSpec((tr, KV_W), lambda i: (i + dk_row0 // tr, 0)), pl.BlockSpec((tr, width), lambda i: (i, 0)), vec, vec, tab, tab]
    return pl.pallas_call(
        body, name=name, grid=(rows // tr,), in_specs=specs,
        out_specs=(pl.BlockSpec((tr, width), lambda i: (i, 0)), vec, vec),
        out_shape=(jax.ShapeDtypeStruct((rows, width), BF16), jax.ShapeDtypeStruct((1, HEAD_DIM), F32),
                   jax.ShapeDtypeStruct((1, HEAD_DIM), F32)),
        compiler_params=_params(("arbitrary",)),
    )(*operands)


def _attn_fwd(q, k, v):
    n_q, n_k = q.shape[0], k.shape[0]
    tq = _tile(n_q, 512, 16)
    gw = GROUP * HEAD_DIM
    scale = HEAD_DIM ** -0.5

    def body(q_ref, k_ref, v_ref, o_ref, lse_ref):
        kv, vv = k_ref[...], v_ref[...]
        for g in range(GROUP):
            cols = slice(g * HEAD_DIM, (g + 1) * HEAD_DIM)
            s = lax.dot_general(q_ref[:, cols], kv, (((1,), (1,)), ((), ())), preferred_element_type=F32) * (scale * LOG2E)
            m = jnp.max(s, axis=-1, keepdims=True)
            p = jnp.exp2(s - m)
            l = jnp.sum(p, axis=-1, keepdims=True)
            o = jnp.dot(p.astype(BF16), vv, preferred_element_type=F32) / l
            o_ref[:, cols] = o.astype(o_ref.dtype)
            lse_ref[:, g:g + 1] = m + jnp.log(l) * LOG2E

    return pl.pallas_call(
        body, name="attn_fwd", grid=(N_KV, n_q // tq),
        in_specs=[pl.BlockSpec((tq, gw), lambda h, i: (i, h)), pl.BlockSpec((n_k, HEAD_DIM), lambda h, i: (0, h)),
                  pl.BlockSpec((n_k, HEAD_DIM), lambda h, i: (0, h))],
        out_specs=(pl.BlockSpec((tq, gw), lambda h, i: (i, h)), pl.BlockSpec((None, tq, GROUP), lambda h, i: (h, i, 0))),
        out_shape=(jax.ShapeDtypeStruct((n_q, ATTN_W + D_RNN), BF16), jax.ShapeDtypeStruct((N_KV, n_q, GROUP), F32)),
        compiler_params=_params(("parallel", "parallel")),
    )(q, k, v)


def _attn_bwd(q, k, v, o, lse, do):
    n_q, n_k = q.shape[0], k.shape[0]
    tq = _tile(n_q, 256, 16)
    gw = GROUP * HEAD_DIM
    scale = HEAD_DIM ** -0.5

    def body(q_ref, k_ref, v_ref, o_ref, lse_ref, do_ref, dq_ref, dk_ref, dv_ref):
        @pl.when(pl.program_id(1) == 0)
        def _():
            dk_ref[...] = jnp.zeros_like(dk_ref)
            dv_ref[...] = jnp.zeros_like(dv_ref)

        kv, vv = k_ref[...], v_ref[...]
        for g in range(GROUP):
            cols = slice(g * HEAD_DIM, (g + 1) * HEAD_DIM)
            qg = q_ref[:, cols]
            dof = do_ref[:, cols].astype(F32)
            dog = dof.astype(BF16)
            s = lax.dot_general(qg, kv, (((1,), (1,)), ((), ())), preferred_element_type=F32) * (scale * LOG2E)
            p = jnp.exp2(s - lse_ref[:, g:g + 1])
            delta = jnp.sum(dof * o_ref[:, cols].astype(F32), axis=-1, keepdims=True)
            dp = lax.dot_general(dog, vv, (((1,), (1,)), ((), ())), preferred_element_type=F32)
            ds = (p * (dp - delta) * scale).astype(BF16)
            pb = p.astype(BF16)
            dq_ref[:, cols] = jnp.dot(ds, kv, preferred_element_type=F32)
            dk_ref[...] += lax.dot_general(ds, qg, (((0,), (0,)), ((), ())), preferred_element_type=F32)
            dv_ref[...] += lax.dot_general(pb, dog, (((0,), (0,)), ((), ())), preferred_element_type=F32)

    qspec = pl.BlockSpec((tq, gw), lambda h, i: (i, h))
    kspec = pl.BlockSpec((n_k, HEAD_DIM), lambda h, i: (0, h))
    return pl.pallas_call(
        body, name="attn_bwd", grid=(N_KV, n_q // tq),
        in_specs=[qspec, kspec, kspec, qspec, pl.BlockSpec((None, tq, GROUP), lambda h, i: (h, i, 0)), qspec],
        out_specs=(qspec, kspec, kspec),
        out_shape=(jax.ShapeDtypeStruct((n_q, ATTN_W), F32), jax.ShapeDtypeStruct((n_k, KV_W), F32),
                   jax.ShapeDtypeStruct((n_k, KV_W), F32)),
        compiler_params=_params(("parallel", "arbitrary")),
    )(q, k, v, o, lse, do)


CONV_COLS = 256
XR_COL0 = ATTN_W + 2 * KV_W


def _shift_rows(v, off):
    if off == 0:
        return v
    n = v.shape[0]
    rolled = pltpu.roll(v, (-off) % n, 0)
    t = lax.broadcasted_iota(jnp.int32, v.shape, 0)
    keep = (t + off >= 0) & (t + off < n)
    return jnp.where(keep, rolled, 0.0)


def _conv_fwd(proj_l, proj_c, w, b):
    n_lat, n_ctx = proj_l.shape[0], proj_c.shape[0]
    blk0 = XR_COL0 // CONV_COLS

    def body(xl_ref, xc_ref, w_ref, b_ref, y_ref):
        for x_ref, rows in ((xc_ref, slice(0, n_ctx)), (xl_ref, slice(n_ctx, n_ctx + n_lat))):
            xv = x_ref[...]
            y = b_ref[...] + jnp.zeros_like(xv)
            for j in range(CONV_W):
                y = y + _shift_rows(xv, j - CONV_W // 2) * w_ref[j:j + 1, :]
            y_ref[rows, :] = y

    return pl.pallas_call(
        body, name="conv_fwd", grid=(D_RNN // CONV_COLS,),
        in_specs=[pl.BlockSpec((n_lat, CONV_COLS), lambda i: (0, blk0 + i)), pl.BlockSpec((n_ctx, CONV_COLS), lambda i: (0, blk0 + i)),
                  pl.BlockSpec((CONV_W, CONV_COLS), lambda i: (0, i)), pl.BlockSpec((1, CONV_COLS), lambda i: (0, i))],
        out_specs=pl.BlockSpec((n_ctx + n_lat, CONV_COLS), lambda i: (0, i)),
        out_shape=jax.ShapeDtypeStruct((n_ctx + n_lat, D_RNN), F32), compiler_params=_params(("parallel",)),
    )(proj_l, proj_c, w, b)


def _conv_bwd(d1, d2, proj_l, proj_c, w):
    n_lat, n_ctx = proj_l.shape[0], proj_c.shape[0]
    blk0 = XR_COL0 // CONV_COLS

    def body(d1_ref, d2_ref, xl_ref, xc_ref, w_ref, dxl_ref, dxc_ref, dw_ref, db_ref):
        dw = [0.0] * CONV_W
        db = 0.0
        for x_ref, dx_ref, rows in ((xc_ref, dxc_ref, slice(0, n_ctx)), (xl_ref, dxl_ref, slice(n_ctx, n_ctx + n_lat))):
            dv = d1_ref[rows, :] + d2_ref[rows, :]
            xv = x_ref[...]
            dx = jnp.zeros_like(dv)
            for j in range(CONV_W):
                off = j - CONV_W // 2
                dx = dx + _shift_rows(dv, -off) * w_ref[j:j + 1, :]
                dw[j] = dw[j] + jnp.sum(dv * _shift_rows(xv, off), axis=0, keepdims=True)
            dx_ref[...] = dx.astype(dx_ref.dtype)
            db = db + jnp.sum(dv, axis=0, keepdims=True)
        for j in range(CONV_W):
            dw_ref[j:j + 1, :] = dw[j]
        db_ref[...] = db

    both = pl.BlockSpec((n_ctx + n_lat, CONV_COLS), lambda i: (0, i))
    return pl.pallas_call(
        body, name="conv_bwd", grid=(D_RNN // CONV_COLS,),
        in_specs=[both, both, pl.BlockSpec((n_lat, CONV_COLS), lambda i: (0, blk0 + i)),
                  pl.BlockSpec((n_ctx, CONV_COLS), lambda i: (0, blk0 + i)), pl.BlockSpec((CONV_W, CONV_COLS), lambda i: (0, i))],
        out_specs=(pl.BlockSpec((n_lat, CONV_COLS), lambda i: (0, i)), pl.BlockSpec((n_ctx, CONV_COLS), lambda i: (0, i)),
                   pl.BlockSpec((CONV_W, CONV_COLS), lambda i: (0, i)), pl.BlockSpec((1, CONV_COLS), lambda i: (0, i))),
        out_shape=(jax.ShapeDtypeStruct((n_lat, D_RNN), BF16), jax.ShapeDtypeStruct((n_ctx, D_RNN), BF16),
                   jax.ShapeDtypeStruct((CONV_W, D_RNN), F32), jax.ShapeDtypeStruct((1, D_RNN), F32)),
        compiler_params=_params(("parallel",)),
    )(d1, d2, proj_l, proj_c, w)


RNN_TB = 256
SCAN_ROWS = 8


def _sigmoid(z):
    return 1.0 / (1.0 + jnp.exp(-z))


def _softplus(z):
    return jnp.maximum(z, 0.0) + jnp.log(1.0 + jnp.exp(-jnp.abs(z)))


def _one_minus_exp(y):
    series = -y * (1.0 + y * (0.5 + y * (1.0 / 6.0 + y * (1.0 / 24.0))))
    return jnp.where(y > -0.03, series, 1.0 - jnp.exp(y))


def _rglru_gates(xv, wa_ref, ba_ref, wx_ref, bx_ref, lam_ref):
    xb = xv.astype(BF16)
    zr = jnp.concatenate([jnp.dot(xb[:, n * RNN_BW:(n + 1) * RNN_BW], wa_ref[n].astype(BF16),
                                  preferred_element_type=F32) for n in range(RNN_BLOCKS)], axis=-1) + ba_ref[...]
    zi = jnp.concatenate([jnp.dot(xb[:, n * RNN_BW:(n + 1) * RNN_BW], wx_ref[n].astype(BF16),
                                  preferred_element_type=F32) for n in range(RNN_BLOCKS)], axis=-1) + bx_ref[...]
    r = _sigmoid(zr)
    gi = _sigmoid(zi)
    sp = _softplus(-lam_ref[...])
    log_a = -RG_C * r * sp
    a = jnp.exp(log_a)
    s = jnp.sqrt(_one_minus_exp(2.0 * log_a))
    return r, gi, sp, a, s


def _scan_rows(n_rows, reverse, step_fn, carry):
    groups = n_rows // SCAN_ROWS

    def trip(gidx, carry):
        gi = (groups - 1 - gidx) if reverse else gidx
        base = pl.multiple_of(gi * SCAN_ROWS, SCAN_ROWS)
        return step_fn(base, carry)

    return lax.fori_loop(0, groups, trip, carry)


def _scan_block_order(nb, nb_c, reverse, adjoint):
    if not reverse:
        return (lambda i: nb - 1 - i) if adjoint else (lambda i: i)
    if adjoint:
        return lambda i: jnp.where(i < nb - nb_c, nb_c + i, i - (nb - nb_c))
    return lambda i: jnp.where(i < nb_c, nb_c - 1 - i, nb + nb_c - 1 - i)


def _rglru_fwd(xs, wa, ba, wx, bx, lam, *, reverse, n_ctx, name):
    rows = xs.shape[0]
    tb = _tile(math.gcd(rows, n_ctx), RNN_TB, SCAN_ROWS)
    nb = rows // tb
    block_of = _scan_block_order(nb, n_ctx // tb, reverse, False)
    order = lambda i: (block_of(i), 0)

    def body(x_ref, wa_ref, ba_ref, wx_ref, bx_ref, lam_ref, h_ref, hp_ref, a_s, b_s, state):
        @pl.when(pl.program_id(0) == 0)
        def _():
            state[...] = jnp.zeros_like(state)

        xv = x_ref[...]
        _, gi, _, a, s = _rglru_gates(xv, wa_ref, ba_ref, wx_ref, bx_ref, lam_ref)
        a_s[...] = a
        b_s[...] = s * (gi * xv)

        def group(base, h):
            av = a_s[pl.ds(base, SCAN_ROWS), :]
            bv = b_s[pl.ds(base, SCAN_ROWS), :]
            outs, prevs = [None] * SCAN_ROWS, [None] * SCAN_ROWS
            for k in range(SCAN_ROWS):
                r_ = SCAN_ROWS - 1 - k if reverse else k
                prevs[r_] = h
                h = av[r_:r_ + 1, :] * h + bv[r_:r_ + 1, :]
                outs[r_] = h
            h_ref[pl.ds(base, SCAN_ROWS), :] = jnp.concatenate(outs, axis=0)
            hp_ref[pl.ds(base, SCAN_ROWS), :] = jnp.concatenate(prevs, axis=0)
            return h

        state[0:1, :] = _scan_rows(tb, reverse, group, state[0:1, :])

    blk = pl.BlockSpec((tb, D_RNN), order)
    wspec = _full((RNN_BLOCKS, RNN_BW, RNN_BW))
    vec = _full((1, D_RNN))
    return pl.pallas_call(
        body, name=name, grid=(nb,), in_specs=[blk, wspec, vec, wspec, vec, vec], out_specs=(blk, blk),
        out_shape=(jax.ShapeDtypeStruct((rows, D_RNN), F32), jax.ShapeDtypeStruct((rows, D_RNN), F32)),
        scratch_shapes=[pltpu.VMEM((tb, D_RNN), F32), pltpu.VMEM((tb, D_RNN), F32), pltpu.VMEM((SCAN_ROWS, D_RNN), F32)],
        compiler_params=_params(("arbitrary",)),
    )(xs, wa, ba, wx, bx, lam)


def _rglru_bwd(xs, h_prev, dh, wa, ba, wx, bx, lam, *, reverse, n_ctx, name):
    rows = xs.shape[0]
    tb = _tile(math.gcd(rows, n_ctx), RNN_TB, SCAN_ROWS)
    nb, nb_c = rows // tb, n_ctx // tb
    back = not reverse
    block_of = _scan_block_order(nb, nb_c, reverse, True)
    order = lambda i: (block_of(i), 0)

    def body(x_ref, hp_ref, dh_ref, wa_ref, ba_ref, wx_ref, bx_ref, lam_ref,
             dx_ref, dwa_ref, dba_ref, dwx_ref, dbx_ref, dlam_ref, a_s, g_s, state):
        @pl.when(pl.program_id(0) == 0)
        def _():
            state[...] = jnp.zeros_like(state)
            dwa_ref[...] = jnp.zeros_like(dwa_ref)
            dwx_ref[...] = jnp.zeros_like(dwx_ref)
            dba_ref[...] = jnp.zeros_like(dba_ref)
            dbx_ref[...] = jnp.zeros_like(dbx_ref)
            dlam_ref[...] = jnp.zeros_like(dlam_ref)

        xv = x_ref[...]
        r, gi, sp, a, s = _rglru_gates(xv, wa_ref, ba_ref, wx_ref, bx_ref, lam_ref)
        a_s[...] = a

        is_latent = block_of(pl.program_id(0)) >= nb_c

        def group(base, carry):
            av = a_s[pl.ds(base, SCAN_ROWS), :]
            dv = jnp.where(is_latent, dh_ref[pl.ds(base, SCAN_ROWS), :], 0.0)
            outs = [None] * SCAN_ROWS
            for k in range(SCAN_ROWS):
                r_ = SCAN_ROWS - 1 - k if back else k
                gt = dv[r_:r_ + 1, :] + carry
                outs[r_] = gt
                carry = av[r_:r_ + 1, :] * gt
            g_s[pl.ds(base, SCAN_ROWS), :] = jnp.concatenate(outs, axis=0)
            return carry

        state[0:1, :] = _scan_rows(tb, back, group, state[0:1, :])

        gv = g_s[...]
        d_a = gv * hp_ref[...]
        d_s = gv * (gi * xv)
        d_gi = gv * (s * xv)
        dx = gv * (s * gi)
        d_log_a = d_a * a - d_s * (a * a) / s
        d_r = d_log_a * (-RG_C * sp)
        lamv = lam_ref[...]
        d_sp = jnp.sum(d_log_a * (-RG_C * r), axis=0, keepdims=True)
        dlam_ref[...] += d_sp * (-_sigmoid(-lamv))
        d_zr = d_r * r * (1.0 - r)
        d_zi = d_gi * gi * (1.0 - gi)
        dba_ref[...] += jnp.sum(d_zr, axis=0, keepdims=True)
        dbx_ref[...] += jnp.sum(d_zi, axis=0, keepdims=True)
        xb = xv.astype(BF16)
        zrb, zib = d_zr.astype(BF16), d_zi.astype(BF16)
        parts = []
        for n in range(RNN_BLOCKS):
            cols = slice(n * RNN_BW, (n + 1) * RNN_BW)
            dwa_ref[n] += lax.dot_general(xb[:, cols], zrb[:, cols], (((0,), (0,)), ((), ())), preferred_element_type=F32)
            dwx_ref[n] += lax.dot_general(xb[:, cols], zib[:, cols], (((0,), (0,)), ((), ())), preferred_element_type=F32)
            parts.append(
                lax.dot_general(zrb[:, cols], wa_ref[n].astype(BF16), (((1,), (1,)), ((), ())), preferred_element_type=F32)
                + lax.dot_general(zib[:, cols], wx_ref[n].astype(BF16), (((1,), (1,)), ((), ())), preferred_element_type=F32))
        dx_ref[...] = dx + jnp.concatenate(parts, axis=-1)

    blk = pl.BlockSpec((tb, D_RNN), order)
    wspec = _full((RNN_BLOCKS, RNN_BW, RNN_BW))
    vec = _full((1, D_RNN))
    wshape = jax.ShapeDtypeStruct((RNN_BLOCKS, RNN_BW, RNN_BW), F32)
    vshape = jax.ShapeDtypeStruct((1, D_RNN), F32)
    dh_blk = pl.BlockSpec((tb, D_RNN), lambda i: (jnp.maximum(block_of(i) - nb_c, 0), 0))
    return pl.pallas_call(
        body, name=name, grid=(nb,), in_specs=[blk, blk, dh_blk, wspec, vec, wspec, vec, vec],
        out_specs=(blk, wspec, vec, wspec, vec, vec),
        out_shape=(jax.ShapeDtypeStruct((rows, D_RNN), F32), wshape, vshape, wshape, vshape, vshape),
        scratch_shapes=[pltpu.VMEM((tb, D_RNN), F32), pltpu.VMEM((tb, D_RNN), F32), pltpu.VMEM((SCAN_ROWS, D_RNN), F32)],
        compiler_params=_params(("arbitrary",)),
    )(xs, h_prev, dh, wa, ba, wx, bx, lam)


def _assemble_d_proj(dp_qk_l, dp_qk_c, dv_all, d_xr_l, d_xr_c, d_gate):
    n_lat, n_ctx = dp_qk_l.shape[0], dp_qk_c.shape[0]
    tr = _tile(math.gcd(n_lat, n_ctx), 256, 16)
    nb_l, nb_c = n_lat // tr, n_ctx // tr
    w_qk = ATTN_W + KV_W

    def body(ql_ref, qc_ref, dv_ref, xl_ref, xc_ref, g_ref, o_ref):
        i = pl.program_id(0)
        o_ref[:, w_qk:XR_COL0] = dv_ref[...].astype(o_ref.dtype)

        @pl.when(i < nb_l)
        def _():
            o_ref[:, :w_qk] = ql_ref[...]
            o_ref[:, XR_COL0:GATE_COL0] = xl_ref[...]
            o_ref[:, GATE_COL0:] = g_ref[...]

        @pl.when(i >= nb_l)
        def _():
            o_ref[:, :w_qk] = qc_ref[...]
            o_ref[:, XR_COL0:GATE_COL0] = xc_ref[...]
            o_ref[:, GATE_COL0:] = jnp.zeros((tr, D_RNN), o_ref.dtype)

    lat = lambda i: (jnp.minimum(i, nb_l - 1), 0)
    ctx = lambda i: (jnp.maximum(i - nb_l, 0), 0)
    return pl.pallas_call(
        body, name="assemble_d_proj", grid=(nb_l + nb_c,),
        in_specs=[pl.BlockSpec((tr, w_qk), lat), pl.BlockSpec((tr, w_qk), ctx),
                  pl.BlockSpec((tr, KV_W), lambda i: (jnp.where(i < nb_l, i + nb_c, i - nb_l), 0)),
                  pl.BlockSpec((tr, D_RNN), lat), pl.BlockSpec((tr, D_RNN), ctx), pl.BlockSpec((tr, D_RNN), lat)],
        out_specs=pl.BlockSpec((tr, GATE_COL0 + D_RNN), lambda i: (i, 0)),
        out_shape=jax.ShapeDtypeStruct((n_lat + n_ctx, GATE_COL0 + D_RNN), BF16),
        compiler_params=_params(("parallel",)),
    )(dp_qk_l, dp_qk_c, dv_all, d_xr_l, d_xr_c, d_gate)


def _gelu(z):
    return 0.5 * z * (1.0 + jnp.tanh(GELU_C * (z + 0.044715 * z * z * z)))


def _gelu_grad(z):
    t = jnp.tanh(GELU_C * (z + 0.044715 * z * z * z))
    return 0.5 * (1.0 + t) + 0.5 * z * (1.0 - t * t) * (GELU_C * (1.0 + 3.0 * 0.044715 * z * z))


GATE_COL0 = XR_COL0 + D_RNN


RNN_OUT_COLS = 512


def _rnn_out_specs(rows, hf_off, hb_off):
    tr = _tile(rows, 256, 16)
    assert hf_off % tr == 0 and hb_off % tr == 0 and GATE_COL0 % RNN_OUT_COLS == 0
    fo, bo, go = hf_off // tr, hb_off // tr, GATE_COL0 // RNN_OUT_COLS
    hf_spec = pl.BlockSpec((tr, RNN_OUT_COLS), lambda i, j: (i + fo, j))
    hb_spec = pl.BlockSpec((tr, RNN_OUT_COLS), lambda i, j: (i + bo, j))
    gate_spec = pl.BlockSpec((tr, RNN_OUT_COLS), lambda i, j: (i, j + go))
    out_spec = pl.BlockSpec((tr, RNN_OUT_COLS), lambda i, j: (i, j))
    return (rows // tr, D_RNN // RNN_OUT_COLS), hf_spec, hb_spec, gate_spec, out_spec


def _rnn_out_fwd(hf, hb, proj, hf_off, hb_off, cat):
    rows = proj.shape[0]
    grid, hf_spec, hb_spec, gate_spec, out_spec = _rnn_out_specs(rows, hf_off, hb_off)
    tr, col0 = out_spec.block_shape[0], ATTN_W // RNN_OUT_COLS

    def body(hf_ref, hb_ref, g_ref, _, o_ref):
        o_ref[...] = ((hf_ref[...] + hb_ref[...]) * _gelu(g_ref[...])).astype(o_ref.dtype)

    return pl.pallas_call(
        body, name="rnn_out_fwd", grid=grid, in_specs=[hf_spec, hb_spec, gate_spec, ANY],
        out_specs=pl.BlockSpec((tr, RNN_OUT_COLS), lambda i, j: (i, j + col0)),
        out_shape=jax.ShapeDtypeStruct(cat.shape, cat.dtype), input_output_aliases={3: 0},
        compiler_params=_params(("parallel", "parallel")),
    )(hf, hb, proj, cat)


def _rnn_out_bwd(d_cat, hf, hb, proj, hf_off, hb_off):
    rows = proj.shape[0]
    grid, hf_spec, hb_spec, gate_spec, out_spec = _rnn_out_specs(rows, hf_off, hb_off)
    do = ATTN_W // RNN_OUT_COLS

    def body(d_ref, hf_ref, hb_ref, g_ref, dh_ref, dg_ref):
        dv, gv = d_ref[...].astype(F32), g_ref[...]
        dh_ref[...] = dv * _gelu(gv)
        dg_ref[...] = (dv * (hf_ref[...] + hb_ref[...]) * _gelu_grad(gv)).astype(dg_ref.dtype)

    tr = out_spec.block_shape[0]
    return pl.pallas_call(
        body, name="rnn_out_bwd", grid=grid,
        in_specs=[pl.BlockSpec((tr, RNN_OUT_COLS), lambda i, j: (i, j + do)), hf_spec, hb_spec, gate_spec],
        out_specs=(out_spec, out_spec),
        out_shape=(jax.ShapeDtypeStruct((rows, D_RNN), F32), jax.ShapeDtypeStruct((rows, D_RNN), BF16)),
        compiler_params=_params(("parallel", "parallel")),
    )(d_cat, hf, hb, proj)


def _gmlp_parts(z_ref, vg_ref, vb_ref, d_gm):
    zu, zv = z_ref[:, :d_gm], z_ref[:, d_gm:]
    u = _gelu(zu)
    v = _gelu(zv)
    mu = jnp.mean(v, axis=-1, keepdims=True)
    vc = v - mu
    rstd = lax.rsqrt(jnp.mean(vc * vc, axis=-1, keepdims=True) + EPS)
    vhat = vc * rstd
    vn = vhat * vg_ref[...] + vb_ref[...]
    return zu, zv, u, vhat, rstd, vn


def _gmlp_fwd(z, v_g, v_b, w_sp, b_sp_t):
    rows, d_gm = z.shape[0], z.shape[1] // 2
    tr = _tile(rows, 256, CHUNK)
    gwid = d_gm // GM_GROUPS

    def body(z_ref, vg_ref, vb_ref, w_ref, b_ref, o_ref):
        _, _, u, _, _, vn = _gmlp_parts(z_ref, vg_ref, vb_ref, d_gm)
        vnb = vn.astype(BF16)
        for g in range(GM_GROUPS):
            wg = w_ref[g].astype(BF16)
            for c in range(tr // CHUNK):
                rs, cs = slice(c * CHUNK, (c + 1) * CHUNK), slice(g * gwid, (g + 1) * gwid)
                sv = jnp.dot(wg, vnb[rs, cs], preferred_element_type=F32) + b_ref[:, g:g + 1]
                o_ref[rs, cs] = (u[rs, cs] * sv).astype(o_ref.dtype)

    return pl.pallas_call(
        body, name="gmlp_fwd", grid=(rows // tr,),
        in_specs=[pl.BlockSpec((tr, 2 * d_gm), lambda i: (i, 0)), _full((1, d_gm)), _full((1, d_gm)),
                  _full(w_sp.shape), _full(b_sp_t.shape)],
        out_specs=pl.BlockSpec((tr, d_gm), lambda i: (i, 0)),
        out_shape=jax.ShapeDtypeStruct((rows, d_gm), BF16), compiler_params=_params(("parallel",)),
    )(z, v_g, v_b, w_sp, b_sp_t)


def _gmlp_bwd(z, dgate, v_g, v_b, w_sp, b_sp_t):
    rows, d_gm = z.shape[0], z.shape[1] // 2
    tr = _tile(rows, 256, CHUNK)
    gwid = d_gm // GM_GROUPS

    def body(z_ref, dg_ref, vg_ref, vb_ref, w_ref, b_ref, dz_ref, dbin_ref, dvg_ref, dvb_ref, dw_ref, dbs_ref, dvn_s):
        @pl.when(pl.program_id(0) == 0)
        def _():
            dbin_ref[...] = jnp.zeros_like(dbin_ref)
            dvg_ref[...] = jnp.zeros_like(dvg_ref)
            dvb_ref[...] = jnp.zeros_like(dvb_ref)
            dw_ref[...] = jnp.zeros_like(dw_ref)
            dbs_ref[...] = jnp.zeros_like(dbs_ref)

        zu, zv, u, vhat, rstd, vn = _gmlp_parts(z_ref, vg_ref, vb_ref, d_gm)
        vnb = vn.astype(BF16)
        dgv = dg_ref[...].astype(F32)
        dsv = dgv * u
        dsvb = dsv.astype(BF16)
        for g in range(GM_GROUPS):
            wg = w_ref[g].astype(BF16)
            cs = slice(g * gwid, (g + 1) * gwid)
            for c in range(tr // CHUNK):
                rs = slice(c * CHUNK, (c + 1) * CHUNK)
                sv = jnp.dot(wg, vnb[rs, cs], preferred_element_type=F32) + b_ref[:, g:g + 1]
                dz_ref[rs, cs] = (dgv[rs, cs] * sv * _gelu_grad(zu[rs, cs])).astype(dz_ref.dtype)
                dw_ref[g] += lax.dot_general(dsvb[rs, cs], vnb[rs, cs], (((1,), (1,)), ((), ())),
                                             preferred_element_type=F32)
                dbs_ref[:, g:g + 1] += jnp.sum(dsv[rs, cs], axis=-1, keepdims=True)
                dvn_s[rs, cs] = lax.dot_general(wg, dsvb[rs, cs], (((0,), (0,)), ((), ())), preferred_element_type=F32)
        dvn = dvn_s[...]
        dvg_ref[...] += jnp.sum(dvn * vhat, axis=0, keepdims=True)
        dvb_ref[...] += jnp.sum(dvn, axis=0, keepdims=True)
        dvh = dvn * vg_ref[...]
        dv = rstd * (dvh - jnp.mean(dvh, axis=-1, keepdims=True) - vhat * jnp.mean(dvh * vhat, axis=-1, keepdims=True))
        dzv = dv * _gelu_grad(zv)
        dz_ref[:, d_gm:] = dzv.astype(dz_ref.dtype)
        dbin_ref[:, d_gm:] += jnp.sum(dzv, axis=0, keepdims=True)
        dbin_ref[:, :d_gm] += jnp.sum(dz_ref[:, :d_gm].astype(F32), axis=0, keepdims=True)

    return pl.pallas_call(
        body, name="gmlp_bwd", grid=(rows // tr,),
        in_specs=[pl.BlockSpec((tr, 2 * d_gm), lambda i: (i, 0)), pl.BlockSpec((tr, d_gm), lambda i: (i, 0)),
                  _full((1, d_gm)), _full((1, d_gm)), _full(w_sp.shape), _full(b_sp_t.shape)],
        out_specs=(pl.BlockSpec((tr, 2 * d_gm), lambda i: (i, 0)), _full((1, 2 * d_gm)), _full((1, d_gm)),
                   _full((1, d_gm)), _full(w_sp.shape), _full(b_sp_t.shape)),
        out_shape=(jax.ShapeDtypeStruct((rows, 2 * d_gm), BF16), jax.ShapeDtypeStruct((1, 2 * d_gm), F32),
                   jax.ShapeDtypeStruct((1, d_gm), F32), jax.ShapeDtypeStruct((1, d_gm), F32),
                   jax.ShapeDtypeStruct(w_sp.shape, F32), jax.ShapeDtypeStruct(b_sp_t.shape, F32)),
        scratch_shapes=[pltpu.VMEM((tr, d_gm), F32)],
        compiler_params=_params(("arbitrary",)),
    )(z, dgate, v_g, v_b, w_sp, b_sp_t)


def _adamw_math(w, g, m, v):
    m = ADAM_B1 * m + (1.0 - ADAM_B1) * g
    v = ADAM_B2 * v + (1.0 - ADAM_B2) * (g * g)
    m_hat = m / (1.0 - ADAM_B1 ** ADAM_STEP)
    v_hat = v / (1.0 - ADAM_B2 ** ADAM_STEP)
    delta = -ADAM_LR * (m_hat / (jnp.sqrt(v_hat) + ADAM_EPS) + ADAM_WD * w)
    return delta, m, v


def _adamw(w, g, m, v, name, rewrite_grad=False):
    shape = w.shape
    if rewrite_grad:
        outs = _rowwise(lambda w_, g_, m_, v_: (g_,) + _adamw_math(w_, g_, m_, v_), (F32,) * 4, _as2d(w), _as2d(g), _as2d(m),
                        _as2d(v), name=name)
        return tuple(o.reshape(shape) for o in outs)
    outs = _rowwise(_adamw_math, (F32, F32, F32), _as2d(w), _as2d(g), _as2d(m), _as2d(v), name=name)
    return (g.reshape(shape),) + tuple(o.reshape(shape) for o in outs)


PACK_COLS = 1024


def _pack(arrays, dtype=F32):
    flat = jnp.concatenate([a.reshape(-1).astype(dtype) for a in arrays])
    pad = (-flat.size) % (16 * PACK_COLS)
    return jnp.pad(flat, (0, pad)).reshape(-1, PACK_COLS)


def _into_slot(pack, dev, name):
    rows, cols = pack.shape
    tr = _rows_tile(rows, cols, budget=512 * 1024)

    def body(dev_ref, p_ref, o_ref):
        o_ref[...] = p_ref[...]

    return pl.pallas_call(
        body, name=name, out_shape=jax.ShapeDtypeStruct((N_DEV, rows, cols), pack.dtype),
        grid_spec=pltpu.PrefetchScalarGridSpec(
            num_scalar_prefetch=1, grid=(rows // tr,), in_specs=[pl.BlockSpec((tr, cols), lambda i, dv: (i, 0))],
            out_specs=pl.BlockSpec((None, tr, cols), lambda i, dv: (dv[0], i, 0))),
        compiler_params=_params(("parallel",)),
    )(dev, pack)


def _unpack(flat, shapes):
    out, pos = [], 0
    for shp in shapes:
        n = math.prod(shp)
        out.append(flat[pos:pos + n].reshape(shp))
        pos += n
    return out


def _unpack_devices(packed8, shapes):
    flat8 = packed8.reshape(N_DEV, -1)
    out, pos = [], 0
    for shp in shapes:
        n = math.prod(shp)
        out.append(flat8[:, pos:pos + n].reshape((N_DEV,) + tuple(shp)))
        pos += n
    return out


def _sum_devices(g8):
    _, rows, cols = g8.shape
    tr = _rows_tile(rows, cols, budget=256 * 1024)

    def body(g_ref, o_ref):
        acc = g_ref[0].astype(F32)
        for d in range(1, N_DEV):
            acc = acc + g_ref[d].astype(F32)
        o_ref[...] = acc

    return pl.pallas_call(
        body, name="sum_devices", grid=(rows // tr,), in_specs=[pl.BlockSpec((N_DEV, tr, cols), lambda i: (0, i, 0))],
        out_specs=pl.BlockSpec((tr, cols), lambda i: (i, 0)), out_shape=jax.ShapeDtypeStruct((rows, cols), F32),
        compiler_params=_params(("parallel",)),
    )(g8)


def _place():
    return lax.axis_index("x"), lax.axis_index("y"), lax.axis_index("c")


def _other_chips(x, y):
    return [(1 - x, y), (x, 1 - y), (1 - x, 1 - y)]


def _remote(src, dst, send_sem, recv_sem, to):
    return pltpu.make_async_remote_copy(src_ref=src, dst_ref=dst, send_sem=send_sem, recv_sem=recv_sem, device_id=to,
                                        device_id_type=MESH)


def _comm_call(body, name, operands, out_shapes, n_remote, n_local, aliases=None):
    return pl.pallas_call(
        body, name=name, out_shape=tuple(out_shapes), in_specs=[ANY] * len(operands), out_specs=tuple(ANY for _ in out_shapes),
        scratch_shapes=[pltpu.SemaphoreType.DMA((n_remote,)), pltpu.SemaphoreType.DMA((n_remote,)),
                        pltpu.SemaphoreType.DMA((max(n_local, 1),))],
        input_output_aliases=aliases or {},
    )(*operands)


def _in_place(arrays):
    return [jax.ShapeDtypeStruct(a.shape, a.dtype) for a in arrays], {i: i for i in range(len(arrays))}


def _allgather8(arrs, name):
    n = len(arrs)

    def body(*refs):
        ins, outs = refs[:n], refs[n:2 * n]
        send, recv, lsem = refs[2 * n:]
        x, y, c = _place()
        me, sib = (x, y, c), (x, y, 1 - c)
        chips = _other_chips(x, y)

        def slot(t, px, py, pc):
            return outs[t].at[4 * px + 2 * py + pc]

        def cp(t, k, block, to, from_input=False):
            src = ins[t] if from_input else slot(t, *block)
            return _remote(src, slot(t, *block), send.at[7 * t + k], recv.at[7 * t + k], to)

        mine = [pltpu.make_async_copy(ins[t], slot(t, *me), lsem.at[t]) for t in range(n)]
        for cpy in mine:
            cpy.start()
        first = []
        for t in range(n):
            first.append(cp(t, 0, me, sib, True))
            first += [cp(t, 1 + j, me, (*chip, c), True) for j, chip in enumerate(chips)]
        for cpy in first:
            cpy.start()
        passed = []
        for t in range(n):
            for j, chip in enumerate(chips):
                cp(t, 1 + j, (*chip, c), me).wait_recv()
                fwd = cp(t, 4 + j, (*chip, c), sib)
                fwd.start()
                passed.append(fwd)
        for t in range(n):
            cp(t, 0, sib, me).wait_recv()
            for j, chip in enumerate(chips):
                cp(t, 4 + j, (*chip, 1 - c), me).wait_recv()
        for cpy in first + passed:
            cpy.wait_send()
        for cpy in mine:
            cpy.wait()

    outs = _comm_call(body, name, arrs, [jax.ShapeDtypeStruct((N_DEV,) + a.shape, a.dtype) for a in arrs], 7 * n, n)
    return list(outs)


def _join_halves(bufs):
    n = len(bufs)
    units = [(k, layer) for k in range(n) for layer in range(bufs[k].shape[0])]

    def body(*refs):
        bufs_ = refs[n:2 * n]
        send, recv, _ = refs[2 * n:]
        x, y, c = _place()
        sent = []
        for u, (k, layer) in enumerate(units):
            half = bufs_[k].shape[1] // 2
            mine = bufs_[k].at[layer, pl.ds(c * half, half)]
            cpy = _remote(mine, mine, send.at[u], recv.at[u], (x, y, 1 - c))
            cpy.start()
            sent.append(cpy)
        for u, (k, layer) in enumerate(units):
            half = bufs_[k].shape[1] // 2
            theirs = bufs_[k].at[layer, pl.ds((1 - c) * half, half)]
            _remote(theirs, theirs, send.at[u], recv.at[u], (x, y, c)).wait_recv()
        for cpy in sent:
            cpy.wait_send()

    shapes, aliases = _in_place(bufs)
    return list(_comm_call(body, "join_halves", bufs, shapes, len(units), 0, aliases))


def _add_halves(grad, other, place):
    _, rows, cols = grad.shape
    half = rows // 2
    tr = _rows_tile(half, cols, itemsize=2, budget=2 * 1024 * 1024)
    per_half = half // tr

    def body(place_ref, g_ref, o_ref, s_ref):
        s_ref[...] = (g_ref[...].astype(F32) + o_ref[...].astype(F32)).astype(s_ref.dtype)

    return pl.pallas_call(
        body, name="add_halves", out_shape=jax.ShapeDtypeStruct((N_CHIPS, half, cols), grad.dtype),
        grid_spec=pltpu.PrefetchScalarGridSpec(
            num_scalar_prefetch=1, grid=(N_CHIPS, per_half),
            in_specs=[pl.BlockSpec((None, tr, cols), lambda k, i, pr: (k, pr[1] * per_half + i, 0)),
                      pl.BlockSpec((None, tr, cols), lambda k, i, pr: (k, i, 0))],
            out_specs=pl.BlockSpec((None, tr, cols), lambda k, i, pr: (k, i, 0))),
        compiler_params=_params(("parallel", "parallel")),
    )(place, grad, other)


def _add_chips(sums, others, place, dest, layer, n_layers):
    _, half, cols = sums.shape
    tr = _rows_tile(half, cols, itemsize=4, budget=2 * 1024 * 1024)
    per_half = half // tr

    def body(place_ref, s_ref, o_ref, *rest):
        acc = s_ref[...].astype(F32)
        for j in range(N_CHIPS - 1):
            acc = acc + o_ref[j].astype(F32)
        rest[-1][...] = acc

    operands = [place, sums, others] + ([] if dest is None else [dest])
    return pl.pallas_call(
        body, name="add_chips", out_shape=jax.ShapeDtypeStruct((n_layers, 2 * half, cols), F32),
        grid_spec=pltpu.PrefetchScalarGridSpec(
            num_scalar_prefetch=1, grid=(per_half,),
            in_specs=[pl.BlockSpec((None, tr, cols), lambda i, pr: (pr[0], i, 0)),
                      pl.BlockSpec((N_CHIPS - 1, tr, cols), lambda i, pr: (0, i, 0))] + ([] if dest is None else [ANY]),
            out_specs=pl.BlockSpec((None, tr, cols), lambda i, pr: (layer, pr[1] * per_half + i, 0))),
        input_output_aliases={} if dest is None else {3: 0},
        compiler_params=_params(("parallel",)),
    )(*operands)


HBM = pl.BlockSpec(memory_space=pltpu.HBM)
SEM = pl.BlockSpec(memory_space=pltpu.SEMAPHORE)
DATAFLOW = pltpu.SideEffectType.DATAFLOW_SIDE_EFFECTING


def _split_start(name, bufs, copies, n_copies, after=None):
    n = len(bufs)
    extra = 0 if after is None else 1

    def body(*refs):
        for cpy in copies(refs[:n], refs[n + extra], refs[n + extra + 1]):
            cpy.start()
        refs[-1][...] = jnp.zeros_like(refs[-1])

    outs = pl.pallas_call(
        body, name=name,
        out_shape=(pltpu.SemaphoreType.DMA((n_copies,)), pltpu.SemaphoreType.DMA((n_copies,)),
                   *[pltpu.HBM(b.shape, b.dtype) for b in bufs], jax.ShapeDtypeStruct((8, LANES), F32)),
        in_specs=[HBM] * n + [ANY] * extra,
        out_specs=(SEM, SEM, *[HBM] * n, pl.BlockSpec(memory_space=pltpu.VMEM)),
        input_output_aliases={i: 2 + i for i in range(n)},
        compiler_params=pltpu.CompilerParams(has_side_effects=DATAFLOW),
    )(*[pltpu.with_memory_space_constraint(b, pltpu.HBM) for b in bufs], *([] if after is None else [after]))
    return outs[0], outs[1], list(outs[2:2 + n]), outs[-1]


def _split_wait(name, bufs, send, recv, copies, after):
    n = len(bufs)

    def body(*refs):
        for cpy in copies(refs[:n], refs[n], refs[n + 1]):
            cpy.wait_send()
            cpy.wait_recv()

    return list(pl.pallas_call(
        body, name=name, out_shape=tuple(pltpu.HBM(b.shape, b.dtype) for b in bufs),
        in_specs=[HBM] * n + [SEM, SEM, ANY], out_specs=tuple([HBM] * n),
        input_output_aliases={i: i for i in range(n)},
        compiler_params=pltpu.CompilerParams(has_side_effects=DATAFLOW),
    )(*bufs, send, recv, after))


def _gather_copies(bufs, send, recv):
    x, y, c = _place()
    out = []
    for u, buf in enumerate(bufs):
        half = buf.shape[1] // 2
        mine = buf.at[2 * x + y, pl.ds(c * half, half)]
        out += [_remote(mine, mine, send.at[3 * u + j], recv.at[3 * u + j], (*chip, c))
                for j, chip in enumerate(_other_chips(x, y))]
    return out


def _exchange_copies(bufs, send, recv):
    x, y, c = _place()
    n = len(bufs) // 2
    out = []
    for k in range(n):
        half = bufs[k].shape[1] // 2
        theirs = bufs[k].at[pl.ds(0, N_CHIPS), pl.ds((1 - c) * half, half)]
        out.append(_remote(theirs, bufs[n + k], send.at[k], recv.at[k], (x, y, 1 - c)))
    return out


def _all_to_all_copies(bufs, send, recv):
    x, y, c = _place()
    n = len(bufs) // 2
    return [_remote(bufs[k].at[2 * chip[0] + chip[1]], bufs[n + k].at[j], send.at[3 * k + j], recv.at[3 * k + j], (*chip, c))
            for k in range(n) for j, chip in enumerate(_other_chips(x, y))]


def _forward_copies(bufs, send, recv):
    x, y, c = _place()
    out = []
    for u, buf in enumerate(bufs):
        half = buf.shape[1] // 2
        for j, chip in enumerate(_other_chips(x, y)):
            landed = buf.at[2 * chip[0] + chip[1], pl.ds(c * half, half)]
            out.append(_remote(landed, landed, send.at[3 * u + j], recv.at[3 * u + j], (x, y, 1 - c)))
    return out


def _gather8_copies(bufs, send, recv):
    x, y, c = _place()
    targets = [(x, y, 1 - c)] + [(*chip, c) for chip in _other_chips(x, y)]
    out = []
    for b, buf in enumerate(bufs):
        mine = buf.at[4 * x + 2 * y + c]
        out += [_remote(mine, mine, send.at[N_CHIPS * b + k], recv.at[N_CHIPS * b + k], to) for k, to in enumerate(targets)]
    return out


def _forward_slots(bufs, name):
    n = len(bufs)

    def body(*refs):
        bufs_ = refs[n:2 * n]
        send, recv, _ = refs[2 * n:]
        x, y, c = _place()
        chips = _other_chips(x, y)
        sent = []
        for b in range(n):
            for j, chip in enumerate(chips):
                slot = bufs_[b].at[4 * chip[0] + 2 * chip[1] + c]
                cpy = _remote(slot, slot, send.at[3 * b + j], recv.at[3 * b + j], (x, y, 1 - c))
                cpy.start()
                sent.append(cpy)
        for b in range(n):
            for j, chip in enumerate(chips):
                slot = bufs_[b].at[4 * chip[0] + 2 * chip[1] + 1 - c]
                _remote(slot, slot, send.at[3 * b + j], recv.at[3 * b + j], (x, y, c)).wait_recv()
        for cpy in sent:
            cpy.wait_send()

    shapes, aliases = _in_place(bufs)
    return list(_comm_call(body, name, bufs, shapes, (N_CHIPS - 1) * n, 0, aliases))


FWD_GROUPS = {'mix': ('ar_out', 'ff_in0', 'ff_out0'), 'l1': ('gm_in', 'gm_out', 'ff_in1', 'ff_out1')}
GRAD_LAYOUT = {'ff_in0': (0, 0), 'ff_in1': (0, 1), 'ff_out0': (1, 0), 'ff_out1': (1, 1), 'ar_in': (2, 0), 'ar_out': (3, 0),
               'gm_in': (4, 0), 'gm_out': (5, 0)}


class _MeshLink:
    def __init__(self, place, shards):
        self.place = place
        self.ready = {'ar_in': shards['ar_in']}
        self.pending, after = {}, shards['ar_in']
        for group, names in FWD_GROUPS.items():
            send, recv, bufs, token = _split_start(f"gather_{group}_start", [shards[n] for n in names], _gather_copies,
                                                   3 * len(names), after)
            self.pending[group] = (names, send, recv, bufs)
            after = token
        self.start_token = after[0, 0]
        self.forwarding, self.exchanging, self.sent, self.last_token = {}, {}, {}, None

    def prefetch(self, group, after):
        names, send, recv, bufs = self.pending.pop(group)
        bufs = _split_wait(f"gather_{group}_wait", bufs, send, recv, _gather_copies, after)
        send, recv, bufs, token = _split_start(f"forward_{group}_start", bufs, _forward_copies, 3 * len(names))
        self.forwarding[group] = (names, send, recv, bufs)
        return token[0, 0]

    def weights(self, group, after):
        if group in self.forwarding:
            names, send, recv, bufs = self.forwarding.pop(group)
            self.ready.update(zip(names, _split_wait(f"forward_{group}_wait", bufs, send, recv, _forward_copies, after)))
        return self.ready

    def gradients(self, group, grads, after=None):
        tok = self.poll(next(iter(grads.values())))
        names, mine = list(grads), list(grads.values())
        landing = [lax.empty((N_CHIPS, g.shape[1] // 2, g.shape[2]), g.dtype) for g in mine]
        send, recv, bufs, token = _split_start(f"exchange_{group}_start", mine + landing, _exchange_copies, len(names), after)
        self.exchanging[group] = (names, send, recv, bufs)
        self.last_token = token
        return token[0, 0] + tok

    def poll(self, after):
        tok = 0.0
        for group in list(self.exchanging):
            names, send, recv, bufs = self.exchanging.pop(group)
            bufs = _split_wait(f"exchange_{group}_wait", bufs, send, recv, _exchange_copies, after)
            sums = [_add_halves(g, r, self.place) for g, r in zip(bufs[:len(names)], bufs[len(names):])]
            landing = [lax.empty((N_CHIPS - 1,) + s.shape[1:], s.dtype) for s in sums]
            send, recv, bufs, token = _split_start(f"grads_{group}_start", sums + landing, _all_to_all_copies, 3 * len(names))
            self.sent[group] = (names, send, recv, bufs)
            self.last_token = token
            tok = tok + token[0, 0]
        return tok

    def reduce(self, groups, after):
        units = {}
        for group in groups:
            names, send, recv, bufs = self.sent.pop(group)
            bufs = _split_wait(f"grads_{group}_wait", bufs, send, recv, _all_to_all_copies, after)
            units.update(zip(names, zip(bufs[:len(names)], bufs[len(names):])))
        n_layers = {p: 1 + max(l for pp, l in GRAD_LAYOUT.values() if pp == p) for p, _ in GRAD_LAYOUT.values()}
        out = {}
        for name, (p, layer) in GRAD_LAYOUT.items():
            if name in units:
                out[p] = _add_chips(*units[name], self.place, out.get(p), layer, n_layers[p])
        params = sorted(out)
        return dict(zip(params, _join_halves([out[p] for p in params])))


def _rope_tables(n):
    n_rows = n // GRID_W
    freqs = ROPE_THETA ** (-jnp.arange(ROPE_PAIRS, dtype=F32) / ROPE_PAIRS)
    ang_r = jnp.arange(n_rows, dtype=F32)[:, None] * freqs
    ang_c = jnp.arange(GRID_W, dtype=F32)[:, None] * freqs

    def per_token(of_row, of_col):
        r = jnp.broadcast_to(of_row[:, None, :], (n_rows, GRID_W, ROPE_PAIRS)).reshape(n, ROPE_PAIRS)
        c = jnp.broadcast_to(of_col[None, :, :], (n_rows, GRID_W, ROPE_PAIRS)).reshape(n, ROPE_PAIRS)
        return r, c

    cos_r, cos_c = per_token(jnp.cos(ang_r), jnp.cos(ang_c))
    sin_r, sin_c = per_token(jnp.sin(ang_r), jnp.sin(ang_c))
    cos = jnp.concatenate([cos_r, cos_r, cos_c, cos_c], axis=-1)
    sin = jnp.concatenate([-sin_r, sin_r, -sin_c, sin_c], axis=-1)
    return cos, sin


def _ffn_fwd(h2, w1, w2, tag):
    r, a = _matmul(h2, w1, kind='nn', b_split='n', out_dtype=BF16, epilogue='relu2', name=f"ffn_in_{tag}")
    f = _matmul(a, w2, kind='nn', b_split='k', out_dtype=F32, name=f"ffn_out_{tag}")
    return r, a, f


def _ffn_bwd(d_f, h2, r, a, w1, w2, tag):
    d_u = _matmul(d_f, w2, kind='nt', b_split='k', out_dtype=BF16, epilogue='times2x', extra=r, name=f"ffn_out_dx_{tag}")
    d_w2 = _matmul(a, d_f, kind='tn', out_split='k', out_dtype=BF16, name=f"ffn_out_dw_{tag}")
    d_w1 = _matmul(h2, d_u, kind='tn', out_split='n', out_dtype=BF16, name=f"ffn_in_dw_{tag}")
    d_h2 = _matmul(d_u, w1, kind='nt', b_split='n', out_dtype=F32, name=f"ffn_in_dx_{tag}")
    return d_h2, d_w1, d_w2


class _LocalLink:
    def __init__(self, big):
        self.big, self.grads, self.start_token = big, {}, 0.0

    def prefetch(self, group, after):
        return 0.0

    def poll(self, after):
        return 0.0

    def weights(self, group, after):
        return self.big

    def gradients(self, group, grads):
        self.grads.update(grads)
        return 0.0


def _local_step(xl0, xc0, target, ml, mc0, sp, link):
    n_lat, n_ctx = xl0.shape[0], xc0.shape[0]
    one = lambda v: 1.0 + v
    g = [[sp['norm_g'][i, k][None, :] for k in range(4)] for i in range(2)]

    sh1, sc1, gt1, sh2, sc2, gt2 = ml[0]
    big = link.weights('ar', None)
    sh1 = sh1 + link.start_token
    n_all = n_lat + n_ctx
    h_all = _norm_fwd(xl0, g[0][0], one(sc1), b=sh1, out_dtype=BF16, name="l0_mod1", into=(0, n_all, None))
    h_all = _norm_fwd(xc0, g[0][0], one(mc0[1]), b=mc0[0], out_dtype=BF16, name="l0_mod1_ctx", into=(n_lat, n_all, h_all))
    proj_l = _matmul(h_all, big['ar_in'], kind='nn', b_split='n', out_dtype=F32, a_rows=(0, n_lat), name="ar_in_lat")
    proj_c = _matmul(h_all, big['ar_in'], kind='nn', b_split='n', out_dtype=F32, a_rows=(n_lat, n_ctx), name="ar_in_ctx")
    cos_l, sin_l = _rope_tables(n_lat)
    cos_c, sin_c = jnp.ones((n_ctx, HEAD_DIM), F32), jnp.zeros((n_ctx, HEAD_DIM), F32)
    q_g, k_g = sp['q_g'], sp['k_g']
    _, k_all, v_all = _qk_fwd(proj_c, q_g, k_g, cos_c, sin_c, name="qk_fwd_ctx", kv_into=(0, n_all, None))
    q_l, k_all, v_all = _qk_fwd(proj_l, q_g, k_g, cos_l, sin_l, name="qk_fwd_lat", kv_into=(n_ctx, n_all, (k_all, v_all)))
    cat, lse = _attn_fwd(q_l, k_all, v_all)
    conv_b = sp['conv_b'] + link.prefetch('mix', cat)
    xs = _conv_fwd(proj_l, proj_c, sp['conv_w'], conv_b)
    rnn_w = [(sp['wa'][d], sp['ba'][d][None, :], sp['wx'][d], sp['bx'][d][None, :], sp['lam'][d][None, :]) for d in range(2)]
    h_f, hp_f = _rglru_fwd(xs, *rnn_w[0], reverse=False, n_ctx=n_ctx, name="rglru_fwd_f")
    h_r, hp_r = _rglru_fwd(xs, *rnn_w[1], reverse=True, n_ctx=n_ctx, name="rglru_fwd_r")
    cat = _rnn_out_fwd(h_f, h_r, proj_l, n_ctx, n_ctx, cat)
    w_mix = link.weights('mix', cat)
    ol0 = _matmul(cat, w_mix['ar_out'], kind='nn', b_split='k', out_dtype=F32, name="ar_out")
    xm0, h2_0 = _res_mod_fwd(ol0, xl0, g[0][1], gt1, g[0][2], one(sc2), sh2, name="l0_res1_mod2")
    r0, a0, f0 = _ffn_fwd(h2_0, w_mix['ff_in0'], w_mix['ff_out0'], "l0")
    th1, tc1, tg1, th2, tc2, tg2 = ml[1]
    xl1, hl1 = _res_mod_fwd(f0, xm0, g[0][3], gt2 + link.prefetch('l1', f0), g[1][0], one(tc1), th1, name="l0_res2_l1_mod1")
    w_l1 = link.weights('l1', xl1)
    z = _matmul(hl1, w_l1['gm_in'], kind='nn', b_split='n', bias=sp['gm_b_in'], out_dtype=F32, name="gm_in")
    b_sp_t = sp['gm_b_sp'].T
    gated = _gmlp_fwd(z, sp['gm_v_g'], sp['gm_v_b'], sp['gm_w_sp'], b_sp_t)
    ol1 = _matmul(gated, w_l1['gm_out'], kind='nn', b_split='k', out_dtype=F32, name="gm_out")
    xm1, h2_1 = _res_mod_fwd(ol1, xl1, g[1][1], tg1, g[1][2], one(tc2), th2, name="l1_res1_mod2")
    r1, a1, f1 = _ffn_fwd(h2_1, w_l1['ff_in1'], w_l1['ff_out1'], "l1")

    dy, d_f1, dg13, d_tg2, loss = _final_res_loss(f1, xm1, g[1][3], tg2, target)

    d_h2, dw_ff_in1, dw_ff_out1 = _ffn_bwd(d_f1, h2_1, r1, a1, w_l1['ff_in1'], w_l1['ff_out1'], "l1")
    tok = link.gradients('ffn1', {'ff_in1': dw_ff_in1, 'ff_out1': dw_ff_out1})
    dxm1, d_ol1, dg12, d_tc2, d_th2, dg11, d_tg1 = _mod_res_bwd(d_h2, xm1, g[1][2], one(tc2) + tok, dy, ol1, g[1][1], tg1,
                                                                name="l1_mod2_res1_bwd")
    d_gated = _matmul(d_ol1, w_l1['gm_out'], kind='nt', b_split='k', out_dtype=F32, name="gm_out_dx")
    dw_gm_out = _matmul(gated, d_ol1, kind='tn', out_split='k', out_dtype=BF16, name="gm_out_dw")
    d_z, d_gm_b_in, d_vg, d_vb, d_wsp, d_bsp_t = _gmlp_bwd(z, d_gated, sp['gm_v_g'], sp['gm_v_b'], sp['gm_w_sp'], b_sp_t)
    dw_gm_in = _matmul(hl1, d_z, kind='tn', out_split='n', out_dtype=BF16, name="gm_in_dw")
    d_hl1 = _matmul(d_z, w_l1['gm_in'], kind='nt', b_split='n', out_dtype=F32, name="gm_in_dx")
    tok = link.gradients('gm', {'gm_in': dw_gm_in, 'gm_out': dw_gm_out})

    dxl1, d_f0, dg10, d_tc1, d_th1, dg03, d_gt2 = _mod_res_bwd(d_hl1, xl1, g[1][0], one(tc1) + tok, dxm1, f0, g[0][3], gt2,
                                                               name="l1_mod1_l0_res2_bwd")
    d_h2, dw_ff_in0, dw_ff_out0 = _ffn_bwd(d_f0, h2_0, r0, a0, w_mix['ff_in0'], w_mix['ff_out0'], "l0")
    tok = link.gradients('ffn0', {'ff_in0': dw_ff_in0, 'ff_out0': dw_ff_out0})
    dxm0, d_ol0, dg02, d_sc2, d_sh2, dg01, d_gt1 = _mod_res_bwd(d_h2, xm0, g[0][2], one(sc2) + tok, dxl1, ol0, g[0][1], gt1,
                                                                name="l0_mod2_res1_bwd")
    d_cat = _matmul(d_ol0, w_mix['ar_out'], kind='nt', b_split='k', out_dtype=F32, name="ar_out_dx")
    dw_ar_out = _matmul(cat, d_ol0, kind='tn', out_split='k', out_dtype=BF16, name="ar_out_dw")
    dq, dk_all, dv_all = _attn_bwd(q_l, k_all, v_all, cat, lse, d_cat)
    tok = link.poll(dq)
    d_h, d_gate = _rnn_out_bwd(d_cat, h_f, h_r, proj_l, n_ctx, n_ctx)
    rnn_wb = [(wa_, ba_ + tok, wx_, bx_, lam_) for wa_, ba_, wx_, bx_, lam_ in rnn_w]
    dxs_f, d_wa0, d_ba0, d_wx0, d_bx0, d_lam0 = _rglru_bwd(
        xs, hp_f, d_h, *rnn_wb[0], reverse=False, n_ctx=n_ctx, name="rglru_bwd_f")
    dxs_r, d_wa1, d_ba1, d_wx1, d_bx1, d_lam1 = _rglru_bwd(
        xs, hp_r, d_h, *rnn_wb[1], reverse=True, n_ctx=n_ctx, name="rglru_bwd_r")
    d_xr_l, d_xr_c, d_cw, d_cb = _conv_bwd(dxs_f, dxs_r, proj_l, proj_c, sp['conv_w'])
    dp_qk_l, d_qg, d_kg_l = _qk_bwd(dq, dk_all, proj_l, q_g, k_g, cos_l, sin_l, name="qk_bwd_lat", dk_row0=n_ctx)
    dp_qk_c, _, d_kg_c = _qk_bwd(None, dk_all, proj_c, q_g, k_g, cos_c, sin_c, name="qk_bwd_ctx")
    d_proj = _assemble_d_proj(dp_qk_l, dp_qk_c, dv_all, d_xr_l, d_xr_c, d_gate)
    dw_ar_in = _matmul(h_all, d_proj, kind='tn', out_split='n', out_dtype=BF16, name="ar_in_dw")
    d_hl = _matmul(d_proj, big['ar_in'], kind='nt', b_split='n', out_dtype=F32, a_rows=(0, n_lat), name="ar_in_dx_lat")
    d_hc = _matmul(d_proj, big['ar_in'], kind='nt', b_split='n', out_dtype=F32, a_rows=(n_lat, n_ctx), name="ar_in_dx_ctx")
    grad_x, dg00, d_sc1, d_sh1 = _norm_bwd(d_hl, xl0, g[0][0], one(sc1), extra=dxm0, out_dtype=F32, name="l0_mod1_bwd")
    _, dg00c, d_mc_scale, d_mc_shift = _norm_bwd(d_hc, xc0, g[0][0], one(mc0[1]), out_dtype=BF16, name="l0_mod1_ctx_bwd")

    zeros_d = jnp.zeros_like(d_sh1)
    small = {
        'd_ml0': jnp.concatenate([d_sh1, d_sc1, d_gt1, d_sh2, d_sc2, d_gt2], axis=1),
        'd_ml1': jnp.concatenate([d_th1, d_tc1, d_tg1, d_th2, d_tc2, d_tg2], axis=1),
        'd_mc0': jnp.concatenate([d_mc_shift, d_mc_scale] + [zeros_d] * 4, axis=1),
        'norm_g': jnp.stack([jnp.concatenate([dg00 + dg00c, dg01, dg02, dg03], axis=0),
                             jnp.concatenate([dg10, dg11, dg12, dg13], axis=0)]),
        'q_g': d_qg, 'k_g': d_kg_l + d_kg_c, 'conv_w': d_cw, 'conv_b': d_cb,
        'wa': jnp.stack([d_wa0, d_wa1]), 'ba': jnp.concatenate([d_ba0, d_ba1], axis=0),
        'wx': jnp.stack([d_wx0, d_wx1]), 'bx': jnp.concatenate([d_bx0, d_bx1], axis=0),
        'lam': jnp.concatenate([d_lam0, d_lam1], axis=0),
        'gm_b_in': d_gm_b_in, 'gm_v_g': d_vg, 'gm_v_b': d_vb, 'gm_w_sp': d_wsp, 'gm_b_sp': d_bsp_t.T,
        'loss': loss,
    }
    return grad_x, small, {'ar_in': dw_ar_in, 'ar_out': dw_ar_out}


MOD_ROWS = 16
SMALL_F32 = ('d_ml0', 'd_ml1', 'd_mc0', 'norm_g', 'q_g', 'k_g', 'conv_w', 'conv_b', 'ba', 'bx', 'lam', 'gm_b_in', 'gm_v_g',
             'gm_v_b', 'gm_b_sp', 'loss')
SMALL_BF16 = ('wa', 'wx', 'gm_w_sp')


def _silu(v):
    return v * _sigmoid(v)


def _chip_concat(gathered, axis):
    return jnp.concatenate([gathered[2 * q] for q in range(N_CHIPS)], axis=axis)


def kernel(x, c, ctx, c_ctx, w_mod, b_mod, norm_g, w_ff_in, w_ff_out, ar_w_in, ar_q_g, ar_k_g, ar_conv_w, ar_conv_b, ar_wa, ar_ba, ar_wx, ar_bx, ar_lambda, ar_w_out, gm_w_in, gm_b_in, gm_v_g, gm_v_b, gm_w_sp, gm_b_sp, gm_w_out, loss_target, m_c_ctx, m_w_mod, m_b_mod, m_norm_g, m_w_ff_in, m_w_ff_out, m_ar_w_in, m_ar_q_g, m_ar_k_g, m_ar_conv_w, m_ar_conv_b, m_ar_wa, m_ar_ba, m_ar_wx, m_ar_bx, m_ar_lambda, m_ar_w_out, m_gm_w_in, m_gm_b_in, m_gm_v_g, m_gm_v_b, m_gm_w_sp, m_gm_b_sp, m_gm_w_out, v_c_ctx, v_w_mod, v_b_mod, v_norm_g, v_w_ff_in, v_w_ff_out, v_ar_w_in, v_ar_q_g, v_ar_k_g, v_ar_conv_w, v_ar_conv_b, v_ar_wa, v_ar_ba, v_ar_wx, v_ar_bx, v_ar_lambda, v_ar_w_out, v_gm_w_in, v_gm_b_in, v_gm_v_g, v_gm_v_b, v_gm_w_sp, v_gm_b_sp, v_gm_w_out):
    weights = dict(c_ctx=c_ctx, w_mod=w_mod, b_mod=b_mod, norm_g=norm_g, w_ff_in=w_ff_in, w_ff_out=w_ff_out, ar_w_in=ar_w_in,
                   ar_q_g=ar_q_g, ar_k_g=ar_k_g, ar_conv_w=ar_conv_w, ar_conv_b=ar_conv_b, ar_wa=ar_wa, ar_ba=ar_ba, ar_wx=ar_wx,
                   ar_bx=ar_bx, ar_lambda=ar_lambda, ar_w_out=ar_w_out, gm_w_in=gm_w_in, gm_b_in=gm_b_in, gm_v_g=gm_v_g,
                   gm_v_b=gm_v_b, gm_w_sp=gm_w_sp, gm_b_sp=gm_b_sp, gm_w_out=gm_w_out)
    m_in = dict(c_ctx=m_c_ctx, w_mod=m_w_mod, b_mod=m_b_mod, norm_g=m_norm_g, w_ff_in=m_w_ff_in, w_ff_out=m_w_ff_out,
                ar_w_in=m_ar_w_in, ar_q_g=m_ar_q_g, ar_k_g=m_ar_k_g, ar_conv_w=m_ar_conv_w, ar_conv_b=m_ar_conv_b, ar_wa=m_ar_wa,
                ar_ba=m_ar_ba, ar_wx=m_ar_wx, ar_bx=m_ar_bx, ar_lambda=m_ar_lambda, ar_w_out=m_ar_w_out, gm_w_in=m_gm_w_in,
                gm_b_in=m_gm_b_in, gm_v_g=m_gm_v_g, gm_v_b=m_gm_v_b, gm_w_sp=m_gm_w_sp, gm_b_sp=m_gm_b_sp, gm_w_out=m_gm_w_out)
    v_in = dict(c_ctx=v_c_ctx, w_mod=v_w_mod, b_mod=v_b_mod, norm_g=v_norm_g, w_ff_in=v_w_ff_in, w_ff_out=v_w_ff_out,
                ar_w_in=v_ar_w_in, ar_q_g=v_ar_q_g, ar_k_g=v_ar_k_g, ar_conv_w=v_ar_conv_w, ar_conv_b=v_ar_conv_b, ar_wa=v_ar_wa,
                ar_ba=v_ar_ba, ar_wx=v_ar_wx, ar_bx=v_ar_bx, ar_lambda=v_ar_lambda, ar_w_out=v_ar_w_out, gm_w_in=v_gm_w_in,
                gm_b_in=v_gm_b_in, gm_v_g=v_gm_v_g, gm_v_b=v_gm_v_b, gm_w_sp=v_gm_w_sp, gm_b_sp=v_gm_b_sp, gm_w_out=v_gm_w_out)

    xi, yi, ci = lax.axis_index("x"), lax.axis_index("y"), lax.axis_index("c")
    chip = 2 * xi + yi
    dev = 4 * xi + 2 * yi + ci
    place = jnp.stack([chip, ci]).astype(jnp.int32)
    n_lat, d = x.shape[1], x.shape[2]
    d6 = 6 * d
    cols_mod = w_mod.shape[2]

    mine = [c, norm_g, ar_conv_w[0], ar_ba[0], ar_bx[0], ar_lambda[0], gm_b_in, gm_v_g, gm_v_b]
    gathered = _allgather8([_pack(mine)], "gather_small_params")[0]
    first = _split_start("gather_first_start", [_cast_shard(ar_w_in, place, 0, "cast_ar_in")], _gather_copies, N_CHIPS - 1, gathered)
    parts = _unpack_devices(gathered, [a.shape for a in mine])
    c_all = parts[0].reshape(N_DEV, d)
    sp = {'norm_g': _chip_concat(parts[1], 2), 'q_g': ar_q_g, 'k_g': ar_k_g, 'conv_w': _chip_concat(parts[2], 1),
          'conv_b': ar_conv_b, 'wa': ar_wa[0], 'ba': _chip_concat(parts[3], 1), 'wx': ar_wx[0], 'bx': _chip_concat(parts[4], 1),
          'lam': _chip_concat(parts[5], 1), 'gm_b_in': _chip_concat(parts[6], 1), 'gm_v_g': _chip_concat(parts[7], 1),
          'gm_v_b': _chip_concat(parts[8], 1), 'gm_w_sp': gm_w_sp[0], 'gm_b_sp': gm_b_sp[0]}

    def mod_operand(c_rows, cc):
        row = lax.broadcasted_iota(jnp.int32, (MOD_ROWS - N_DEV, d), 0)
        lower = jnp.where(row == 0, jnp.broadcast_to(_silu(cc), (MOD_ROWS - N_DEV, d)), 0.0)
        sig = _sigmoid(cc)
        return jnp.concatenate([_silu(c_rows), lower], axis=0), sig * (1.0 + cc * (1.0 - sig))

    s_mod, dsilu_ctx = _small(mod_operand, [((MOD_ROWS, d), F32), ((1, d), F32)], c_all, c_ctx[None, :], name="mod_operand")
    b_mod_mine = lax.dynamic_slice(b_mod, (0, chip * cols_mod), (2, cols_mod))
    mod = [_matmul(s_mod, w_mod, kind='nn', b_layer=i, bias=b_mod_mine[i][None, :], out_dtype=F32, name=f"mod_fwd_{i}")
           for i in range(2)]
    mod_all = _allgather8([jnp.concatenate(mod, axis=0)], "gather_mod")[0]
    mod_all = _chip_concat(mod_all, 1).reshape(2, MOD_ROWS, d6)
    ml = [jnp.split(lax.dynamic_slice(mod_all[i], (dev, 0), (1, d6)), 6, axis=1) for i in range(2)]
    mc0 = jnp.split(mod_all[0, N_DEV:N_DEV + 1], 6, axis=1)[:2]

    names = ('w_ff_in', 'w_ff_out', 'ar_w_in', 'ar_w_out', 'gm_w_in', 'gm_w_out')
    keys = {'w_ff_in': ('ff_in0', 'ff_in1'), 'w_ff_out': ('ff_out0', 'ff_out1'), 'ar_w_in': ('ar_in',), 'ar_w_out': ('ar_out',),
            'gm_w_in': ('gm_in',), 'gm_w_out': ('gm_out',)}
    shards = {key: _cast_shard(weights[n], place, layer, f"cast_{key}", after=first[3]) for n in names
              for layer, key in enumerate(keys[n]) if key != 'ar_in'}
    send, recv, bufs, _ = first
    bufs = _split_wait("gather_first_wait", bufs, send, recv, _gather_copies, mod_all)
    send, recv, bufs, token = _split_start("forward_first_start", bufs, _forward_copies, N_CHIPS - 1)
    shards['ar_in'] = _split_wait("forward_first_wait", bufs, send, recv, _forward_copies, token)[0]
    link = _MeshLink(place, shards)

    grad_x, small, last_grads = _local_step(x[0], ctx[0], loss_target[0], ml, mc0, sp, link)

    def step(n, grad):
        return _adamw(weights[n], grad.reshape(weights[n].shape), m_in[n], v_in[n], f"adamw_{n}", rewrite_grad=n in names)

    small_f32, small_bf16 = [small[k] for k in SMALL_F32], [small[k] for k in SMALL_BF16]
    dev_arr = dev.astype(jnp.int32)[None]
    slots = [_into_slot(_pack(small_f32), dev_arr, "small_grads_slot_f32"),
             _into_slot(_pack(small_bf16, BF16), dev_arr, "small_grads_slot_bf16")]
    s_send, s_recv, slots, s_token = _split_start("small_grads_start", slots, _gather8_copies, 2 * N_CHIPS, grad_x)
    link.gradients('ar', last_grads, s_token)
    link.poll(link.last_token)
    reduced = link.reduce(('ffn1', 'gm', 'ffn0'), link.last_token)
    stepped = {n: step(n, reduced[names.index(n)]) for n in ('w_ff_in', 'w_ff_out', 'gm_w_in', 'gm_w_out')}
    reduced = link.reduce(('ar',), stepped['gm_w_out'][1])
    stepped.update({n: step(n, reduced[names.index(n)]) for n in ('ar_w_in', 'ar_w_out')})
    slots = _split_wait("small_grads_wait", slots, s_send, s_recv, _gather8_copies, stepped['ar_w_out'][1])
    small8, small8_bf16 = _forward_slots(slots, "small_grads_forward")
    total = dict(zip(SMALL_F32, _unpack(_sum_devices(small8).reshape(-1), [a.shape for a in small_f32])))
    total.update(zip(SMALL_BF16, _unpack(_sum_devices(small8_bf16).reshape(-1), [a.shape for a in small_bf16])))
    per_dev = _unpack_devices(small8, [(d6,), (d6,)])
    pad_rows = jnp.zeros((MOD_ROWS - N_DEV - 1, d6), F32)
    d_mod = [jnp.concatenate([per_dev[0], total['d_mc0'], pad_rows], axis=0),
             jnp.concatenate([per_dev[1], jnp.zeros((MOD_ROWS - N_DEV, d6), F32)], axis=0)]
    d_mod_mine = [lax.dynamic_slice(dm, (0, chip * cols_mod), (MOD_ROWS, cols_mod)) for dm in d_mod]
    g_w_mod = None
    for i in range(2):
        g_w_mod = _matmul(s_mod, d_mod_mine[i], kind='tn', out_dtype=F32, out_stack=(i, 2, g_w_mod), name=f"mod_dw_{i}")
    d_s_part = _matmul(d_mod_mine[0], w_mod, kind='nt', b_layer=0, out_dtype=F32, name="mod_ds")
    d_s_all = _allgather8([d_s_part[N_DEV:]], "gather_mod_ds")[0]

    def c_ctx_grad(parts_, dsilu):
        acc = parts_[0, 0:1]
        for q in range(1, N_CHIPS):
            acc = acc + parts_[2 * q, 0:1]
        return (acc * dsilu,)

    g_c_ctx = _small(c_ctx_grad, [((1, d), F32)], d_s_all, dsilu_ctx, name="c_ctx_grad")[0].reshape(d)

    def mine_of(full_grad, axis, n_shard):
        return lax.dynamic_slice_in_dim(full_grad, chip * n_shard, n_shard, axis=axis)

    grads_out = {
        'c_ctx': g_c_ctx, 'w_mod': g_w_mod,
        'b_mod': jnp.stack([total['d_ml0'][0] + total['d_mc0'][0], total['d_ml1'][0]]),
        'norm_g': mine_of(total['norm_g'], 2, norm_g.shape[2]),
        'ar_q_g': total['q_g'], 'ar_k_g': total['k_g'], 'ar_conv_w': mine_of(total['conv_w'], 1, ar_conv_w.shape[2])[None],
        'ar_conv_b': total['conv_b'], 'ar_wa': total['wa'][None], 'ar_ba': mine_of(total['ba'], 1, ar_ba.shape[2])[None],
        'ar_wx': total['wx'][None], 'ar_bx': mine_of(total['bx'], 1, ar_bx.shape[2])[None],
        'ar_lambda': mine_of(total['lam'], 1, ar_lambda.shape[2])[None],
        'gm_b_in': mine_of(total['gm_b_in'], 1, gm_b_in.shape[1]),
        'gm_v_g': mine_of(total['gm_v_g'], 1, gm_v_g.shape[1]), 'gm_v_b': mine_of(total['gm_v_b'], 1, gm_v_b.shape[1]),
        'gm_w_sp': total['gm_w_sp'][None], 'gm_b_sp': total['gm_b_sp'][None],
    }
    stepped.update({n: step(n, grad) for n, grad in grads_out.items()})
    stepped = [stepped[n] for n in weights]
    loss = total['loss'].reshape(())
    return (loss, grad_x[None], *[s[0] for s in stepped], *[s[1] for s in stepped], *[s[2] for s in stepped],
            *[s[3] for s in stepped])
```

```python
import math

import jax
import jax.numpy as jnp
from jax import lax
from jax.experimental import pallas as pl
from jax.experimental.pallas import tpu as pltpu

F32 = jnp.float32
BF16 = jnp.bfloat16
MESH = pl.DeviceIdType.MESH
ANY = pl.BlockSpec(memory_space=pl.ANY)

VMEM_LIMIT_BYTES = 52 * 1024 * 1024
LANES = 128
N_CHIPS = 4
N_DEV = 8

HEAD_DIM = 128
N_HEADS = 8
N_KV = 2
GROUP = N_HEADS // N_KV
ATTN_W = N_HEADS * HEAD_DIM
KV_W = N_KV * HEAD_DIM
D_RNN = 1024
RNN_BLOCKS = 8
RNN_BW = D_RNN // RNN_BLOCKS
CONV_W = 4
RG_C = 8.0
GRID_W = 64
ROPE_THETA = 10000.0
ROPE_PAIRS = HEAD_DIM // 4
GM_GROUPS = 16
CHUNK = 128
EPS = 1e-6
ADAM_LR, ADAM_B1, ADAM_B2, ADAM_EPS, ADAM_WD, ADAM_STEP = 0.001, 0.9, 0.999, 1e-08, 0.01, 10
GELU_C = math.sqrt(2.0 / math.pi)
LOG2E = math.log2(math.e)


def _params(sem=None):
    return pltpu.CompilerParams(dimension_semantics=sem, vmem_limit_bytes=VMEM_LIMIT_BYTES)


def _tile(dim, pref, unit):
    best = None
    t = unit
    while t <= min(dim, pref):
        if dim % t == 0:
            best = t
        t += unit
    return best if best is not None else dim


def _full(shape):
    nd = len(shape)
    return pl.BlockSpec(shape, lambda *_: (0,) * nd)


def _blocked_map(split, per_q):
    assert split == 'n'
    return lambda r, c: (c // per_q, r, c % per_q)


def _logical_shape(arr, split):
    if split == 'n':
        return arr.shape[1], arr.shape[0] * arr.shape[2]
    if split == 'k':
        return arr.shape[0] * arr.shape[1], arr.shape[2]
    return arr.shape[-2:]


def _matmul(a, b, *, kind, name, out_dtype, b_split=None, out_split=None, bias=None, epilogue=None, extra=None,
            a_rows=None, b_layer=None, out_stack=None, pref=(1024, 1024, 2048)):
    if b_split == 'k':
        b, b_split = b.reshape(-1, b.shape[-1]), None
    blocked_rows_out = out_split == 'k'
    if blocked_rows_out:
        assert epilogue != 'relu2'
        out_split = None
    b_rows, b_cols = _logical_shape(b, b_split)
    row0, nt_groups = 0, 1
    if kind == 'nn':
        m, kc = a.shape
        n = b_cols
        assert b_rows == kc
    elif kind == 'nt':
        m, kc = a.shape
        n = b_rows
        assert b_cols == kc
    if a_rows is not None:
        assert kind != 'tn'
        row0, m = a_rows
    if kind == 'tn':
        kc, m = a.shape
        n = b_cols
        assert b_rows == kc
    b_row_ext = b.shape[1] if b_split == 'k' else b_rows
    b_col_ext = b.shape[2] if b_split == 'n' else b_cols
    out_row_ext = m // N_CHIPS if out_split == 'k' else m
    out_col_ext = n // N_CHIPS if out_split == 'n' else n
    if kind == 'nn':
        ti = _tile(math.gcd(min(m, out_row_ext), row0), pref[0], 16)
        tj = _tile(math.gcd(b_col_ext, out_col_ext), pref[1], LANES)
        tl = _tile(b_row_ext, pref[2], LANES)
        a_spec = pl.BlockSpec((ti, tl), lambda i, j, l: (i + row0 // ti, l))
        b_tile, b_rc = (tl, tj), (lambda i, j, l: (l, j))
        dims = (((1,), (0,)), ((), ()))
    elif kind == 'nt':
        ti = _tile(math.gcd(min(m, out_row_ext), row0), pref[0], 16)
        tj = _tile(math.gcd(b_row_ext, out_col_ext), pref[1], LANES)
        if b_split == 'n' and b.shape[2] < pref[2]:
            nt_groups = max(k for k in (1, 2, 4) if k * b.shape[2] <= pref[2])
        tl = nt_groups * b.shape[2] if nt_groups > 1 else _tile(b_col_ext, pref[2], LANES)
        a_spec = pl.BlockSpec((ti, tl), lambda i, j, l: (i + row0 // ti, l))
        b_tile, b_rc = (tj, tl), (lambda i, j, l: (j, l))
        dims = (((1,), (1,)), ((), ()))
    else:
        ti = _tile(out_row_ext, pref[0], LANES)
        tj = _tile(math.gcd(b_col_ext, out_col_ext), pref[1], LANES)
        tl = _tile(b_row_ext, pref[2], 16)
        a_spec = pl.BlockSpec((tl, ti), lambda i, j, l: (l, i))
        b_tile, b_rc = (tl, tj), (lambda i, j, l: (l, j))
        dims = (((0,), (0,)), ((), ()))
    grid = (m // ti, n // tj, kc // tl)
    n_l = grid[2]

    if nt_groups > 1:
        b_spec = pl.BlockSpec((nt_groups, tj, b.shape[2]), lambda i, j, l: (l, j, 0))
    elif b_layer is not None:
        b_spec = pl.BlockSpec((None,) + b_tile, lambda i, j, l: (b_layer, *b_rc(i, j, l)))
    elif b_split is None:
        b_spec = pl.BlockSpec(b_tile, b_rc)
    else:
        per_q = (b.shape[2] // b_tile[1]) if b_split == 'n' else (b.shape[1] // b_tile[0])
        bmap = _blocked_map(b_split, per_q)
        b_spec = pl.BlockSpec((None,) + b_tile, lambda i, j, l: bmap(*b_rc(i, j, l)))
    if out_stack is not None:
        layer, n_layers, _ = out_stack
        out_shape2 = (n_layers, m, n)
        o_spec = pl.BlockSpec((None, ti, tj), lambda i, j, l: (layer, i, j))
    elif out_split is None:
        out_shape2 = (m, n)
        o_spec = pl.BlockSpec((ti, tj), lambda i, j, l: (i, j))
    else:
        out_shape2 = (N_CHIPS, m // N_CHIPS, n) if out_split == 'k' else (N_CHIPS, m, n // N_CHIPS)
        per_q = (out_shape2[2] // tj) if out_split == 'n' else (out_shape2[1] // ti)
        omap = _blocked_map(out_split, per_q)
        o_spec = pl.BlockSpec((None, ti, tj), lambda i, j, l: omap(i, j))

    in_specs = [a_spec, b_spec]
    operands = [a, b]
    if bias is not None:
        in_specs.append(pl.BlockSpec((1, tj), lambda i, j, l: (0, j)))
        operands.append(bias)
    if extra is not None:
        in_specs.append(pl.BlockSpec((ti, tj), lambda i, j, l: (i, j)))
        operands.append(extra)
    if epilogue == 'relu2':
        out_shape = (jax.ShapeDtypeStruct(out_shape2, out_dtype), jax.ShapeDtypeStruct(out_shape2, out_dtype))
        out_specs = (o_spec, o_spec)
    else:
        out_shape = jax.ShapeDtypeStruct(out_shape2, out_dtype)
        out_specs = o_spec
    has_bias, has_extra = bias is not None, extra is not None
    has_dest = out_stack is not None and out_stack[2] is not None
    if has_dest:
        in_specs.append(ANY)
        operands.append(out_stack[2])

    def body(*refs):
        a_ref, b_ref = refs[0], refs[1]
        pos = 2
        bias_ref = extra_ref = None
        if has_bias:
            bias_ref = refs[pos]
            pos += 1
        if has_extra:
            extra_ref = refs[pos]
            pos += 1
        if has_dest:
            pos += 1
        outs = refs[pos:] if n_l == 1 else refs[pos:-1]

        def finish(acc):
            if has_bias:
                acc = acc + bias_ref[...]
            if epilogue == 'relu2':
                r = jnp.maximum(acc, 0.0)
                outs[0][...] = r.astype(outs[0].dtype)
                outs[1][...] = (r * r).astype(outs[1].dtype)
            elif epilogue == 'times2x':
                outs[0][...] = (acc * (2.0 * extra_ref[...].astype(F32))).astype(outs[0].dtype)
            else:
                outs[0][...] = acc.astype(outs[0].dtype)

        def product():
            if nt_groups == 1:
                return lax.dot_general(a_ref[...].astype(BF16), b_ref[...].astype(BF16), dims, preferred_element_type=F32)
            width = b_ref.shape[2]
            return sum(lax.dot_general(a_ref[:, s * width:(s + 1) * width].astype(BF16), b_ref[s].astype(BF16), dims,
                                       preferred_element_type=F32) for s in range(nt_groups))

        if n_l == 1:
            finish(product())
            return
        acc_ref = refs[-1]
        step = pl.program_id(2)

        @pl.when(step == 0)
        def _():
            acc_ref[...] = jnp.zeros_like(acc_ref)

        acc_ref[...] += product()

        @pl.when(step == n_l - 1)
        def _():
            finish(acc_ref[...])

    result = pl.pallas_call(
        body, name=name, grid=grid, in_specs=in_specs, out_specs=out_specs, out_shape=out_shape,
        input_output_aliases={len(operands) - 1: 0} if has_dest else {},
        scratch_shapes=[] if n_l == 1 else [pltpu.VMEM((ti, tj), F32)],
        compiler_params=_params(("parallel", "parallel", "arbitrary")),
    )(*operands)
    return result.reshape(N_CHIPS, m // N_CHIPS, n) if blocked_rows_out else result


def _small(fn, out_shapes, *arrays, name):
    n_in = len(arrays)

    def body(*refs):
        res = fn(*[r[...] for r in refs[:n_in]])
        for o_ref, v in zip(refs[n_in:], res):
            o_ref[...] = v.astype(o_ref.dtype)

    return pl.pallas_call(
        body, name=name, out_shape=tuple(jax.ShapeDtypeStruct(s, d) for s, d in out_shapes),
        in_specs=[_full(a.shape) for a in arrays], out_specs=tuple(_full(s) for s, _ in out_shapes), grid=(1,),
        compiler_params=_params(("arbitrary",)),
    )(*arrays)


def _rows_tile(rows, cols, itemsize=4, budget=2 * 1024 * 1024):
    return _tile(rows, max(16, budget // (cols * itemsize)), 16)


def _rowwise(fn, out_dtypes, *arrays, name):
    rows, cols = arrays[0].shape
    tr = _rows_tile(rows, cols)
    n_in = len(arrays)

    def body(*refs):
        res = fn(*[r[...] for r in refs[:n_in]])
        for o_ref, v in zip(refs[n_in:], res):
            o_ref[...] = v.astype(o_ref.dtype)

    spec = pl.BlockSpec((tr, cols), lambda i: (i, 0))
    return pl.pallas_call(
        body, name=name, grid=(rows // tr,), in_specs=[spec] * n_in, out_specs=tuple(spec for _ in out_dtypes),
        out_shape=tuple(jax.ShapeDtypeStruct((rows, cols), d) for d in out_dtypes),
        compiler_params=_params(("parallel",)),
    )(*arrays)


def _as2d(a):
    return a.reshape(1, a.size) if a.ndim < 2 else a.reshape(-1, a.shape[-1])


def _cast_shard(w, place, layer, name, after=None):
    _, rows, cols = w.shape
    tr = _rows_tile(rows, cols)

    def body(place_ref, w_ref, *rest):
        rest[-1][...] = w_ref[...].astype(rest[-1].dtype)

    return pl.pallas_call(
        body, name=name, out_shape=jax.ShapeDtypeStruct((N_CHIPS, rows, cols), BF16),
        grid_spec=pltpu.PrefetchScalarGridSpec(
            num_scalar_prefetch=1, grid=(rows // tr,),
            in_specs=[pl.BlockSpec((None, tr, cols), lambda i, pr: (layer, i, 0))] + ([] if after is None else [ANY]),
            out_specs=pl.BlockSpec((None, tr, cols), lambda i, pr: (pr[0], i, 0))),
        compiler_params=_params(("parallel",)),
    )(place, w, *([] if after is None else [after]))


def _norm_fwd(x, g, a, b=None, res=None, *, out_dtype, name, into=None):
    rows, d = x.shape
    row0, total, dest = into if into is not None else (0, rows, None)
    tr = _rows_tile(math.gcd(rows, row0), d, budget=4 * 1024 * 1024)
    has_b, has_res = b is not None, res is not None

    def body(*refs):
        x_ref, g_ref, a_ref = refs[:3]
        pos = 3
        xv = x_ref[...]
        rstd = lax.rsqrt(jnp.mean(xv * xv, axis=-1, keepdims=True) + EPS)
        y = (xv * rstd * g_ref[...]) * a_ref[...]
        if has_b:
            y = y + refs[pos][...]
            pos += 1
        if has_res:
            y = y + refs[pos][...]
            pos += 1
        refs[-1][...] = y.astype(refs[-1].dtype)

    row = pl.BlockSpec((tr, d), lambda i: (i, 0))
    vec = pl.BlockSpec((1, d), lambda i: (0, 0))
    operands, specs = [x, g, a], [row, vec, vec]
    if has_b:
        operands.append(b)
        specs.append(vec)
    if has_res:
        operands.append(res)
        specs.append(row)
    if dest is not None:
        operands.append(dest)
        specs.append(ANY)
    return pl.pallas_call(
        body, name=name, grid=(rows // tr,), in_specs=specs, out_specs=pl.BlockSpec((tr, d), lambda i: (i + row0 // tr, 0)),
        out_shape=jax.ShapeDtypeStruct((total, d), out_dtype), compiler_params=_params(("parallel",)),
        input_output_aliases={} if dest is None else {len(operands) - 1: 0},
    )(*operands)


def _rstd(v):
    return lax.rsqrt(jnp.mean(v * v, axis=-1, keepdims=True) + EPS)


def _res_mod_fwd(o, x, g_res, gate, g_mod, a_mod, b_mod, *, name):
    rows, d = x.shape
    tr = _rows_tile(rows, d)

    def body(o_ref, x_ref, gr_ref, gate_ref, gm_ref, a_ref, b_ref, xm_ref, h_ref):
        ov = o_ref[...]
        xm = x_ref[...] + (ov * _rstd(ov) * gr_ref[...]) * gate_ref[...]
        xm_ref[...] = xm
        h_ref[...] = ((xm * _rstd(xm) * gm_ref[...]) * a_ref[...] + b_ref[...]).astype(h_ref.dtype)

    row = pl.BlockSpec((tr, d), lambda i: (i, 0))
    vec = pl.BlockSpec((1, d), lambda i: (0, 0))
    return pl.pallas_call(
        body, name=name, grid=(rows // tr,), in_specs=[row, row, vec, vec, vec, vec, vec], out_specs=(row, row),
        out_shape=(jax.ShapeDtypeStruct((rows, d), F32), jax.ShapeDtypeStruct((rows, d), BF16)),
        compiler_params=_params(("parallel",)),
    )(o, x, g_res, gate, g_mod, a_mod, b_mod)


def _mod_res_bwd(d_h, xm, g_mod, a_mod, extra, o, g_res, gate, *, name):
    rows, d = xm.shape
    tr = _rows_tile(rows, d)

    def body(dh_ref, xm_ref, gm_ref, a_ref, ex_ref, o_ref, gr_ref, gate_ref,
             dxm_ref, do_ref, dgm_ref, da_ref, db_ref, dgr_ref, dgate_ref):
        @pl.when(pl.program_id(0) == 0)
        def _():
            for ref in (dgm_ref, da_ref, db_ref, dgr_ref, dgate_ref):
                ref[...] = jnp.zeros_like(ref)

        def norm_adjoint(dy, xv, gain, scale, dgain_ref, dscale_ref):
            rstd = _rstd(xv)
            nrm = xv * rstd
            dscale_ref[...] += jnp.sum(dy * (nrm * gain), axis=0, keepdims=True)
            dt = dy * scale
            dgain_ref[...] += jnp.sum(dt * nrm, axis=0, keepdims=True)
            dn = dt * gain
            return rstd * (dn - nrm * jnp.mean(dn * nrm, axis=-1, keepdims=True))

        dhv = dh_ref[...].astype(F32)
        db_ref[...] += jnp.sum(dhv, axis=0, keepdims=True)
        dxm = norm_adjoint(dhv, xm_ref[...], gm_ref[...], a_ref[...], dgm_ref, da_ref) + ex_ref[...]
        dxm_ref[...] = dxm
        do_ref[...] = norm_adjoint(dxm, o_ref[...], gr_ref[...], gate_ref[...], dgr_ref, dgate_ref).astype(do_ref.dtype)

    row = pl.BlockSpec((tr, d), lambda i: (i, 0))
    vec = pl.BlockSpec((1, d), lambda i: (0, 0))
    vshape = jax.ShapeDtypeStruct((1, d), F32)
    return pl.pallas_call(
        body, name=name, grid=(rows // tr,), in_specs=[row, row, vec, vec, row, row, vec, vec],
        out_specs=(row, row, vec, vec, vec, vec, vec),
        out_shape=(jax.ShapeDtypeStruct((rows, d), F32), jax.ShapeDtypeStruct((rows, d), BF16)) + (vshape,) * 5,
        compiler_params=_params(("arbitrary",)),
    )(d_h, xm, g_mod, a_mod, extra, o, g_res, gate)


def _norm_bwd(dy, x, g, a, extra=None, *, out_dtype, name):
    rows, d = x.shape
    tr = _rows_tile(rows, d)
    has_extra = extra is not None

    def body(*refs):
        dy_ref, x_ref, g_ref, a_ref = refs[:4]
        pos = 4
        extra_ref = None
        if has_extra:
            extra_ref = refs[pos]
            pos += 1
        dx_ref, dg_ref, da_ref, db_ref = refs[pos:pos + 4]

        @pl.when(pl.program_id(0) == 0)
        def _():
            dg_ref[...] = jnp.zeros_like(dg_ref)
            da_ref[...] = jnp.zeros_like(da_ref)
            db_ref[...] = jnp.zeros_like(db_ref)

        xv = x_ref[...]
        dyv = dy_ref[...].astype(F32)
        rstd = lax.rsqrt(jnp.mean(xv * xv, axis=-1, keepdims=True) + EPS)
        nrm = xv * rstd
        gv = g_ref[...]
        da_ref[...] += jnp.sum(dyv * (nrm * gv), axis=0, keepdims=True)
        db_ref[...] += jnp.sum(dyv, axis=0, keepdims=True)
        dt = dyv * a_ref[...]
        dg_ref[...] += jnp.sum(dt * nrm, axis=0, keepdims=True)
        dn = dt * gv
        dx = rstd * (dn - nrm * jnp.mean(dn * nrm, axis=-1, keepdims=True))
        if has_extra:
            dx = dx + extra_ref[...]
        dx_ref[...] = dx.astype(dx_ref.dtype)

    row = pl.BlockSpec((tr, d), lambda i: (i, 0))
    vec = pl.BlockSpec((1, d), lambda i: (0, 0))
    operands, specs = [dy, x, g, a], [row, row, vec, vec]
    if has_extra:
        operands.append(extra)
        specs.append(row)
    vshape = jax.ShapeDtypeStruct((1, d), F32)
    return pl.pallas_call(
        body, name=name, grid=(rows // tr,), in_specs=specs, out_specs=(row, vec, vec, vec),
        out_shape=(jax.ShapeDtypeStruct((rows, d), out_dtype), vshape, vshape, vshape),
        compiler_params=_params(("arbitrary",)),
    )(*operands)


def _final_res_loss(f, x, g, gate, target):
    rows, d = x.shape
    tr = _rows_tile(rows, d)

    def body(f_ref, x_ref, g_ref, gate_ref, t_ref, dy_ref, df_ref, dg_ref, dgate_ref, loss_ref):
        @pl.when(pl.program_id(0) == 0)
        def _():
            for ref in (dg_ref, dgate_ref, loss_ref):
                ref[...] = jnp.zeros_like(ref)

        fv, gv, gatev = f_ref[...], g_ref[...], gate_ref[...]
        rstd = _rstd(fv)
        nrm = fv * rstd
        err = x_ref[...] + (nrm * gv) * gatev - t_ref[...]
        loss_ref[...] += jnp.sum(jnp.sum(err * err, axis=-1, keepdims=True), axis=0, keepdims=True) * (0.5 / d)
        dy = err * (1.0 / d)
        dy_ref[...] = dy
        dgate_ref[...] += jnp.sum(dy * (nrm * gv), axis=0, keepdims=True)
        dt = dy * gatev
        dg_ref[...] += jnp.sum(dt * nrm, axis=0, keepdims=True)
        dn = dt * gv
        df_ref[...] = (rstd * (dn - nrm * jnp.mean(dn * nrm, axis=-1, keepdims=True))).astype(df_ref.dtype)

    row = pl.BlockSpec((tr, d), lambda i: (i, 0))
    vec = pl.BlockSpec((1, d), lambda i: (0, 0))
    vshape = jax.ShapeDtypeStruct((1, d), F32)
    return pl.pallas_call(
        body, name="final_res_loss", grid=(rows // tr,), in_specs=[row, row, vec, vec, row],
        out_specs=(row, row, vec, vec, _full((1, 1))),
        out_shape=(jax.ShapeDtypeStruct((rows, d), F32), jax.ShapeDtypeStruct((rows, d), BF16), vshape, vshape,
                   jax.ShapeDtypeStruct((1, 1), F32)),
        compiler_params=_params(("arbitrary",)),
    )(f, x, g, gate, target)


def _rope_partner(v):
    lane = lax.broadcasted_iota(jnp.int32, v.shape, 1)
    up = pltpu.roll(v, HEAD_DIM - ROPE_PAIRS, 1)
    down = pltpu.roll(v, ROPE_PAIRS, 1)
    return jnp.where((lane % (2 * ROPE_PAIRS)) < ROPE_PAIRS, up, down)


def _qk_fwd(proj, q_g, k_g, cos, sin, *, name, kv_into=None):
    rows = proj.shape[0]
    row0, total, kv_dest = kv_into if kv_into is not None else (0, rows, None)
    tr = _tile(math.gcd(rows, row0), 256, 16)
    width = ATTN_W + 2 * KV_W

    def body(p_ref, qg_ref, kg_ref, cos_ref, sin_ref, *rest):
        q_ref, k_ref, v_ref = rest[-3:]
        cosv, sinv = cos_ref[...], sin_ref[...]
        for h in range(N_HEADS + N_KV):
            xv = p_ref[:, h * HEAD_DIM:(h + 1) * HEAD_DIM]
            gain = qg_ref[...] if h < N_HEADS else kg_ref[...]
            t = xv * lax.rsqrt(jnp.mean(xv * xv, axis=-1, keepdims=True) + EPS) * gain
            y = t * cosv + _rope_partner(t) * sinv
            if h < N_HEADS:
                q_ref[:, h * HEAD_DIM:(h + 1) * HEAD_DIM] = y.astype(BF16)
            else:
                k_ref[:, (h - N_HEADS) * HEAD_DIM:(h - N_HEADS + 1) * HEAD_DIM] = y.astype(BF16)
        v_ref[...] = p_ref[:, ATTN_W + KV_W:width].astype(BF16)

    vec = _full((1, HEAD_DIM))
    tab = pl.BlockSpec((tr, HEAD_DIM), lambda i: (i, 0))
    kv_spec = pl.BlockSpec((tr, KV_W), lambda i: (i + row0 // tr, 0))
    kv_shape = jax.ShapeDtypeStruct((total, KV_W), BF16)
    return pl.pallas_call(
        body, name=name, grid=(rows // tr,),
        in_specs=[pl.BlockSpec((tr, width), lambda i: (i, 0)), vec, vec, tab, tab] + ([] if kv_dest is None else [ANY, ANY]),
        out_specs=(pl.BlockSpec((tr, ATTN_W), lambda i: (i, 0)), kv_spec, kv_spec),
        out_shape=(jax.ShapeDtypeStruct((rows, ATTN_W), BF16), kv_shape, kv_shape),
        input_output_aliases={} if kv_dest is None else {5: 1, 6: 2},
        compiler_params=_params(("parallel",)),
    )(proj, q_g, k_g, cos, sin, *([] if kv_dest is None else kv_dest))


def _qk_bwd(dq, dk, proj, q_g, k_g, cos, sin, *, name, dk_row0=0):
    rows = proj.shape[0]
    tr = _tile(math.gcd(rows, dk_row0), 256, 16)
    width = ATTN_W + KV_W
    has_q = dq is not None

    def body(*refs):
        pos = 0
        dq_ref = None
        if has_q:
            dq_ref = refs[0]
            pos = 1
        dk_ref, p_ref, qg_ref, kg_ref, cos_ref, sin_ref, dp_ref, dqg_ref, dkg_ref = refs[pos:pos + 9]

        @pl.when(pl.program_id(0) == 0)
        def _():
            dqg_ref[...] = jnp.zeros_like(dqg_ref)
            dkg_ref[...] = jnp.zeros_like(dkg_ref)

        cosv, sinv = cos_ref[...], sin_ref[...]
        for h in range(N_HEADS + N_KV):
            cols = slice(h * HEAD_DIM, (h + 1) * HEAD_DIM)
            if h < N_HEADS and not has_q:
                dp_ref[:, cols] = jnp.zeros((tr, HEAD_DIM), dp_ref.dtype)
                continue
            if h < N_HEADS:
                dyv, gain, dgain_ref = dq_ref[:, cols], qg_ref[...], dqg_ref
            else:
                hk = h - N_HEADS
                dyv, gain, dgain_ref = dk_ref[:, hk * HEAD_DIM:(hk + 1) * HEAD_DIM], kg_ref[...], dkg_ref
            dyv = dyv.astype(F32)
            dt = dyv * cosv + _rope_partner(dyv * sinv)
            xv = p_ref[:, cols]
            rstd = lax.rsqrt(jnp.mean(xv * xv, axis=-1, keepdims=True) + EPS)
            nrm = xv * rstd
            dgain_ref[...] += jnp.sum(dt * nrm, axis=0, keepdims=True)
            dn = dt * gain
            dp_ref[:, cols] = (rstd * (dn - nrm * jnp.mean(dn * nrm, axis=-1, keepdims=True))).astype(dp_ref.dtype)

    vec = _full((1, HEAD_DIM))
    tab = pl.BlockSpec((tr, HEAD_DIM), lambda i: (i, 0))
    operands = ([dq] if has_q else []) + [dk, proj, q_g, k_g, cos, sin]
    specs = ([pl.BlockSpec((tr, ATTN_W), lambda i: (i, 0))] if has_q else []) + [
        pl.BlockSpec((tr, KV_W), lambda i: (i + dk_row0 // tr, 0)), pl.BlockSpec((tr, width), lambda i: (i, 0)), vec, vec, tab, tab]
    return pl.pallas_call(
        body, name=name, grid=(rows // tr,), in_specs=specs,
        out_specs=(pl.BlockSpec((tr, width), lambda i: (i, 0)), vec, vec),
        out_shape=(jax.ShapeDtypeStruct((rows, width), BF16), jax.ShapeDtypeStruct((1, HEAD_DIM), F32),
                   jax.ShapeDtypeStruct((1, HEAD_DIM), F32)),
        compiler_params=_params(("arbitrary",)),
    )(*operands)


def _attn_fwd(q, k, v):
    n_q, n_k = q.shape[0], k.shape[0]
    tq = _tile(n_q, 512, 16)
    gw = GROUP * HEAD_DIM
    scale = HEAD_DIM ** -0.5

    def body(q_ref, k_ref, v_ref, o_ref, lse_ref):
        kv, vv = k_ref[...], v_ref[...]
        for g in range(GROUP):
            cols = slice(g * HEAD_DIM, (g + 1) * HEAD_DIM)
            s = lax.dot_general(q_ref[:, cols], kv, (((1,), (1,)), ((), ())), preferred_element_type=F32) * (scale * LOG2E)
            m = jnp.max(s, axis=-1, keepdims=True)
            p = jnp.exp2(s - m)
            l = jnp.sum(p, axis=-1, keepdims=True)
            o = jnp.dot(p.astype(BF16), vv, preferred_element_type=F32) / l
            o_ref[:, cols] = o.astype(o_ref.dtype)
            lse_ref[:, g:g + 1] = m + jnp.log(l) * LOG2E

    return pl.pallas_call(
        body, name="attn_fwd", grid=(N_KV, n_q // tq),
        in_specs=[pl.BlockSpec((tq, gw), lambda h, i: (i, h)), pl.BlockSpec((n_k, HEAD_DIM), lambda h, i: (0, h)),
                  pl.BlockSpec((n_k, HEAD_DIM), lambda h, i: (0, h))],
        out_specs=(pl.BlockSpec((tq, gw), lambda h, i: (i, h)), pl.BlockSpec((None, tq, GROUP), lambda h, i: (h, i, 0))),
        out_shape=(jax.ShapeDtypeStruct((n_q, ATTN_W + D_RNN), BF16), jax.ShapeDtypeStruct((N_KV, n_q, GROUP), F32)),
        compiler_params=_params(("parallel", "parallel")),
    )(q, k, v)


def _attn_bwd(q, k, v, o, lse, do):
    n_q, n_k = q.shape[0], k.shape[0]
    tq = _tile(n_q, 256, 16)
    gw = GROUP * HEAD_DIM
    scale = HEAD_DIM ** -0.5

    def body(q_ref, k_ref, v_ref, o_ref, lse_ref, do_ref, dq_ref, dk_ref, dv_ref):
        @pl.when(pl.program_id(1) == 0)
        def _():
            dk_ref[...] = jnp.zeros_like(dk_ref)
            dv_ref[...] = jnp.zeros_like(dv_ref)

        kv, vv = k_ref[...], v_ref[...]
        for g in range(GROUP):
            cols = slice(g * HEAD_DIM, (g + 1) * HEAD_DIM)
            qg = q_ref[:, cols]
            dof = do_ref[:, cols].astype(F32)
            dog = dof.astype(BF16)
            s = lax.dot_general(qg, kv, (((1,), (1,)), ((), ())), preferred_element_type=F32) * (scale * LOG2E)
            p = jnp.exp2(s - lse_ref[:, g:g + 1])
            delta = jnp.sum(dof * o_ref[:, cols].astype(F32), axis=-1, keepdims=True)
            dp = lax.dot_general(dog, vv, (((1,), (1,)), ((), ())), preferred_element_type=F32)
            ds = (p * (dp - delta) * scale).astype(BF16)
            pb = p.astype(BF16)
            dq_ref[:, cols] = jnp.dot(ds, kv, preferred_element_type=F32)
            dk_ref[...] += lax.dot_general(ds, qg, (((0,), (0,)), ((), ())), preferred_element_type=F32)
            dv_ref[...] += lax.dot_general(pb, dog, (((0,), (0,)), ((), ())), preferred_element_type=F32)

    qspec = pl.BlockSpec((tq, gw), lambda h, i: (i, h))
    kspec = pl.BlockSpec((n_k, HEAD_DIM), lambda h, i: (0, h))
    return pl.pallas_call(
        body, name="attn_bwd", grid=(N_KV, n_q // tq),
        in_specs=[qspec, kspec, kspec, qspec, pl.BlockSpec((None, tq, GROUP), lambda h, i: (h, i, 0)), qspec],
        out_specs=(qspec, kspec, kspec),
        out_shape=(jax.ShapeDtypeStruct((n_q, ATTN_W), F32), jax.ShapeDtypeStruct((n_k, KV_W), F32),
                   jax.ShapeDtypeStruct((n_k, KV_W), F32)),
        compiler_params=_params(("parallel", "arbitrary")),
    )(q, k, v, o, lse, do)


CONV_COLS = 256
XR_COL0 = ATTN_W + 2 * KV_W


def _shift_rows(v, off):
    if off == 0:
        return v
    n = v.shape[0]
    rolled = pltpu.roll(v, (-off) % n, 0)
    t = lax.broadcasted_iota(jnp.int32, v.shape, 0)
    keep = (t + off >= 0) & (t + off < n)
    return jnp.where(keep, rolled, 0.0)


def _conv_fwd(proj_l, proj_c, w, b):
    n_lat, n_ctx = proj_l.shape[0], proj_c.shape[0]
    blk0 = XR_COL0 // CONV_COLS

    def body(xl_ref, xc_ref, w_ref, b_ref, y_ref):
        for x_ref, rows in ((xc_ref, slice(0, n_ctx)), (xl_ref, slice(n_ctx, n_ctx + n_lat))):
            xv = x_ref[...]
            y = b_ref[...] + jnp.zeros_like(xv)
            for j in range(CONV_W):
                y = y + _shift_rows(xv, j - CONV_W // 2) * w_ref[j:j + 1, :]
            y_ref[rows, :] = y

    return pl.pallas_call(
        body, name="conv_fwd", grid=(D_RNN // CONV_COLS,),
        in_specs=[pl.BlockSpec((n_lat, CONV_COLS), lambda i: (0, blk0 + i)), pl.BlockSpec((n_ctx, CONV_COLS), lambda i: (0, blk0 + i)),
                  pl.BlockSpec((CONV_W, CONV_COLS), lambda i: (0, i)), pl.BlockSpec((1, CONV_COLS), lambda i: (0, i))],
        out_specs=pl.BlockSpec((n_ctx + n_lat, CONV_COLS), lambda i: (0, i)),
        out_shape=jax.ShapeDtypeStruct((n_ctx + n_lat, D_RNN), F32), compiler_params=_params(("parallel",)),
    )(proj_l, proj_c, w, b)


def _conv_bwd(d1, d2, proj_l, proj_c, w):
    n_lat, n_ctx = proj_l.shape[0], proj_c.shape[0]
    blk0 = XR_COL0 // CONV_COLS

    def body(d1_ref, d2_ref, xl_ref, xc_ref, w_ref, dxl_ref, dxc_ref, dw_ref, db_ref):
        dw = [0.0] * CONV_W
        db = 0.0
        for x_ref, dx_ref, rows in ((xc_ref, dxc_ref, slice(0, n_ctx)), (xl_ref, dxl_ref, slice(n_ctx, n_ctx + n_lat))):
            dv = d1_ref[rows, :] + d2_ref[rows, :]
            xv = x_ref[...]
            dx = jnp.zeros_like(dv)
            for j in range(CONV_W):
                off = j - CONV_W // 2
                dx = dx + _shift_rows(dv, -off) * w_ref[j:j + 1, :]
                dw[j] = dw[j] + jnp.sum(dv * _shift_rows(xv, off), axis=0, keepdims=True)
            dx_ref[...] = dx.astype(dx_ref.dtype)
            db = db + jnp.sum(dv, axis=0, keepdims=True)
        for j in range(CONV_W):
            dw_ref[j:j + 1, :] = dw[j]
        db_ref[...] = db

    both = pl.BlockSpec((n_ctx + n_lat, CONV_COLS), lambda i: (0, i))
    return pl.pallas_call(
        body, name="conv_bwd", grid=(D_RNN // CONV_COLS,),
        in_specs=[both, both, pl.BlockSpec((n_lat, CONV_COLS), lambda i: (0, blk0 + i)),
                  pl.BlockSpec((n_ctx, CONV_COLS), lambda i: (0, blk0 + i)), pl.BlockSpec((CONV_W, CONV_COLS), lambda i: (0, i))],
        out_specs=(pl.BlockSpec((n_lat, CONV_COLS), lambda i: (0, i)), pl.BlockSpec((n_ctx, CONV_COLS), lambda i: (0, i)),
                   pl.BlockSpec((CONV_W, CONV_COLS), lambda i: (0, i)), pl.BlockSpec((1, CONV_COLS), lambda i: (0, i))),
        out_shape=(jax.ShapeDtypeStruct((n_lat, D_RNN), BF16), jax.ShapeDtypeStruct((n_ctx, D_RNN), BF16),
                   jax.ShapeDtypeStruct((CONV_W, D_RNN), F32), jax.ShapeDtypeStruct((1, D_RNN), F32)),
        compiler_params=_params(("parallel",)),
    )(d1, d2, proj_l, proj_c, w)


RNN_TB = 256
SCAN_ROWS = 8


def _sigmoid(z):
    return 1.0 / (1.0 + jnp.exp(-z))


def _softplus(z):
    return jnp.maximum(z, 0.0) + jnp.log(1.0 + jnp.exp(-jnp.abs(z)))


def _one_minus_exp(y, exp_y):
    series = -y * (1.0 + y * (0.5 + y * (1.0 / 6.0 + y * (1.0 / 24.0))))
    return jnp.where(y > -0.03, series, 1.0 - exp_y)


def _rglru_gates(xv, wa_ref, ba_ref, wx_ref, bx_ref, lam_ref):
    xb = xv.astype(BF16)
    zr = jnp.concatenate([jnp.dot(xb[:, n * RNN_BW:(n + 1) * RNN_BW], wa_ref[n].astype(BF16),
                                  preferred_element_type=F32) for n in range(RNN_BLOCKS)], axis=-1) + ba_ref[...]
    zi = jnp.concatenate([jnp.dot(xb[:, n * RNN_BW:(n + 1) * RNN_BW], wx_ref[n].astype(BF16),
                                  preferred_element_type=F32) for n in range(RNN_BLOCKS)], axis=-1) + bx_ref[...]
    r = _sigmoid(zr)
    gi = _sigmoid(zi)
    sp = _softplus(-lam_ref[...])
    log_a = -RG_C * r * sp
    a = jnp.exp(log_a)
    s = jnp.sqrt(_one_minus_exp(2.0 * log_a, a * a))
    return r, gi, sp, a, s


def _scan_rows(n_rows, reverse, step_fn, carry):
    groups = n_rows // SCAN_ROWS

    def trip(gidx, carry):
        gi = (groups - 1 - gidx) if reverse else gidx
        base = pl.multiple_of(gi * SCAN_ROWS, SCAN_ROWS)
        return step_fn(base, carry)

    return lax.fori_loop(0, groups, trip, carry)


def _scan_block_order(nb, nb_c, reverse, adjoint):
    if not reverse:
        return (lambda i: nb - 1 - i) if adjoint else (lambda i: i)
    if adjoint:
        return lambda i: jnp.where(i < nb - nb_c, nb_c + i, i - (nb - nb_c))
    return lambda i: jnp.where(i < nb_c, nb_c - 1 - i, nb + nb_c - 1 - i)


def _rglru_fwd(xs, wa, ba, wx, bx, lam, *, reverse, n_ctx, name):
    rows = xs.shape[0]
    tb = _tile(math.gcd(rows, n_ctx), RNN_TB, SCAN_ROWS)
    nb = rows // tb
    block_of = _scan_block_order(nb, n_ctx // tb, reverse, False)
    order = lambda i: (block_of(i), 0)

    def body(x_ref, wa_ref, ba_ref, wx_ref, bx_ref, lam_ref, h_ref, hp_ref, a_s, b_s, state):
        @pl.when(pl.program_id(0) == 0)
        def _():
            state[...] = jnp.zeros_like(state)

        xv = x_ref[...]
        _, gi, _, a, s = _rglru_gates(xv, wa_ref, ba_ref, wx_ref, bx_ref, lam_ref)
        a_s[...] = a
        b_s[...] = s * (gi * xv)

        def group(base, h):
            av = a_s[pl.ds(base, SCAN_ROWS), :]
            bv = b_s[pl.ds(base, SCAN_ROWS), :]
            outs, prevs = [None] * SCAN_ROWS, [None] * SCAN_ROWS
            for k in range(SCAN_ROWS):
                r_ = SCAN_ROWS - 1 - k if reverse else k
                prevs[r_] = h
                h = av[r_:r_ + 1, :] * h + bv[r_:r_ + 1, :]
                outs[r_] = h
            h_ref[pl.ds(base, SCAN_ROWS), :] = jnp.concatenate(outs, axis=0)
            hp_ref[pl.ds(base, SCAN_ROWS), :] = jnp.concatenate(prevs, axis=0)
            return h

        state[0:1, :] = _scan_rows(tb, reverse, group, state[0:1, :])

    blk = pl.BlockSpec((tb, D_RNN), order)
    wspec = _full((RNN_BLOCKS, RNN_BW, RNN_BW))
    vec = _full((1, D_RNN))
    return pl.pallas_call(
        body, name=name, grid=(nb,), in_specs=[blk, wspec, vec, wspec, vec, vec], out_specs=(blk, blk),
        out_shape=(jax.ShapeDtypeStruct((rows, D_RNN), F32), jax.ShapeDtypeStruct((rows, D_RNN), F32)),
        scratch_shapes=[pltpu.VMEM((tb, D_RNN), F32), pltpu.VMEM((tb, D_RNN), F32), pltpu.VMEM((SCAN_ROWS, D_RNN), F32)],
        compiler_params=_params(("arbitrary",)),
    )(xs, wa, ba, wx, bx, lam)


def _rglru_bwd(xs, h_prev, dh, wa, ba, wx, bx, lam, *, reverse, n_ctx, name):
    rows = xs.shape[0]
    tb = _tile(math.gcd(rows, n_ctx), RNN_TB, SCAN_ROWS)
    nb, nb_c = rows // tb, n_ctx // tb
    back = not reverse
    block_of = _scan_block_order(nb, nb_c, reverse, True)
    order = lambda i: (block_of(i), 0)

    def body(x_ref, hp_ref, dh_ref, wa_ref, ba_ref, wx_ref, bx_ref, lam_ref,
             dx_ref, dwa_ref, dba_ref, dwx_ref, dbx_ref, dlam_ref, a_s, g_s, state):
        @pl.when(pl.program_id(0) == 0)
        def _():
            state[...] = jnp.zeros_like(state)
            dwa_ref[...] = jnp.zeros_like(dwa_ref)
            dwx_ref[...] = jnp.zeros_like(dwx_ref)
            dba_ref[...] = jnp.zeros_like(dba_ref)
            dbx_ref[...] = jnp.zeros_like(dbx_ref)
            dlam_ref[...] = jnp.zeros_like(dlam_ref)

        xv = x_ref[...]
        r, gi, sp, a, s = _rglru_gates(xv, wa_ref, ba_ref, wx_ref, bx_ref, lam_ref)
        a_s[...] = a

        is_latent = block_of(pl.program_id(0)) >= nb_c

        def group(base, carry):
            av = a_s[pl.ds(base, SCAN_ROWS), :]
            dv = jnp.where(is_latent, dh_ref[pl.ds(base, SCAN_ROWS), :], 0.0)
            outs = [None] * SCAN_ROWS
            for k in range(SCAN_ROWS):
                r_ = SCAN_ROWS - 1 - k if back else k
                gt = dv[r_:r_ + 1, :] + carry
                outs[r_] = gt
                carry = av[r_:r_ + 1, :] * gt
            g_s[pl.ds(base, SCAN_ROWS), :] = jnp.concatenate(outs, axis=0)
            return carry

        state[0:1, :] = _scan_rows(tb, back, group, state[0:1, :])

        gv = g_s[...]
        d_a = gv * hp_ref[...]
        d_s = gv * (gi * xv)
        d_gi = gv * (s * xv)
        dx = gv * (s * gi)
        d_log_a = d_a * a - d_s * (a * a) / s
        d_r = d_log_a * (-RG_C * sp)
        lamv = lam_ref[...]
        d_sp = jnp.sum(d_log_a * (-RG_C * r), axis=0, keepdims=True)
        dlam_ref[...] += d_sp * (-_sigmoid(-lamv))
        d_zr = d_r * r * (1.0 - r)
        d_zi = d_gi * gi * (1.0 - gi)
        dba_ref[...] += jnp.sum(d_zr, axis=0, keepdims=True)
        dbx_ref[...] += jnp.sum(d_zi, axis=0, keepdims=True)
        xb = xv.astype(BF16)
        zrb, zib = d_zr.astype(BF16), d_zi.astype(BF16)
        parts = []
        for n in range(RNN_BLOCKS):
            cols = slice(n * RNN_BW, (n + 1) * RNN_BW)
            dwa_ref[n] += lax.dot_general(xb[:, cols], zrb[:, cols], (((0,), (0,)), ((), ())), preferred_element_type=F32)
            dwx_ref[n] += lax.dot_general(xb[:, cols], zib[:, cols], (((0,), (0,)), ((), ())), preferred_element_type=F32)
            parts.append(
                lax.dot_general(zrb[:, cols], wa_ref[n].astype(BF16), (((1,), (1,)), ((), ())), preferred_element_type=F32)
                + lax.dot_general(zib[:, cols], wx_ref[n].astype(BF16), (((1,), (1,)), ((), ())), preferred_element_type=F32))
        dx_ref[...] = dx + jnp.concatenate(parts, axis=-1)

    blk = pl.BlockSpec((tb, D_RNN), order)
    wspec = _full((RNN_BLOCKS, RNN_BW, RNN_BW))
    vec = _full((1, D_RNN))
    wshape = jax.ShapeDtypeStruct((RNN_BLOCKS, RNN_BW, RNN_BW), F32)
    vshape = jax.ShapeDtypeStruct((1, D_RNN), F32)
    dh_blk = pl.BlockSpec((tb, D_RNN), lambda i: (jnp.maximum(block_of(i) - nb_c, 0), 0))
    return pl.pallas_call(
        body, name=name, grid=(nb,), in_specs=[blk, blk, dh_blk, wspec, vec, wspec, vec, vec],
        out_specs=(blk, wspec, vec, wspec, vec, vec),
        out_shape=(jax.ShapeDtypeStruct((rows, D_RNN), F32), wshape, vshape, wshape, vshape, vshape),
        scratch_shapes=[pltpu.VMEM((tb, D_RNN), F32), pltpu.VMEM((tb, D_RNN), F32), pltpu.VMEM((SCAN_ROWS, D_RNN), F32)],
        compiler_params=_params(("arbitrary",)),
    )(xs, h_prev, dh, wa, ba, wx, bx, lam)


def _assemble_d_proj(dp_qk_l, dp_qk_c, dv_all, d_xr_l, d_xr_c, d_gate):
    n_lat, n_ctx = dp_qk_l.shape[0], dp_qk_c.shape[0]
    tr = _tile(math.gcd(n_lat, n_ctx), 256, 16)
    nb_l, nb_c = n_lat // tr, n_ctx // tr
    w_qk = ATTN_W + KV_W

    def body(ql_ref, qc_ref, dv_ref, xl_ref, xc_ref, g_ref, o_ref):
        i = pl.program_id(0)
        o_ref[:, w_qk:XR_COL0] = dv_ref[...].astype(o_ref.dtype)

        @pl.when(i < nb_l)
        def _():
            o_ref[:, :w_qk] = ql_ref[...]
            o_ref[:, XR_COL0:GATE_COL0] = xl_ref[...]
            o_ref[:, GATE_COL0:] = g_ref[...]

        @pl.when(i >= nb_l)
        def _():
            o_ref[:, :w_qk] = qc_ref[...]
            o_ref[:, XR_COL0:GATE_COL0] = xc_ref[...]
            o_ref[:, GATE_COL0:] = jnp.zeros((tr, D_RNN), o_ref.dtype)

    lat = lambda i: (jnp.minimum(i, nb_l - 1), 0)
    ctx = lambda i: (jnp.maximum(i - nb_l, 0), 0)
    return pl.pallas_call(
        body, name="assemble_d_proj", grid=(nb_l + nb_c,),
        in_specs=[pl.BlockSpec((tr, w_qk), lat), pl.BlockSpec((tr, w_qk), ctx),
                  pl.BlockSpec((tr, KV_W), lambda i: (jnp.where(i < nb_l, i + nb_c, i - nb_l), 0)),
                  pl.BlockSpec((tr, D_RNN), lat), pl.BlockSpec((tr, D_RNN), ctx), pl.BlockSpec((tr, D_RNN), lat)],
        out_specs=pl.BlockSpec((tr, GATE_COL0 + D_RNN), lambda i: (i, 0)),
        out_shape=jax.ShapeDtypeStruct((n_lat + n_ctx, GATE_COL0 + D_RNN), BF16),
        compiler_params=_params(("parallel",)),
    )(dp_qk_l, dp_qk_c, dv_all, d_xr_l, d_xr_c, d_gate)


def _gelu(z):
    return 0.5 * z * (1.0 + jnp.tanh(GELU_C * (z + 0.044715 * z * z * z)))


def _gelu_grad(z):
    t = jnp.tanh(GELU_C * (z + 0.044715 * z * z * z))
    return 0.5 * (1.0 + t) + 0.5 * z * (1.0 - t * t) * (GELU_C * (1.0 + 3.0 * 0.044715 * z * z))


GATE_COL0 = XR_COL0 + D_RNN


RNN_OUT_COLS = 512


def _rnn_out_specs(rows, hf_off, hb_off):
    tr = _tile(rows, 256, 16)
    assert hf_off % tr == 0 and hb_off % tr == 0 and GATE_COL0 % RNN_OUT_COLS == 0
    fo, bo, go = hf_off // tr, hb_off // tr, GATE_COL0 // RNN_OUT_COLS
    hf_spec = pl.BlockSpec((tr, RNN_OUT_COLS), lambda i, j: (i + fo, j))
    hb_spec = pl.BlockSpec((tr, RNN_OUT_COLS), lambda i, j: (i + bo, j))
    gate_spec = pl.BlockSpec((tr, RNN_OUT_COLS), lambda i, j: (i, j + go))
    out_spec = pl.BlockSpec((tr, RNN_OUT_COLS), lambda i, j: (i, j))
    return (rows // tr, D_RNN // RNN_OUT_COLS), hf_spec, hb_spec, gate_spec, out_spec


def _rnn_out_fwd(hf, hb, proj, hf_off, hb_off, cat):
    rows = proj.shape[0]
    grid, hf_spec, hb_spec, gate_spec, out_spec = _rnn_out_specs(rows, hf_off, hb_off)
    tr, col0 = out_spec.block_shape[0], ATTN_W // RNN_OUT_COLS

    def body(hf_ref, hb_ref, g_ref, _, o_ref):
        o_ref[...] = ((hf_ref[...] + hb_ref[...]) * _gelu(g_ref[...])).astype(o_ref.dtype)

    return pl.pallas_call(
        body, name="rnn_out_fwd", grid=grid, in_specs=[hf_spec, hb_spec, gate_spec, ANY],
        out_specs=pl.BlockSpec((tr, RNN_OUT_COLS), lambda i, j: (i, j + col0)),
        out_shape=jax.ShapeDtypeStruct(cat.shape, cat.dtype), input_output_aliases={3: 0},
        compiler_params=_params(("parallel", "parallel")),
    )(hf, hb, proj, cat)


def _rnn_out_bwd(d_cat, hf, hb, proj, hf_off, hb_off):
    rows = proj.shape[0]
    grid, hf_spec, hb_spec, gate_spec, out_spec = _rnn_out_specs(rows, hf_off, hb_off)
    do = ATTN_W // RNN_OUT_COLS

    def body(d_ref, hf_ref, hb_ref, g_ref, dh_ref, dg_ref):
        dv, gv = d_ref[...].astype(F32), g_ref[...]
        dh_ref[...] = dv * _gelu(gv)
        dg_ref[...] = (dv * (hf_ref[...] + hb_ref[...]) * _gelu_grad(gv)).astype(dg_ref.dtype)

    tr = out_spec.block_shape[0]
    return pl.pallas_call(
        body, name="rnn_out_bwd", grid=grid,
        in_specs=[pl.BlockSpec((tr, RNN_OUT_COLS), lambda i, j: (i, j + do)), hf_spec, hb_spec, gate_spec],
        out_specs=(out_spec, out_spec),
        out_shape=(jax.ShapeDtypeStruct((rows, D_RNN), F32), jax.ShapeDtypeStruct((rows, D_RNN), BF16)),
        compiler_params=_params(("parallel", "parallel")),
    )(d_cat, hf, hb, proj)


def _gmlp_parts(z_ref, vg_ref, vb_ref, d_gm):
    zu, zv = z_ref[:, :d_gm], z_ref[:, d_gm:]
    u = _gelu(zu)
    v = _gelu(zv)
    mu = jnp.mean(v, axis=-1, keepdims=True)
    vc = v - mu
    rstd = lax.rsqrt(jnp.mean(vc * vc, axis=-1, keepdims=True) + EPS)
    vhat = vc * rstd
    vn = vhat * vg_ref[...] + vb_ref[...]
    return zu, zv, u, vhat, rstd, vn


def _gmlp_fwd(z, v_g, v_b, w_sp, b_sp_t):
    rows, d_gm = z.shape[0], z.shape[1] // 2
    tr = _tile(rows, 256, CHUNK)
    gwid = d_gm // GM_GROUPS

    def body(z_ref, vg_ref, vb_ref, w_ref, b_ref, o_ref):
        _, _, u, _, _, vn = _gmlp_parts(z_ref, vg_ref, vb_ref, d_gm)
        vnb = vn.astype(BF16)
        for g in range(GM_GROUPS):
            wg = w_ref[g].astype(BF16)
            for c in range(tr // CHUNK):
                rs, cs = slice(c * CHUNK, (c + 1) * CHUNK), slice(g * gwid, (g + 1) * gwid)
                sv = jnp.dot(wg, vnb[rs, cs], preferred_element_type=F32) + b_ref[:, g:g + 1]
                o_ref[rs, cs] = (u[rs, cs] * sv).astype(o_ref.dtype)

    return pl.pallas_call(
        body, name="gmlp_fwd", grid=(rows // tr,),
        in_specs=[pl.BlockSpec((tr, 2 * d_gm), lambda i: (i, 0)), _full((1, d_gm)), _full((1, d_gm)),
                  _full(w_sp.shape), _full(b_sp_t.shape)],
        out_specs=pl.BlockSpec((tr, d_gm), lambda i: (i, 0)),
        out_shape=jax.ShapeDtypeStruct((rows, d_gm), BF16), compiler_params=_params(("parallel",)),
    )(z, v_g, v_b, w_sp, b_sp_t)


def _gmlp_bwd(z, dgate, v_g, v_b, w_sp, b_sp_t):
    rows, d_gm = z.shape[0], z.shape[1] // 2
    tr = _tile(rows, 256, CHUNK)
    gwid = d_gm // GM_GROUPS

    def body(z_ref, dg_ref, vg_ref, vb_ref, w_ref, b_ref, dz_ref, dbin_ref, dvg_ref, dvb_ref, dw_ref, dbs_ref, dvn_s):
        @pl.when(pl.program_id(0) == 0)
        def _():
            dbin_ref[...] = jnp.zeros_like(dbin_ref)
            dvg_ref[...] = jnp.zeros_like(dvg_ref)
            dvb_ref[...] = jnp.zeros_like(dvb_ref)
            dw_ref[...] = jnp.zeros_like(dw_ref)
            dbs_ref[...] = jnp.zeros_like(dbs_ref)

        zu, zv, u, vhat, rstd, vn = _gmlp_parts(z_ref, vg_ref, vb_ref, d_gm)
        vnb = vn.astype(BF16)
        dgv = dg_ref[...].astype(F32)
        dsv = dgv * u
        dsvb = dsv.astype(BF16)
        for g in range(GM_GROUPS):
            wg = w_ref[g].astype(BF16)
            cs = slice(g * gwid, (g + 1) * gwid)
            for c in range(tr // CHUNK):
                rs = slice(c * CHUNK, (c + 1) * CHUNK)
                sv = jnp.dot(wg, vnb[rs, cs], preferred_element_type=F32) + b_ref[:, g:g + 1]
                dz_ref[rs, cs] = (dgv[rs, cs] * sv * _gelu_grad(zu[rs, cs])).astype(dz_ref.dtype)
                dw_ref[g] += lax.dot_general(dsvb[rs, cs], vnb[rs, cs], (((1,), (1,)), ((), ())),
                                             preferred_element_type=F32)
                dbs_ref[:, g:g + 1] += jnp.sum(dsv[rs, cs], axis=-1, keepdims=True)
                dvn_s[rs, cs] = lax.dot_general(wg, dsvb[rs, cs], (((0,), (0,)), ((), ())), preferred_element_type=F32)
        dvn = dvn_s[...]
        dvg_ref[...] += jnp.sum(dvn * vhat, axis=0, keepdims=True)
        dvb_ref[...] += jnp.sum(dvn, axis=0, keepdims=True)
        dvh = dvn * vg_ref[...]
        dv = rstd * (dvh - jnp.mean(dvh, axis=-1, keepdims=True) - vhat * jnp.mean(dvh * vhat, axis=-1, keepdims=True))
        dzv = dv * _gelu_grad(zv)
        dz_ref[:, d_gm:] = dzv.astype(dz_ref.dtype)
        dbin_ref[:, d_gm:] += jnp.sum(dzv, axis=0, keepdims=True)
        dbin_ref[:, :d_gm] += jnp.sum(dz_ref[:, :d_gm].astype(F32), axis=0, keepdims=True)

    return pl.pallas_call(
        body, name="gmlp_bwd", grid=(rows // tr,),
        in_specs=[pl.BlockSpec((tr, 2 * d_gm), lambda i: (i, 0)), pl.BlockSpec((tr, d_gm), lambda i: (i, 0)),
                  _full((1, d_gm)), _full((1, d_gm)), _full(w_sp.shape), _full(b_sp_t.shape)],
        out_specs=(pl.BlockSpec((tr, 2 * d_gm), lambda i: (i, 0)), _full((1, 2 * d_gm)), _full((1, d_gm)),
                   _full((1, d_gm)), _full(w_sp.shape), _full(b_sp_t.shape)),
        out_shape=(jax.ShapeDtypeStruct((rows, 2 * d_gm), BF16), jax.ShapeDtypeStruct((1, 2 * d_gm), F32),
                   jax.ShapeDtypeStruct((1, d_gm), F32), jax.ShapeDtypeStruct((1, d_gm), F32),
                   jax.ShapeDtypeStruct(w_sp.shape, F32), jax.ShapeDtypeStruct(b_sp_t.shape, F32)),
        scratch_shapes=[pltpu.VMEM((tr, d_gm), F32)],
        compiler_params=_params(("arbitrary",)),
    )(z, dgate, v_g, v_b, w_sp, b_sp_t)


def _adamw_math(w, g, m, v):
    m = ADAM_B1 * m + (1.0 - ADAM_B1) * g
    v = ADAM_B2 * v + (1.0 - ADAM_B2) * (g * g)
    m_hat = m / (1.0 - ADAM_B1 ** ADAM_STEP)
    v_hat = v / (1.0 - ADAM_B2 ** ADAM_STEP)
    delta = -ADAM_LR * (m_hat / (jnp.sqrt(v_hat) + ADAM_EPS) + ADAM_WD * w)
    return delta, m, v


def _adamw(w, g, m, v, name, rewrite_grad=False):
    shape = w.shape
    if rewrite_grad:
        outs = _rowwise(lambda w_, g_, m_, v_: (g_,) + _adamw_math(w_, g_, m_, v_), (F32,) * 4, _as2d(w), _as2d(g), _as2d(m),
                        _as2d(v), name=name)
        return tuple(o.reshape(shape) for o in outs)
    outs = _rowwise(_adamw_math, (F32, F32, F32), _as2d(w), _as2d(g), _as2d(m), _as2d(v), name=name)
    return (g.reshape(shape),) + tuple(o.reshape(shape) for o in outs)


PACK_COLS = 1024


def _pack(arrays, dtype=F32):
    flat = jnp.concatenate([a.reshape(-1).astype(dtype) for a in arrays])
    pad = (-flat.size) % (16 * PACK_COLS)
    return jnp.pad(flat, (0, pad)).reshape(-1, PACK_COLS)


def _into_slot(pack, dev, name):
    rows, cols = pack.shape
    tr = _rows_tile(rows, cols, budget=512 * 1024)

    def body(dev_ref, p_ref, o_ref):
        o_ref[...] = p_ref[...]

    return pl.pallas_call(
        body, name=name, out_shape=jax.ShapeDtypeStruct((N_DEV, rows, cols), pack.dtype),
        grid_spec=pltpu.PrefetchScalarGridSpec(
            num_scalar_prefetch=1, grid=(rows // tr,), in_specs=[pl.BlockSpec((tr, cols), lambda i, dv: (i, 0))],
            out_specs=pl.BlockSpec((None, tr, cols), lambda i, dv: (dv[0], i, 0))),
        compiler_params=_params(("parallel",)),
    )(dev, pack)


def _unpack(flat, shapes):
    out, pos = [], 0
    for shp in shapes:
        n = math.prod(shp)
        out.append(flat[pos:pos + n].reshape(shp))
        pos += n
    return out


def _unpack_devices(packed8, shapes):
    flat8 = packed8.reshape(N_DEV, -1)
    out, pos = [], 0
    for shp in shapes:
        n = math.prod(shp)
        out.append(flat8[:, pos:pos + n].reshape((N_DEV,) + tuple(shp)))
        pos += n
    return out


def _sum_devices(g8):
    _, rows, cols = g8.shape
    tr = _rows_tile(rows, cols, budget=256 * 1024)

    def body(g_ref, o_ref):
        acc = g_ref[0].astype(F32)
        for d in range(1, N_DEV):
            acc = acc + g_ref[d].astype(F32)
        o_ref[...] = acc

    return pl.pallas_call(
        body, name="sum_devices", grid=(rows // tr,), in_specs=[pl.BlockSpec((N_DEV, tr, cols), lambda i: (0, i, 0))],
        out_specs=pl.BlockSpec((tr, cols), lambda i: (i, 0)), out_shape=jax.ShapeDtypeStruct((rows, cols), F32),
        compiler_params=_params(("parallel",)),
    )(g8)


def _place():
    return lax.axis_index("x"), lax.axis_index("y"), lax.axis_index("c")


def _other_chips(x, y):
    return [(1 - x, y), (x, 1 - y), (1 - x, 1 - y)]


def _remote(src, dst, send_sem, recv_sem, to):
    return pltpu.make_async_remote_copy(src_ref=src, dst_ref=dst, send_sem=send_sem, recv_sem=recv_sem, device_id=to,
                                        device_id_type=MESH)


def _comm_call(body, name, operands, out_shapes, n_remote, n_local, aliases=None):
    return pl.pallas_call(
        body, name=name, out_shape=tuple(out_shapes), in_specs=[ANY] * len(operands), out_specs=tuple(ANY for _ in out_shapes),
        scratch_shapes=[pltpu.SemaphoreType.DMA((n_remote,)), pltpu.SemaphoreType.DMA((n_remote,)),
                        pltpu.SemaphoreType.DMA((max(n_local, 1),))],
        input_output_aliases=aliases or {},
    )(*operands)


def _in_place(arrays):
    return [jax.ShapeDtypeStruct(a.shape, a.dtype) for a in arrays], {i: i for i in range(len(arrays))}


def _allgather8(arrs, name):
    n = len(arrs)

    def body(*refs):
        ins, outs = refs[:n], refs[n:2 * n]
        send, recv, lsem = refs[2 * n:]
        x, y, c = _place()
        me, sib = (x, y, c), (x, y, 1 - c)
        chips = _other_chips(x, y)

        def slot(t, px, py, pc):
            return outs[t].at[4 * px + 2 * py + pc]

        def cp(t, k, block, to, from_input=False):
            src = ins[t] if from_input else slot(t, *block)
            return _remote(src, slot(t, *block), send.at[7 * t + k], recv.at[7 * t + k], to)

        mine = [pltpu.make_async_copy(ins[t], slot(t, *me), lsem.at[t]) for t in range(n)]
        for cpy in mine:
            cpy.start()
        first = []
        for t in range(n):
            first.append(cp(t, 0, me, sib, True))
            first += [cp(t, 1 + j, me, (*chip, c), True) for j, chip in enumerate(chips)]
        for cpy in first:
            cpy.start()
        passed = []
        for t in range(n):
            for j, chip in enumerate(chips):
                cp(t, 1 + j, (*chip, c), me).wait_recv()
                fwd = cp(t, 4 + j, (*chip, c), sib)
                fwd.start()
                passed.append(fwd)
        for t in range(n):
            cp(t, 0, sib, me).wait_recv()
            for j, chip in enumerate(chips):
                cp(t, 4 + j, (*chip, 1 - c), me).wait_recv()
        for cpy in first + passed:
            cpy.wait_send()
        for cpy in mine:
            cpy.wait()

    outs = _comm_call(body, name, arrs, [jax.ShapeDtypeStruct((N_DEV,) + a.shape, a.dtype) for a in arrs], 7 * n, n)
    return list(outs)


def _join_halves(bufs):
    n = len(bufs)
    units = [(k, layer) for k in range(n) for layer in range(bufs[k].shape[0])]

    def body(*refs):
        bufs_ = refs[n:2 * n]
        send, recv, _ = refs[2 * n:]
        x, y, c = _place()
        sent = []
        for u, (k, layer) in enumerate(units):
            half = bufs_[k].shape[1] // 2
            mine = bufs_[k].at[layer, pl.ds(c * half, half)]
            cpy = _remote(mine, mine, send.at[u], recv.at[u], (x, y, 1 - c))
            cpy.start()
            sent.append(cpy)
        for u, (k, layer) in enumerate(units):
            half = bufs_[k].shape[1] // 2
            theirs = bufs_[k].at[layer, pl.ds((1 - c) * half, half)]
            _remote(theirs, theirs, send.at[u], recv.at[u], (x, y, c)).wait_recv()
        for cpy in sent:
            cpy.wait_send()

    shapes, aliases = _in_place(bufs)
    return list(_comm_call(body, "join_halves", bufs, shapes, len(units), 0, aliases))


def _add_halves(grad, other, place):
    _, rows, cols = grad.shape
    half = rows // 2
    tr = _rows_tile(half, cols, itemsize=2, budget=2 * 1024 * 1024)
    per_half = half // tr

    def body(place_ref, g_ref, o_ref, s_ref):
        s_ref[...] = (g_ref[...].astype(F32) + o_ref[...].astype(F32)).astype(s_ref.dtype)

    return pl.pallas_call(
        body, name="add_halves", out_shape=jax.ShapeDtypeStruct((N_CHIPS, half, cols), grad.dtype),
        grid_spec=pltpu.PrefetchScalarGridSpec(
            num_scalar_prefetch=1, grid=(N_CHIPS, per_half),
            in_specs=[pl.BlockSpec((None, tr, cols), lambda k, i, pr: (k, pr[1] * per_half + i, 0)),
                      pl.BlockSpec((None, tr, cols), lambda k, i, pr: (k, i, 0))],
            out_specs=pl.BlockSpec((None, tr, cols), lambda k, i, pr: (k, i, 0))),
        compiler_params=_params(("parallel", "parallel")),
    )(place, grad, other)


def _add_chips(sums, others, place, dest, layer, n_layers):
    _, half, cols = sums.shape
    tr = _rows_tile(half, cols, itemsize=4, budget=2 * 1024 * 1024)
    per_half = half // tr

    def body(place_ref, s_ref, o_ref, *rest):
        acc = s_ref[...].astype(F32)
        for j in range(N_CHIPS - 1):
            acc = acc + o_ref[j].astype(F32)
        rest[-1][...] = acc

    operands = [place, sums, others] + ([] if dest is None else [dest])
    return pl.pallas_call(
        body, name="add_chips", out_shape=jax.ShapeDtypeStruct((n_layers, 2 * half, cols), F32),
        grid_spec=pltpu.PrefetchScalarGridSpec(
            num_scalar_prefetch=1, grid=(per_half,),
            in_specs=[pl.BlockSpec((None, tr, cols), lambda i, pr: (pr[0], i, 0)),
                      pl.BlockSpec((N_CHIPS - 1, tr, cols), lambda i, pr: (0, i, 0))] + ([] if dest is None else [ANY]),
            out_specs=pl.BlockSpec((None, tr, cols), lambda i, pr: (layer, pr[1] * per_half + i, 0))),
        input_output_aliases={} if dest is None else {3: 0},
        compiler_params=_params(("parallel",)),
    )(*operands)


HBM = pl.BlockSpec(memory_space=pltpu.HBM)
SEM = pl.BlockSpec(memory_space=pltpu.SEMAPHORE)
DATAFLOW = pltpu.SideEffectType.DATAFLOW_SIDE_EFFECTING


def _split_start(name, bufs, copies, n_copies, after=None):
    n = len(bufs)
    extra = 0 if after is None else 1

    def body(*refs):
        for cpy in copies(refs[:n], refs[n + extra], refs[n + extra + 1]):
            cpy.start()
        refs[-1][...] = jnp.zeros_like(refs[-1])

    outs = pl.pallas_call(
        body, name=name,
        out_shape=(pltpu.SemaphoreType.DMA((n_copies,)), pltpu.SemaphoreType.DMA((n_copies,)),
                   *[pltpu.HBM(b.shape, b.dtype) for b in bufs], jax.ShapeDtypeStruct((8, LANES), F32)),
        in_specs=[HBM] * n + [ANY] * extra,
        out_specs=(SEM, SEM, *[HBM] * n, pl.BlockSpec(memory_space=pltpu.VMEM)),
        input_output_aliases={i: 2 + i for i in range(n)},
        compiler_params=pltpu.CompilerParams(has_side_effects=DATAFLOW),
    )(*[pltpu.with_memory_space_constraint(b, pltpu.HBM) for b in bufs], *([] if after is None else [after]))
    return outs[0], outs[1], list(outs[2:2 + n]), outs[-1]


def _split_wait(name, bufs, send, recv, copies, after):
    n = len(bufs)

    def body(*refs):
        for cpy in copies(refs[:n], refs[n], refs[n + 1]):
            cpy.wait_send()
            cpy.wait_recv()

    return list(pl.pallas_call(
        body, name=name, out_shape=tuple(pltpu.HBM(b.shape, b.dtype) for b in bufs),
        in_specs=[HBM] * n + [SEM, SEM, ANY], out_specs=tuple([HBM] * n),
        input_output_aliases={i: i for i in range(n)},
        compiler_params=pltpu.CompilerParams(has_side_effects=DATAFLOW),
    )(*bufs, send, recv, after))


def _gather_copies(bufs, send, recv):
    x, y, c = _place()
    out = []
    for u, buf in enumerate(bufs):
        half = buf.shape[1] // 2
        mine = buf.at[2 * x + y, pl.ds(c * half, half)]
        out += [_remote(mine, mine, send.at[3 * u + j], recv.at[3 * u + j], (*chip, c))
                for j, chip in enumerate(_other_chips(x, y))]
    return out


def _exchange_copies(bufs, send, recv):
    x, y, c = _place()
    n = len(bufs) // 2
    out = []
    for k in range(n):
        half = bufs[k].shape[1] // 2
        theirs = bufs[k].at[pl.ds(0, N_CHIPS), pl.ds((1 - c) * half, half)]
        out.append(_remote(theirs, bufs[n + k], send.at[k], recv.at[k], (x, y, 1 - c)))
    return out


def _all_to_all_copies(bufs, send, recv):
    x, y, c = _place()
    n = len(bufs) // 2
    return [_remote(bufs[k].at[2 * chip[0] + chip[1]], bufs[n + k].at[j], send.at[3 * k + j], recv.at[3 * k + j], (*chip, c))
            for k in range(n) for j, chip in enumerate(_other_chips(x, y))]


def _forward_copies(bufs, send, recv):
    x, y, c = _place()
    out = []
    for u, buf in enumerate(bufs):
        half = buf.shape[1] // 2
        for j, chip in enumerate(_other_chips(x, y)):
            landed = buf.at[2 * chip[0] + chip[1], pl.ds(c * half, half)]
            out.append(_remote(landed, landed, send.at[3 * u + j], recv.at[3 * u + j], (x, y, 1 - c)))
    return out


def _gather8_copies(bufs, send, recv):
    x, y, c = _place()
    targets = [(x, y, 1 - c)] + [(*chip, c) for chip in _other_chips(x, y)]
    out = []
    for b, buf in enumerate(bufs):
        mine = buf.at[4 * x + 2 * y + c]
        out += [_remote(mine, mine, send.at[N_CHIPS * b + k], recv.at[N_CHIPS * b + k], to) for k, to in enumerate(targets)]
    return out


def _forward_slots(bufs, name):
    n = len(bufs)

    def body(*refs):
        bufs_ = refs[n:2 * n]
        send, recv, _ = refs[2 * n:]
        x, y, c = _place()
        chips = _other_chips(x, y)
        sent = []
        for b in range(n):
            for j, chip in enumerate(chips):
                slot = bufs_[b].at[4 * chip[0] + 2 * chip[1] + c]
                cpy = _remote(slot, slot, send.at[3 * b + j], recv.at[3 * b + j], (x, y, 1 - c))
                cpy.start()
                sent.append(cpy)
        for b in range(n):
            for j, chip in enumerate(chips):
                slot = bufs_[b].at[4 * chip[0] + 2 * chip[1] + 1 - c]
                _remote(slot, slot, send.at[3 * b + j], recv.at[3 * b + j], (x, y, c)).wait_recv()
        for cpy in sent:
            cpy.wait_send()

    shapes, aliases = _in_place(bufs)
    return list(_comm_call(body, name, bufs, shapes, (N_CHIPS - 1) * n, 0, aliases))


FWD_GROUPS = {'mix': ('ar_out', 'ff_in0', 'ff_out0'), 'l1': ('gm_in', 'gm_out', 'ff_in1', 'ff_out1')}
GRAD_LAYOUT = {'ff_in0': (0, 0), 'ff_in1': (0, 1), 'ff_out0': (1, 0), 'ff_out1': (1, 1), 'ar_in': (2, 0), 'ar_out': (3, 0),
               'gm_in': (4, 0), 'gm_out': (5, 0)}


class _MeshLink:
    def __init__(self, place, shards):
        self.place = place
        self.ready = {'ar_in': shards['ar_in']}
        self.pending, after = {}, shards['ar_in']
        for group, names in FWD_GROUPS.items():
            send, recv, bufs, token = _split_start(f"gather_{group}_start", [shards[n] for n in names], _gather_copies,
                                                   3 * len(names), after)
            self.pending[group] = (names, send, recv, bufs)
            after = token
        self.start_token = after[0, 0]
        self.forwarding, self.exchanging, self.sent, self.last_token = {}, {}, {}, None

    def prefetch(self, group, after):
        names, send, recv, bufs = self.pending.pop(group)
        bufs = _split_wait(f"gather_{group}_wait", bufs, send, recv, _gather_copies, after)
        send, recv, bufs, token = _split_start(f"forward_{group}_start", bufs, _forward_copies, 3 * len(names))
        self.forwarding[group] = (names, send, recv, bufs)
        return token[0, 0]

    def weights(self, group, after):
        if group in self.forwarding:
            names, send, recv, bufs = self.forwarding.pop(group)
            self.ready.update(zip(names, _split_wait(f"forward_{group}_wait", bufs, send, recv, _forward_copies, after)))
        return self.ready

    def gradients(self, group, grads, after=None):
        tok = self.poll(next(iter(grads.values())))
        names, mine = list(grads), list(grads.values())
        landing = [lax.empty((N_CHIPS, g.shape[1] // 2, g.shape[2]), g.dtype) for g in mine]
        send, recv, bufs, token = _split_start(f"exchange_{group}_start", mine + landing, _exchange_copies, len(names), after)
        self.exchanging[group] = (names, send, recv, bufs)
        self.last_token = token
        return token[0, 0] + tok

    def poll(self, after):
        tok = 0.0
        for group in list(self.exchanging):
            names, send, recv, bufs = self.exchanging.pop(group)
            bufs = _split_wait(f"exchange_{group}_wait", bufs, send, recv, _exchange_copies, after)
            sums = [_add_halves(g, r, self.place) for g, r in zip(bufs[:len(names)], bufs[len(names):])]
            landing = [lax.empty((N_CHIPS - 1,) + s.shape[1:], s.dtype) for s in sums]
            send, recv, bufs, token = _split_start(f"grads_{group}_start", sums + landing, _all_to_all_copies, 3 * len(names))
            self.sent[group] = (names, send, recv, bufs)
            self.last_token = token
            tok = tok + token[0, 0]
        return tok

    def reduce(self, groups, after):
        units = {}
        for group in groups:
            names, send, recv, bufs = self.sent.pop(group)
            bufs = _split_wait(f"grads_{group}_wait", bufs, send, recv, _all_to_all_copies, after)
            units.update(zip(names, zip(bufs[:len(names)], bufs[len(names):])))
        n_layers = {p: 1 + max(l for pp, l in GRAD_LAYOUT.values() if pp == p) for p, _ in GRAD_LAYOUT.values()}
        out = {}
        for name, (p, layer) in GRAD_LAYOUT.items():
            if name in units:
                out[p] = _add_chips(*units[name], self.place, out.get(p), layer, n_layers[p])
        params = sorted(out)
        return dict(zip(params, _join_halves([out[p] for p in params])))


def _rope_tables(n):
    n_rows = n // GRID_W
    freqs = ROPE_THETA ** (-jnp.arange(ROPE_PAIRS, dtype=F32) / ROPE_PAIRS)
    ang_r = jnp.arange(n_rows, dtype=F32)[:, None] * freqs
    ang_c = jnp.arange(GRID_W, dtype=F32)[:, None] * freqs

    def per_token(of_row, of_col):
        r = jnp.broadcast_to(of_row[:, None, :], (n_rows, GRID_W, ROPE_PAIRS)).reshape(n, ROPE_PAIRS)
        c = jnp.broadcast_to(of_col[None, :, :], (n_rows, GRID_W, ROPE_PAIRS)).reshape(n, ROPE_PAIRS)
        return r, c

    cos_r, cos_c = per_token(jnp.cos(ang_r), jnp.cos(ang_c))
    sin_r, sin_c = per_token(jnp.sin(ang_r), jnp.sin(ang_c))
    cos = jnp.concatenate([cos_r, cos_r, cos_c, cos_c], axis=-1)
    sin = jnp.concatenate([-sin_r, sin_r, -sin_c, sin_c], axis=-1)
    return cos, sin


def _ffn_fwd(h2, w1, w2, tag):
    r, a = _matmul(h2, w1, kind='nn', b_split='n', out_dtype=BF16, epilogue='relu2', name=f"ffn_in_{tag}")
    f = _matmul(a, w2, kind='nn', b_split='k', out_dtype=F32, name=f"ffn_out_{tag}")
    return r, a, f


def _ffn_bwd(d_f, h2, r, a, w1, w2, tag):
    d_u = _matmul(d_f, w2, kind='nt', b_split='k', out_dtype=BF16, epilogue='times2x', extra=r, name=f"ffn_out_dx_{tag}")
    d_w2 = _matmul(a, d_f, kind='tn', out_split='k', out_dtype=BF16, name=f"ffn_out_dw_{tag}")
    d_w1 = _matmul(h2, d_u, kind='tn', out_split='n', out_dtype=BF16, name=f"ffn_in_dw_{tag}")
    d_h2 = _matmul(d_u, w1, kind='nt', b_split='n', out_dtype=F32, name=f"ffn_in_dx_{tag}")
    return d_h2, d_w1, d_w2


def _local_step(xl0, xc0, target, ml, mc0, sp, link):
    n_lat, n_ctx = xl0.shape[0], xc0.shape[0]
    one = lambda v: 1.0 + v
    g = [[sp['norm_g'][i, k][None, :] for k in range(4)] for i in range(2)]

    sh1, sc1, gt1, sh2, sc2, gt2 = ml[0]
    big = link.weights('ar', None)
    sh1 = sh1 + link.start_token
    n_all = n_lat + n_ctx
    h_all = _norm_fwd(xl0, g[0][0], one(sc1), b=sh1, out_dtype=BF16, name="l0_mod1", into=(0, n_all, None))
    h_all = _norm_fwd(xc0, g[0][0], one(mc0[1]), b=mc0[0], out_dtype=BF16, name="l0_mod1_ctx", into=(n_lat, n_all, h_all))
    proj_l = _matmul(h_all, big['ar_in'], kind='nn', b_split='n', out_dtype=F32, a_rows=(0, n_lat), name="ar_in_lat")
    proj_c = _matmul(h_all, big['ar_in'], kind='nn', b_split='n', out_dtype=F32, a_rows=(n_lat, n_ctx), name="ar_in_ctx")
    cos_l, sin_l = _rope_tables(n_lat)
    cos_c, sin_c = jnp.ones((n_ctx, HEAD_DIM), F32), jnp.zeros((n_ctx, HEAD_DIM), F32)
    q_g, k_g = sp['q_g'], sp['k_g']
    _, k_all, v_all = _qk_fwd(proj_c, q_g, k_g, cos_c, sin_c, name="qk_fwd_ctx", kv_into=(0, n_all, None))
    q_l, k_all, v_all = _qk_fwd(proj_l, q_g, k_g, cos_l, sin_l, name="qk_fwd_lat", kv_into=(n_ctx, n_all, (k_all, v_all)))
    cat, lse = _attn_fwd(q_l, k_all, v_all)
    conv_b = sp['conv_b'] + link.prefetch('mix', cat)
    xs = _conv_fwd(proj_l, proj_c, sp['conv_w'], conv_b)
    rnn_w = [(sp['wa'][d], sp['ba'][d][None, :], sp['wx'][d], sp['bx'][d][None, :], sp['lam'][d][None, :]) for d in range(2)]
    h_f, hp_f = _rglru_fwd(xs, *rnn_w[0], reverse=False, n_ctx=n_ctx, name="rglru_fwd_f")
    h_r, hp_r = _rglru_fwd(xs, *rnn_w[1], reverse=True, n_ctx=n_ctx, name="rglru_fwd_r")
    cat = _rnn_out_fwd(h_f, h_r, proj_l, n_ctx, n_ctx, cat)
    w_mix = link.weights('mix', cat)
    ol0 = _matmul(cat, w_mix['ar_out'], kind='nn', b_split='k', out_dtype=F32, name="ar_out")
    xm0, h2_0 = _res_mod_fwd(ol0, xl0, g[0][1], gt1, g[0][2], one(sc2), sh2, name="l0_res1_mod2")
    r0, a0, f0 = _ffn_fwd(h2_0, w_mix['ff_in0'], w_mix['ff_out0'], "l0")
    th1, tc1, tg1, th2, tc2, tg2 = ml[1]
    xl1, hl1 = _res_mod_fwd(f0, xm0, g[0][3], gt2 + link.prefetch('l1', f0), g[1][0], one(tc1), th1, name="l0_res2_l1_mod1")
    w_l1 = link.weights('l1', xl1)
    z = _matmul(hl1, w_l1['gm_in'], kind='nn', b_split='n', bias=sp['gm_b_in'], out_dtype=F32, name="gm_in")
    b_sp_t = sp['gm_b_sp'].T
    gated = _gmlp_fwd(z, sp['gm_v_g'], sp['gm_v_b'], sp['gm_w_sp'], b_sp_t)
    ol1 = _matmul(gated, w_l1['gm_out'], kind='nn', b_split='k', out_dtype=F32, name="gm_out")
    xm1, h2_1 = _res_mod_fwd(ol1, xl1, g[1][1], tg1, g[1][2], one(tc2), th2, name="l1_res1_mod2")
    r1, a1, f1 = _ffn_fwd(h2_1, w_l1['ff_in1'], w_l1['ff_out1'], "l1")

    dy, d_f1, dg13, d_tg2, loss = _final_res_loss(f1, xm1, g[1][3], tg2, target)

    d_h2, dw_ff_in1, dw_ff_out1 = _ffn_bwd(d_f1, h2_1, r1, a1, w_l1['ff_in1'], w_l1['ff_out1'], "l1")
    tok = link.gradients('ffn1', {'ff_in1': dw_ff_in1, 'ff_out1': dw_ff_out1})
    dxm1, d_ol1, dg12, d_tc2, d_th2, dg11, d_tg1 = _mod_res_bwd(d_h2, xm1, g[1][2], one(tc2) + tok, dy, ol1, g[1][1], tg1,
                                                                name="l1_mod2_res1_bwd")
    d_gated = _matmul(d_ol1, w_l1['gm_out'], kind='nt', b_split='k', out_dtype=F32, name="gm_out_dx")
    dw_gm_out = _matmul(gated, d_ol1, kind='tn', out_split='k', out_dtype=BF16, name="gm_out_dw")
    d_z, d_gm_b_in, d_vg, d_vb, d_wsp, d_bsp_t = _gmlp_bwd(z, d_gated, sp['gm_v_g'], sp['gm_v_b'], sp['gm_w_sp'], b_sp_t)
    dw_gm_in = _matmul(hl1, d_z, kind='tn', out_split='n', out_dtype=BF16, name="gm_in_dw")
    d_hl1 = _matmul(d_z, w_l1['gm_in'], kind='nt', b_split='n', out_dtype=F32, name="gm_in_dx")
    tok = link.gradients('gm', {'gm_in': dw_gm_in, 'gm_out': dw_gm_out})

    dxl1, d_f0, dg10, d_tc1, d_th1, dg03, d_gt2 = _mod_res_bwd(d_hl1, xl1, g[1][0], one(tc1) + tok, dxm1, f0, g[0][3], gt2,
                                                               name="l1_mod1_l0_res2_bwd")
    d_h2, dw_ff_in0, dw_ff_out0 = _ffn_bwd(d_f0, h2_0, r0, a0, w_mix['ff_in0'], w_mix['ff_out0'], "l0")
    tok = link.gradients('ffn0', {'ff_in0': dw_ff_in0, 'ff_out0': dw_ff_out0})
    dxm0, d_ol0, dg02, d_sc2, d_sh2, dg01, d_gt1 = _mod_res_bwd(d_h2, xm0, g[0][2], one(sc2) + tok, dxl1, ol0, g[0][1], gt1,
                                                                name="l0_mod2_res1_bwd")
    d_cat = _matmul(d_ol0, w_mix['ar_out'], kind='nt', b_split='k', out_dtype=F32, name="ar_out_dx")
    dw_ar_out = _matmul(cat, d_ol0, kind='tn', out_split='k', out_dtype=BF16, name="ar_out_dw")
    dq, dk_all, dv_all = _attn_bwd(q_l, k_all, v_all, cat, lse, d_cat)
    tok = link.poll(dq)
    d_h, d_gate = _rnn_out_bwd(d_cat, h_f, h_r, proj_l, n_ctx, n_ctx)
    rnn_wb = [(wa_, ba_ + tok, wx_, bx_, lam_) for wa_, ba_, wx_, bx_, lam_ in rnn_w]
    dxs_f, d_wa0, d_ba0, d_wx0, d_bx0, d_lam0 = _rglru_bwd(
        xs, hp_f, d_h, *rnn_wb[0], reverse=False, n_ctx=n_ctx, name="rglru_bwd_f")
    dxs_r, d_wa1, d_ba1, d_wx1, d_bx1, d_lam1 = _rglru_bwd(
        xs, hp_r, d_h, *rnn_wb[1], reverse=True, n_ctx=n_ctx, name="rglru_bwd_r")
    d_xr_l, d_xr_c, d_cw, d_cb = _conv_bwd(dxs_f, dxs_r, proj_l, proj_c, sp['conv_w'])
    dp_qk_l, d_qg, d_kg_l = _qk_bwd(dq, dk_all, proj_l, q_g, k_g, cos_l, sin_l, name="qk_bwd_lat", dk_row0=n_ctx)
    dp_qk_c, _, d_kg_c = _qk_bwd(None, dk_all, proj_c, q_g, k_g, cos_c, sin_c, name="qk_bwd_ctx")
    d_proj = _assemble_d_proj(dp_qk_l, dp_qk_c, dv_all, d_xr_l, d_xr_c, d_gate)
    dw_ar_in = _matmul(h_all, d_proj, kind='tn', out_split='n', out_dtype=BF16, name="ar_in_dw")
    d_hl = _matmul(d_proj, big['ar_in'], kind='nt', b_split='n', out_dtype=F32, a_rows=(0, n_lat), name="ar_in_dx_lat")
    d_hc = _matmul(d_proj, big['ar_in'], kind='nt', b_split='n', out_dtype=F32, a_rows=(n_lat, n_ctx), name="ar_in_dx_ctx")
    grad_x, dg00, d_sc1, d_sh1 = _norm_bwd(d_hl, xl0, g[0][0], one(sc1), extra=dxm0, out_dtype=F32, name="l0_mod1_bwd")
    _, dg00c, d_mc_scale, d_mc_shift = _norm_bwd(d_hc, xc0, g[0][0], one(mc0[1]), out_dtype=BF16, name="l0_mod1_ctx_bwd")

    zeros_d = jnp.zeros_like(d_sh1)
    small = {
        'd_ml0': jnp.concatenate([d_sh1, d_sc1, d_gt1, d_sh2, d_sc2, d_gt2], axis=1),
        'd_ml1': jnp.concatenate([d_th1, d_tc1, d_tg1, d_th2, d_tc2, d_tg2], axis=1),
        'd_mc0': jnp.concatenate([d_mc_shift, d_mc_scale] + [zeros_d] * 4, axis=1),
        'norm_g': jnp.stack([jnp.concatenate([dg00 + dg00c, dg01, dg02, dg03], axis=0),
                             jnp.concatenate([dg10, dg11, dg12, dg13], axis=0)]),
        'q_g': d_qg, 'k_g': d_kg_l + d_kg_c, 'conv_w': d_cw, 'conv_b': d_cb,
        'wa': jnp.stack([d_wa0, d_wa1]), 'ba': jnp.concatenate([d_ba0, d_ba1], axis=0),
        'wx': jnp.stack([d_wx0, d_wx1]), 'bx': jnp.concatenate([d_bx0, d_bx1], axis=0),
        'lam': jnp.concatenate([d_lam0, d_lam1], axis=0),
        'gm_b_in': d_gm_b_in, 'gm_v_g': d_vg, 'gm_v_b': d_vb, 'gm_w_sp': d_wsp, 'gm_b_sp': d_bsp_t.T,
        'loss': loss,
    }
    return grad_x, small, {'ar_in': dw_ar_in, 'ar_out': dw_ar_out}


MOD_ROWS = 16
SMALL_F32 = ('d_ml0', 'd_ml1', 'd_mc0', 'norm_g', 'q_g', 'k_g', 'conv_w', 'conv_b', 'ba', 'bx', 'lam', 'gm_b_in', 'gm_v_g',
             'gm_v_b', 'gm_b_sp', 'loss')
SMALL_BF16 = ('wa', 'wx', 'gm_w_sp')


def _silu(v):
    return v * _sigmoid(v)


def _chip_concat(gathered, axis):
    return jnp.concatenate([gathered[2 * q] for q in range(N_CHIPS)], axis=axis)


def kernel(x, c, ctx, c_ctx, w_mod, b_mod, norm_g, w_ff_in, w_ff_out, ar_w_in, ar_q_g, ar_k_g, ar_conv_w, ar_conv_b, ar_wa, ar_ba, ar_wx, ar_bx, ar_lambda, ar_w_out, gm_w_in, gm_b_in, gm_v_g, gm_v_b, gm_w_sp, gm_b_sp, gm_w_out, loss_target, m_c_ctx, m_w_mod, m_b_mod, m_norm_g, m_w_ff_in, m_w_ff_out, m_ar_w_in, m_ar_q_g, m_ar_k_g, m_ar_conv_w, m_ar_conv_b, m_ar_wa, m_ar_ba, m_ar_wx, m_ar_bx, m_ar_lambda, m_ar_w_out, m_gm_w_in, m_gm_b_in, m_gm_v_g, m_gm_v_b, m_gm_w_sp, m_gm_b_sp, m_gm_w_out, v_c_ctx, v_w_mod, v_b_mod, v_norm_g, v_w_ff_in, v_w_ff_out, v_ar_w_in, v_ar_q_g, v_ar_k_g, v_ar_conv_w, v_ar_conv_b, v_ar_wa, v_ar_ba, v_ar_wx, v_ar_bx, v_ar_lambda, v_ar_w_out, v_gm_w_in, v_gm_b_in, v_gm_v_g, v_gm_v_b, v_gm_w_sp, v_gm_b_sp, v_gm_w_out):
    weights = dict(c_ctx=c_ctx, w_mod=w_mod, b_mod=b_mod, norm_g=norm_g, w_ff_in=w_ff_in, w_ff_out=w_ff_out, ar_w_in=ar_w_in,
                   ar_q_g=ar_q_g, ar_k_g=ar_k_g, ar_conv_w=ar_conv_w, ar_conv_b=ar_conv_b, ar_wa=ar_wa, ar_ba=ar_ba, ar_wx=ar_wx,
                   ar_bx=ar_bx, ar_lambda=ar_lambda, ar_w_out=ar_w_out, gm_w_in=gm_w_in, gm_b_in=gm_b_in, gm_v_g=gm_v_g,
                   gm_v_b=gm_v_b, gm_w_sp=gm_w_sp, gm_b_sp=gm_b_sp, gm_w_out=gm_w_out)
    m_in = dict(c_ctx=m_c_ctx, w_mod=m_w_mod, b_mod=m_b_mod, norm_g=m_norm_g, w_ff_in=m_w_ff_in, w_ff_out=m_w_ff_out,
                ar_w_in=m_ar_w_in, ar_q_g=m_ar_q_g, ar_k_g=m_ar_k_g, ar_conv_w=m_ar_conv_w, ar_conv_b=m_ar_conv_b, ar_wa=m_ar_wa,
                ar_ba=m_ar_ba, ar_wx=m_ar_wx, ar_bx=m_ar_bx, ar_lambda=m_ar_lambda, ar_w_out=m_ar_w_out, gm_w_in=m_gm_w_in,
                gm_b_in=m_gm_b_in, gm_v_g=m_gm_v_g, gm_v_b=m_gm_v_b, gm_w_sp=m_gm_w_sp, gm_b_sp=m_gm_b_sp, gm_w_out=m_gm_w_out)
    v_in = dict(c_ctx=v_c_ctx, w_mod=v_w_mod, b_mod=v_b_mod, norm_g=v_norm_g, w_ff_in=v_w_ff_in, w_ff_out=v_w_ff_out,
                ar_w_in=v_ar_w_in, ar_q_g=v_ar_q_g, ar_k_g=v_ar_k_g, ar_conv_w=v_ar_conv_w, ar_conv_b=v_ar_conv_b, ar_wa=v_ar_wa,
                ar_ba=v_ar_ba, ar_wx=v_ar_wx, ar_bx=v_ar_bx, ar_lambda=v_ar_lambda, ar_w_out=v_ar_w_out, gm_w_in=v_gm_w_in,
                gm_b_in=v_gm_b_in, gm_v_g=v_gm_v_g, gm_v_b=v_gm_v_b, gm_w_sp=v_gm_w_sp, gm_b_sp=v_gm_b_sp, gm_w_out=v_gm_w_out)

    xi, yi, ci = lax.axis_index("x"), lax.axis_index("y"), lax.axis_index("c")
    chip = 2 * xi + yi
    dev = 4 * xi + 2 * yi + ci
    place = jnp.stack([chip, ci]).astype(jnp.int32)
    n_lat, d = x.shape[1], x.shape[2]
    d6 = 6 * d
    cols_mod = w_mod.shape[2]

    mine = [c, norm_g, ar_conv_w[0], ar_ba[0], ar_bx[0], ar_lambda[0], gm_b_in, gm_v_g, gm_v_b]
    gathered = _allgather8([_pack(mine)], "gather_small_params")[0]
    first = _split_start("gather_first_start", [_cast_shard(ar_w_in, place, 0, "cast_ar_in")], _gather_copies, N_CHIPS - 1, gathered)
    parts = _unpack_devices(gathered, [a.shape for a in mine])
    c_all = parts[0].reshape(N_DEV, d)
    sp = {'norm_g': _chip_concat(parts[1], 2), 'q_g': ar_q_g, 'k_g': ar_k_g, 'conv_w': _chip_concat(parts[2], 1),
          'conv_b': ar_conv_b, 'wa': ar_wa[0], 'ba': _chip_concat(parts[3], 1), 'wx': ar_wx[0], 'bx': _chip_concat(parts[4], 1),
          'lam': _chip_concat(parts[5], 1), 'gm_b_in': _chip_concat(parts[6], 1), 'gm_v_g': _chip_concat(parts[7], 1),
          'gm_v_b': _chip_concat(parts[8], 1), 'gm_w_sp': gm_w_sp[0], 'gm_b_sp': gm_b_sp[0]}

    def mod_operand(c_rows, cc):
        row = lax.broadcasted_iota(jnp.int32, (MOD_ROWS - N_DEV, d), 0)
        lower = jnp.where(row == 0, jnp.broadcast_to(_silu(cc), (MOD_ROWS - N_DEV, d)), 0.0)
        sig = _sigmoid(cc)
        return jnp.concatenate([_silu(c_rows), lower], axis=0), sig * (1.0 + cc * (1.0 - sig))

    s_mod, dsilu_ctx = _small(mod_operand, [((MOD_ROWS, d), F32), ((1, d), F32)], c_all, c_ctx[None, :], name="mod_operand")
    b_mod_mine = lax.dynamic_slice(b_mod, (0, chip * cols_mod), (2, cols_mod))
    mod = [_matmul(s_mod, w_mod, kind='nn', b_layer=i, bias=b_mod_mine[i][None, :], out_dtype=F32, name=f"mod_fwd_{i}")
           for i in range(2)]
    mod_all = _allgather8([jnp.concatenate(mod, axis=0)], "gather_mod")[0]
    mod_all = _chip_concat(mod_all, 1).reshape(2, MOD_ROWS, d6)
    ml = [jnp.split(lax.dynamic_slice(mod_all[i], (dev, 0), (1, d6)), 6, axis=1) for i in range(2)]
    mc0 = jnp.split(mod_all[0, N_DEV:N_DEV + 1], 6, axis=1)[:2]

    names = ('w_ff_in', 'w_ff_out', 'ar_w_in', 'ar_w_out', 'gm_w_in', 'gm_w_out')
    keys = {'w_ff_in': ('ff_in0', 'ff_in1'), 'w_ff_out': ('ff_out0', 'ff_out1'), 'ar_w_in': ('ar_in',), 'ar_w_out': ('ar_out',),
            'gm_w_in': ('gm_in',), 'gm_w_out': ('gm_out',)}
    shards = {key: _cast_shard(weights[n], place, layer, f"cast_{key}", after=first[3]) for n in names
              for layer, key in enumerate(keys[n]) if key != 'ar_in'}
    send, recv, bufs, _ = first
    bufs = _split_wait("gather_first_wait", bufs, send, recv, _gather_copies, mod_all)
    send, recv, bufs, token = _split_start("forward_first_start", bufs, _forward_copies, N_CHIPS - 1)
    shards['ar_in'] = _split_wait("forward_first_wait", bufs, send, recv, _forward_copies, token)[0]
    link = _MeshLink(place, shards)

    grad_x, small, last_grads = _local_step(x[0], ctx[0], loss_target[0], ml, mc0, sp, link)

    def step(n, grad):
        return _adamw(weights[n], grad.reshape(weights[n].shape), m_in[n], v_in[n], f"adamw_{n}", rewrite_grad=n in names)

    small_f32, small_bf16 = [small[k] for k in SMALL_F32], [small[k] for k in SMALL_BF16]
    dev_arr = dev.astype(jnp.int32)[None]
    slots = [_into_slot(_pack(small_f32), dev_arr, "small_grads_slot_f32"),
             _into_slot(_pack(small_bf16, BF16), dev_arr, "small_grads_slot_bf16")]
    s_send, s_recv, slots, s_token = _split_start("small_grads_start", slots, _gather8_copies, 2 * N_CHIPS, grad_x)
    link.gradients('ar', last_grads, s_token)
    link.poll(link.last_token)
    reduced = link.reduce(('ffn1', 'gm', 'ffn0'), link.last_token)
    stepped = {n: step(n, reduced[names.index(n)]) for n in ('w_ff_in', 'w_ff_out', 'gm_w_in', 'gm_w_out')}
    reduced = link.reduce(('ar',), stepped['gm_w_out'][1])
    stepped.update({n: step(n, reduced[names.index(n)]) for n in ('ar_w_in', 'ar_w_out')})
    slots = _split_wait("small_grads_wait", slots, s_send, s_recv, _gather8_copies, stepped['ar_w_out'][1])
    small8, small8_bf16 = _forward_slots(slots, "small_grads_forward")
    total = dict(zip(SMALL_F32, _unpack(_sum_devices(small8).reshape(-1), [a.shape for a in small_f32])))
    total.update(zip(SMALL_BF16, _unpack(_sum_devices(small8_bf16).reshape(-1), [a.shape for a in small_bf16])))
    per_dev = _unpack_devices(small8, [(d6,), (d6,)])
    pad_rows = jnp.zeros((MOD_ROWS - N_DEV - 1, d6), F32)
    d_mod = [jnp.concatenate([per_dev[0], total['d_mc0'], pad_rows], axis=0),
             jnp.concatenate([per_dev[1], jnp.zeros((MOD_ROWS - N_DEV, d6), F32)], axis=0)]
    d_mod_mine = [lax.dynamic_slice(dm, (0, chip * cols_mod), (MOD_ROWS, cols_mod)) for dm in d_mod]
    g_w_mod = None
    for i in range(2):
        g_w_mod = _matmul(s_mod, d_mod_mine[i], kind='tn', out_dtype=F32, out_stack=(i, 2, g_w_mod), name=f"mod_dw_{i}")
    d_s_part = _matmul(d_mod_mine[0], w_mod, kind='nt', b_layer=0, out_dtype=F32, name="mod_ds")
    d_s_all = _allgather8([d_s_part[N_DEV:]], "gather_mod_ds")[0]

    def c_ctx_grad(parts_, dsilu):
        acc = parts_[0, 0:1]
        for q in range(1, N_CHIPS):
            acc = acc + parts_[2 * q, 0:1]
        return (acc * dsilu,)

    g_c_ctx = _small(c_ctx_grad, [((1, d), F32)], d_s_all, dsilu_ctx, name="c_ctx_grad")[0].reshape(d)

    def mine_of(full_grad, axis, n_shard):
        return lax.dynamic_slice_in_dim(full_grad, chip * n_shard, n_shard, axis=axis)

    grads_out = {
        'c_ctx': g_c_ctx, 'w_mod': g_w_mod,
        'b_mod': jnp.stack([total['d_ml0'][0] + total['d_mc0'][0], total['d_ml1'][0]]),
        'norm_g': mine_of(total['norm_g'], 2, norm_g.shape[2]),
        'ar_q_g': total['q_g'], 'ar_k_g': total['k_g'], 'ar_conv_w': mine_of(total['conv_w'], 1, ar_conv_w.shape[2])[None],
        'ar_conv_b': total['conv_b'], 'ar_wa': total['wa'][None], 'ar_ba': mine_of(total['ba'], 1, ar_ba.shape[2])[None],
        'ar_wx': total['wx'][None], 'ar_bx': mine_of(total['bx'], 1, ar_bx.shape[2])[None],
        'ar_lambda': mine_of(total['lam'], 1, ar_lambda.shape[2])[None],
        'gm_b_in': mine_of(total['gm_b_in'], 1, gm_b_in.shape[1]),
        'gm_v_g': mine_of(total['gm_v_g'], 1, gm_v_g.shape[1]), 'gm_v_b': mine_of(total['gm_v_b'], 1, gm_v_b.shape[1]),
        'gm_w_sp': total['gm_w_sp'][None], 'gm_b_sp': total['gm_b_sp'][None],
    }
    stepped.update({n: step(n, grad) for n, grad in grads_out.items()})
    stepped = [stepped[n] for n in weights]
    loss = total['loss'].reshape(())
    return (loss, grad_x[None], *[s[0] for s in stepped], *[s[1] for s in stepped], *[s[2] for s in stepped],
            *[s[3] for s in stepped])
```

```python
import math

import jax
import jax.numpy as jnp
from jax import lax
from jax.experimental import pallas as pl
from jax.experimental.pallas import tpu as pltpu

F32 = jnp.float32
BF16 = jnp.bfloat16
MESH = pl.DeviceIdType.MESH
ANY = pl.BlockSpec(memory_space=pl.ANY)

VMEM_LIMIT_BYTES = 52 * 1024 * 1024
LANES = 128
N_CHIPS = 4
N_DEV = 8

HEAD_DIM = 128
N_HEADS = 8
N_KV = 2
GROUP = N_HEADS // N_KV
ATTN_W = N_HEADS * HEAD_DIM
KV_W = N_KV * HEAD_DIM
D_RNN = 1024
RNN_BLOCKS = 8
RNN_BW = D_RNN // RNN_BLOCKS
CONV_W = 4
RG_C = 8.0
GRID_W = 64
ROPE_THETA = 10000.0
ROPE_PAIRS = HEAD_DIM // 4
GM_GROUPS = 16
CHUNK = 128
EPS = 1e-6
ADAM_LR, ADAM_B1, ADAM_B2, ADAM_EPS, ADAM_WD, ADAM_STEP = 0.001, 0.9, 0.999, 1e-08, 0.01, 10
GELU_C = math.sqrt(2.0 / math.pi)
LOG2E = math.log2(math.e)


def _params(sem=None):
    return pltpu.CompilerParams(dimension_semantics=sem, vmem_limit_bytes=VMEM_LIMIT_BYTES)


def _tile(dim, pref, unit):
    best = None
    t = unit
    while t <= min(dim, pref):
        if dim % t == 0:
            best = t
        t += unit
    return best if best is not None else dim


def _full(shape):
    nd = len(shape)
    return pl.BlockSpec(shape, lambda *_: (0,) * nd)


def _blocked_map(split, per_q):
    assert split == 'n'
    return lambda r, c: (c // per_q, r, c % per_q)


def _logical_shape(arr, split):
    if split == 'n':
        return arr.shape[1], arr.shape[0] * arr.shape[2]
    if split == 'k':
        return arr.shape[0] * arr.shape[1], arr.shape[2]
    return arr.shape[-2:]


def _matmul(a, b, *, kind, name, out_dtype, b_split=None, out_split=None, bias=None, epilogue=None, extra=None,
            a_rows=None, b_layer=None, out_stack=None, pref=(1024, 1024, 2048)):
    if b_split == 'k':
        b, b_split = b.reshape(-1, b.shape[-1]), None
    blocked_rows_out = out_split == 'k'
    if blocked_rows_out:
        assert epilogue != 'relu2'
        out_split = None
    b_rows, b_cols = _logical_shape(b, b_split)
    row0, nt_groups = 0, 1
    if kind == 'nn':
        m, kc = a.shape
        n = b_cols
        assert b_rows == kc
    elif kind == 'nt':
        m, kc = a.shape
        n = b_rows
        assert b_cols == kc
    if a_rows is not None:
        assert kind != 'tn'
        row0, m = a_rows
    if kind == 'tn':
        kc, m = a.shape
        n = b_cols
        assert b_rows == kc
    b_row_ext = b.shape[1] if b_split == 'k' else b_rows
    b_col_ext = b.shape[2] if b_split == 'n' else b_cols
    out_row_ext = m // N_CHIPS if out_split == 'k' else m
    out_col_ext = n // N_CHIPS if out_split == 'n' else n
    if kind == 'nn':
        ti = _tile(math.gcd(min(m, out_row_ext), row0), pref[0], 16)
        tj = _tile(math.gcd(b_col_ext, out_col_ext), pref[1], LANES)
        tl = _tile(b_row_ext, pref[2], LANES)
        a_spec = pl.BlockSpec((ti, tl), lambda i, j, l: (i + row0 // ti, l))
        b_tile, b_rc = (tl, tj), (lambda i, j, l: (l, j))
        dims = (((1,), (0,)), ((), ()))
    elif kind == 'nt':
        ti = _tile(math.gcd(min(m, out_row_ext), row0), pref[0], 16)
        tj = _tile(math.gcd(b_row_ext, out_col_ext), pref[1], LANES)
        if b_split == 'n' and b.shape[2] < pref[2]:
            nt_groups = max(k for k in (1, 2, 4) if k * b.shape[2] <= pref[2])
        tl = nt_groups * b.shape[2] if nt_groups > 1 else _tile(b_col_ext, pref[2], LANES)
        a_spec = pl.BlockSpec((ti, tl), lambda i, j, l: (i + row0 // ti, l))
        b_tile, b_rc = (tj, tl), (lambda i, j, l: (j, l))
        dims = (((1,), (1,)), ((), ()))
    else:
        ti = _tile(out_row_ext, pref[0], LANES)
        tj = _tile(math.gcd(b_col_ext, out_col_ext), pref[1], LANES)
        tl = _tile(b_row_ext, pref[2], 16)
        a_spec = pl.BlockSpec((tl, ti), lambda i, j, l: (l, i))
        b_tile, b_rc = (tl, tj), (lambda i, j, l: (l, j))
        dims = (((0,), (0,)), ((), ()))
    grid = (m // ti, n // tj, kc // tl)
    n_l = grid[2]

    if nt_groups > 1:
        b_spec = pl.BlockSpec((nt_groups, tj, b.shape[2]), lambda i, j, l: (l, j, 0))
    elif b_layer is not None:
        b_spec = pl.BlockSpec((None,) + b_tile, lambda i, j, l: (b_layer, *b_rc(i, j, l)))
    elif b_split is None:
        b_spec = pl.BlockSpec(b_tile, b_rc)
    else:
        per_q = (b.shape[2] // b_tile[1]) if b_split == 'n' else (b.shape[1] // b_tile[0])
        bmap = _blocked_map(b_split, per_q)
        b_spec = pl.BlockSpec((None,) + b_tile, lambda i, j, l: bmap(*b_rc(i, j, l)))
    if out_stack is not None:
        layer, n_layers, _ = out_stack
        out_shape2 = (n_layers, m, n)
        o_spec = pl.BlockSpec((None, ti, tj), lambda i, j, l: (layer, i, j))
    elif out_split is None:
        out_shape2 = (m, n)
        o_spec = pl.BlockSpec((ti, tj), lambda i, j, l: (i, j))
    else:
        out_shape2 = (N_CHIPS, m // N_CHIPS, n) if out_split == 'k' else (N_CHIPS, m, n // N_CHIPS)
        per_q = (out_shape2[2] // tj) if out_split == 'n' else (out_shape2[1] // ti)
        omap = _blocked_map(out_split, per_q)
        o_spec = pl.BlockSpec((None, ti, tj), lambda i, j, l: omap(i, j))

    in_specs = [a_spec, b_spec]
    operands = [a, b]
    if bias is not None:
        in_specs.append(pl.BlockSpec((1, tj), lambda i, j, l: (0, j)))
        operands.append(bias)
    if extra is not None:
        in_specs.append(pl.BlockSpec((ti, tj), lambda i, j, l: (i, j)))
        operands.append(extra)
    if epilogue == 'relu2':
        out_shape = (jax.ShapeDtypeStruct(out_shape2, out_dtype), jax.ShapeDtypeStruct(out_shape2, out_dtype))
        out_specs = (o_spec, o_spec)
    else:
        out_shape = jax.ShapeDtypeStruct(out_shape2, out_dtype)
        out_specs = o_spec
    has_bias, has_extra = bias is not None, extra is not None
    has_dest = out_stack is not None and out_stack[2] is not None
    if has_dest:
        in_specs.append(ANY)
        operands.append(out_stack[2])

    def body(*refs):
        a_ref, b_ref = refs[0], refs[1]
        pos = 2
        bias_ref = extra_ref = None
        if has_bias:
            bias_ref = refs[pos]
            pos += 1
        if has_extra:
            extra_ref = refs[pos]
            pos += 1
        if has_dest:
            pos += 1
        outs = refs[pos:] if n_l == 1 else refs[pos:-1]

        def finish(acc):
            if has_bias:
                acc = acc + bias_ref[...]
            if epilogue == 'relu2':
                r = jnp.maximum(acc, 0.0)
                outs[0][...] = r.astype(outs[0].dtype)
                outs[1][...] = (r * r).astype(outs[1].dtype)
            elif epilogue == 'times2x':
                outs[0][...] = (acc * (2.0 * extra_ref[...].astype(F32))).astype(outs[0].dtype)
            else:
                outs[0][...] = acc.astype(outs[0].dtype)

        def product():
            if nt_groups == 1:
                return lax.dot_general(a_ref[...].astype(BF16), b_ref[...].astype(BF16), dims, preferred_element_type=F32)
            width = b_ref.shape[2]
            return sum(lax.dot_general(a_ref[:, s * width:(s + 1) * width].astype(BF16), b_ref[s].astype(BF16), dims,
                                       preferred_element_type=F32) for s in range(nt_groups))

        if n_l == 1:
            finish(product())
            return
        acc_ref = refs[-1]
        step = pl.program_id(2)

        @pl.when(step == 0)
        def _():
            acc_ref[...] = jnp.zeros_like(acc_ref)

        acc_ref[...] += product()

        @pl.when(step == n_l - 1)
        def _():
            finish(acc_ref[...])

    result = pl.pallas_call(
        body, name=name, grid=grid, in_specs=in_specs, out_specs=out_specs, out_shape=out_shape,
        input_output_aliases={len(operands) - 1: 0} if has_dest else {},
        scratch_shapes=[] if n_l == 1 else [pltpu.VMEM((ti, tj), F32)],
        compiler_params=_params(("parallel", "parallel", "arbitrary")),
    )(*operands)
    return result.reshape(N_CHIPS, m // N_CHIPS, n) if blocked_rows_out else result


def _small(fn, out_shapes, *arrays, name):
    n_in = len(arrays)

    def body(*refs):
        res = fn(*[r[...] for r in refs[:n_in]])
        for o_ref, v in zip(refs[n_in:], res):
            o_ref[...] = v.astype(o_ref.dtype)

    return pl.pallas_call(
        body, name=name, out_shape=tuple(jax.ShapeDtypeStruct(s, d) for s, d in out_shapes),
        in_specs=[_full(a.shape) for a in arrays], out_specs=tuple(_full(s) for s, _ in out_shapes), grid=(1,),
        compiler_params=_params(("arbitrary",)),
    )(*arrays)


def _rows_tile(rows, cols, itemsize=4, budget=2 * 1024 * 1024):
    return _tile(rows, max(16, budget // (cols * itemsize)), 16)


def _rowwise(fn, out_dtypes, *arrays, name):
    rows, cols = arrays[0].shape
    tr = _rows_tile(rows, cols)
    n_in = len(arrays)

    def body(*refs):
        res = fn(*[r[...] for r in refs[:n_in]])
        for o_ref, v in zip(refs[n_in:], res):
            o_ref[...] = v.astype(o_ref.dtype)

    spec = pl.BlockSpec((tr, cols), lambda i: (i, 0))
    return pl.pallas_call(
        body, name=name, grid=(rows // tr,), in_specs=[spec] * n_in, out_specs=tuple(spec for _ in out_dtypes),
        out_shape=tuple(jax.ShapeDtypeStruct((rows, cols), d) for d in out_dtypes),
        compiler_params=_params(("parallel",)),
    )(*arrays)


def _as2d(a):
    return a.reshape(1, a.size) if a.ndim < 2 else a.reshape(-1, a.shape[-1])


def _cast_shard(w, place, layer, name, after=None):
    _, rows, cols = w.shape
    tr = _rows_tile(rows, cols)

    def body(place_ref, w_ref, *rest):
        rest[-1][...] = w_ref[...].astype(rest[-1].dtype)

    return pl.pallas_call(
        body, name=name, out_shape=jax.ShapeDtypeStruct((N_CHIPS, rows, cols), BF16),
        grid_spec=pltpu.PrefetchScalarGridSpec(
            num_scalar_prefetch=1, grid=(rows // tr,),
            in_specs=[pl.BlockSpec((None, tr, cols), lambda i, pr: (layer, i, 0))] + ([] if after is None else [ANY]),
            out_specs=pl.BlockSpec((None, tr, cols), lambda i, pr: (pr[0], i, 0))),
        compiler_params=_params(("parallel",)),
    )(place, w, *([] if after is None else [after]))


def _norm_fwd(x, g, a, b=None, res=None, *, out_dtype, name, into=None):
    rows, d = x.shape
    row0, total, dest = into if into is not None else (0, rows, None)
    tr = _rows_tile(math.gcd(rows, row0), d, budget=4 * 1024 * 1024)
    has_b, has_res = b is not None, res is not None

    def body(*refs):
        x_ref, g_ref, a_ref = refs[:3]
        pos = 3
        xv = x_ref[...]
        rstd = lax.rsqrt(jnp.mean(xv * xv, axis=-1, keepdims=True) + EPS)
        y = (xv * rstd * g_ref[...]) * a_ref[...]
        if has_b:
            y = y + refs[pos][...]
            pos += 1
        if has_res:
            y = y + refs[pos][...]
            pos += 1
        refs[-1][...] = y.astype(refs[-1].dtype)

    row = pl.BlockSpec((tr, d), lambda i: (i, 0))
    vec = pl.BlockSpec((1, d), lambda i: (0, 0))
    operands, specs = [x, g, a], [row, vec, vec]
    if has_b:
        operands.append(b)
        specs.append(vec)
    if has_res:
        operands.append(res)
        specs.append(row)
    if dest is not None:
        operands.append(dest)
        specs.append(ANY)
    return pl.pallas_call(
        body, name=name, grid=(rows // tr,), in_specs=specs, out_specs=pl.BlockSpec((tr, d), lambda i: (i + row0 // tr, 0)),
        out_shape=jax.ShapeDtypeStruct((total, d), out_dtype), compiler_params=_params(("parallel",)),
        input_output_aliases={} if dest is None else {len(operands) - 1: 0},
    )(*operands)


def _rstd(v):
    return lax.rsqrt(jnp.mean(v * v, axis=-1, keepdims=True) + EPS)


def _res_mod_fwd(o, x, g_res, gate, g_mod, a_mod, b_mod, *, name):
    rows, d = x.shape
    tr = _rows_tile(rows, d)

    def body(o_ref, x_ref, gr_ref, gate_ref, gm_ref, a_ref, b_ref, xm_ref, h_ref):
        ov = o_ref[...]
        xm = x_ref[...] + (ov * _rstd(ov) * gr_ref[...]) * gate_ref[...]
        xm_ref[...] = xm
        h_ref[...] = ((xm * _rstd(xm) * gm_ref[...]) * a_ref[...] + b_ref[...]).astype(h_ref.dtype)

    row = pl.BlockSpec((tr, d), lambda i: (i, 0))
    vec = pl.BlockSpec((1, d), lambda i: (0, 0))
    return pl.pallas_call(
        body, name=name, grid=(rows // tr,), in_specs=[row, row, vec, vec, vec, vec, vec], out_specs=(row, row),
        out_shape=(jax.ShapeDtypeStruct((rows, d), F32), jax.ShapeDtypeStruct((rows, d), BF16)),
        compiler_params=_params(("parallel",)),
    )(o, x, g_res, gate, g_mod, a_mod, b_mod)


def _mod_res_bwd(d_h, xm, g_mod, a_mod, extra, o, g_res, gate, *, name):
    rows, d = xm.shape
    tr = _rows_tile(rows, d)

    def body(dh_ref, xm_ref, gm_ref, a_ref, ex_ref, o_ref, gr_ref, gate_ref,
             dxm_ref, do_ref, dgm_ref, da_ref, db_ref, dgr_ref, dgate_ref):
        @pl.when(pl.program_id(0) == 0)
        def _():
            for ref in (dgm_ref, da_ref, db_ref, dgr_ref, dgate_ref):
                ref[...] = jnp.zeros_like(ref)

        def norm_adjoint(dy, xv, gain, scale, dgain_ref, dscale_ref):
            rstd = _rstd(xv)
            nrm = xv * rstd
            dscale_ref[...] += jnp.sum(dy * (nrm * gain), axis=0, keepdims=True)
            dt = dy * scale
            dgain_ref[...] += jnp.sum(dt * nrm, axis=0, keepdims=True)
            dn = dt * gain
            return rstd * (dn - nrm * jnp.mean(dn * nrm, axis=-1, keepdims=True))

        dhv = dh_ref[...].astype(F32)
        db_ref[...] += jnp.sum(dhv, axis=0, keepdims=True)
        dxm = norm_adjoint(dhv, xm_ref[...], gm_ref[...], a_ref[...], dgm_ref, da_ref) + ex_ref[...]
        dxm_ref[...] = dxm
        do_ref[...] = norm_adjoint(dxm, o_ref[...], gr_ref[...], gate_ref[...], dgr_ref, dgate_ref).astype(do_ref.dtype)

    row = pl.BlockSpec((tr, d), lambda i: (i, 0))
    vec = pl.BlockSpec((1, d), lambda i: (0, 0))
    vshape = jax.ShapeDtypeStruct((1, d), F32)
    return pl.pallas_call(
        body, name=name, grid=(rows // tr,), in_specs=[row, row, vec, vec, row, row, vec, vec],
        out_specs=(row, row, vec, vec, vec, vec, vec),
        out_shape=(jax.ShapeDtypeStruct((rows, d), F32), jax.ShapeDtypeStruct((rows, d), BF16)) + (vshape,) * 5,
        compiler_params=_params(("arbitrary",)),
    )(d_h, xm, g_mod, a_mod, extra, o, g_res, gate)


def _norm_bwd(dy, x, g, a, extra=None, *, out_dtype, name):
    rows, d = x.shape
    tr = _rows_tile(rows, d)
    has_extra = extra is not None

    def body(*refs):
        dy_ref, x_ref, g_ref, a_ref = refs[:4]
        pos = 4
        extra_ref = None
        if has_extra:
            extra_ref = refs[pos]
            pos += 1
        dx_ref, dg_ref, da_ref, db_ref = refs[pos:pos + 4]

        @pl.when(pl.program_id(0) == 0)
        def _():
            dg_ref[...] = jnp.zeros_like(dg_ref)
            da_ref[...] = jnp.zeros_like(da_ref)
            db_ref[...] = jnp.zeros_like(db_ref)

        xv = x_ref[...]
        dyv = dy_ref[...].astype(F32)
        rstd = lax.rsqrt(jnp.mean(xv * xv, axis=-1, keepdims=True) + EPS)
        nrm = xv * rstd
        gv = g_ref[...]
        da_ref[...] += jnp.sum(dyv * (nrm * gv), axis=0, keepdims=True)
        db_ref[...] += jnp.sum(dyv, axis=0, keepdims=True)
        dt = dyv * a_ref[...]
        dg_ref[...] += jnp.sum(dt * nrm, axis=0, keepdims=True)
        dn = dt * gv
        dx = rstd * (dn - nrm * jnp.mean(dn * nrm, axis=-1, keepdims=True))
        if has_extra:
            dx = dx + extra_ref[...]
        dx_ref[...] = dx.astype(dx_ref.dtype)

    row = pl.BlockSpec((tr, d), lambda i: (i, 0))
    vec = pl.BlockSpec((1, d), lambda i: (0, 0))
    operands, specs = [dy, x, g, a], [row, row, vec, vec]
    if has_extra:
        operands.append(extra)
        specs.append(row)
    vshape = jax.ShapeDtypeStruct((1, d), F32)
    return pl.pallas_call(
        body, name=name, grid=(rows // tr,), in_specs=specs, out_specs=(row, vec, vec, vec),
        out_shape=(jax.ShapeDtypeStruct((rows, d), out_dtype), vshape, vshape, vshape),
        compiler_params=_params(("arbitrary",)),
    )(*operands)


def _final_res_loss(f, x, g, gate, target):
    rows, d = x.shape
    tr = _rows_tile(rows, d)

    def body(f_ref, x_ref, g_ref, gate_ref, t_ref, dy_ref, df_ref, dg_ref, dgate_ref, loss_ref):
        @pl.when(pl.program_id(0) == 0)
        def _():
            for ref in (dg_ref, dgate_ref, loss_ref):
                ref[...] = jnp.zeros_like(ref)

        fv, gv, gatev = f_ref[...], g_ref[...], gate_ref[...]
        rstd = _rstd(fv)
        nrm = fv * rstd
        err = x_ref[...] + (nrm * gv) * gatev - t_ref[...]
        loss_ref[...] += jnp.sum(jnp.sum(err * err, axis=-1, keepdims=True), axis=0, keepdims=True) * (0.5 / d)
        dy = err * (1.0 / d)
        dy_ref[...] = dy
        dgate_ref[...] += jnp.sum(dy * (nrm * gv), axis=0, keepdims=True)
        dt = dy * gatev
        dg_ref[...] += jnp.sum(dt * nrm, axis=0, keepdims=True)
        dn = dt * gv
        df_ref[...] = (rstd * (dn - nrm * jnp.mean(dn * nrm, axis=-1, keepdims=True))).astype(df_ref.dtype)

    row = pl.BlockSpec((tr, d), lambda i: (i, 0))
    vec = pl.BlockSpec((1, d), lambda i: (0, 0))
    vshape = jax.ShapeDtypeStruct((1, d), F32)
    return pl.pallas_call(
        body, name="final_res_loss", grid=(rows // tr,), in_specs=[row, row, vec, vec, row],
        out_specs=(row, row, vec, vec, _full((1, 1))),
        out_shape=(jax.ShapeDtypeStruct((rows, d), F32), jax.ShapeDtypeStruct((rows, d), BF16), vshape, vshape,
                   jax.ShapeDtypeStruct((1, 1), F32)),
        compiler_params=_params(("arbitrary",)),
    )(f, x, g, gate, target)


def _rope_partner(v):
    lane = lax.broadcasted_iota(jnp.int32, v.shape, 1)
    up = pltpu.roll(v, HEAD_DIM - ROPE_PAIRS, 1)
    down = pltpu.roll(v, ROPE_PAIRS, 1)
    return jnp.where((lane % (2 * ROPE_PAIRS)) < ROPE_PAIRS, up, down)


def _qk_fwd(proj, q_g, k_g, cos, sin, *, name, kv_into=None):
    rows = proj.shape[0]
    row0, total, kv_dest = kv_into if kv_into is not None else (0, rows, None)
    tr = _tile(math.gcd(rows, row0), 256, 16)
    width = ATTN_W + 2 * KV_W

    def body(p_ref, qg_ref, kg_ref, cos_ref, sin_ref, *rest):
        q_ref, k_ref, v_ref = rest[-3:]
        cosv, sinv = cos_ref[...], sin_ref[...]
        for h in range(N_HEADS + N_KV):
            xv = p_ref[:, h * HEAD_DIM:(h + 1) * HEAD_DIM]
            gain = qg_ref[...] if h < N_HEADS else kg_ref[...]
            t = xv * lax.rsqrt(jnp.mean(xv * xv, axis=-1, keepdims=True) + EPS) * gain
            y = t * cosv + _rope_partner(t) * sinv
            if h < N_HEADS:
                q_ref[:, h * HEAD_DIM:(h + 1) * HEAD_DIM] = y.astype(BF16)
            else:
                k_ref[:, (h - N_HEADS) * HEAD_DIM:(h - N_HEADS + 1) * HEAD_DIM] = y.astype(BF16)
        v_ref[...] = p_ref[:, ATTN_W + KV_W:width].astype(BF16)

    vec = _full((1, HEAD_DIM))
    tab = pl.BlockSpec((tr, HEAD_DIM), lambda i: (i, 0))
    kv_spec = pl.BlockSpec((tr, KV_W), lambda i: (i + row0 // tr, 0))
    kv_shape = jax.ShapeDtypeStruct((total, KV_W), BF16)
    return pl.pallas_call(
        body, name=name, grid=(rows // tr,),
        in_specs=[pl.BlockSpec((tr, width), lambda i: (i, 0)), vec, vec, tab, tab] + ([] if kv_dest is None else [ANY, ANY]),
        out_specs=(pl.BlockSpec((tr, ATTN_W), lambda i: (i, 0)), kv_spec, kv_spec),
        out_shape=(jax.ShapeDtypeStruct((rows, ATTN_W), BF16), kv_shape, kv_shape),
        input_output_aliases={} if kv_dest is None else {5: 1, 6: 2},
        compiler_params=_params(("parallel",)),
    )(proj, q_g, k_g, cos, sin, *([] if kv_dest is None else kv_dest))


def _qk_bwd(dq, dk, proj, q_g, k_g, cos, sin, *, name, dk_row0=0):
    rows = proj.shape[0]
    tr = _tile(math.gcd(rows, dk_row0), 256, 16)
    width = ATTN_W + KV_W
    has_q = dq is not None

    def body(*refs):
        pos = 0
        dq_ref = None
        if has_q:
            dq_ref = refs[0]
            pos = 1
        dk_ref, p_ref, qg_ref, kg_ref, cos_ref, sin_ref, dp_ref, dqg_ref, dkg_ref = refs[pos:pos + 9]

        @pl.when(pl.program_id(0) == 0)
        def _():
            dqg_ref[...] = jnp.zeros_like(dqg_ref)
            dkg_ref[...] = jnp.zeros_like(dkg_ref)

        cosv, sinv = cos_ref[...], sin_ref[...]
        for h in range(N_HEADS + N_KV):
            cols = slice(h * HEAD_DIM, (h + 1) * HEAD_DIM)
            if h < N_HEADS and not has_q:
                dp_ref[:, cols] = jnp.zeros((tr, HEAD_DIM), dp_ref.dtype)
                continue
            if h < N_HEADS:
                dyv, gain, dgain_ref = dq_ref[:, cols], qg_ref[...], dqg_ref
            else:
                hk = h - N_HEADS
                dyv, gain, dgain_ref = dk_ref[:, hk * HEAD_DIM:(hk + 1) * HEAD_DIM], kg_ref[...], dkg_ref
            dyv = dyv.astype(F32)
            dt = dyv * cosv + _rope_partner(dyv * sinv)
            xv = p_ref[:, cols]
            rstd = lax.rsqrt(jnp.mean(xv * xv, axis=-1, keepdims=True) + EPS)
            nrm = xv * rstd
            dgain_ref[...] += jnp.sum(dt * nrm, axis=0, keepdims=True)
            dn = dt * gain
            dp_ref[:, cols] = (rstd * (dn - nrm * jnp.mean(dn * nrm, axis=-1, keepdims=True))).astype(dp_ref.dtype)

    vec = _full((1, HEAD_DIM))
    tab = pl.BlockSpec((tr, HEAD_DIM), lambda i: (i, 0))
    operands = ([dq] if has_q else []) + [dk, proj, q_g, k_g, cos, sin]
    specs = ([pl.BlockSpec((tr, ATTN_W), lambda i: (i, 0))] if has_q else []) + [
        pl.BlockSpec((tr, KV_W), lambda i: (i + dk_row0 // tr, 0)), pl.BlockSpec((tr, width), lambda i: (i, 0)), vec, vec, tab, tab]
    return pl.pallas_call(
        body, name=name, grid=(rows // tr,), in_specs=specs,
        out_specs=(pl.BlockSpec((tr, width), lambda i: (i, 0)), vec, vec),
        out_shape=(jax.ShapeDtypeStruct((rows, width), BF16), jax.ShapeDtypeStruct((1, HEAD_DIM), F32),
                   jax.ShapeDtypeStruct((1, HEAD_DIM), F32)),
        compiler_params=_params(("arbitrary",)),
    )(*operands)


def _attn_fwd(q, k, v):
    n_q, n_k = q.shape[0], k.shape[0]
    tq = _tile(n_q, 512, 16)
    gw = GROUP * HEAD_DIM
    scale = HEAD_DIM ** -0.5

    def body(q_ref, k_ref, v_ref, o_ref, lse_ref):
        kv, vv = k_ref[...], v_ref[...]
        for g in range(GROUP):
            cols = slice(g * HEAD_DIM, (g + 1) * HEAD_DIM)
            s = lax.dot_general(q_ref[:, cols], kv, (((1,), (1,)), ((), ())), preferred_element_type=F32) * (scale * LOG2E)
            m = jnp.max(s, axis=-1, keepdims=True)
            p = jnp.exp2(s - m)
            l = jnp.sum(p, axis=-1, keepdims=True)
            o = jnp.dot(p.astype(BF16), vv, preferred_element_type=F32) / l
            o_ref[:, cols] = o.astype(o_ref.dtype)
            lse_ref[:, g:g + 1] = m + jnp.log(l) * LOG2E

    return pl.pallas_call(
        body, name="attn_fwd", grid=(N_KV, n_q // tq),
        in_specs=[pl.BlockSpec((tq, gw), lambda h, i: (i, h)), pl.BlockSpec((n_k, HEAD_DIM), lambda h, i: (0, h)),
                  pl.BlockSpec((n_k, HEAD_DIM), lambda h, i: (0, h))],
        out_specs=(pl.BlockSpec((tq, gw), lambda h, i: (i, h)), pl.BlockSpec((None, tq, GROUP), lambda h, i: (h, i, 0))),
        out_shape=(jax.ShapeDtypeStruct((n_q, ATTN_W + D_RNN), BF16), jax.ShapeDtypeStruct((N_KV, n_q, GROUP), F32)),
        compiler_params=_params(("parallel", "parallel")),
    )(q, k, v)


def _attn_bwd(q, k, v, o, lse, do):
    n_q, n_k = q.shape[0], k.shape[0]
    tq = _tile(n_q, 512, 16)
    gw = GROUP * HEAD_DIM
    scale = HEAD_DIM ** -0.5

    def body(q_ref, k_ref, v_ref, o_ref, lse_ref, do_ref, dq_ref, dk_ref, dv_ref):
        @pl.when(pl.program_id(1) == 0)
        def _():
            dk_ref[...] = jnp.zeros_like(dk_ref)
            dv_ref[...] = jnp.zeros_like(dv_ref)

        kv, vv = k_ref[...], v_ref[...]
        for g in range(GROUP):
            cols = slice(g * HEAD_DIM, (g + 1) * HEAD_DIM)
            qg = q_ref[:, cols]
            dof = do_ref[:, cols].astype(F32)
            dog = dof.astype(BF16)
            s = lax.dot_general(qg, kv, (((1,), (1,)), ((), ())), preferred_element_type=F32) * (scale * LOG2E)
            p = jnp.exp2(s - lse_ref[:, g:g + 1])
            delta = jnp.sum(dof * o_ref[:, cols].astype(F32), axis=-1, keepdims=True)
            dp = lax.dot_general(dog, vv, (((1,), (1,)), ((), ())), preferred_element_type=F32)
            ds = (p * (dp - delta) * scale).astype(BF16)
            pb = p.astype(BF16)
            dq_ref[:, cols] = jnp.dot(ds, kv, preferred_element_type=F32)
            dk_ref[...] += lax.dot_general(ds, qg, (((0,), (0,)), ((), ())), preferred_element_type=F32)
            dv_ref[...] += lax.dot_general(pb, dog, (((0,), (0,)), ((), ())), preferred_element_type=F32)

    qspec = pl.BlockSpec((tq, gw), lambda h, i: (i, h))
    kspec = pl.BlockSpec((n_k, HEAD_DIM), lambda h, i: (0, h))
    return pl.pallas_call(
        body, name="attn_bwd", grid=(N_KV, n_q // tq),
        in_specs=[qspec, kspec, kspec, qspec, pl.BlockSpec((None, tq, GROUP), lambda h, i: (h, i, 0)), qspec],
        out_specs=(qspec, kspec, kspec),
        out_shape=(jax.ShapeDtypeStruct((n_q, ATTN_W), F32), jax.ShapeDtypeStruct((n_k, KV_W), F32),
                   jax.ShapeDtypeStruct((n_k, KV_W), F32)),
        compiler_params=_params(("parallel", "arbitrary")),
    )(q, k, v, o, lse, do)


CONV_COLS = 256
XR_COL0 = ATTN_W + 2 * KV_W


def _shift_rows(v, off):
    if off == 0:
        return v
    n = v.shape[0]
    rolled = pltpu.roll(v, (-off) % n, 0)
    t = lax.broadcasted_iota(jnp.int32, v.shape, 0)
    keep = (t + off >= 0) & (t + off < n)
    return jnp.where(keep, rolled, 0.0)


def _conv_fwd(proj_l, proj_c, w, b):
    n_lat, n_ctx = proj_l.shape[0], proj_c.shape[0]
    blk0 = XR_COL0 // CONV_COLS

    def body(xl_ref, xc_ref, w_ref, b_ref, y_ref):
        for x_ref, rows in ((xc_ref, slice(0, n_ctx)), (xl_ref, slice(n_ctx, n_ctx + n_lat))):
            xv = x_ref[...]
            y = b_ref[...] + jnp.zeros_like(xv)
            for j in range(CONV_W):
                y = y + _shift_rows(xv, j - CONV_W // 2) * w_ref[j:j + 1, :]
            y_ref[rows, :] = y

    return pl.pallas_call(
        body, name="conv_fwd", grid=(D_RNN // CONV_COLS,),
        in_specs=[pl.BlockSpec((n_lat, CONV_COLS), lambda i: (0, blk0 + i)), pl.BlockSpec((n_ctx, CONV_COLS), lambda i: (0, blk0 + i)),
                  pl.BlockSpec((CONV_W, CONV_COLS), lambda i: (0, i)), pl.BlockSpec((1, CONV_COLS), lambda i: (0, i))],
        out_specs=pl.BlockSpec((n_ctx + n_lat, CONV_COLS), lambda i: (0, i)),
        out_shape=jax.ShapeDtypeStruct((n_ctx + n_lat, D_RNN), F32), compiler_params=_params(("parallel",)),
    )(proj_l, proj_c, w, b)


def _conv_bwd(d1, d2, proj_l, proj_c, w):
    n_lat, n_ctx = proj_l.shape[0], proj_c.shape[0]
    blk0 = XR_COL0 // CONV_COLS

    def body(d1_ref, d2_ref, xl_ref, xc_ref, w_ref, dxl_ref, dxc_ref, dw_ref, db_ref):
        dw = [0.0] * CONV_W
        db = 0.0
        for x_ref, dx_ref, rows in ((xc_ref, dxc_ref, slice(0, n_ctx)), (xl_ref, dxl_ref, slice(n_ctx, n_ctx + n_lat))):
            dv = d1_ref[rows, :] + d2_ref[rows, :]
            xv = x_ref[...]
            dx = jnp.zeros_like(dv)
            for j in range(CONV_W):
                off = j - CONV_W // 2
                dx = dx + _shift_rows(dv, -off) * w_ref[j:j + 1, :]
                dw[j] = dw[j] + jnp.sum(dv * _shift_rows(xv, off), axis=0, keepdims=True)
            dx_ref[...] = dx.astype(dx_ref.dtype)
            db = db + jnp.sum(dv, axis=0, keepdims=True)
        for j in range(CONV_W):
            dw_ref[j:j + 1, :] = dw[j]
        db_ref[...] = db

    both = pl.BlockSpec((n_ctx + n_lat, CONV_COLS), lambda i: (0, i))
    return pl.pallas_call(
        body, name="conv_bwd", grid=(D_RNN // CONV_COLS,),
        in_specs=[both, both, pl.BlockSpec((n_lat, CONV_COLS), lambda i: (0, blk0 + i)),
                  pl.BlockSpec((n_ctx, CONV_COLS), lambda i: (0, blk0 + i)), pl.BlockSpec((CONV_W, CONV_COLS), lambda i: (0, i))],
        out_specs=(pl.BlockSpec((n_lat, CONV_COLS), lambda i: (0, i)), pl.BlockSpec((n_ctx, CONV_COLS), lambda i: (0, i)),
                   pl.BlockSpec((CONV_W, CONV_COLS), lambda i: (0, i)), pl.BlockSpec((1, CONV_COLS), lambda i: (0, i))),
        out_shape=(jax.ShapeDtypeStruct((n_lat, D_RNN), BF16), jax.ShapeDtypeStruct((n_ctx, D_RNN), BF16),
                   jax.ShapeDtypeStruct((CONV_W, D_RNN), F32), jax.ShapeDtypeStruct((1, D_RNN), F32)),
        compiler_params=_params(("parallel",)),
    )(d1, d2, proj_l, proj_c, w)


RNN_TB = 256
SCAN_ROWS = 8


def _sigmoid(z):
    return 1.0 / (1.0 + jnp.exp(-z))


def _softplus(z):
    return jnp.maximum(z, 0.0) + jnp.log(1.0 + jnp.exp(-jnp.abs(z)))


def _one_minus_exp(y, exp_y):
    series = -y * (1.0 + y * (0.5 + y * (1.0 / 6.0 + y * (1.0 / 24.0))))
    return jnp.where(y > -0.03, series, 1.0 - exp_y)


def _rglru_gates(xv, wa_ref, ba_ref, wx_ref, bx_ref, lam_ref):
    xb = xv.astype(BF16)
    zr = jnp.concatenate([jnp.dot(xb[:, n * RNN_BW:(n + 1) * RNN_BW], wa_ref[n].astype(BF16),
                                  preferred_element_type=F32) for n in range(RNN_BLOCKS)], axis=-1) + ba_ref[...]
    zi = jnp.concatenate([jnp.dot(xb[:, n * RNN_BW:(n + 1) * RNN_BW], wx_ref[n].astype(BF16),
                                  preferred_element_type=F32) for n in range(RNN_BLOCKS)], axis=-1) + bx_ref[...]
    r = _sigmoid(zr)
    gi = _sigmoid(zi)
    sp = _softplus(-lam_ref[...])
    log_a = -RG_C * r * sp
    a = jnp.exp(log_a)
    s = jnp.sqrt(_one_minus_exp(2.0 * log_a, a * a))
    return r, gi, sp, a, s


def _scan_rows(n_rows, reverse, step_fn, carry):
    groups = n_rows // SCAN_ROWS

    def trip(gidx, carry):
        gi = (groups - 1 - gidx) if reverse else gidx
        base = pl.multiple_of(gi * SCAN_ROWS, SCAN_ROWS)
        return step_fn(base, carry)

    return lax.fori_loop(0, groups, trip, carry)


def _scan_block_order(nb, nb_c, reverse, adjoint):
    if not reverse:
        return (lambda i: nb - 1 - i) if adjoint else (lambda i: i)
    if adjoint:
        return lambda i: jnp.where(i < nb - nb_c, nb_c + i, i - (nb - nb_c))
    return lambda i: jnp.where(i < nb_c, nb_c - 1 - i, nb + nb_c - 1 - i)


def _rglru_fwd(xs, wa, ba, wx, bx, lam, *, reverse, n_ctx, name):
    rows = xs.shape[0]
    tb = _tile(math.gcd(rows, n_ctx), RNN_TB, SCAN_ROWS)
    nb = rows // tb
    block_of = _scan_block_order(nb, n_ctx // tb, reverse, False)
    order = lambda i: (block_of(i), 0)

    def body(x_ref, wa_ref, ba_ref, wx_ref, bx_ref, lam_ref, h_ref, hp_ref, a_s, b_s, state):
        @pl.when(pl.program_id(0) == 0)
        def _():
            state[...] = jnp.zeros_like(state)

        xv = x_ref[...]
        _, gi, _, a, s = _rglru_gates(xv, wa_ref, ba_ref, wx_ref, bx_ref, lam_ref)
        a_s[...] = a
        b_s[...] = s * (gi * xv)

        def group(base, h):
            av = a_s[pl.ds(base, SCAN_ROWS), :]
            bv = b_s[pl.ds(base, SCAN_ROWS), :]
            outs, prevs = [None] * SCAN_ROWS, [None] * SCAN_ROWS
            for k in range(SCAN_ROWS):
                r_ = SCAN_ROWS - 1 - k if reverse else k
                prevs[r_] = h
                h = av[r_:r_ + 1, :] * h + bv[r_:r_ + 1, :]
                outs[r_] = h
            h_ref[pl.ds(base, SCAN_ROWS), :] = jnp.concatenate(outs, axis=0)
            hp_ref[pl.ds(base, SCAN_ROWS), :] = jnp.concatenate(prevs, axis=0)
            return h

        state[0:1, :] = _scan_rows(tb, reverse, group, state[0:1, :])

    blk = pl.BlockSpec((tb, D_RNN), order)
    wspec = _full((RNN_BLOCKS, RNN_BW, RNN_BW))
    vec = _full((1, D_RNN))
    return pl.pallas_call(
        body, name=name, grid=(nb,), in_specs=[blk, wspec, vec, wspec, vec, vec], out_specs=(blk, blk),
        out_shape=(jax.ShapeDtypeStruct((rows, D_RNN), F32), jax.ShapeDtypeStruct((rows, D_RNN), F32)),
        scratch_shapes=[pltpu.VMEM((tb, D_RNN), F32), pltpu.VMEM((tb, D_RNN), F32), pltpu.VMEM((SCAN_ROWS, D_RNN), F32)],
        compiler_params=_params(("arbitrary",)),
    )(xs, wa, ba, wx, bx, lam)


def _rglru_bwd(xs, h_prev, dh, wa, ba, wx, bx, lam, *, reverse, n_ctx, name):
    rows = xs.shape[0]
    tb = _tile(math.gcd(rows, n_ctx), RNN_TB, SCAN_ROWS)
    nb, nb_c = rows // tb, n_ctx // tb
    back = not reverse
    block_of = _scan_block_order(nb, nb_c, reverse, True)
    order = lambda i: (block_of(i), 0)

    def body(x_ref, hp_ref, dh_ref, wa_ref, ba_ref, wx_ref, bx_ref, lam_ref,
             dx_ref, dwa_ref, dba_ref, dwx_ref, dbx_ref, dlam_ref, a_s, g_s, state):
        @pl.when(pl.program_id(0) == 0)
        def _():
            state[...] = jnp.zeros_like(state)
            dwa_ref[...] = jnp.zeros_like(dwa_ref)
            dwx_ref[...] = jnp.zeros_like(dwx_ref)
            dba_ref[...] = jnp.zeros_like(dba_ref)
            dbx_ref[...] = jnp.zeros_like(dbx_ref)
            dlam_ref[...] = jnp.zeros_like(dlam_ref)

        xv = x_ref[...]
        r, gi, sp, a, s = _rglru_gates(xv, wa_ref, ba_ref, wx_ref, bx_ref, lam_ref)
        a_s[...] = a

        is_latent = block_of(pl.program_id(0)) >= nb_c

        def group(base, carry):
            av = a_s[pl.ds(base, SCAN_ROWS), :]
            dv = jnp.where(is_latent, dh_ref[pl.ds(base, SCAN_ROWS), :], 0.0)
            outs = [None] * SCAN_ROWS
            for k in range(SCAN_ROWS):
                r_ = SCAN_ROWS - 1 - k if back else k
                gt = dv[r_:r_ + 1, :] + carry
                outs[r_] = gt
                carry = av[r_:r_ + 1, :] * gt
            g_s[pl.ds(base, SCAN_ROWS), :] = jnp.concatenate(outs, axis=0)
            return carry

        state[0:1, :] = _scan_rows(tb, back, group, state[0:1, :])

        gv = g_s[...]
        d_a = gv * hp_ref[...]
        d_s = gv * (gi * xv)
        d_gi = gv * (s * xv)
        dx = gv * (s * gi)
        d_log_a = d_a * a - d_s * (a * a) / s
        d_r = d_log_a * (-RG_C * sp)
        lamv = lam_ref[...]
        d_sp = jnp.sum(d_log_a * (-RG_C * r), axis=0, keepdims=True)
        dlam_ref[...] += d_sp * (-_sigmoid(-lamv))
        d_zr = d_r * r * (1.0 - r)
        d_zi = d_gi * gi * (1.0 - gi)
        dba_ref[...] += jnp.sum(d_zr, axis=0, keepdims=True)
        dbx_ref[...] += jnp.sum(d_zi, axis=0, keepdims=True)
        xb = xv.astype(BF16)
        zrb, zib = d_zr.astype(BF16), d_zi.astype(BF16)
        parts = []
        for n in range(RNN_BLOCKS):
            cols = slice(n * RNN_BW, (n + 1) * RNN_BW)
            dwa_ref[n] += lax.dot_general(xb[:, cols], zrb[:, cols], (((0,), (0,)), ((), ())), preferred_element_type=F32)
            dwx_ref[n] += lax.dot_general(xb[:, cols], zib[:, cols], (((0,), (0,)), ((), ())), preferred_element_type=F32)
            parts.append(
                lax.dot_general(zrb[:, cols], wa_ref[n].astype(BF16), (((1,), (1,)), ((), ())), preferred_element_type=F32)
                + lax.dot_general(zib[:, cols], wx_ref[n].astype(BF16), (((1,), (1,)), ((), ())), preferred_element_type=F32))
        dx_ref[...] = dx + jnp.concatenate(parts, axis=-1)

    blk = pl.BlockSpec((tb, D_RNN), order)
    wspec = _full((RNN_BLOCKS, RNN_BW, RNN_BW))
    vec = _full((1, D_RNN))
    wshape = jax.ShapeDtypeStruct((RNN_BLOCKS, RNN_BW, RNN_BW), F32)
    vshape = jax.ShapeDtypeStruct((1, D_RNN), F32)
    dh_blk = pl.BlockSpec((tb, D_RNN), lambda i: (jnp.maximum(block_of(i) - nb_c, 0), 0))
    return pl.pallas_call(
        body, name=name, grid=(nb,), in_specs=[blk, blk, dh_blk, wspec, vec, wspec, vec, vec],
        out_specs=(blk, wspec, vec, wspec, vec, vec),
        out_shape=(jax.ShapeDtypeStruct((rows, D_RNN), F32), wshape, vshape, wshape, vshape, vshape),
        scratch_shapes=[pltpu.VMEM((tb, D_RNN), F32), pltpu.VMEM((tb, D_RNN), F32), pltpu.VMEM((SCAN_ROWS, D_RNN), F32)],
        compiler_params=_params(("arbitrary",)),
    )(xs, h_prev, dh, wa, ba, wx, bx, lam)


def _assemble_d_proj(dp_qk_l, dp_qk_c, dv_all, d_xr_l, d_xr_c, d_gate):
    n_lat, n_ctx = dp_qk_l.shape[0], dp_qk_c.shape[0]
    tr = _tile(math.gcd(n_lat, n_ctx), 256, 16)
    nb_l, nb_c = n_lat // tr, n_ctx // tr
    w_qk = ATTN_W + KV_W

    def body(ql_ref, qc_ref, dv_ref, xl_ref, xc_ref, g_ref, o_ref):
        i = pl.program_id(0)
        o_ref[:, w_qk:XR_COL0] = dv_ref[...].astype(o_ref.dtype)

        @pl.when(i < nb_l)
        def _():
            o_ref[:, :w_qk] = ql_ref[...]
            o_ref[:, XR_COL0:GATE_COL0] = xl_ref[...]
            o_ref[:, GATE_COL0:] = g_ref[...]

        @pl.when(i >= nb_l)
        def _():
            o_ref[:, :w_qk] = qc_ref[...]
            o_ref[:, XR_COL0:GATE_COL0] = xc_ref[...]
            o_ref[:, GATE_COL0:] = jnp.zeros((tr, D_RNN), o_ref.dtype)

    lat = lambda i: (jnp.minimum(i, nb_l - 1), 0)
    ctx = lambda i: (jnp.maximum(i - nb_l, 0), 0)
    return pl.pallas_call(
        body, name="assemble_d_proj", grid=(nb_l + nb_c,),
        in_specs=[pl.BlockSpec((tr, w_qk), lat), pl.BlockSpec((tr, w_qk), ctx),
                  pl.BlockSpec((tr, KV_W), lambda i: (jnp.where(i < nb_l, i + nb_c, i - nb_l), 0)),
                  pl.BlockSpec((tr, D_RNN), lat), pl.BlockSpec((tr, D_RNN), ctx), pl.BlockSpec((tr, D_RNN), lat)],
        out_specs=pl.BlockSpec((tr, GATE_COL0 + D_RNN), lambda i: (i, 0)),
        out_shape=jax.ShapeDtypeStruct((n_lat + n_ctx, GATE_COL0 + D_RNN), BF16),
        compiler_params=_params(("parallel",)),
    )(dp_qk_l, dp_qk_c, dv_all, d_xr_l, d_xr_c, d_gate)


def _gelu(z):
    return 0.5 * z * (1.0 + jnp.tanh(GELU_C * (z + 0.044715 * z * z * z)))


def _gelu_grad(z):
    t = jnp.tanh(GELU_C * (z + 0.044715 * z * z * z))
    return 0.5 * (1.0 + t) + 0.5 * z * (1.0 - t * t) * (GELU_C * (1.0 + 3.0 * 0.044715 * z * z))


GATE_COL0 = XR_COL0 + D_RNN


RNN_OUT_COLS = 512


def _rnn_out_specs(rows, hf_off, hb_off):
    tr = _tile(rows, 256, 16)
    assert hf_off % tr == 0 and hb_off % tr == 0 and GATE_COL0 % RNN_OUT_COLS == 0
    fo, bo, go = hf_off // tr, hb_off // tr, GATE_COL0 // RNN_OUT_COLS
    hf_spec = pl.BlockSpec((tr, RNN_OUT_COLS), lambda i, j: (i + fo, j))
    hb_spec = pl.BlockSpec((tr, RNN_OUT_COLS), lambda i, j: (i + bo, j))
    gate_spec = pl.BlockSpec((tr, RNN_OUT_COLS), lambda i, j: (i, j + go))
    out_spec = pl.BlockSpec((tr, RNN_OUT_COLS), lambda i, j: (i, j))
    return (rows // tr, D_RNN // RNN_OUT_COLS), hf_spec, hb_spec, gate_spec, out_spec


def _rnn_out_fwd(hf, hb, proj, hf_off, hb_off, cat):
    rows = proj.shape[0]
    grid, hf_spec, hb_spec, gate_spec, out_spec = _rnn_out_specs(rows, hf_off, hb_off)
    tr, col0 = out_spec.block_shape[0], ATTN_W // RNN_OUT_COLS

    def body(hf_ref, hb_ref, g_ref, _, o_ref):
        o_ref[...] = ((hf_ref[...] + hb_ref[...]) * _gelu(g_ref[...])).astype(o_ref.dtype)

    return pl.pallas_call(
        body, name="rnn_out_fwd", grid=grid, in_specs=[hf_spec, hb_spec, gate_spec, ANY],
        out_specs=pl.BlockSpec((tr, RNN_OUT_COLS), lambda i, j: (i, j + col0)),
        out_shape=jax.ShapeDtypeStruct(cat.shape, cat.dtype), input_output_aliases={3: 0},
        compiler_params=_params(("parallel", "parallel")),
    )(hf, hb, proj, cat)


def _rnn_out_bwd(d_cat, hf, hb, proj, hf_off, hb_off):
    rows = proj.shape[0]
    grid, hf_spec, hb_spec, gate_spec, out_spec = _rnn_out_specs(rows, hf_off, hb_off)
    do = ATTN_W // RNN_OUT_COLS

    def body(d_ref, hf_ref, hb_ref, g_ref, dh_ref, dg_ref):
        dv, gv = d_ref[...].astype(F32), g_ref[...]
        dh_ref[...] = dv * _gelu(gv)
        dg_ref[...] = (dv * (hf_ref[...] + hb_ref[...]) * _gelu_grad(gv)).astype(dg_ref.dtype)

    tr = out_spec.block_shape[0]
    return pl.pallas_call(
        body, name="rnn_out_bwd", grid=grid,
        in_specs=[pl.BlockSpec((tr, RNN_OUT_COLS), lambda i, j: (i, j + do)), hf_spec, hb_spec, gate_spec],
        out_specs=(out_spec, out_spec),
        out_shape=(jax.ShapeDtypeStruct((rows, D_RNN), F32), jax.ShapeDtypeStruct((rows, D_RNN), BF16)),
        compiler_params=_params(("parallel", "parallel")),
    )(d_cat, hf, hb, proj)


def _gmlp_parts(z_ref, vg_ref, vb_ref, d_gm):
    zu, zv = z_ref[:, :d_gm], z_ref[:, d_gm:]
    u = _gelu(zu)
    v = _gelu(zv)
    mu = jnp.mean(v, axis=-1, keepdims=True)
    vc = v - mu
    rstd = lax.rsqrt(jnp.mean(vc * vc, axis=-1, keepdims=True) + EPS)
    vhat = vc * rstd
    vn = vhat * vg_ref[...] + vb_ref[...]
    return zu, zv, u, vhat, rstd, vn


def _gmlp_fwd(z, v_g, v_b, w_sp, b_sp_t):
    rows, d_gm = z.shape[0], z.shape[1] // 2
    tr = _tile(rows, 256, CHUNK)
    gwid = d_gm // GM_GROUPS

    def body(z_ref, vg_ref, vb_ref, w_ref, b_ref, o_ref):
        _, _, u, _, _, vn = _gmlp_parts(z_ref, vg_ref, vb_ref, d_gm)
        vnb = vn.astype(BF16)
        for g in range(GM_GROUPS):
            wg = w_ref[g].astype(BF16)
            for c in range(tr // CHUNK):
                rs, cs = slice(c * CHUNK, (c + 1) * CHUNK), slice(g * gwid, (g + 1) * gwid)
                sv = jnp.dot(wg, vnb[rs, cs], preferred_element_type=F32) + b_ref[:, g:g + 1]
                o_ref[rs, cs] = (u[rs, cs] * sv).astype(o_ref.dtype)

    return pl.pallas_call(
        body, name="gmlp_fwd", grid=(rows // tr,),
        in_specs=[pl.BlockSpec((tr, 2 * d_gm), lambda i: (i, 0)), _full((1, d_gm)), _full((1, d_gm)),
                  _full(w_sp.shape), _full(b_sp_t.shape)],
        out_specs=pl.BlockSpec((tr, d_gm), lambda i: (i, 0)),
        out_shape=jax.ShapeDtypeStruct((rows, d_gm), BF16), compiler_params=_params(("parallel",)),
    )(z, v_g, v_b, w_sp, b_sp_t)


def _gmlp_bwd(z, dgate, v_g, v_b, w_sp, b_sp_t):
    rows, d_gm = z.shape[0], z.shape[1] // 2
    tr = _tile(rows, 256, CHUNK)
    gwid = d_gm // GM_GROUPS

    def body(z_ref, dg_ref, vg_ref, vb_ref, w_ref, b_ref, dz_ref, dbin_ref, dvg_ref, dvb_ref, dw_ref, dbs_ref, dvn_s):
        @pl.when(pl.program_id(0) == 0)
        def _():
            dbin_ref[...] = jnp.zeros_like(dbin_ref)
            dvg_ref[...] = jnp.zeros_like(dvg_ref)
            dvb_ref[...] = jnp.zeros_like(dvb_ref)
            dw_ref[...] = jnp.zeros_like(dw_ref)
            dbs_ref[...] = jnp.zeros_like(dbs_ref)

        zu, zv, u, vhat, rstd, vn = _gmlp_parts(z_ref, vg_ref, vb_ref, d_gm)
        vnb = vn.astype(BF16)
        dgv = dg_ref[...].astype(F32)
        dsv = dgv * u
        dsvb = dsv.astype(BF16)
        for g in range(GM_GROUPS):
            wg = w_ref[g].astype(BF16)
            cs = slice(g * gwid, (g + 1) * gwid)
            for c in range(tr // CHUNK):
                rs = slice(c * CHUNK, (c + 1) * CHUNK)
                sv = jnp.dot(wg, vnb[rs, cs], preferred_element_type=F32) + b_ref[:, g:g + 1]
                dz_ref[rs, cs] = (dgv[rs, cs] * sv * _gelu_grad(zu[rs, cs])).astype(dz_ref.dtype)
                dw_ref[g] += lax.dot_general(dsvb[rs, cs], vnb[rs, cs], (((1,), (1,)), ((), ())),
                                             preferred_element_type=F32)
                dbs_ref[:, g:g + 1] += jnp.sum(dsv[rs, cs], axis=-1, keepdims=True)
                dvn_s[rs, cs] = lax.dot_general(wg, dsvb[rs, cs], (((0,), (0,)), ((), ())), preferred_element_type=F32)
        dvn = dvn_s[...]
        dvg_ref[...] += jnp.sum(dvn * vhat, axis=0, keepdims=True)
        dvb_ref[...] += jnp.sum(dvn, axis=0, keepdims=True)
        dvh = dvn * vg_ref[...]
        dv = rstd * (dvh - jnp.mean(dvh, axis=-1, keepdims=True) - vhat * jnp.mean(dvh * vhat, axis=-1, keepdims=True))
        dzv = dv * _gelu_grad(zv)
        dz_ref[:, d_gm:] = dzv.astype(dz_ref.dtype)
        dbin_ref[:, d_gm:] += jnp.sum(dzv, axis=0, keepdims=True)
        dbin_ref[:, :d_gm] += jnp.sum(dz_ref[:, :d_gm].astype(F32), axis=0, keepdims=True)

    return pl.pallas_call(
        body, name="gmlp_bwd", grid=(rows // tr,),
        in_specs=[pl.BlockSpec((tr, 2 * d_gm), lambda i: (i, 0)), pl.BlockSpec((tr, d_gm), lambda i: (i, 0)),
                  _full((1, d_gm)), _full((1, d_gm)), _full(w_sp.shape), _full(b_sp_t.shape)],
        out_specs=(pl.BlockSpec((tr, 2 * d_gm), lambda i: (i, 0)), _full((1, 2 * d_gm)), _full((1, d_gm)),
                   _full((1, d_gm)), _full(w_sp.shape), _full(b_sp_t.shape)),
        out_shape=(jax.ShapeDtypeStruct((rows, 2 * d_gm), BF16), jax.ShapeDtypeStruct((1, 2 * d_gm), F32),
                   jax.ShapeDtypeStruct((1, d_gm), F32), jax.ShapeDtypeStruct((1, d_gm), F32),
                   jax.ShapeDtypeStruct(w_sp.shape, F32), jax.ShapeDtypeStruct(b_sp_t.shape, F32)),
        scratch_shapes=[pltpu.VMEM((tr, d_gm), F32)],
        compiler_params=_params(("arbitrary",)),
    )(z, dgate, v_g, v_b, w_sp, b_sp_t)


def _adamw_math(w, g, m, v):
    m = ADAM_B1 * m + (1.0 - ADAM_B1) * g
    v = ADAM_B2 * v + (1.0 - ADAM_B2) * (g * g)
    m_hat = m / (1.0 - ADAM_B1 ** ADAM_STEP)
    v_hat = v / (1.0 - ADAM_B2 ** ADAM_STEP)
    delta = -ADAM_LR * (m_hat / (jnp.sqrt(v_hat) + ADAM_EPS) + ADAM_WD * w)
    return delta, m, v


def _adamw(w, g, m, v, name, rewrite_grad=False):
    shape = w.shape
    if rewrite_grad:
        outs = _rowwise(lambda w_, g_, m_, v_: (g_,) + _adamw_math(w_, g_, m_, v_), (F32,) * 4, _as2d(w), _as2d(g), _as2d(m),
                        _as2d(v), name=name)
        return tuple(o.reshape(shape) for o in outs)
    outs = _rowwise(_adamw_math, (F32, F32, F32), _as2d(w), _as2d(g), _as2d(m), _as2d(v), name=name)
    return (g.reshape(shape),) + tuple(o.reshape(shape) for o in outs)


PACK_COLS = 1024


def _pack(arrays, dtype=F32):
    flat = jnp.concatenate([a.reshape(-1).astype(dtype) for a in arrays])
    pad = (-flat.size) % (16 * PACK_COLS)
    return jnp.pad(flat, (0, pad)).reshape(-1, PACK_COLS)


def _into_slot(pack, dev, name):
    rows, cols = pack.shape
    tr = _rows_tile(rows, cols, budget=512 * 1024)

    def body(dev_ref, p_ref, o_ref):
        o_ref[...] = p_ref[...]

    return pl.pallas_call(
        body, name=name, out_shape=jax.ShapeDtypeStruct((N_DEV, rows, cols), pack.dtype),
        grid_spec=pltpu.PrefetchScalarGridSpec(
            num_scalar_prefetch=1, grid=(rows // tr,), in_specs=[pl.BlockSpec((tr, cols), lambda i, dv: (i, 0))],
            out_specs=pl.BlockSpec((None, tr, cols), lambda i, dv: (dv[0], i, 0))),
        compiler_params=_params(("parallel",)),
    )(dev, pack)


def _unpack(flat, shapes):
    out, pos = [], 0
    for shp in shapes:
        n = math.prod(shp)
        out.append(flat[pos:pos + n].reshape(shp))
        pos += n
    return out


def _unpack_devices(packed8, shapes):
    flat8 = packed8.reshape(N_DEV, -1)
    out, pos = [], 0
    for shp in shapes:
        n = math.prod(shp)
        out.append(flat8[:, pos:pos + n].reshape((N_DEV,) + tuple(shp)))
        pos += n
    return out


def _sum_devices(g8):
    _, rows, cols = g8.shape
    tr = _rows_tile(rows, cols, budget=256 * 1024)

    def body(g_ref, o_ref):
        acc = g_ref[0].astype(F32)
        for d in range(1, N_DEV):
            acc = acc + g_ref[d].astype(F32)
        o_ref[...] = acc

    return pl.pallas_call(
        body, name="sum_devices", grid=(rows // tr,), in_specs=[pl.BlockSpec((N_DEV, tr, cols), lambda i: (0, i, 0))],
        out_specs=pl.BlockSpec((tr, cols), lambda i: (i, 0)), out_shape=jax.ShapeDtypeStruct((rows, cols), F32),
        compiler_params=_params(("parallel",)),
    )(g8)


def _place():
    return lax.axis_index("x"), lax.axis_index("y"), lax.axis_index("c")


def _other_chips(x, y):
    return [(1 - x, y), (x, 1 - y), (1 - x, 1 - y)]


def _remote(src, dst, send_sem, recv_sem, to):
    return pltpu.make_async_remote_copy(src_ref=src, dst_ref=dst, send_sem=send_sem, recv_sem=recv_sem, device_id=to,
                                        device_id_type=MESH)


def _comm_call(body, name, operands, out_shapes, n_remote, n_local, aliases=None):
    return pl.pallas_call(
        body, name=name, out_shape=tuple(out_shapes), in_specs=[ANY] * len(operands), out_specs=tuple(ANY for _ in out_shapes),
        scratch_shapes=[pltpu.SemaphoreType.DMA((n_remote,)), pltpu.SemaphoreType.DMA((n_remote,)),
                        pltpu.SemaphoreType.DMA((max(n_local, 1),))],
        input_output_aliases=aliases or {},
    )(*operands)


def _in_place(arrays):
    return [jax.ShapeDtypeStruct(a.shape, a.dtype) for a in arrays], {i: i for i in range(len(arrays))}


def _allgather8(arrs, name):
    n = len(arrs)

    def body(*refs):
        ins, outs = refs[:n], refs[n:2 * n]
        send, recv, lsem = refs[2 * n:]
        x, y, c = _place()
        me, sib = (x, y, c), (x, y, 1 - c)
        chips = _other_chips(x, y)

        def slot(t, px, py, pc):
            return outs[t].at[4 * px + 2 * py + pc]

        def cp(t, k, block, to, from_input=False):
            src = ins[t] if from_input else slot(t, *block)
            return _remote(src, slot(t, *block), send.at[7 * t + k], recv.at[7 * t + k], to)

        mine = [pltpu.make_async_copy(ins[t], slot(t, *me), lsem.at[t]) for t in range(n)]
        for cpy in mine:
            cpy.start()
        first = []
        for t in range(n):
            first.append(cp(t, 0, me, sib, True))
            first += [cp(t, 1 + j, me, (*chip, c), True) for j, chip in enumerate(chips)]
        for cpy in first:
            cpy.start()
        passed = []
        for t in range(n):
            for j, chip in enumerate(chips):
                cp(t, 1 + j, (*chip, c), me).wait_recv()
                fwd = cp(t, 4 + j, (*chip, c), sib)
                fwd.start()
                passed.append(fwd)
        for t in range(n):
            cp(t, 0, sib, me).wait_recv()
            for j, chip in enumerate(chips):
                cp(t, 4 + j, (*chip, 1 - c), me).wait_recv()
        for cpy in first + passed:
            cpy.wait_send()
        for cpy in mine:
            cpy.wait()

    outs = _comm_call(body, name, arrs, [jax.ShapeDtypeStruct((N_DEV,) + a.shape, a.dtype) for a in arrs], 7 * n, n)
    return list(outs)


def _join_halves(bufs):
    n = len(bufs)
    units = [(k, layer) for k in range(n) for layer in range(bufs[k].shape[0])]

    def body(*refs):
        bufs_ = refs[n:2 * n]
        send, recv, _ = refs[2 * n:]
        x, y, c = _place()
        sent = []
        for u, (k, layer) in enumerate(units):
            half = bufs_[k].shape[1] // 2
            mine = bufs_[k].at[layer, pl.ds(c * half, half)]
            cpy = _remote(mine, mine, send.at[u], recv.at[u], (x, y, 1 - c))
            cpy.start()
            sent.append(cpy)
        for u, (k, layer) in enumerate(units):
            half = bufs_[k].shape[1] // 2
            theirs = bufs_[k].at[layer, pl.ds((1 - c) * half, half)]
            _remote(theirs, theirs, send.at[u], recv.at[u], (x, y, c)).wait_recv()
        for cpy in sent:
            cpy.wait_send()

    shapes, aliases = _in_place(bufs)
    return list(_comm_call(body, "join_halves", bufs, shapes, len(units), 0, aliases))


def _add_halves(grad, other, place):
    _, rows, cols = grad.shape
    half = rows // 2
    tr = _rows_tile(half, cols, itemsize=2, budget=2 * 1024 * 1024)
    per_half = half // tr

    def body(place_ref, g_ref, o_ref, s_ref):
        s_ref[...] = (g_ref[...].astype(F32) + o_ref[...].astype(F32)).astype(s_ref.dtype)

    return pl.pallas_call(
        body, name="add_halves", out_shape=jax.ShapeDtypeStruct((N_CHIPS, half, cols), grad.dtype),
        grid_spec=pltpu.PrefetchScalarGridSpec(
            num_scalar_prefetch=1, grid=(N_CHIPS, per_half),
            in_specs=[pl.BlockSpec((None, tr, cols), lambda k, i, pr: (k, pr[1] * per_half + i, 0)),
                      pl.BlockSpec((None, tr, cols), lambda k, i, pr: (k, i, 0))],
            out_specs=pl.BlockSpec((None, tr, cols), lambda k, i, pr: (k, i, 0))),
        compiler_params=_params(("parallel", "parallel")),
    )(place, grad, other)


def _add_chips(sums, others, place, dest, layer, n_layers):
    _, half, cols = sums.shape
    tr = _rows_tile(half, cols, itemsize=4, budget=2 * 1024 * 1024)
    per_half = half // tr

    def body(place_ref, s_ref, o_ref, *rest):
        acc = s_ref[...].astype(F32)
        for j in range(N_CHIPS - 1):
            acc = acc + o_ref[j].astype(F32)
        rest[-1][...] = acc

    operands = [place, sums, others] + ([] if dest is None else [dest])
    return pl.pallas_call(
        body, name="add_chips", out_shape=jax.ShapeDtypeStruct((n_layers, 2 * half, cols), F32),
        grid_spec=pltpu.PrefetchScalarGridSpec(
            num_scalar_prefetch=1, grid=(per_half,),
            in_specs=[pl.BlockSpec((None, tr, cols), lambda i, pr: (pr[0], i, 0)),
                      pl.BlockSpec((N_CHIPS - 1, tr, cols), lambda i, pr: (0, i, 0))] + ([] if dest is None else [ANY]),
            out_specs=pl.BlockSpec((None, tr, cols), lambda i, pr: (layer, pr[1] * per_half + i, 0))),
        input_output_aliases={} if dest is None else {3: 0},
        compiler_params=_params(("parallel",)),
    )(*operands)


HBM = pl.BlockSpec(memory_space=pltpu.HBM)
SEM = pl.BlockSpec(memory_space=pltpu.SEMAPHORE)
DATAFLOW = pltpu.SideEffectType.DATAFLOW_SIDE_EFFECTING


def _split_start(name, bufs, copies, n_copies, after=None):
    n = len(bufs)
    extra = 0 if after is None else 1

    def body(*refs):
        for cpy in copies(refs[:n], refs[n + extra], refs[n + extra + 1]):
            cpy.start()
        refs[-1][...] = jnp.zeros_like(refs[-1])

    outs = pl.pallas_call(
        body, name=name,
        out_shape=(pltpu.SemaphoreType.DMA((n_copies,)), pltpu.SemaphoreType.DMA((n_copies,)),
                   *[pltpu.HBM(b.shape, b.dtype) for b in bufs], jax.ShapeDtypeStruct((8, LANES), F32)),
        in_specs=[HBM] * n + [ANY] * extra,
        out_specs=(SEM, SEM, *[HBM] * n, pl.BlockSpec(memory_space=pltpu.VMEM)),
        input_output_aliases={i: 2 + i for i in range(n)},
        compiler_params=pltpu.CompilerParams(has_side_effects=DATAFLOW),
    )(*[pltpu.with_memory_space_constraint(b, pltpu.HBM) for b in bufs], *([] if after is None else [after]))
    return outs[0], outs[1], list(outs[2:2 + n]), outs[-1]


def _split_wait(name, bufs, send, recv, copies, after):
    n = len(bufs)

    def body(*refs):
        for cpy in copies(refs[:n], refs[n], refs[n + 1]):
            cpy.wait_send()
            cpy.wait_recv()

    return list(pl.pallas_call(
        body, name=name, out_shape=tuple(pltpu.HBM(b.shape, b.dtype) for b in bufs),
        in_specs=[HBM] * n + [SEM, SEM, ANY], out_specs=tuple([HBM] * n),
        input_output_aliases={i: i for i in range(n)},
        compiler_params=pltpu.CompilerParams(has_side_effects=DATAFLOW),
    )(*bufs, send, recv, after))


def _gather_copies(bufs, send, recv):
    x, y, c = _place()
    out = []
    for u, buf in enumerate(bufs):
        half = buf.shape[1] // 2
        mine = buf.at[2 * x + y, pl.ds(c * half, half)]
        out += [_remote(mine, mine, send.at[3 * u + j], recv.at[3 * u + j], (*chip, c))
                for j, chip in enumerate(_other_chips(x, y))]
    return out


def _exchange_copies(bufs, send, recv):
    x, y, c = _place()
    n = len(bufs) // 2
    out = []
    for k in range(n):
        half = bufs[k].shape[1] // 2
        theirs = bufs[k].at[pl.ds(0, N_CHIPS), pl.ds((1 - c) * half, half)]
        out.append(_remote(theirs, bufs[n + k], send.at[k], recv.at[k], (x, y, 1 - c)))
    return out


def _all_to_all_copies(bufs, send, recv):
    x, y, c = _place()
    n = len(bufs) // 2
    return [_remote(bufs[k].at[2 * chip[0] + chip[1]], bufs[n + k].at[j], send.at[3 * k + j], recv.at[3 * k + j], (*chip, c))
            for k in range(n) for j, chip in enumerate(_other_chips(x, y))]


def _forward_copies(bufs, send, recv):
    x, y, c = _place()
    out = []
    for u, buf in enumerate(bufs):
        half = buf.shape[1] // 2
        for j, chip in enumerate(_other_chips(x, y)):
            landed = buf.at[2 * chip[0] + chip[1], pl.ds(c * half, half)]
            out.append(_remote(landed, landed, send.at[3 * u + j], recv.at[3 * u + j], (x, y, 1 - c)))
    return out


def _gather8_copies(bufs, send, recv):
    x, y, c = _place()
    targets = [(x, y, 1 - c)] + [(*chip, c) for chip in _other_chips(x, y)]
    out = []
    for b, buf in enumerate(bufs):
        mine = buf.at[4 * x + 2 * y + c]
        out += [_remote(mine, mine, send.at[N_CHIPS * b + k], recv.at[N_CHIPS * b + k], to) for k, to in enumerate(targets)]
    return out


def _forward_slots(bufs, name):
    n = len(bufs)

    def body(*refs):
        bufs_ = refs[n:2 * n]
        send, recv, _ = refs[2 * n:]
        x, y, c = _place()
        chips = _other_chips(x, y)
        sent = []
        for b in range(n):
            for j, chip in enumerate(chips):
                slot = bufs_[b].at[4 * chip[0] + 2 * chip[1] + c]
                cpy = _remote(slot, slot, send.at[3 * b + j], recv.at[3 * b + j], (x, y, 1 - c))
                cpy.start()
                sent.append(cpy)
        for b in range(n):
            for j, chip in enumerate(chips):
                slot = bufs_[b].at[4 * chip[0] + 2 * chip[1] + 1 - c]
                _remote(slot, slot, send.at[3 * b + j], recv.at[3 * b + j], (x, y, c)).wait_recv()
        for cpy in sent:
            cpy.wait_send()

    shapes, aliases = _in_place(bufs)
    return list(_comm_call(body, name, bufs, shapes, (N_CHIPS - 1) * n, 0, aliases))


FWD_GROUPS = {'mix': ('ar_out', 'ff_in0', 'ff_out0'), 'l1': ('gm_in', 'gm_out', 'ff_in1', 'ff_out1')}
GRAD_LAYOUT = {'ff_in0': (0, 0), 'ff_in1': (0, 1), 'ff_out0': (1, 0), 'ff_out1': (1, 1), 'ar_in': (2, 0), 'ar_out': (3, 0),
               'gm_in': (4, 0), 'gm_out': (5, 0)}


class _MeshLink:
    def __init__(self, place, shards):
        self.place = place
        self.ready = {'ar_in': shards['ar_in']}
        self.pending, after = {}, shards['ar_in']
        for group, names in FWD_GROUPS.items():
            send, recv, bufs, token = _split_start(f"gather_{group}_start", [shards[n] for n in names], _gather_copies,
                                                   3 * len(names), after)
            self.pending[group] = (names, send, recv, bufs)
            after = token
        self.start_token = after[0, 0]
        self.forwarding, self.exchanging, self.sent, self.last_token = {}, {}, {}, None

    def prefetch(self, group, after):
        names, send, recv, bufs = self.pending.pop(group)
        bufs = _split_wait(f"gather_{group}_wait", bufs, send, recv, _gather_copies, after)
        send, recv, bufs, token = _split_start(f"forward_{group}_start", bufs, _forward_copies, 3 * len(names))
        self.forwarding[group] = (names, send, recv, bufs)
        return token[0, 0]

    def weights(self, group, after):
        if group in self.forwarding:
            names, send, recv, bufs = self.forwarding.pop(group)
            self.ready.update(zip(names, _split_wait(f"forward_{group}_wait", bufs, send, recv, _forward_copies, after)))
        return self.ready

    def gradients(self, group, grads, after=None):
        tok = self.poll(next(iter(grads.values())))
        names, mine = list(grads), list(grads.values())
        landing = [lax.empty((N_CHIPS, g.shape[1] // 2, g.shape[2]), g.dtype) for g in mine]
        send, recv, bufs, token = _split_start(f"exchange_{group}_start", mine + landing, _exchange_copies, len(names), after)
        self.exchanging[group] = (names, send, recv, bufs)
        self.last_token = token
        return token[0, 0] + tok

    def poll(self, after):
        tok = 0.0
        for group in list(self.exchanging):
            names, send, recv, bufs = self.exchanging.pop(group)
            bufs = _split_wait(f"exchange_{group}_wait", bufs, send, recv, _exchange_copies, after)
            sums = [_add_halves(g, r, self.place) for g, r in zip(bufs[:len(names)], bufs[len(names):])]
            landing = [lax.empty((N_CHIPS - 1,) + s.shape[1:], s.dtype) for s in sums]
            send, recv, bufs, token = _split_start(f"grads_{group}_start", sums + landing, _all_to_all_copies, 3 * len(names))
            self.sent[group] = (names, send, recv, bufs)
            self.last_token = token
            tok = tok + token[0, 0]
        return tok

    def reduce(self, groups, after):
        units = {}
        for group in groups:
            names, send, recv, bufs = self.sent.pop(group)
            bufs = _split_wait(f"grads_{group}_wait", bufs, send, recv, _all_to_all_copies, after)
            units.update(zip(names, zip(bufs[:len(names)], bufs[len(names):])))
        n_layers = {p: 1 + max(l for pp, l in GRAD_LAYOUT.values() if pp == p) for p, _ in GRAD_LAYOUT.values()}
        out = {}
        for name, (p, layer) in GRAD_LAYOUT.items():
            if name in units:
                out[p] = _add_chips(*units[name], self.place, out.get(p), layer, n_layers[p])
        params = sorted(out)
        return dict(zip(params, _join_halves([out[p] for p in params])))


def _rope_tables(n):
    n_rows = n // GRID_W
    freqs = ROPE_THETA ** (-jnp.arange(ROPE_PAIRS, dtype=F32) / ROPE_PAIRS)
    ang_r = jnp.arange(n_rows, dtype=F32)[:, None] * freqs
    ang_c = jnp.arange(GRID_W, dtype=F32)[:, None] * freqs

    def per_token(of_row, of_col):
        r = jnp.broadcast_to(of_row[:, None, :], (n_rows, GRID_W, ROPE_PAIRS)).reshape(n, ROPE_PAIRS)
        c = jnp.broadcast_to(of_col[None, :, :], (n_rows, GRID_W, ROPE_PAIRS)).reshape(n, ROPE_PAIRS)
        return r, c

    cos_r, cos_c = per_token(jnp.cos(ang_r), jnp.cos(ang_c))
    sin_r, sin_c = per_token(jnp.sin(ang_r), jnp.sin(ang_c))
    cos = jnp.concatenate([cos_r, cos_r, cos_c, cos_c], axis=-1)
    sin = jnp.concatenate([-sin_r, sin_r, -sin_c, sin_c], axis=-1)
    return cos, sin


def _ffn_fwd(h2, w1, w2, tag):
    r, a = _matmul(h2, w1, kind='nn', b_split='n', out_dtype=BF16, epilogue='relu2', name=f"ffn_in_{tag}")
    f = _matmul(a, w2, kind='nn', b_split='k', out_dtype=F32, name=f"ffn_out_{tag}")
    return r, a, f


def _ffn_bwd(d_f, h2, r, a, w1, w2, tag):
    d_u = _matmul(d_f, w2, kind='nt', b_split='k', out_dtype=BF16, epilogue='times2x', extra=r, name=f"ffn_out_dx_{tag}")
    d_w2 = _matmul(a, d_f, kind='tn', out_split='k', out_dtype=BF16, name=f"ffn_out_dw_{tag}")
    d_w1 = _matmul(h2, d_u, kind='tn', out_split='n', out_dtype=BF16, name=f"ffn_in_dw_{tag}")
    d_h2 = _matmul(d_u, w1, kind='nt', b_split='n', out_dtype=F32, name=f"ffn_in_dx_{tag}")
    return d_h2, d_w1, d_w2


def _local_step(xl0, xc0, target, ml, mc0, sp, link):
    n_lat, n_ctx = xl0.shape[0], xc0.shape[0]
    one = lambda v: 1.0 + v
    g = [[sp['norm_g'][i, k][None, :] for k in range(4)] for i in range(2)]

    sh1, sc1, gt1, sh2, sc2, gt2 = ml[0]
    big = link.weights('ar', None)
    sh1 = sh1 + link.start_token
    n_all = n_lat + n_ctx
    h_all = _norm_fwd(xl0, g[0][0], one(sc1), b=sh1, out_dtype=BF16, name="l0_mod1", into=(0, n_all, None))
    h_all = _norm_fwd(xc0, g[0][0], one(mc0[1]), b=mc0[0], out_dtype=BF16, name="l0_mod1_ctx", into=(n_lat, n_all, h_all))
    proj_l = _matmul(h_all, big['ar_in'], kind='nn', b_split='n', out_dtype=F32, a_rows=(0, n_lat), name="ar_in_lat")
    proj_c = _matmul(h_all, big['ar_in'], kind='nn', b_split='n', out_dtype=F32, a_rows=(n_lat, n_ctx), name="ar_in_ctx")
    cos_l, sin_l = _rope_tables(n_lat)
    cos_c, sin_c = jnp.ones((n_ctx, HEAD_DIM), F32), jnp.zeros((n_ctx, HEAD_DIM), F32)
    q_g, k_g = sp['q_g'], sp['k_g']
    _, k_all, v_all = _qk_fwd(proj_c, q_g, k_g, cos_c, sin_c, name="qk_fwd_ctx", kv_into=(0, n_all, None))
    q_l, k_all, v_all = _qk_fwd(proj_l, q_g, k_g, cos_l, sin_l, name="qk_fwd_lat", kv_into=(n_ctx, n_all, (k_all, v_all)))
    cat, lse = _attn_fwd(q_l, k_all, v_all)
    conv_b = sp['conv_b'] + link.prefetch('mix', cat)
    xs = _conv_fwd(proj_l, proj_c, sp['conv_w'], conv_b)
    rnn_w = [(sp['wa'][d], sp['ba'][d][None, :], sp['wx'][d], sp['bx'][d][None, :], sp['lam'][d][None, :]) for d in range(2)]
    h_f, hp_f = _rglru_fwd(xs, *rnn_w[0], reverse=False, n_ctx=n_ctx, name="rglru_fwd_f")
    h_r, hp_r = _rglru_fwd(xs, *rnn_w[1], reverse=True, n_ctx=n_ctx, name="rglru_fwd_r")
    cat = _rnn_out_fwd(h_f, h_r, proj_l, n_ctx, n_ctx, cat)
    w_mix = link.weights('mix', cat)
    ol0 = _matmul(cat, w_mix['ar_out'], kind='nn', b_split='k', out_dtype=F32, name="ar_out")
    xm0, h2_0 = _res_mod_fwd(ol0, xl0, g[0][1], gt1, g[0][2], one(sc2), sh2, name="l0_res1_mod2")
    r0, a0, f0 = _ffn_fwd(h2_0, w_mix['ff_in0'], w_mix['ff_out0'], "l0")
    th1, tc1, tg1, th2, tc2, tg2 = ml[1]
    xl1, hl1 = _res_mod_fwd(f0, xm0, g[0][3], gt2 + link.prefetch('l1', f0), g[1][0], one(tc1), th1, name="l0_res2_l1_mod1")
    w_l1 = link.weights('l1', xl1)
    z = _matmul(hl1, w_l1['gm_in'], kind='nn', b_split='n', bias=sp['gm_b_in'], out_dtype=F32, name="gm_in")
    b_sp_t = sp['gm_b_sp'].T
    gated = _gmlp_fwd(z, sp['gm_v_g'], sp['gm_v_b'], sp['gm_w_sp'], b_sp_t)
    ol1 = _matmul(gated, w_l1['gm_out'], kind='nn', b_split='k', out_dtype=F32, name="gm_out")
    xm1, h2_1 = _res_mod_fwd(ol1, xl1, g[1][1], tg1, g[1][2], one(tc2), th2, name="l1_res1_mod2")
    r1, a1, f1 = _ffn_fwd(h2_1, w_l1['ff_in1'], w_l1['ff_out1'], "l1")

    dy, d_f1, dg13, d_tg2, loss = _final_res_loss(f1, xm1, g[1][3], tg2, target)

    d_h2, dw_ff_in1, dw_ff_out1 = _ffn_bwd(d_f1, h2_1, r1, a1, w_l1['ff_in1'], w_l1['ff_out1'], "l1")
    tok = link.gradients('ffn1', {'ff_in1': dw_ff_in1, 'ff_out1': dw_ff_out1})
    dxm1, d_ol1, dg12, d_tc2, d_th2, dg11, d_tg1 = _mod_res_bwd(d_h2, xm1, g[1][2], one(tc2) + tok, dy, ol1, g[1][1], tg1,
                                                                name="l1_mod2_res1_bwd")
    d_gated = _matmul(d_ol1, w_l1['gm_out'], kind='nt', b_split='k', out_dtype=F32, name="gm_out_dx")
    dw_gm_out = _matmul(gated, d_ol1, kind='tn', out_split='k', out_dtype=BF16, name="gm_out_dw")
    d_z, d_gm_b_in, d_vg, d_vb, d_wsp, d_bsp_t = _gmlp_bwd(z, d_gated, sp['gm_v_g'], sp['gm_v_b'], sp['gm_w_sp'], b_sp_t)
    dw_gm_in = _matmul(hl1, d_z, kind='tn', out_split='n', out_dtype=BF16, name="gm_in_dw")
    d_hl1 = _matmul(d_z, w_l1['gm_in'], kind='nt', b_split='n', out_dtype=F32, name="gm_in_dx")
    tok = link.gradients('gm', {'gm_in': dw_gm_in, 'gm_out': dw_gm_out})

    dxl1, d_f0, dg10, d_tc1, d_th1, dg03, d_gt2 = _mod_res_bwd(d_hl1, xl1, g[1][0], one(tc1) + tok, dxm1, f0, g[0][3], gt2,
                                                               name="l1_mod1_l0_res2_bwd")
    d_h2, dw_ff_in0, dw_ff_out0 = _ffn_bwd(d_f0, h2_0, r0, a0, w_mix['ff_in0'], w_mix['ff_out0'], "l0")
    tok = link.gradients('ffn0', {'ff_in0': dw_ff_in0, 'ff_out0': dw_ff_out0})
    dxm0, d_ol0, dg02, d_sc2, d_sh2, dg01, d_gt1 = _mod_res_bwd(d_h2, xm0, g[0][2], one(sc2) + tok, dxl1, ol0, g[0][1], gt1,
                                                                name="l0_mod2_res1_bwd")
    d_cat = _matmul(d_ol0, w_mix['ar_out'], kind='nt', b_split='k', out_dtype=F32, name="ar_out_dx")
    dw_ar_out = _matmul(cat, d_ol0, kind='tn', out_split='k', out_dtype=BF16, name="ar_out_dw")
    dq, dk_all, dv_all = _attn_bwd(q_l, k_all, v_all, cat, lse, d_cat)
    tok = link.poll(dq)
    d_h, d_gate = _rnn_out_bwd(d_cat, h_f, h_r, proj_l, n_ctx, n_ctx)
    rnn_wb = [(wa_, ba_ + tok, wx_, bx_, lam_) for wa_, ba_, wx_, bx_, lam_ in rnn_w]
    dxs_f, d_wa0, d_ba0, d_wx0, d_bx0, d_lam0 = _rglru_bwd(
        xs, hp_f, d_h, *rnn_wb[0], reverse=False, n_ctx=n_ctx, name="rglru_bwd_f")
    dxs_r, d_wa1, d_ba1, d_wx1, d_bx1, d_lam1 = _rglru_bwd(
        xs, hp_r, d_h, *rnn_wb[1], reverse=True, n_ctx=n_ctx, name="rglru_bwd_r")
    d_xr_l, d_xr_c, d_cw, d_cb = _conv_bwd(dxs_f, dxs_r, proj_l, proj_c, sp['conv_w'])
    dp_qk_l, d_qg, d_kg_l = _qk_bwd(dq, dk_all, proj_l, q_g, k_g, cos_l, sin_l, name="qk_bwd_lat", dk_row0=n_ctx)
    dp_qk_c, _, d_kg_c = _qk_bwd(None, dk_all, proj_c, q_g, k_g, cos_c, sin_c, name="qk_bwd_ctx")
    d_proj = _assemble_d_proj(dp_qk_l, dp_qk_c, dv_all, d_xr_l, d_xr_c, d_gate)
    dw_ar_in = _matmul(h_all, d_proj, kind='tn', out_split='n', out_dtype=BF16, name="ar_in_dw")
    d_hl = _matmul(d_proj, big['ar_in'], kind='nt', b_split='n', out_dtype=F32, a_rows=(0, n_lat), name="ar_in_dx_lat")
    d_hc = _matmul(d_proj, big['ar_in'], kind='nt', b_split='n', out_dtype=F32, a_rows=(n_lat, n_ctx), name="ar_in_dx_ctx")
    grad_x, dg00, d_sc1, d_sh1 = _norm_bwd(d_hl, xl0, g[0][0], one(sc1), extra=dxm0, out_dtype=F32, name="l0_mod1_bwd")
    _, dg00c, d_mc_scale, d_mc_shift = _norm_bwd(d_hc, xc0, g[0][0], one(mc0[1]), out_dtype=BF16, name="l0_mod1_ctx_bwd")

    zeros_d = jnp.zeros_like(d_sh1)
    small = {
        'd_ml0': jnp.concatenate([d_sh1, d_sc1, d_gt1, d_sh2, d_sc2, d_gt2], axis=1),
        'd_ml1': jnp.concatenate([d_th1, d_tc1, d_tg1, d_th2, d_tc2, d_tg2], axis=1),
        'd_mc0': jnp.concatenate([d_mc_shift, d_mc_scale] + [zeros_d] * 4, axis=1),
        'norm_g': jnp.stack([jnp.concatenate([dg00 + dg00c, dg01, dg02, dg03], axis=0),
                             jnp.concatenate([dg10, dg11, dg12, dg13], axis=0)]),
        'q_g': d_qg, 'k_g': d_kg_l + d_kg_c, 'conv_w': d_cw, 'conv_b': d_cb,
        'wa': jnp.stack([d_wa0, d_wa1]), 'ba': jnp.concatenate([d_ba0, d_ba1], axis=0),
        'wx': jnp.stack([d_wx0, d_wx1]), 'bx': jnp.concatenate([d_bx0, d_bx1], axis=0),
        'lam': jnp.concatenate([d_lam0, d_lam1], axis=0),
        'gm_b_in': d_gm_b_in, 'gm_v_g': d_vg, 'gm_v_b': d_vb, 'gm_w_sp': d_wsp, 'gm_b_sp': d_bsp_t.T,
        'loss': loss,
    }
    return grad_x, small, {'ar_in': dw_ar_in, 'ar_out': dw_ar_out}


MOD_ROWS = 16
SMALL_F32 = ('d_ml0', 'd_ml1', 'd_mc0', 'norm_g', 'q_g', 'k_g', 'conv_w', 'conv_b', 'ba', 'bx', 'lam', 'gm_b_in', 'gm_v_g',
             'gm_v_b', 'gm_b_sp', 'loss')
SMALL_BF16 = ('wa', 'wx', 'gm_w_sp')


def _silu(v):
    return v * _sigmoid(v)


def _chip_concat(gathered, axis):
    return jnp.concatenate([gathered[2 * q] for q in range(N_CHIPS)], axis=axis)


def kernel(x, c, ctx, c_ctx, w_mod, b_mod, norm_g, w_ff_in, w_ff_out, ar_w_in, ar_q_g, ar_k_g, ar_conv_w, ar_conv_b, ar_wa, ar_ba, ar_wx, ar_bx, ar_lambda, ar_w_out, gm_w_in, gm_b_in, gm_v_g, gm_v_b, gm_w_sp, gm_b_sp, gm_w_out, loss_target, m_c_ctx, m_w_mod, m_b_mod, m_norm_g, m_w_ff_in, m_w_ff_out, m_ar_w_in, m_ar_q_g, m_ar_k_g, m_ar_conv_w, m_ar_conv_b, m_ar_wa, m_ar_ba, m_ar_wx, m_ar_bx, m_ar_lambda, m_ar_w_out, m_gm_w_in, m_gm_b_in, m_gm_v_g, m_gm_v_b, m_gm_w_sp, m_gm_b_sp, m_gm_w_out, v_c_ctx, v_w_mod, v_b_mod, v_norm_g, v_w_ff_in, v_w_ff_out, v_ar_w_in, v_ar_q_g, v_ar_k_g, v_ar_conv_w, v_ar_conv_b, v_ar_wa, v_ar_ba, v_ar_wx, v_ar_bx, v_ar_lambda, v_ar_w_out, v_gm_w_in, v_gm_b_in, v_gm_v_g, v_gm_v_b, v_gm_w_sp, v_gm_b_sp, v_gm_w_out):
    weights = dict(c_ctx=c_ctx, w_mod=w_mod, b_mod=b_mod, norm_g=norm_g, w_ff_in=w_ff_in, w_ff_out=w_ff_out, ar_w_in=ar_w_in,
                   ar_q_g=ar_q_g, ar_k_g=ar_k_g, ar_conv_w=ar_conv_w, ar_conv_b=ar_conv_b, ar_wa=ar_wa, ar_ba=ar_ba, ar_wx=ar_wx,
                   ar_bx=ar_bx, ar_lambda=ar_lambda, ar_w_out=ar_w_out, gm_w_in=gm_w_in, gm_b_in=gm_b_in, gm_v_g=gm_v_g,
                   gm_v_b=gm_v_b, gm_w_sp=gm_w_sp, gm_b_sp=gm_b_sp, gm_w_out=gm_w_out)
    m_in = dict(c_ctx=m_c_ctx, w_mod=m_w_mod, b_mod=m_b_mod, norm_g=m_norm_g, w_ff_in=m_w_ff_in, w_ff_out=m_w_ff_out,
                ar_w_in=m_ar_w_in, ar_q_g=m_ar_q_g, ar_k_g=m_ar_k_g, ar_conv_w=m_ar_conv_w, ar_conv_b=m_ar_conv_b, ar_wa=m_ar_wa,
                ar_ba=m_ar_ba, ar_wx=m_ar_wx, ar_bx=m_ar_bx, ar_lambda=m_ar_lambda, ar_w_out=m_ar_w_out, gm_w_in=m_gm_w_in,
                gm_b_in=m_gm_b_in, gm_v_g=m_gm_v_g, gm_v_b=m_gm_v_b, gm_w_sp=m_gm_w_sp, gm_b_sp=m_gm_b_sp, gm_w_out=m_gm_w_out)
    v_in = dict(c_ctx=v_c_ctx, w_mod=v_w_mod, b_mod=v_b_mod, norm_g=v_norm_g, w_ff_in=v_w_ff_in, w_ff_out=v_w_ff_out,
                ar_w_in=v_ar_w_in, ar_q_g=v_ar_q_g, ar_k_g=v_ar_k_g, ar_conv_w=v_ar_conv_w, ar_conv_b=v_ar_conv_b, ar_wa=v_ar_wa,
                ar_ba=v_ar_ba, ar_wx=v_ar_wx, ar_bx=v_ar_bx, ar_lambda=v_ar_lambda, ar_w_out=v_ar_w_out, gm_w_in=v_gm_w_in,
                gm_b_in=v_gm_b_in, gm_v_g=v_gm_v_g, gm_v_b=v_gm_v_b, gm_w_sp=v_gm_w_sp, gm_b_sp=v_gm_b_sp, gm_w_out=v_gm_w_out)

    xi, yi, ci = lax.axis_index("x"), lax.axis_index("y"), lax.axis_index("c")
    chip = 2 * xi + yi
    dev = 4 * xi + 2 * yi + ci
    place = jnp.stack([chip, ci]).astype(jnp.int32)
    n_lat, d = x.shape[1], x.shape[2]
    d6 = 6 * d
    cols_mod = w_mod.shape[2]

    mine = [c, norm_g, ar_conv_w[0], ar_ba[0], ar_bx[0], ar_lambda[0], gm_b_in, gm_v_g, gm_v_b]
    gathered = _allgather8([_pack(mine)], "gather_small_params")[0]
    first = _split_start("gather_first_start", [_cast_shard(ar_w_in, place, 0, "cast_ar_in")], _gather_copies, N_CHIPS - 1, gathered)
    parts = _unpack_devices(gathered, [a.shape for a in mine])
    c_all = parts[0].reshape(N_DEV, d)
    sp = {'norm_g': _chip_concat(parts[1], 2), 'q_g': ar_q_g, 'k_g': ar_k_g, 'conv_w': _chip_concat(parts[2], 1),
          'conv_b': ar_conv_b, 'wa': ar_wa[0], 'ba': _chip_concat(parts[3], 1), 'wx': ar_wx[0], 'bx': _chip_concat(parts[4], 1),
          'lam': _chip_concat(parts[5], 1), 'gm_b_in': _chip_concat(parts[6], 1), 'gm_v_g': _chip_concat(parts[7], 1),
          'gm_v_b': _chip_concat(parts[8], 1), 'gm_w_sp': gm_w_sp[0], 'gm_b_sp': gm_b_sp[0]}

    def mod_operand(c_rows, cc):
        row = lax.broadcasted_iota(jnp.int32, (MOD_ROWS - N_DEV, d), 0)
        lower = jnp.where(row == 0, jnp.broadcast_to(_silu(cc), (MOD_ROWS - N_DEV, d)), 0.0)
        sig = _sigmoid(cc)
        return jnp.concatenate([_silu(c_rows), lower], axis=0), sig * (1.0 + cc * (1.0 - sig))

    s_mod, dsilu_ctx = _small(mod_operand, [((MOD_ROWS, d), F32), ((1, d), F32)], c_all, c_ctx[None, :], name="mod_operand")
    b_mod_mine = lax.dynamic_slice(b_mod, (0, chip * cols_mod), (2, cols_mod))
    mod = [_matmul(s_mod, w_mod, kind='nn', b_layer=i, bias=b_mod_mine[i][None, :], out_dtype=F32, name=f"mod_fwd_{i}")
           for i in range(2)]
    mod_all = _allgather8([jnp.concatenate(mod, axis=0)], "gather_mod")[0]
    mod_all = _chip_concat(mod_all, 1).reshape(2, MOD_ROWS, d6)
    ml = [jnp.split(lax.dynamic_slice(mod_all[i], (dev, 0), (1, d6)), 6, axis=1) for i in range(2)]
    mc0 = jnp.split(mod_all[0, N_DEV:N_DEV + 1], 6, axis=1)[:2]

    names = ('w_ff_in', 'w_ff_out', 'ar_w_in', 'ar_w_out', 'gm_w_in', 'gm_w_out')
    keys = {'w_ff_in': ('ff_in0', 'ff_in1'), 'w_ff_out': ('ff_out0', 'ff_out1'), 'ar_w_in': ('ar_in',), 'ar_w_out': ('ar_out',),
            'gm_w_in': ('gm_in',), 'gm_w_out': ('gm_out',)}
    shards = {key: _cast_shard(weights[n], place, layer, f"cast_{key}", after=first[3]) for n in names
              for layer, key in enumerate(keys[n]) if key != 'ar_in'}
    send, recv, bufs, _ = first
    bufs = _split_wait("gather_first_wait", bufs, send, recv, _gather_copies, mod_all)
    send, recv, bufs, token = _split_start("forward_first_start", bufs, _forward_copies, N_CHIPS - 1)
    shards['ar_in'] = _split_wait("forward_first_wait", bufs, send, recv, _forward_copies, token)[0]
    link = _MeshLink(place, shards)

    grad_x, small, last_grads = _local_step(x[0], ctx[0], loss_target[0], ml, mc0, sp, link)

    def step(n, grad):
        return _adamw(weights[n], grad.reshape(weights[n].shape), m_in[n], v_in[n], f"adamw_{n}", rewrite_grad=n in names)

    small_f32, small_bf16 = [small[k] for k in SMALL_F32], [small[k] for k in SMALL_BF16]
    dev_arr = dev.astype(jnp.int32)[None]
    slots = [_into_slot(_pack(small_f32), dev_arr, "small_grads_slot_f32"),
             _into_slot(_pack(small_bf16, BF16), dev_arr, "small_grads_slot_bf16")]
    s_send, s_recv, slots, s_token = _split_start("small_grads_start", slots, _gather8_copies, 2 * N_CHIPS, grad_x)
    link.gradients('ar', last_grads, s_token)
    link.poll(link.last_token)
    reduced = link.reduce(('ffn1', 'gm', 'ffn0'), link.last_token)
    stepped = {n: step(n, reduced[names.index(n)]) for n in ('w_ff_in', 'w_ff_out', 'gm_w_in', 'gm_w_out')}
    reduced = link.reduce(('ar',), stepped['gm_w_out'][1])
    stepped.update({n: step(n, reduced[names.index(n)]) for n in ('ar_w_in', 'ar_w_out')})
    slots = _split_wait("small_grads_wait", slots, s_send, s_recv, _gather8_copies, stepped['ar_w_out'][1])
    small8, small8_bf16 = _forward_slots(slots, "small_grads_forward")
    total = dict(zip(SMALL_F32, _unpack(_sum_devices(small8).reshape(-1), [a.shape for a in small_f32])))
    total.update(zip(SMALL_BF16, _unpack(_sum_devices(small8_bf16).reshape(-1), [a.shape for a in small_bf16])))
    per_dev = _unpack_devices(small8, [(d6,), (d6,)])
    pad_rows = jnp.zeros((MOD_ROWS - N_DEV - 1, d6), F32)
    d_mod = [jnp.concatenate([per_dev[0], total['d_mc0'], pad_rows], axis=0),
             jnp.concatenate([per_dev[1], jnp.zeros((MOD_ROWS - N_DEV, d6), F32)], axis=0)]
    d_mod_mine = [lax.dynamic_slice(dm, (0, chip * cols_mod), (MOD_ROWS, cols_mod)) for dm in d_mod]
    g_w_mod = None
    for i in range(2):
        g_w_mod = _matmul(s_mod, d_mod_mine[i], kind='tn', out_dtype=F32, out_stack=(i, 2, g_w_mod), name=f"mod_dw_{i}")
    d_s_part = _matmul(d_mod_mine[0], w_mod, kind='nt', b_layer=0, out_dtype=F32, name="mod_ds")
    d_s_all = _allgather8([d_s_part[N_DEV:]], "gather_mod_ds")[0]

    def c_ctx_grad(parts_, dsilu):
        acc = parts_[0, 0:1]
        for q in range(1, N_CHIPS):
            acc = acc + parts_[2 * q, 0:1]
        return (acc * dsilu,)

    g_c_ctx = _small(c_ctx_grad, [((1, d), F32)], d_s_all, dsilu_ctx, name="c_ctx_grad")[0].reshape(d)

    def mine_of(full_grad, axis, n_shard):
        return lax.dynamic_slice_in_dim(full_grad, chip * n_shard, n_shard, axis=axis)

    grads_out = {
        'c_ctx': g_c_ctx, 'w_mod': g_w_mod,
        'b_mod': jnp.stack([total['d_ml0'][0] + total['d_mc0'][0], total['d_ml1'][0]]),
        'norm_g': mine_of(total['norm_g'], 2, norm_g.shape[2]),
        'ar_q_g': total['q_g'], 'ar_k_g': total['k_g'], 'ar_conv_w': mine_of(total['conv_w'], 1, ar_conv_w.shape[2])[None],
        'ar_conv_b': total['conv_b'], 'ar_wa': total['wa'][None], 'ar_ba': mine_of(total['ba'], 1, ar_ba.shape[2])[None],
        'ar_wx': total['wx'][None], 'ar_bx': mine_of(total['bx'], 1, ar_bx.shape[2])[None],
        'ar_lambda': mine_of(total['lam'], 1, ar_lambda.shape[2])[None],
        'gm_b_in': mine_of(total['gm_b_in'], 1, gm_b_in.shape[1]),
        'gm_v_g': mine_of(total['gm_v_g'], 1, gm_v_g.shape[1]), 'gm_v_b': mine_of(total['gm_v_b'], 1, gm_v_b.shape[1]),
        'gm_w_sp': total['gm_w_sp'][None], 'gm_b_sp': total['gm_b_sp'][None],
    }
    stepped.update({n: step(n, grad) for n, grad in grads_out.items()})
    stepped = [stepped[n] for n in weights]
    loss = total['loss'].reshape(())
    return (loss, grad_x[None], *[s[0] for s in stepped], *[s[1] for s in stepped], *[s[2] for s in stepped],
            *[s[3] for s in stepped])
```
